```python
import jax, jax.numpy as jnp
from jax import lax
import numpy as np

D_MODEL = 2048
BATCH = 8
SEQ = 8192
DEPTH = 4

D_MIX = D_MODEL
D_SGU = D_MIX // 2
SGU_GROUPS = 8
SGU_GROUP_DIM = D_SGU // SGU_GROUPS
SGU_CHUNK = 128
D_DN = D_MIX - D_SGU
DN_HEADS = 8
DN_HEAD_DIM = D_DN // DN_HEADS
DN_CHUNK = 64
CONV_WIDTH = 5
NORM_EPS = 1e-6
IN_SIZES = (D_SGU, D_SGU, D_SGU, 3 * D_DN, D_DN, DN_HEADS, DN_HEADS, DN_HEADS, DN_HEADS)
D_IN = 3 * D_SGU + 4 * D_DN + 4 * DN_HEADS

kernel_name = "hybrid_gmlp_gated_deltanet_encoder"


def _rmsnorm(x, w):
    x32 = x.astype(jnp.float32)
    y = x32 * lax.rsqrt(jnp.mean(x32 * x32, axis=-1, keepdims=True) + NORM_EPS)
    return (y * w.astype(jnp.float32)).astype(x.dtype)


def _layernorm(x, g, b):
    x32 = x.astype(jnp.float32)
    mu = jnp.mean(x32, axis=-1, keepdims=True)
    xc = x32 - mu
    var = jnp.mean(xc * xc, axis=-1, keepdims=True)
    y = xc * lax.rsqrt(var + NORM_EPS)
    return (y * g.astype(jnp.float32) + b.astype(jnp.float32)).astype(x.dtype)


def _l2norm(x):
    return x * lax.rsqrt(jnp.sum(x * x, axis=-1, keepdims=True) + NORM_EPS)


def _split_in(proj):
    offsets = [int(o) for o in np.cumsum(IN_SIZES)[:-1]]
    return jnp.split(proj, offsets, axis=-1)


def _short_conv(x, w):
    c = x.shape[-1]
    pad = CONV_WIDTH // 2
    return lax.conv_general_dilated(
        x, w[:, None, :].astype(x.dtype), window_strides=(1,),
        padding=((pad, pad),), dimension_numbers=("NWC", "WIO", "NWC"),
        feature_group_count=c)


def _gated_delta_chunked(q, k, v, g, beta):
    b_, h, s, dk = q.shape
    dv = v.shape[-1]
    nc = s // DN_CHUNK
    q = q.reshape(b_, h, nc, DN_CHUNK, dk)
    k = k.reshape(b_, h, nc, DN_CHUNK, dk)
    v = v.reshape(b_, h, nc, DN_CHUNK, dv)
    g_cum = jnp.cumsum(g.reshape(b_, h, nc, DN_CHUNK), axis=-1)
    beta = beta.reshape(b_, h, nc, DN_CHUNK)

    lower = jnp.tril(jnp.ones((DN_CHUNK, DN_CHUNK), dtype=bool))
    strict = jnp.tril(jnp.ones((DN_CHUNK, DN_CHUNK), dtype=bool), k=-1)
    diff = g_cum[..., :, None] - g_cum[..., None, :]
    decay = jnp.where(lower, jnp.exp(jnp.where(lower, diff, 0.0)), 0.0)

    k_beta = k * beta[..., None]
    m = jnp.where(strict, jnp.einsum("bhnid,bhnjd->bhnij", k_beta, k) * decay, 0.0)
    eye = jnp.eye(DN_CHUNK, dtype=q.dtype)
    t_inv = lax.linalg.triangular_solve(
        eye + m, jnp.broadcast_to(eye, m.shape), left_side=True, lower=True,
        unit_diagonal=True)
    u = jnp.einsum("bhnij,bhnjd->bhnid", t_inv, v * beta[..., None])
    w = jnp.einsum("bhnij,bhnjd->bhnid", t_inv, k_beta * jnp.exp(g_cum)[..., None])
    attn = jnp.einsum("bhnid,bhnjd->bhnij", q, k) * decay
    q_dec = q * jnp.exp(g_cum)[..., None]
    k_dec = k * jnp.exp(g_cum[..., -1:] - g_cum)[..., None]
    g_last = jnp.exp(g_cum[..., -1])

    to_scan = lambda t: jnp.moveaxis(t, 2, 0)
    xs = (to_scan(q_dec), to_scan(k_dec), to_scan(u), to_scan(w), to_scan(attn), to_scan(g_last))

    def step(state, inp):
        q_c, k_c, u_c, w_c, a_c, gl = inp
        v_new = u_c - jnp.einsum("bhid,bhde->bhie", w_c, state)
        o_c = jnp.einsum("bhid,bhde->bhie", q_c, state) + jnp.einsum("bhij,bhje->bhie", a_c, v_new)
        state = state * gl[..., None, None] + jnp.einsum("bhid,bhie->bhde", k_c, v_new)
        return state, o_c

    state0 = jnp.zeros((b_, h, dk, dv), dtype=q.dtype)
    _, o = lax.scan(step, state0, xs)
    return jnp.moveaxis(o, 0, 2).reshape(b_, h, s, dv)


def _sgu_branch(u, v, z, ln_g, ln_b, w_s, b_s):
    b_, s, _ = u.shape
    u = jax.nn.gelu(u, approximate=False)
    v = _layernorm(jax.nn.gelu(v, approximate=False), ln_g, ln_b)
    v = v.reshape(b_, s // SGU_CHUNK, SGU_CHUNK, SGU_GROUPS, SGU_GROUP_DIM)
    sp = jnp.einsum("gij,bcjgd->bcigd", w_s, v) + b_s.T[None, None, :, :, None]
    return u * sp.reshape(b_, s, D_SGU) * jax.nn.silu(z)


def _dn_gates(a, b, a_log, dt_bias):
    g = -jnp.exp(a_log.astype(jnp.float32)) * jax.nn.softplus(
        a.astype(jnp.float32) + dt_bias.astype(jnp.float32))
    beta = jax.nn.sigmoid(b.astype(jnp.float32))
    return g.transpose(0, 2, 1), beta.transpose(0, 2, 1)


def _deltanet_branch(qkv, z, a_f, a_b, b_f, b_b, conv_w, a_log_f, a_log_b,
                     dt_bias_f, dt_bias_b, norm_w):
    b_, s, _ = qkv.shape
    qkv = jax.nn.silu(_short_conv(qkv, conv_w))
    q, k, v = jnp.split(qkv, 3, axis=-1)
    heads = lambda t: t.reshape(b_, s, DN_HEADS, DN_HEAD_DIM).transpose(0, 2, 1, 3).astype(jnp.float32)
    q = _l2norm(heads(q)) * (DN_HEAD_DIM ** -0.5)
    k = _l2norm(heads(k))
    v = heads(v)
    g_f, beta_f = _dn_gates(a_f, b_f, a_log_f, dt_bias_f)
    g_b, beta_b = _dn_gates(a_b, b_b, a_log_b, dt_bias_b)
    rev = lambda t: jnp.flip(t, axis=2)
    o_f = _gated_delta_chunked(q, k, v, g_f, beta_f)
    o_b = rev(_gated_delta_chunked(rev(q), rev(k), rev(v), rev(g_b), rev(beta_b)))
    o = (o_f + o_b).transpose(0, 2, 1, 3)
    o = _rmsnorm(o, norm_w) * jax.nn.silu(z.reshape(b_, s, DN_HEADS, DN_HEAD_DIM).astype(jnp.float32))
    return o.reshape(b_, s, D_DN).astype(z.dtype)


def _fwd_setup_inputs(seed: int = 0) -> dict:
    key = jax.random.key(seed)
    ks = jax.random.split(key, 16)
    f32 = jnp.float32
    x = jax.random.normal(ks[0], (BATCH, SEQ, D_MODEL), f32)
    norm_w = 1.0 + 0.02 * jax.random.normal(ks[1], (DEPTH, D_MODEL), f32)
    w_in = jax.random.normal(ks[2], (DEPTH, D_MODEL, D_IN), f32) * D_MODEL ** -0.5
    sgu_ln_g = 1.0 + 0.02 * jax.random.normal(ks[3], (DEPTH, D_SGU), f32)
    sgu_ln_b = 0.02 * jax.random.normal(ks[4], (DEPTH, D_SGU), f32)
    sgu_w = jax.random.normal(ks[5], (DEPTH, SGU_GROUPS, SGU_CHUNK, SGU_CHUNK), f32) * SGU_CHUNK ** -0.5
    sgu_b = 1.0 + 0.1 * jax.random.normal(ks[6], (DEPTH, SGU_GROUPS, SGU_CHUNK), f32)
    conv_w = jax.random.normal(ks[7], (DEPTH, CONV_WIDTH, 3 * D_DN), f32) * CONV_WIDTH ** -0.5
    a_log_f = jnp.log(jax.random.uniform(ks[8], (DEPTH, DN_HEADS), f32, 1.0, 16.0))
    a_log_b = jnp.log(jax.random.uniform(ks[9], (DEPTH, DN_HEADS), f32, 1.0, 16.0))
    dt_f = jnp.exp(jax.random.uniform(ks[10], (DEPTH, DN_HEADS), f32, np.log(1e-3), np.log(1e-1)))
    dt_b = jnp.exp(jax.random.uniform(ks[11], (DEPTH, DN_HEADS), f32, np.log(1e-3), np.log(1e-1)))
    dt_bias_f = dt_f + jnp.log(-jnp.expm1(-dt_f))
    dt_bias_b = dt_b + jnp.log(-jnp.expm1(-dt_b))
    dn_norm_w = 1.0 + 0.02 * jax.random.normal(ks[12], (DEPTH, DN_HEAD_DIM), f32)
    w_out = jax.random.normal(ks[13], (DEPTH, D_MIX, D_MODEL), f32) * D_MIX ** -0.5
    final_norm_w = 1.0 + 0.02 * jax.random.normal(ks[14], (D_MODEL,), f32)
    return {"x": x, "norm_w": norm_w, "w_in": w_in, "sgu_ln_g": sgu_ln_g,
            "sgu_ln_b": sgu_ln_b, "sgu_w": sgu_w, "sgu_b": sgu_b, "conv_w": conv_w,
            "a_log_f": a_log_f, "a_log_b": a_log_b, "dt_bias_f": dt_bias_f,
            "dt_bias_b": dt_bias_b, "dn_norm_w": dn_norm_w, "w_out": w_out,
            "final_norm_w": final_norm_w}


def _fwd_reference(x, norm_w, w_in, sgu_ln_g, sgu_ln_b, sgu_w, sgu_b, conv_w,
              a_log_f, a_log_b, dt_bias_f, dt_bias_b, dn_norm_w, w_out,
              final_norm_w):
    for l in range(DEPTH):
        h = _rmsnorm(x, norm_w[l])
        proj = jnp.einsum("bsd,de->bse", h, w_in[l])
        u, v, z_a, qkv, z_b, a_f, a_b, b_f, b_b = _split_in(proj)
        y_a = _sgu_branch(u, v, z_a, sgu_ln_g[l], sgu_ln_b[l], sgu_w[l], sgu_b[l])
        y_b = _deltanet_branch(qkv, z_b, a_f, a_b, b_f, b_b, conv_w[l], a_log_f[l],
                               a_log_b[l], dt_bias_f[l], dt_bias_b[l], dn_norm_w[l])
        y = jnp.concatenate([y_a, y_b], axis=-1)
        x = x + jnp.einsum("bse,ed->bsd", y, w_out[l])
    return _rmsnorm(x, final_norm_w)


import jax as _jax
import jax.numpy as _jnp

TWIN_FORMAT = 'train_step'
FWD_PARAMS = ['x', 'norm_w', 'w_in', 'sgu_ln_g', 'sgu_ln_b', 'sgu_w', 'sgu_b', 'conv_w', 'a_log_f', 'a_log_b', 'dt_bias_f', 'dt_bias_b', 'dn_norm_w', 'w_out', 'final_norm_w']
TWIN_WEIGHTS = ['norm_w', 'w_in', 'sgu_ln_g', 'sgu_ln_b', 'sgu_w', 'sgu_b', 'conv_w', 'a_log_f', 'a_log_b', 'dt_bias_f', 'dt_bias_b', 'dn_norm_w', 'w_out', 'final_norm_w']
TWIN_DIFF_INPUT = 'x'
TWIN_INPUTS = ['x', 'norm_w', 'w_in', 'sgu_ln_g', 'sgu_ln_b', 'sgu_w', 'sgu_b', 'conv_w', 'a_log_f', 'a_log_b', 'dt_bias_f', 'dt_bias_b', 'dn_norm_w', 'w_out', 'final_norm_w', 'loss_target', 'm_norm_w', 'm_w_in', 'm_sgu_ln_g', 'm_sgu_ln_b', 'm_sgu_w', 'm_sgu_b', 'm_conv_w', 'm_a_log_f', 'm_a_log_b', 'm_dt_bias_f', 'm_dt_bias_b', 'm_dn_norm_w', 'm_w_out', 'm_final_norm_w', 'v_norm_w', 'v_w_in', 'v_sgu_ln_g', 'v_sgu_ln_b', 'v_sgu_w', 'v_sgu_b', 'v_conv_w', 'v_a_log_f', 'v_a_log_b', 'v_dt_bias_f', 'v_dt_bias_b', 'v_dn_norm_w', 'v_w_out', 'v_final_norm_w']
TWIN_OUTPUTS = ['loss', 'grad_x', 'grad_norm_w', 'grad_w_in', 'grad_sgu_ln_g', 'grad_sgu_ln_b', 'grad_sgu_w', 'grad_sgu_b', 'grad_conv_w', 'grad_a_log_f', 'grad_a_log_b', 'grad_dt_bias_f', 'grad_dt_bias_b', 'grad_dn_norm_w', 'grad_w_out', 'grad_final_norm_w', 'delta_norm_w', 'delta_w_in', 'delta_sgu_ln_g', 'delta_sgu_ln_b', 'delta_sgu_w', 'delta_sgu_b', 'delta_conv_w', 'delta_a_log_f', 'delta_a_log_b', 'delta_dt_bias_f', 'delta_dt_bias_b', 'delta_dn_norm_w', 'delta_w_out', 'delta_final_norm_w', 'new_m_norm_w', 'new_m_w_in', 'new_m_sgu_ln_g', 'new_m_sgu_ln_b', 'new_m_sgu_w', 'new_m_sgu_b', 'new_m_conv_w', 'new_m_a_log_f', 'new_m_a_log_b', 'new_m_dt_bias_f', 'new_m_dt_bias_b', 'new_m_dn_norm_w', 'new_m_w_out', 'new_m_final_norm_w', 'new_v_norm_w', 'new_v_w_in', 'new_v_sgu_ln_g', 'new_v_sgu_ln_b', 'new_v_sgu_w', 'new_v_sgu_b', 'new_v_conv_w', 'new_v_a_log_f', 'new_v_a_log_b', 'new_v_dt_bias_f', 'new_v_dt_bias_b', 'new_v_dn_norm_w', 'new_v_w_out', 'new_v_final_norm_w']
TWIN_LEAF_KINDS = {'loss': 'loss', 'grad_x': 'grad_x', 'grad_norm_w': 'grad_w', 'grad_w_in': 'grad_w', 'grad_sgu_ln_g': 'grad_w', 'grad_sgu_ln_b': 'grad_w', 'grad_sgu_w': 'grad_w', 'grad_sgu_b': 'grad_w', 'grad_conv_w': 'grad_w', 'grad_a_log_f': 'grad_w', 'grad_a_log_b': 'grad_w', 'grad_dt_bias_f': 'grad_w', 'grad_dt_bias_b': 'grad_w', 'grad_dn_norm_w': 'grad_w', 'grad_w_out': 'grad_w', 'grad_final_norm_w': 'grad_w', 'delta_norm_w': 'delta_w', 'delta_w_in': 'delta_w', 'delta_sgu_ln_g': 'delta_w', 'delta_sgu_ln_b': 'delta_w', 'delta_sgu_w': 'delta_w', 'delta_sgu_b': 'delta_w', 'delta_conv_w': 'delta_w', 'delta_a_log_f': 'delta_w', 'delta_a_log_b': 'delta_w', 'delta_dt_bias_f': 'delta_w', 'delta_dt_bias_b': 'delta_w', 'delta_dn_norm_w': 'delta_w', 'delta_w_out': 'delta_w', 'delta_final_norm_w': 'delta_w', 'new_m_norm_w': 'new_m', 'new_m_w_in': 'new_m', 'new_m_sgu_ln_g': 'new_m', 'new_m_sgu_ln_b': 'new_m', 'new_m_sgu_w': 'new_m', 'new_m_sgu_b': 'new_m', 'new_m_conv_w': 'new_m', 'new_m_a_log_f': 'new_m', 'new_m_a_log_b': 'new_m', 'new_m_dt_bias_f': 'new_m', 'new_m_dt_bias_b': 'new_m', 'new_m_dn_norm_w': 'new_m', 'new_m_w_out': 'new_m', 'new_m_final_norm_w': 'new_m', 'new_v_norm_w': 'new_v', 'new_v_w_in': 'new_v', 'new_v_sgu_ln_g': 'new_v', 'new_v_sgu_ln_b': 'new_v', 'new_v_sgu_w': 'new_v', 'new_v_sgu_b': 'new_v', 'new_v_conv_w': 'new_v', 'new_v_a_log_f': 'new_v', 'new_v_a_log_b': 'new_v', 'new_v_dt_bias_f': 'new_v', 'new_v_dt_bias_b': 'new_v', 'new_v_dn_norm_w': 'new_v', 'new_v_w_out': 'new_v', 'new_v_final_norm_w': 'new_v'}


def _forward(args):
    return _fwd_reference(*[args[k] for k in FWD_PARAMS])


def _output_shape():
    def fwd():
        inp = _fwd_setup_inputs(0)
        return _fwd_reference(*[inp[k] for k in FWD_PARAMS])
    out = _jax.eval_shape(fwd)
    return out.shape, out.dtype

N_MICROBATCH = 1
ADAM_LR = 0.001
ADAM_B1 = 0.9
ADAM_B2 = 0.999
ADAM_EPS = 1e-08
ADAM_WD = 0.01
ADAM_STEP = 10
PER_EXAMPLE_BATCH_AXIS = {'x': 0, 'loss_target': 0}
SHARED_INPUTS = []
_WEIGHT_DTYPES = {'norm_w': _jnp.float32, 'w_in': _jnp.float32, 'sgu_ln_g': _jnp.float32, 'sgu_ln_b': _jnp.float32, 'sgu_w': _jnp.float32, 'sgu_b': _jnp.float32, 'conv_w': _jnp.float32, 'a_log_f': _jnp.float32, 'a_log_b': _jnp.float32, 'dt_bias_f': _jnp.float32, 'dt_bias_b': _jnp.float32, 'dn_norm_w': _jnp.float32, 'w_out': _jnp.float32, 'final_norm_w': _jnp.float32}
MOMENT_SCALE = {'norm_w': 1.114126e-01, 'w_in': 5.887219e-02, 'sgu_ln_g': 4.512898e-02, 'sgu_ln_b': 4.484742e-02, 'sgu_w': 4.573728e-02, 'sgu_b': 4.586053e-02, 'conv_w': 5.192734e-02, 'a_log_f': 1.932562e-01, 'a_log_b': 1.985882e-01, 'dt_bias_f': 1.778791e-01, 'dt_bias_b': 1.927330e-01, 'dn_norm_w': 2.271471e-01, 'w_out': 6.889443e-02, 'final_norm_w': 3.196331e+01}


def _to_microbatches(a, axis):
    t = _jnp.moveaxis(a, axis, 0)
    t = t.reshape((N_MICROBATCH, t.shape[0] // N_MICROBATCH) + t.shape[1:])
    return _jnp.moveaxis(t, 1, axis + 1)


def setup_inputs(seed: int = 0) -> dict:
    inp = _fwd_setup_inputs(seed)
    key = _jax.random.fold_in(_jax.random.key(seed), 7919)
    shape, _ = _output_shape()
    out = dict(inp)
    out["loss_target"] = _jax.random.normal(_jax.random.fold_in(key, 0), shape, _jnp.float32)
    for i, name in enumerate(TWIN_WEIGHTS):
        w = inp[name].astype(_jnp.float32)
        if MOMENT_SCALE is None:
            s = _jnp.sqrt(_jnp.mean(_jnp.square(w)) + 1e-30)
        else:
            s = MOMENT_SCALE[name]
        km, kv = _jax.random.split(_jax.random.fold_in(key, i + 1))
        out[name] = w
        out["m_" + name] = s * _jax.random.normal(km, w.shape, _jnp.float32)
        out["v_" + name] = (s * s) * _jax.random.uniform(kv, w.shape, _jnp.float32, 0.5, 1.5)
    if N_MICROBATCH > 1:
        for name, axis in PER_EXAMPLE_BATCH_AXIS.items():
            out[name] = _to_microbatches(out[name], axis)
    return {'x': out['x'], 'norm_w': out['norm_w'], 'w_in': out['w_in'], 'sgu_ln_g': out['sgu_ln_g'], 'sgu_ln_b': out['sgu_ln_b'], 'sgu_w': out['sgu_w'], 'sgu_b': out['sgu_b'], 'conv_w': out['conv_w'], 'a_log_f': out['a_log_f'], 'a_log_b': out['a_log_b'], 'dt_bias_f': out['dt_bias_f'], 'dt_bias_b': out['dt_bias_b'], 'dn_norm_w': out['dn_norm_w'], 'w_out': out['w_out'], 'final_norm_w': out['final_norm_w'], 'loss_target': out['loss_target'], 'm_norm_w': out['m_norm_w'], 'm_w_in': out['m_w_in'], 'm_sgu_ln_g': out['m_sgu_ln_g'], 'm_sgu_ln_b': out['m_sgu_ln_b'], 'm_sgu_w': out['m_sgu_w'], 'm_sgu_b': out['m_sgu_b'], 'm_conv_w': out['m_conv_w'], 'm_a_log_f': out['m_a_log_f'], 'm_a_log_b': out['m_a_log_b'], 'm_dt_bias_f': out['m_dt_bias_f'], 'm_dt_bias_b': out['m_dt_bias_b'], 'm_dn_norm_w': out['m_dn_norm_w'], 'm_w_out': out['m_w_out'], 'm_final_norm_w': out['m_final_norm_w'], 'v_norm_w': out['v_norm_w'], 'v_w_in': out['v_w_in'], 'v_sgu_ln_g': out['v_sgu_ln_g'], 'v_sgu_ln_b': out['v_sgu_ln_b'], 'v_sgu_w': out['v_sgu_w'], 'v_sgu_b': out['v_sgu_b'], 'v_conv_w': out['v_conv_w'], 'v_a_log_f': out['v_a_log_f'], 'v_a_log_b': out['v_a_log_b'], 'v_dt_bias_f': out['v_dt_bias_f'], 'v_dt_bias_b': out['v_dt_bias_b'], 'v_dn_norm_w': out['v_dn_norm_w'], 'v_w_out': out['v_w_out'], 'v_final_norm_w': out['v_final_norm_w']}


def _loss(weights, diff, rest, loss_target):
    with _jax.named_scope("forward"):
        args = {**rest, TWIN_DIFF_INPUT: diff, **{k: w.astype(_WEIGHT_DTYPES[k]) for k, w in weights.items()}}
        y = _forward(args)
    with _jax.named_scope("loss_head"):
        err = _jnp.square(y.astype(_jnp.float32) - loss_target)
        return 0.5 * _jnp.sum(_jnp.mean(err, axis=-1)) if err.ndim else 0.5 * err


def _adamw(w, g, m, v):
    m = ADAM_B1 * m + (1.0 - ADAM_B1) * g
    v = ADAM_B2 * v + (1.0 - ADAM_B2) * _jnp.square(g)
    m_hat = m / (1.0 - ADAM_B1 ** ADAM_STEP)
    v_hat = v / (1.0 - ADAM_B2 ** ADAM_STEP)
    delta = -ADAM_LR * (m_hat / (_jnp.sqrt(v_hat) + ADAM_EPS) + ADAM_WD * w)
    return delta, m, v


def reference(x, norm_w, w_in, sgu_ln_g, sgu_ln_b, sgu_w, sgu_b, conv_w, a_log_f, a_log_b, dt_bias_f, dt_bias_b, dn_norm_w, w_out, final_norm_w, loss_target, m_norm_w, m_w_in, m_sgu_ln_g, m_sgu_ln_b, m_sgu_w, m_sgu_b, m_conv_w, m_a_log_f, m_a_log_b, m_dt_bias_f, m_dt_bias_b, m_dn_norm_w, m_w_out, m_final_norm_w, v_norm_w, v_w_in, v_sgu_ln_g, v_sgu_ln_b, v_sgu_w, v_sgu_b, v_conv_w, v_a_log_f, v_a_log_b, v_dt_bias_f, v_dt_bias_b, v_dn_norm_w, v_w_out, v_final_norm_w):
    given = dict(x=x, norm_w=norm_w, w_in=w_in, sgu_ln_g=sgu_ln_g, sgu_ln_b=sgu_ln_b, sgu_w=sgu_w, sgu_b=sgu_b, conv_w=conv_w, a_log_f=a_log_f, a_log_b=a_log_b, dt_bias_f=dt_bias_f, dt_bias_b=dt_bias_b, dn_norm_w=dn_norm_w, w_out=w_out, final_norm_w=final_norm_w, loss_target=loss_target, m_norm_w=m_norm_w, m_w_in=m_w_in, m_sgu_ln_g=m_sgu_ln_g, m_sgu_ln_b=m_sgu_ln_b, m_sgu_w=m_sgu_w, m_sgu_b=m_sgu_b, m_conv_w=m_conv_w, m_a_log_f=m_a_log_f, m_a_log_b=m_a_log_b, m_dt_bias_f=m_dt_bias_f, m_dt_bias_b=m_dt_bias_b, m_dn_norm_w=m_dn_norm_w, m_w_out=m_w_out, m_final_norm_w=m_final_norm_w, v_norm_w=v_norm_w, v_w_in=v_w_in, v_sgu_ln_g=v_sgu_ln_g, v_sgu_ln_b=v_sgu_ln_b, v_sgu_w=v_sgu_w, v_sgu_b=v_sgu_b, v_conv_w=v_conv_w, v_a_log_f=v_a_log_f, v_a_log_b=v_a_log_b, v_dt_bias_f=v_dt_bias_f, v_dt_bias_b=v_dt_bias_b, v_dn_norm_w=v_dn_norm_w, v_w_out=v_w_out, v_final_norm_w=v_final_norm_w)
    weights = {n: given[n] for n in TWIN_WEIGHTS}
    shared = {n: given[n] for n in SHARED_INPUTS}
    per_example = {n: given[n] for n in ['x']}
    grad_fn = _jax.value_and_grad(_loss, argnums=(0, 1))

    def one_microbatch(ex, loss_target):
        ex = dict(ex)
        diff = ex.pop(TWIN_DIFF_INPUT)
        return grad_fn(weights, diff, {**shared, **ex}, loss_target)

    if N_MICROBATCH == 1:
        loss, (grad_w, grad_x) = one_microbatch(per_example, given["loss_target"])
    else:
        def body(carry, xs):
            loss_sum, grad_sum = carry
            l_k, (gw_k, gx_k) = one_microbatch(xs[0], xs[1])
            with _jax.named_scope("update"):
                return (loss_sum + l_k, _jax.tree.map(_jnp.add, grad_sum, gw_k)), gx_k

        init = (_jnp.zeros((), _jnp.float32), _jax.tree.map(_jnp.zeros_like, weights))
        (loss, grad_w), grad_x = _jax.lax.scan(body, init, (per_example, given["loss_target"]))
    with _jax.named_scope("update"):
        delta_w, new_m, new_v = {}, {}, {}
        for n in TWIN_WEIGHTS:
            delta_w[n], new_m[n], new_v[n] = _adamw(weights[n], grad_w[n], given["m_" + n], given["v_" + n])
    return (loss, grad_x, *[grad_w[n] for n in TWIN_WEIGHTS], *[delta_w[n] for n in TWIN_WEIGHTS],
            *[new_m[n] for n in TWIN_WEIGHTS], *[new_v[n] for n in TWIN_WEIGHTS])
```

```python
import functools

import numpy as np
import jax
import jax.numpy as jnp
from jax import lax
from jax.experimental import pallas as pl
from jax.experimental.pallas import tpu as pltpu

F32 = jnp.float32
BF16 = jnp.bfloat16
HIGHEST = lax.Precision.HIGHEST

N_DEV = 8
DEPTH = 4
D_MODEL = 2048
D_SGU = 1024
D_DN = 1024
N_GROUPS = 8
HEAD = 128
SGU_CHUNK = 128
DN_CHUNK = 64
CONV_W = 5
N_GATE = 32
D_IN = 3 * D_SGU + 4 * D_DN + N_GATE
LANE = 128
HALO = 8
EPS = 1e-6
ADAM_LR, ADAM_B1, ADAM_B2, ADAM_EPS, ADAM_WD, ADAM_STEP = 0.001, 0.9, 0.999, 1e-08, 0.01, 10
FLAT_W = 1024
VMEM_MARGIN = 8 << 20

MESH_AXES = ("x", "y", "c")
WEIGHTS = ("norm_w", "w_in", "sgu_ln_g", "sgu_ln_b", "sgu_w", "sgu_b", "conv_w", "a_log_f", "a_log_b",
           "dt_bias_f", "dt_bias_b", "dn_norm_w", "w_out", "final_norm_w")
SHARDED = ("w_in", "w_out", "conv_w")
FLAT_ORDER = SHARDED + tuple(n for n in WEIGHTS if n not in SHARDED)


def _bdot(a, b, dims):
    return lax.dot_general(a.astype(BF16), b.astype(BF16), (dims, ((), ())), preferred_element_type=F32)


@jax.custom_vjp
def _mm_nn(a, b):
    return _bdot(a, b, ((1,), (0,)))


def _mm_nn_fwd(a, b):
    return _mm_nn(a, b), (a, b)


def _mm_nn_bwd(res, ct):
    a, b = res
    return _bdot(ct, b, ((1,), (1,))), _bdot(a, ct, ((0,), (0,)))


_mm_nn.defvjp(_mm_nn_fwd, _mm_nn_bwd)


@jax.custom_vjp
def _mm_nt(a, b):
    return _bdot(a, b, ((1,), (1,)))


def _mm_nt_fwd(a, b):
    return _mm_nt(a, b), (a, b)


def _mm_nt_bwd(res, ct):
    a, b = res
    return _bdot(ct, b, ((1,), (0,))), _bdot(ct, a, ((0,), (0,)))


_mm_nt.defvjp(_mm_nt_fwd, _mm_nt_bwd)


@jax.custom_vjp
def _mm_tn(a, b):
    return _bdot(a, b, ((0,), (0,)))


def _mm_tn_fwd(a, b):
    return _mm_tn(a, b), (a, b)


def _mm_tn_bwd(res, ct):
    a, b = res
    return _bdot(b, ct, ((1,), (1,))), _bdot(a, ct, ((1,), (0,)))


_mm_tn.defvjp(_mm_tn_fwd, _mm_tn_bwd)


@jax.custom_vjp
def _split_lanes(x):
    return tuple(x[:, i * LANE:(i + 1) * LANE] for i in range(x.shape[1] // LANE))


def _split_lanes_fwd(x):
    return _split_lanes(x), None


def _split_lanes_bwd(_, cts):
    return (jnp.concatenate(cts, axis=1),)


_split_lanes.defvjp(_split_lanes_fwd, _split_lanes_bwd)


def _silu(t):
    return t * jax.nn.sigmoid(t)


def _gelu(t):
    return 0.5 * t * (1.0 + lax.erf(t * np.float32(0.7071067811865476)))


def _rms(x, w):
    return x * lax.rsqrt(jnp.mean(x * x, axis=-1, keepdims=True) + EPS) * w


def _params(sem, vmem_bytes=None):
    kw = dict(dimension_semantics=sem)
    if vmem_bytes is not None:
        kw["vmem_limit_bytes"] = int(vmem_bytes)
    return pltpu.CompilerParams(**kw)


def _matmul(pairs, *, name, trans_b=False, add=None, out_dtype=F32, tm=512, tn=512):
    m = pairs[0][0].shape[0]
    n = pairs[0][1].shape[0] if trans_b else pairs[0][1].shape[1]
    tm, tn = min(tm, m), min(tn, n)
    assert m % tm == 0 and n % tn == 0
    n_pairs = len(pairs)
    contract = ((1,), (1,)) if trans_b else ((1,), (0,))

    def body(*refs):
        o_ref = refs[-1]
        acc = None
        for i in range(n_pairs):
            d = lax.dot_general(refs[2 * i][...].astype(BF16), refs[2 * i + 1][...].astype(BF16),
                                (contract, ((), ())), preferred_element_type=F32)
            acc = d if acc is None else acc + d
        if add is not None:
            acc = acc + refs[2 * n_pairs][...]
        o_ref[...] = acc.astype(out_dtype)

    in_specs, args, vmem = [], [], 0
    for a, b in pairs:
        k = a.shape[1]
        in_specs.append(pl.BlockSpec((tm, k), lambda i, j: (i, 0)))
        in_specs.append(pl.BlockSpec((tn, k), lambda i, j: (j, 0)) if trans_b else pl.BlockSpec((k, tn), lambda i, j: (0, j)))
        args += [a, b]
        vmem += 2 * (tm * k * a.dtype.itemsize + tn * k * b.dtype.itemsize)
    if add is not None:
        in_specs.append(pl.BlockSpec((tm, tn), lambda i, j: (i, j)))
        args.append(add)
        vmem += 2 * tm * tn * 4
    vmem += 2 * tm * tn * jnp.dtype(out_dtype).itemsize + 2 * tm * tn * 4
    return pl.pallas_call(
        body, name=name, grid=(m // tm, n // tn), in_specs=in_specs,
        out_specs=pl.BlockSpec((tm, tn), lambda i, j: (i, j)),
        out_shape=jax.ShapeDtypeStruct((m, n), out_dtype),
        compiler_params=_params(("parallel", "arbitrary"), vmem + VMEM_MARGIN),
    )(*args)


def _rms_fwd(x, w, *, tile=512):
    s, d = x.shape
    tile = min(tile, s)

    def body(x_ref, w_ref, h_ref):
        h_ref[...] = _rms(x_ref[...], w_ref[...]).astype(BF16)

    return pl.pallas_call(
        body, name="rms_fwd", grid=(s // tile,),
        in_specs=[pl.BlockSpec((tile, d), lambda i: (i, 0)), pl.BlockSpec((1, d), lambda i: (0, 0))],
        out_specs=pl.BlockSpec((tile, d), lambda i: (i, 0)),
        out_shape=jax.ShapeDtypeStruct((s, d), BF16),
        compiler_params=_params(("arbitrary",)),
    )(x, w)


def _rms_bwd(x, w, dh, dx_out, *, tile=256):
    s, d = x.shape
    tile = min(tile, s)

    def body(x_ref, w_ref, dh_ref, dxo_ref, dx_ref, dxb_ref, dw_ref):
        _, vjp = jax.vjp(_rms, x_ref[...], w_ref[...])
        dxv, dwv = vjp(dh_ref[...])
        dx = dxo_ref[...] + dxv
        dx_ref[...] = dx
        dxb_ref[...] = dx.astype(BF16)

        @pl.when(pl.program_id(0) == 0)
        def _():
            dw_ref[...] = jnp.zeros_like(dw_ref)

        dw_ref[...] += dwv

    row = pl.BlockSpec((tile, d), lambda i: (i, 0))
    vec = pl.BlockSpec((1, d), lambda i: (0, 0))
    return pl.pallas_call(
        body, name="rms_bwd", grid=(s // tile,),
        in_specs=[row, vec, row, row], out_specs=[row, row, vec],
        out_shape=[jax.ShapeDtypeStruct((s, d), F32), jax.ShapeDtypeStruct((s, d), BF16), jax.ShapeDtypeStruct((1, d), F32)],
        compiler_params=_params(("arbitrary",)),
    )(x, w, dh, dx_out)


def _final_loss(x, w, target, *, tile=256):
    s, d = x.shape
    tile = min(tile, s)

    def body(x_ref, w_ref, t_ref, loss_ref, dx_ref, dxb_ref, dw_ref):
        def f(xv, wv):
            err = jnp.square(_rms(xv, wv) - t_ref[...])
            per_token = jnp.sum(err, axis=1, keepdims=True) * np.float32(1.0 / d)
            return 0.5 * jnp.sum(per_token, axis=0, keepdims=True)

        loss, vjp = jax.vjp(f, x_ref[...], w_ref[...])
        dxv, dwv = vjp(jnp.ones((1, 1), F32))
        dx_ref[...] = dxv
        dxb_ref[...] = dxv.astype(BF16)

        @pl.when(pl.program_id(0) == 0)
        def _():
            dw_ref[...] = jnp.zeros_like(dw_ref)
            loss_ref[...] = jnp.zeros_like(loss_ref)

        dw_ref[...] += dwv
        loss_ref[...] += jnp.broadcast_to(loss, (1, LANE))

    row = pl.BlockSpec((tile, d), lambda i: (i, 0))
    vec = pl.BlockSpec((1, d), lambda i: (0, 0))
    return pl.pallas_call(
        body, name="final_loss", grid=(s // tile,),
        in_specs=[row, vec, row], out_specs=[pl.BlockSpec((1, LANE), lambda i: (0, 0)), row, row, vec],
        out_shape=[jax.ShapeDtypeStruct((1, LANE), F32), jax.ShapeDtypeStruct((s, d), F32),
                   jax.ShapeDtypeStruct((s, d), BF16), jax.ShapeDtypeStruct((1, d), F32)],
        compiler_params=_params(("arbitrary",)),
    )(x, w, target)


def _sgu_chunk(u, v, z, ln_g, ln_b, ws, bcols):
    vg = _gelu(v)
    xc = vg - jnp.mean(vg, axis=-1, keepdims=True)
    vl = xc * lax.rsqrt(jnp.mean(xc * xc, axis=-1, keepdims=True) + EPS) * ln_g + ln_b
    sp = jnp.concatenate([_mm_nn(ws[g], vg_) + bcols[g] for g, vg_ in enumerate(_split_lanes(vl))], axis=1)
    return _gelu(u) * sp * _silu(z)


def _sgu_specs(tile):
    col = lambda j: pl.BlockSpec((tile, D_SGU), lambda i, j=j: (i, j))
    vec = pl.BlockSpec((1, D_SGU), lambda i: (0, 0))
    wspec = pl.BlockSpec((N_GROUPS, SGU_CHUNK, SGU_CHUNK), lambda i: (0, 0, 0))
    bspec = pl.BlockSpec((SGU_CHUNK, N_GROUPS), lambda i: (0, 0))
    return col, vec, wspec, bspec


def _sgu_fwd(proj, ln_g, ln_b, w_s, b_t, *, tile=256):
    s = proj.shape[0]
    tile = min(tile, s)
    col, vec, wspec, bspec = _sgu_specs(tile)

    def body(u_ref, v_ref, z_ref, g_ref, b_ref, w_ref, bt_ref, y_ref):
        ws = tuple(w_ref[g] for g in range(N_GROUPS))
        bcols = tuple(bt_ref[:, g:g + 1] for g in range(N_GROUPS))
        for c in range(tile // SGU_CHUNK):
            rows = pl.ds(c * SGU_CHUNK, SGU_CHUNK)
            y = _sgu_chunk(u_ref[rows, :], v_ref[rows, :], z_ref[rows, :], g_ref[...], b_ref[...], ws, bcols)
            y_ref[rows, :] = y.astype(BF16)

    return pl.pallas_call(
        body, name="sgu_fwd", grid=(s // tile,),
        in_specs=[col(0), col(1), col(2), vec, vec, wspec, bspec],
        out_specs=pl.BlockSpec((tile, D_SGU), lambda i: (i, 0)),
        out_shape=jax.ShapeDtypeStruct((s, D_SGU), BF16),
        compiler_params=_params(("arbitrary",)),
    )(proj, proj, proj, ln_g, ln_b, w_s, b_t)


def _sgu_bwd(proj, ln_g, ln_b, w_s, b_t, dy, *, tile=128):
    s = proj.shape[0]
    tile = min(tile, s)
    col, vec, wspec, bspec = _sgu_specs(tile)

    def body(u_ref, v_ref, z_ref, g_ref, b_ref, w_ref, bt_ref, dy_ref, dp_ref, dg_ref, db_ref, dw_ref, dbt_ref):
        @pl.when(pl.program_id(0) == 0)
        def _():
            dg_ref[...] = jnp.zeros_like(dg_ref)
            db_ref[...] = jnp.zeros_like(db_ref)
            dw_ref[...] = jnp.zeros_like(dw_ref)
            dbt_ref[...] = jnp.zeros_like(dbt_ref)

        ws = tuple(w_ref[g] for g in range(N_GROUPS))
        bcols = tuple(bt_ref[:, g:g + 1] for g in range(N_GROUPS))
        for c in range(tile // SGU_CHUNK):
            rows = pl.ds(c * SGU_CHUNK, SGU_CHUNK)
            _, vjp = jax.vjp(_sgu_chunk, u_ref[rows, :], v_ref[rows, :], z_ref[rows, :], g_ref[...], b_ref[...], ws, bcols)
            du, dv, dz, dg, db, dws, dbc = vjp(dy_ref[rows, :])
            dp_ref[rows, pl.ds(0, D_SGU)] = du.astype(BF16)
            dp_ref[rows, pl.ds(D_SGU, D_SGU)] = dv.astype(BF16)
            dp_ref[rows, pl.ds(2 * D_SGU, D_SGU)] = dz.astype(BF16)
            dg_ref[...] += dg
            db_ref[...] += db
            for g in range(N_GROUPS):
                dw_ref[g] += dws[g]
                dbt_ref[:, g:g + 1] += dbc[g]

    return pl.pallas_call(
        body, name="sgu_bwd", grid=(s // tile,),
        in_specs=[col(0), col(1), col(2), vec, vec, wspec, bspec, pl.BlockSpec((tile, D_SGU), lambda i: (i, 0))],
        out_specs=[pl.BlockSpec((tile, 3 * D_SGU), lambda i: (i, 0)), vec, vec, wspec, bspec],
        out_shape=[jax.ShapeDtypeStruct((s, 3 * D_SGU), BF16), jax.ShapeDtypeStruct((1, D_SGU), F32),
                   jax.ShapeDtypeStruct((1, D_SGU), F32), jax.ShapeDtypeStruct((N_GROUPS, SGU_CHUNK, SGU_CHUNK), F32),
                   jax.ShapeDtypeStruct((SGU_CHUNK, N_GROUPS), F32)],
        compiler_params=_params(("arbitrary",)),
    )(proj, proj, proj, ln_g, ln_b, w_s, b_t, dy)


def _qkv_head(cq, ck, cv):
    q, k, v = _silu(cq), _silu(ck), _silu(cv)
    q = q * lax.rsqrt(jnp.sum(q * q, axis=-1, keepdims=True) + EPS) * np.float32(HEAD ** -0.5)
    k = k * lax.rsqrt(jnp.sum(k * k, axis=-1, keepdims=True) + EPS)
    return q, k, v


def _gate_fn(p, a_log, dt_bias):
    lane = lax.broadcasted_iota(jnp.int32, p.shape, 1)
    g = -jnp.exp(a_log) * jax.nn.softplus(p + dt_bias)
    return jnp.where(lane < 16, g, jnp.where(lane < N_GATE, jax.nn.sigmoid(p), 0.0))


def _halo_maps(tile, s):
    per = tile // HALO
    prev = lambda i: jnp.maximum(i * per - 1, 0)
    nxt = lambda i: jnp.minimum((i + 1) * per, s // HALO - 1)
    return prev, nxt


def _prep_fwd(proj, conv_w, pg, a_log, dt_bias, *, tile=256):
    s, width = proj.shape
    tile = min(tile, s)
    n_blk = s // tile
    prev, nxt = _halo_maps(tile, s)

    def body(x_ref, xp_ref, xn_ref, w_ref, pg_ref, al_ref, dt_ref, q_ref, k_ref, v_ref, gate_ref, ext):
        i = pl.program_id(0)
        ext[pl.ds(0, HALO), :] = jnp.where(i > 0, xp_ref[...], 0.0)
        ext[pl.ds(HALO, tile), :] = x_ref[...]
        ext[pl.ds(HALO + tile, HALO), :] = jnp.where(i < n_blk - 1, xn_ref[...], 0.0)
        c = None
        for j in range(CONV_W):
            term = ext[pl.ds(HALO - CONV_W // 2 + j, tile), :] * w_ref[j:j + 1, :]
            c = term if c is None else c + term
        for h in range(N_GROUPS):
            q, k, v = _qkv_head(c[:, h * HEAD:(h + 1) * HEAD], c[:, D_DN + h * HEAD:D_DN + (h + 1) * HEAD],
                                c[:, 2 * D_DN + h * HEAD:2 * D_DN + (h + 1) * HEAD])
            q_ref[h] = q
            k_ref[h] = k
            v_ref[h] = v
        gate_ref[...] = _gate_fn(pg_ref[...], al_ref[...], dt_ref[...])

    hm = pl.BlockSpec((N_GROUPS, tile, HEAD), lambda i: (0, i, 0))
    lane_vec = pl.BlockSpec((1, LANE), lambda i: (0, 0))
    return pl.pallas_call(
        body, name="prep_fwd", grid=(n_blk,),
        in_specs=[pl.BlockSpec((tile, width), lambda i: (i, 0)),
                  pl.BlockSpec((HALO, width), lambda i: (prev(i), 0)),
                  pl.BlockSpec((HALO, width), lambda i: (nxt(i), 0)),
                  pl.BlockSpec((HALO, width), lambda i: (0, 0)),
                  pl.BlockSpec((tile, LANE), lambda i: (i, 0)), lane_vec, lane_vec],
        out_specs=[hm, hm, hm, pl.BlockSpec((tile, LANE), lambda i: (i, 0))],
        out_shape=[jax.ShapeDtypeStruct((N_GROUPS, s, HEAD), F32)] * 3 + [jax.ShapeDtypeStruct((s, LANE), F32)],
        scratch_shapes=[pltpu.VMEM((tile + 2 * HALO, width), F32)],
        compiler_params=_params(("arbitrary",)),
    )(proj, proj, proj, conv_w, pg, a_log, dt_bias)


def _prep_bwd(proj, conv_w, pg, a_log, dt_bias, dq, dk, dv, dgates, *, tile=128):
    s, width = proj.shape
    tile = min(tile, s)
    n_blk = s // tile
    prev, nxt = _halo_maps(tile, s)
    ext_rows = tile + 2 * HALO
    pad = CONV_W // 2

    def body(x_ref, xp_ref, xn_ref, w_ref, pg_ref, al_ref, dt_ref,
             dq_ref, dqp_ref, dqn_ref, dk_ref, dkp_ref, dkn_ref, dv_ref, dvp_ref, dvn_ref,
             dg0_ref, dg1_ref, dg2_ref,
             dp_ref, dpg_ref, dw_ref, dal_ref, ddt_ref, xext, dcext):
        i = pl.program_id(0)
        first, last = i > 0, i < n_blk - 1

        @pl.when(i == 0)
        def _():
            dw_ref[...] = jnp.zeros_like(dw_ref)
            dal_ref[...] = jnp.zeros_like(dal_ref)
            ddt_ref[...] = jnp.zeros_like(ddt_ref)

        zeros = jnp.zeros((HALO, width), F32)
        xext[pl.ds(0, HALO), :] = zeros
        xext[pl.ds(HALO, HALO), :] = jnp.where(first, xp_ref[...], 0.0)
        xext[pl.ds(2 * HALO, tile), :] = x_ref[...]
        xext[pl.ds(2 * HALO + tile, HALO), :] = jnp.where(last, xn_ref[...], 0.0)
        xext[pl.ds(3 * HALO + tile, HALO), :] = zeros
        c = None
        for j in range(CONV_W):
            term = xext[pl.ds(HALO - pad + j, ext_rows), :] * w_ref[j:j + 1, :]
            c = term if c is None else c + term
        for h in range(N_GROUPS):
            lanes = [pl.ds(part * D_DN + h * HEAD, HEAD) for part in range(3)]
            cts = []
            for cur, before, after in ((dq_ref, dqp_ref, dqn_ref), (dk_ref, dkp_ref, dkn_ref), (dv_ref, dvp_ref, dvn_ref)):
                cts.append(jnp.concatenate([jnp.where(first, before[h], 0.0), cur[h], jnp.where(last, after[h], 0.0)], axis=0))
            _, vjp = jax.vjp(_qkv_head, c[:, h * HEAD:(h + 1) * HEAD], c[:, D_DN + h * HEAD:D_DN + (h + 1) * HEAD],
                             c[:, 2 * D_DN + h * HEAD:2 * D_DN + (h + 1) * HEAD])
            for lane_sl, dc in zip(lanes, vjp(tuple(cts))):
                dcext[:, lane_sl] = dc
        dx, dc_blk = None, dcext[pl.ds(HALO, tile), :]
        for j in range(CONV_W):
            term = dcext[pl.ds(HALO + pad - j, tile), :] * w_ref[j:j + 1, :]
            dx = term if dx is None else dx + term
            dw_ref[j:j + 1, :] += jnp.sum(dc_blk * xext[pl.ds(2 * HALO - pad + j, tile), :], axis=0, keepdims=True)
        dp_ref[...] = dx.astype(BF16)
        _, gvjp = jax.vjp(_gate_fn, pg_ref[...], al_ref[...], dt_ref[...])
        dpg, dal, ddt = gvjp(dg0_ref[...] + dg1_ref[...] + dg2_ref[...])
        dpg_ref[...] = dpg.astype(BF16)
        dal_ref[...] += dal
        ddt_ref[...] += ddt

    xs = [pl.BlockSpec((tile, width), lambda i: (i, 0)), pl.BlockSpec((HALO, width), lambda i: (prev(i), 0)),
          pl.BlockSpec((HALO, width), lambda i: (nxt(i), 0))]
    hm = [pl.BlockSpec((N_GROUPS, tile, HEAD), lambda i: (0, i, 0)),
          pl.BlockSpec((N_GROUPS, HALO, HEAD), lambda i: (0, prev(i), 0)),
          pl.BlockSpec((N_GROUPS, HALO, HEAD), lambda i: (0, nxt(i), 0))]
    gate = pl.BlockSpec((tile, LANE), lambda i: (i, 0))
    lane_vec = pl.BlockSpec((1, LANE), lambda i: (0, 0))
    wspec = pl.BlockSpec((HALO, width), lambda i: (0, 0))
    return pl.pallas_call(
        body, name="prep_bwd", grid=(n_blk,),
        in_specs=xs + [wspec, gate, lane_vec, lane_vec] + hm * 3 + [gate] * 3,
        out_specs=[pl.BlockSpec((tile, width), lambda i: (i, 0)), gate, wspec, lane_vec, lane_vec],
        out_shape=[jax.ShapeDtypeStruct((s, width), BF16), jax.ShapeDtypeStruct((s, LANE), BF16),
                   jax.ShapeDtypeStruct((HALO, width), F32), jax.ShapeDtypeStruct((1, LANE), F32),
                   jax.ShapeDtypeStruct((1, LANE), F32)],
        scratch_shapes=[pltpu.VMEM((tile + 4 * HALO, width), F32), pltpu.VMEM((ext_rows, width), F32)],
        compiler_params=_params(("arbitrary",)),
    )(proj, proj, proj, conv_w, pg, a_log, dt_bias, dq, dq, dq, dk, dk, dk, dv, dv, dv, *dgates)


def _hdot(a, b):
    return jnp.dot(a, b, precision=HIGHEST, preferred_element_type=F32)


def _delta_chunk(q, k, v, g_col, g_row, beta, state, reverse):
    n = DN_CHUNK
    ri = lax.broadcasted_iota(jnp.int32, (n, n), 0)
    ci = lax.broadcasted_iota(jnp.int32, (n, n), 1)
    if reverse:
        incl, incl_t, strict = ri <= ci, ri >= ci, ri < ci
    else:
        incl, incl_t, strict = ri >= ci, ri <= ci, ri > ci
    gc_col = jnp.sum(jnp.where(incl, g_row, 0.0), axis=1, keepdims=True)
    gc_row = jnp.sum(jnp.where(incl_t, g_col, 0.0), axis=0, keepdims=True)
    g_tot = jnp.sum(g_row, axis=1, keepdims=True)
    decay = jnp.where(incl, jnp.exp(jnp.where(incl, gc_col - gc_row, 0.0)), 0.0)
    kb = k * beta
    m = jnp.where(strict, _mm_nt(kb, k) * decay, 0.0)
    p = -m
    t = (ri == ci).astype(F32) + p
    for _ in range(5):
        p = _hdot(p, p)
        t = t + _hdot(t, p)
    e_gc = jnp.exp(gc_col)
    u = _mm_nn(t, v * beta)
    w = _mm_nn(t, kb * e_gc)
    attn = _mm_nt(q, k) * decay
    v_new = u - _mm_nn(w, state)
    o = _mm_nn(q * e_gc, state) + _mm_nn(attn, v_new)
    new_state = state * jnp.exp(g_tot) + _mm_tn(k * jnp.exp(g_tot - gc_col), v_new)
    return o, new_state


def _lane_col(block, lane):
    ids = lax.broadcasted_iota(jnp.int32, block.shape, 1)
    return jnp.sum(jnp.where(ids == lane, block, 0.0), axis=1, keepdims=True)


def _delta_fwd(q, k, v, gates, g_rows, *, reverse):
    s = q.shape[1]
    nc = s // DN_CHUNK
    g_lane, b_lane = (N_GROUPS, 3 * N_GROUPS) if reverse else (0, 2 * N_GROUPS)
    chunk_of = (lambda c: nc - 1 - c) if reverse else (lambda c: c)

    def body(q_ref, k_ref, v_ref, gate_ref, grow_ref, o_ref, st_ref, state):
        c, h = pl.program_id(0), pl.program_id(1)

        @pl.when(c == 0)
        def _():
            state[h] = jnp.zeros((HEAD, HEAD), F32)

        gb = gate_ref[...]
        prev = state[h]
        st_ref[0, 0] = prev
        o, new = _delta_chunk(q_ref[0], k_ref[0], v_ref[0], _lane_col(gb, g_lane + h), grow_ref[0, pl.ds(h, 1), :],
                              _lane_col(gb, b_lane + h), prev, reverse)
        o_ref[0] = o
        state[h] = new

    hm = pl.BlockSpec((1, DN_CHUNK, HEAD), lambda c, h: (h, chunk_of(c), 0))
    return pl.pallas_call(
        body, name="delta_fwd_rev" if reverse else "delta_fwd", grid=(nc, N_GROUPS),
        in_specs=[hm, hm, hm, pl.BlockSpec((DN_CHUNK, LANE), lambda c, h: (chunk_of(c), 0)),
                  pl.BlockSpec((1, N_GROUPS, DN_CHUNK), lambda c, h: (chunk_of(c), 0, 0))],
        out_specs=[hm, pl.BlockSpec((1, 1, HEAD, HEAD), lambda c, h: (chunk_of(c), h, 0, 0))],
        out_shape=[jax.ShapeDtypeStruct((N_GROUPS, s, HEAD), F32), jax.ShapeDtypeStruct((nc, N_GROUPS, HEAD, HEAD), F32)],
        scratch_shapes=[pltpu.VMEM((N_GROUPS, HEAD, HEAD), F32)],
        compiler_params=_params(("arbitrary", "arbitrary")),
    )(q, k, v, gates, g_rows)


def _delta_bwd(q, k, v, gates, g_rows, states, do, acc, *, reverse):
    s = q.shape[1]
    nc = s // DN_CHUNK
    g_lane, b_lane = (N_GROUPS, 3 * N_GROUPS) if reverse else (0, 2 * N_GROUPS)
    chunk_of = (lambda c: c) if reverse else (lambda c: nc - 1 - c)
    n_acc = 0 if acc is None else 3

    def body(*refs):
        q_ref, k_ref, v_ref, gate_ref, grow_ref, st_ref, do_ref = refs[:7]
        acc_refs = refs[7:7 + n_acc]
        dq_ref, dk_ref, dv_ref, dgate_ref, dgrow_ref, dstate = refs[7 + n_acc:]
        c, h = pl.program_id(0), pl.program_id(1)

        @pl.when(c == 0)
        def _():
            dstate[h] = jnp.zeros((HEAD, HEAD), F32)

        gb = gate_ref[...]
        _, vjp = jax.vjp(functools.partial(_delta_chunk, reverse=reverse), q_ref[0], k_ref[0], v_ref[0],
                         _lane_col(gb, g_lane + h), grow_ref[0, pl.ds(h, 1), :], _lane_col(gb, b_lane + h), st_ref[0, 0])
        dq, dk, dv, dg_col, dg_row, dbeta, dprev = vjp((do_ref[0], dstate[h]))
        if n_acc:
            dq, dk, dv = dq + acc_refs[0][0], dk + acc_refs[1][0], dv + acc_refs[2][0]
        dq_ref[0] = dq
        dk_ref[0] = dk
        dv_ref[0] = dv
        dstate[h] = dprev
        ids = lax.broadcasted_iota(jnp.int32, (DN_CHUNK, LANE), 1)
        mine = jnp.where(ids == g_lane + h, dg_col, 0.0) + jnp.where(ids == b_lane + h, dbeta, 0.0)

        @pl.when(h == 0)
        def _():
            dgate_ref[...] = mine

        @pl.when(h != 0)
        def _():
            dgate_ref[...] += mine

        dgrow_ref[0, pl.ds(h, 1), :] = dg_row

    hm = pl.BlockSpec((1, DN_CHUNK, HEAD), lambda c, h: (h, chunk_of(c), 0))
    gate = pl.BlockSpec((DN_CHUNK, LANE), lambda c, h: (chunk_of(c), 0))
    rows = pl.BlockSpec((1, N_GROUPS, DN_CHUNK), lambda c, h: (chunk_of(c), 0, 0))
    st = pl.BlockSpec((1, 1, HEAD, HEAD), lambda c, h: (chunk_of(c), h, 0, 0))
    hm_shape = jax.ShapeDtypeStruct((N_GROUPS, s, HEAD), F32)
    return pl.pallas_call(
        body, name="delta_bwd_rev" if reverse else "delta_bwd", grid=(nc, N_GROUPS),
        in_specs=[hm, hm, hm, gate, rows, st, hm] + [hm] * n_acc,
        out_specs=[hm, hm, hm, gate, rows],
        out_shape=[hm_shape] * 3 + [jax.ShapeDtypeStruct((s, LANE), F32), jax.ShapeDtypeStruct((nc, N_GROUPS, DN_CHUNK), F32)],
        scratch_shapes=[pltpu.VMEM((N_GROUPS, HEAD, HEAD), F32)],
        compiler_params=_params(("arbitrary", "arbitrary")),
    )(q, k, v, gates, g_rows, states, do, *(acc or ()))


def _post_head(o_f, o_b, z, w):
    return _rms(o_f + o_b, w) * _silu(z)


def _post_fwd(o_f, o_b, z, w, *, tile=256):
    s = z.shape[0]
    tile = min(tile, s)

    def body(of_ref, ob_ref, z_ref, w_ref, y_ref):
        for h in range(N_GROUPS):
            lanes = pl.ds(h * HEAD, HEAD)
            y_ref[:, lanes] = _post_head(of_ref[h], ob_ref[h], z_ref[:, lanes], w_ref[...]).astype(BF16)

    hm = pl.BlockSpec((N_GROUPS, tile, HEAD), lambda i: (0, i, 0))
    row = pl.BlockSpec((tile, D_DN), lambda i: (i, 0))
    return pl.pallas_call(
        body, name="post_fwd", grid=(s // tile,),
        in_specs=[hm, hm, row, pl.BlockSpec((1, HEAD), lambda i: (0, 0))], out_specs=row,
        out_shape=jax.ShapeDtypeStruct((s, D_DN), BF16),
        compiler_params=_params(("arbitrary",)),
    )(o_f, o_b, z, w)


def _post_bwd(o_f, o_b, z, w, dy, *, tile=256):
    s = z.shape[0]
    tile = min(tile, s)

    def body(of_ref, ob_ref, z_ref, w_ref, dy_ref, do_ref, dz_ref, dw_ref):
        @pl.when(pl.program_id(0) == 0)
        def _():
            dw_ref[...] = jnp.zeros_like(dw_ref)

        for h in range(N_GROUPS):
            lanes = pl.ds(h * HEAD, HEAD)
            _, vjp = jax.vjp(_post_head, of_ref[h], ob_ref[h], z_ref[:, lanes], w_ref[...])
            do, _, dz, dw = vjp(dy_ref[:, lanes])
            do_ref[h] = do
            dz_ref[:, lanes] = dz.astype(BF16)
            dw_ref[...] += dw

    hm = pl.BlockSpec((N_GROUPS, tile, HEAD), lambda i: (0, i, 0))
    row = pl.BlockSpec((tile, D_DN), lambda i: (i, 0))
    vec = pl.BlockSpec((1, HEAD), lambda i: (0, 0))
    return pl.pallas_call(
        body, name="post_bwd", grid=(s // tile,),
        in_specs=[hm, hm, row, vec, pl.BlockSpec((tile, D_DN), lambda i: (i, 1))], out_specs=[hm, row, vec],
        out_shape=[jax.ShapeDtypeStruct((N_GROUPS, s, HEAD), F32), jax.ShapeDtypeStruct((s, D_DN), BF16),
                   jax.ShapeDtypeStruct((1, HEAD), F32)],
        compiler_params=_params(("arbitrary",)),
    )(o_f, o_b, z, w, dy)


def _my_index():
    return 4 * lax.axis_index("x") + 2 * lax.axis_index("y") + lax.axis_index("c")


def _peer(k):
    x, y, c = lax.axis_index("x"), lax.axis_index("y"), lax.axis_index("c")
    px, py, pc = x ^ ((k >> 2) & 1), y ^ ((k >> 1) & 1), c ^ (k & 1)
    return (px, py, pc), 4 * px + 2 * py + pc


def _exchange(name, arrays, scatter):
    n_arr = len(arrays)

    def body(*refs):
        in_refs, out_refs = refs[:n_arr], refs[n_arr:2 * n_arr]
        send_sems, recv_sems, local_sems = refs[2 * n_arr:]
        me = _my_index()
        local = []
        for a in range(n_arr):
            src = in_refs[a].at[me] if scatter else in_refs[a]
            local.append(pltpu.make_async_copy(src, out_refs[a].at[me], local_sems.at[a]))
            local[-1].start()
        sends = []
        for k in range(1, N_DEV):
            peer, peer_idx = _peer(k)
            for a in range(n_arr):
                src = in_refs[a].at[peer_idx] if scatter else in_refs[a]
                sends.append(pltpu.make_async_remote_copy(
                    src_ref=src, dst_ref=out_refs[a].at[me], send_sem=send_sems.at[a, k - 1], recv_sem=recv_sems.at[a, k - 1],
                    device_id=peer, device_id_type=pl.DeviceIdType.MESH))
                sends[-1].start()
        for k in range(1, N_DEV):
            peer, peer_idx = _peer(k)
            for a in range(n_arr):
                src = in_refs[a].at[me] if scatter else in_refs[a]
                pltpu.make_async_remote_copy(
                    src_ref=src, dst_ref=out_refs[a].at[peer_idx], send_sem=send_sems.at[a, k - 1], recv_sem=recv_sems.at[a, k - 1],
                    device_id=peer, device_id_type=pl.DeviceIdType.MESH).wait_recv()
        for cp in sends:
            cp.wait_send()
        for cp in local:
            cp.wait()

    any_spec = pl.BlockSpec(memory_space=pl.ANY)
    out_shape = [jax.ShapeDtypeStruct(a.shape if scatter else (N_DEV,) + a.shape, a.dtype) for a in arrays]
    return pl.pallas_call(
        body, name=name, in_specs=[any_spec] * n_arr, out_specs=[any_spec] * n_arr, out_shape=out_shape,
        scratch_shapes=[pltpu.SemaphoreType.DMA((n_arr, N_DEV - 1)), pltpu.SemaphoreType.DMA((n_arr, N_DEV - 1)),
                        pltpu.SemaphoreType.DMA((n_arr,))],
        compiler_params=pltpu.CompilerParams(has_side_effects=True),
    )(*arrays)


def _adamw(parts, w, m, v, *, tile=256):
    rows = w.shape[0]
    tile = min(tile, rows)
    assert rows % tile == 0

    def body(p_ref, w_ref, m_ref, v_ref, g_ref, d_ref, mo_ref, vo_ref):
        g = p_ref[0]
        for j in range(1, N_DEV):
            g = g + p_ref[j]
        m_new = ADAM_B1 * m_ref[...] + (1.0 - ADAM_B1) * g
        v_new = ADAM_B2 * v_ref[...] + (1.0 - ADAM_B2) * jnp.square(g)
        m_hat = m_new / np.float32(1.0 - ADAM_B1 ** ADAM_STEP)
        v_hat = v_new / np.float32(1.0 - ADAM_B2 ** ADAM_STEP)
        g_ref[...] = g
        d_ref[...] = -ADAM_LR * (m_hat / (jnp.sqrt(v_hat) + ADAM_EPS) + ADAM_WD * w_ref[...])
        mo_ref[...] = m_new
        vo_ref[...] = v_new

    row = pl.BlockSpec((tile, FLAT_W), lambda i: (i, 0))
    return pl.pallas_call(
        body, name="adamw", grid=(rows // tile,),
        in_specs=[pl.BlockSpec((N_DEV, tile, FLAT_W), lambda i: (0, i, 0)), row, row, row], out_specs=[row] * 4,
        out_shape=[jax.ShapeDtypeStruct((rows, FLAT_W), F32)] * 4,
        compiler_params=_params(("arbitrary",), 2 * (N_DEV + 7) * tile * FLAT_W * 4 + VMEM_MARGIN),
    )(parts, w, m, v)


def _flat_rows(n_elems, row_multiple):
    rows = -(-n_elems // FLAT_W)
    return -(-rows // row_multiple) * row_multiple


def _to_flat(pieces, row_multiple=256):
    flat = jnp.concatenate([p.reshape(-1) for p in pieces])
    rows = _flat_rows(flat.shape[0], row_multiple)
    return jnp.pad(flat, (0, rows * FLAT_W - flat.shape[0])).reshape(rows, FLAT_W)


def _lane_row(*pieces):
    flat = jnp.concatenate([p.reshape(-1) for p in pieces])
    return jnp.pad(flat, (0, LANE - flat.shape[0])).reshape(1, LANE)


def kernel(x, norm_w, w_in, sgu_ln_g, sgu_ln_b, sgu_w, sgu_b, conv_w, a_log_f, a_log_b, dt_bias_f, dt_bias_b, dn_norm_w, w_out, final_norm_w, loss_target, m_norm_w, m_w_in, m_sgu_ln_g, m_sgu_ln_b, m_sgu_w, m_sgu_b, m_conv_w, m_a_log_f, m_a_log_b, m_dt_bias_f, m_dt_bias_b, m_dn_norm_w, m_w_out, m_final_norm_w, v_norm_w, v_w_in, v_sgu_ln_g, v_sgu_ln_b, v_sgu_w, v_sgu_b, v_conv_w, v_a_log_f, v_a_log_b, v_dt_bias_f, v_dt_bias_b, v_dn_norm_w, v_w_out, v_final_norm_w):
    weights = dict(norm_w=norm_w, w_in=w_in, sgu_ln_g=sgu_ln_g, sgu_ln_b=sgu_ln_b, sgu_w=sgu_w, sgu_b=sgu_b, conv_w=conv_w,
                   a_log_f=a_log_f, a_log_b=a_log_b, dt_bias_f=dt_bias_f, dt_bias_b=dt_bias_b, dn_norm_w=dn_norm_w,
                   w_out=w_out, final_norm_w=final_norm_w)
    m_in = dict(norm_w=m_norm_w, w_in=m_w_in, sgu_ln_g=m_sgu_ln_g, sgu_ln_b=m_sgu_ln_b, sgu_w=m_sgu_w, sgu_b=m_sgu_b,
                conv_w=m_conv_w, a_log_f=m_a_log_f, a_log_b=m_a_log_b, dt_bias_f=m_dt_bias_f, dt_bias_b=m_dt_bias_b,
                dn_norm_w=m_dn_norm_w, w_out=m_w_out, final_norm_w=m_final_norm_w)
    v_in = dict(norm_w=v_norm_w, w_in=v_w_in, sgu_ln_g=v_sgu_ln_g, sgu_ln_b=v_sgu_ln_b, sgu_w=v_sgu_w, sgu_b=v_sgu_b,
                conv_w=v_conv_w, a_log_f=v_a_log_f, a_log_b=v_a_log_b, dt_bias_f=v_dt_bias_f, dt_bias_b=v_dt_bias_b,
                dn_norm_w=v_dn_norm_w, w_out=v_w_out, final_norm_w=v_final_norm_w)
    s = x.shape[1]
    nc = s // DN_CHUNK
    depth = w_in.shape[0]
    in_shard, out_shard, conv_shard = w_in.shape[2], w_out.shape[1], conv_w.shape[2]
    x0 = x.reshape(s, D_MODEL)
    target = loss_target.reshape(s, D_MODEL)

    big = _to_flat([w_in.astype(BF16), w_out.astype(BF16)], row_multiple=16)
    small = _to_flat([conv_w], row_multiple=8)
    big_all, small_all = _exchange("gather_weights", [big, small], scatter=False)
    big_all = big_all.reshape(N_DEV, -1)
    n_in = depth * D_MODEL * in_shard
    w_in_all = big_all[:, :n_in].reshape(N_DEV, depth, D_MODEL, in_shard).transpose(1, 2, 0, 3).reshape(depth, D_MODEL, D_IN)
    w_out_all = big_all[:, n_in:n_in + depth * out_shard * D_MODEL].reshape(N_DEV, depth, out_shard, D_MODEL)
    w_out_all = w_out_all.transpose(1, 0, 2, 3).reshape(depth, D_MODEL, D_MODEL)
    conv_all = small_all.reshape(N_DEV, -1)[:, :depth * CONV_W * conv_shard].reshape(N_DEV, depth, CONV_W, conv_shard)
    conv_all = conv_all.transpose(1, 2, 0, 3).reshape(depth, CONV_W, 3 * D_DN)
    conv_all = jnp.pad(conv_all, ((0, 0), (0, HALO - CONV_W), (0, 0)))
    n_main = D_IN - N_GATE
    w_sgu, w_qkv, w_zb = w_in_all[:, :, :3 * D_SGU], w_in_all[:, :, 3 * D_SGU:3 * D_SGU + 3 * D_DN], w_in_all[:, :, 3 * D_SGU + 3 * D_DN:n_main]
    w_gate = jnp.pad(w_in_all[:, :, n_main:], ((0, 0), (0, 0), (0, LANE - N_GATE)))

    sgu_b_t = jnp.swapaxes(sgu_b, 1, 2)

    def rows_of(gates, first_lane):
        return gates[:, first_lane:first_lane + N_GROUPS].reshape(nc, DN_CHUNK, N_GROUPS).transpose(0, 2, 1)

    saved = []
    xl = x0
    for l in range(depth):
        nw = norm_w[l].reshape(1, D_MODEL)
        h = _rms_fwd(xl, nw)
        p_sgu = _matmul([(h, w_sgu[l])], name="in_proj_sgu")
        p_qkv = _matmul([(h, w_qkv[l])], name="in_proj_qkv")
        p_zb = _matmul([(h, w_zb[l])], name="in_proj_zb")
        p_gate = _matmul([(h, w_gate[l])], name="in_proj_gate")
        ln_g, ln_b = sgu_ln_g[l].reshape(1, D_SGU), sgu_ln_b[l].reshape(1, D_SGU)
        y_a = _sgu_fwd(p_sgu, ln_g, ln_b, sgu_w[l], sgu_b_t[l])
        a_log = _lane_row(a_log_f[l], a_log_b[l])
        dt_bias = _lane_row(dt_bias_f[l], dt_bias_b[l])
        q, k, v, gates = _prep_fwd(p_qkv, conv_all[l], p_gate, a_log, dt_bias)
        rows_f, rows_b = rows_of(gates, 0), rows_of(gates, N_GROUPS)
        o_f, st_f = _delta_fwd(q, k, v, gates, rows_f, reverse=False)
        o_b, st_b = _delta_fwd(q, k, v, gates, rows_b, reverse=True)
        dn_w = dn_norm_w[l].reshape(1, HEAD)
        y_b = _post_fwd(o_f, o_b, p_zb, dn_w)
        x_next = _matmul([(y_a, w_out_all[l, :D_SGU]), (y_b, w_out_all[l, D_SGU:])], name="out_proj", add=xl)
        saved.append(dict(x=xl, nw=nw, h=h, p_sgu=p_sgu, p_qkv=p_qkv, p_zb=p_zb, p_gate=p_gate, ln_g=ln_g, ln_b=ln_b,
                          a_log=a_log, dt_bias=dt_bias, q=q, k=k, v=v, gates=gates, rows_f=rows_f, rows_b=rows_b,
                          o_f=o_f, o_b=o_b, st_f=st_f, st_b=st_b, dn_w=dn_w, y_a=y_a, y_b=y_b))
        xl = x_next

    loss_part, dx, dx_bf, d_final = _final_loss(xl, final_norm_w.reshape(1, D_MODEL), target)
    loss = lax.psum(loss_part[0, 0], MESH_AXES)

    grads = {n: [None] * depth for n in WEIGHTS if n != "final_norm_w"}
    for l in reversed(range(depth)):
        sv = saved[l]
        dy = _matmul([(dx_bf, w_out_all[l])], name="out_proj_dy", trans_b=True)
        g_out_a = _matmul([(sv["y_a"].T, dx_bf)], name="out_proj_dw")
        g_out_b = _matmul([(sv["y_b"].T, dx_bf)], name="out_proj_dw")
        grads["w_out"][l] = jnp.concatenate([g_out_a, g_out_b], axis=0)
        do, d_zb, d_dn = _post_bwd(sv["o_f"], sv["o_b"], sv["p_zb"], sv["dn_w"], dy)
        dq, dk, dv, dgate_f, drows_f = _delta_bwd(sv["q"], sv["k"], sv["v"], sv["gates"], sv["rows_f"], sv["st_f"], do, None, reverse=False)
        dq, dk, dv, dgate_b, drows_b = _delta_bwd(sv["q"], sv["k"], sv["v"], sv["gates"], sv["rows_b"], sv["st_b"], do, (dq, dk, dv), reverse=True)
        drows = jnp.concatenate([drows_f.transpose(0, 2, 1).reshape(s, N_GROUPS), drows_b.transpose(0, 2, 1).reshape(s, N_GROUPS)], axis=1)
        drows = jnp.pad(drows, ((0, 0), (0, LANE - 2 * N_GROUPS)))
        d_qkv, d_gate, d_conv, d_alog, d_dt = _prep_bwd(sv["p_qkv"], conv_all[l], sv["p_gate"], sv["a_log"], sv["dt_bias"],
                                                        dq, dk, dv, (dgate_f, dgate_b, drows))
        d_sgu, d_lg, d_lb, d_sw, d_sbt = _sgu_bwd(sv["p_sgu"], sv["ln_g"], sv["ln_b"], sgu_w[l], sgu_b_t[l], dy)
        dh = _matmul([(d_sgu, w_sgu[l]), (d_qkv, w_qkv[l]), (d_zb, w_zb[l]), (d_gate, w_gate[l])], name="in_proj_dh", trans_b=True)
        h_t = sv["h"].T
        g_in = [_matmul([(h_t, d_sgu)], name="in_proj_dw_sgu"), _matmul([(h_t, d_qkv)], name="in_proj_dw_qkv"),
                _matmul([(h_t, d_zb)], name="in_proj_dw_zb"), _matmul([(h_t, d_gate)], name="in_proj_dw_gate")[:, :N_GATE]]
        grads["w_in"][l] = jnp.concatenate(g_in, axis=1)
        dx, dx_bf, d_nw = _rms_bwd(sv["x"], sv["nw"], dh, dx)
        grads["norm_w"][l] = d_nw.reshape(D_MODEL)
        grads["sgu_ln_g"][l], grads["sgu_ln_b"][l] = d_lg.reshape(D_SGU), d_lb.reshape(D_SGU)
        grads["sgu_w"][l], grads["sgu_b"][l] = d_sw, d_sbt.T
        grads["conv_w"][l] = d_conv[:CONV_W]
        grads["a_log_f"][l], grads["a_log_b"][l] = d_alog[0, :N_GROUPS], d_alog[0, N_GROUPS:2 * N_GROUPS]
        grads["dt_bias_f"][l], grads["dt_bias_b"][l] = d_dt[0, :N_GROUPS], d_dt[0, N_GROUPS:2 * N_GROUPS]
        grads["dn_norm_w"][l] = d_dn.reshape(HEAD)
    grad_x = dx.reshape(1, s, D_MODEL)
    g_full = {n: jnp.stack(g) for n, g in grads.items()}
    g_full["final_norm_w"] = d_final.reshape(D_MODEL)

    by_dev = jnp.concatenate([
        g_full["w_in"].reshape(depth, D_MODEL, N_DEV, in_shard).transpose(2, 0, 1, 3).reshape(N_DEV, -1),
        g_full["w_out"].reshape(depth, N_DEV, out_shard, D_MODEL).transpose(1, 0, 2, 3).reshape(N_DEV, -1),
        g_full["conv_w"].reshape(depth, CONV_W, N_DEV, conv_shard).transpose(2, 0, 1, 3).reshape(N_DEV, -1)], axis=1)
    replicated = jnp.concatenate([g_full[n].reshape(-1) for n in FLAT_ORDER if n not in SHARDED])
    n_flat = by_dev.shape[1] + replicated.shape[0]
    rows = _flat_rows(n_flat, 256)
    send = jnp.concatenate([by_dev, jnp.broadcast_to(replicated, (N_DEV, replicated.shape[0])),
                            jnp.zeros((N_DEV, rows * FLAT_W - n_flat), F32)], axis=1).reshape(N_DEV, rows, FLAT_W)
    (parts,) = _exchange("exchange_grads", [send], scatter=True)

    w_flat = _to_flat([weights[n] for n in FLAT_ORDER])
    m_flat = _to_flat([m_in[n] for n in FLAT_ORDER])
    v_flat = _to_flat([v_in[n] for n in FLAT_ORDER])
    outs = _adamw(parts, w_flat, m_flat, v_flat)
    unpacked = [dict() for _ in outs]
    off = 0
    for n in FLAT_ORDER:
        size = int(np.prod(weights[n].shape))
        for o, d in zip(outs, unpacked):
            d[n] = o.reshape(-1)[off:off + size].reshape(weights[n].shape)
        off += size
    g_out, delta_out, m_out, v_out = unpacked
    return (loss, grad_x, *[g_out[n] for n in WEIGHTS], *[delta_out[n] for n in WEIGHTS],
            *[m_out[n] for n in WEIGHTS], *[v_out[n] for n in WEIGHTS])
```

```python
import numpy as np
import jax
import jax.numpy as jnp
from jax import lax
from jax.experimental import pallas as pl
from jax.experimental.pallas import tpu as pltpu

F32 = jnp.float32
BF16 = jnp.bfloat16

N_DEV = 8
D_MODEL = 2048
D_SGU = 1024
D_DN = 1024
N_GROUPS = 8
HEAD = 128
SGU_CHUNK = 128
DN_CHUNK = 64
CONV_W = 5
N_GATE = 32
D_MAIN = 3 * D_SGU + 4 * D_DN
D_IN = D_MAIN + N_GATE
LANE = 128
SUBLANES = 16
HALO = 8
EPS = 1e-6
ADAM_LR, ADAM_B1, ADAM_B2, ADAM_EPS, ADAM_WD, ADAM_STEP = 0.001, 0.9, 0.999, 1e-08, 0.01, 10
FLAT_W = 1024
MM_TILE = 512
VMEM_MARGIN = 8 << 20

MESH_AXES = ("x", "y", "c")
WEIGHTS = ("norm_w", "w_in", "sgu_ln_g", "sgu_ln_b", "sgu_w", "sgu_b", "conv_w", "a_log_f", "a_log_b",
           "dt_bias_f", "dt_bias_b", "dn_norm_w", "w_out", "final_norm_w")
SMALL = ("conv_w",) + tuple(n for n in WEIGHTS if n not in ("w_in", "w_out", "conv_w"))


def _bdot(a, b, dims):
    return lax.dot_general(a.astype(BF16), b.astype(BF16), (dims, ((), ())), preferred_element_type=F32)


@jax.custom_vjp
def _mm_nn(a, b):
    return _bdot(a, b, ((1,), (0,)))


def _mm_nn_fwd(a, b):
    return _mm_nn(a, b), (a, b)


def _mm_nn_bwd(res, ct):
    a, b = res
    return _bdot(ct, b, ((1,), (1,))), _bdot(a, ct, ((0,), (0,)))


_mm_nn.defvjp(_mm_nn_fwd, _mm_nn_bwd)


@jax.custom_vjp
def _mm_nt(a, b):
    return _bdot(a, b, ((1,), (1,)))


def _mm_nt_fwd(a, b):
    return _mm_nt(a, b), (a, b)


def _mm_nt_bwd(res, ct):
    a, b = res
    return _bdot(ct, b, ((1,), (0,))), _bdot(ct, a, ((0,), (0,)))


_mm_nt.defvjp(_mm_nt_fwd, _mm_nt_bwd)


@jax.custom_vjp
def _mm_tn(a, b):
    return _bdot(a, b, ((0,), (0,)))


def _mm_tn_fwd(a, b):
    return _mm_tn(a, b), (a, b)


def _mm_tn_bwd(res, ct):
    a, b = res
    return _bdot(b, ct, ((1,), (1,))), _bdot(a, ct, ((1,), (0,)))


_mm_tn.defvjp(_mm_tn_fwd, _mm_tn_bwd)


def _split_hi_lo(a):
    hi = a.astype(BF16)
    return hi, (a - hi.astype(F32)).astype(BF16)


def _dot3(a, b, dims=((1,), (0,))):
    (ah, al), (bh, bl) = _split_hi_lo(a), _split_hi_lo(b)
    d = lambda x, y: lax.dot_general(x, y, (dims, ((), ())), preferred_element_type=F32)
    return d(ah, bh) + (d(ah, bl) + d(al, bh))


@jax.custom_vjp
def _inverse_given(m, t):
    return t


def _inverse_given_fwd(m, t):
    return t, t


def _inverse_given_bwd(t, ct):
    dm = -_dot3(t, _dot3(ct, t, ((1,), (1,))), ((0,), (0,)))
    return dm, jnp.zeros_like(t)


_inverse_given.defvjp(_inverse_given_fwd, _inverse_given_bwd)


@jax.custom_vjp
def _split_lanes(x):
    return tuple(x[:, i * LANE:(i + 1) * LANE] for i in range(x.shape[1] // LANE))


def _split_lanes_fwd(x):
    return _split_lanes(x), None


def _split_lanes_bwd(_, cts):
    return (jnp.concatenate(cts, axis=1),)


_split_lanes.defvjp(_split_lanes_fwd, _split_lanes_bwd)


def _silu(t):
    return t * jax.nn.sigmoid(t)


def _gelu(t):
    return 0.5 * t * (1.0 + lax.erf(t * np.float32(0.7071067811865476)))


def _rms(x, w):
    return x * lax.rsqrt(jnp.mean(x * x, axis=-1, keepdims=True) + EPS) * w


def _params(sem, vmem_bytes=None):
    kw = dict(dimension_semantics=sem)
    if vmem_bytes is not None:
        kw["vmem_limit_bytes"] = int(vmem_bytes)
    return pltpu.CompilerParams(**kw)


def _lhs(a, k=None, k_block=0):
    k = a.shape[1] if k is None else k
    tm = min(MM_TILE, a.shape[0])
    return a, pl.BlockSpec((tm, k), lambda i, j: (i, k_block)), tm * k * a.dtype.itemsize


def _rhs(b, tn, *, layer=None, transposed=False, k=None, k_block=0, n_offset=0):
    lead = () if layer is None else (layer,)
    none = () if layer is None else (None,)
    if transposed:
        k = b.shape[-1]
        spec = pl.BlockSpec(none + (tn, k), lambda i, j: lead + (j + n_offset, 0))
    else:
        k = b.shape[-2] if k is None else k
        spec = pl.BlockSpec(none + (k, tn), lambda i, j: lead + (k_block, j))
    return b, spec, tn * k * b.dtype.itemsize


def _matmul(pairs, m, n, *, name, trans_b=False, add=None, out_dtype=F32, tn=MM_TILE, out_t_rows=None, transpose_into=None,
            out_row_block=0):
    tm = min(MM_TILE, m)
    assert m % tm == 0 and n % tn == 0
    n_pairs = len(pairs)
    contract = ((1,), (1,)) if trans_b else ((1,), (0,))
    transposed_out = out_t_rows is not None or transpose_into is not None

    def body(*refs):
        o_ref = refs[-1]
        acc = None
        for i in range(n_pairs):
            d = lax.dot_general(refs[2 * i][...].astype(BF16), refs[2 * i + 1][...].astype(BF16),
                                (contract, ((), ())), preferred_element_type=F32)
            acc = d if acc is None else acc + d
        if add is not None:
            acc = acc + refs[2 * n_pairs][...]
        o_ref[...] = (acc.T if transposed_out else acc).astype(out_dtype)

    in_specs, args, vmem, aliases = [], [], 0, {}
    for (a, a_spec, a_bytes), (b, b_spec, b_bytes) in pairs:
        in_specs += [a_spec, b_spec]
        args += [a, b]
        vmem += 2 * (a_bytes + b_bytes)
    if add is not None:
        in_specs.append(pl.BlockSpec((tm, tn), lambda i, j: (i, j)))
        args.append(add)
        vmem += 2 * tm * tn * 4
    vmem += 2 * tm * tn * jnp.dtype(out_dtype).itemsize + 3 * tm * tn * 4
    if transposed_out:
        out_spec = pl.BlockSpec((tn, tm), lambda i, j: (j + out_row_block, i))
        if transpose_into is not None:
            in_specs.append(pl.BlockSpec(memory_space=pl.ANY))
            args.append(transpose_into)
            aliases = {len(args) - 1: 0}
            out_shape = jax.ShapeDtypeStruct(transpose_into.shape, out_dtype)
        else:
            out_shape = jax.ShapeDtypeStruct((out_t_rows, m), out_dtype)
    else:
        out_spec = pl.BlockSpec((tm, tn), lambda i, j: (i, j))
        out_shape = jax.ShapeDtypeStruct((m, n), out_dtype)
    return pl.pallas_call(
        body, name=name, grid=(m // tm, n // tn), in_specs=in_specs, out_specs=out_spec, out_shape=out_shape,
        input_output_aliases=aliases, compiler_params=_params(("parallel", "arbitrary"), vmem + VMEM_MARGIN),
    )(*args)


def _rms_fwd(x, w, *, tile=512):
    s, d = x.shape
    tile = min(tile, s)

    def body(x_ref, w_ref, h_ref):
        h_ref[...] = _rms(x_ref[...], w_ref[...]).astype(BF16)

    return pl.pallas_call(
        body, name="rms_fwd", grid=(s // tile,),
        in_specs=[pl.BlockSpec((tile, d), lambda i: (i, 0)), pl.BlockSpec((1, d), lambda i: (0, 0))],
        out_specs=pl.BlockSpec((tile, d), lambda i: (i, 0)),
        out_shape=jax.ShapeDtypeStruct((s, d), BF16),
        compiler_params=_params(("arbitrary",)),
    )(x, w)


def _rms_bwd(x, w, dh, dx_out, *, tile=256):
    s, d = x.shape
    tile = min(tile, s)

    def body(x_ref, w_ref, dh_ref, dxo_ref, dx_ref, dxb_ref, dw_ref):
        _, vjp = jax.vjp(_rms, x_ref[...], w_ref[...])
        dxv, dwv = vjp(dh_ref[...])
        dx = dxo_ref[...] + dxv
        dx_ref[...] = dx
        dxb_ref[...] = dx.astype(BF16)

        @pl.when(pl.program_id(0) == 0)
        def _():
            dw_ref[...] = jnp.zeros_like(dw_ref)

        dw_ref[...] += dwv

    row = pl.BlockSpec((tile, d), lambda i: (i, 0))
    vec = pl.BlockSpec((1, d), lambda i: (0, 0))
    return pl.pallas_call(
        body, name="rms_bwd", grid=(s // tile,),
        in_specs=[row, vec, row, row], out_specs=[row, row, vec],
        out_shape=[jax.ShapeDtypeStruct((s, d), F32), jax.ShapeDtypeStruct((s, d), BF16), jax.ShapeDtypeStruct((1, d), F32)],
        compiler_params=_params(("arbitrary",)),
    )(x, w, dh, dx_out)


def _final_loss(x, w, target, *, tile=256):
    s, d = x.shape
    tile = min(tile, s)

    def body(x_ref, w_ref, t_ref, loss_ref, dx_ref, dxb_ref, dw_ref):
        def f(xv, wv):
            err = jnp.square(_rms(xv, wv) - t_ref[...])
            per_token = jnp.sum(err, axis=1, keepdims=True) * np.float32(1.0 / d)
            return 0.5 * jnp.sum(per_token, axis=0, keepdims=True)

        loss, vjp = jax.vjp(f, x_ref[...], w_ref[...])
        dxv, dwv = vjp(jnp.ones((1, 1), F32))
        dx_ref[...] = dxv
        dxb_ref[...] = dxv.astype(BF16)

        @pl.when(pl.program_id(0) == 0)
        def _():
            dw_ref[...] = jnp.zeros_like(dw_ref)
            loss_ref[...] = jnp.zeros_like(loss_ref)

        dw_ref[...] += dwv
        loss_ref[...] += jnp.broadcast_to(loss, (1, LANE))

    row = pl.BlockSpec((tile, d), lambda i: (i, 0))
    vec = pl.BlockSpec((1, d), lambda i: (0, 0))
    return pl.pallas_call(
        body, name="final_loss", grid=(s // tile,),
        in_specs=[row, vec, row], out_specs=[pl.BlockSpec((1, LANE), lambda i: (0, 0)), row, row, vec],
        out_shape=[jax.ShapeDtypeStruct((1, LANE), F32), jax.ShapeDtypeStruct((s, d), F32),
                   jax.ShapeDtypeStruct((s, d), BF16), jax.ShapeDtypeStruct((1, d), F32)],
        compiler_params=_params(("arbitrary",)),
    )(x, w, target)


def _sgu_chunk(u, v, z, ln_g, ln_b, ws, bcols):
    vg = _gelu(v)
    xc = vg - jnp.mean(vg, axis=-1, keepdims=True)
    vl = xc * lax.rsqrt(jnp.mean(xc * xc, axis=-1, keepdims=True) + EPS) * ln_g + ln_b
    sp = jnp.concatenate([_mm_nn(ws[g], vg_) + bcols[g] for g, vg_ in enumerate(_split_lanes(vl))], axis=1)
    return _gelu(u) * sp * _silu(z)


def _sgu_specs(tile):
    col = lambda j: pl.BlockSpec((tile, D_SGU), lambda i, j=j: (i, j))
    vec = pl.BlockSpec((1, D_SGU), lambda i: (0, 0))
    wspec = pl.BlockSpec((N_GROUPS, SGU_CHUNK, SGU_CHUNK), lambda i: (0, 0, 0))
    bspec = pl.BlockSpec((SGU_CHUNK, N_GROUPS), lambda i: (0, 0))
    return col, vec, wspec, bspec


def _sgu_fwd(proj, ln_g, ln_b, w_s, b_t, *, tile=256):
    s = proj.shape[0]
    tile = min(tile, s)
    col, vec, wspec, bspec = _sgu_specs(tile)

    def body(u_ref, v_ref, z_ref, g_ref, b_ref, w_ref, bt_ref, y_ref):
        ws = tuple(w_ref[g] for g in range(N_GROUPS))
        bcols = tuple(bt_ref[:, g:g + 1] for g in range(N_GROUPS))
        for c in range(tile // SGU_CHUNK):
            rows = pl.ds(c * SGU_CHUNK, SGU_CHUNK)
            y = _sgu_chunk(u_ref[rows, :], v_ref[rows, :], z_ref[rows, :], g_ref[...], b_ref[...], ws, bcols)
            y_ref[rows, :] = y.astype(BF16)

    return pl.pallas_call(
        body, name="sgu_fwd", grid=(s // tile,),
        in_specs=[col(0), col(1), col(2), vec, vec, wspec, bspec],
        out_specs=pl.BlockSpec((tile, D_SGU), lambda i: (i, 0)),
        out_shape=jax.ShapeDtypeStruct((s, D_SGU + D_DN), BF16),
        compiler_params=_params(("arbitrary",)),
    )(proj, proj, proj, ln_g, ln_b, w_s, b_t)


def _sgu_bwd(proj, ln_g, ln_b, w_s, b_t, dy, dp, *, tile=128):
    s = proj.shape[0]
    tile = min(tile, s)
    col, vec, wspec, bspec = _sgu_specs(tile)

    def body(u_ref, v_ref, z_ref, g_ref, b_ref, w_ref, bt_ref, dy_ref, _, dp_ref, dg_ref, db_ref, dw_ref, dbt_ref):
        @pl.when(pl.program_id(0) == 0)
        def _():
            dg_ref[...] = jnp.zeros_like(dg_ref)
            db_ref[...] = jnp.zeros_like(db_ref)
            dw_ref[...] = jnp.zeros_like(dw_ref)
            dbt_ref[...] = jnp.zeros_like(dbt_ref)

        ws = tuple(w_ref[g] for g in range(N_GROUPS))
        bcols = tuple(bt_ref[:, g:g + 1] for g in range(N_GROUPS))
        for c in range(tile // SGU_CHUNK):
            rows = pl.ds(c * SGU_CHUNK, SGU_CHUNK)
            _, vjp = jax.vjp(_sgu_chunk, u_ref[rows, :], v_ref[rows, :], z_ref[rows, :], g_ref[...], b_ref[...], ws, bcols)
            du, dv, dz, dg, db, dws, dbc = vjp(dy_ref[rows, :])
            dp_ref[rows, pl.ds(0, D_SGU)] = du.astype(BF16)
            dp_ref[rows, pl.ds(D_SGU, D_SGU)] = dv.astype(BF16)
            dp_ref[rows, pl.ds(2 * D_SGU, D_SGU)] = dz.astype(BF16)
            dg_ref[...] += dg
            db_ref[...] += db
            for g in range(N_GROUPS):
                dw_ref[g] += dws[g]
                dbt_ref[:, g:g + 1] += dbc[g]

    return pl.pallas_call(
        body, name="sgu_bwd", grid=(s // tile,),
        in_specs=[col(0), col(1), col(2), vec, vec, wspec, bspec, pl.BlockSpec((tile, D_SGU), lambda i: (i, 0)),
                  pl.BlockSpec(memory_space=pl.ANY)],
        out_specs=[pl.BlockSpec((tile, 3 * D_SGU), lambda i: (i, 0)), vec, vec, wspec, bspec],
        out_shape=[jax.ShapeDtypeStruct(dp.shape, BF16), jax.ShapeDtypeStruct((1, D_SGU), F32),
                   jax.ShapeDtypeStruct((1, D_SGU), F32), jax.ShapeDtypeStruct((N_GROUPS, SGU_CHUNK, SGU_CHUNK), F32),
                   jax.ShapeDtypeStruct((SGU_CHUNK, N_GROUPS), F32)],
        input_output_aliases={8: 0},
        compiler_params=_params(("arbitrary",)),
    )(proj, proj, proj, ln_g, ln_b, w_s, b_t, dy, dp)


QKV_BLOCK = 1


def _qkv_head(cq, ck, cv):
    q, k, v = _silu(cq), _silu(ck), _silu(cv)
    q = q * lax.rsqrt(jnp.sum(q * q, axis=-1, keepdims=True) + EPS) * np.float32(HEAD ** -0.5)
    k = k * lax.rsqrt(jnp.sum(k * k, axis=-1, keepdims=True) + EPS)
    return q, k, v


def _gate_fn(p, a_log, dt_bias):
    lane = lax.broadcasted_iota(jnp.int32, p.shape, 1)
    g = -jnp.exp(a_log) * jax.nn.softplus(p + dt_bias)
    return jnp.where(lane < 16, g, jnp.where(lane < N_GATE, jax.nn.sigmoid(p), 0.0))


def _halo_maps(tile, s):
    per = tile // HALO
    prev = lambda i: jnp.maximum(i * per - 1, 0)
    nxt = lambda i: jnp.minimum((i + 1) * per, s // HALO - 1)
    return prev, nxt


def _prep_fwd(proj, conv_w, pg, a_log, dt_bias, *, tile=256):
    s = proj.shape[0]
    width = 3 * D_DN
    tile = min(tile, s)
    n_blk = s // tile
    prev, nxt = _halo_maps(tile, s)

    def body(x_ref, xp_ref, xn_ref, w_ref, pg_ref, al_ref, dt_ref, q_ref, k_ref, v_ref, gate_ref, ext):
        i = pl.program_id(0)
        ext[pl.ds(0, HALO), :] = jnp.where(i > 0, xp_ref[...], 0.0)
        ext[pl.ds(HALO, tile), :] = x_ref[...]
        ext[pl.ds(HALO + tile, HALO), :] = jnp.where(i < n_blk - 1, xn_ref[...], 0.0)
        c = None
        for j in range(CONV_W):
            term = ext[pl.ds(HALO - CONV_W // 2 + j, tile), :] * w_ref[j:j + 1, :]
            c = term if c is None else c + term
        for h in range(N_GROUPS):
            q, k, v = _qkv_head(c[:, h * HEAD:(h + 1) * HEAD], c[:, D_DN + h * HEAD:D_DN + (h + 1) * HEAD],
                                c[:, 2 * D_DN + h * HEAD:2 * D_DN + (h + 1) * HEAD])
            q_ref[h] = q
            k_ref[h] = k
            v_ref[h] = v
        gate_ref[...] = _gate_fn(pg_ref[...], al_ref[...], dt_ref[...])

    hm = pl.BlockSpec((N_GROUPS, tile, HEAD), lambda i: (0, i, 0))
    lane_vec = pl.BlockSpec((1, LANE), lambda i: (0, 0))
    return pl.pallas_call(
        body, name="prep_fwd", grid=(n_blk,),
        in_specs=[pl.BlockSpec((tile, width), lambda i: (i, QKV_BLOCK)),
                  pl.BlockSpec((HALO, width), lambda i: (prev(i), QKV_BLOCK)),
                  pl.BlockSpec((HALO, width), lambda i: (nxt(i), QKV_BLOCK)),
                  pl.BlockSpec((HALO, width), lambda i: (0, 0)),
                  pl.BlockSpec((tile, LANE), lambda i: (i, 0)), lane_vec, lane_vec],
        out_specs=[hm, hm, hm, pl.BlockSpec((tile, LANE), lambda i: (i, 0))],
        out_shape=[jax.ShapeDtypeStruct((N_GROUPS, s, HEAD), F32)] * 3 + [jax.ShapeDtypeStruct((s, LANE), F32)],
        scratch_shapes=[pltpu.VMEM((tile + 2 * HALO, width), F32)],
        compiler_params=_params(("arbitrary",)),
    )(proj, proj, proj, conv_w, pg, a_log, dt_bias)


def _prep_bwd(proj, conv_w, pg, a_log, dt_bias, dq, dk, dv, dgates, dp, *, tile=128):
    s = proj.shape[0]
    width = 3 * D_DN
    tile = min(tile, s)
    n_blk = s // tile
    prev, nxt = _halo_maps(tile, s)
    ext_rows = tile + 2 * HALO
    pad = CONV_W // 2

    def body(x_ref, xp_ref, xn_ref, w_ref, pg_ref, al_ref, dt_ref,
             dq_ref, dqp_ref, dqn_ref, dk_ref, dkp_ref, dkn_ref, dv_ref, dvp_ref, dvn_ref,
             dg0_ref, dg1_ref, dg2_ref, _,
             dp_ref, dpg_ref, dw_ref, dal_ref, ddt_ref, xext, dcext):
        i = pl.program_id(0)
        has_prev, has_next = i > 0, i < n_blk - 1

        @pl.when(i == 0)
        def _():
            dw_ref[...] = jnp.zeros_like(dw_ref)
            dal_ref[...] = jnp.zeros_like(dal_ref)
            ddt_ref[...] = jnp.zeros_like(ddt_ref)

        zeros = jnp.zeros((HALO, width), F32)
        xext[pl.ds(0, HALO), :] = zeros
        xext[pl.ds(HALO, HALO), :] = jnp.where(has_prev, xp_ref[...], 0.0)
        xext[pl.ds(2 * HALO, tile), :] = x_ref[...]
        xext[pl.ds(2 * HALO + tile, HALO), :] = jnp.where(has_next, xn_ref[...], 0.0)
        xext[pl.ds(3 * HALO + tile, HALO), :] = zeros
        c = None
        for j in range(CONV_W):
            term = xext[pl.ds(HALO - pad + j, ext_rows), :] * w_ref[j:j + 1, :]
            c = term if c is None else c + term
        for h in range(N_GROUPS):
            lanes = [pl.ds(part * D_DN + h * HEAD, HEAD) for part in range(3)]
            cts = []
            for cur, before, after in ((dq_ref, dqp_ref, dqn_ref), (dk_ref, dkp_ref, dkn_ref), (dv_ref, dvp_ref, dvn_ref)):
                cts.append(jnp.concatenate([jnp.where(has_prev, before[h], 0.0), cur[h], jnp.where(has_next, after[h], 0.0)], axis=0))
            _, vjp = jax.vjp(_qkv_head, c[:, h * HEAD:(h + 1) * HEAD], c[:, D_DN + h * HEAD:D_DN + (h + 1) * HEAD],
                             c[:, 2 * D_DN + h * HEAD:2 * D_DN + (h + 1) * HEAD])
            for lane_sl, dc in zip(lanes, vjp(tuple(cts))):
                dcext[:, lane_sl] = dc
        dx, dc_blk = None, dcext[pl.ds(HALO, tile), :]
        for j in range(CONV_W):
            term = dcext[pl.ds(HALO + pad - j, tile), :] * w_ref[j:j + 1, :]
            dx = term if dx is None else dx + term
            dw_ref[j:j + 1, :] += jnp.sum(dc_blk * xext[pl.ds(2 * HALO - pad + j, tile), :], axis=0, keepdims=True)
        dp_ref[...] = dx.astype(BF16)
        _, gvjp = jax.vjp(_gate_fn, pg_ref[...], al_ref[...], dt_ref[...])
        dpg, dal, ddt = gvjp(dg0_ref[...] + dg1_ref[...] + dg2_ref[...])
        dpg_ref[...] = dpg.astype(BF16)
        dal_ref[...] += dal
        ddt_ref[...] += ddt

    xs = [pl.BlockSpec((tile, width), lambda i: (i, QKV_BLOCK)), pl.BlockSpec((HALO, width), lambda i: (prev(i), QKV_BLOCK)),
          pl.BlockSpec((HALO, width), lambda i: (nxt(i), QKV_BLOCK))]
    hm = [pl.BlockSpec((N_GROUPS, tile, HEAD), lambda i: (0, i, 0)),
          pl.BlockSpec((N_GROUPS, HALO, HEAD), lambda i: (0, prev(i), 0)),
          pl.BlockSpec((N_GROUPS, HALO, HEAD), lambda i: (0, nxt(i), 0))]
    gate = pl.BlockSpec((tile, LANE), lambda i: (i, 0))
    lane_vec = pl.BlockSpec((1, LANE), lambda i: (0, 0))
    wspec = pl.BlockSpec((HALO, width), lambda i: (0, 0))
    n_in = 3 + 4 + 9 + 3
    return pl.pallas_call(
        body, name="prep_bwd", grid=(n_blk,),
        in_specs=xs + [wspec, gate, lane_vec, lane_vec] + hm * 3 + [gate] * 3 + [pl.BlockSpec(memory_space=pl.ANY)],
        out_specs=[pl.BlockSpec((tile, width), lambda i: (i, QKV_BLOCK)), gate, wspec, lane_vec, lane_vec],
        out_shape=[jax.ShapeDtypeStruct(dp.shape, BF16), jax.ShapeDtypeStruct((s, LANE), BF16),
                   jax.ShapeDtypeStruct((HALO, width), F32), jax.ShapeDtypeStruct((1, LANE), F32),
                   jax.ShapeDtypeStruct((1, LANE), F32)],
        input_output_aliases={n_in: 0},
        scratch_shapes=[pltpu.VMEM((tile + 4 * HALO, width), F32), pltpu.VMEM((ext_rows, width), F32)],
        compiler_params=_params(("arbitrary",)),
    )(proj, proj, proj, conv_w, pg, a_log, dt_bias, dq, dq, dq, dk, dk, dk, dv, dv, dv, *dgates, dp)


def _each(f, *cols):
    return tuple(f(*a) for a in zip(*cols))


def _delta_chunk(q, k, v, g_col, g_row, beta, state, t_known, reverse):
    n = DN_CHUNK
    ri = lax.broadcasted_iota(jnp.int32, (n, n), 0)
    ci = lax.broadcasted_iota(jnp.int32, (n, n), 1)
    if reverse:
        incl, incl_t, strict = ri <= ci, ri >= ci, ri < ci
    else:
        incl, incl_t, strict = ri >= ci, ri <= ci, ri > ci
    gc_col = _each(lambda gr: jnp.sum(jnp.where(incl, gr, 0.0), axis=1, keepdims=True), g_row)
    gc_row = _each(lambda gc: jnp.sum(jnp.where(incl_t, gc, 0.0), axis=0, keepdims=True), g_col)
    g_tot = _each(lambda gr: jnp.sum(gr, axis=1, keepdims=True), g_row)
    decay = _each(lambda a, b: jnp.where(incl, jnp.exp(jnp.where(incl, a - b, 0.0)), 0.0), gc_col, gc_row)
    kb = _each(lambda a, b: a * b, k, beta)
    kk = _each(_mm_nt, kb, k)
    m = _each(lambda a, d: jnp.where(strict, a * d, 0.0), kk, decay)
    if t_known is None:
        eye = (ri == ci).astype(F32)
        p = _each(lambda a: -a, m)
        t = _each(lambda a: eye + a, p)
        for _ in range(5):
            p = _each(lambda a: _dot3(a, a), p)
            t = _each(lambda a, b: a + _dot3(a, b), t, p)
    else:
        t = _each(_inverse_given, m, t_known)
    e_gc = _each(jnp.exp, gc_col)
    u = _each(lambda t_, v_, b_: _mm_nn(t_, v_ * b_), t, v, beta)
    w = _each(lambda t_, kb_, e_: _mm_nn(t_, kb_ * e_), t, kb, e_gc)
    attn = _each(lambda q_, k_, d_: _mm_nt(q_, k_) * d_, q, k, decay)
    ws = _each(_mm_nn, w, state)
    v_new = _each(lambda a, b: a - b, u, ws)
    qs = _each(lambda q_, e_, s_: _mm_nn(q_ * e_, s_), q, e_gc, state)
    av = _each(_mm_nn, attn, v_new)
    o = _each(lambda a, b: a + b, qs, av)
    kv = _each(lambda k_, gt, gc, vn: _mm_tn(k_ * jnp.exp(gt - gc), vn), k, g_tot, gc_col, v_new)
    new_state = _each(lambda s_, gt, kv_: s_ * jnp.exp(gt) + kv_, state, g_tot, kv)
    return o, new_state, t


def _delta_lanes(reverse):
    return (N_GROUPS, 3 * N_GROUPS) if reverse else (0, 2 * N_GROUPS)


def _delta_fwd(q, k, v, gates, g_rows, *, reverse):
    s = q.shape[1]
    nc = s // DN_CHUNK
    g_lane, b_lane = _delta_lanes(reverse)
    chunk_of = (lambda c: nc - 1 - c) if reverse else (lambda c: c)
    heads = range(N_GROUPS)

    def body(q_ref, k_ref, v_ref, gate_ref, grow_ref, o_ref, st_ref, t_ref, state):
        @pl.when(pl.program_id(0) == 0)
        def _():
            state[...] = jnp.zeros_like(state)

        gb = gate_ref[...]
        prev = tuple(state[j] for j in heads)
        o, new, t = _delta_chunk(tuple(q_ref[j] for j in heads), tuple(k_ref[j] for j in heads), tuple(v_ref[j] for j in heads),
                                 tuple(gb[:, g_lane + j:g_lane + j + 1] for j in heads), tuple(grow_ref[0, j:j + 1, :] for j in heads),
                                 tuple(gb[:, b_lane + j:b_lane + j + 1] for j in heads), prev, None, reverse)
        for j in heads:
            st_ref[0, j] = prev[j]
            o_ref[j] = o[j]
            t_ref[0, j] = t[j]
            state[j] = new[j]

    hm = pl.BlockSpec((N_GROUPS, DN_CHUNK, HEAD), lambda c: (0, chunk_of(c), 0))
    return pl.pallas_call(
        body, name="delta_fwd_rev" if reverse else "delta_fwd", grid=(nc,),
        in_specs=[hm, hm, hm, pl.BlockSpec((DN_CHUNK, LANE), lambda c: (chunk_of(c), 0)),
                  pl.BlockSpec((1, N_GROUPS, DN_CHUNK), lambda c: (chunk_of(c), 0, 0))],
        out_specs=[hm, pl.BlockSpec((1, N_GROUPS, HEAD, HEAD), lambda c: (chunk_of(c), 0, 0, 0)),
                   pl.BlockSpec((1, N_GROUPS, DN_CHUNK, DN_CHUNK), lambda c: (chunk_of(c), 0, 0, 0))],
        out_shape=[jax.ShapeDtypeStruct((N_GROUPS, s, HEAD), F32), jax.ShapeDtypeStruct((nc, N_GROUPS, HEAD, HEAD), F32),
                   jax.ShapeDtypeStruct((nc, N_GROUPS, DN_CHUNK, DN_CHUNK), F32)],
        scratch_shapes=[pltpu.VMEM((N_GROUPS, HEAD, HEAD), F32)],
        compiler_params=_params(("arbitrary",)),
    )(q, k, v, gates, g_rows)


def _delta_bwd(q, k, v, gates, g_rows, states, t_inv, do, acc, *, reverse):
    s = q.shape[1]
    nc = s // DN_CHUNK
    g_lane, b_lane = _delta_lanes(reverse)
    chunk_of = (lambda c: c) if reverse else (lambda c: nc - 1 - c)
    n_acc = 0 if acc is None else 3
    heads = range(N_GROUPS)

    def body(*refs):
        q_ref, k_ref, v_ref, gate_ref, grow_ref, st_ref, t_ref, do_ref = refs[:8]
        acc_refs = refs[8:8 + n_acc]
        dq_ref, dk_ref, dv_ref, dgate_ref, dgrow_ref, dstate = refs[8 + n_acc:]

        @pl.when(pl.program_id(0) == 0)
        def _():
            dstate[...] = jnp.zeros_like(dstate)

        gb = gate_ref[...]
        t_known = tuple(t_ref[0, j] for j in heads)
        f = lambda q_, k_, v_, gc_, gr_, b_, s_: _delta_chunk(q_, k_, v_, gc_, gr_, b_, s_, t_known, reverse)[:2]
        _, vjp = jax.vjp(f, tuple(q_ref[j] for j in heads), tuple(k_ref[j] for j in heads), tuple(v_ref[j] for j in heads),
                         tuple(gb[:, g_lane + j:g_lane + j + 1] for j in heads), tuple(grow_ref[0, j:j + 1, :] for j in heads),
                         tuple(gb[:, b_lane + j:b_lane + j + 1] for j in heads), tuple(st_ref[0, j] for j in heads))
        dq, dk, dv, dg_col, dg_row, dbeta, dprev = vjp((tuple(do_ref[j] for j in heads), tuple(dstate[j] for j in heads)))
        ids = lax.broadcasted_iota(jnp.int32, (DN_CHUNK, LANE), 1)
        dgate = jnp.zeros((DN_CHUNK, LANE), F32)
        for j in heads:
            if n_acc:
                dq_ref[j] = dq[j] + acc_refs[0][j]
                dk_ref[j] = dk[j] + acc_refs[1][j]
                dv_ref[j] = dv[j] + acc_refs[2][j]
            else:
                dq_ref[j] = dq[j]
                dk_ref[j] = dk[j]
                dv_ref[j] = dv[j]
            dstate[j] = dprev[j]
            dgate = dgate + jnp.where(ids == g_lane + j, dg_col[j], 0.0) + jnp.where(ids == b_lane + j, dbeta[j], 0.0)
            dgrow_ref[0, j:j + 1, :] = dg_row[j]
        dgate_ref[...] = dgate

    hm = pl.BlockSpec((N_GROUPS, DN_CHUNK, HEAD), lambda c: (0, chunk_of(c), 0))
    gate = pl.BlockSpec((DN_CHUNK, LANE), lambda c: (chunk_of(c), 0))
    rows = pl.BlockSpec((1, N_GROUPS, DN_CHUNK), lambda c: (chunk_of(c), 0, 0))
    st = pl.BlockSpec((1, N_GROUPS, HEAD, HEAD), lambda c: (chunk_of(c), 0, 0, 0))
    ts = pl.BlockSpec((1, N_GROUPS, DN_CHUNK, DN_CHUNK), lambda c: (chunk_of(c), 0, 0, 0))
    hm_shape = jax.ShapeDtypeStruct((N_GROUPS, s, HEAD), F32)
    return pl.pallas_call(
        body, name="delta_bwd_rev" if reverse else "delta_bwd", grid=(nc,),
        in_specs=[hm, hm, hm, gate, rows, st, ts, hm] + [hm] * n_acc,
        out_specs=[hm, hm, hm, gate, rows],
        out_shape=[hm_shape] * 3 + [jax.ShapeDtypeStruct((s, LANE), F32), jax.ShapeDtypeStruct((nc, N_GROUPS, DN_CHUNK), F32)],
        scratch_shapes=[pltpu.VMEM((N_GROUPS, HEAD, HEAD), F32)],
        compiler_params=_params(("arbitrary",)),
    )(q, k, v, gates, g_rows, states, t_inv, do, *(acc or ()))


ZB_BLOCK = 6


def _post_head(o_f, o_b, z, w):
    return _rms(o_f + o_b, w) * _silu(z)


def _post_fwd(o_f, o_b, proj, w, y, *, tile=256):
    s = proj.shape[0]
    tile = min(tile, s)

    def body(of_ref, ob_ref, z_ref, w_ref, _, y_ref):
        for h in range(N_GROUPS):
            lanes = pl.ds(h * HEAD, HEAD)
            y_ref[:, lanes] = _post_head(of_ref[h], ob_ref[h], z_ref[:, lanes], w_ref[...]).astype(BF16)

    hm = pl.BlockSpec((N_GROUPS, tile, HEAD), lambda i: (0, i, 0))
    return pl.pallas_call(
        body, name="post_fwd", grid=(s // tile,),
        in_specs=[hm, hm, pl.BlockSpec((tile, D_DN), lambda i: (i, ZB_BLOCK)), pl.BlockSpec((1, HEAD), lambda i: (0, 0)),
                  pl.BlockSpec(memory_space=pl.ANY)],
        out_specs=pl.BlockSpec((tile, D_DN), lambda i: (i, 1)),
        out_shape=jax.ShapeDtypeStruct(y.shape, BF16), input_output_aliases={4: 0},
        compiler_params=_params(("arbitrary",)),
    )(o_f, o_b, proj, w, y)


def _post_bwd(o_f, o_b, proj, w, dy, *, tile=256):
    s = proj.shape[0]
    tile = min(tile, s)

    def body(of_ref, ob_ref, z_ref, w_ref, dy_ref, do_ref, dp_ref, dw_ref):
        @pl.when(pl.program_id(0) == 0)
        def _():
            dw_ref[...] = jnp.zeros_like(dw_ref)

        for h in range(N_GROUPS):
            lanes = pl.ds(h * HEAD, HEAD)
            _, vjp = jax.vjp(_post_head, of_ref[h], ob_ref[h], z_ref[:, lanes], w_ref[...])
            do, _, dz, dw = vjp(dy_ref[:, lanes])
            do_ref[h] = do
            dp_ref[:, lanes] = dz.astype(BF16)
            dw_ref[...] += dw

    hm = pl.BlockSpec((N_GROUPS, tile, HEAD), lambda i: (0, i, 0))
    zb = pl.BlockSpec((tile, D_DN), lambda i: (i, ZB_BLOCK))
    vec = pl.BlockSpec((1, HEAD), lambda i: (0, 0))
    return pl.pallas_call(
        body, name="post_bwd", grid=(s // tile,),
        in_specs=[hm, hm, zb, vec, pl.BlockSpec((tile, D_DN), lambda i: (i, 1))], out_specs=[hm, zb, vec],
        out_shape=[jax.ShapeDtypeStruct((N_GROUPS, s, HEAD), F32), jax.ShapeDtypeStruct((s, D_MAIN), BF16),
                   jax.ShapeDtypeStruct((1, HEAD), F32)],
        compiler_params=_params(("arbitrary",)),
    )(o_f, o_b, proj, w, dy)


def _my_index():
    return 4 * lax.axis_index("x") + 2 * lax.axis_index("y") + lax.axis_index("c")


def _peer(k):
    x, y, c = lax.axis_index("x"), lax.axis_index("y"), lax.axis_index("c")
    px, py, pc = x ^ ((k >> 2) & 1), y ^ ((k >> 1) & 1), c ^ (k & 1)
    return (px, py, pc), 4 * px + 2 * py + pc


def _exchange(name, inputs, out_shapes, copies):
    n_in, n_out, n_cp = len(inputs), len(out_shapes), len(copies)

    def body(*refs):
        in_refs, out_refs = refs[:n_in], refs[n_in:n_in + n_out]
        send_sems, recv_sems, local_sems = refs[n_in + n_out:]
        me = _my_index()
        local = []
        for a, (i_in, i_out, src_of, dst_of) in enumerate(copies):
            local.append(pltpu.make_async_copy(src_of(in_refs[i_in], me), dst_of(out_refs[i_out], me), local_sems.at[a]))
            local[-1].start()
        sends = []
        for k in range(1, N_DEV):
            peer, peer_idx = _peer(k)
            for a, (i_in, i_out, src_of, dst_of) in enumerate(copies):
                sends.append(pltpu.make_async_remote_copy(
                    src_ref=src_of(in_refs[i_in], peer_idx), dst_ref=dst_of(out_refs[i_out], me),
                    send_sem=send_sems.at[a, k - 1], recv_sem=recv_sems.at[a, k - 1],
                    device_id=peer, device_id_type=pl.DeviceIdType.MESH))
                sends[-1].start()
        for k in range(1, N_DEV):
            peer, peer_idx = _peer(k)
            for a, (i_in, i_out, src_of, dst_of) in enumerate(copies):
                pltpu.make_async_remote_copy(
                    src_ref=src_of(in_refs[i_in], me), dst_ref=dst_of(out_refs[i_out], peer_idx),
                    send_sem=send_sems.at[a, k - 1], recv_sem=recv_sems.at[a, k - 1],
                    device_id=peer, device_id_type=pl.DeviceIdType.MESH).wait_recv()
        for cp in sends:
            cp.wait_send()
        for cp in local:
            cp.wait()

    any_spec = pl.BlockSpec(memory_space=pl.ANY)
    return pl.pallas_call(
        body, name=name, in_specs=[any_spec] * n_in, out_specs=[any_spec] * n_out, out_shape=out_shapes,
        scratch_shapes=[pltpu.SemaphoreType.DMA((n_cp, N_DEV - 1)), pltpu.SemaphoreType.DMA((n_cp, N_DEV - 1)),
                        pltpu.SemaphoreType.DMA((n_cp,))],
    )(*inputs)


def _whole(ref, _):
    return ref


def _slot(ref, dev):
    return ref.at[dev]


def _row_block(size, axis):
    def pick(ref, dev):
        start = pl.multiple_of(dev * size, size) if size % SUBLANES == 0 else dev * size
        return ref.at[tuple([slice(None)] * axis + [pl.ds(start, size)])]
    return pick


def _adamw(name, parts, w, m, v, block):
    nd = len(block)
    grid = tuple(w.shape[d] // block[d] for d in range(nd) if block[d] != w.shape[d])
    lead = len(grid)

    def body(p_ref, w_ref, m_ref, v_ref, g_ref, d_ref, mo_ref, vo_ref):
        g = p_ref[0]
        for j in range(1, N_DEV):
            g = g + p_ref[j]
        m_new = ADAM_B1 * m_ref[...] + (1.0 - ADAM_B1) * g
        v_new = ADAM_B2 * v_ref[...] + (1.0 - ADAM_B2) * jnp.square(g)
        m_hat = m_new / np.float32(1.0 - ADAM_B1 ** ADAM_STEP)
        v_hat = v_new / np.float32(1.0 - ADAM_B2 ** ADAM_STEP)
        g_ref[...] = g
        d_ref[...] = -ADAM_LR * (m_hat / (jnp.sqrt(v_hat) + ADAM_EPS) + ADAM_WD * w_ref[...])
        mo_ref[...] = m_new
        vo_ref[...] = v_new

    idx = lambda *i: tuple(i) + (0,) * (nd - lead)
    spec = pl.BlockSpec(block, idx)
    pspec = pl.BlockSpec((N_DEV,) + tuple(block), lambda *i: (0,) + idx(*i))
    n_blk = int(np.prod(block))
    return pl.pallas_call(
        body, name=name, grid=grid, in_specs=[pspec, spec, spec, spec], out_specs=[spec] * 4,
        out_shape=[jax.ShapeDtypeStruct(w.shape, F32)] * 4,
        compiler_params=_params(("arbitrary",) * lead, 2 * (N_DEV + 7) * n_blk * 4 + VMEM_MARGIN),
    )(parts, w, m, v)


def _to_flat(pieces, row_multiple):
    flat = jnp.concatenate([p.reshape(-1) for p in pieces])
    rows = -(-flat.shape[0] // FLAT_W)
    rows = -(-rows // row_multiple) * row_multiple
    return jnp.pad(flat, (0, rows * FLAT_W - flat.shape[0])).reshape(rows, FLAT_W)


def _lane_row(*pieces):
    flat = jnp.concatenate([p.reshape(-1) for p in pieces])
    return jnp.pad(flat, (0, LANE - flat.shape[0])).reshape(1, LANE)


def _rows_view(a):
    return a.reshape(a.shape[:-1] + (SUBLANES, LANE))


def _in_shard_t(a):
    return _rows_view(jnp.swapaxes(a, 1, 2))


def kernel(x, norm_w, w_in, sgu_ln_g, sgu_ln_b, sgu_w, sgu_b, conv_w, a_log_f, a_log_b, dt_bias_f, dt_bias_b, dn_norm_w, w_out, final_norm_w, loss_target, m_norm_w, m_w_in, m_sgu_ln_g, m_sgu_ln_b, m_sgu_w, m_sgu_b, m_conv_w, m_a_log_f, m_a_log_b, m_dt_bias_f, m_dt_bias_b, m_dn_norm_w, m_w_out, m_final_norm_w, v_norm_w, v_w_in, v_sgu_ln_g, v_sgu_ln_b, v_sgu_w, v_sgu_b, v_conv_w, v_a_log_f, v_a_log_b, v_dt_bias_f, v_dt_bias_b, v_dn_norm_w, v_w_out, v_final_norm_w):
    weights = dict(norm_w=norm_w, w_in=w_in, sgu_ln_g=sgu_ln_g, sgu_ln_b=sgu_ln_b, sgu_w=sgu_w, sgu_b=sgu_b, conv_w=conv_w,
                   a_log_f=a_log_f, a_log_b=a_log_b, dt_bias_f=dt_bias_f, dt_bias_b=dt_bias_b, dn_norm_w=dn_norm_w,
                   w_out=w_out, final_norm_w=final_norm_w)
    m_in = dict(norm_w=m_norm_w, w_in=m_w_in, sgu_ln_g=m_sgu_ln_g, sgu_ln_b=m_sgu_ln_b, sgu_w=m_sgu_w, sgu_b=m_sgu_b,
                conv_w=m_conv_w, a_log_f=m_a_log_f, a_log_b=m_a_log_b, dt_bias_f=m_dt_bias_f, dt_bias_b=m_dt_bias_b,
                dn_norm_w=m_dn_norm_w, w_out=m_w_out, final_norm_w=m_final_norm_w)
    v_in = dict(norm_w=v_norm_w, w_in=v_w_in, sgu_ln_g=v_sgu_ln_g, sgu_ln_b=v_sgu_ln_b, sgu_w=v_sgu_w, sgu_b=v_sgu_b,
                conv_w=v_conv_w, a_log_f=v_a_log_f, a_log_b=v_a_log_b, dt_bias_f=v_dt_bias_f, dt_bias_b=v_dt_bias_b,
                dn_norm_w=v_dn_norm_w, w_out=v_w_out, final_norm_w=v_final_norm_w)
    s = x.shape[1]
    nc = s // DN_CHUNK
    depth = w_in.shape[0]
    in_shard, out_shard, conv_shard = w_in.shape[2], w_out.shape[1], conv_w.shape[2]
    x0 = x.reshape(s, D_MODEL)
    target = loss_target.reshape(s, D_MODEL)

    w_in_all, w_out_all, conv_all = _exchange(
        "gather_weights",
        [_in_shard_t(w_in.astype(BF16)), w_out.astype(BF16), conv_w],
        [jax.ShapeDtypeStruct((depth, N_DEV * in_shard, SUBLANES, LANE), BF16),
         jax.ShapeDtypeStruct((depth, D_MODEL, D_MODEL), BF16),
         jax.ShapeDtypeStruct((N_DEV,) + conv_w.shape, F32)],
        [(0, 0, _whole, _row_block(in_shard, 1)), (1, 1, _whole, _row_block(out_shard, 1)), (2, 2, _whole, _slot)])
    w_in_t = w_in_all.reshape(depth, D_IN, D_MODEL)
    w_gate_t = jnp.pad(w_in_t[:, D_MAIN:], ((0, 0), (0, LANE - N_GATE), (0, 0)))
    conv_full = conv_all.transpose(1, 2, 0, 3).reshape(depth, CONV_W, 3 * D_DN)
    conv_full = jnp.pad(conv_full, ((0, 0), (0, HALO - CONV_W), (0, 0)))
    sgu_b_t = jnp.swapaxes(sgu_b, 1, 2)

    def rows_of(gates, first_lane):
        return gates[:, first_lane:first_lane + N_GROUPS].reshape(nc, DN_CHUNK, N_GROUPS).transpose(0, 2, 1)

    saved = []
    xl = x0
    for l in range(depth):
        nw = norm_w[l].reshape(1, D_MODEL)
        h = _rms_fwd(xl, nw)
        proj = _matmul([(_lhs(h), _rhs(w_in_t, MM_TILE, layer=l, transposed=True))], s, D_MAIN, name="in_proj", trans_b=True)
        p_gate = _matmul([(_lhs(h), _rhs(w_gate_t, LANE, layer=l, transposed=True))], s, LANE, name="in_proj_gate",
                         trans_b=True, tn=LANE)
        ln_g, ln_b = sgu_ln_g[l].reshape(1, D_SGU), sgu_ln_b[l].reshape(1, D_SGU)
        y = _sgu_fwd(proj, ln_g, ln_b, sgu_w[l], sgu_b_t[l])
        a_log = _lane_row(a_log_f[l], a_log_b[l])
        dt_bias = _lane_row(dt_bias_f[l], dt_bias_b[l])
        q, k, v, gates = _prep_fwd(proj, conv_full[l], p_gate, a_log, dt_bias)
        rows_f, rows_b = rows_of(gates, 0), rows_of(gates, N_GROUPS)
        o_f, st_f, t_f = _delta_fwd(q, k, v, gates, rows_f, reverse=False)
        o_b, st_b, t_b = _delta_fwd(q, k, v, gates, rows_b, reverse=True)
        dn_w = dn_norm_w[l].reshape(1, HEAD)
        y = _post_fwd(o_f, o_b, proj, dn_w, y)
        x_next = _matmul([(_lhs(y), _rhs(w_out_all, MM_TILE, layer=l))], s, D_MODEL, name="out_proj", add=xl)
        saved.append(dict(x=xl, nw=nw, h=h, proj=proj, p_gate=p_gate, ln_g=ln_g, ln_b=ln_b, a_log=a_log, dt_bias=dt_bias,
                          q=q, k=k, v=v, gates=gates, rows_f=rows_f, rows_b=rows_b, o_f=o_f, o_b=o_b, st_f=st_f, st_b=st_b,
                          t_f=t_f, t_b=t_b, dn_w=dn_w, y=y))
        xl = x_next

    loss_part, dx, dx_bf, d_final = _final_loss(xl, final_norm_w.reshape(1, D_MODEL), target)
    loss = lax.psum(loss_part[0, 0], MESH_AXES)

    small_names = [n for n in SMALL if n not in ("conv_w", "final_norm_w")]
    grads = {n: [None] * depth for n in small_names + ["conv_w", "w_in_t", "w_out"]}
    t_rows = -(-D_IN // LANE) * LANE
    for l in reversed(range(depth)):
        sv = saved[l]
        dy = _matmul([(_lhs(dx_bf), _rhs(w_out_all, MM_TILE, layer=l, transposed=True))], s, D_MODEL, name="out_proj_dy", trans_b=True)
        grads["w_out"][l] = _matmul([(_lhs(sv["y"].T), _rhs(dx_bf, MM_TILE))], D_MODEL, D_MODEL, name="out_proj_dw")
        do, dp, d_dn = _post_bwd(sv["o_f"], sv["o_b"], sv["proj"], sv["dn_w"], dy)
        dq, dk, dv, dgate_f, drows_f = _delta_bwd(sv["q"], sv["k"], sv["v"], sv["gates"], sv["rows_f"], sv["st_f"], sv["t_f"],
                                                  do, None, reverse=False)
        dq, dk, dv, dgate_b, drows_b = _delta_bwd(sv["q"], sv["k"], sv["v"], sv["gates"], sv["rows_b"], sv["st_b"], sv["t_b"],
                                                  do, (dq, dk, dv), reverse=True)
        drows = jnp.concatenate([drows_f.transpose(0, 2, 1).reshape(s, N_GROUPS), drows_b.transpose(0, 2, 1).reshape(s, N_GROUPS)], axis=1)
        drows = jnp.pad(drows, ((0, 0), (0, LANE - 2 * N_GROUPS)))
        dp, d_gate, d_conv, d_alog, d_dt = _prep_bwd(sv["proj"], conv_full[l], sv["p_gate"], sv["a_log"], sv["dt_bias"],
                                                     dq, dk, dv, (dgate_f, dgate_b, drows), dp)
        dp, d_lg, d_lb, d_sw, d_sbt = _sgu_bwd(sv["proj"], sv["ln_g"], sv["ln_b"], sgu_w[l], sgu_b_t[l], dy, dp)
        dh = _matmul([(_lhs(dp), _rhs(w_in_t, MM_TILE, layer=l, k=D_MAIN)), (_lhs(d_gate), _rhs(w_gate_t, MM_TILE, layer=l))],
                     s, D_MODEL, name="in_proj_dh")
        h_t = sv["h"].T
        g_in_t = _matmul([(_lhs(h_t), _rhs(dp, MM_TILE))], D_MODEL, D_MAIN, name="in_proj_dw", out_t_rows=t_rows)
        g_in_t = _matmul([(_lhs(h_t), _rhs(d_gate, LANE))], D_MODEL, LANE, name="in_proj_dw_gate", tn=LANE,
                         transpose_into=g_in_t, out_row_block=D_MAIN // LANE)
        grads["w_in_t"][l] = _rows_view(g_in_t)
        dx, dx_bf, d_nw = _rms_bwd(sv["x"], sv["nw"], dh, dx)
        grads["norm_w"][l] = d_nw.reshape(D_MODEL)
        grads["sgu_ln_g"][l], grads["sgu_ln_b"][l] = d_lg.reshape(D_SGU), d_lb.reshape(D_SGU)
        grads["sgu_w"][l], grads["sgu_b"][l] = d_sw, d_sbt.T
        grads["conv_w"][l] = d_conv[:CONV_W]
        grads["a_log_f"][l], grads["a_log_b"][l] = d_alog[0, :N_GROUPS], d_alog[0, N_GROUPS:2 * N_GROUPS]
        grads["dt_bias_f"][l], grads["dt_bias_b"][l] = d_dt[0, :N_GROUPS], d_dt[0, N_GROUPS:2 * N_GROUPS]
        grads["dn_norm_w"][l] = d_dn.reshape(HEAD)
    grad_x = dx.reshape(1, s, D_MODEL)

    g_small = {n: jnp.stack(grads[n]) for n in small_names}
    g_small["final_norm_w"] = d_final.reshape(D_MODEL)
    conv_by_dev = jnp.stack(grads["conv_w"]).reshape(depth, CONV_W, N_DEV, conv_shard).transpose(2, 0, 1, 3).reshape(N_DEV, -1)
    replicated = jnp.concatenate([g_small[n].reshape(-1) for n in SMALL if n != "conv_w"])
    n_small = conv_by_dev.shape[1] + replicated.shape[0]
    small_rows = -(-n_small // (FLAT_W * HALO)) * HALO
    send_small = jnp.concatenate([conv_by_dev, jnp.broadcast_to(replicated, (N_DEV, replicated.shape[0])),
                                  jnp.zeros((N_DEV, small_rows * FLAT_W - n_small), F32)], axis=1).reshape(N_DEV, small_rows, FLAT_W)
    layer_slot = lambda l: (lambda ref, dev: ref.at[dev, l])
    copies = [(l, 0, _row_block(in_shard, 0), layer_slot(l)) for l in range(depth)]
    copies += [(depth + l, 1, _row_block(out_shard, 0), layer_slot(l)) for l in range(depth)]
    copies += [(2 * depth, 2, _slot, _slot)]
    parts_in, parts_out, parts_small = _exchange(
        "exchange_grads", grads["w_in_t"] + grads["w_out"] + [send_small],
        [jax.ShapeDtypeStruct((N_DEV, depth, in_shard, SUBLANES, LANE), F32),
         jax.ShapeDtypeStruct((N_DEV, depth, out_shard, D_MODEL), F32),
         jax.ShapeDtypeStruct((N_DEV, small_rows, FLAT_W), F32)], copies)

    in_rows = 100 if in_shard % 100 == 0 else in_shard
    res_in = _adamw("adamw_in", parts_in, _in_shard_t(w_in), _in_shard_t(m_w_in), _in_shard_t(v_w_in),
                    (1, in_rows, SUBLANES, LANE))
    res_in = [jnp.swapaxes(r.reshape(depth, in_shard, D_MODEL), 1, 2) for r in res_in]
    res_out = _adamw("adamw_out", parts_out, w_out, m_w_out, v_w_out, (1, min(128, out_shard), D_MODEL))
    res_small = _adamw("adamw_small", parts_small, _to_flat([weights[n] for n in SMALL], HALO), _to_flat([m_in[n] for n in SMALL], HALO),
                       _to_flat([v_in[n] for n in SMALL], HALO), (small_rows // 4 if small_rows % 32 == 0 else small_rows, FLAT_W))
    outs = [dict(w_in=a, w_out=b) for a, b in zip(res_in, res_out)]
    off = 0
    for n in SMALL:
        size = int(np.prod(weights[n].shape))
        for o, d in zip(res_small, outs):
            d[n] = o.reshape(-1)[off:off + size].reshape(weights[n].shape)
        off += size
    g_out, delta_out, m_out, v_out = outs
    return (loss, grad_x, *[g_out[n] for n in WEIGHTS], *[delta_out[n] for n in WEIGHTS],
            *[m_out[n] for n in WEIGHTS], *[v_out[n] for n in WEIGHTS])
```

```python
import numpy as np
import jax
import jax.numpy as jnp
from jax import lax
from jax.experimental import pallas as pl
from jax.experimental.pallas import tpu as pltpu

F32 = jnp.float32
BF16 = jnp.bfloat16

N_DEV = 8
D_MODEL = 2048
D_SGU = 1024
D_DN = 1024
N_GROUPS = 8
HEAD = 128
SGU_CHUNK = 128
DN_CHUNK = 64
CONV_W = 5
N_GATE = 32
D_MAIN = 3 * D_SGU + 4 * D_DN
D_IN = D_MAIN + N_GATE
LANE = 128
SUBLANES = 16
HALO = 8
EPS = 1e-6
ADAM_LR, ADAM_B1, ADAM_B2, ADAM_EPS, ADAM_WD, ADAM_STEP = 0.001, 0.9, 0.999, 1e-08, 0.01, 10
FLAT_W = 1024
MM_TILE = 512
VMEM_MARGIN = 8 << 20

MESH_AXES = ("x", "y", "c")
WEIGHTS = ("norm_w", "w_in", "sgu_ln_g", "sgu_ln_b", "sgu_w", "sgu_b", "conv_w", "a_log_f", "a_log_b",
           "dt_bias_f", "dt_bias_b", "dn_norm_w", "w_out", "final_norm_w")
SMALL = ("conv_w",) + tuple(n for n in WEIGHTS if n not in ("w_in", "w_out", "conv_w"))


def _bdot(a, b, dims):
    return lax.dot_general(a.astype(BF16), b.astype(BF16), (dims, ((), ())), preferred_element_type=F32)


@jax.custom_vjp
def _mm_nn(a, b):
    return _bdot(a, b, ((1,), (0,)))


def _mm_nn_fwd(a, b):
    return _mm_nn(a, b), (a, b)


def _mm_nn_bwd(res, ct):
    a, b = res
    return _bdot(ct, b, ((1,), (1,))), _bdot(a, ct, ((0,), (0,)))


_mm_nn.defvjp(_mm_nn_fwd, _mm_nn_bwd)


@jax.custom_vjp
def _mm_nt(a, b):
    return _bdot(a, b, ((1,), (1,)))


def _mm_nt_fwd(a, b):
    return _mm_nt(a, b), (a, b)


def _mm_nt_bwd(res, ct):
    a, b = res
    return _bdot(ct, b, ((1,), (0,))), _bdot(ct, a, ((0,), (0,)))


_mm_nt.defvjp(_mm_nt_fwd, _mm_nt_bwd)


@jax.custom_vjp
def _mm_tn(a, b):
    return _bdot(a, b, ((0,), (0,)))


def _mm_tn_fwd(a, b):
    return _mm_tn(a, b), (a, b)


def _mm_tn_bwd(res, ct):
    a, b = res
    return _bdot(b, ct, ((1,), (1,))), _bdot(a, ct, ((1,), (0,)))


_mm_tn.defvjp(_mm_tn_fwd, _mm_tn_bwd)


def _split_hi_lo(a):
    hi = a.astype(BF16)
    return hi, (a - hi.astype(F32)).astype(BF16)


def _dot3(a, b, dims=((1,), (0,))):
    (ah, al), (bh, bl) = _split_hi_lo(a), _split_hi_lo(b)
    d = lambda x, y: lax.dot_general(x, y, (dims, ((), ())), preferred_element_type=F32)
    return d(ah, bh) + (d(ah, bl) + d(al, bh))


@jax.custom_vjp
def _inverse_given(m, t):
    return t


def _inverse_given_fwd(m, t):
    return t, t


def _inverse_given_bwd(t, ct):
    dm = -_dot3(t, _dot3(ct, t, ((1,), (1,))), ((0,), (0,)))
    return dm, jnp.zeros_like(t)


_inverse_given.defvjp(_inverse_given_fwd, _inverse_given_bwd)


@jax.custom_vjp
def _split_lanes(x):
    return tuple(x[:, i * LANE:(i + 1) * LANE] for i in range(x.shape[1] // LANE))


def _split_lanes_fwd(x):
    return _split_lanes(x), None


def _split_lanes_bwd(_, cts):
    return (jnp.concatenate(cts, axis=1),)


_split_lanes.defvjp(_split_lanes_fwd, _split_lanes_bwd)


def _silu(t):
    return t * jax.nn.sigmoid(t)


def _gelu(t):
    return 0.5 * t * (1.0 + lax.erf(t * np.float32(0.7071067811865476)))


def _rms(x, w):
    return x * lax.rsqrt(jnp.mean(x * x, axis=-1, keepdims=True) + EPS) * w


def _params(sem, vmem_bytes=None):
    kw = dict(dimension_semantics=sem)
    if vmem_bytes is not None:
        kw["vmem_limit_bytes"] = int(vmem_bytes)
    return pltpu.CompilerParams(**kw)


def _lhs(a, k=None, k_block=0):
    k = a.shape[1] if k is None else k
    tm = min(MM_TILE, a.shape[0])
    return a, pl.BlockSpec((tm, k), lambda i, j: (i, k_block)), tm * k * a.dtype.itemsize


def _rhs(b, tn, *, layer=None, transposed=False, k=None, k_block=0, n_offset=0):
    lead = () if layer is None else (layer,)
    none = () if layer is None else (None,)
    if transposed:
        k = b.shape[-1]
        spec = pl.BlockSpec(none + (tn, k), lambda i, j: lead + (j + n_offset, 0))
    else:
        k = b.shape[-2] if k is None else k
        spec = pl.BlockSpec(none + (k, tn), lambda i, j: lead + (k_block, j))
    return b, spec, tn * k * b.dtype.itemsize


def _matmul(pairs, m, n, *, name, trans_b=False, add=None, out_dtype=F32, tn=MM_TILE, out_t_rows=None, transpose_into=None,
            out_row_block=0):
    tm = min(MM_TILE, m)
    assert m % tm == 0 and n % tn == 0
    n_pairs = len(pairs)
    contract = ((1,), (1,)) if trans_b else ((1,), (0,))
    transposed_out = out_t_rows is not None or transpose_into is not None

    def body(*refs):
        o_ref = refs[-1]
        acc = None
        for i in range(n_pairs):
            d = lax.dot_general(refs[2 * i][...].astype(BF16), refs[2 * i + 1][...].astype(BF16),
                                (contract, ((), ())), preferred_element_type=F32)
            acc = d if acc is None else acc + d
        if add is not None:
            acc = acc + refs[2 * n_pairs][...]
        o_ref[...] = (acc.T if transposed_out else acc).astype(out_dtype)

    in_specs, args, vmem, aliases = [], [], 0, {}
    for (a, a_spec, a_bytes), (b, b_spec, b_bytes) in pairs:
        in_specs += [a_spec, b_spec]
        args += [a, b]
        vmem += 2 * (a_bytes + b_bytes)
    if add is not None:
        in_specs.append(pl.BlockSpec((tm, tn), lambda i, j: (i, j)))
        args.append(add)
        vmem += 2 * tm * tn * 4
    vmem += 2 * tm * tn * jnp.dtype(out_dtype).itemsize + 3 * tm * tn * 4
    if transposed_out:
        out_spec = pl.BlockSpec((tn, tm), lambda i, j: (j + out_row_block, i))
        if transpose_into is not None:
            in_specs.append(pl.BlockSpec(memory_space=pl.ANY))
            args.append(transpose_into)
            aliases = {len(args) - 1: 0}
            out_shape = jax.ShapeDtypeStruct(transpose_into.shape, out_dtype)
        else:
            out_shape = jax.ShapeDtypeStruct((out_t_rows, m), out_dtype)
    else:
        out_spec = pl.BlockSpec((tm, tn), lambda i, j: (i, j))
        out_shape = jax.ShapeDtypeStruct((m, n), out_dtype)
    return pl.pallas_call(
        body, name=name, grid=(m // tm, n // tn), in_specs=in_specs, out_specs=out_spec, out_shape=out_shape,
        input_output_aliases=aliases, compiler_params=_params(("parallel", "arbitrary"), vmem + VMEM_MARGIN),
    )(*args)


def _rms_fwd(x, w, *, tile=512):
    s, d = x.shape
    tile = min(tile, s)

    def body(x_ref, w_ref, h_ref):
        h_ref[...] = _rms(x_ref[...], w_ref[...]).astype(BF16)

    return pl.pallas_call(
        body, name="rms_fwd", grid=(s // tile,),
        in_specs=[pl.BlockSpec((tile, d), lambda i: (i, 0)), pl.BlockSpec((1, d), lambda i: (0, 0))],
        out_specs=pl.BlockSpec((tile, d), lambda i: (i, 0)),
        out_shape=jax.ShapeDtypeStruct((s, d), BF16),
        compiler_params=_params(("arbitrary",)),
    )(x, w)


def _rms_bwd(x, w, dh, dx_out, *, tile=256):
    s, d = x.shape
    tile = min(tile, s)

    def body(x_ref, w_ref, dh_ref, dxo_ref, dx_ref, dxb_ref, dw_ref):
        _, vjp = jax.vjp(_rms, x_ref[...], w_ref[...])
        dxv, dwv = vjp(dh_ref[...])
        dx = dxo_ref[...] + dxv
        dx_ref[...] = dx
        dxb_ref[...] = dx.astype(BF16)

        @pl.when(pl.program_id(0) == 0)
        def _():
            dw_ref[...] = jnp.zeros_like(dw_ref)

        dw_ref[...] += dwv

    row = pl.BlockSpec((tile, d), lambda i: (i, 0))
    vec = pl.BlockSpec((1, d), lambda i: (0, 0))
    return pl.pallas_call(
        body, name="rms_bwd", grid=(s // tile,),
        in_specs=[row, vec, row, row], out_specs=[row, row, vec],
        out_shape=[jax.ShapeDtypeStruct((s, d), F32), jax.ShapeDtypeStruct((s, d), BF16), jax.ShapeDtypeStruct((1, d), F32)],
        compiler_params=_params(("arbitrary",)),
    )(x, w, dh, dx_out)


def _final_loss(x, w, target, *, tile=256):
    s, d = x.shape
    tile = min(tile, s)

    def body(x_ref, w_ref, t_ref, loss_ref, dx_ref, dxb_ref, dw_ref):
        def f(xv, wv):
            err = jnp.square(_rms(xv, wv) - t_ref[...])
            per_token = jnp.sum(err, axis=1, keepdims=True) * np.float32(1.0 / d)
            return 0.5 * jnp.sum(per_token, axis=0, keepdims=True)

        loss, vjp = jax.vjp(f, x_ref[...], w_ref[...])
        dxv, dwv = vjp(jnp.ones((1, 1), F32))
        dx_ref[...] = dxv
        dxb_ref[...] = dxv.astype(BF16)

        @pl.when(pl.program_id(0) == 0)
        def _():
            dw_ref[...] = jnp.zeros_like(dw_ref)
            loss_ref[...] = jnp.zeros_like(loss_ref)

        dw_ref[...] += dwv
        loss_ref[...] += jnp.broadcast_to(loss, (1, LANE))

    row = pl.BlockSpec((tile, d), lambda i: (i, 0))
    vec = pl.BlockSpec((1, d), lambda i: (0, 0))
    return pl.pallas_call(
        body, name="final_loss", grid=(s // tile,),
        in_specs=[row, vec, row], out_specs=[pl.BlockSpec((1, LANE), lambda i: (0, 0)), row, row, vec],
        out_shape=[jax.ShapeDtypeStruct((1, LANE), F32), jax.ShapeDtypeStruct((s, d), F32),
                   jax.ShapeDtypeStruct((s, d), BF16), jax.ShapeDtypeStruct((1, d), F32)],
        compiler_params=_params(("arbitrary",)),
    )(x, w, target)


def _sgu_chunk(u, v, z, ln_g, ln_b, ws, bcols):
    vg = _gelu(v)
    xc = vg - jnp.mean(vg, axis=-1, keepdims=True)
    vl = xc * lax.rsqrt(jnp.mean(xc * xc, axis=-1, keepdims=True) + EPS) * ln_g + ln_b
    sp = jnp.concatenate([_mm_nn(ws[g], vg_) + bcols[g] for g, vg_ in enumerate(_split_lanes(vl))], axis=1)
    return _gelu(u) * sp * _silu(z)


def _sgu_specs(tile):
    col = lambda j: pl.BlockSpec((tile, D_SGU), lambda i, j=j: (i, j))
    vec = pl.BlockSpec((1, D_SGU), lambda i: (0, 0))
    wspec = pl.BlockSpec((N_GROUPS, SGU_CHUNK, SGU_CHUNK), lambda i: (0, 0, 0))
    bspec = pl.BlockSpec((SGU_CHUNK, N_GROUPS), lambda i: (0, 0))
    return col, vec, wspec, bspec


def _sgu_fwd(proj, ln_g, ln_b, w_s, b_t, *, tile=256):
    s = proj.shape[0]
    tile = min(tile, s)
    col, vec, wspec, bspec = _sgu_specs(tile)

    def body(u_ref, v_ref, z_ref, g_ref, b_ref, w_ref, bt_ref, y_ref):
        ws = tuple(w_ref[g] for g in range(N_GROUPS))
        bcols = tuple(bt_ref[:, g:g + 1] for g in range(N_GROUPS))
        for c in range(tile // SGU_CHUNK):
            rows = pl.ds(c * SGU_CHUNK, SGU_CHUNK)
            y = _sgu_chunk(u_ref[rows, :], v_ref[rows, :], z_ref[rows, :], g_ref[...], b_ref[...], ws, bcols)
            y_ref[rows, :] = y.astype(BF16)

    return pl.pallas_call(
        body, name="sgu_fwd", grid=(s // tile,),
        in_specs=[col(0), col(1), col(2), vec, vec, wspec, bspec],
        out_specs=pl.BlockSpec((tile, D_SGU), lambda i: (i, 0)),
        out_shape=jax.ShapeDtypeStruct((s, D_SGU + D_DN), BF16),
        compiler_params=_params(("arbitrary",)),
    )(proj, proj, proj, ln_g, ln_b, w_s, b_t)


def _sgu_bwd(proj, ln_g, ln_b, w_s, b_t, dy, dp, *, tile=128):
    s = proj.shape[0]
    tile = min(tile, s)
    col, vec, wspec, bspec = _sgu_specs(tile)

    def body(u_ref, v_ref, z_ref, g_ref, b_ref, w_ref, bt_ref, dy_ref, _, dp_ref, dg_ref, db_ref, dw_ref, dbt_ref):
        @pl.when(pl.program_id(0) == 0)
        def _():
            dg_ref[...] = jnp.zeros_like(dg_ref)
            db_ref[...] = jnp.zeros_like(db_ref)
            dw_ref[...] = jnp.zeros_like(dw_ref)
            dbt_ref[...] = jnp.zeros_like(dbt_ref)

        ws = tuple(w_ref[g] for g in range(N_GROUPS))
        bcols = tuple(bt_ref[:, g:g + 1] for g in range(N_GROUPS))
        for c in range(tile // SGU_CHUNK):
            rows = pl.ds(c * SGU_CHUNK, SGU_CHUNK)
            _, vjp = jax.vjp(_sgu_chunk, u_ref[rows, :], v_ref[rows, :], z_ref[rows, :], g_ref[...], b_ref[...], ws, bcols)
            du, dv, dz, dg, db, dws, dbc = vjp(dy_ref[rows, :])
            dp_ref[rows, pl.ds(0, D_SGU)] = du.astype(BF16)
            dp_ref[rows, pl.ds(D_SGU, D_SGU)] = dv.astype(BF16)
            dp_ref[rows, pl.ds(2 * D_SGU, D_SGU)] = dz.astype(BF16)
            dg_ref[...] += dg
            db_ref[...] += db
            for g in range(N_GROUPS):
                dw_ref[g] += dws[g]
                dbt_ref[:, g:g + 1] += dbc[g]

    return pl.pallas_call(
        body, name="sgu_bwd", grid=(s // tile,),
        in_specs=[col(0), col(1), col(2), vec, vec, wspec, bspec, pl.BlockSpec((tile, D_SGU), lambda i: (i, 0)),
                  pl.BlockSpec(memory_space=pl.ANY)],
        out_specs=[pl.BlockSpec((tile, 3 * D_SGU), lambda i: (i, 0)), vec, vec, wspec, bspec],
        out_shape=[jax.ShapeDtypeStruct(dp.shape, BF16), jax.ShapeDtypeStruct((1, D_SGU), F32),
                   jax.ShapeDtypeStruct((1, D_SGU), F32), jax.ShapeDtypeStruct((N_GROUPS, SGU_CHUNK, SGU_CHUNK), F32),
                   jax.ShapeDtypeStruct((SGU_CHUNK, N_GROUPS), F32)],
        input_output_aliases={8: 0},
        compiler_params=_params(("arbitrary",)),
    )(proj, proj, proj, ln_g, ln_b, w_s, b_t, dy, dp)


QKV_BLOCK = 1


def _qkv_head(cq, ck, cv):
    q, k, v = _silu(cq), _silu(ck), _silu(cv)
    q = q * lax.rsqrt(jnp.sum(q * q, axis=-1, keepdims=True) + EPS) * np.float32(HEAD ** -0.5)
    k = k * lax.rsqrt(jnp.sum(k * k, axis=-1, keepdims=True) + EPS)
    return q, k, v


def _gate_fn(p, a_log, dt_bias):
    lane = lax.broadcasted_iota(jnp.int32, p.shape, 1)
    g = -jnp.exp(a_log) * jax.nn.softplus(p + dt_bias)
    return jnp.where(lane < 16, g, jnp.where(lane < N_GATE, jax.nn.sigmoid(p), 0.0))


def _halo_maps(tile, s):
    per = tile // HALO
    prev = lambda i: jnp.maximum(i * per - 1, 0)
    nxt = lambda i: jnp.minimum((i + 1) * per, s // HALO - 1)
    return prev, nxt


def _prep_fwd(proj, conv_w, pg, a_log, dt_bias, *, tile=256):
    s = proj.shape[0]
    width = 3 * D_DN
    tile = min(tile, s)
    n_blk = s // tile
    prev, nxt = _halo_maps(tile, s)

    def body(x_ref, xp_ref, xn_ref, w_ref, pg_ref, al_ref, dt_ref, q_ref, k_ref, v_ref, gate_ref, ext):
        i = pl.program_id(0)
        ext[pl.ds(0, HALO), :] = jnp.where(i > 0, xp_ref[...], 0.0)
        ext[pl.ds(HALO, tile), :] = x_ref[...]
        ext[pl.ds(HALO + tile, HALO), :] = jnp.where(i < n_blk - 1, xn_ref[...], 0.0)
        c = None
        for j in range(CONV_W):
            term = ext[pl.ds(HALO - CONV_W // 2 + j, tile), :] * w_ref[j:j + 1, :]
            c = term if c is None else c + term
        for h in range(N_GROUPS):
            q, k, v = _qkv_head(c[:, h * HEAD:(h + 1) * HEAD], c[:, D_DN + h * HEAD:D_DN + (h + 1) * HEAD],
                                c[:, 2 * D_DN + h * HEAD:2 * D_DN + (h + 1) * HEAD])
            q_ref[h] = q
            k_ref[h] = k
            v_ref[h] = v
        gate_ref[...] = _gate_fn(pg_ref[...], al_ref[...], dt_ref[...])

    hm = pl.BlockSpec((N_GROUPS, tile, HEAD), lambda i: (0, i, 0))
    lane_vec = pl.BlockSpec((1, LANE), lambda i: (0, 0))
    return pl.pallas_call(
        body, name="prep_fwd", grid=(n_blk,),
        in_specs=[pl.BlockSpec((tile, width), lambda i: (i, QKV_BLOCK)),
                  pl.BlockSpec((HALO, width), lambda i: (prev(i), QKV_BLOCK)),
                  pl.BlockSpec((HALO, width), lambda i: (nxt(i), QKV_BLOCK)),
                  pl.BlockSpec((HALO, width), lambda i: (0, 0)),
                  pl.BlockSpec((tile, LANE), lambda i: (i, 0)), lane_vec, lane_vec],
        out_specs=[hm, hm, hm, pl.BlockSpec((tile, LANE), lambda i: (i, 0))],
        out_shape=[jax.ShapeDtypeStruct((N_GROUPS, s, HEAD), F32)] * 3 + [jax.ShapeDtypeStruct((s, LANE), F32)],
        scratch_shapes=[pltpu.VMEM((tile + 2 * HALO, width), F32)],
        compiler_params=_params(("arbitrary",)),
    )(proj, proj, proj, conv_w, pg, a_log, dt_bias)


def _prep_bwd(proj, conv_w, pg, a_log, dt_bias, dq, dk, dv, dgates, dp, *, tile=128):
    s = proj.shape[0]
    width = 3 * D_DN
    tile = min(tile, s)
    n_blk = s // tile
    prev, nxt = _halo_maps(tile, s)
    ext_rows = tile + 2 * HALO
    pad = CONV_W // 2

    def body(x_ref, xp_ref, xn_ref, w_ref, pg_ref, al_ref, dt_ref,
             dq_ref, dqp_ref, dqn_ref, dk_ref, dkp_ref, dkn_ref, dv_ref, dvp_ref, dvn_ref,
             dg0_ref, dg1_ref, dg2_ref, _,
             dp_ref, dpg_ref, dw_ref, dal_ref, ddt_ref, xext, dcext):
        i = pl.program_id(0)
        has_prev, has_next = i > 0, i < n_blk - 1

        @pl.when(i == 0)
        def _():
            dw_ref[...] = jnp.zeros_like(dw_ref)
            dal_ref[...] = jnp.zeros_like(dal_ref)
            ddt_ref[...] = jnp.zeros_like(ddt_ref)

        zeros = jnp.zeros((HALO, width), F32)
        xext[pl.ds(0, HALO), :] = zeros
        xext[pl.ds(HALO, HALO), :] = jnp.where(has_prev, xp_ref[...], 0.0)
        xext[pl.ds(2 * HALO, tile), :] = x_ref[...]
        xext[pl.ds(2 * HALO + tile, HALO), :] = jnp.where(has_next, xn_ref[...], 0.0)
        xext[pl.ds(3 * HALO + tile, HALO), :] = zeros
        c = None
        for j in range(CONV_W):
            term = xext[pl.ds(HALO - pad + j, ext_rows), :] * w_ref[j:j + 1, :]
            c = term if c is None else c + term
        for h in range(N_GROUPS):
            lanes = [pl.ds(part * D_DN + h * HEAD, HEAD) for part in range(3)]
            cts = []
            for cur, before, after in ((dq_ref, dqp_ref, dqn_ref), (dk_ref, dkp_ref, dkn_ref), (dv_ref, dvp_ref, dvn_ref)):
                cts.append(jnp.concatenate([jnp.where(has_prev, before[h], 0.0), cur[h], jnp.where(has_next, after[h], 0.0)], axis=0))
            _, vjp = jax.vjp(_qkv_head, c[:, h * HEAD:(h + 1) * HEAD], c[:, D_DN + h * HEAD:D_DN + (h + 1) * HEAD],
                             c[:, 2 * D_DN + h * HEAD:2 * D_DN + (h + 1) * HEAD])
            for lane_sl, dc in zip(lanes, vjp(tuple(cts))):
                dcext[:, lane_sl] = dc
        dx, dc_blk = None, dcext[pl.ds(HALO, tile), :]
        for j in range(CONV_W):
            term = dcext[pl.ds(HALO + pad - j, tile), :] * w_ref[j:j + 1, :]
            dx = term if dx is None else dx + term
            dw_ref[j:j + 1, :] += jnp.sum(dc_blk * xext[pl.ds(2 * HALO - pad + j, tile), :], axis=0, keepdims=True)
        dp_ref[...] = dx.astype(BF16)
        _, gvjp = jax.vjp(_gate_fn, pg_ref[...], al_ref[...], dt_ref[...])
        dpg, dal, ddt = gvjp(dg0_ref[...] + dg1_ref[...] + dg2_ref[...])
        dpg_ref[...] = dpg.astype(BF16)
        dal_ref[...] += dal
        ddt_ref[...] += ddt

    xs = [pl.BlockSpec((tile, width), lambda i: (i, QKV_BLOCK)), pl.BlockSpec((HALO, width), lambda i: (prev(i), QKV_BLOCK)),
          pl.BlockSpec((HALO, width), lambda i: (nxt(i), QKV_BLOCK))]
    hm = [pl.BlockSpec((N_GROUPS, tile, HEAD), lambda i: (0, i, 0)),
          pl.BlockSpec((N_GROUPS, HALO, HEAD), lambda i: (0, prev(i), 0)),
          pl.BlockSpec((N_GROUPS, HALO, HEAD), lambda i: (0, nxt(i), 0))]
    gate = pl.BlockSpec((tile, LANE), lambda i: (i, 0))
    lane_vec = pl.BlockSpec((1, LANE), lambda i: (0, 0))
    wspec = pl.BlockSpec((HALO, width), lambda i: (0, 0))
    n_in = 3 + 4 + 9 + 3
    return pl.pallas_call(
        body, name="prep_bwd", grid=(n_blk,),
        in_specs=xs + [wspec, gate, lane_vec, lane_vec] + hm * 3 + [gate] * 3 + [pl.BlockSpec(memory_space=pl.ANY)],
        out_specs=[pl.BlockSpec((tile, width), lambda i: (i, QKV_BLOCK)), gate, wspec, lane_vec, lane_vec],
        out_shape=[jax.ShapeDtypeStruct(dp.shape, BF16), jax.ShapeDtypeStruct((s, LANE), BF16),
                   jax.ShapeDtypeStruct((HALO, width), F32), jax.ShapeDtypeStruct((1, LANE), F32),
                   jax.ShapeDtypeStruct((1, LANE), F32)],
        input_output_aliases={n_in: 0},
        scratch_shapes=[pltpu.VMEM((tile + 4 * HALO, width), F32), pltpu.VMEM((ext_rows, width), F32)],
        compiler_params=_params(("arbitrary",)),
    )(proj, proj, proj, conv_w, pg, a_log, dt_bias, dq, dq, dq, dk, dk, dk, dv, dv, dv, *dgates, dp)


def _each(f, *cols):
    return tuple(f(*a) for a in zip(*cols))


def _delta_chunk(q, k, v, g_col, g_row, beta, state, t_known, reverse):
    n = DN_CHUNK
    ri = lax.broadcasted_iota(jnp.int32, (n, n), 0)
    ci = lax.broadcasted_iota(jnp.int32, (n, n), 1)
    if reverse:
        incl, incl_t, strict = ri <= ci, ri >= ci, ri < ci
    else:
        incl, incl_t, strict = ri >= ci, ri <= ci, ri > ci
    gc_col = _each(lambda gr: jnp.sum(jnp.where(incl, gr, 0.0), axis=1, keepdims=True), g_row)
    gc_row = _each(lambda gc: jnp.sum(jnp.where(incl_t, gc, 0.0), axis=0, keepdims=True), g_col)
    g_tot = _each(lambda gr: jnp.sum(gr, axis=1, keepdims=True), g_row)
    decay = _each(lambda a, b: jnp.where(incl, jnp.exp(jnp.where(incl, a - b, 0.0)), 0.0), gc_col, gc_row)
    kb = _each(lambda a, b: a * b, k, beta)
    kk = _each(_mm_nt, kb, k)
    m = _each(lambda a, d: jnp.where(strict, a * d, 0.0), kk, decay)
    if t_known is None:
        eye = (ri == ci).astype(F32)
        p = _each(lambda a: -a, m)
        t = _each(lambda a: eye + a, p)
        for _ in range(5):
            p = _each(lambda a: _dot3(a, a), p)
            t = _each(lambda a, b: a + _dot3(a, b), t, p)
    else:
        t = _each(_inverse_given, m, t_known)
    e_gc = _each(jnp.exp, gc_col)
    u = _each(lambda t_, v_, b_: _mm_nn(t_, v_ * b_), t, v, beta)
    w = _each(lambda t_, kb_, e_: _mm_nn(t_, kb_ * e_), t, kb, e_gc)
    attn = _each(lambda q_, k_, d_: _mm_nt(q_, k_) * d_, q, k, decay)
    ws = _each(_mm_nn, w, state)
    v_new = _each(lambda a, b: a - b, u, ws)
    qs = _each(lambda q_, e_, s_: _mm_nn(q_ * e_, s_), q, e_gc, state)
    av = _each(_mm_nn, attn, v_new)
    o = _each(lambda a, b: a + b, qs, av)
    kv = _each(lambda k_, gt, gc, vn: _mm_tn(k_ * jnp.exp(gt - gc), vn), k, g_tot, gc_col, v_new)
    new_state = _each(lambda s_, gt, kv_: s_ * jnp.exp(gt) + kv_, state, g_tot, kv)
    return o, new_state, t


def _delta_lanes(reverse):
    return (N_GROUPS, 3 * N_GROUPS) if reverse else (0, 2 * N_GROUPS)


def _delta_fwd(q, k, v, gates, g_rows, *, reverse, comm=None):
    s = q.shape[1]
    nc = s // DN_CHUNK
    g_lane, b_lane = _delta_lanes(reverse)
    chunk_of = (lambda c: nc - 1 - c) if reverse else (lambda c: c)
    heads = range(N_GROUPS)

    def body(q_ref, k_ref, v_ref, gate_ref, grow_ref, o_ref, st_ref, t_ref, state):
        @pl.when(pl.program_id(0) == 0)
        def _():
            state[...] = jnp.zeros_like(state)

        gb = gate_ref[...]
        prev = tuple(state[j] for j in heads)
        o, new, t = _delta_chunk(tuple(q_ref[j] for j in heads), tuple(k_ref[j] for j in heads), tuple(v_ref[j] for j in heads),
                                 tuple(gb[:, g_lane + j:g_lane + j + 1] for j in heads), tuple(grow_ref[0, j:j + 1, :] for j in heads),
                                 tuple(gb[:, b_lane + j:b_lane + j + 1] for j in heads), prev, None, reverse)
        for j in heads:
            st_ref[0, j] = prev[j]
            o_ref[j] = o[j]
            t_ref[0, j] = t[j]
            state[j] = new[j]

    hm = pl.BlockSpec((N_GROUPS, DN_CHUNK, HEAD), lambda c: (0, chunk_of(c), 0))
    name = ("delta_fwd_rev" if reverse else "delta_fwd") + ("" if comm is None else "_comm")
    return _call_with_comm(
        body, comm, name=name, grid=(nc,),
        in_specs=[hm, hm, hm, pl.BlockSpec((DN_CHUNK, LANE), lambda c: (chunk_of(c), 0)),
                  pl.BlockSpec((1, N_GROUPS, DN_CHUNK), lambda c: (chunk_of(c), 0, 0))],
        out_specs=[hm, pl.BlockSpec((1, N_GROUPS, HEAD, HEAD), lambda c: (chunk_of(c), 0, 0, 0)),
                   pl.BlockSpec((1, N_GROUPS, DN_CHUNK, DN_CHUNK), lambda c: (chunk_of(c), 0, 0, 0))],
        out_shape=[jax.ShapeDtypeStruct((N_GROUPS, s, HEAD), F32), jax.ShapeDtypeStruct((nc, N_GROUPS, HEAD, HEAD), F32),
                   jax.ShapeDtypeStruct((nc, N_GROUPS, DN_CHUNK, DN_CHUNK), F32)],
        scratch_shapes=[pltpu.VMEM((N_GROUPS, HEAD, HEAD), F32)],
        args=(q, k, v, gates, g_rows), compiler_params=_params(("arbitrary",)))


def _delta_bwd(q, k, v, gates, g_rows, states, t_inv, do, acc, *, reverse, comm=None):
    s = q.shape[1]
    nc = s // DN_CHUNK
    g_lane, b_lane = _delta_lanes(reverse)
    chunk_of = (lambda c: c) if reverse else (lambda c: nc - 1 - c)
    n_acc = 0 if acc is None else 3
    heads = range(N_GROUPS)

    def body(*refs):
        q_ref, k_ref, v_ref, gate_ref, grow_ref, st_ref, t_ref, do_ref = refs[:8]
        acc_refs = refs[8:8 + n_acc]
        dq_ref, dk_ref, dv_ref, dgate_ref, dgrow_ref, dstate = refs[8 + n_acc:]

        @pl.when(pl.program_id(0) == 0)
        def _():
            dstate[...] = jnp.zeros_like(dstate)

        gb = gate_ref[...]
        t_known = tuple(t_ref[0, j] for j in heads)
        f = lambda q_, k_, v_, gc_, gr_, b_, s_: _delta_chunk(q_, k_, v_, gc_, gr_, b_, s_, t_known, reverse)[:2]
        _, vjp = jax.vjp(f, tuple(q_ref[j] for j in heads), tuple(k_ref[j] for j in heads), tuple(v_ref[j] for j in heads),
                         tuple(gb[:, g_lane + j:g_lane + j + 1] for j in heads), tuple(grow_ref[0, j:j + 1, :] for j in heads),
                         tuple(gb[:, b_lane + j:b_lane + j + 1] for j in heads), tuple(st_ref[0, j] for j in heads))
        dq, dk, dv, dg_col, dg_row, dbeta, dprev = vjp((tuple(do_ref[j] for j in heads), tuple(dstate[j] for j in heads)))
        ids = lax.broadcasted_iota(jnp.int32, (DN_CHUNK, LANE), 1)
        dgate = jnp.zeros((DN_CHUNK, LANE), F32)
        for j in heads:
            if n_acc:
                dq_ref[j] = dq[j] + acc_refs[0][j]
                dk_ref[j] = dk[j] + acc_refs[1][j]
                dv_ref[j] = dv[j] + acc_refs[2][j]
            else:
                dq_ref[j] = dq[j]
                dk_ref[j] = dk[j]
                dv_ref[j] = dv[j]
            dstate[j] = dprev[j]
            dgate = dgate + jnp.where(ids == g_lane + j, dg_col[j], 0.0) + jnp.where(ids == b_lane + j, dbeta[j], 0.0)
            dgrow_ref[0, j:j + 1, :] = dg_row[j]
        dgate_ref[...] = dgate

    hm = pl.BlockSpec((N_GROUPS, DN_CHUNK, HEAD), lambda c: (0, chunk_of(c), 0))
    gate = pl.BlockSpec((DN_CHUNK, LANE), lambda c: (chunk_of(c), 0))
    rows = pl.BlockSpec((1, N_GROUPS, DN_CHUNK), lambda c: (chunk_of(c), 0, 0))
    st = pl.BlockSpec((1, N_GROUPS, HEAD, HEAD), lambda c: (chunk_of(c), 0, 0, 0))
    ts = pl.BlockSpec((1, N_GROUPS, DN_CHUNK, DN_CHUNK), lambda c: (chunk_of(c), 0, 0, 0))
    hm_shape = jax.ShapeDtypeStruct((N_GROUPS, s, HEAD), F32)
    name = ("delta_bwd_rev" if reverse else "delta_bwd") + ("" if comm is None else "_comm")
    return _call_with_comm(
        body, comm, name=name, grid=(nc,),
        in_specs=[hm, hm, hm, gate, rows, st, ts, hm] + [hm] * n_acc,
        out_specs=[hm, hm, hm, gate, rows],
        out_shape=[hm_shape] * 3 + [jax.ShapeDtypeStruct((s, LANE), F32), jax.ShapeDtypeStruct((nc, N_GROUPS, DN_CHUNK), F32)],
        scratch_shapes=[pltpu.VMEM((N_GROUPS, HEAD, HEAD), F32)],
        args=(q, k, v, gates, g_rows, states, t_inv, do, *(acc or ())), compiler_params=_params(("arbitrary",)))


ZB_BLOCK = 6


def _post_head(o_f, o_b, z, w):
    return _rms(o_f + o_b, w) * _silu(z)


def _post_fwd(o_f, o_b, proj, w, y, *, tile=256):
    s = proj.shape[0]
    tile = min(tile, s)

    def body(of_ref, ob_ref, z_ref, w_ref, _, y_ref):
        for h in range(N_GROUPS):
            lanes = pl.ds(h * HEAD, HEAD)
            y_ref[:, lanes] = _post_head(of_ref[h], ob_ref[h], z_ref[:, lanes], w_ref[...]).astype(BF16)

    hm = pl.BlockSpec((N_GROUPS, tile, HEAD), lambda i: (0, i, 0))
    return pl.pallas_call(
        body, name="post_fwd", grid=(s // tile,),
        in_specs=[hm, hm, pl.BlockSpec((tile, D_DN), lambda i: (i, ZB_BLOCK)), pl.BlockSpec((1, HEAD), lambda i: (0, 0)),
                  pl.BlockSpec(memory_space=pl.ANY)],
        out_specs=pl.BlockSpec((tile, D_DN), lambda i: (i, 1)),
        out_shape=jax.ShapeDtypeStruct(y.shape, BF16), input_output_aliases={4: 0},
        compiler_params=_params(("arbitrary",)),
    )(o_f, o_b, proj, w, y)


def _post_bwd(o_f, o_b, proj, w, dy, *, tile=256):
    s = proj.shape[0]
    tile = min(tile, s)

    def body(of_ref, ob_ref, z_ref, w_ref, dy_ref, do_ref, dp_ref, dw_ref):
        @pl.when(pl.program_id(0) == 0)
        def _():
            dw_ref[...] = jnp.zeros_like(dw_ref)

        for h in range(N_GROUPS):
            lanes = pl.ds(h * HEAD, HEAD)
            _, vjp = jax.vjp(_post_head, of_ref[h], ob_ref[h], z_ref[:, lanes], w_ref[...])
            do, _, dz, dw = vjp(dy_ref[:, lanes])
            do_ref[h] = do
            dp_ref[:, lanes] = dz.astype(BF16)
            dw_ref[...] += dw

    hm = pl.BlockSpec((N_GROUPS, tile, HEAD), lambda i: (0, i, 0))
    zb = pl.BlockSpec((tile, D_DN), lambda i: (i, ZB_BLOCK))
    vec = pl.BlockSpec((1, HEAD), lambda i: (0, 0))
    return pl.pallas_call(
        body, name="post_bwd", grid=(s // tile,),
        in_specs=[hm, hm, zb, vec, pl.BlockSpec((tile, D_DN), lambda i: (i, 1))], out_specs=[hm, zb, vec],
        out_shape=[jax.ShapeDtypeStruct((N_GROUPS, s, HEAD), F32), jax.ShapeDtypeStruct((s, D_MAIN), BF16),
                   jax.ShapeDtypeStruct((1, HEAD), F32)],
        compiler_params=_params(("arbitrary",)),
    )(o_f, o_b, proj, w, dy)


def _my_index():
    return 4 * lax.axis_index("x") + 2 * lax.axis_index("y") + lax.axis_index("c")


def _peer(k):
    x, y, c = lax.axis_index("x"), lax.axis_index("y"), lax.axis_index("c")
    px, py, pc = x ^ ((k >> 2) & 1), y ^ ((k >> 1) & 1), c ^ (k & 1)
    return (px, py, pc), 4 * px + 2 * py + pc


class _Comm:
    def __init__(self, inputs, outputs, copies):
        self.inputs, self.outputs, self.copies = list(inputs), list(outputs), list(copies)
        self.kept = [o for o in self.outputs if not isinstance(o, jax.ShapeDtypeStruct)]

    def out_shapes(self):
        return [jax.ShapeDtypeStruct(o.shape, o.dtype) for o in self.outputs]

    def aliases(self, first_kept_input, first_output):
        kept_at = [i for i, o in enumerate(self.outputs) if not isinstance(o, jax.ShapeDtypeStruct)]
        return {first_kept_input + n: first_output + i for n, i in enumerate(kept_at)}

    def scratch(self):
        n = len(self.copies)
        return [pltpu.SemaphoreType.DMA((n, N_DEV - 1)), pltpu.SemaphoreType.DMA((n, N_DEV - 1)), pltpu.SemaphoreType.DMA((n,))]

    def _descriptors(self, in_refs, out_refs, sems, arrivals):
        send_sems, recv_sems, local_sems = sems
        me = _my_index()
        local, remote = [], []
        for k in range(N_DEV):
            peer, peer_idx = _peer(k)
            for a, (i_in, i_out, src_of, dst_of) in enumerate(self.copies):
                if k == 0:
                    if not arrivals:
                        local.append(pltpu.make_async_copy(src_of(in_refs[i_in], me), dst_of(out_refs[i_out], me), local_sems.at[a]))
                    continue
                receiver, sender = (me, peer_idx) if arrivals else (peer_idx, me)
                remote.append(pltpu.make_async_remote_copy(
                    src_ref=src_of(in_refs[i_in], receiver), dst_ref=dst_of(out_refs[i_out], sender),
                    send_sem=send_sems.at[a, k - 1], recv_sem=recv_sems.at[a, k - 1], device_id=peer,
                    device_id_type=pl.DeviceIdType.MESH))
        return local, remote

    def start(self, in_refs, out_refs, sems):
        local, sends = self._descriptors(in_refs, out_refs, sems, arrivals=False)
        for cp in local + sends:
            cp.start()

    def wait(self, in_refs, out_refs, sems):
        _, arriving = self._descriptors(in_refs, out_refs, sems, arrivals=True)
        for cp in arriving:
            cp.wait_recv()
        local, sends = self._descriptors(in_refs, out_refs, sems, arrivals=False)
        for cp in sends:
            cp.wait_send()
        for cp in local:
            cp.wait()


def _call_with_comm(body, comm, *, name, grid, in_specs, out_specs, out_shape, scratch_shapes, args, compiler_params):
    if comm is None:
        return pl.pallas_call(body, name=name, grid=grid, in_specs=in_specs, out_specs=out_specs, out_shape=out_shape,
                              scratch_shapes=scratch_shapes, compiler_params=compiler_params)(*args), []
    n_in, n_out, n_scratch = len(in_specs), len(out_specs), len(scratch_shapes)
    c_in, c_kept, c_out = len(comm.inputs), len(comm.kept), len(comm.outputs)
    last = grid[0] - 1

    def wrapped(*refs):
        main_in, comm_in = refs[:n_in], refs[n_in:n_in + c_in]
        o = n_in + c_in + c_kept
        main_out, comm_out = refs[o:o + n_out], refs[o + n_out:o + n_out + c_out]
        s = o + n_out + c_out
        main_scratch, sems = refs[s:s + n_scratch], refs[s + n_scratch:]

        @pl.when(pl.program_id(0) == 0)
        def _():
            comm.start(comm_in, comm_out, sems)

        body(*main_in, *main_out, *main_scratch)

        @pl.when(pl.program_id(0) == last)
        def _():
            comm.wait(comm_in, comm_out, sems)

    any_spec = pl.BlockSpec(memory_space=pl.ANY)
    results = pl.pallas_call(
        wrapped, name=name, grid=grid, in_specs=list(in_specs) + [any_spec] * (c_in + c_kept),
        out_specs=list(out_specs) + [any_spec] * c_out, out_shape=list(out_shape) + comm.out_shapes(),
        scratch_shapes=list(scratch_shapes) + comm.scratch(), input_output_aliases=comm.aliases(n_in + c_in, n_out),
        compiler_params=compiler_params,
    )(*args, *comm.inputs, *comm.kept)
    return results[:n_out], results[n_out:]


def _exchange(name, comm):
    c_in, c_kept, c_out = len(comm.inputs), len(comm.kept), len(comm.outputs)

    def body(*refs):
        in_refs, out_refs, sems = refs[:c_in], refs[c_in + c_kept:c_in + c_kept + c_out], refs[c_in + c_kept + c_out:]
        comm.start(in_refs, out_refs, sems)
        comm.wait(in_refs, out_refs, sems)

    any_spec = pl.BlockSpec(memory_space=pl.ANY)
    return pl.pallas_call(
        body, name=name, in_specs=[any_spec] * (c_in + c_kept), out_specs=[any_spec] * c_out, out_shape=comm.out_shapes(),
        scratch_shapes=comm.scratch(), input_output_aliases=comm.aliases(c_in, 0),
    )(*comm.inputs, *comm.kept)


def _whole(ref, _):
    return ref


def _slot(ref, dev):
    return ref.at[dev]


def _row_block(size, axis):
    def pick(ref, dev):
        start = pl.multiple_of(dev * size, size) if size % SUBLANES == 0 else dev * size
        return ref.at[tuple([slice(None)] * axis + [pl.ds(start, size)])]
    return pick


def _adamw(name, parts, w, m, v, block):
    nd = len(block)
    grid = tuple(w.shape[d] // block[d] for d in range(nd) if block[d] != w.shape[d])
    lead = len(grid)

    def body(p_ref, w_ref, m_ref, v_ref, g_ref, d_ref, mo_ref, vo_ref):
        g = p_ref[0].astype(F32)
        for j in range(1, N_DEV):
            g = g + p_ref[j].astype(F32)
        m_new = ADAM_B1 * m_ref[...] + (1.0 - ADAM_B1) * g
        v_new = ADAM_B2 * v_ref[...] + (1.0 - ADAM_B2) * jnp.square(g)
        m_hat = m_new / np.float32(1.0 - ADAM_B1 ** ADAM_STEP)
        v_hat = v_new / np.float32(1.0 - ADAM_B2 ** ADAM_STEP)
        g_ref[...] = g
        d_ref[...] = -ADAM_LR * (m_hat / (jnp.sqrt(v_hat) + ADAM_EPS) + ADAM_WD * w_ref[...])
        mo_ref[...] = m_new
        vo_ref[...] = v_new

    idx = lambda *i: tuple(i) + (0,) * (nd - lead)
    spec = pl.BlockSpec(block, idx)
    pspec = pl.BlockSpec((N_DEV,) + tuple(block), lambda *i: (0,) + idx(*i))
    n_blk = int(np.prod(block))
    return pl.pallas_call(
        body, name=name, grid=grid, in_specs=[pspec, spec, spec, spec], out_specs=[spec] * 4,
        out_shape=[jax.ShapeDtypeStruct(w.shape, F32)] * 4,
        compiler_params=_params(("arbitrary",) * lead, 2 * (N_DEV + 7) * n_blk * 4 + VMEM_MARGIN),
    )(parts, w, m, v)


def _to_flat(pieces, row_multiple):
    flat = jnp.concatenate([p.reshape(-1) for p in pieces])
    rows = -(-flat.shape[0] // FLAT_W)
    rows = -(-rows // row_multiple) * row_multiple
    return jnp.pad(flat, (0, rows * FLAT_W - flat.shape[0])).reshape(rows, FLAT_W)


def _lane_row(*pieces):
    flat = jnp.concatenate([p.reshape(-1) for p in pieces])
    return jnp.pad(flat, (0, LANE - flat.shape[0])).reshape(1, LANE)


def _rows_view(a):
    return a.reshape(a.shape[:-1] + (SUBLANES, LANE))


def _in_shard_t(a):
    return _rows_view(jnp.swapaxes(a, 1, 2))


def kernel(x, norm_w, w_in, sgu_ln_g, sgu_ln_b, sgu_w, sgu_b, conv_w, a_log_f, a_log_b, dt_bias_f, dt_bias_b, dn_norm_w, w_out, final_norm_w, loss_target, m_norm_w, m_w_in, m_sgu_ln_g, m_sgu_ln_b, m_sgu_w, m_sgu_b, m_conv_w, m_a_log_f, m_a_log_b, m_dt_bias_f, m_dt_bias_b, m_dn_norm_w, m_w_out, m_final_norm_w, v_norm_w, v_w_in, v_sgu_ln_g, v_sgu_ln_b, v_sgu_w, v_sgu_b, v_conv_w, v_a_log_f, v_a_log_b, v_dt_bias_f, v_dt_bias_b, v_dn_norm_w, v_w_out, v_final_norm_w):
    weights = dict(norm_w=norm_w, w_in=w_in, sgu_ln_g=sgu_ln_g, sgu_ln_b=sgu_ln_b, sgu_w=sgu_w, sgu_b=sgu_b, conv_w=conv_w,
                   a_log_f=a_log_f, a_log_b=a_log_b, dt_bias_f=dt_bias_f, dt_bias_b=dt_bias_b, dn_norm_w=dn_norm_w,
                   w_out=w_out, final_norm_w=final_norm_w)
    m_in = dict(norm_w=m_norm_w, w_in=m_w_in, sgu_ln_g=m_sgu_ln_g, sgu_ln_b=m_sgu_ln_b, sgu_w=m_sgu_w, sgu_b=m_sgu_b,
                conv_w=m_conv_w, a_log_f=m_a_log_f, a_log_b=m_a_log_b, dt_bias_f=m_dt_bias_f, dt_bias_b=m_dt_bias_b,
                dn_norm_w=m_dn_norm_w, w_out=m_w_out, final_norm_w=m_final_norm_w)
    v_in = dict(norm_w=v_norm_w, w_in=v_w_in, sgu_ln_g=v_sgu_ln_g, sgu_ln_b=v_sgu_ln_b, sgu_w=v_sgu_w, sgu_b=v_sgu_b,
                conv_w=v_conv_w, a_log_f=v_a_log_f, a_log_b=v_a_log_b, dt_bias_f=v_dt_bias_f, dt_bias_b=v_dt_bias_b,
                dn_norm_w=v_dn_norm_w, w_out=v_w_out, final_norm_w=v_final_norm_w)
    s = x.shape[1]
    nc = s // DN_CHUNK
    depth = w_in.shape[0]
    in_shard, out_shard, conv_shard = w_in.shape[2], w_out.shape[1], conv_w.shape[2]
    x0 = x.reshape(s, D_MODEL)
    target = loss_target.reshape(s, D_MODEL)

    w_in_shard = _in_shard_t(w_in.astype(BF16))
    w_out_shard = w_out.astype(BF16)

    def gather_plan(l, more_inputs=(), more_outputs=(), more_copies=()):
        return _Comm([w_in_shard[l], w_out_shard[l], *more_inputs],
                     [jax.ShapeDtypeStruct((N_DEV * in_shard, SUBLANES, LANE), BF16),
                      jax.ShapeDtypeStruct((D_MODEL, D_MODEL), BF16), *more_outputs],
                     [(0, 0, _whole, _row_block(in_shard, 0)), (1, 1, _whole, _row_block(out_shard, 0)), *more_copies])

    gathered_in, gathered_out, conv_all = _exchange(
        "gather_first", gather_plan(0, [conv_w], [jax.ShapeDtypeStruct((N_DEV,) + conv_w.shape, F32)], [(2, 2, _whole, _slot)]))
    conv_full = conv_all.transpose(1, 2, 0, 3).reshape(depth, CONV_W, 3 * D_DN)
    conv_full = jnp.pad(conv_full, ((0, 0), (0, HALO - CONV_W), (0, 0)))
    sgu_b_t = jnp.swapaxes(sgu_b, 1, 2)

    def rows_of(gates, first_lane):
        return gates[:, first_lane:first_lane + N_GROUPS].reshape(nc, DN_CHUNK, N_GROUPS).transpose(0, 2, 1)

    saved = []
    xl = x0
    for l in range(depth):
        w_in_t = gathered_in.reshape(D_IN, D_MODEL)
        w_gate_t = jnp.pad(w_in_t[D_MAIN:], ((0, LANE - N_GATE), (0, 0)))
        w_out_l = gathered_out
        nw = norm_w[l].reshape(1, D_MODEL)
        h = _rms_fwd(xl, nw)
        proj = _matmul([(_lhs(h), _rhs(w_in_t, MM_TILE, transposed=True))], s, D_MAIN, name="in_proj", trans_b=True)
        p_gate = _matmul([(_lhs(h), _rhs(w_gate_t, LANE, transposed=True))], s, LANE, name="in_proj_gate", trans_b=True, tn=LANE)
        ln_g, ln_b = sgu_ln_g[l].reshape(1, D_SGU), sgu_ln_b[l].reshape(1, D_SGU)
        y = _sgu_fwd(proj, ln_g, ln_b, sgu_w[l], sgu_b_t[l])
        a_log = _lane_row(a_log_f[l], a_log_b[l])
        dt_bias = _lane_row(dt_bias_f[l], dt_bias_b[l])
        q, k, v, gates = _prep_fwd(proj, conv_full[l], p_gate, a_log, dt_bias)
        rows_f, rows_b = rows_of(gates, 0), rows_of(gates, N_GROUPS)
        (o_f, st_f, t_f), next_weights = _delta_fwd(q, k, v, gates, rows_f, reverse=False,
                                                    comm=gather_plan(l + 1) if l + 1 < depth else None)
        (o_b, st_b, t_b), _ = _delta_fwd(q, k, v, gates, rows_b, reverse=True)
        dn_w = dn_norm_w[l].reshape(1, HEAD)
        y = _post_fwd(o_f, o_b, proj, dn_w, y)
        x_next = _matmul([(_lhs(y), _rhs(w_out_l, MM_TILE))], s, D_MODEL, name="out_proj", add=xl)
        saved.append(dict(x=xl, nw=nw, h=h, proj=proj, p_gate=p_gate, ln_g=ln_g, ln_b=ln_b, a_log=a_log, dt_bias=dt_bias,
                          q=q, k=k, v=v, gates=gates, rows_f=rows_f, rows_b=rows_b, o_f=o_f, o_b=o_b, st_f=st_f, st_b=st_b,
                          t_f=t_f, t_b=t_b, dn_w=dn_w, y=y, w_in_t=w_in_t, w_gate_t=w_gate_t, w_out=w_out_l))
        xl = x_next
        if next_weights:
            gathered_in, gathered_out = next_weights

    loss_part, dx, dx_bf, d_final = _final_loss(xl, final_norm_w.reshape(1, D_MODEL), target)
    loss = lax.psum(loss_part[0, 0], MESH_AXES)

    small_names = [n for n in SMALL if n not in ("conv_w", "final_norm_w")]
    grads = {n: [None] * depth for n in small_names + ["conv_w"]}
    t_rows = -(-D_IN // LANE) * LANE
    layer_slot = lambda l: (lambda ref, dev: ref.at[dev, l])
    parts = [jax.ShapeDtypeStruct((N_DEV, depth, in_shard, SUBLANES, LANE), BF16),
             jax.ShapeDtypeStruct((N_DEV, depth, out_shard, D_MODEL), BF16)]

    def exchange_plan(layer, g_in_rows, g_out, more_inputs=(), more_outputs=(), more_copies=()):
        return _Comm([g_in_rows, g_out, *more_inputs], [*parts, *more_outputs],
                     [(0, 0, _row_block(in_shard, 0), layer_slot(layer)), (1, 1, _row_block(out_shard, 0), layer_slot(layer)),
                      *more_copies])

    pending = None
    for l in reversed(range(depth)):
        sv = saved[l]
        dy = _matmul([(_lhs(dx_bf), _rhs(sv["w_out"], MM_TILE, transposed=True))], s, D_MODEL, name="out_proj_dy", trans_b=True)
        g_out = _matmul([(_lhs(sv["y"].T), _rhs(dx_bf, MM_TILE))], D_MODEL, D_MODEL, name="out_proj_dw", out_dtype=BF16)
        do, dp, d_dn = _post_bwd(sv["o_f"], sv["o_b"], sv["proj"], sv["dn_w"], dy)
        (dq, dk, dv, dgate_f, drows_f), exchanged = _delta_bwd(
            sv["q"], sv["k"], sv["v"], sv["gates"], sv["rows_f"], sv["st_f"], sv["t_f"], do, None, reverse=False,
            comm=None if pending is None else exchange_plan(*pending))
        if exchanged:
            parts = list(exchanged)
        (dq, dk, dv, dgate_b, drows_b), _ = _delta_bwd(sv["q"], sv["k"], sv["v"], sv["gates"], sv["rows_b"], sv["st_b"], sv["t_b"],
                                                       do, (dq, dk, dv), reverse=True)
        drows = jnp.concatenate([drows_f.transpose(0, 2, 1).reshape(s, N_GROUPS), drows_b.transpose(0, 2, 1).reshape(s, N_GROUPS)], axis=1)
        drows = jnp.pad(drows, ((0, 0), (0, LANE - 2 * N_GROUPS)))
        dp, d_gate, d_conv, d_alog, d_dt = _prep_bwd(sv["proj"], conv_full[l], sv["p_gate"], sv["a_log"], sv["dt_bias"],
                                                     dq, dk, dv, (dgate_f, dgate_b, drows), dp)
        dp, d_lg, d_lb, d_sw, d_sbt = _sgu_bwd(sv["proj"], sv["ln_g"], sv["ln_b"], sgu_w[l], sgu_b_t[l], dy, dp)
        dh = _matmul([(_lhs(dp), _rhs(sv["w_in_t"], MM_TILE, k=D_MAIN)), (_lhs(d_gate), _rhs(sv["w_gate_t"], MM_TILE))],
                     s, D_MODEL, name="in_proj_dh")
        h_t = sv["h"].T
        g_in_t = _matmul([(_lhs(h_t), _rhs(dp, MM_TILE))], D_MODEL, D_MAIN, name="in_proj_dw", out_dtype=BF16, out_t_rows=t_rows)
        g_in_t = _matmul([(_lhs(h_t), _rhs(d_gate, LANE))], D_MODEL, LANE, name="in_proj_dw_gate", out_dtype=BF16, tn=LANE,
                         transpose_into=g_in_t, out_row_block=D_MAIN // LANE)
        pending = (l, _rows_view(g_in_t), g_out)
        dx, dx_bf, d_nw = _rms_bwd(sv["x"], sv["nw"], dh, dx)
        grads["norm_w"][l] = d_nw.reshape(D_MODEL)
        grads["sgu_ln_g"][l], grads["sgu_ln_b"][l] = d_lg.reshape(D_SGU), d_lb.reshape(D_SGU)
        grads["sgu_w"][l], grads["sgu_b"][l] = d_sw, d_sbt.T
        grads["conv_w"][l] = d_conv[:CONV_W]
        grads["a_log_f"][l], grads["a_log_b"][l] = d_alog[0, :N_GROUPS], d_alog[0, N_GROUPS:2 * N_GROUPS]
        grads["dt_bias_f"][l], grads["dt_bias_b"][l] = d_dt[0, :N_GROUPS], d_dt[0, N_GROUPS:2 * N_GROUPS]
        grads["dn_norm_w"][l] = d_dn.reshape(HEAD)
    grad_x = dx.reshape(1, s, D_MODEL)

    g_small = {n: jnp.stack(grads[n]) for n in small_names}
    g_small["final_norm_w"] = d_final.reshape(D_MODEL)
    conv_by_dev = jnp.stack(grads["conv_w"]).reshape(depth, CONV_W, N_DEV, conv_shard).transpose(2, 0, 1, 3).reshape(N_DEV, -1)
    replicated = jnp.concatenate([g_small[n].reshape(-1) for n in SMALL if n != "conv_w"])
    n_small = conv_by_dev.shape[1] + replicated.shape[0]
    small_rows = -(-n_small // (FLAT_W * HALO)) * HALO
    send_small = jnp.concatenate([conv_by_dev, jnp.broadcast_to(replicated, (N_DEV, replicated.shape[0])),
                                  jnp.zeros((N_DEV, small_rows * FLAT_W - n_small), F32)], axis=1).reshape(N_DEV, small_rows, FLAT_W)
    parts_in, parts_out, parts_small = _exchange(
        "exchange_last", exchange_plan(*pending, [send_small], [jax.ShapeDtypeStruct((N_DEV, small_rows, FLAT_W), F32)],
                                       [(2, 2, _slot, _slot)]))

    in_rows = 100 if in_shard % 100 == 0 else in_shard
    res_in = _adamw("adamw_in", parts_in, _in_shard_t(w_in), _in_shard_t(m_w_in), _in_shard_t(v_w_in),
                    (1, in_rows, SUBLANES, LANE))
    res_in = [jnp.swapaxes(r.reshape(depth, in_shard, D_MODEL), 1, 2) for r in res_in]
    res_out = _adamw("adamw_out", parts_out, w_out, m_w_out, v_w_out, (1, min(128, out_shard), D_MODEL))
    res_small = _adamw("adamw_small", parts_small, _to_flat([weights[n] for n in SMALL], HALO), _to_flat([m_in[n] for n in SMALL], HALO),
                       _to_flat([v_in[n] for n in SMALL], HALO), (small_rows // 4 if small_rows % 32 == 0 else small_rows, FLAT_W))
    outs = [dict(w_in=a, w_out=b) for a, b in zip(res_in, res_out)]
    off = 0
    for n in SMALL:
        size = int(np.prod(weights[n].shape))
        for o, d in zip(res_small, outs):
            d[n] = o.reshape(-1)[off:off + size].reshape(weights[n].shape)
        off += size
    g_out, delta_out, m_out, v_out = outs
    return (loss, grad_x, *[g_out[n] for n in WEIGHTS], *[delta_out[n] for n in WEIGHTS],
            *[m_out[n] for n in WEIGHTS], *[v_out[n] for n in WEIGHTS])
```

```python
import numpy as np
import jax
import jax.numpy as jnp
from jax import lax
from jax.experimental import pallas as pl
from jax.experimental.pallas import tpu as pltpu

F32 = jnp.float32
BF16 = jnp.bfloat16

N_DEV = 8
D_MODEL = 2048
D_SGU = 1024
D_DN = 1024
N_GROUPS = 8
HEAD = 128
SGU_CHUNK = 128
DN_CHUNK = 64
CONV_W = 5
N_GATE = 32
D_MAIN = 3 * D_SGU + 4 * D_DN
D_IN = D_MAIN + N_GATE
LANE = 128
SUBLANES = 16
HALO = 8
EPS = 1e-6
ADAM_LR, ADAM_B1, ADAM_B2, ADAM_EPS, ADAM_WD, ADAM_STEP = 0.001, 0.9, 0.999, 1e-08, 0.01, 10
FLAT_W = 1024
MM_TILE = 512
MM_WIDE = 1024
VMEM_MARGIN = 8 << 20

MESH_AXES = ("x", "y", "c")
WEIGHTS = ("norm_w", "w_in", "sgu_ln_g", "sgu_ln_b", "sgu_w", "sgu_b", "conv_w", "a_log_f", "a_log_b",
           "dt_bias_f", "dt_bias_b", "dn_norm_w", "w_out", "final_norm_w")
SMALL = ("conv_w",) + tuple(n for n in WEIGHTS if n not in ("w_in", "w_out", "conv_w"))


def _bdot(a, b, dims):
    return lax.dot_general(a.astype(BF16), b.astype(BF16), (dims, ((), ())), preferred_element_type=F32)


@jax.custom_vjp
def _mm_nn(a, b):
    return _bdot(a, b, ((1,), (0,)))


def _mm_nn_fwd(a, b):
    return _mm_nn(a, b), (a, b)


def _mm_nn_bwd(res, ct):
    a, b = res
    return _bdot(ct, b, ((1,), (1,))), _bdot(a, ct, ((0,), (0,)))


_mm_nn.defvjp(_mm_nn_fwd, _mm_nn_bwd)


@jax.custom_vjp
def _mm_nt(a, b):
    return _bdot(a, b, ((1,), (1,)))


def _mm_nt_fwd(a, b):
    return _mm_nt(a, b), (a, b)


def _mm_nt_bwd(res, ct):
    a, b = res
    return _bdot(ct, b, ((1,), (0,))), _bdot(ct, a, ((0,), (0,)))


_mm_nt.defvjp(_mm_nt_fwd, _mm_nt_bwd)


@jax.custom_vjp
def _mm_tn(a, b):
    return _bdot(a, b, ((0,), (0,)))


def _mm_tn_fwd(a, b):
    return _mm_tn(a, b), (a, b)


def _mm_tn_bwd(res, ct):
    a, b = res
    return _bdot(b, ct, ((1,), (1,))), _bdot(a, ct, ((1,), (0,)))


_mm_tn.defvjp(_mm_tn_fwd, _mm_tn_bwd)


def _split_hi_lo(a):
    hi = a.astype(BF16)
    return hi, (a - hi.astype(F32)).astype(BF16)


def _dot3(a, b, dims=((1,), (0,))):
    (ah, al), (bh, bl) = _split_hi_lo(a), _split_hi_lo(b)
    d = lambda x, y: lax.dot_general(x, y, (dims, ((), ())), preferred_element_type=F32)
    return d(ah, bh) + (d(ah, bl) + d(al, bh))


@jax.custom_vjp
def _inverse_given(m, t):
    return t


def _inverse_given_fwd(m, t):
    return t, t


def _inverse_given_bwd(t, ct):
    dm = -_dot3(t, _dot3(ct, t, ((1,), (1,))), ((0,), (0,)))
    return dm, jnp.zeros_like(t)


_inverse_given.defvjp(_inverse_given_fwd, _inverse_given_bwd)


@jax.custom_vjp
def _split_lanes(x):
    return tuple(x[:, i * LANE:(i + 1) * LANE] for i in range(x.shape[1] // LANE))


def _split_lanes_fwd(x):
    return _split_lanes(x), None


def _split_lanes_bwd(_, cts):
    return (jnp.concatenate(cts, axis=1),)


_split_lanes.defvjp(_split_lanes_fwd, _split_lanes_bwd)


def _silu(t):
    return t * jax.nn.sigmoid(t)


def _gelu(t):
    return 0.5 * t * (1.0 + lax.erf(t * np.float32(0.7071067811865476)))


def _rms(x, w):
    return x * lax.rsqrt(jnp.mean(x * x, axis=-1, keepdims=True) + EPS) * w


def _params(sem, vmem_bytes=None):
    kw = dict(dimension_semantics=sem)
    if vmem_bytes is not None:
        kw["vmem_limit_bytes"] = int(vmem_bytes)
    return pltpu.CompilerParams(**kw)


def _lhs(a, k=None, k_block=0):
    k = a.shape[1] if k is None else k
    tm = min(MM_TILE, a.shape[0])
    return a, pl.BlockSpec((tm, k), lambda i, j: (i, k_block)), tm * k * a.dtype.itemsize


def _rhs(b, tn, *, layer=None, transposed=False, k=None, k_block=0, n_offset=0):
    lead = () if layer is None else (layer,)
    none = () if layer is None else (None,)
    if transposed:
        k = b.shape[-1]
        spec = pl.BlockSpec(none + (tn, k), lambda i, j: lead + (j + n_offset, 0))
    else:
        k = b.shape[-2] if k is None else k
        spec = pl.BlockSpec(none + (k, tn), lambda i, j: lead + (k_block, j))
    return b, spec, tn * k * b.dtype.itemsize


def _matmul(pairs, m, n, *, name, trans_b=False, add=None, out_dtype=F32, tn=MM_TILE, out_t_rows=None, transpose_into=None,
            out_row_block=0):
    tm = min(MM_TILE, m)
    assert m % tm == 0 and n % tn == 0
    n_pairs = len(pairs)
    contract = ((1,), (1,)) if trans_b else ((1,), (0,))
    transposed_out = out_t_rows is not None or transpose_into is not None

    def body(*refs):
        o_ref = refs[-1]
        acc = None
        for i in range(n_pairs):
            d = lax.dot_general(refs[2 * i][...].astype(BF16), refs[2 * i + 1][...].astype(BF16),
                                (contract, ((), ())), preferred_element_type=F32)
            acc = d if acc is None else acc + d
        if add is not None:
            acc = acc + refs[2 * n_pairs][...]
        o_ref[...] = (acc.T if transposed_out else acc).astype(out_dtype)

    in_specs, args, vmem, aliases = [], [], 0, {}
    for (a, a_spec, a_bytes), (b, b_spec, b_bytes) in pairs:
        in_specs += [a_spec, b_spec]
        args += [a, b]
        vmem += 2 * (a_bytes + b_bytes)
    if add is not None:
        in_specs.append(pl.BlockSpec((tm, tn), lambda i, j: (i, j)))
        args.append(add)
        vmem += 2 * tm * tn * 4
    vmem += 2 * tm * tn * jnp.dtype(out_dtype).itemsize + 3 * tm * tn * 4
    if transposed_out:
        out_spec = pl.BlockSpec((tn, tm), lambda i, j: (j + out_row_block, i))
        if transpose_into is not None:
            in_specs.append(pl.BlockSpec(memory_space=pl.ANY))
            args.append(transpose_into)
            aliases = {len(args) - 1: 0}
            out_shape = jax.ShapeDtypeStruct(transpose_into.shape, out_dtype)
        else:
            out_shape = jax.ShapeDtypeStruct((out_t_rows, m), out_dtype)
    else:
        out_spec = pl.BlockSpec((tm, tn), lambda i, j: (i, j))
        out_shape = jax.ShapeDtypeStruct((m, n), out_dtype)
    return pl.pallas_call(
        body, name=name, grid=(m // tm, n // tn), in_specs=in_specs, out_specs=out_spec, out_shape=out_shape,
        input_output_aliases=aliases, compiler_params=_params(("parallel", "arbitrary"), vmem + VMEM_MARGIN),
    )(*args)


def _rms_fwd(x, w, *, tile=512):
    s, d = x.shape
    tile = min(tile, s)

    def body(x_ref, w_ref, h_ref, ht_ref):
        h = _rms(x_ref[...], w_ref[...])
        h_ref[...] = h.astype(BF16)
        ht_ref[...] = h.T.astype(BF16)

    return pl.pallas_call(
        body, name="rms_fwd", grid=(s // tile,),
        in_specs=[pl.BlockSpec((tile, d), lambda i: (i, 0)), pl.BlockSpec((1, d), lambda i: (0, 0))],
        out_specs=[pl.BlockSpec((tile, d), lambda i: (i, 0)), pl.BlockSpec((d, tile), lambda i: (0, i))],
        out_shape=[jax.ShapeDtypeStruct((s, d), BF16), jax.ShapeDtypeStruct((d, s), BF16)],
        compiler_params=_params(("arbitrary",)),
    )(x, w)


def _rms_bwd(x, w, dh, dx_out, *, tile=256):
    s, d = x.shape
    tile = min(tile, s)

    def body(x_ref, w_ref, dh_ref, dxo_ref, dx_ref, dxb_ref, dw_ref):
        _, vjp = jax.vjp(_rms, x_ref[...], w_ref[...])
        dxv, dwv = vjp(dh_ref[...])
        dx = dxo_ref[...] + dxv
        dx_ref[...] = dx
        dxb_ref[...] = dx.astype(BF16)

        @pl.when(pl.program_id(0) == 0)
        def _():
            dw_ref[...] = jnp.zeros_like(dw_ref)

        dw_ref[...] += dwv

    row = pl.BlockSpec((tile, d), lambda i: (i, 0))
    vec = pl.BlockSpec((1, d), lambda i: (0, 0))
    return pl.pallas_call(
        body, name="rms_bwd", grid=(s // tile,),
        in_specs=[row, vec, row, row], out_specs=[row, row, vec],
        out_shape=[jax.ShapeDtypeStruct((s, d), F32), jax.ShapeDtypeStruct((s, d), BF16), jax.ShapeDtypeStruct((1, d), F32)],
        compiler_params=_params(("arbitrary",)),
    )(x, w, dh, dx_out)


def _final_loss(x, w, target, *, tile=256):
    s, d = x.shape
    tile = min(tile, s)

    def body(x_ref, w_ref, t_ref, loss_ref, dx_ref, dxb_ref, dw_ref):
        def f(xv, wv):
            err = jnp.square(_rms(xv, wv) - t_ref[...])
            per_token = jnp.sum(err, axis=1, keepdims=True) * np.float32(1.0 / d)
            return 0.5 * jnp.sum(per_token, axis=0, keepdims=True)

        loss, vjp = jax.vjp(f, x_ref[...], w_ref[...])
        dxv, dwv = vjp(jnp.ones((1, 1), F32))
        dx_ref[...] = dxv
        dxb_ref[...] = dxv.astype(BF16)

        @pl.when(pl.program_id(0) == 0)
        def _():
            dw_ref[...] = jnp.zeros_like(dw_ref)
            loss_ref[...] = jnp.zeros_like(loss_ref)

        dw_ref[...] += dwv
        loss_ref[...] += jnp.broadcast_to(loss, (1, LANE))

    row = pl.BlockSpec((tile, d), lambda i: (i, 0))
    vec = pl.BlockSpec((1, d), lambda i: (0, 0))
    return pl.pallas_call(
        body, name="final_loss", grid=(s // tile,),
        in_specs=[row, vec, row], out_specs=[pl.BlockSpec((1, LANE), lambda i: (0, 0)), row, row, vec],
        out_shape=[jax.ShapeDtypeStruct((1, LANE), F32), jax.ShapeDtypeStruct((s, d), F32),
                   jax.ShapeDtypeStruct((s, d), BF16), jax.ShapeDtypeStruct((1, d), F32)],
        compiler_params=_params(("arbitrary",)),
    )(x, w, target)


def _sgu_chunk(u, v, z, ln_g, ln_b, ws, bcols):
    vg = _gelu(v)
    xc = vg - jnp.mean(vg, axis=-1, keepdims=True)
    vl = xc * lax.rsqrt(jnp.mean(xc * xc, axis=-1, keepdims=True) + EPS) * ln_g + ln_b
    sp = jnp.concatenate([_mm_nn(ws[g], vg_) + bcols[g] for g, vg_ in enumerate(_split_lanes(vl))], axis=1)
    return _gelu(u) * sp * _silu(z)


def _sgu_specs(tile):
    col = lambda j: pl.BlockSpec((tile, D_SGU), lambda i, j=j: (i, j))
    vec = pl.BlockSpec((1, D_SGU), lambda i: (0, 0))
    wspec = pl.BlockSpec((N_GROUPS, SGU_CHUNK, SGU_CHUNK), lambda i: (0, 0, 0))
    bspec = pl.BlockSpec((SGU_CHUNK, N_GROUPS), lambda i: (0, 0))
    return col, vec, wspec, bspec


def _sgu_fwd(proj, ln_g, ln_b, w_s, b_t, *, tile=256):
    s = proj.shape[0]
    tile = min(tile, s)
    col, vec, wspec, bspec = _sgu_specs(tile)

    def body(u_ref, v_ref, z_ref, g_ref, b_ref, w_ref, bt_ref, y_ref):
        ws = tuple(w_ref[g] for g in range(N_GROUPS))
        bcols = tuple(bt_ref[:, g:g + 1] for g in range(N_GROUPS))
        for c in range(tile // SGU_CHUNK):
            rows = pl.ds(c * SGU_CHUNK, SGU_CHUNK)
            y = _sgu_chunk(u_ref[rows, :], v_ref[rows, :], z_ref[rows, :], g_ref[...], b_ref[...], ws, bcols)
            y_ref[rows, :] = y.astype(BF16)

    return pl.pallas_call(
        body, name="sgu_fwd", grid=(s // tile,),
        in_specs=[col(0), col(1), col(2), vec, vec, wspec, bspec],
        out_specs=pl.BlockSpec((tile, D_SGU), lambda i: (i, 0)),
        out_shape=jax.ShapeDtypeStruct((s, D_SGU + D_DN), BF16),
        compiler_params=_params(("arbitrary",)),
    )(proj, proj, proj, ln_g, ln_b, w_s, b_t)


def _sgu_bwd(proj, ln_g, ln_b, w_s, b_t, dy, dp, *, tile=128):
    s = proj.shape[0]
    tile = min(tile, s)
    col, vec, wspec, bspec = _sgu_specs(tile)

    def body(u_ref, v_ref, z_ref, g_ref, b_ref, w_ref, bt_ref, dy_ref, _, dp_ref, dg_ref, db_ref, dw_ref, dbt_ref):
        @pl.when(pl.program_id(0) == 0)
        def _():
            dg_ref[...] = jnp.zeros_like(dg_ref)
            db_ref[...] = jnp.zeros_like(db_ref)
            dw_ref[...] = jnp.zeros_like(dw_ref)
            dbt_ref[...] = jnp.zeros_like(dbt_ref)

        ws = tuple(w_ref[g] for g in range(N_GROUPS))
        bcols = tuple(bt_ref[:, g:g + 1] for g in range(N_GROUPS))
        for c in range(tile // SGU_CHUNK):
            rows = pl.ds(c * SGU_CHUNK, SGU_CHUNK)
            _, vjp = jax.vjp(_sgu_chunk, u_ref[rows, :], v_ref[rows, :], z_ref[rows, :], g_ref[...], b_ref[...], ws, bcols)
            du, dv, dz, dg, db, dws, dbc = vjp(dy_ref[rows, :])
            dp_ref[rows, pl.ds(0, D_SGU)] = du.astype(BF16)
            dp_ref[rows, pl.ds(D_SGU, D_SGU)] = dv.astype(BF16)
            dp_ref[rows, pl.ds(2 * D_SGU, D_SGU)] = dz.astype(BF16)
            dg_ref[...] += dg
            db_ref[...] += db
            for g in range(N_GROUPS):
                dw_ref[g] += dws[g]
                dbt_ref[:, g:g + 1] += dbc[g]

    return pl.pallas_call(
        body, name="sgu_bwd", grid=(s // tile,),
        in_specs=[col(0), col(1), col(2), vec, vec, wspec, bspec, pl.BlockSpec((tile, D_SGU), lambda i: (i, 0)),
                  pl.BlockSpec(memory_space=pl.ANY)],
        out_specs=[pl.BlockSpec((tile, 3 * D_SGU), lambda i: (i, 0)), vec, vec, wspec, bspec],
        out_shape=[jax.ShapeDtypeStruct(dp.shape, BF16), jax.ShapeDtypeStruct((1, D_SGU), F32),
                   jax.ShapeDtypeStruct((1, D_SGU), F32), jax.ShapeDtypeStruct((N_GROUPS, SGU_CHUNK, SGU_CHUNK), F32),
                   jax.ShapeDtypeStruct((SGU_CHUNK, N_GROUPS), F32)],
        input_output_aliases={8: 0},
        compiler_params=_params(("arbitrary",)),
    )(proj, proj, proj, ln_g, ln_b, w_s, b_t, dy, dp)


QKV_BLOCK = 1


def _qkv_head(cq, ck, cv):
    q, k, v = _silu(cq), _silu(ck), _silu(cv)
    q = q * lax.rsqrt(jnp.sum(q * q, axis=-1, keepdims=True) + EPS) * np.float32(HEAD ** -0.5)
    k = k * lax.rsqrt(jnp.sum(k * k, axis=-1, keepdims=True) + EPS)
    return q, k, v


def _gate_fn(p, a_log, dt_bias):
    lane = lax.broadcasted_iota(jnp.int32, p.shape, 1)
    g = -jnp.exp(a_log) * jax.nn.softplus(p + dt_bias)
    return jnp.where(lane < 16, g, jnp.where(lane < N_GATE, jax.nn.sigmoid(p), 0.0))


def _shifted_rows(full, shift, start, rows):
    n = full.shape[0]
    assert start % HALO == 0 and 0 <= start + shift and start + shift + rows <= n
    rolled = full if shift == 0 else pltpu.roll(full, (-shift) % n, 0)
    return rolled[start:start + rows, :]


def _halo_maps(tile, s):
    per = tile // HALO
    prev = lambda i: jnp.maximum(i * per - 1, 0)
    nxt = lambda i: jnp.minimum((i + 1) * per, s // HALO - 1)
    return prev, nxt


def _prep_fwd(proj, conv_w, pg, a_log, dt_bias, *, tile=256):
    s = proj.shape[0]
    width = 3 * D_DN
    tile = min(tile, s)
    n_blk = s // tile
    prev, nxt = _halo_maps(tile, s)

    def body(x_ref, xp_ref, xn_ref, w_ref, pg_ref, al_ref, dt_ref, q_ref, k_ref, v_ref, gate_ref, ext):
        i = pl.program_id(0)
        ext[pl.ds(0, HALO), :] = jnp.where(i > 0, xp_ref[...], 0.0)
        ext[pl.ds(HALO, tile), :] = x_ref[...]
        ext[pl.ds(HALO + tile, HALO), :] = jnp.where(i < n_blk - 1, xn_ref[...], 0.0)
        full, c = ext[...], None
        for j in range(CONV_W):
            term = _shifted_rows(full, j - CONV_W // 2, HALO, tile) * w_ref[j:j + 1, :]
            c = term if c is None else c + term
        for h in range(N_GROUPS):
            q, k, v = _qkv_head(c[:, h * HEAD:(h + 1) * HEAD], c[:, D_DN + h * HEAD:D_DN + (h + 1) * HEAD],
                                c[:, 2 * D_DN + h * HEAD:2 * D_DN + (h + 1) * HEAD])
            q_ref[h] = q
            k_ref[h] = k
            v_ref[h] = v
        gate_ref[...] = _gate_fn(pg_ref[...], al_ref[...], dt_ref[...])

    hm = pl.BlockSpec((N_GROUPS, tile, HEAD), lambda i: (0, i, 0))
    lane_vec = pl.BlockSpec((1, LANE), lambda i: (0, 0))
    return pl.pallas_call(
        body, name="prep_fwd", grid=(n_blk,),
        in_specs=[pl.BlockSpec((tile, width), lambda i: (i, QKV_BLOCK)),
                  pl.BlockSpec((HALO, width), lambda i: (prev(i), QKV_BLOCK)),
                  pl.BlockSpec((HALO, width), lambda i: (nxt(i), QKV_BLOCK)),
                  pl.BlockSpec((HALO, width), lambda i: (0, 0)),
                  pl.BlockSpec((tile, LANE), lambda i: (i, 0)), lane_vec, lane_vec],
        out_specs=[hm, hm, hm, pl.BlockSpec((tile, LANE), lambda i: (i, 0))],
        out_shape=[jax.ShapeDtypeStruct((N_GROUPS, s, HEAD), F32)] * 3 + [jax.ShapeDtypeStruct((s, LANE), F32)],
        scratch_shapes=[pltpu.VMEM((tile + 2 * HALO, width), F32)],
        compiler_params=_params(("arbitrary",)),
    )(proj, proj, proj, conv_w, pg, a_log, dt_bias)


def _prep_bwd(proj, conv_w, pg, a_log, dt_bias, dq, dk, dv, dgates, dp, *, tile=128):
    s = proj.shape[0]
    width = 3 * D_DN
    tile = min(tile, s)
    n_blk = s // tile
    prev, nxt = _halo_maps(tile, s)
    ext_rows = tile + 2 * HALO
    pad = CONV_W // 2

    def body(x_ref, xp_ref, xn_ref, w_ref, pg_ref, al_ref, dt_ref,
             dq_ref, dqp_ref, dqn_ref, dk_ref, dkp_ref, dkn_ref, dv_ref, dvp_ref, dvn_ref,
             dg0_ref, dg1_ref, dg2_ref, _,
             dp_ref, dpg_ref, dw_ref, dal_ref, ddt_ref, xext, dcext):
        i = pl.program_id(0)
        has_prev, has_next = i > 0, i < n_blk - 1

        @pl.when(i == 0)
        def _():
            dw_ref[...] = jnp.zeros_like(dw_ref)
            dal_ref[...] = jnp.zeros_like(dal_ref)
            ddt_ref[...] = jnp.zeros_like(ddt_ref)

        zeros = jnp.zeros((HALO, width), F32)
        xext[pl.ds(0, HALO), :] = zeros
        xext[pl.ds(HALO, HALO), :] = jnp.where(has_prev, xp_ref[...], 0.0)
        xext[pl.ds(2 * HALO, tile), :] = x_ref[...]
        xext[pl.ds(2 * HALO + tile, HALO), :] = jnp.where(has_next, xn_ref[...], 0.0)
        xext[pl.ds(3 * HALO + tile, HALO), :] = zeros
        xfull, c = xext[...], None
        for j in range(CONV_W):
            term = _shifted_rows(xfull, j - pad, HALO, ext_rows) * w_ref[j:j + 1, :]
            c = term if c is None else c + term
        for h in range(N_GROUPS):
            lanes = [pl.ds(part * D_DN + h * HEAD, HEAD) for part in range(3)]
            cts = []
            for cur, before, after in ((dq_ref, dqp_ref, dqn_ref), (dk_ref, dkp_ref, dkn_ref), (dv_ref, dvp_ref, dvn_ref)):
                cts.append(jnp.concatenate([jnp.where(has_prev, before[h], 0.0), cur[h], jnp.where(has_next, after[h], 0.0)], axis=0))
            _, vjp = jax.vjp(_qkv_head, c[:, h * HEAD:(h + 1) * HEAD], c[:, D_DN + h * HEAD:D_DN + (h + 1) * HEAD],
                             c[:, 2 * D_DN + h * HEAD:2 * D_DN + (h + 1) * HEAD])
            for lane_sl, dc in zip(lanes, vjp(tuple(cts))):
                dcext[:, lane_sl] = dc
        dcfull = dcext[...]
        dx, dc_blk = None, dcfull[HALO:HALO + tile, :]
        for j in range(CONV_W):
            term = _shifted_rows(dcfull, pad - j, HALO, tile) * w_ref[j:j + 1, :]
            dx = term if dx is None else dx + term
            dw_ref[j:j + 1, :] += jnp.sum(dc_blk * _shifted_rows(xfull, j - pad, 2 * HALO, tile), axis=0, keepdims=True)
        dp_ref[...] = dx.astype(BF16)
        _, gvjp = jax.vjp(_gate_fn, pg_ref[...], al_ref[...], dt_ref[...])
        dpg, dal, ddt = gvjp(dg0_ref[...] + dg1_ref[...] + dg2_ref[...])
        dpg_ref[...] = dpg.astype(BF16)
        dal_ref[...] += dal
        ddt_ref[...] += ddt

    xs = [pl.BlockSpec((tile, width), lambda i: (i, QKV_BLOCK)), pl.BlockSpec((HALO, width), lambda i: (prev(i), QKV_BLOCK)),
          pl.BlockSpec((HALO, width), lambda i: (nxt(i), QKV_BLOCK))]
    hm = [pl.BlockSpec((N_GROUPS, tile, HEAD), lambda i: (0, i, 0)),
          pl.BlockSpec((N_GROUPS, HALO, HEAD), lambda i: (0, prev(i), 0)),
          pl.BlockSpec((N_GROUPS, HALO, HEAD), lambda i: (0, nxt(i), 0))]
    gate = pl.BlockSpec((tile, LANE), lambda i: (i, 0))
    lane_vec = pl.BlockSpec((1, LANE), lambda i: (0, 0))
    wspec = pl.BlockSpec((HALO, width), lambda i: (0, 0))
    n_in = 3 + 4 + 9 + 3
    return pl.pallas_call(
        body, name="prep_bwd", grid=(n_blk,),
        in_specs=xs + [wspec, gate, lane_vec, lane_vec] + hm * 3 + [gate] * 3 + [pl.BlockSpec(memory_space=pl.ANY)],
        out_specs=[pl.BlockSpec((tile, width), lambda i: (i, QKV_BLOCK)), gate, wspec, lane_vec, lane_vec],
        out_shape=[jax.ShapeDtypeStruct(dp.shape, BF16), jax.ShapeDtypeStruct((s, LANE), BF16),
                   jax.ShapeDtypeStruct((HALO, width), F32), jax.ShapeDtypeStruct((1, LANE), F32),
                   jax.ShapeDtypeStruct((1, LANE), F32)],
        input_output_aliases={n_in: 0},
        scratch_shapes=[pltpu.VMEM((tile + 4 * HALO, width), F32), pltpu.VMEM((ext_rows, width), F32)],
        compiler_params=_params(("arbitrary",)),
    )(proj, proj, proj, conv_w, pg, a_log, dt_bias, dq, dq, dq, dk, dk, dk, dv, dv, dv, *dgates, dp)


def _each(f, *cols):
    return tuple(f(*a) for a in zip(*cols))


def _delta_chunk(q, k, v, g_col, g_row, beta, state, t_known, reverse):
    n = DN_CHUNK
    ri = lax.broadcasted_iota(jnp.int32, (n, n), 0)
    ci = lax.broadcasted_iota(jnp.int32, (n, n), 1)
    if reverse:
        incl, incl_t, strict = ri <= ci, ri >= ci, ri < ci
    else:
        incl, incl_t, strict = ri >= ci, ri <= ci, ri > ci
    gc_col = _each(lambda gr: jnp.sum(jnp.where(incl, gr, 0.0), axis=1, keepdims=True), g_row)
    gc_row = _each(lambda gc: jnp.sum(jnp.where(incl_t, gc, 0.0), axis=0, keepdims=True), g_col)
    g_tot = _each(lambda gr: jnp.sum(gr, axis=1, keepdims=True), g_row)
    decay = _each(lambda a, b: jnp.where(incl, jnp.exp(jnp.where(incl, a - b, 0.0)), 0.0), gc_col, gc_row)
    kb = _each(lambda a, b: a * b, k, beta)
    kk = _each(_mm_nt, kb, k)
    m = _each(lambda a, d: jnp.where(strict, a * d, 0.0), kk, decay)
    if t_known is None:
        eye = (ri == ci).astype(F32)
        p = _each(lambda a: -a, m)
        t = _each(lambda a: eye + a, p)
        for _ in range(5):
            p = _each(lambda a: _dot3(a, a), p)
            t = _each(lambda a, b: a + _dot3(a, b), t, p)
    else:
        t = _each(_inverse_given, m, t_known)
    e_gc = _each(jnp.exp, gc_col)
    u = _each(lambda t_, v_, b_: _mm_nn(t_, v_ * b_), t, v, beta)
    w = _each(lambda t_, kb_, e_: _mm_nn(t_, kb_ * e_), t, kb, e_gc)
    attn = _each(lambda q_, k_, d_: _mm_nt(q_, k_) * d_, q, k, decay)
    ws = _each(_mm_nn, w, state)
    v_new = _each(lambda a, b: a - b, u, ws)
    qs = _each(lambda q_, e_, s_: _mm_nn(q_ * e_, s_), q, e_gc, state)
    av = _each(_mm_nn, attn, v_new)
    o = _each(lambda a, b: a + b, qs, av)
    kv = _each(lambda k_, gt, gc, vn: _mm_tn(k_ * jnp.exp(gt - gc), vn), k, g_tot, gc_col, v_new)
    new_state = _each(lambda s_, gt, kv_: s_ * jnp.exp(gt) + kv_, state, g_tot, kv)
    return o, new_state, t


def _delta_lanes(reverse):
    return (N_GROUPS, 3 * N_GROUPS) if reverse else (0, 2 * N_GROUPS)


def _delta_fwd(q, k, v, gates, g_rows, *, reverse, comm=None):
    s = q.shape[1]
    nc = s // DN_CHUNK
    g_lane, b_lane = _delta_lanes(reverse)
    chunk_of = (lambda c: nc - 1 - c) if reverse else (lambda c: c)
    heads = range(N_GROUPS)

    def body(q_ref, k_ref, v_ref, gate_ref, grow_ref, o_ref, st_ref, t_ref, state):
        @pl.when(pl.program_id(0) == 0)
        def _():
            state[...] = jnp.zeros_like(state)

        gb = gate_ref[...]
        prev = tuple(state[j] for j in heads)
        o, new, t = _delta_chunk(tuple(q_ref[j] for j in heads), tuple(k_ref[j] for j in heads), tuple(v_ref[j] for j in heads),
                                 tuple(gb[:, g_lane + j:g_lane + j + 1] for j in heads), tuple(grow_ref[0, j:j + 1, :] for j in heads),
                                 tuple(gb[:, b_lane + j:b_lane + j + 1] for j in heads), prev, None, reverse)
        for j in heads:
            st_ref[0, j] = prev[j]
            o_ref[j] = o[j]
            t_ref[0, j] = t[j]
            state[j] = new[j]

    hm = pl.BlockSpec((N_GROUPS, DN_CHUNK, HEAD), lambda c: (0, chunk_of(c), 0))
    name = ("delta_fwd_rev" if reverse else "delta_fwd") + ("" if comm is None else "_comm")
    return _call_with_comm(
        body, comm, name=name, grid=(nc,),
        in_specs=[hm, hm, hm, pl.BlockSpec((DN_CHUNK, LANE), lambda c: (chunk_of(c), 0)),
                  pl.BlockSpec((1, N_GROUPS, DN_CHUNK), lambda c: (chunk_of(c), 0, 0))],
        out_specs=[hm, pl.BlockSpec((1, N_GROUPS, HEAD, HEAD), lambda c: (chunk_of(c), 0, 0, 0)),
                   pl.BlockSpec((1, N_GROUPS, DN_CHUNK, DN_CHUNK), lambda c: (chunk_of(c), 0, 0, 0))],
        out_shape=[jax.ShapeDtypeStruct((N_GROUPS, s, HEAD), F32), jax.ShapeDtypeStruct((nc, N_GROUPS, HEAD, HEAD), F32),
                   jax.ShapeDtypeStruct((nc, N_GROUPS, DN_CHUNK, DN_CHUNK), F32)],
        scratch_shapes=[pltpu.VMEM((N_GROUPS, HEAD, HEAD), F32)],
        args=(q, k, v, gates, g_rows), compiler_params=_params(("arbitrary",)))


def _delta_bwd(q, k, v, gates, g_rows, states, t_inv, do, acc, *, reverse, comm=None):
    s = q.shape[1]
    nc = s // DN_CHUNK
    g_lane, b_lane = _delta_lanes(reverse)
    chunk_of = (lambda c: c) if reverse else (lambda c: nc - 1 - c)
    n_acc = 0 if acc is None else 3
    heads = range(N_GROUPS)

    def body(*refs):
        q_ref, k_ref, v_ref, gate_ref, grow_ref, st_ref, t_ref, do_ref = refs[:8]
        acc_refs = refs[8:8 + n_acc]
        dq_ref, dk_ref, dv_ref, dgate_ref, dgrow_ref, dstate = refs[8 + n_acc:]

        @pl.when(pl.program_id(0) == 0)
        def _():
            dstate[...] = jnp.zeros_like(dstate)

        gb = gate_ref[...]
        t_known = tuple(t_ref[0, j] for j in heads)
        f = lambda q_, k_, v_, gc_, gr_, b_, s_: _delta_chunk(q_, k_, v_, gc_, gr_, b_, s_, t_known, reverse)[:2]
        _, vjp = jax.vjp(f, tuple(q_ref[j] for j in heads), tuple(k_ref[j] for j in heads), tuple(v_ref[j] for j in heads),
                         tuple(gb[:, g_lane + j:g_lane + j + 1] for j in heads), tuple(grow_ref[0, j:j + 1, :] for j in heads),
                         tuple(gb[:, b_lane + j:b_lane + j + 1] for j in heads), tuple(st_ref[0, j] for j in heads))
        dq, dk, dv, dg_col, dg_row, dbeta, dprev = vjp((tuple(do_ref[j] for j in heads), tuple(dstate[j] for j in heads)))
        ids = lax.broadcasted_iota(jnp.int32, (DN_CHUNK, LANE), 1)
        dgate = jnp.zeros((DN_CHUNK, LANE), F32)
        for j in heads:
            if n_acc:
                dq_ref[j] = dq[j] + acc_refs[0][j]
                dk_ref[j] = dk[j] + acc_refs[1][j]
                dv_ref[j] = dv[j] + acc_refs[2][j]
            else:
                dq_ref[j] = dq[j]
                dk_ref[j] = dk[j]
                dv_ref[j] = dv[j]
            dstate[j] = dprev[j]
            dgate = dgate + jnp.where(ids == g_lane + j, dg_col[j], 0.0) + jnp.where(ids == b_lane + j, dbeta[j], 0.0)
            dgrow_ref[0, j:j + 1, :] = dg_row[j]
        dgate_ref[...] = dgate

    hm = pl.BlockSpec((N_GROUPS, DN_CHUNK, HEAD), lambda c: (0, chunk_of(c), 0))
    gate = pl.BlockSpec((DN_CHUNK, LANE), lambda c: (chunk_of(c), 0))
    rows = pl.BlockSpec((1, N_GROUPS, DN_CHUNK), lambda c: (chunk_of(c), 0, 0))
    st = pl.BlockSpec((1, N_GROUPS, HEAD, HEAD), lambda c: (chunk_of(c), 0, 0, 0))
    ts = pl.BlockSpec((1, N_GROUPS, DN_CHUNK, DN_CHUNK), lambda c: (chunk_of(c), 0, 0, 0))
    hm_shape = jax.ShapeDtypeStruct((N_GROUPS, s, HEAD), F32)
    name = ("delta_bwd_rev" if reverse else "delta_bwd") + ("" if comm is None else "_comm")
    return _call_with_comm(
        body, comm, name=name, grid=(nc,),
        in_specs=[hm, hm, hm, gate, rows, st, ts, hm] + [hm] * n_acc,
        out_specs=[hm, hm, hm, gate, rows],
        out_shape=[hm_shape] * 3 + [jax.ShapeDtypeStruct((s, LANE), F32), jax.ShapeDtypeStruct((nc, N_GROUPS, DN_CHUNK), F32)],
        scratch_shapes=[pltpu.VMEM((N_GROUPS, HEAD, HEAD), F32)],
        args=(q, k, v, gates, g_rows, states, t_inv, do, *(acc or ())), compiler_params=_params(("arbitrary",)))


ZB_BLOCK = 6


def _post_head(o_f, o_b, z, w):
    return _rms(o_f + o_b, w) * _silu(z)


def _post_fwd(o_f, o_b, proj, w, y, *, tile=256):
    s = proj.shape[0]
    tile = min(tile, s)

    def body(of_ref, ob_ref, z_ref, w_ref, _, y_ref):
        for h in range(N_GROUPS):
            lanes = pl.ds(h * HEAD, HEAD)
            y_ref[:, lanes] = _post_head(of_ref[h], ob_ref[h], z_ref[:, lanes], w_ref[...]).astype(BF16)

    hm = pl.BlockSpec((N_GROUPS, tile, HEAD), lambda i: (0, i, 0))
    return pl.pallas_call(
        body, name="post_fwd", grid=(s // tile,),
        in_specs=[hm, hm, pl.BlockSpec((tile, D_DN), lambda i: (i, ZB_BLOCK)), pl.BlockSpec((1, HEAD), lambda i: (0, 0)),
                  pl.BlockSpec(memory_space=pl.ANY)],
        out_specs=pl.BlockSpec((tile, D_DN), lambda i: (i, 1)),
        out_shape=jax.ShapeDtypeStruct(y.shape, BF16), input_output_aliases={4: 0},
        compiler_params=_params(("arbitrary",)),
    )(o_f, o_b, proj, w, y)


def _post_bwd(o_f, o_b, proj, w, dy, *, tile=256):
    s = proj.shape[0]
    tile = min(tile, s)

    def body(of_ref, ob_ref, z_ref, w_ref, dy_ref, do_ref, dp_ref, dw_ref):
        @pl.when(pl.program_id(0) == 0)
        def _():
            dw_ref[...] = jnp.zeros_like(dw_ref)

        for h in range(N_GROUPS):
            lanes = pl.ds(h * HEAD, HEAD)
            _, vjp = jax.vjp(_post_head, of_ref[h], ob_ref[h], z_ref[:, lanes], w_ref[...])
            do, _, dz, dw = vjp(dy_ref[:, lanes])
            do_ref[h] = do
            dp_ref[:, lanes] = dz.astype(BF16)
            dw_ref[...] += dw

    hm = pl.BlockSpec((N_GROUPS, tile, HEAD), lambda i: (0, i, 0))
    zb = pl.BlockSpec((tile, D_DN), lambda i: (i, ZB_BLOCK))
    vec = pl.BlockSpec((1, HEAD), lambda i: (0, 0))
    return pl.pallas_call(
        body, name="post_bwd", grid=(s // tile,),
        in_specs=[hm, hm, zb, vec, pl.BlockSpec((tile, D_DN), lambda i: (i, 1))], out_specs=[hm, zb, vec],
        out_shape=[jax.ShapeDtypeStruct((N_GROUPS, s, HEAD), F32), jax.ShapeDtypeStruct((s, D_MAIN), BF16),
                   jax.ShapeDtypeStruct((1, HEAD), F32)],
        compiler_params=_params(("arbitrary",)),
    )(o_f, o_b, proj, w, dy)


def _my_index():
    return 4 * lax.axis_index("x") + 2 * lax.axis_index("y") + lax.axis_index("c")


def _peer(k):
    x, y, c = lax.axis_index("x"), lax.axis_index("y"), lax.axis_index("c")
    px, py, pc = x ^ ((k >> 2) & 1), y ^ ((k >> 1) & 1), c ^ (k & 1)
    return (px, py, pc), 4 * px + 2 * py + pc


class _Comm:
    def __init__(self, inputs, outputs, copies):
        self.inputs, self.outputs, self.copies = list(inputs), list(outputs), list(copies)
        self.kept = [o for o in self.outputs if not isinstance(o, jax.ShapeDtypeStruct)]

    def out_shapes(self):
        return [jax.ShapeDtypeStruct(o.shape, o.dtype) for o in self.outputs]

    def aliases(self, first_kept_input, first_output):
        kept_at = [i for i, o in enumerate(self.outputs) if not isinstance(o, jax.ShapeDtypeStruct)]
        return {first_kept_input + n: first_output + i for n, i in enumerate(kept_at)}

    def scratch(self):
        n = len(self.copies)
        return [pltpu.SemaphoreType.DMA((n, N_DEV - 1)), pltpu.SemaphoreType.DMA((n, N_DEV - 1)), pltpu.SemaphoreType.DMA((n,))]

    def _descriptors(self, in_refs, out_refs, sems, arrivals):
        send_sems, recv_sems, local_sems = sems
        me = _my_index()
        local, remote = [], []
        for k in range(N_DEV):
            peer, peer_idx = _peer(k)
            for a, (i_in, i_out, src_of, dst_of) in enumerate(self.copies):
                if k == 0:
                    if not arrivals:
                        local.append(pltpu.make_async_copy(src_of(in_refs[i_in], me), dst_of(out_refs[i_out], me), local_sems.at[a]))
                    continue
                receiver, sender = (me, peer_idx) if arrivals else (peer_idx, me)
                remote.append(pltpu.make_async_remote_copy(
                    src_ref=src_of(in_refs[i_in], receiver), dst_ref=dst_of(out_refs[i_out], sender),
                    send_sem=send_sems.at[a, k - 1], recv_sem=recv_sems.at[a, k - 1], device_id=peer,
                    device_id_type=pl.DeviceIdType.MESH))
        return local, remote

    def start(self, in_refs, out_refs, sems):
        local, sends = self._descriptors(in_refs, out_refs, sems, arrivals=False)
        for cp in local + sends:
            cp.start()

    def wait(self, in_refs, out_refs, sems):
        _, arriving = self._descriptors(in_refs, out_refs, sems, arrivals=True)
        for cp in arriving:
            cp.wait_recv()
        local, sends = self._descriptors(in_refs, out_refs, sems, arrivals=False)
        for cp in sends:
            cp.wait_send()
        for cp in local:
            cp.wait()


def _call_with_comm(body, comm, *, name, grid, in_specs, out_specs, out_shape, scratch_shapes, args, compiler_params):
    if comm is None:
        return pl.pallas_call(body, name=name, grid=grid, in_specs=in_specs, out_specs=out_specs, out_shape=out_shape,
                              scratch_shapes=scratch_shapes, compiler_params=compiler_params)(*args), []
    n_in, n_out, n_scratch = len(in_specs), len(out_specs), len(scratch_shapes)
    c_in, c_kept, c_out = len(comm.inputs), len(comm.kept), len(comm.outputs)
    last = grid[0] - 1

    def wrapped(*refs):
        main_in, comm_in = refs[:n_in], refs[n_in:n_in + c_in]
        o = n_in + c_in + c_kept
        main_out, comm_out = refs[o:o + n_out], refs[o + n_out:o + n_out + c_out]
        s = o + n_out + c_out
        main_scratch, sems = refs[s:s + n_scratch], refs[s + n_scratch:]

        @pl.when(pl.program_id(0) == 0)
        def _():
            comm.start(comm_in, comm_out, sems)

        body(*main_in, *main_out, *main_scratch)

        @pl.when(pl.program_id(0) == last)
        def _():
            comm.wait(comm_in, comm_out, sems)

    any_spec = pl.BlockSpec(memory_space=pl.ANY)
    results = pl.pallas_call(
        wrapped, name=name, grid=grid, in_specs=list(in_specs) + [any_spec] * (c_in + c_kept),
        out_specs=list(out_specs) + [any_spec] * c_out, out_shape=list(out_shape) + comm.out_shapes(),
        scratch_shapes=list(scratch_shapes) + comm.scratch(), input_output_aliases=comm.aliases(n_in + c_in, n_out),
        compiler_params=compiler_params,
    )(*args, *comm.inputs, *comm.kept)
    return results[:n_out], results[n_out:]


def _exchange(name, comm):
    c_in, c_kept, c_out = len(comm.inputs), len(comm.kept), len(comm.outputs)

    def body(*refs):
        in_refs, out_refs, sems = refs[:c_in], refs[c_in + c_kept:c_in + c_kept + c_out], refs[c_in + c_kept + c_out:]
        comm.start(in_refs, out_refs, sems)
        comm.wait(in_refs, out_refs, sems)

    any_spec = pl.BlockSpec(memory_space=pl.ANY)
    return pl.pallas_call(
        body, name=name, in_specs=[any_spec] * (c_in + c_kept), out_specs=[any_spec] * c_out, out_shape=comm.out_shapes(),
        scratch_shapes=comm.scratch(), input_output_aliases=comm.aliases(c_in, 0),
    )(*comm.inputs, *comm.kept)


def _whole(ref, _):
    return ref


def _slot(ref, dev):
    return ref.at[dev]


def _row_block(size, axis):
    def pick(ref, dev):
        start = pl.multiple_of(dev * size, size) if size % SUBLANES == 0 else dev * size
        return ref.at[tuple([slice(None)] * axis + [pl.ds(start, size)])]
    return pick


def _adamw(name, parts, w, m, v, block):
    nd = len(block)
    grid = tuple(w.shape[d] // block[d] for d in range(nd) if block[d] != w.shape[d])
    lead = len(grid)

    def body(p_ref, w_ref, m_ref, v_ref, g_ref, d_ref, mo_ref, vo_ref):
        g = p_ref[0].astype(F32)
        for j in range(1, N_DEV):
            g = g + p_ref[j].astype(F32)
        m_new = ADAM_B1 * m_ref[...] + (1.0 - ADAM_B1) * g
        v_new = ADAM_B2 * v_ref[...] + (1.0 - ADAM_B2) * jnp.square(g)
        m_hat = m_new / np.float32(1.0 - ADAM_B1 ** ADAM_STEP)
        v_hat = v_new / np.float32(1.0 - ADAM_B2 ** ADAM_STEP)
        g_ref[...] = g
        d_ref[...] = -ADAM_LR * (m_hat / (jnp.sqrt(v_hat) + ADAM_EPS) + ADAM_WD * w_ref[...])
        mo_ref[...] = m_new
        vo_ref[...] = v_new

    idx = lambda *i: tuple(i) + (0,) * (nd - lead)
    spec = pl.BlockSpec(block, idx)
    pspec = pl.BlockSpec((N_DEV,) + tuple(block), lambda *i: (0,) + idx(*i))
    n_blk = int(np.prod(block))
    return pl.pallas_call(
        body, name=name, grid=grid, in_specs=[pspec, spec, spec, spec], out_specs=[spec] * 4,
        out_shape=[jax.ShapeDtypeStruct(w.shape, F32)] * 4,
        compiler_params=_params(("arbitrary",) * lead, 2 * (N_DEV + 7) * n_blk * 4 + VMEM_MARGIN),
    )(parts, w, m, v)


def _to_flat(pieces, row_multiple):
    flat = jnp.concatenate([p.reshape(-1) for p in pieces])
    rows = -(-flat.shape[0] // FLAT_W)
    rows = -(-rows // row_multiple) * row_multiple
    return jnp.pad(flat, (0, rows * FLAT_W - flat.shape[0])).reshape(rows, FLAT_W)


def _lane_row(*pieces):
    flat = jnp.concatenate([p.reshape(-1) for p in pieces])
    return jnp.pad(flat, (0, LANE - flat.shape[0])).reshape(1, LANE)


def _rows_view(a):
    return a.reshape(a.shape[:-1] + (SUBLANES, LANE))


def _in_shard_t(a):
    return _rows_view(jnp.swapaxes(a, 1, 2))


def kernel(x, norm_w, w_in, sgu_ln_g, sgu_ln_b, sgu_w, sgu_b, conv_w, a_log_f, a_log_b, dt_bias_f, dt_bias_b, dn_norm_w, w_out, final_norm_w, loss_target, m_norm_w, m_w_in, m_sgu_ln_g, m_sgu_ln_b, m_sgu_w, m_sgu_b, m_conv_w, m_a_log_f, m_a_log_b, m_dt_bias_f, m_dt_bias_b, m_dn_norm_w, m_w_out, m_final_norm_w, v_norm_w, v_w_in, v_sgu_ln_g, v_sgu_ln_b, v_sgu_w, v_sgu_b, v_conv_w, v_a_log_f, v_a_log_b, v_dt_bias_f, v_dt_bias_b, v_dn_norm_w, v_w_out, v_final_norm_w):
    weights = dict(norm_w=norm_w, w_in=w_in, sgu_ln_g=sgu_ln_g, sgu_ln_b=sgu_ln_b, sgu_w=sgu_w, sgu_b=sgu_b, conv_w=conv_w,
                   a_log_f=a_log_f, a_log_b=a_log_b, dt_bias_f=dt_bias_f, dt_bias_b=dt_bias_b, dn_norm_w=dn_norm_w,
                   w_out=w_out, final_norm_w=final_norm_w)
    m_in = dict(norm_w=m_norm_w, w_in=m_w_in, sgu_ln_g=m_sgu_ln_g, sgu_ln_b=m_sgu_ln_b, sgu_w=m_sgu_w, sgu_b=m_sgu_b,
                conv_w=m_conv_w, a_log_f=m_a_log_f, a_log_b=m_a_log_b, dt_bias_f=m_dt_bias_f, dt_bias_b=m_dt_bias_b,
                dn_norm_w=m_dn_norm_w, w_out=m_w_out, final_norm_w=m_final_norm_w)
    v_in = dict(norm_w=v_norm_w, w_in=v_w_in, sgu_ln_g=v_sgu_ln_g, sgu_ln_b=v_sgu_ln_b, sgu_w=v_sgu_w, sgu_b=v_sgu_b,
                conv_w=v_conv_w, a_log_f=v_a_log_f, a_log_b=v_a_log_b, dt_bias_f=v_dt_bias_f, dt_bias_b=v_dt_bias_b,
                dn_norm_w=v_dn_norm_w, w_out=v_w_out, final_norm_w=v_final_norm_w)
    s = x.shape[1]
    nc = s // DN_CHUNK
    depth = w_in.shape[0]
    in_shard, out_shard, conv_shard = w_in.shape[2], w_out.shape[1], conv_w.shape[2]
    x0 = x.reshape(s, D_MODEL)
    target = loss_target.reshape(s, D_MODEL)

    w_in_shard = _in_shard_t(w_in.astype(BF16))
    w_out_shard = w_out.astype(BF16)

    gathered_in_shape = jax.ShapeDtypeStruct((N_DEV * in_shard, SUBLANES, LANE), BF16)
    gathered_out_shape = jax.ShapeDtypeStruct((D_MODEL, D_MODEL), BF16)
    gathered_in, gathered_out, conv_all = _exchange("gather_first", _Comm(
        [w_in_shard[0], w_out_shard[0], conv_w],
        [gathered_in_shape, gathered_out_shape, jax.ShapeDtypeStruct((N_DEV,) + conv_w.shape, F32)],
        [(0, 0, _whole, _row_block(in_shard, 0)), (1, 1, _whole, _row_block(out_shard, 0)), (2, 2, _whole, _slot)]))
    gather_in_plan = lambda l: _Comm([w_in_shard[l]], [gathered_in_shape], [(0, 0, _whole, _row_block(in_shard, 0))])
    gather_out_plan = lambda l: _Comm([w_out_shard[l]], [gathered_out_shape], [(0, 0, _whole, _row_block(out_shard, 0))])
    conv_full = conv_all.transpose(1, 2, 0, 3).reshape(depth, CONV_W, 3 * D_DN)
    conv_full = jnp.pad(conv_full, ((0, 0), (0, HALO - CONV_W), (0, 0)))
    sgu_b_t = jnp.swapaxes(sgu_b, 1, 2)

    def rows_of(gates, first_lane):
        return gates[:, first_lane:first_lane + N_GROUPS].reshape(nc, DN_CHUNK, N_GROUPS).transpose(0, 2, 1)

    saved = []
    xl = x0
    for l in range(depth):
        w_in_t = gathered_in.reshape(D_IN, D_MODEL)
        w_gate_t = jnp.pad(w_in_t[D_MAIN:], ((0, LANE - N_GATE), (0, 0)))
        w_out_l = gathered_out
        nw = norm_w[l].reshape(1, D_MODEL)
        h, h_t = _rms_fwd(xl, nw)
        proj = _matmul([(_lhs(h), _rhs(w_in_t, MM_WIDE, transposed=True))], s, D_MAIN, name="in_proj", trans_b=True, tn=MM_WIDE)
        p_gate = _matmul([(_lhs(h), _rhs(w_gate_t, LANE, transposed=True))], s, LANE, name="in_proj_gate", trans_b=True, tn=LANE)
        ln_g, ln_b = sgu_ln_g[l].reshape(1, D_SGU), sgu_ln_b[l].reshape(1, D_SGU)
        y = _sgu_fwd(proj, ln_g, ln_b, sgu_w[l], sgu_b_t[l])
        a_log = _lane_row(a_log_f[l], a_log_b[l])
        dt_bias = _lane_row(dt_bias_f[l], dt_bias_b[l])
        q, k, v, gates = _prep_fwd(proj, conv_full[l], p_gate, a_log, dt_bias)
        rows_f, rows_b = rows_of(gates, 0), rows_of(gates, N_GROUPS)
        more = l + 1 < depth
        (o_f, st_f, t_f), next_in = _delta_fwd(q, k, v, gates, rows_f, reverse=False, comm=gather_in_plan(l + 1) if more else None)
        (o_b, st_b, t_b), next_out = _delta_fwd(q, k, v, gates, rows_b, reverse=True, comm=gather_out_plan(l + 1) if more else None)
        dn_w = dn_norm_w[l].reshape(1, HEAD)
        y = _post_fwd(o_f, o_b, proj, dn_w, y)
        x_next = _matmul([(_lhs(y), _rhs(w_out_l, MM_WIDE))], s, D_MODEL, name="out_proj", add=xl, tn=MM_WIDE)
        saved.append(dict(x=xl, nw=nw, h_t=h_t, proj=proj, p_gate=p_gate, ln_g=ln_g, ln_b=ln_b, a_log=a_log, dt_bias=dt_bias,
                          q=q, k=k, v=v, gates=gates, rows_f=rows_f, rows_b=rows_b, o_f=o_f, o_b=o_b, st_f=st_f, st_b=st_b,
                          t_f=t_f, t_b=t_b, dn_w=dn_w, y=y, w_in_t=w_in_t, w_gate_t=w_gate_t, w_out=w_out_l))
        xl = x_next
        if more:
            (gathered_in,), (gathered_out,) = next_in, next_out

    loss_part, dx, dx_bf, d_final = _final_loss(xl, final_norm_w.reshape(1, D_MODEL), target)
    loss = lax.psum(loss_part[0, 0], MESH_AXES)

    small_names = [n for n in SMALL if n not in ("conv_w", "final_norm_w")]
    grads = {n: [None] * depth for n in small_names + ["conv_w"]}
    t_rows = -(-D_IN // LANE) * LANE
    layer_slot = lambda l: (lambda ref, dev: ref.at[dev, l])
    parts = [jax.ShapeDtypeStruct((N_DEV, depth, in_shard, SUBLANES, LANE), BF16),
             jax.ShapeDtypeStruct((N_DEV, depth, out_shard, D_MODEL), BF16)]

    def exchange_plan(layer, g_in_rows, g_out, more_inputs=(), more_outputs=(), more_copies=()):
        return _Comm([g_in_rows, g_out, *more_inputs], [*parts, *more_outputs],
                     [(0, 0, _row_block(in_shard, 0), layer_slot(layer)), (1, 1, _row_block(out_shard, 0), layer_slot(layer)),
                      *more_copies])

    pending = None
    for l in reversed(range(depth)):
        sv = saved[l]
        dy = _matmul([(_lhs(dx_bf), _rhs(sv["w_out"], MM_WIDE, transposed=True))], s, D_MODEL, name="out_proj_dy", trans_b=True,
                     tn=MM_WIDE)
        g_out = _matmul([(_lhs(sv["y"].T), _rhs(dx_bf, MM_TILE))], D_MODEL, D_MODEL, name="out_proj_dw", out_dtype=BF16)
        do, dp, d_dn = _post_bwd(sv["o_f"], sv["o_b"], sv["proj"], sv["dn_w"], dy)
        (dq, dk, dv, dgate_f, drows_f), exchanged = _delta_bwd(
            sv["q"], sv["k"], sv["v"], sv["gates"], sv["rows_f"], sv["st_f"], sv["t_f"], do, None, reverse=False,
            comm=None if pending is None else exchange_plan(*pending))
        if exchanged:
            parts = list(exchanged)
        (dq, dk, dv, dgate_b, drows_b), _ = _delta_bwd(sv["q"], sv["k"], sv["v"], sv["gates"], sv["rows_b"], sv["st_b"], sv["t_b"],
                                                       do, (dq, dk, dv), reverse=True)
        drows = jnp.concatenate([drows_f.transpose(0, 2, 1).reshape(s, N_GROUPS), drows_b.transpose(0, 2, 1).reshape(s, N_GROUPS)], axis=1)
        drows = jnp.pad(drows, ((0, 0), (0, LANE - 2 * N_GROUPS)))
        dp, d_gate, d_conv, d_alog, d_dt = _prep_bwd(sv["proj"], conv_full[l], sv["p_gate"], sv["a_log"], sv["dt_bias"],
                                                     dq, dk, dv, (dgate_f, dgate_b, drows), dp)
        dp, d_lg, d_lb, d_sw, d_sbt = _sgu_bwd(sv["proj"], sv["ln_g"], sv["ln_b"], sgu_w[l], sgu_b_t[l], dy, dp)
        dh = _matmul([(_lhs(dp), _rhs(sv["w_in_t"], MM_TILE, k=D_MAIN)), (_lhs(d_gate), _rhs(sv["w_gate_t"], MM_TILE))],
                     s, D_MODEL, name="in_proj_dh")
        h_t = sv["h_t"]
        g_in_t = _matmul([(_lhs(h_t), _rhs(dp, MM_TILE))], D_MODEL, D_MAIN, name="in_proj_dw", out_dtype=BF16, out_t_rows=t_rows)
        g_in_t = _matmul([(_lhs(h_t), _rhs(d_gate, LANE))], D_MODEL, LANE, name="in_proj_dw_gate", out_dtype=BF16, tn=LANE,
                         transpose_into=g_in_t, out_row_block=D_MAIN // LANE)
        pending = (l, _rows_view(g_in_t), g_out)
        dx, dx_bf, d_nw = _rms_bwd(sv["x"], sv["nw"], dh, dx)
        grads["norm_w"][l] = d_nw.reshape(D_MODEL)
        grads["sgu_ln_g"][l], grads["sgu_ln_b"][l] = d_lg.reshape(D_SGU), d_lb.reshape(D_SGU)
        grads["sgu_w"][l], grads["sgu_b"][l] = d_sw, d_sbt.T
        grads["conv_w"][l] = d_conv[:CONV_W]
        grads["a_log_f"][l], grads["a_log_b"][l] = d_alog[0, :N_GROUPS], d_alog[0, N_GROUPS:2 * N_GROUPS]
        grads["dt_bias_f"][l], grads["dt_bias_b"][l] = d_dt[0, :N_GROUPS], d_dt[0, N_GROUPS:2 * N_GROUPS]
        grads["dn_norm_w"][l] = d_dn.reshape(HEAD)
    grad_x = dx.reshape(1, s, D_MODEL)

    g_small = {n: jnp.stack(grads[n]) for n in small_names}
    g_small["final_norm_w"] = d_final.reshape(D_MODEL)
    conv_by_dev = jnp.stack(grads["conv_w"]).reshape(depth, CONV_W, N_DEV, conv_shard).transpose(2, 0, 1, 3).reshape(N_DEV, -1)
    replicated = jnp.concatenate([g_small[n].reshape(-1) for n in SMALL if n != "conv_w"])
    n_small = conv_by_dev.shape[1] + replicated.shape[0]
    small_rows = -(-n_small // (FLAT_W * HALO)) * HALO
    send_small = jnp.concatenate([conv_by_dev, jnp.broadcast_to(replicated, (N_DEV, replicated.shape[0])),
                                  jnp.zeros((N_DEV, small_rows * FLAT_W - n_small), F32)], axis=1).reshape(N_DEV, small_rows, FLAT_W)
    parts_in, parts_out, parts_small = _exchange(
        "exchange_last", exchange_plan(*pending, [send_small], [jax.ShapeDtypeStruct((N_DEV, small_rows, FLAT_W), F32)],
                                       [(2, 2, _slot, _slot)]))

    in_rows = 100 if in_shard % 100 == 0 else in_shard
    res_in = _adamw("adamw_in", parts_in, _in_shard_t(w_in), _in_shard_t(m_w_in), _in_shard_t(v_w_in),
                    (1, in_rows, SUBLANES, LANE))
    res_in = [jnp.swapaxes(r.reshape(depth, in_shard, D_MODEL), 1, 2) for r in res_in]
    res_out = _adamw("adamw_out", parts_out, w_out, m_w_out, v_w_out, (1, min(128, out_shard), D_MODEL))
    res_small = _adamw("adamw_small", parts_small, _to_flat([weights[n] for n in SMALL], HALO), _to_flat([m_in[n] for n in SMALL], HALO),
                       _to_flat([v_in[n] for n in SMALL], HALO), (small_rows // 4 if small_rows % 32 == 0 else small_rows, FLAT_W))
    outs = [dict(w_in=a, w_out=b) for a, b in zip(res_in, res_out)]
    off = 0
    for n in SMALL:
        size = int(np.prod(weights[n].shape))
        for o, d in zip(res_small, outs):
            d[n] = o.reshape(-1)[off:off + size].reshape(weights[n].shape)
        off += size
    g_out, delta_out, m_out, v_out = outs
    return (loss, grad_x, *[g_out[n] for n in WEIGHTS], *[delta_out[n] for n in WEIGHTS],
            *[m_out[n] for n in WEIGHTS], *[v_out[n] for n in WEIGHTS])
```

```python
import numpy as np
import jax
import jax.numpy as jnp
from jax import lax
from jax.experimental import pallas as pl
from jax.experimental.pallas import tpu as pltpu

F32 = jnp.float32
BF16 = jnp.bfloat16

N_DEV = 8
D_MODEL = 2048
D_SGU = 1024
D_DN = 1024
N_GROUPS = 8
HEAD = 128
SGU_CHUNK = 128
DN_CHUNK = 64
CONV_W = 5
N_GATE = 32
D_MAIN = 3 * D_SGU + 4 * D_DN
D_IN = D_MAIN + N_GATE
LANE = 128
SUBLANES = 16
HALO = 8
EPS = 1e-6
ADAM_LR, ADAM_B1, ADAM_B2, ADAM_EPS, ADAM_WD, ADAM_STEP = 0.001, 0.9, 0.999, 1e-08, 0.01, 10
FLAT_W = 1024
MM_TILE = 512
MM_WIDE = 1024
VMEM_MARGIN = 8 << 20

MESH_AXES = ("x", "y", "c")
WEIGHTS = ("norm_w", "w_in", "sgu_ln_g", "sgu_ln_b", "sgu_w", "sgu_b", "conv_w", "a_log_f", "a_log_b",
           "dt_bias_f", "dt_bias_b", "dn_norm_w", "w_out", "final_norm_w")
SMALL = ("conv_w",) + tuple(n for n in WEIGHTS if n not in ("w_in", "w_out", "conv_w"))


def _bdot(a, b, dims):
    return lax.dot_general(a.astype(BF16), b.astype(BF16), (dims, ((), ())), preferred_element_type=F32)


@jax.custom_vjp
def _mm_nn(a, b):
    return _bdot(a, b, ((1,), (0,)))


def _mm_nn_fwd(a, b):
    return _mm_nn(a, b), (a, b)


def _mm_nn_bwd(res, ct):
    a, b = res
    return _bdot(ct, b, ((1,), (1,))), _bdot(a, ct, ((0,), (0,)))


_mm_nn.defvjp(_mm_nn_fwd, _mm_nn_bwd)


@jax.custom_vjp
def _mm_nt(a, b):
    return _bdot(a, b, ((1,), (1,)))


def _mm_nt_fwd(a, b):
    return _mm_nt(a, b), (a, b)


def _mm_nt_bwd(res, ct):
    a, b = res
    return _bdot(ct, b, ((1,), (0,))), _bdot(ct, a, ((0,), (0,)))


_mm_nt.defvjp(_mm_nt_fwd, _mm_nt_bwd)


@jax.custom_vjp
def _mm_tn(a, b):
    return _bdot(a, b, ((0,), (0,)))


def _mm_tn_fwd(a, b):
    return _mm_tn(a, b), (a, b)


def _mm_tn_bwd(res, ct):
    a, b = res
    return _bdot(b, ct, ((1,), (1,))), _bdot(a, ct, ((1,), (0,)))


_mm_tn.defvjp(_mm_tn_fwd, _mm_tn_bwd)


@jax.custom_vjp
def _inverse_given(m, t):
    return t


def _inverse_given_fwd(m, t):
    return t, t


def _inverse_given_bwd(t, ct):
    dm = -_bdot(t, _bdot(ct, t, ((1,), (1,))), ((0,), (0,)))
    return dm, jnp.zeros_like(t)


_inverse_given.defvjp(_inverse_given_fwd, _inverse_given_bwd)


@jax.custom_vjp
def _split_lanes(x):
    return tuple(x[:, i * LANE:(i + 1) * LANE] for i in range(x.shape[1] // LANE))


def _split_lanes_fwd(x):
    return _split_lanes(x), None


def _split_lanes_bwd(_, cts):
    return (jnp.concatenate(cts, axis=1),)


_split_lanes.defvjp(_split_lanes_fwd, _split_lanes_bwd)


def _silu(t):
    return t * jax.nn.sigmoid(t)


def _gelu(t):
    return 0.5 * t * (1.0 + lax.erf(t * np.float32(0.7071067811865476)))


def _rms(x, w):
    return x * lax.rsqrt(jnp.mean(x * x, axis=-1, keepdims=True) + EPS) * w


def _params(sem, vmem_bytes=None):
    kw = dict(dimension_semantics=sem)
    if vmem_bytes is not None:
        kw["vmem_limit_bytes"] = int(vmem_bytes)
    return pltpu.CompilerParams(**kw)


def _lhs(a, k=None, k_block=0):
    k = a.shape[1] if k is None else k
    tm = min(MM_TILE, a.shape[0])
    return a, pl.BlockSpec((tm, k), lambda i, j: (i, k_block)), tm * k * a.dtype.itemsize


def _rhs(b, tn, *, layer=None, transposed=False, k=None, k_block=0, n_offset=0):
    lead = () if layer is None else (layer,)
    none = () if layer is None else (None,)
    if transposed:
        k = b.shape[-1]
        spec = pl.BlockSpec(none + (tn, k), lambda i, j: lead + (j + n_offset, 0))
    else:
        k = b.shape[-2] if k is None else k
        spec = pl.BlockSpec(none + (k, tn), lambda i, j: lead + (k_block, j))
    return b, spec, tn * k * b.dtype.itemsize


def _matmul(pairs, m, n, *, name, trans_b=False, add=None, out_dtype=F32, tn=MM_TILE, out_t_rows=None, transpose_into=None,
            out_row_block=0):
    tm = min(MM_TILE, m)
    assert m % tm == 0 and n % tn == 0
    n_pairs = len(pairs)
    contract = ((1,), (1,)) if trans_b else ((1,), (0,))
    transposed_out = out_t_rows is not None or transpose_into is not None

    def body(*refs):
        o_ref = refs[-1]
        acc = None
        for i in range(n_pairs):
            d = lax.dot_general(refs[2 * i][...].astype(BF16), refs[2 * i + 1][...].astype(BF16),
                                (contract, ((), ())), preferred_element_type=F32)
            acc = d if acc is None else acc + d
        if add is not None:
            acc = acc + refs[2 * n_pairs][...]
        o_ref[...] = (acc.T if transposed_out else acc).astype(out_dtype)

    in_specs, args, vmem, aliases = [], [], 0, {}
    for (a, a_spec, a_bytes), (b, b_spec, b_bytes) in pairs:
        in_specs += [a_spec, b_spec]
        args += [a, b]
        vmem += 2 * (a_bytes + b_bytes)
    if add is not None:
        in_specs.append(pl.BlockSpec((tm, tn), lambda i, j: (i, j)))
        args.append(add)
        vmem += 2 * tm * tn * 4
    vmem += 2 * tm * tn * jnp.dtype(out_dtype).itemsize + 3 * tm * tn * 4
    if transposed_out:
        out_spec = pl.BlockSpec((tn, tm), lambda i, j: (j + out_row_block, i))
        if transpose_into is not None:
            in_specs.append(pl.BlockSpec(memory_space=pl.ANY))
            args.append(transpose_into)
            aliases = {len(args) - 1: 0}
            out_shape = jax.ShapeDtypeStruct(transpose_into.shape, out_dtype)
        else:
            out_shape = jax.ShapeDtypeStruct((out_t_rows, m), out_dtype)
    else:
        out_spec = pl.BlockSpec((tm, tn), lambda i, j: (i, j))
        out_shape = jax.ShapeDtypeStruct((m, n), out_dtype)
    return pl.pallas_call(
        body, name=name, grid=(m // tm, n // tn), in_specs=in_specs, out_specs=out_spec, out_shape=out_shape,
        input_output_aliases=aliases, compiler_params=_params(("parallel", "arbitrary"), vmem + VMEM_MARGIN),
    )(*args)


def _rms_fwd(x, w, *, tile=512):
    s, d = x.shape
    tile = min(tile, s)

    def body(x_ref, w_ref, h_ref, ht_ref):
        h = _rms(x_ref[...], w_ref[...])
        h_ref[...] = h.astype(BF16)
        ht_ref[...] = h.T.astype(BF16)

    return pl.pallas_call(
        body, name="rms_fwd", grid=(s // tile,),
        in_specs=[pl.BlockSpec((tile, d), lambda i: (i, 0)), pl.BlockSpec((1, d), lambda i: (0, 0))],
        out_specs=[pl.BlockSpec((tile, d), lambda i: (i, 0)), pl.BlockSpec((d, tile), lambda i: (0, i))],
        out_shape=[jax.ShapeDtypeStruct((s, d), BF16), jax.ShapeDtypeStruct((d, s), BF16)],
        compiler_params=_params(("arbitrary",)),
    )(x, w)


def _rms_bwd(x, w, dh, dx_out, *, tile=256):
    s, d = x.shape
    tile = min(tile, s)

    def body(x_ref, w_ref, dh_ref, dxo_ref, dx_ref, dxb_ref, dw_ref):
        _, vjp = jax.vjp(_rms, x_ref[...], w_ref[...])
        dxv, dwv = vjp(dh_ref[...])
        dx = dxo_ref[...] + dxv
        dx_ref[...] = dx
        dxb_ref[...] = dx.astype(BF16)

        @pl.when(pl.program_id(0) == 0)
        def _():
            dw_ref[...] = jnp.zeros_like(dw_ref)

        dw_ref[...] += dwv

    row = pl.BlockSpec((tile, d), lambda i: (i, 0))
    vec = pl.BlockSpec((1, d), lambda i: (0, 0))
    return pl.pallas_call(
        body, name="rms_bwd", grid=(s // tile,),
        in_specs=[row, vec, row, row], out_specs=[row, row, vec],
        out_shape=[jax.ShapeDtypeStruct((s, d), F32), jax.ShapeDtypeStruct((s, d), BF16), jax.ShapeDtypeStruct((1, d), F32)],
        compiler_params=_params(("arbitrary",)),
    )(x, w, dh, dx_out)


def _final_loss(x, w, target, *, tile=256):
    s, d = x.shape
    tile = min(tile, s)

    def body(x_ref, w_ref, t_ref, loss_ref, dx_ref, dxb_ref, dw_ref):
        def f(xv, wv):
            err = jnp.square(_rms(xv, wv) - t_ref[...])
            per_token = jnp.sum(err, axis=1, keepdims=True) * np.float32(1.0 / d)
            return 0.5 * jnp.sum(per_token, axis=0, keepdims=True)

        loss, vjp = jax.vjp(f, x_ref[...], w_ref[...])
        dxv, dwv = vjp(jnp.ones((1, 1), F32))
        dx_ref[...] = dxv
        dxb_ref[...] = dxv.astype(BF16)

        @pl.when(pl.program_id(0) == 0)
        def _():
            dw_ref[...] = jnp.zeros_like(dw_ref)
            loss_ref[...] = jnp.zeros_like(loss_ref)

        dw_ref[...] += dwv
        loss_ref[...] += jnp.broadcast_to(loss, (1, LANE))

    row = pl.BlockSpec((tile, d), lambda i: (i, 0))
    vec = pl.BlockSpec((1, d), lambda i: (0, 0))
    return pl.pallas_call(
        body, name="final_loss", grid=(s // tile,),
        in_specs=[row, vec, row], out_specs=[pl.BlockSpec((1, LANE), lambda i: (0, 0)), row, row, vec],
        out_shape=[jax.ShapeDtypeStruct((1, LANE), F32), jax.ShapeDtypeStruct((s, d), F32),
                   jax.ShapeDtypeStruct((s, d), BF16), jax.ShapeDtypeStruct((1, d), F32)],
        compiler_params=_params(("arbitrary",)),
    )(x, w, target)


def _sgu_chunk(u, v, z, ln_g, ln_b, ws, bcols):
    vg = _gelu(v)
    xc = vg - jnp.mean(vg, axis=-1, keepdims=True)
    vl = xc * lax.rsqrt(jnp.mean(xc * xc, axis=-1, keepdims=True) + EPS) * ln_g + ln_b
    sp = jnp.concatenate([_mm_nn(ws[g], vg_) + bcols[g] for g, vg_ in enumerate(_split_lanes(vl))], axis=1)
    return _gelu(u) * sp * _silu(z)


def _sgu_specs(tile):
    col = lambda j: pl.BlockSpec((tile, D_SGU), lambda i, j=j: (i, j))
    vec = pl.BlockSpec((1, D_SGU), lambda i: (0, 0))
    wspec = pl.BlockSpec((N_GROUPS, SGU_CHUNK, SGU_CHUNK), lambda i: (0, 0, 0))
    bspec = pl.BlockSpec((SGU_CHUNK, N_GROUPS), lambda i: (0, 0))
    return col, vec, wspec, bspec


def _sgu_fwd(proj, ln_g, ln_b, w_s, b_t, *, tile=256):
    s = proj.shape[0]
    tile = min(tile, s)
    col, vec, wspec, bspec = _sgu_specs(tile)

    def body(u_ref, v_ref, z_ref, g_ref, b_ref, w_ref, bt_ref, y_ref, yt_ref):
        ws = tuple(w_ref[g] for g in range(N_GROUPS))
        bcols = tuple(bt_ref[:, g:g + 1] for g in range(N_GROUPS))
        for c in range(tile // SGU_CHUNK):
            rows = pl.ds(c * SGU_CHUNK, SGU_CHUNK)
            y = _sgu_chunk(u_ref[rows, :], v_ref[rows, :], z_ref[rows, :], g_ref[...], b_ref[...], ws, bcols)
            y_ref[rows, :] = y.astype(BF16)
            yt_ref[:, rows] = y.T.astype(BF16)

    return pl.pallas_call(
        body, name="sgu_fwd", grid=(s // tile,),
        in_specs=[col(0), col(1), col(2), vec, vec, wspec, bspec],
        out_specs=[pl.BlockSpec((tile, D_SGU), lambda i: (i, 0)), pl.BlockSpec((D_SGU, tile), lambda i: (0, i))],
        out_shape=[jax.ShapeDtypeStruct((s, D_SGU + D_DN), BF16), jax.ShapeDtypeStruct((D_SGU + D_DN, s), BF16)],
        compiler_params=_params(("arbitrary",)),
    )(proj, proj, proj, ln_g, ln_b, w_s, b_t)


def _sgu_bwd(proj, ln_g, ln_b, w_s, b_t, dy, dp, *, tile=128):
    s = proj.shape[0]
    tile = min(tile, s)
    col, vec, wspec, bspec = _sgu_specs(tile)

    def body(u_ref, v_ref, z_ref, g_ref, b_ref, w_ref, bt_ref, dy_ref, _, dp_ref, dg_ref, db_ref, dw_ref, dbt_ref):
        @pl.when(pl.program_id(0) == 0)
        def _():
            dg_ref[...] = jnp.zeros_like(dg_ref)
            db_ref[...] = jnp.zeros_like(db_ref)
            dw_ref[...] = jnp.zeros_like(dw_ref)
            dbt_ref[...] = jnp.zeros_like(dbt_ref)

        ws = tuple(w_ref[g] for g in range(N_GROUPS))
        bcols = tuple(bt_ref[:, g:g + 1] for g in range(N_GROUPS))
        for c in range(tile // SGU_CHUNK):
            rows = pl.ds(c * SGU_CHUNK, SGU_CHUNK)
            _, vjp = jax.vjp(_sgu_chunk, u_ref[rows, :], v_ref[rows, :], z_ref[rows, :], g_ref[...], b_ref[...], ws, bcols)
            du, dv, dz, dg, db, dws, dbc = vjp(dy_ref[rows, :])
            dp_ref[rows, pl.ds(0, D_SGU)] = du.astype(BF16)
            dp_ref[rows, pl.ds(D_SGU, D_SGU)] = dv.astype(BF16)
            dp_ref[rows, pl.ds(2 * D_SGU, D_SGU)] = dz.astype(BF16)
            dg_ref[...] += dg
            db_ref[...] += db
            for g in range(N_GROUPS):
                dw_ref[g] += dws[g]
                dbt_ref[:, g:g + 1] += dbc[g]

    return pl.pallas_call(
        body, name="sgu_bwd", grid=(s // tile,),
        in_specs=[col(0), col(1), col(2), vec, vec, wspec, bspec, pl.BlockSpec((tile, D_SGU), lambda i: (i, 0)),
                  pl.BlockSpec(memory_space=pl.ANY)],
        out_specs=[pl.BlockSpec((tile, 3 * D_SGU), lambda i: (i, 0)), vec, vec, wspec, bspec],
        out_shape=[jax.ShapeDtypeStruct(dp.shape, BF16), jax.ShapeDtypeStruct((1, D_SGU), F32),
                   jax.ShapeDtypeStruct((1, D_SGU), F32), jax.ShapeDtypeStruct((N_GROUPS, SGU_CHUNK, SGU_CHUNK), F32),
                   jax.ShapeDtypeStruct((SGU_CHUNK, N_GROUPS), F32)],
        input_output_aliases={8: 0},
        compiler_params=_params(("arbitrary",)),
    )(proj, proj, proj, ln_g, ln_b, w_s, b_t, dy, dp)


QKV_BLOCK = 1


def _qkv_head(cq, ck, cv):
    q, k, v = _silu(cq), _silu(ck), _silu(cv)
    q = q * lax.rsqrt(jnp.sum(q * q, axis=-1, keepdims=True) + EPS) * np.float32(HEAD ** -0.5)
    k = k * lax.rsqrt(jnp.sum(k * k, axis=-1, keepdims=True) + EPS)
    return q, k, v


def _gate_fn(p, a_log, dt_bias):
    lane = lax.broadcasted_iota(jnp.int32, p.shape, 1)
    g = -jnp.exp(a_log) * jax.nn.softplus(p + dt_bias)
    return jnp.where(lane < 16, g, jnp.where(lane < N_GATE, jax.nn.sigmoid(p), 0.0))


def _shifted_rows(full, shift, start, rows):
    n = full.shape[0]
    assert start % HALO == 0 and 0 <= start + shift and start + shift + rows <= n
    rolled = full if shift == 0 else pltpu.roll(full, (-shift) % n, 0)
    return rolled[start:start + rows, :]


def _halo_maps(tile, s):
    per = tile // HALO
    prev = lambda i: jnp.maximum(i * per - 1, 0)
    nxt = lambda i: jnp.minimum((i + 1) * per, s // HALO - 1)
    return prev, nxt


def _prep_fwd(proj, conv_w, pg, a_log, dt_bias, *, tile=256):
    s = proj.shape[0]
    width = 3 * D_DN
    tile = min(tile, s)
    n_blk = s // tile
    prev, nxt = _halo_maps(tile, s)

    def body(x_ref, xp_ref, xn_ref, w_ref, pg_ref, al_ref, dt_ref, q_ref, k_ref, v_ref, gate_ref, ext):
        i = pl.program_id(0)
        ext[pl.ds(0, HALO), :] = jnp.where(i > 0, xp_ref[...], 0.0)
        ext[pl.ds(HALO, tile), :] = x_ref[...]
        ext[pl.ds(HALO + tile, HALO), :] = jnp.where(i < n_blk - 1, xn_ref[...], 0.0)
        full, c = ext[...], None
        for j in range(CONV_W):
            term = _shifted_rows(full, j - CONV_W // 2, HALO, tile) * w_ref[j:j + 1, :]
            c = term if c is None else c + term
        for h in range(N_GROUPS):
            q, k, v = _qkv_head(c[:, h * HEAD:(h + 1) * HEAD], c[:, D_DN + h * HEAD:D_DN + (h + 1) * HEAD],
                                c[:, 2 * D_DN + h * HEAD:2 * D_DN + (h + 1) * HEAD])
            q_ref[h] = q
            k_ref[h] = k
            v_ref[h] = v
        gate_ref[...] = _gate_fn(pg_ref[...], al_ref[...], dt_ref[...])

    hm = pl.BlockSpec((N_GROUPS, tile, HEAD), lambda i: (0, i, 0))
    lane_vec = pl.BlockSpec((1, LANE), lambda i: (0, 0))
    return pl.pallas_call(
        body, name="prep_fwd", grid=(n_blk,),
        in_specs=[pl.BlockSpec((tile, width), lambda i: (i, QKV_BLOCK)),
                  pl.BlockSpec((HALO, width), lambda i: (prev(i), QKV_BLOCK)),
                  pl.BlockSpec((HALO, width), lambda i: (nxt(i), QKV_BLOCK)),
                  pl.BlockSpec((HALO, width), lambda i: (0, 0)),
                  pl.BlockSpec((tile, LANE), lambda i: (i, 0)), lane_vec, lane_vec],
        out_specs=[hm, hm, hm, pl.BlockSpec((tile, LANE), lambda i: (i, 0))],
        out_shape=[jax.ShapeDtypeStruct((N_GROUPS, s, HEAD), F32)] * 3 + [jax.ShapeDtypeStruct((s, LANE), F32)],
        scratch_shapes=[pltpu.VMEM((tile + 2 * HALO, width), F32)],
        compiler_params=_params(("arbitrary",)),
    )(proj, proj, proj, conv_w, pg, a_log, dt_bias)


def _prep_bwd(proj, conv_w, pg, a_log, dt_bias, dq, dk, dv, dgates, dp, *, tile=128):
    s = proj.shape[0]
    width = 3 * D_DN
    tile = min(tile, s)
    n_blk = s // tile
    prev, nxt = _halo_maps(tile, s)
    ext_rows = tile + 2 * HALO
    pad = CONV_W // 2

    def body(x_ref, xp_ref, xn_ref, w_ref, pg_ref, al_ref, dt_ref,
             dq_ref, dqp_ref, dqn_ref, dk_ref, dkp_ref, dkn_ref, dv_ref, dvp_ref, dvn_ref,
             dg0_ref, dg1_ref, dg2_ref, _,
             dp_ref, dpg_ref, dw_ref, dal_ref, ddt_ref, xext, dcext):
        i = pl.program_id(0)
        has_prev, has_next = i > 0, i < n_blk - 1

        @pl.when(i == 0)
        def _():
            dw_ref[...] = jnp.zeros_like(dw_ref)
            dal_ref[...] = jnp.zeros_like(dal_ref)
            ddt_ref[...] = jnp.zeros_like(ddt_ref)

        zeros = jnp.zeros((HALO, width), F32)
        xext[pl.ds(0, HALO), :] = zeros
        xext[pl.ds(HALO, HALO), :] = jnp.where(has_prev, xp_ref[...], 0.0)
        xext[pl.ds(2 * HALO, tile), :] = x_ref[...]
        xext[pl.ds(2 * HALO + tile, HALO), :] = jnp.where(has_next, xn_ref[...], 0.0)
        xext[pl.ds(3 * HALO + tile, HALO), :] = zeros
        xfull, c = xext[...], None
        for j in range(CONV_W):
            term = _shifted_rows(xfull, j - pad, HALO, ext_rows) * w_ref[j:j + 1, :]
            c = term if c is None else c + term
        for h in range(N_GROUPS):
            lanes = [pl.ds(part * D_DN + h * HEAD, HEAD) for part in range(3)]
            cts = []
            for cur, before, after in ((dq_ref, dqp_ref, dqn_ref), (dk_ref, dkp_ref, dkn_ref), (dv_ref, dvp_ref, dvn_ref)):
                cts.append(jnp.concatenate([jnp.where(has_prev, before[h], 0.0), cur[h], jnp.where(has_next, after[h], 0.0)], axis=0))
            _, vjp = jax.vjp(_qkv_head, c[:, h * HEAD:(h + 1) * HEAD], c[:, D_DN + h * HEAD:D_DN + (h + 1) * HEAD],
                             c[:, 2 * D_DN + h * HEAD:2 * D_DN + (h + 1) * HEAD])
            for lane_sl, dc in zip(lanes, vjp(tuple(cts))):
                dcext[:, lane_sl] = dc
        dcfull = dcext[...]
        dx, dc_blk = None, dcfull[HALO:HALO + tile, :]
        for j in range(CONV_W):
            term = _shifted_rows(dcfull, pad - j, HALO, tile) * w_ref[j:j + 1, :]
            dx = term if dx is None else dx + term
            dw_ref[j:j + 1, :] += jnp.sum(dc_blk * _shifted_rows(xfull, j - pad, 2 * HALO, tile), axis=0, keepdims=True)
        dp_ref[...] = dx.astype(BF16)
        _, gvjp = jax.vjp(_gate_fn, pg_ref[...], al_ref[...], dt_ref[...])
        dpg, dal, ddt = gvjp(dg0_ref[...] + dg1_ref[...] + dg2_ref[...])
        dpg_ref[...] = dpg.astype(BF16)
        dal_ref[...] += dal
        ddt_ref[...] += ddt

    xs = [pl.BlockSpec((tile, width), lambda i: (i, QKV_BLOCK)), pl.BlockSpec((HALO, width), lambda i: (prev(i), QKV_BLOCK)),
          pl.BlockSpec((HALO, width), lambda i: (nxt(i), QKV_BLOCK))]
    hm = [pl.BlockSpec((N_GROUPS, tile, HEAD), lambda i: (0, i, 0)),
          pl.BlockSpec((N_GROUPS, HALO, HEAD), lambda i: (0, prev(i), 0)),
          pl.BlockSpec((N_GROUPS, HALO, HEAD), lambda i: (0, nxt(i), 0))]
    gate = pl.BlockSpec((tile, LANE), lambda i: (i, 0))
    lane_vec = pl.BlockSpec((1, LANE), lambda i: (0, 0))
    wspec = pl.BlockSpec((HALO, width), lambda i: (0, 0))
    n_in = 3 + 4 + 9 + 3
    return pl.pallas_call(
        body, name="prep_bwd", grid=(n_blk,),
        in_specs=xs + [wspec, gate, lane_vec, lane_vec] + hm * 3 + [gate] * 3 + [pl.BlockSpec(memory_space=pl.ANY)],
        out_specs=[pl.BlockSpec((tile, width), lambda i: (i, QKV_BLOCK)), gate, wspec, lane_vec, lane_vec],
        out_shape=[jax.ShapeDtypeStruct(dp.shape, BF16), jax.ShapeDtypeStruct((s, LANE), BF16),
                   jax.ShapeDtypeStruct((HALO, width), F32), jax.ShapeDtypeStruct((1, LANE), F32),
                   jax.ShapeDtypeStruct((1, LANE), F32)],
        input_output_aliases={n_in: 0},
        scratch_shapes=[pltpu.VMEM((tile + 4 * HALO, width), F32), pltpu.VMEM((ext_rows, width), F32)],
        compiler_params=_params(("arbitrary",)),
    )(proj, proj, proj, conv_w, pg, a_log, dt_bias, dq, dq, dq, dk, dk, dk, dv, dv, dv, *dgates, dp)


def _each(f, *cols):
    return tuple(f(*a) for a in zip(*cols))


def _delta_chunk(q, k, v, g_col, g_row, beta, state, t_known, reverse):
    n = DN_CHUNK
    ri = lax.broadcasted_iota(jnp.int32, (n, n), 0)
    ci = lax.broadcasted_iota(jnp.int32, (n, n), 1)
    if reverse:
        incl, incl_t, strict = ri <= ci, ri >= ci, ri < ci
    else:
        incl, incl_t, strict = ri >= ci, ri <= ci, ri > ci
    gc_col = _each(lambda gr: jnp.sum(jnp.where(incl, gr, 0.0), axis=1, keepdims=True), g_row)
    gc_row = _each(lambda gc: jnp.sum(jnp.where(incl_t, gc, 0.0), axis=0, keepdims=True), g_col)
    g_tot = _each(lambda gr: jnp.sum(gr, axis=1, keepdims=True), g_row)
    decay = _each(lambda a, b: jnp.where(incl, jnp.exp(jnp.where(incl, a - b, 0.0)), 0.0), gc_col, gc_row)
    kb = _each(lambda a, b: a * b, k, beta)
    kk = _each(_mm_nt, kb, k)
    m = _each(lambda a, d: jnp.where(strict, a * d, 0.0), kk, decay)
    if t_known is None:
        eye = (ri == ci).astype(F32)
        p = _each(lambda a: -a, m)
        t = _each(lambda a: eye + a, p)
        for _ in range(5):
            p = _each(lambda a: _bdot(a, a, ((1,), (0,))), p)
            t = _each(lambda a, b: a + _bdot(a, b, ((1,), (0,))), t, p)
    else:
        t = _each(_inverse_given, m, t_known)
    e_gc = _each(jnp.exp, gc_col)
    u = _each(lambda t_, v_, b_: _mm_nn(t_, v_ * b_), t, v, beta)
    w = _each(lambda t_, kb_, e_: _mm_nn(t_, kb_ * e_), t, kb, e_gc)
    attn = _each(lambda q_, k_, d_: _mm_nt(q_, k_) * d_, q, k, decay)
    ws = _each(_mm_nn, w, state)
    v_new = _each(lambda a, b: a - b, u, ws)
    qs = _each(lambda q_, e_, s_: _mm_nn(q_ * e_, s_), q, e_gc, state)
    av = _each(_mm_nn, attn, v_new)
    o = _each(lambda a, b: a + b, qs, av)
    kv = _each(lambda k_, gt, gc, vn: _mm_tn(k_ * jnp.exp(gt - gc), vn), k, g_tot, gc_col, v_new)
    new_state = _each(lambda s_, gt, kv_: s_ * jnp.exp(gt) + kv_, state, g_tot, kv)
    return o, new_state, t


def _delta_lanes(reverse):
    return (N_GROUPS, 3 * N_GROUPS) if reverse else (0, 2 * N_GROUPS)


def _delta_fwd(q, k, v, gates, g_rows, *, reverse, comm=None):
    s = q.shape[1]
    nc = s // DN_CHUNK
    g_lane, b_lane = _delta_lanes(reverse)
    chunk_of = (lambda c: nc - 1 - c) if reverse else (lambda c: c)
    heads = range(N_GROUPS)

    def body(q_ref, k_ref, v_ref, gate_ref, grow_ref, o_ref, st_ref, t_ref, state):
        @pl.when(pl.program_id(0) == 0)
        def _():
            state[...] = jnp.zeros_like(state)

        gb = gate_ref[...]
        prev = tuple(state[j] for j in heads)
        o, new, t = _delta_chunk(tuple(q_ref[j] for j in heads), tuple(k_ref[j] for j in heads), tuple(v_ref[j] for j in heads),
                                 tuple(gb[:, g_lane + j:g_lane + j + 1] for j in heads), tuple(grow_ref[0, j:j + 1, :] for j in heads),
                                 tuple(gb[:, b_lane + j:b_lane + j + 1] for j in heads), prev, None, reverse)
        for j in heads:
            st_ref[0, j] = prev[j]
            o_ref[j] = o[j]
            t_ref[0, j] = t[j]
            state[j] = new[j]

    hm = pl.BlockSpec((N_GROUPS, DN_CHUNK, HEAD), lambda c: (0, chunk_of(c), 0))
    name = ("delta_fwd_rev" if reverse else "delta_fwd") + ("" if comm is None else "_comm")
    return _call_with_comm(
        body, comm, name=name, grid=(nc,),
        in_specs=[hm, hm, hm, pl.BlockSpec((DN_CHUNK, LANE), lambda c: (chunk_of(c), 0)),
                  pl.BlockSpec((1, N_GROUPS, DN_CHUNK), lambda c: (chunk_of(c), 0, 0))],
        out_specs=[hm, pl.BlockSpec((1, N_GROUPS, HEAD, HEAD), lambda c: (chunk_of(c), 0, 0, 0)),
                   pl.BlockSpec((1, N_GROUPS, DN_CHUNK, DN_CHUNK), lambda c: (chunk_of(c), 0, 0, 0))],
        out_shape=[jax.ShapeDtypeStruct((N_GROUPS, s, HEAD), F32), jax.ShapeDtypeStruct((nc, N_GROUPS, HEAD, HEAD), F32),
                   jax.ShapeDtypeStruct((nc, N_GROUPS, DN_CHUNK, DN_CHUNK), F32)],
        scratch_shapes=[pltpu.VMEM((N_GROUPS, HEAD, HEAD), F32)],
        args=(q, k, v, gates, g_rows), compiler_params=_params(("arbitrary",)))


def _delta_bwd(q, k, v, gates, g_rows, states, t_inv, do, acc, *, reverse, comm=None):
    s = q.shape[1]
    nc = s // DN_CHUNK
    g_lane, b_lane = _delta_lanes(reverse)
    chunk_of = (lambda c: c) if reverse else (lambda c: nc - 1 - c)
    n_acc = 0 if acc is None else 3
    heads = range(N_GROUPS)

    def body(*refs):
        q_ref, k_ref, v_ref, gate_ref, grow_ref, st_ref, t_ref, do_ref = refs[:8]
        acc_refs = refs[8:8 + n_acc]
        dq_ref, dk_ref, dv_ref, dgate_ref, dgrow_ref, dstate = refs[8 + n_acc:]

        @pl.when(pl.program_id(0) == 0)
        def _():
            dstate[...] = jnp.zeros_like(dstate)

        gb = gate_ref[...]
        t_known = tuple(t_ref[0, j] for j in heads)
        f = lambda q_, k_, v_, gc_, gr_, b_, s_: _delta_chunk(q_, k_, v_, gc_, gr_, b_, s_, t_known, reverse)[:2]
        _, vjp = jax.vjp(f, tuple(q_ref[j] for j in heads), tuple(k_ref[j] for j in heads), tuple(v_ref[j] for j in heads),
                         tuple(gb[:, g_lane + j:g_lane + j + 1] for j in heads), tuple(grow_ref[0, j:j + 1, :] for j in heads),
                         tuple(gb[:, b_lane + j:b_lane + j + 1] for j in heads), tuple(st_ref[0, j] for j in heads))
        dq, dk, dv, dg_col, dg_row, dbeta, dprev = vjp((tuple(do_ref[j] for j in heads), tuple(dstate[j] for j in heads)))
        ids = lax.broadcasted_iota(jnp.int32, (DN_CHUNK, LANE), 1)
        dgate = jnp.zeros((DN_CHUNK, LANE), F32)
        for j in heads:
            if n_acc:
                dq_ref[j] = dq[j] + acc_refs[0][j]
                dk_ref[j] = dk[j] + acc_refs[1][j]
                dv_ref[j] = dv[j] + acc_refs[2][j]
            else:
                dq_ref[j] = dq[j]
                dk_ref[j] = dk[j]
                dv_ref[j] = dv[j]
            dstate[j] = dprev[j]
            dgate = dgate + jnp.where(ids == g_lane + j, dg_col[j], 0.0) + jnp.where(ids == b_lane + j, dbeta[j], 0.0)
            dgrow_ref[0, j:j + 1, :] = dg_row[j]
        dgate_ref[...] = dgate

    hm = pl.BlockSpec((N_GROUPS, DN_CHUNK, HEAD), lambda c: (0, chunk_of(c), 0))
    gate = pl.BlockSpec((DN_CHUNK, LANE), lambda c: (chunk_of(c), 0))
    rows = pl.BlockSpec((1, N_GROUPS, DN_CHUNK), lambda c: (chunk_of(c), 0, 0))
    st = pl.BlockSpec((1, N_GROUPS, HEAD, HEAD), lambda c: (chunk_of(c), 0, 0, 0))
    ts = pl.BlockSpec((1, N_GROUPS, DN_CHUNK, DN_CHUNK), lambda c: (chunk_of(c), 0, 0, 0))
    hm_shape = jax.ShapeDtypeStruct((N_GROUPS, s, HEAD), F32)
    name = ("delta_bwd_rev" if reverse else "delta_bwd") + ("" if comm is None else "_comm")
    return _call_with_comm(
        body, comm, name=name, grid=(nc,),
        in_specs=[hm, hm, hm, gate, rows, st, ts, hm] + [hm] * n_acc,
        out_specs=[hm, hm, hm, gate, rows],
        out_shape=[hm_shape] * 3 + [jax.ShapeDtypeStruct((s, LANE), F32), jax.ShapeDtypeStruct((nc, N_GROUPS, DN_CHUNK), F32)],
        scratch_shapes=[pltpu.VMEM((N_GROUPS, HEAD, HEAD), F32)],
        args=(q, k, v, gates, g_rows, states, t_inv, do, *(acc or ())), compiler_params=_params(("arbitrary",)))


ZB_BLOCK = 6


def _post_head(o_f, o_b, z, w):
    return _rms(o_f + o_b, w) * _silu(z)


def _post_fwd(o_f, o_b, proj, w, y, y_t, *, tile=256):
    s = proj.shape[0]
    tile = min(tile, s)

    def body(of_ref, ob_ref, z_ref, w_ref, _, __, y_ref, yt_ref):
        for h in range(N_GROUPS):
            lanes = pl.ds(h * HEAD, HEAD)
            y_h = _post_head(of_ref[h], ob_ref[h], z_ref[:, lanes], w_ref[...])
            y_ref[:, lanes] = y_h.astype(BF16)
            yt_ref[lanes, :] = y_h.T.astype(BF16)

    hm = pl.BlockSpec((N_GROUPS, tile, HEAD), lambda i: (0, i, 0))
    any_spec = pl.BlockSpec(memory_space=pl.ANY)
    return pl.pallas_call(
        body, name="post_fwd", grid=(s // tile,),
        in_specs=[hm, hm, pl.BlockSpec((tile, D_DN), lambda i: (i, ZB_BLOCK)), pl.BlockSpec((1, HEAD), lambda i: (0, 0)),
                  any_spec, any_spec],
        out_specs=[pl.BlockSpec((tile, D_DN), lambda i: (i, 1)), pl.BlockSpec((D_DN, tile), lambda i: (1, i))],
        out_shape=[jax.ShapeDtypeStruct(y.shape, BF16), jax.ShapeDtypeStruct(y_t.shape, BF16)],
        input_output_aliases={4: 0, 5: 1},
        compiler_params=_params(("arbitrary",)),
    )(o_f, o_b, proj, w, y, y_t)


def _post_bwd(o_f, o_b, proj, w, dy, *, tile=256):
    s = proj.shape[0]
    tile = min(tile, s)

    def body(of_ref, ob_ref, z_ref, w_ref, dy_ref, do_ref, dp_ref, dw_ref):
        @pl.when(pl.program_id(0) == 0)
        def _():
            dw_ref[...] = jnp.zeros_like(dw_ref)

        for h in range(N_GROUPS):
            lanes = pl.ds(h * HEAD, HEAD)
            _, vjp = jax.vjp(_post_head, of_ref[h], ob_ref[h], z_ref[:, lanes], w_ref[...])
            do, _, dz, dw = vjp(dy_ref[:, lanes])
            do_ref[h] = do
            dp_ref[:, lanes] = dz.astype(BF16)
            dw_ref[...] += dw

    hm = pl.BlockSpec((N_GROUPS, tile, HEAD), lambda i: (0, i, 0))
    zb = pl.BlockSpec((tile, D_DN), lambda i: (i, ZB_BLOCK))
    vec = pl.BlockSpec((1, HEAD), lambda i: (0, 0))
    return pl.pallas_call(
        body, name="post_bwd", grid=(s // tile,),
        in_specs=[hm, hm, zb, vec, pl.BlockSpec((tile, D_DN), lambda i: (i, 1))], out_specs=[hm, zb, vec],
        out_shape=[jax.ShapeDtypeStruct((N_GROUPS, s, HEAD), F32), jax.ShapeDtypeStruct((s, D_MAIN), BF16),
                   jax.ShapeDtypeStruct((1, HEAD), F32)],
        compiler_params=_params(("arbitrary",)),
    )(o_f, o_b, proj, w, dy)


def _my_index():
    return 4 * lax.axis_index("x") + 2 * lax.axis_index("y") + lax.axis_index("c")


def _peer(k):
    x, y, c = lax.axis_index("x"), lax.axis_index("y"), lax.axis_index("c")
    px, py, pc = x ^ ((k >> 2) & 1), y ^ ((k >> 1) & 1), c ^ (k & 1)
    return (px, py, pc), 4 * px + 2 * py + pc


class _Comm:
    def __init__(self, inputs, outputs, copies):
        self.inputs, self.outputs, self.copies = list(inputs), list(outputs), list(copies)
        self.kept = [o for o in self.outputs if not isinstance(o, jax.ShapeDtypeStruct)]

    def out_shapes(self):
        return [jax.ShapeDtypeStruct(o.shape, o.dtype) for o in self.outputs]

    def aliases(self, first_kept_input, first_output):
        kept_at = [i for i, o in enumerate(self.outputs) if not isinstance(o, jax.ShapeDtypeStruct)]
        return {first_kept_input + n: first_output + i for n, i in enumerate(kept_at)}

    def scratch(self):
        n = len(self.copies)
        return [pltpu.SemaphoreType.DMA((n, N_DEV - 1)), pltpu.SemaphoreType.DMA((n, N_DEV - 1)), pltpu.SemaphoreType.DMA((n,))]

    def _descriptors(self, in_refs, out_refs, sems, arrivals):
        send_sems, recv_sems, local_sems = sems
        me = _my_index()
        local, remote = [], []
        for k in range(N_DEV):
            peer, peer_idx = _peer(k)
            for a, (i_in, i_out, src_of, dst_of) in enumerate(self.copies):
                if k == 0:
                    if not arrivals:
                        local.append(pltpu.make_async_copy(src_of(in_refs[i_in], me), dst_of(out_refs[i_out], me), local_sems.at[a]))
                    continue
                receiver, sender = (me, peer_idx) if arrivals else (peer_idx, me)
                remote.append(pltpu.make_async_remote_copy(
                    src_ref=src_of(in_refs[i_in], receiver), dst_ref=dst_of(out_refs[i_out], sender),
                    send_sem=send_sems.at[a, k - 1], recv_sem=recv_sems.at[a, k - 1], device_id=peer,
                    device_id_type=pl.DeviceIdType.MESH))
        return local, remote

    def start(self, in_refs, out_refs, sems):
        local, sends = self._descriptors(in_refs, out_refs, sems, arrivals=False)
        for cp in local + sends:
            cp.start()

    def wait(self, in_refs, out_refs, sems):
        _, arriving = self._descriptors(in_refs, out_refs, sems, arrivals=True)
        for cp in arriving:
            cp.wait_recv()
        local, sends = self._descriptors(in_refs, out_refs, sems, arrivals=False)
        for cp in sends:
            cp.wait_send()
        for cp in local:
            cp.wait()


def _call_with_comm(body, comm, *, name, grid, in_specs, out_specs, out_shape, scratch_shapes, args, compiler_params):
    if comm is None:
        return pl.pallas_call(body, name=name, grid=grid, in_specs=in_specs, out_specs=out_specs, out_shape=out_shape,
                              scratch_shapes=scratch_shapes, compiler_params=compiler_params)(*args), []
    n_in, n_out, n_scratch = len(in_specs), len(out_specs), len(scratch_shapes)
    c_in, c_kept, c_out = len(comm.inputs), len(comm.kept), len(comm.outputs)
    last = grid[0] - 1

    def wrapped(*refs):
        main_in, comm_in = refs[:n_in], refs[n_in:n_in + c_in]
        o = n_in + c_in + c_kept
        main_out, comm_out = refs[o:o + n_out], refs[o + n_out:o + n_out + c_out]
        s = o + n_out + c_out
        main_scratch, sems = refs[s:s + n_scratch], refs[s + n_scratch:]

        @pl.when(pl.program_id(0) == 0)
        def _():
            comm.start(comm_in, comm_out, sems)

        body(*main_in, *main_out, *main_scratch)

        @pl.when(pl.program_id(0) == last)
        def _():
            comm.wait(comm_in, comm_out, sems)

    any_spec = pl.BlockSpec(memory_space=pl.ANY)
    results = pl.pallas_call(
        wrapped, name=name, grid=grid, in_specs=list(in_specs) + [any_spec] * (c_in + c_kept),
        out_specs=list(out_specs) + [any_spec] * c_out, out_shape=list(out_shape) + comm.out_shapes(),
        scratch_shapes=list(scratch_shapes) + comm.scratch(), input_output_aliases=comm.aliases(n_in + c_in, n_out),
        compiler_params=compiler_params,
    )(*args, *comm.inputs, *comm.kept)
    return results[:n_out], results[n_out:]


def _exchange(name, comm):
    c_in, c_kept, c_out = len(comm.inputs), len(comm.kept), len(comm.outputs)

    def body(*refs):
        in_refs, out_refs, sems = refs[:c_in], refs[c_in + c_kept:c_in + c_kept + c_out], refs[c_in + c_kept + c_out:]
        comm.start(in_refs, out_refs, sems)
        comm.wait(in_refs, out_refs, sems)

    any_spec = pl.BlockSpec(memory_space=pl.ANY)
    return pl.pallas_call(
        body, name=name, in_specs=[any_spec] * (c_in + c_kept), out_specs=[any_spec] * c_out, out_shape=comm.out_shapes(),
        scratch_shapes=comm.scratch(), input_output_aliases=comm.aliases(c_in, 0),
    )(*comm.inputs, *comm.kept)


def _whole(ref, _):
    return ref


def _slot(ref, dev):
    return ref.at[dev]


def _row_block(size, axis):
    def pick(ref, dev):
        start = pl.multiple_of(dev * size, size) if size % SUBLANES == 0 else dev * size
        return ref.at[tuple([slice(None)] * axis + [pl.ds(start, size)])]
    return pick


def _blocked(shape, block):
    nd = len(block)
    grid = tuple(shape[d] // block[d] for d in range(nd) if block[d] != shape[d])
    return grid, (lambda *i: tuple(i) + (0,) * (nd - len(grid)))


def _sum_parts(name, parts, block):
    n_parts = parts.shape[0]
    grid, idx = _blocked(parts.shape[1:], block)

    def body(p_ref, g_ref):
        g = p_ref[0].astype(F32)
        for j in range(1, n_parts):
            g = g + p_ref[j].astype(F32)
        g_ref[...] = g

    return pl.pallas_call(
        body, name=name, grid=grid, in_specs=[pl.BlockSpec((n_parts,) + tuple(block), lambda *i: (0,) + idx(*i))],
        out_specs=pl.BlockSpec(block, idx), out_shape=jax.ShapeDtypeStruct(parts.shape[1:], F32),
        compiler_params=_params(("arbitrary",) * len(grid)),
    )(parts)


def _adamw(name, parts, w, m, v, block):
    n_parts = parts.shape[0]
    nd = len(block)
    grid, _ = _blocked(w.shape, block)
    lead = len(grid)

    def body(p_ref, w_ref, m_ref, v_ref, g_ref, d_ref, mo_ref, vo_ref):
        g = p_ref[0].astype(F32)
        for j in range(1, n_parts):
            g = g + p_ref[j].astype(F32)
        m_new = ADAM_B1 * m_ref[...] + (1.0 - ADAM_B1) * g
        v_new = ADAM_B2 * v_ref[...] + (1.0 - ADAM_B2) * jnp.square(g)
        m_hat = m_new / np.float32(1.0 - ADAM_B1 ** ADAM_STEP)
        v_hat = v_new / np.float32(1.0 - ADAM_B2 ** ADAM_STEP)
        g_ref[...] = g
        d_ref[...] = -ADAM_LR * (m_hat / (jnp.sqrt(v_hat) + ADAM_EPS) + ADAM_WD * w_ref[...])
        mo_ref[...] = m_new
        vo_ref[...] = v_new

    idx = lambda *i: tuple(i) + (0,) * (nd - lead)
    spec = pl.BlockSpec(block, idx)
    pspec = pl.BlockSpec((n_parts,) + tuple(block), lambda *i: (0,) + idx(*i))
    n_blk = int(np.prod(block[:-1])) * (-(-block[-1] // LANE) * LANE)
    return pl.pallas_call(
        body, name=name, grid=grid, in_specs=[pspec, spec, spec, spec], out_specs=[spec] * 4,
        out_shape=[jax.ShapeDtypeStruct(w.shape, F32)] * 4,
        compiler_params=_params(("arbitrary",) * lead, 2 * (n_parts + 7) * n_blk * 4 + VMEM_MARGIN),
    )(parts, w, m, v)


def _to_flat(pieces, row_multiple):
    flat = jnp.concatenate([p.reshape(-1) for p in pieces])
    rows = -(-flat.shape[0] // FLAT_W)
    rows = -(-rows // row_multiple) * row_multiple
    return jnp.pad(flat, (0, rows * FLAT_W - flat.shape[0])).reshape(rows, FLAT_W)


def _lane_row(*pieces):
    flat = jnp.concatenate([p.reshape(-1) for p in pieces])
    return jnp.pad(flat, (0, LANE - flat.shape[0])).reshape(1, LANE)


def _rows_view(a):
    return a.reshape(a.shape[:-1] + (SUBLANES, LANE))


def _in_shard_t(a):
    return _rows_view(jnp.swapaxes(a, 1, 2))


def kernel(x, norm_w, w_in, sgu_ln_g, sgu_ln_b, sgu_w, sgu_b, conv_w, a_log_f, a_log_b, dt_bias_f, dt_bias_b, dn_norm_w, w_out, final_norm_w, loss_target, m_norm_w, m_w_in, m_sgu_ln_g, m_sgu_ln_b, m_sgu_w, m_sgu_b, m_conv_w, m_a_log_f, m_a_log_b, m_dt_bias_f, m_dt_bias_b, m_dn_norm_w, m_w_out, m_final_norm_w, v_norm_w, v_w_in, v_sgu_ln_g, v_sgu_ln_b, v_sgu_w, v_sgu_b, v_conv_w, v_a_log_f, v_a_log_b, v_dt_bias_f, v_dt_bias_b, v_dn_norm_w, v_w_out, v_final_norm_w):
    weights = dict(norm_w=norm_w, w_in=w_in, sgu_ln_g=sgu_ln_g, sgu_ln_b=sgu_ln_b, sgu_w=sgu_w, sgu_b=sgu_b, conv_w=conv_w,
                   a_log_f=a_log_f, a_log_b=a_log_b, dt_bias_f=dt_bias_f, dt_bias_b=dt_bias_b, dn_norm_w=dn_norm_w,
                   w_out=w_out, final_norm_w=final_norm_w)
    m_in = dict(norm_w=m_norm_w, w_in=m_w_in, sgu_ln_g=m_sgu_ln_g, sgu_ln_b=m_sgu_ln_b, sgu_w=m_sgu_w, sgu_b=m_sgu_b,
                conv_w=m_conv_w, a_log_f=m_a_log_f, a_log_b=m_a_log_b, dt_bias_f=m_dt_bias_f, dt_bias_b=m_dt_bias_b,
                dn_norm_w=m_dn_norm_w, w_out=m_w_out, final_norm_w=m_final_norm_w)
    v_in = dict(norm_w=v_norm_w, w_in=v_w_in, sgu_ln_g=v_sgu_ln_g, sgu_ln_b=v_sgu_ln_b, sgu_w=v_sgu_w, sgu_b=v_sgu_b,
                conv_w=v_conv_w, a_log_f=v_a_log_f, a_log_b=v_a_log_b, dt_bias_f=v_dt_bias_f, dt_bias_b=v_dt_bias_b,
                dn_norm_w=v_dn_norm_w, w_out=v_w_out, final_norm_w=v_final_norm_w)
    s = x.shape[1]
    nc = s // DN_CHUNK
    depth = w_in.shape[0]
    in_shard, out_shard, conv_shard = w_in.shape[2], w_out.shape[1], conv_w.shape[2]
    x0 = x.reshape(s, D_MODEL)
    target = loss_target.reshape(s, D_MODEL)

    w_in_shard = _in_shard_t(w_in.astype(BF16))
    w_out_shard = w_out.astype(BF16)

    gathered_in_shape = jax.ShapeDtypeStruct((N_DEV * in_shard, SUBLANES, LANE), BF16)
    gathered_out_shape = jax.ShapeDtypeStruct((D_MODEL, D_MODEL), BF16)
    gathered_in, gathered_out, conv_all = _exchange("gather_first", _Comm(
        [w_in_shard[0], w_out_shard[0], conv_w],
        [gathered_in_shape, gathered_out_shape, jax.ShapeDtypeStruct((N_DEV,) + conv_w.shape, F32)],
        [(0, 0, _whole, _row_block(in_shard, 0)), (1, 1, _whole, _row_block(out_shard, 0)), (2, 2, _whole, _slot)]))
    gather_in_plan = lambda l: _Comm([w_in_shard[l]], [gathered_in_shape], [(0, 0, _whole, _row_block(in_shard, 0))])
    gather_out_plan = lambda l: _Comm([w_out_shard[l]], [gathered_out_shape], [(0, 0, _whole, _row_block(out_shard, 0))])
    conv_full = conv_all.transpose(1, 2, 0, 3).reshape(depth, CONV_W, 3 * D_DN)
    conv_full = jnp.pad(conv_full, ((0, 0), (0, HALO - CONV_W), (0, 0)))
    sgu_b_t = jnp.swapaxes(sgu_b, 1, 2)

    def rows_of(gates, first_lane):
        return gates[:, first_lane:first_lane + N_GROUPS].reshape(nc, DN_CHUNK, N_GROUPS).transpose(0, 2, 1)

    saved = []
    xl = x0
    for l in range(depth):
        w_in_t = gathered_in.reshape(D_IN, D_MODEL)
        w_gate_t = jnp.pad(w_in_t[D_MAIN:], ((0, LANE - N_GATE), (0, 0)))
        w_out_l = gathered_out
        nw = norm_w[l].reshape(1, D_MODEL)
        h, h_t = _rms_fwd(xl, nw)
        proj = _matmul([(_lhs(h), _rhs(w_in_t, MM_WIDE, transposed=True))], s, D_MAIN, name="in_proj", trans_b=True, tn=MM_WIDE)
        p_gate = _matmul([(_lhs(h), _rhs(w_gate_t, LANE, transposed=True))], s, LANE, name="in_proj_gate", trans_b=True, tn=LANE)
        ln_g, ln_b = sgu_ln_g[l].reshape(1, D_SGU), sgu_ln_b[l].reshape(1, D_SGU)
        y, y_t = _sgu_fwd(proj, ln_g, ln_b, sgu_w[l], sgu_b_t[l])
        a_log = _lane_row(a_log_f[l], a_log_b[l])
        dt_bias = _lane_row(dt_bias_f[l], dt_bias_b[l])
        q, k, v, gates = _prep_fwd(proj, conv_full[l], p_gate, a_log, dt_bias)
        rows_f, rows_b = rows_of(gates, 0), rows_of(gates, N_GROUPS)
        more = l + 1 < depth
        (o_f, st_f, t_f), next_in = _delta_fwd(q, k, v, gates, rows_f, reverse=False, comm=gather_in_plan(l + 1) if more else None)
        (o_b, st_b, t_b), next_out = _delta_fwd(q, k, v, gates, rows_b, reverse=True, comm=gather_out_plan(l + 1) if more else None)
        dn_w = dn_norm_w[l].reshape(1, HEAD)
        y, y_t = _post_fwd(o_f, o_b, proj, dn_w, y, y_t)
        x_next = _matmul([(_lhs(y), _rhs(w_out_l, MM_WIDE))], s, D_MODEL, name="out_proj", add=xl, tn=MM_WIDE)
        saved.append(dict(x=xl, nw=nw, h_t=h_t, proj=proj, p_gate=p_gate, ln_g=ln_g, ln_b=ln_b, a_log=a_log, dt_bias=dt_bias,
                          q=q, k=k, v=v, gates=gates, rows_f=rows_f, rows_b=rows_b, o_f=o_f, o_b=o_b, st_f=st_f, st_b=st_b,
                          t_f=t_f, t_b=t_b, dn_w=dn_w, y_t=y_t, w_in_t=w_in_t, w_gate_t=w_gate_t, w_out=w_out_l))
        xl = x_next
        if more:
            (gathered_in,), (gathered_out,) = next_in, next_out

    loss_part, dx, dx_bf, d_final = _final_loss(xl, final_norm_w.reshape(1, D_MODEL), target)
    loss = lax.psum(loss_part[0, 0], MESH_AXES)

    small_names = [n for n in SMALL if n not in ("conv_w", "final_norm_w")]
    grads = {n: [None] * depth for n in small_names + ["conv_w"]}
    t_rows = -(-D_IN // LANE) * LANE
    layer_slot = lambda l: (lambda ref, dev: ref.at[dev, l])
    parts = [jax.ShapeDtypeStruct((N_DEV, depth, in_shard, SUBLANES, LANE), BF16),
             jax.ShapeDtypeStruct((N_DEV, depth, out_shard, D_MODEL), BF16)]

    def exchange_plan(layer, g_in_rows, g_out, more_inputs=(), more_outputs=(), more_copies=()):
        return _Comm([g_in_rows, g_out, *more_inputs], [*parts, *more_outputs],
                     [(0, 0, _row_block(in_shard, 0), layer_slot(layer)), (1, 1, _row_block(out_shard, 0), layer_slot(layer)),
                      *more_copies])

    pending = None
    for l in reversed(range(depth)):
        sv = saved[l]
        dy = _matmul([(_lhs(dx_bf), _rhs(sv["w_out"], MM_WIDE, transposed=True))], s, D_MODEL, name="out_proj_dy", trans_b=True,
                     tn=MM_WIDE)
        g_out = _matmul([(_lhs(sv["y_t"]), _rhs(dx_bf, MM_TILE))], D_MODEL, D_MODEL, name="out_proj_dw", out_dtype=BF16)
        do, dp, d_dn = _post_bwd(sv["o_f"], sv["o_b"], sv["proj"], sv["dn_w"], dy)
        (dq, dk, dv, dgate_f, drows_f), exchanged = _delta_bwd(
            sv["q"], sv["k"], sv["v"], sv["gates"], sv["rows_f"], sv["st_f"], sv["t_f"], do, None, reverse=False,
            comm=None if pending is None else exchange_plan(*pending))
        if exchanged:
            parts = list(exchanged)
        (dq, dk, dv, dgate_b, drows_b), _ = _delta_bwd(sv["q"], sv["k"], sv["v"], sv["gates"], sv["rows_b"], sv["st_b"], sv["t_b"],
                                                       do, (dq, dk, dv), reverse=True)
        drows = jnp.concatenate([drows_f.transpose(0, 2, 1).reshape(s, N_GROUPS), drows_b.transpose(0, 2, 1).reshape(s, N_GROUPS)], axis=1)
        drows = jnp.pad(drows, ((0, 0), (0, LANE - 2 * N_GROUPS)))
        dp, d_gate, d_conv, d_alog, d_dt = _prep_bwd(sv["proj"], conv_full[l], sv["p_gate"], sv["a_log"], sv["dt_bias"],
                                                     dq, dk, dv, (dgate_f, dgate_b, drows), dp)
        dp, d_lg, d_lb, d_sw, d_sbt = _sgu_bwd(sv["proj"], sv["ln_g"], sv["ln_b"], sgu_w[l], sgu_b_t[l], dy, dp)
        dh = _matmul([(_lhs(dp), _rhs(sv["w_in_t"], MM_TILE, k=D_MAIN)), (_lhs(d_gate), _rhs(sv["w_gate_t"], MM_TILE))],
                     s, D_MODEL, name="in_proj_dh")
        h_t = sv["h_t"]
        g_in_t = _matmul([(_lhs(h_t), _rhs(dp, MM_TILE))], D_MODEL, D_MAIN, name="in_proj_dw", out_dtype=BF16, out_t_rows=t_rows)
        g_in_t = _matmul([(_lhs(h_t), _rhs(d_gate, LANE))], D_MODEL, LANE, name="in_proj_dw_gate", out_dtype=BF16, tn=LANE,
                         transpose_into=g_in_t, out_row_block=D_MAIN // LANE)
        pending = (l, _rows_view(g_in_t), g_out)
        dx, dx_bf, d_nw = _rms_bwd(sv["x"], sv["nw"], dh, dx)
        grads["norm_w"][l] = d_nw.reshape(D_MODEL)
        grads["sgu_ln_g"][l], grads["sgu_ln_b"][l] = d_lg.reshape(D_SGU), d_lb.reshape(D_SGU)
        grads["sgu_w"][l], grads["sgu_b"][l] = d_sw, d_sbt.T
        grads["conv_w"][l] = d_conv[:CONV_W]
        grads["a_log_f"][l], grads["a_log_b"][l] = d_alog[0, :N_GROUPS], d_alog[0, N_GROUPS:2 * N_GROUPS]
        grads["dt_bias_f"][l], grads["dt_bias_b"][l] = d_dt[0, :N_GROUPS], d_dt[0, N_GROUPS:2 * N_GROUPS]
        grads["dn_norm_w"][l] = d_dn.reshape(HEAD)
    grad_x = dx.reshape(1, s, D_MODEL)

    g_small = {n: jnp.stack(grads[n]) for n in small_names}
    g_small["final_norm_w"] = d_final.reshape(D_MODEL)
    conv_by_dev = jnp.stack(grads["conv_w"]).reshape(depth, CONV_W, N_DEV, conv_shard).transpose(2, 0, 1, 3).reshape(N_DEV, -1)
    replicated = jnp.concatenate([g_small[n].reshape(-1) for n in SMALL if n != "conv_w"])
    n_small = conv_by_dev.shape[1] + replicated.shape[0]
    small_rows = -(-n_small // (FLAT_W * HALO)) * HALO
    send_small = jnp.concatenate([conv_by_dev, jnp.broadcast_to(replicated, (N_DEV, replicated.shape[0])),
                                  jnp.zeros((N_DEV, small_rows * FLAT_W - n_small), F32)], axis=1).reshape(N_DEV, small_rows, FLAT_W)
    parts_in, parts_out, parts_small = _exchange(
        "exchange_last", exchange_plan(*pending, [send_small], [jax.ShapeDtypeStruct((N_DEV, small_rows, FLAT_W), F32)],
                                       [(2, 2, _slot, _slot)]))

    in_rows = 100 if in_shard % 100 == 0 else in_shard
    g_in_t = _sum_parts("sum_in", parts_in, (1, in_rows, SUBLANES, LANE))
    g_in = jnp.swapaxes(g_in_t.reshape(depth, in_shard, D_MODEL), 1, 2)
    res_in = _adamw("adamw_in", g_in[None], w_in, m_w_in, v_w_in, (1, min(256, D_MODEL), in_shard))
    res_out = _adamw("adamw_out", parts_out, w_out, m_w_out, v_w_out, (1, min(128, out_shard), D_MODEL))
    res_small = _adamw("adamw_small", parts_small, _to_flat([weights[n] for n in SMALL], HALO), _to_flat([m_in[n] for n in SMALL], HALO),
                       _to_flat([v_in[n] for n in SMALL], HALO), (small_rows // 4 if small_rows % 32 == 0 else small_rows, FLAT_W))
    outs = [dict(w_in=a, w_out=b) for a, b in zip(res_in, res_out)]
    off = 0
    for n in SMALL:
        size = int(np.prod(weights[n].shape))
        for o, d in zip(res_small, outs):
            d[n] = o.reshape(-1)[off:off + size].reshape(weights[n].shape)
        off += size
    g_out, delta_out, m_out, v_out = outs
    return (loss, grad_x, *[g_out[n] for n in WEIGHTS], *[delta_out[n] for n in WEIGHTS],
            *[m_out[n] for n in WEIGHTS], *[v_out[n] for n in WEIGHTS])
```

```python
import functools

import numpy as np
import jax
import jax.numpy as jnp
from jax import lax
from jax.experimental import pallas as pl
from jax.experimental.pallas import tpu as pltpu

F32 = jnp.float32
BF16 = jnp.bfloat16

N_DEV = 8
D_MODEL = 2048
D_SGU = 1024
D_DN = 1024
N_GROUPS = 8
HEAD = 128
SGU_CHUNK = 128
DN_CHUNK = 64
CONV_W = 5
N_GATE = 32
D_MAIN = 3 * D_SGU + 4 * D_DN
D_IN = D_MAIN + N_GATE
LANE = 128
SUBLANES = 16
HALO = 8
EPS = 1e-6
ADAM_LR, ADAM_B1, ADAM_B2, ADAM_EPS, ADAM_WD, ADAM_STEP = 0.001, 0.9, 0.999, 1e-08, 0.01, 10
FLAT_W = 1024
MM_TILE = 512
MM_WIDE = 1024
VMEM_MARGIN = 8 << 20

MESH_AXES = ("x", "y", "c")
WEIGHTS = ("norm_w", "w_in", "sgu_ln_g", "sgu_ln_b", "sgu_w", "sgu_b", "conv_w", "a_log_f", "a_log_b",
           "dt_bias_f", "dt_bias_b", "dn_norm_w", "w_out", "final_norm_w")
SMALL = ("conv_w",) + tuple(n for n in WEIGHTS if n not in ("w_in", "w_out", "conv_w"))


def _bdot(a, b, dims):
    return lax.dot_general(a.astype(BF16), b.astype(BF16), (dims, ((), ())), preferred_element_type=F32)


@jax.custom_vjp
def _mm_nn(a, b):
    return _bdot(a, b, ((1,), (0,)))


def _mm_nn_fwd(a, b):
    return _mm_nn(a, b), (a, b)


def _mm_nn_bwd(res, ct):
    a, b = res
    return _bdot(ct, b, ((1,), (1,))), _bdot(a, ct, ((0,), (0,)))


_mm_nn.defvjp(_mm_nn_fwd, _mm_nn_bwd)


@jax.custom_vjp
def _mm_nt(a, b):
    return _bdot(a, b, ((1,), (1,)))


def _mm_nt_fwd(a, b):
    return _mm_nt(a, b), (a, b)


def _mm_nt_bwd(res, ct):
    a, b = res
    return _bdot(ct, b, ((1,), (0,))), _bdot(ct, a, ((0,), (0,)))


_mm_nt.defvjp(_mm_nt_fwd, _mm_nt_bwd)


@jax.custom_vjp
def _mm_tn(a, b):
    return _bdot(a, b, ((0,), (0,)))


def _mm_tn_fwd(a, b):
    return _mm_tn(a, b), (a, b)


def _mm_tn_bwd(res, ct):
    a, b = res
    return _bdot(b, ct, ((1,), (1,))), _bdot(a, ct, ((1,), (0,)))


_mm_tn.defvjp(_mm_tn_fwd, _mm_tn_bwd)


@jax.custom_vjp
def _inverse_given(m, t):
    return t


def _inverse_given_fwd(m, t):
    return t, t


def _inverse_given_bwd(t, ct):
    dm = -_bdot(t, _bdot(ct, t, ((1,), (1,))), ((0,), (0,)))
    return dm, jnp.zeros_like(t)


_inverse_given.defvjp(_inverse_given_fwd, _inverse_given_bwd)


@jax.custom_vjp
def _split_lanes(x):
    return tuple(x[:, i * LANE:(i + 1) * LANE] for i in range(x.shape[1] // LANE))


def _split_lanes_fwd(x):
    return _split_lanes(x), None


def _split_lanes_bwd(_, cts):
    return (jnp.concatenate(cts, axis=1),)


_split_lanes.defvjp(_split_lanes_fwd, _split_lanes_bwd)


def _silu(t):
    return t * jax.nn.sigmoid(t)


def _gelu(t):
    return 0.5 * t * (1.0 + lax.erf(t * np.float32(0.7071067811865476)))


def _rms(x, w):
    return x * lax.rsqrt(jnp.mean(x * x, axis=-1, keepdims=True) + EPS) * w


def _params(sem, vmem_bytes=None):
    kw = dict(dimension_semantics=sem)
    if vmem_bytes is not None:
        kw["vmem_limit_bytes"] = int(vmem_bytes)
    return pltpu.CompilerParams(**kw)


def _lhs(a, k=None, k_block=0):
    k = a.shape[1] if k is None else k
    tm = min(MM_TILE, a.shape[0])
    return a, pl.BlockSpec((tm, k), lambda i, j: (i, k_block)), tm * k * a.dtype.itemsize


def _rhs(b, tn, *, layer=None, transposed=False, k=None, k_block=0, n_offset=0):
    lead = () if layer is None else (layer,)
    none = () if layer is None else (None,)
    if transposed:
        k = b.shape[-1]
        spec = pl.BlockSpec(none + (tn, k), lambda i, j: lead + (j + n_offset, 0))
    else:
        k = b.shape[-2] if k is None else k
        spec = pl.BlockSpec(none + (k, tn), lambda i, j: lead + (k_block, j))
    return b, spec, tn * k * b.dtype.itemsize


def _matmul(pairs, m, n, *, name, trans_b=False, add=None, out_dtype=F32, tn=MM_TILE, out_t_rows=None, transpose_into=None,
            out_row_block=0, comm=None):
    tm = min(MM_TILE, m)
    assert m % tm == 0 and n % tn == 0
    n_pairs = len(pairs)
    contract = ((1,), (1,)) if trans_b else ((1,), (0,))
    transposed_out = out_t_rows is not None or transpose_into is not None

    def body(*refs):
        o_ref = refs[-1]
        acc = None
        for i in range(n_pairs):
            d = lax.dot_general(refs[2 * i][...].astype(BF16), refs[2 * i + 1][...].astype(BF16),
                                (contract, ((), ())), preferred_element_type=F32)
            acc = d if acc is None else acc + d
        if add is not None:
            acc = acc + refs[2 * n_pairs][...]
        o_ref[...] = (acc.T if transposed_out else acc).astype(out_dtype)

    in_specs, args, vmem, aliases = [], [], 0, {}
    for (a, a_spec, a_bytes), (b, b_spec, b_bytes) in pairs:
        in_specs += [a_spec, b_spec]
        args += [a, b]
        vmem += 2 * (a_bytes + b_bytes)
    if add is not None:
        in_specs.append(pl.BlockSpec((tm, tn), lambda i, j: (i, j)))
        args.append(add)
        vmem += 2 * tm * tn * 4
    vmem += 2 * tm * tn * jnp.dtype(out_dtype).itemsize + 3 * tm * tn * 4
    if transposed_out:
        out_spec = pl.BlockSpec((tn, tm), lambda i, j: (j + out_row_block, i))
        if transpose_into is not None:
            in_specs.append(pl.BlockSpec(memory_space=pl.ANY))
            args.append(transpose_into)
            aliases = {len(args) - 1: 0}
            out_shape = jax.ShapeDtypeStruct(transpose_into.shape, out_dtype)
        else:
            out_shape = jax.ShapeDtypeStruct((out_t_rows, m), out_dtype)
    else:
        out_spec = pl.BlockSpec((tm, tn), lambda i, j: (i, j))
        out_shape = jax.ShapeDtypeStruct((m, n), out_dtype)
    if comm is not None:
        assert not aliases
        (out,), sent = _call_with_comm(
            body, comm, name=name, grid=(m // tm, n // tn), in_specs=in_specs, out_specs=[out_spec], out_shape=[out_shape],
            scratch_shapes=[], args=args, compiler_params=_params(("arbitrary", "arbitrary"), vmem + VMEM_MARGIN))
        return out, sent
    return pl.pallas_call(
        body, name=name, grid=(m // tm, n // tn), in_specs=in_specs, out_specs=out_spec, out_shape=out_shape,
        input_output_aliases=aliases, compiler_params=_params(("parallel", "arbitrary"), vmem + VMEM_MARGIN),
    )(*args)


def _rms_fwd(x, w, *, tile=512):
    s, d = x.shape
    tile = min(tile, s)

    def body(x_ref, w_ref, h_ref, ht_ref):
        h = _rms(x_ref[...], w_ref[...])
        h_ref[...] = h.astype(BF16)
        ht_ref[...] = h.T.astype(BF16)

    return pl.pallas_call(
        body, name="rms_fwd", grid=(s // tile,),
        in_specs=[pl.BlockSpec((tile, d), lambda i: (i, 0)), pl.BlockSpec((1, d), lambda i: (0, 0))],
        out_specs=[pl.BlockSpec((tile, d), lambda i: (i, 0)), pl.BlockSpec((d, tile), lambda i: (0, i))],
        out_shape=[jax.ShapeDtypeStruct((s, d), BF16), jax.ShapeDtypeStruct((d, s), BF16)],
        compiler_params=_params(("arbitrary",)),
    )(x, w)


def _rms_bwd(x, w, dh, dx_out, *, tile=256):
    s, d = x.shape
    tile = min(tile, s)

    def body(x_ref, w_ref, dh_ref, dxo_ref, dx_ref, dxb_ref, dw_ref):
        _, vjp = jax.vjp(_rms, x_ref[...], w_ref[...])
        dxv, dwv = vjp(dh_ref[...])
        dx = dxo_ref[...] + dxv
        dx_ref[...] = dx
        dxb_ref[...] = dx.astype(BF16)

        @pl.when(pl.program_id(0) == 0)
        def _():
            dw_ref[...] = jnp.zeros_like(dw_ref)

        dw_ref[...] += dwv

    row = pl.BlockSpec((tile, d), lambda i: (i, 0))
    vec = pl.BlockSpec((1, d), lambda i: (0, 0))
    return pl.pallas_call(
        body, name="rms_bwd", grid=(s // tile,),
        in_specs=[row, vec, row, row], out_specs=[row, row, vec],
        out_shape=[jax.ShapeDtypeStruct((s, d), F32), jax.ShapeDtypeStruct((s, d), BF16), jax.ShapeDtypeStruct((1, d), F32)],
        compiler_params=_params(("arbitrary",)),
    )(x, w, dh, dx_out)


def _final_loss(x, w, target, *, tile=256):
    s, d = x.shape
    tile = min(tile, s)

    def body(x_ref, w_ref, t_ref, loss_ref, dx_ref, dxb_ref, dw_ref):
        def f(xv, wv):
            err = jnp.square(_rms(xv, wv) - t_ref[...])
            per_token = jnp.sum(err, axis=1, keepdims=True) * np.float32(1.0 / d)
            return 0.5 * jnp.sum(per_token, axis=0, keepdims=True)

        loss, vjp = jax.vjp(f, x_ref[...], w_ref[...])
        dxv, dwv = vjp(jnp.ones((1, 1), F32))
        dx_ref[...] = dxv
        dxb_ref[...] = dxv.astype(BF16)

        @pl.when(pl.program_id(0) == 0)
        def _():
            dw_ref[...] = jnp.zeros_like(dw_ref)
            loss_ref[...] = jnp.zeros_like(loss_ref)

        dw_ref[...] += dwv
        loss_ref[...] += jnp.broadcast_to(loss, (1, LANE))

    row = pl.BlockSpec((tile, d), lambda i: (i, 0))
    vec = pl.BlockSpec((1, d), lambda i: (0, 0))
    return pl.pallas_call(
        body, name="final_loss", grid=(s // tile,),
        in_specs=[row, vec, row], out_specs=[pl.BlockSpec((1, LANE), lambda i: (0, 0)), row, row, vec],
        out_shape=[jax.ShapeDtypeStruct((1, LANE), F32), jax.ShapeDtypeStruct((s, d), F32),
                   jax.ShapeDtypeStruct((s, d), BF16), jax.ShapeDtypeStruct((1, d), F32)],
        compiler_params=_params(("arbitrary",)),
    )(x, w, target)


def _sgu_chunk(u, v, z, ln_g, ln_b, ws, bcols):
    vg = _gelu(v)
    xc = vg - jnp.mean(vg, axis=-1, keepdims=True)
    vl = xc * lax.rsqrt(jnp.mean(xc * xc, axis=-1, keepdims=True) + EPS) * ln_g + ln_b
    sp = jnp.concatenate([_mm_nn(ws[g], vg_) + bcols[g] for g, vg_ in enumerate(_split_lanes(vl))], axis=1)
    return _gelu(u) * sp * _silu(z)


def _sgu_specs(tile):
    col = lambda j: pl.BlockSpec((tile, D_SGU), lambda i, j=j: (i, j))
    vec = pl.BlockSpec((1, D_SGU), lambda i: (0, 0))
    wspec = pl.BlockSpec((N_GROUPS, SGU_CHUNK, SGU_CHUNK), lambda i: (0, 0, 0))
    bspec = pl.BlockSpec((SGU_CHUNK, N_GROUPS), lambda i: (0, 0))
    return col, vec, wspec, bspec


def _sgu_fwd(proj, ln_g, ln_b, w_s, b_t, *, tile=256):
    s = proj.shape[0]
    tile = min(tile, s)
    col, vec, wspec, bspec = _sgu_specs(tile)

    def body(u_ref, v_ref, z_ref, g_ref, b_ref, w_ref, bt_ref, y_ref, yt_ref):
        ws = tuple(w_ref[g] for g in range(N_GROUPS))
        bcols = tuple(bt_ref[:, g:g + 1] for g in range(N_GROUPS))
        for c in range(tile // SGU_CHUNK):
            rows = pl.ds(c * SGU_CHUNK, SGU_CHUNK)
            y = _sgu_chunk(u_ref[rows, :], v_ref[rows, :], z_ref[rows, :], g_ref[...], b_ref[...], ws, bcols)
            y_ref[rows, :] = y.astype(BF16)
            yt_ref[:, rows] = y.T.astype(BF16)

    return pl.pallas_call(
        body, name="sgu_fwd", grid=(s // tile,),
        in_specs=[col(0), col(1), col(2), vec, vec, wspec, bspec],
        out_specs=[pl.BlockSpec((tile, D_SGU), lambda i: (i, 0)), pl.BlockSpec((D_SGU, tile), lambda i: (0, i))],
        out_shape=[jax.ShapeDtypeStruct((s, D_SGU + D_DN), BF16), jax.ShapeDtypeStruct((D_SGU + D_DN, s), BF16)],
        compiler_params=_params(("arbitrary",)),
    )(proj, proj, proj, ln_g, ln_b, w_s, b_t)


def _sgu_bwd(proj, ln_g, ln_b, w_s, b_t, dy, dp, *, tile=128):
    s = proj.shape[0]
    tile = min(tile, s)
    col, vec, wspec, bspec = _sgu_specs(tile)

    def body(u_ref, v_ref, z_ref, g_ref, b_ref, w_ref, bt_ref, dy_ref, _, dp_ref, dg_ref, db_ref, dw_ref, dbt_ref):
        @pl.when(pl.program_id(0) == 0)
        def _():
            dg_ref[...] = jnp.zeros_like(dg_ref)
            db_ref[...] = jnp.zeros_like(db_ref)
            dw_ref[...] = jnp.zeros_like(dw_ref)
            dbt_ref[...] = jnp.zeros_like(dbt_ref)

        ws = tuple(w_ref[g] for g in range(N_GROUPS))
        bcols = tuple(bt_ref[:, g:g + 1] for g in range(N_GROUPS))
        for c in range(tile // SGU_CHUNK):
            rows = pl.ds(c * SGU_CHUNK, SGU_CHUNK)
            _, vjp = jax.vjp(_sgu_chunk, u_ref[rows, :], v_ref[rows, :], z_ref[rows, :], g_ref[...], b_ref[...], ws, bcols)
            du, dv, dz, dg, db, dws, dbc = vjp(dy_ref[rows, :])
            dp_ref[rows, pl.ds(0, D_SGU)] = du.astype(BF16)
            dp_ref[rows, pl.ds(D_SGU, D_SGU)] = dv.astype(BF16)
            dp_ref[rows, pl.ds(2 * D_SGU, D_SGU)] = dz.astype(BF16)
            dg_ref[...] += dg
            db_ref[...] += db
            for g in range(N_GROUPS):
                dw_ref[g] += dws[g]
                dbt_ref[:, g:g + 1] += dbc[g]

    return pl.pallas_call(
        body, name="sgu_bwd", grid=(s // tile,),
        in_specs=[col(0), col(1), col(2), vec, vec, wspec, bspec, pl.BlockSpec((tile, D_SGU), lambda i: (i, 0)),
                  pl.BlockSpec(memory_space=pl.ANY)],
        out_specs=[pl.BlockSpec((tile, 3 * D_SGU), lambda i: (i, 0)), vec, vec, wspec, bspec],
        out_shape=[jax.ShapeDtypeStruct(dp.shape, BF16), jax.ShapeDtypeStruct((1, D_SGU), F32),
                   jax.ShapeDtypeStruct((1, D_SGU), F32), jax.ShapeDtypeStruct((N_GROUPS, SGU_CHUNK, SGU_CHUNK), F32),
                   jax.ShapeDtypeStruct((SGU_CHUNK, N_GROUPS), F32)],
        input_output_aliases={8: 0},
        compiler_params=_params(("arbitrary",)),
    )(proj, proj, proj, ln_g, ln_b, w_s, b_t, dy, dp)


QKV_BLOCK = 1


def _qkv_head(cq, ck, cv):
    q, k, v = _silu(cq), _silu(ck), _silu(cv)
    q = q * lax.rsqrt(jnp.sum(q * q, axis=-1, keepdims=True) + EPS) * np.float32(HEAD ** -0.5)
    k = k * lax.rsqrt(jnp.sum(k * k, axis=-1, keepdims=True) + EPS)
    return q, k, v


def _gate_fn(p, a_log, dt_bias):
    lane = lax.broadcasted_iota(jnp.int32, p.shape, 1)
    g = -jnp.exp(a_log) * jax.nn.softplus(p + dt_bias)
    return jnp.where(lane < 16, g, jnp.where(lane < N_GATE, jax.nn.sigmoid(p), 0.0))


def _shifted_rows(full, shift, start, rows):
    n = full.shape[0]
    assert start % HALO == 0 and 0 <= start + shift and start + shift + rows <= n
    rolled = full if shift == 0 else pltpu.roll(full, (-shift) % n, 0)
    return rolled[start:start + rows, :]


def _halo_maps(tile, s):
    per = tile // HALO
    prev = lambda i: jnp.maximum(i * per - 1, 0)
    nxt = lambda i: jnp.minimum((i + 1) * per, s // HALO - 1)
    return prev, nxt


def _prep_fwd(proj, conv_w, pg, a_log, dt_bias, *, tile=256):
    s = proj.shape[0]
    width = 3 * D_DN
    tile = min(tile, s)
    n_blk = s // tile
    prev, nxt = _halo_maps(tile, s)

    def body(x_ref, xp_ref, xn_ref, w_ref, pg_ref, al_ref, dt_ref, q_ref, k_ref, v_ref, gate_ref, ext):
        i = pl.program_id(0)
        ext[pl.ds(0, HALO), :] = jnp.where(i > 0, xp_ref[...], 0.0)
        ext[pl.ds(HALO, tile), :] = x_ref[...]
        ext[pl.ds(HALO + tile, HALO), :] = jnp.where(i < n_blk - 1, xn_ref[...], 0.0)
        full, c = ext[...], None
        for j in range(CONV_W):
            term = _shifted_rows(full, j - CONV_W // 2, HALO, tile) * w_ref[j:j + 1, :]
            c = term if c is None else c + term
        for h in range(N_GROUPS):
            q, k, v = _qkv_head(c[:, h * HEAD:(h + 1) * HEAD], c[:, D_DN + h * HEAD:D_DN + (h + 1) * HEAD],
                                c[:, 2 * D_DN + h * HEAD:2 * D_DN + (h + 1) * HEAD])
            q_ref[h] = q
            k_ref[h] = k
            v_ref[h] = v
        gate_ref[...] = _gate_fn(pg_ref[...], al_ref[...], dt_ref[...])

    hm = pl.BlockSpec((N_GROUPS, tile, HEAD), lambda i: (0, i, 0))
    lane_vec = pl.BlockSpec((1, LANE), lambda i: (0, 0))
    return pl.pallas_call(
        body, name="prep_fwd", grid=(n_blk,),
        in_specs=[pl.BlockSpec((tile, width), lambda i: (i, QKV_BLOCK)),
                  pl.BlockSpec((HALO, width), lambda i: (prev(i), QKV_BLOCK)),
                  pl.BlockSpec((HALO, width), lambda i: (nxt(i), QKV_BLOCK)),
                  pl.BlockSpec((HALO, width), lambda i: (0, 0)),
                  pl.BlockSpec((tile, LANE), lambda i: (i, 0)), lane_vec, lane_vec],
        out_specs=[hm, hm, hm, pl.BlockSpec((tile, LANE), lambda i: (i, 0))],
        out_shape=[jax.ShapeDtypeStruct((N_GROUPS, s, HEAD), F32)] * 3 + [jax.ShapeDtypeStruct((s, LANE), F32)],
        scratch_shapes=[pltpu.VMEM((tile + 2 * HALO, width), F32)],
        compiler_params=_params(("arbitrary",)),
    )(proj, proj, proj, conv_w, pg, a_log, dt_bias)


def _prep_bwd(proj, conv_w, pg, a_log, dt_bias, dq, dk, dv, dgates, dp, *, tile=128):
    s = proj.shape[0]
    width = 3 * D_DN
    tile = min(tile, s)
    n_blk = s // tile
    prev, nxt = _halo_maps(tile, s)
    ext_rows = tile + 2 * HALO
    pad = CONV_W // 2

    def body(x_ref, xp_ref, xn_ref, w_ref, pg_ref, al_ref, dt_ref,
             dq_ref, dqp_ref, dqn_ref, dk_ref, dkp_ref, dkn_ref, dv_ref, dvp_ref, dvn_ref,
             dg0_ref, dg1_ref, dg2_ref, _,
             dp_ref, dpg_ref, dw_ref, dal_ref, ddt_ref, xext, dcext):
        i = pl.program_id(0)
        has_prev, has_next = i > 0, i < n_blk - 1

        @pl.when(i == 0)
        def _():
            dw_ref[...] = jnp.zeros_like(dw_ref)
            dal_ref[...] = jnp.zeros_like(dal_ref)
            ddt_ref[...] = jnp.zeros_like(ddt_ref)

        zeros = jnp.zeros((HALO, width), F32)
        xext[pl.ds(0, HALO), :] = zeros
        xext[pl.ds(HALO, HALO), :] = jnp.where(has_prev, xp_ref[...], 0.0)
        xext[pl.ds(2 * HALO, tile), :] = x_ref[...]
        xext[pl.ds(2 * HALO + tile, HALO), :] = jnp.where(has_next, xn_ref[...], 0.0)
        xext[pl.ds(3 * HALO + tile, HALO), :] = zeros
        xfull, c = xext[...], None
        for j in range(CONV_W):
            term = _shifted_rows(xfull, j - pad, HALO, ext_rows) * w_ref[j:j + 1, :]
            c = term if c is None else c + term
        for h in range(N_GROUPS):
            lanes = [pl.ds(part * D_DN + h * HEAD, HEAD) for part in range(3)]
            cts = []
            for cur, before, after in ((dq_ref, dqp_ref, dqn_ref), (dk_ref, dkp_ref, dkn_ref), (dv_ref, dvp_ref, dvn_ref)):
                cts.append(jnp.concatenate([jnp.where(has_prev, before[h], 0.0), cur[h], jnp.where(has_next, after[h], 0.0)], axis=0))
            _, vjp = jax.vjp(_qkv_head, c[:, h * HEAD:(h + 1) * HEAD], c[:, D_DN + h * HEAD:D_DN + (h + 1) * HEAD],
                             c[:, 2 * D_DN + h * HEAD:2 * D_DN + (h + 1) * HEAD])
            for lane_sl, dc in zip(lanes, vjp(tuple(cts))):
                dcext[:, lane_sl] = dc
        dcfull = dcext[...]
        dx, dc_blk = None, dcfull[HALO:HALO + tile, :]
        for j in range(CONV_W):
            term = _shifted_rows(dcfull, pad - j, HALO, tile) * w_ref[j:j + 1, :]
            dx = term if dx is None else dx + term
            dw_ref[j:j + 1, :] += jnp.sum(dc_blk * _shifted_rows(xfull, j - pad, 2 * HALO, tile), axis=0, keepdims=True)
        dp_ref[...] = dx.astype(BF16)
        _, gvjp = jax.vjp(_gate_fn, pg_ref[...], al_ref[...], dt_ref[...])
        dpg, dal, ddt = gvjp(dg0_ref[...] + dg1_ref[...] + dg2_ref[...])
        dpg_ref[...] = dpg.astype(BF16)
        dal_ref[...] += dal
        ddt_ref[...] += ddt

    xs = [pl.BlockSpec((tile, width), lambda i: (i, QKV_BLOCK)), pl.BlockSpec((HALO, width), lambda i: (prev(i), QKV_BLOCK)),
          pl.BlockSpec((HALO, width), lambda i: (nxt(i), QKV_BLOCK))]
    hm = [pl.BlockSpec((N_GROUPS, tile, HEAD), lambda i: (0, i, 0)),
          pl.BlockSpec((N_GROUPS, HALO, HEAD), lambda i: (0, prev(i), 0)),
          pl.BlockSpec((N_GROUPS, HALO, HEAD), lambda i: (0, nxt(i), 0))]
    gate = pl.BlockSpec((tile, LANE), lambda i: (i, 0))
    lane_vec = pl.BlockSpec((1, LANE), lambda i: (0, 0))
    wspec = pl.BlockSpec((HALO, width), lambda i: (0, 0))
    n_in = 3 + 4 + 9 + 3
    return pl.pallas_call(
        body, name="prep_bwd", grid=(n_blk,),
        in_specs=xs + [wspec, gate, lane_vec, lane_vec] + hm * 3 + [gate] * 3 + [pl.BlockSpec(memory_space=pl.ANY)],
        out_specs=[pl.BlockSpec((tile, width), lambda i: (i, QKV_BLOCK)), gate, wspec, lane_vec, lane_vec],
        out_shape=[jax.ShapeDtypeStruct(dp.shape, BF16), jax.ShapeDtypeStruct((s, LANE), BF16),
                   jax.ShapeDtypeStruct((HALO, width), F32), jax.ShapeDtypeStruct((1, LANE), F32),
                   jax.ShapeDtypeStruct((1, LANE), F32)],
        input_output_aliases={n_in: 0},
        scratch_shapes=[pltpu.VMEM((tile + 4 * HALO, width), F32), pltpu.VMEM((ext_rows, width), F32)],
        compiler_params=_params(("arbitrary",)),
    )(proj, proj, proj, conv_w, pg, a_log, dt_bias, dq, dq, dq, dk, dk, dk, dv, dv, dv, *dgates, dp)


def _each(f, *cols):
    return tuple(f(*a) for a in zip(*cols))


def _delta_chunk(q, k, v, g_col, g_row, beta, state, t_known, reverse):
    n = DN_CHUNK
    ri = lax.broadcasted_iota(jnp.int32, (n, n), 0)
    ci = lax.broadcasted_iota(jnp.int32, (n, n), 1)
    masks = {False: (ri >= ci, ri <= ci, ri > ci), True: (ri <= ci, ri >= ci, ri < ci)}
    flags = tuple(reverse) if isinstance(reverse, (tuple, list)) else (reverse,) * len(q)
    incl, incl_t, strict = (tuple(masks[f][i] for f in flags) for i in range(3))
    gc_col = _each(lambda gr, mk: jnp.sum(jnp.where(mk, gr, 0.0), axis=1, keepdims=True), g_row, incl)
    gc_row = _each(lambda gc, mk: jnp.sum(jnp.where(mk, gc, 0.0), axis=0, keepdims=True), g_col, incl_t)
    g_tot = _each(lambda gr: jnp.sum(gr, axis=1, keepdims=True), g_row)
    decay = _each(lambda a, b, mk: jnp.where(mk, jnp.exp(jnp.where(mk, a - b, 0.0)), 0.0), gc_col, gc_row, incl)
    kb = _each(lambda a, b: a * b, k, beta)
    kk = _each(_mm_nt, kb, k)
    m = _each(lambda a, d, mk: jnp.where(mk, a * d, 0.0), kk, decay, strict)
    if t_known is None:
        eye = (ri == ci).astype(F32)
        p = _each(lambda a: -a, m)
        t = _each(lambda a: eye + a, p)
        for _ in range(5):
            p = _each(lambda a: _bdot(a, a, ((1,), (0,))), p)
            t = _each(lambda a, b: a + _bdot(a, b, ((1,), (0,))), t, p)
    else:
        t = _each(_inverse_given, m, t_known)
    e_gc = _each(jnp.exp, gc_col)
    u = _each(lambda t_, v_, b_: _mm_nn(t_, v_ * b_), t, v, beta)
    w = _each(lambda t_, kb_, e_: _mm_nn(t_, kb_ * e_), t, kb, e_gc)
    attn = _each(lambda q_, k_, d_: _mm_nt(q_, k_) * d_, q, k, decay)
    ws = _each(_mm_nn, w, state)
    v_new = _each(lambda a, b: a - b, u, ws)
    qs = _each(lambda q_, e_, s_: _mm_nn(q_ * e_, s_), q, e_gc, state)
    av = _each(_mm_nn, attn, v_new)
    o = _each(lambda a, b: a + b, qs, av)
    kv = _each(lambda k_, gt, gc, vn: _mm_tn(k_ * jnp.exp(gt - gc), vn), k, g_tot, gc_col, v_new)
    new_state = _each(lambda s_, gt, kv_: s_ * jnp.exp(gt) + kv_, state, g_tot, kv)
    return o, new_state, t


def _delta_lanes(reverse):
    return (N_GROUPS, 3 * N_GROUPS) if reverse else (0, 2 * N_GROUPS)


def _delta_fwd(q, k, v, gates, rows_f, rows_b, *, comm=None):
    s = q.shape[1]
    nc = s // DN_CHUNK
    heads = range(N_GROUPS)

    def body(qf_ref, kf_ref, vf_ref, gf_ref, rf_ref, qb_ref, kb_ref, vb_ref, gb_ref, rb_ref,
             of_ref, sf_ref, tf_ref, ob_ref, sb_ref, tb_ref, state):
        @pl.when(pl.program_id(0) == 0)
        def _():
            state[...] = jnp.zeros_like(state)

        qs, ks, vs, g_cols, g_rows, betas, flags = [], [], [], [], [], [], []
        for reverse, q_ref, k_ref, v_ref, gate, row_ref in ((False, qf_ref, kf_ref, vf_ref, gf_ref[...], rf_ref),
                                                            (True, qb_ref, kb_ref, vb_ref, gb_ref[...], rb_ref)):
            g_lane, b_lane = _delta_lanes(reverse)
            for j in heads:
                qs.append(q_ref[j])
                ks.append(k_ref[j])
                vs.append(v_ref[j])
                g_cols.append(gate[:, g_lane + j:g_lane + j + 1])
                g_rows.append(row_ref[0, j:j + 1, :])
                betas.append(gate[:, b_lane + j:b_lane + j + 1])
                flags.append(reverse)
        prev = tuple(state[n] for n in range(2 * N_GROUPS))
        o, new, t = _delta_chunk(tuple(qs), tuple(ks), tuple(vs), tuple(g_cols), tuple(g_rows), tuple(betas), prev, None, tuple(flags))
        for n in range(2 * N_GROUPS):
            o_ref, st_ref, t_ref = (of_ref, sf_ref, tf_ref) if n < N_GROUPS else (ob_ref, sb_ref, tb_ref)
            j = n % N_GROUPS
            st_ref[0, j] = prev[n]
            o_ref[j] = o[n]
            t_ref[0, j] = t[n]
            state[n] = new[n]

    def specs(chunk_of):
        hm = pl.BlockSpec((N_GROUPS, DN_CHUNK, HEAD), lambda c: (0, chunk_of(c), 0))
        ins = [hm, hm, hm, pl.BlockSpec((DN_CHUNK, LANE), lambda c: (chunk_of(c), 0)),
               pl.BlockSpec((1, N_GROUPS, DN_CHUNK), lambda c: (chunk_of(c), 0, 0))]
        outs = [hm, pl.BlockSpec((1, N_GROUPS, HEAD, HEAD), lambda c: (chunk_of(c), 0, 0, 0)),
                pl.BlockSpec((1, N_GROUPS, DN_CHUNK, DN_CHUNK), lambda c: (chunk_of(c), 0, 0, 0))]
        return ins, outs

    (ins_f, outs_f), (ins_b, outs_b) = specs(lambda c: c), specs(lambda c: nc - 1 - c)
    shapes = [jax.ShapeDtypeStruct((N_GROUPS, s, HEAD), F32), jax.ShapeDtypeStruct((nc, N_GROUPS, HEAD, HEAD), F32),
              jax.ShapeDtypeStruct((nc, N_GROUPS, DN_CHUNK, DN_CHUNK), F32)]
    results, sent = _call_with_comm(
        body, comm, name="delta_fwd" + ("" if comm is None else "_comm"), grid=(nc,),
        in_specs=ins_f + ins_b, out_specs=outs_f + outs_b, out_shape=shapes + shapes,
        scratch_shapes=[pltpu.VMEM((2 * N_GROUPS, HEAD, HEAD), F32)],
        args=(q, k, v, gates, rows_f, q, k, v, gates, rows_b), compiler_params=_params(("arbitrary",)))
    return results[:3], results[3:], sent


def _delta_bwd(q, k, v, gates, g_rows, states, t_inv, do, acc, *, reverse, comm=None):
    s = q.shape[1]
    nc = s // DN_CHUNK
    g_lane, b_lane = _delta_lanes(reverse)
    chunk_of = (lambda c: c) if reverse else (lambda c: nc - 1 - c)
    n_acc = 0 if acc is None else 3
    heads = range(N_GROUPS)

    def body(*refs):
        q_ref, k_ref, v_ref, gate_ref, grow_ref, st_ref, t_ref, do_ref = refs[:8]
        acc_refs = refs[8:8 + n_acc]
        dq_ref, dk_ref, dv_ref, dgate_ref, dgrow_ref, dstate = refs[8 + n_acc:]

        @pl.when(pl.program_id(0) == 0)
        def _():
            dstate[...] = jnp.zeros_like(dstate)

        gb = gate_ref[...]
        t_known = tuple(t_ref[0, j] for j in heads)
        f = lambda q_, k_, v_, gc_, gr_, b_, s_: _delta_chunk(q_, k_, v_, gc_, gr_, b_, s_, t_known, reverse)[:2]
        _, vjp = jax.vjp(f, tuple(q_ref[j] for j in heads), tuple(k_ref[j] for j in heads), tuple(v_ref[j] for j in heads),
                         tuple(gb[:, g_lane + j:g_lane + j + 1] for j in heads), tuple(grow_ref[0, j:j + 1, :] for j in heads),
                         tuple(gb[:, b_lane + j:b_lane + j + 1] for j in heads), tuple(st_ref[0, j] for j in heads))
        dq, dk, dv, dg_col, dg_row, dbeta, dprev = vjp((tuple(do_ref[j] for j in heads), tuple(dstate[j] for j in heads)))
        ids = lax.broadcasted_iota(jnp.int32, (DN_CHUNK, LANE), 1)
        dgate = jnp.zeros((DN_CHUNK, LANE), F32)
        for j in heads:
            if n_acc:
                dq_ref[j] = dq[j] + acc_refs[0][j]
                dk_ref[j] = dk[j] + acc_refs[1][j]
                dv_ref[j] = dv[j] + acc_refs[2][j]
            else:
                dq_ref[j] = dq[j]
                dk_ref[j] = dk[j]
                dv_ref[j] = dv[j]
            dstate[j] = dprev[j]
            dgate = dgate + jnp.where(ids == g_lane + j, dg_col[j], 0.0) + jnp.where(ids == b_lane + j, dbeta[j], 0.0)
            dgrow_ref[0, j:j + 1, :] = dg_row[j]
        dgate_ref[...] = dgate

    hm = pl.BlockSpec((N_GROUPS, DN_CHUNK, HEAD), lambda c: (0, chunk_of(c), 0))
    gate = pl.BlockSpec((DN_CHUNK, LANE), lambda c: (chunk_of(c), 0))
    rows = pl.BlockSpec((1, N_GROUPS, DN_CHUNK), lambda c: (chunk_of(c), 0, 0))
    st = pl.BlockSpec((1, N_GROUPS, HEAD, HEAD), lambda c: (chunk_of(c), 0, 0, 0))
    ts = pl.BlockSpec((1, N_GROUPS, DN_CHUNK, DN_CHUNK), lambda c: (chunk_of(c), 0, 0, 0))
    hm_shape = jax.ShapeDtypeStruct((N_GROUPS, s, HEAD), F32)
    name = ("delta_bwd_rev" if reverse else "delta_bwd") + ("" if comm is None else "_comm")
    return _call_with_comm(
        body, comm, name=name, grid=(nc,),
        in_specs=[hm, hm, hm, gate, rows, st, ts, hm] + [hm] * n_acc,
        out_specs=[hm, hm, hm, gate, rows],
        out_shape=[hm_shape] * 3 + [jax.ShapeDtypeStruct((s, LANE), F32), jax.ShapeDtypeStruct((nc, N_GROUPS, DN_CHUNK), F32)],
        scratch_shapes=[pltpu.VMEM((N_GROUPS, HEAD, HEAD), F32)],
        args=(q, k, v, gates, g_rows, states, t_inv, do, *(acc or ())), compiler_params=_params(("arbitrary",)))


ZB_BLOCK = 6


def _post_head(o_f, o_b, z, w):
    return _rms(o_f + o_b, w) * _silu(z)


def _post_fwd(o_f, o_b, proj, w, y, y_t, *, tile=256):
    s = proj.shape[0]
    tile = min(tile, s)

    def body(of_ref, ob_ref, z_ref, w_ref, _, __, y_ref, yt_ref):
        for h in range(N_GROUPS):
            lanes = pl.ds(h * HEAD, HEAD)
            y_h = _post_head(of_ref[h], ob_ref[h], z_ref[:, lanes], w_ref[...])
            y_ref[:, lanes] = y_h.astype(BF16)
            yt_ref[lanes, :] = y_h.T.astype(BF16)

    hm = pl.BlockSpec((N_GROUPS, tile, HEAD), lambda i: (0, i, 0))
    any_spec = pl.BlockSpec(memory_space=pl.ANY)
    return pl.pallas_call(
        body, name="post_fwd", grid=(s // tile,),
        in_specs=[hm, hm, pl.BlockSpec((tile, D_DN), lambda i: (i, ZB_BLOCK)), pl.BlockSpec((1, HEAD), lambda i: (0, 0)),
                  any_spec, any_spec],
        out_specs=[pl.BlockSpec((tile, D_DN), lambda i: (i, 1)), pl.BlockSpec((D_DN, tile), lambda i: (1, i))],
        out_shape=[jax.ShapeDtypeStruct(y.shape, BF16), jax.ShapeDtypeStruct(y_t.shape, BF16)],
        input_output_aliases={4: 0, 5: 1},
        compiler_params=_params(("arbitrary",)),
    )(o_f, o_b, proj, w, y, y_t)


def _post_bwd(o_f, o_b, proj, w, dy, *, tile=256):
    s = proj.shape[0]
    tile = min(tile, s)

    def body(of_ref, ob_ref, z_ref, w_ref, dy_ref, do_ref, dp_ref, dw_ref):
        @pl.when(pl.program_id(0) == 0)
        def _():
            dw_ref[...] = jnp.zeros_like(dw_ref)

        for h in range(N_GROUPS):
            lanes = pl.ds(h * HEAD, HEAD)
            _, vjp = jax.vjp(_post_head, of_ref[h], ob_ref[h], z_ref[:, lanes], w_ref[...])
            do, _, dz, dw = vjp(dy_ref[:, lanes])
            do_ref[h] = do
            dp_ref[:, lanes] = dz.astype(BF16)
            dw_ref[...] += dw

    hm = pl.BlockSpec((N_GROUPS, tile, HEAD), lambda i: (0, i, 0))
    zb = pl.BlockSpec((tile, D_DN), lambda i: (i, ZB_BLOCK))
    vec = pl.BlockSpec((1, HEAD), lambda i: (0, 0))
    return pl.pallas_call(
        body, name="post_bwd", grid=(s // tile,),
        in_specs=[hm, hm, zb, vec, pl.BlockSpec((tile, D_DN), lambda i: (i, 1))], out_specs=[hm, zb, vec],
        out_shape=[jax.ShapeDtypeStruct((N_GROUPS, s, HEAD), F32), jax.ShapeDtypeStruct((s, D_MAIN), BF16),
                   jax.ShapeDtypeStruct((1, HEAD), F32)],
        compiler_params=_params(("arbitrary",)),
    )(o_f, o_b, proj, w, dy)


def _my_index():
    return 4 * lax.axis_index("x") + 2 * lax.axis_index("y") + lax.axis_index("c")


def _peer(k):
    x, y, c = lax.axis_index("x"), lax.axis_index("y"), lax.axis_index("c")
    px, py, pc = x ^ ((k >> 2) & 1), y ^ ((k >> 1) & 1), c ^ (k & 1)
    return (px, py, pc), 4 * px + 2 * py + pc


class _Comm:
    def __init__(self, inputs, outputs, copies):
        self.inputs, self.outputs, self.copies = list(inputs), list(outputs), list(copies)
        self.kept = [o for o in self.outputs if not isinstance(o, jax.ShapeDtypeStruct)]

    def out_shapes(self):
        return [jax.ShapeDtypeStruct(o.shape, o.dtype) for o in self.outputs]

    def aliases(self, first_kept_input, first_output):
        kept_at = [i for i, o in enumerate(self.outputs) if not isinstance(o, jax.ShapeDtypeStruct)]
        return {first_kept_input + n: first_output + i for n, i in enumerate(kept_at)}

    def scratch(self):
        n = len(self.copies)
        return [pltpu.SemaphoreType.DMA((n, N_DEV - 1)), pltpu.SemaphoreType.DMA((n, N_DEV - 1)), pltpu.SemaphoreType.DMA((n,))]

    def _descriptors(self, in_refs, out_refs, sems, arrivals):
        send_sems, recv_sems, local_sems = sems
        me = _my_index()
        local, remote = [], []
        for k in range(N_DEV):
            peer, peer_idx = _peer(k)
            for a, (i_in, i_out, src_of, dst_of) in enumerate(self.copies):
                if k == 0:
                    if not arrivals:
                        local.append(pltpu.make_async_copy(src_of(in_refs[i_in], me), dst_of(out_refs[i_out], me), local_sems.at[a]))
                    continue
                receiver, sender = (me, peer_idx) if arrivals else (peer_idx, me)
                remote.append(pltpu.make_async_remote_copy(
                    src_ref=src_of(in_refs[i_in], receiver), dst_ref=dst_of(out_refs[i_out], sender),
                    send_sem=send_sems.at[a, k - 1], recv_sem=recv_sems.at[a, k - 1], device_id=peer,
                    device_id_type=pl.DeviceIdType.MESH))
        return local, remote

    def start(self, in_refs, out_refs, sems):
        local, sends = self._descriptors(in_refs, out_refs, sems, arrivals=False)
        for cp in local + sends:
            cp.start()

    def wait(self, in_refs, out_refs, sems):
        _, arriving = self._descriptors(in_refs, out_refs, sems, arrivals=True)
        for cp in arriving:
            cp.wait_recv()
        local, sends = self._descriptors(in_refs, out_refs, sems, arrivals=False)
        for cp in sends:
            cp.wait_send()
        for cp in local:
            cp.wait()


def _call_with_comm(body, comm, *, name, grid, in_specs, out_specs, out_shape, scratch_shapes, args, compiler_params):
    if comm is None:
        return pl.pallas_call(body, name=name, grid=grid, in_specs=in_specs, out_specs=out_specs, out_shape=out_shape,
                              scratch_shapes=scratch_shapes, compiler_params=compiler_params)(*args), []
    n_in, n_out, n_scratch = len(in_specs), len(out_specs), len(scratch_shapes)
    c_in, c_kept, c_out = len(comm.inputs), len(comm.kept), len(comm.outputs)

    def at_step(position):
        here = [pl.program_id(d) == p for d, p in enumerate(position)]
        return here[0] if len(here) == 1 else functools.reduce(jnp.logical_and, here)

    def wrapped(*refs):
        main_in, comm_in = refs[:n_in], refs[n_in:n_in + c_in]
        o = n_in + c_in + c_kept
        main_out, comm_out = refs[o:o + n_out], refs[o + n_out:o + n_out + c_out]
        s = o + n_out + c_out
        main_scratch, sems = refs[s:s + n_scratch], refs[s + n_scratch:]

        @pl.when(at_step([0] * len(grid)))
        def _():
            comm.start(comm_in, comm_out, sems)

        body(*main_in, *main_out, *main_scratch)

        @pl.when(at_step([g - 1 for g in grid]))
        def _():
            comm.wait(comm_in, comm_out, sems)

    any_spec = pl.BlockSpec(memory_space=pl.ANY)
    results = pl.pallas_call(
        wrapped, name=name, grid=grid, in_specs=list(in_specs) + [any_spec] * (c_in + c_kept),
        out_specs=list(out_specs) + [any_spec] * c_out, out_shape=list(out_shape) + comm.out_shapes(),
        scratch_shapes=list(scratch_shapes) + comm.scratch(), input_output_aliases=comm.aliases(n_in + c_in, n_out),
        compiler_params=compiler_params,
    )(*args, *comm.inputs, *comm.kept)
    return results[:n_out], results[n_out:]


def _exchange(name, comm):
    c_in, c_kept, c_out = len(comm.inputs), len(comm.kept), len(comm.outputs)

    def body(*refs):
        in_refs, out_refs, sems = refs[:c_in], refs[c_in + c_kept:c_in + c_kept + c_out], refs[c_in + c_kept + c_out:]
        comm.start(in_refs, out_refs, sems)
        comm.wait(in_refs, out_refs, sems)

    any_spec = pl.BlockSpec(memory_space=pl.ANY)
    return pl.pallas_call(
        body, name=name, in_specs=[any_spec] * (c_in + c_kept), out_specs=[any_spec] * c_out, out_shape=comm.out_shapes(),
        scratch_shapes=comm.scratch(), input_output_aliases=comm.aliases(c_in, 0),
    )(*comm.inputs, *comm.kept)


def _whole(ref, _):
    return ref


def _slot(ref, dev):
    return ref.at[dev]


def _row_block(size, axis):
    def pick(ref, dev):
        start = pl.multiple_of(dev * size, size) if size % SUBLANES == 0 else dev * size
        return ref.at[tuple([slice(None)] * axis + [pl.ds(start, size)])]
    return pick


def _blocked(shape, block):
    nd = len(block)
    grid = tuple(shape[d] // block[d] for d in range(nd) if block[d] != shape[d])
    return grid, (lambda *i: tuple(i) + (0,) * (nd - len(grid)))


def _sum_parts(name, parts, block, comm=None):
    n_parts = parts.shape[0]
    grid, idx = _blocked(parts.shape[1:], block)

    def body(p_ref, g_ref):
        g = p_ref[0].astype(F32)
        for j in range(1, n_parts):
            g = g + p_ref[j].astype(F32)
        g_ref[...] = g

    (total,), sent = _call_with_comm(
        body, comm, name=name, grid=grid, in_specs=[pl.BlockSpec((n_parts,) + tuple(block), lambda *i: (0,) + idx(*i))],
        out_specs=[pl.BlockSpec(block, idx)], out_shape=[jax.ShapeDtypeStruct(parts.shape[1:], F32)], scratch_shapes=[],
        args=(parts,), compiler_params=_params(("arbitrary",) * len(grid)))
    return total, sent


def _adamw(name, parts, w, m, v, block):
    n_parts = parts.shape[0]
    nd = len(block)
    grid, _ = _blocked(w.shape, block)
    lead = len(grid)

    def body(p_ref, w_ref, m_ref, v_ref, g_ref, d_ref, mo_ref, vo_ref):
        g = p_ref[0].astype(F32)
        for j in range(1, n_parts):
            g = g + p_ref[j].astype(F32)
        m_new = ADAM_B1 * m_ref[...] + (1.0 - ADAM_B1) * g
        v_new = ADAM_B2 * v_ref[...] + (1.0 - ADAM_B2) * jnp.square(g)
        m_hat = m_new / np.float32(1.0 - ADAM_B1 ** ADAM_STEP)
        v_hat = v_new / np.float32(1.0 - ADAM_B2 ** ADAM_STEP)
        g_ref[...] = g
        d_ref[...] = -ADAM_LR * (m_hat / (jnp.sqrt(v_hat) + ADAM_EPS) + ADAM_WD * w_ref[...])
        mo_ref[...] = m_new
        vo_ref[...] = v_new

    idx = lambda *i: tuple(i) + (0,) * (nd - lead)
    spec = pl.BlockSpec(block, idx)
    pspec = pl.BlockSpec((n_parts,) + tuple(block), lambda *i: (0,) + idx(*i))
    n_blk = int(np.prod(block[:-1])) * (-(-block[-1] // LANE) * LANE)
    return pl.pallas_call(
        body, name=name, grid=grid, in_specs=[pspec, spec, spec, spec], out_specs=[spec] * 4,
        out_shape=[jax.ShapeDtypeStruct(w.shape, F32)] * 4,
        compiler_params=_params(("arbitrary",) * lead, 2 * (n_parts + 7) * n_blk * 4 + VMEM_MARGIN),
    )(parts, w, m, v)


def _to_flat(pieces, row_multiple):
    flat = jnp.concatenate([p.reshape(-1) for p in pieces])
    rows = -(-flat.shape[0] // FLAT_W)
    rows = -(-rows // row_multiple) * row_multiple
    return jnp.pad(flat, (0, rows * FLAT_W - flat.shape[0])).reshape(rows, FLAT_W)


def _lane_row(*pieces):
    flat = jnp.concatenate([p.reshape(-1) for p in pieces])
    return jnp.pad(flat, (0, LANE - flat.shape[0])).reshape(1, LANE)


def _rows_view(a):
    return a.reshape(a.shape[:-1] + (SUBLANES, LANE))


def _in_shard_t(a):
    return _rows_view(jnp.swapaxes(a, 1, 2))


def kernel(x, norm_w, w_in, sgu_ln_g, sgu_ln_b, sgu_w, sgu_b, conv_w, a_log_f, a_log_b, dt_bias_f, dt_bias_b, dn_norm_w, w_out, final_norm_w, loss_target, m_norm_w, m_w_in, m_sgu_ln_g, m_sgu_ln_b, m_sgu_w, m_sgu_b, m_conv_w, m_a_log_f, m_a_log_b, m_dt_bias_f, m_dt_bias_b, m_dn_norm_w, m_w_out, m_final_norm_w, v_norm_w, v_w_in, v_sgu_ln_g, v_sgu_ln_b, v_sgu_w, v_sgu_b, v_conv_w, v_a_log_f, v_a_log_b, v_dt_bias_f, v_dt_bias_b, v_dn_norm_w, v_w_out, v_final_norm_w):
    weights = dict(norm_w=norm_w, w_in=w_in, sgu_ln_g=sgu_ln_g, sgu_ln_b=sgu_ln_b, sgu_w=sgu_w, sgu_b=sgu_b, conv_w=conv_w,
                   a_log_f=a_log_f, a_log_b=a_log_b, dt_bias_f=dt_bias_f, dt_bias_b=dt_bias_b, dn_norm_w=dn_norm_w,
                   w_out=w_out, final_norm_w=final_norm_w)
    m_in = dict(norm_w=m_norm_w, w_in=m_w_in, sgu_ln_g=m_sgu_ln_g, sgu_ln_b=m_sgu_ln_b, sgu_w=m_sgu_w, sgu_b=m_sgu_b,
                conv_w=m_conv_w, a_log_f=m_a_log_f, a_log_b=m_a_log_b, dt_bias_f=m_dt_bias_f, dt_bias_b=m_dt_bias_b,
                dn_norm_w=m_dn_norm_w, w_out=m_w_out, final_norm_w=m_final_norm_w)
    v_in = dict(norm_w=v_norm_w, w_in=v_w_in, sgu_ln_g=v_sgu_ln_g, sgu_ln_b=v_sgu_ln_b, sgu_w=v_sgu_w, sgu_b=v_sgu_b,
                conv_w=v_conv_w, a_log_f=v_a_log_f, a_log_b=v_a_log_b, dt_bias_f=v_dt_bias_f, dt_bias_b=v_dt_bias_b,
                dn_norm_w=v_dn_norm_w, w_out=v_w_out, final_norm_w=v_final_norm_w)
    s = x.shape[1]
    nc = s // DN_CHUNK
    depth = w_in.shape[0]
    in_shard, out_shard, conv_shard = w_in.shape[2], w_out.shape[1], conv_w.shape[2]
    x0 = x.reshape(s, D_MODEL)
    target = loss_target.reshape(s, D_MODEL)

    w_in_shard = _in_shard_t(w_in.astype(BF16))
    w_out_shard = w_out.astype(BF16)

    gathered_in_shape = jax.ShapeDtypeStruct((N_DEV * in_shard, SUBLANES, LANE), BF16)
    gathered_out_shape = jax.ShapeDtypeStruct((D_MODEL, D_MODEL), BF16)
    gathered_in, gathered_out, conv_all = _exchange("gather_first", _Comm(
        [w_in_shard[0], w_out_shard[0], conv_w],
        [gathered_in_shape, gathered_out_shape, jax.ShapeDtypeStruct((N_DEV,) + conv_w.shape, F32)],
        [(0, 0, _whole, _row_block(in_shard, 0)), (1, 1, _whole, _row_block(out_shard, 0)), (2, 2, _whole, _slot)]))
    gather_plan = lambda l: _Comm([w_in_shard[l], w_out_shard[l]], [gathered_in_shape, gathered_out_shape],
                                  [(0, 0, _whole, _row_block(in_shard, 0)), (1, 1, _whole, _row_block(out_shard, 0))])
    conv_full = conv_all.transpose(1, 2, 0, 3).reshape(depth, CONV_W, 3 * D_DN)
    conv_full = jnp.pad(conv_full, ((0, 0), (0, HALO - CONV_W), (0, 0)))
    sgu_b_t = jnp.swapaxes(sgu_b, 1, 2)

    def rows_of(gates, first_lane):
        return gates[:, first_lane:first_lane + N_GROUPS].reshape(nc, DN_CHUNK, N_GROUPS).transpose(0, 2, 1)

    saved = []
    xl = x0
    for l in range(depth):
        w_in_t = gathered_in.reshape(D_IN, D_MODEL)
        w_gate_t = jnp.pad(w_in_t[D_MAIN:], ((0, LANE - N_GATE), (0, 0)))
        w_out_l = gathered_out
        nw = norm_w[l].reshape(1, D_MODEL)
        h, h_t = _rms_fwd(xl, nw)
        proj = _matmul([(_lhs(h), _rhs(w_in_t, MM_WIDE, transposed=True))], s, D_MAIN, name="in_proj", trans_b=True, tn=MM_WIDE)
        p_gate = _matmul([(_lhs(h), _rhs(w_gate_t, LANE, transposed=True))], s, LANE, name="in_proj_gate", trans_b=True, tn=LANE)
        ln_g, ln_b = sgu_ln_g[l].reshape(1, D_SGU), sgu_ln_b[l].reshape(1, D_SGU)
        y, y_t = _sgu_fwd(proj, ln_g, ln_b, sgu_w[l], sgu_b_t[l])
        a_log = _lane_row(a_log_f[l], a_log_b[l])
        dt_bias = _lane_row(dt_bias_f[l], dt_bias_b[l])
        q, k, v, gates = _prep_fwd(proj, conv_full[l], p_gate, a_log, dt_bias)
        rows_f, rows_b = rows_of(gates, 0), rows_of(gates, N_GROUPS)
        more = l + 1 < depth
        (o_f, st_f, t_f), (o_b, st_b, t_b), next_weights = _delta_fwd(q, k, v, gates, rows_f, rows_b,
                                                                     comm=gather_plan(l + 1) if more else None)
        dn_w = dn_norm_w[l].reshape(1, HEAD)
        y, y_t = _post_fwd(o_f, o_b, proj, dn_w, y, y_t)
        x_next = _matmul([(_lhs(y), _rhs(w_out_l, MM_WIDE))], s, D_MODEL, name="out_proj", add=xl, tn=MM_WIDE)
        saved.append(dict(x=xl, nw=nw, h_t=h_t, proj=proj, p_gate=p_gate, ln_g=ln_g, ln_b=ln_b, a_log=a_log, dt_bias=dt_bias,
                          q=q, k=k, v=v, gates=gates, rows_f=rows_f, rows_b=rows_b, o_f=o_f, o_b=o_b, st_f=st_f, st_b=st_b,
                          t_f=t_f, t_b=t_b, dn_w=dn_w, y_t=y_t, w_in_t=w_in_t, w_gate_t=w_gate_t, w_out=w_out_l))
        xl = x_next
        if more:
            gathered_in, gathered_out = next_weights

    loss_part, dx, dx_bf, d_final = _final_loss(xl, final_norm_w.reshape(1, D_MODEL), target)
    loss = lax.psum(loss_part[0, 0], MESH_AXES)

    small_names = [n for n in SMALL if n not in ("conv_w", "final_norm_w")]
    grads = {n: [None] * depth for n in small_names + ["conv_w"]}
    t_rows = -(-D_IN // LANE) * LANE
    layer_slot = lambda l: (lambda ref, dev: ref.at[dev, l])
    parts = [jax.ShapeDtypeStruct((N_DEV, depth, in_shard, SUBLANES, LANE), BF16),
             jax.ShapeDtypeStruct((N_DEV, depth, out_shard, D_MODEL), BF16)]

    def exchange_plan(layer, g_in_rows, g_out, more_inputs=(), more_outputs=(), more_copies=()):
        return _Comm([g_in_rows, g_out, *more_inputs], [*parts, *more_outputs],
                     [(0, 0, _row_block(in_shard, 0), layer_slot(layer)), (1, 1, _row_block(out_shard, 0), layer_slot(layer)),
                      *more_copies])

    pending = None
    for l in reversed(range(depth)):
        sv = saved[l]
        dy = _matmul([(_lhs(dx_bf), _rhs(sv["w_out"], MM_WIDE, transposed=True))], s, D_MODEL, name="out_proj_dy", trans_b=True,
                     tn=MM_WIDE)
        g_out = _matmul([(_lhs(sv["y_t"]), _rhs(dx_bf, MM_TILE))], D_MODEL, D_MODEL, name="out_proj_dw", out_dtype=BF16)
        do, dp, d_dn = _post_bwd(sv["o_f"], sv["o_b"], sv["proj"], sv["dn_w"], dy)
        (dq, dk, dv, dgate_f, drows_f), exchanged = _delta_bwd(
            sv["q"], sv["k"], sv["v"], sv["gates"], sv["rows_f"], sv["st_f"], sv["t_f"], do, None, reverse=False,
            comm=None if pending is None else exchange_plan(*pending))
        if exchanged:
            parts = list(exchanged)
        (dq, dk, dv, dgate_b, drows_b), _ = _delta_bwd(sv["q"], sv["k"], sv["v"], sv["gates"], sv["rows_b"], sv["st_b"], sv["t_b"],
                                                       do, (dq, dk, dv), reverse=True)
        drows = jnp.concatenate([drows_f.transpose(0, 2, 1).reshape(s, N_GROUPS), drows_b.transpose(0, 2, 1).reshape(s, N_GROUPS)], axis=1)
        drows = jnp.pad(drows, ((0, 0), (0, LANE - 2 * N_GROUPS)))
        dp, d_gate, d_conv, d_alog, d_dt = _prep_bwd(sv["proj"], conv_full[l], sv["p_gate"], sv["a_log"], sv["dt_bias"],
                                                     dq, dk, dv, (dgate_f, dgate_b, drows), dp)
        dp, d_lg, d_lb, d_sw, d_sbt = _sgu_bwd(sv["proj"], sv["ln_g"], sv["ln_b"], sgu_w[l], sgu_b_t[l], dy, dp)
        h_t = sv["h_t"]
        g_in_t = _matmul([(_lhs(h_t), _rhs(dp, MM_TILE))], D_MODEL, D_MAIN, name="in_proj_dw", out_dtype=BF16, out_t_rows=t_rows)
        g_in_t = _matmul([(_lhs(h_t), _rhs(d_gate, LANE))], D_MODEL, LANE, name="in_proj_dw_gate", out_dtype=BF16, tn=LANE,
                         transpose_into=g_in_t, out_row_block=D_MAIN // LANE)
        pending = (l, _rows_view(g_in_t), g_out)
        dh_pairs = [(_lhs(dp), _rhs(sv["w_in_t"], MM_TILE, k=D_MAIN)), (_lhs(d_gate), _rhs(sv["w_gate_t"], MM_TILE))]
        if l > 0:
            dh = _matmul(dh_pairs, s, D_MODEL, name="in_proj_dh")
        else:
            dh, parts = _matmul(dh_pairs, s, D_MODEL, name="in_proj_dh_comm", comm=exchange_plan(*pending))
        dx, dx_bf, d_nw = _rms_bwd(sv["x"], sv["nw"], dh, dx)
        grads["norm_w"][l] = d_nw.reshape(D_MODEL)
        grads["sgu_ln_g"][l], grads["sgu_ln_b"][l] = d_lg.reshape(D_SGU), d_lb.reshape(D_SGU)
        grads["sgu_w"][l], grads["sgu_b"][l] = d_sw, d_sbt.T
        grads["conv_w"][l] = d_conv[:CONV_W]
        grads["a_log_f"][l], grads["a_log_b"][l] = d_alog[0, :N_GROUPS], d_alog[0, N_GROUPS:2 * N_GROUPS]
        grads["dt_bias_f"][l], grads["dt_bias_b"][l] = d_dt[0, :N_GROUPS], d_dt[0, N_GROUPS:2 * N_GROUPS]
        grads["dn_norm_w"][l] = d_dn.reshape(HEAD)
    grad_x = dx.reshape(1, s, D_MODEL)

    g_small = {n: jnp.stack(grads[n]) for n in small_names}
    g_small["final_norm_w"] = d_final.reshape(D_MODEL)
    conv_by_dev = jnp.stack(grads["conv_w"]).reshape(depth, CONV_W, N_DEV, conv_shard).transpose(2, 0, 1, 3).reshape(N_DEV, -1)
    replicated = jnp.concatenate([g_small[n].reshape(-1) for n in SMALL if n != "conv_w"])
    n_small = conv_by_dev.shape[1] + replicated.shape[0]
    small_rows = -(-n_small // (FLAT_W * HALO)) * HALO
    send_small = jnp.concatenate([conv_by_dev, jnp.broadcast_to(replicated, (N_DEV, replicated.shape[0])),
                                  jnp.zeros((N_DEV, small_rows * FLAT_W - n_small), F32)], axis=1).reshape(N_DEV, small_rows, FLAT_W)
    parts_in, parts_out = parts

    in_rows = 100 if in_shard % 100 == 0 else in_shard
    g_in_t, (parts_small,) = _sum_parts("sum_in_comm", parts_in, (1, in_rows, SUBLANES, LANE), comm=_Comm(
        [send_small], [jax.ShapeDtypeStruct((N_DEV, small_rows, FLAT_W), F32)], [(0, 0, _slot, _slot)]))
    g_in = jnp.swapaxes(g_in_t.reshape(depth, in_shard, D_MODEL), 1, 2)
    res_in = _adamw("adamw_in", g_in[None], w_in, m_w_in, v_w_in, (1, min(256, D_MODEL), in_shard))
    res_out = _adamw("adamw_out", parts_out, w_out, m_w_out, v_w_out, (1, min(128, out_shard), D_MODEL))
    res_small = _adamw("adamw_small", parts_small, _to_flat([weights[n] for n in SMALL], HALO), _to_flat([m_in[n] for n in SMALL], HALO),
                       _to_flat([v_in[n] for n in SMALL], HALO), (small_rows // 4 if small_rows % 32 == 0 else small_rows, FLAT_W))
    outs = [dict(w_in=a, w_out=b) for a, b in zip(res_in, res_out)]
    off = 0
    for n in SMALL:
        size = int(np.prod(weights[n].shape))
        for o, d in zip(res_small, outs):
            d[n] = o.reshape(-1)[off:off + size].reshape(weights[n].shape)
        off += size
    g_out, delta_out, m_out, v_out = outs
    return (loss, grad_x, *[g_out[n] for n in WEIGHTS], *[delta_out[n] for n in WEIGHTS],
            *[m_out[n] for n in WEIGHTS], *[v_out[n] for n in WEIGHTS])
```

```python
import functools

import numpy as np
import jax
import jax.numpy as jnp
from jax import lax
from jax.experimental import pallas as pl
from jax.experimental.pallas import tpu as pltpu

F32 = jnp.float32
BF16 = jnp.bfloat16

N_DEV = 8
D_MODEL = 2048
D_SGU = 1024
D_DN = 1024
N_GROUPS = 8
HEAD = 128
SGU_CHUNK = 128
DN_CHUNK = 64
CONV_W = 5
N_GATE = 32
D_MAIN = 3 * D_SGU + 4 * D_DN
D_IN = D_MAIN + N_GATE
LANE = 128
SUBLANES = 16
HALO = 8
EPS = 1e-6
ADAM_LR, ADAM_B1, ADAM_B2, ADAM_EPS, ADAM_WD, ADAM_STEP = 0.001, 0.9, 0.999, 1e-08, 0.01, 10
FLAT_W = 1024
MM_TILE = 512
MM_WIDE = 1024
VMEM_MARGIN = 8 << 20

MESH_AXES = ("x", "y", "c")
WEIGHTS = ("norm_w", "w_in", "sgu_ln_g", "sgu_ln_b", "sgu_w", "sgu_b", "conv_w", "a_log_f", "a_log_b",
           "dt_bias_f", "dt_bias_b", "dn_norm_w", "w_out", "final_norm_w")
SMALL = ("conv_w",) + tuple(n for n in WEIGHTS if n not in ("w_in", "w_out", "conv_w"))


def _bdot(a, b, dims):
    return lax.dot_general(a.astype(BF16), b.astype(BF16), (dims, ((), ())), preferred_element_type=F32)


@jax.custom_vjp
def _mm_nn(a, b):
    return _bdot(a, b, ((1,), (0,)))


def _mm_nn_fwd(a, b):
    return _mm_nn(a, b), (a, b)


def _mm_nn_bwd(res, ct):
    a, b = res
    return _bdot(ct, b, ((1,), (1,))), _bdot(a, ct, ((0,), (0,)))


_mm_nn.defvjp(_mm_nn_fwd, _mm_nn_bwd)


@jax.custom_vjp
def _mm_nt(a, b):
    return _bdot(a, b, ((1,), (1,)))


def _mm_nt_fwd(a, b):
    return _mm_nt(a, b), (a, b)


def _mm_nt_bwd(res, ct):
    a, b = res
    return _bdot(ct, b, ((1,), (0,))), _bdot(ct, a, ((0,), (0,)))


_mm_nt.defvjp(_mm_nt_fwd, _mm_nt_bwd)


@jax.custom_vjp
def _mm_tn(a, b):
    return _bdot(a, b, ((0,), (0,)))


def _mm_tn_fwd(a, b):
    return _mm_tn(a, b), (a, b)


def _mm_tn_bwd(res, ct):
    a, b = res
    return _bdot(b, ct, ((1,), (1,))), _bdot(a, ct, ((1,), (0,)))


_mm_tn.defvjp(_mm_tn_fwd, _mm_tn_bwd)


@jax.custom_vjp
def _inverse_given(m, t):
    return t


def _inverse_given_fwd(m, t):
    return t, t


def _inverse_given_bwd(t, ct):
    dm = -_bdot(t, _bdot(ct, t, ((1,), (1,))), ((0,), (0,)))
    return dm, jnp.zeros_like(t)


_inverse_given.defvjp(_inverse_given_fwd, _inverse_given_bwd)


@jax.custom_vjp
def _split_lanes(x):
    return tuple(x[:, i * LANE:(i + 1) * LANE] for i in range(x.shape[1] // LANE))


def _split_lanes_fwd(x):
    return _split_lanes(x), None


def _split_lanes_bwd(_, cts):
    return (jnp.concatenate(cts, axis=1),)


_split_lanes.defvjp(_split_lanes_fwd, _split_lanes_bwd)


def _silu(t):
    return t * jax.nn.sigmoid(t)


def _gelu(t):
    return 0.5 * t * (1.0 + lax.erf(t * np.float32(0.7071067811865476)))


def _rms(x, w):
    return x * lax.rsqrt(jnp.mean(x * x, axis=-1, keepdims=True) + EPS) * w


def _params(sem, vmem_bytes=None):
    kw = dict(dimension_semantics=sem)
    if vmem_bytes is not None:
        kw["vmem_limit_bytes"] = int(vmem_bytes)
    return pltpu.CompilerParams(**kw)


def _lhs(a, k=None, k_block=0, tm=MM_TILE):
    k = a.shape[1] if k is None else k
    tm = min(tm, a.shape[0])
    return a, pl.BlockSpec((tm, k), lambda i, j: (i, k_block)), tm * k * a.dtype.itemsize


def _rhs(b, tn, *, layer=None, transposed=False, k=None, k_block=0, n_offset=0):
    lead = () if layer is None else (layer,)
    none = () if layer is None else (None,)
    if transposed:
        k = b.shape[-1]
        spec = pl.BlockSpec(none + (tn, k), lambda i, j: lead + (j + n_offset, 0))
    else:
        k = b.shape[-2] if k is None else k
        spec = pl.BlockSpec(none + (k, tn), lambda i, j: lead + (k_block, j))
    return b, spec, tn * k * b.dtype.itemsize


def _matmul(pairs, m, n, *, name, trans_b=False, add=None, out_dtype=F32, tn=MM_TILE, out_t_rows=None, transpose_into=None,
            out_row_block=0, comm=None):
    tm = pairs[0][0][1].block_shape[0]
    assert m % tm == 0 and n % tn == 0 and all(a_spec.block_shape[0] == tm for (_, a_spec, _), _ in pairs)
    n_pairs = len(pairs)
    contract = ((1,), (1,)) if trans_b else ((1,), (0,))
    transposed_out = out_t_rows is not None or transpose_into is not None

    def body(*refs):
        o_ref = refs[-1]
        acc = None
        for i in range(n_pairs):
            d = lax.dot_general(refs[2 * i][...].astype(BF16), refs[2 * i + 1][...].astype(BF16),
                                (contract, ((), ())), preferred_element_type=F32)
            acc = d if acc is None else acc + d
        if add is not None:
            acc = acc + refs[2 * n_pairs][...]
        o_ref[...] = (acc.T if transposed_out else acc).astype(out_dtype)

    in_specs, args, vmem, aliases = [], [], 0, {}
    for (a, a_spec, a_bytes), (b, b_spec, b_bytes) in pairs:
        in_specs += [a_spec, b_spec]
        args += [a, b]
        vmem += 2 * (a_bytes + b_bytes)
    if add is not None:
        in_specs.append(pl.BlockSpec((tm, tn), lambda i, j: (i, j)))
        args.append(add)
        vmem += 2 * tm * tn * 4
    vmem += 2 * tm * tn * jnp.dtype(out_dtype).itemsize + 3 * tm * tn * 4
    if transposed_out:
        out_spec = pl.BlockSpec((tn, tm), lambda i, j: (j + out_row_block, i))
        if transpose_into is not None:
            in_specs.append(pl.BlockSpec(memory_space=pl.ANY))
            args.append(transpose_into)
            aliases = {len(args) - 1: 0}
            out_shape = jax.ShapeDtypeStruct(transpose_into.shape, out_dtype)
        else:
            out_shape = jax.ShapeDtypeStruct((out_t_rows, m), out_dtype)
    else:
        out_spec = pl.BlockSpec((tm, tn), lambda i, j: (i, j))
        out_shape = jax.ShapeDtypeStruct((m, n), out_dtype)
    if comm is not None:
        assert not aliases
        (out,), sent = _call_with_comm(
            body, comm, name=name, grid=(m // tm, n // tn), in_specs=in_specs, out_specs=[out_spec], out_shape=[out_shape],
            scratch_shapes=[], args=args, compiler_params=_params(("arbitrary", "arbitrary"), vmem + VMEM_MARGIN))
        return out, sent
    return pl.pallas_call(
        body, name=name, grid=(m // tm, n // tn), in_specs=in_specs, out_specs=out_spec, out_shape=out_shape,
        input_output_aliases=aliases, compiler_params=_params(("parallel", "arbitrary"), vmem + VMEM_MARGIN),
    )(*args)


def _rms_fwd(x, w, *, tile=512):
    s, d = x.shape
    tile = min(tile, s)

    def body(x_ref, w_ref, h_ref, ht_ref):
        h = _rms(x_ref[...], w_ref[...])
        h_ref[...] = h.astype(BF16)
        ht_ref[...] = h.T.astype(BF16)

    return pl.pallas_call(
        body, name="rms_fwd", grid=(s // tile,),
        in_specs=[pl.BlockSpec((tile, d), lambda i: (i, 0)), pl.BlockSpec((1, d), lambda i: (0, 0))],
        out_specs=[pl.BlockSpec((tile, d), lambda i: (i, 0)), pl.BlockSpec((d, tile), lambda i: (0, i))],
        out_shape=[jax.ShapeDtypeStruct((s, d), BF16), jax.ShapeDtypeStruct((d, s), BF16)],
        compiler_params=_params(("arbitrary",)),
    )(x, w)


def _rms_bwd(x, w, dh, dx_out, *, tile=256):
    s, d = x.shape
    tile = min(tile, s)

    def body(x_ref, w_ref, dh_ref, dxo_ref, dx_ref, dxb_ref, dw_ref):
        _, vjp = jax.vjp(_rms, x_ref[...], w_ref[...])
        dxv, dwv = vjp(dh_ref[...])
        dx = dxo_ref[...] + dxv
        dx_ref[...] = dx
        dxb_ref[...] = dx.astype(BF16)

        @pl.when(pl.program_id(0) == 0)
        def _():
            dw_ref[...] = jnp.zeros_like(dw_ref)

        dw_ref[...] += dwv

    row = pl.BlockSpec((tile, d), lambda i: (i, 0))
    vec = pl.BlockSpec((1, d), lambda i: (0, 0))
    return pl.pallas_call(
        body, name="rms_bwd", grid=(s // tile,),
        in_specs=[row, vec, row, row], out_specs=[row, row, vec],
        out_shape=[jax.ShapeDtypeStruct((s, d), F32), jax.ShapeDtypeStruct((s, d), BF16), jax.ShapeDtypeStruct((1, d), F32)],
        compiler_params=_params(("arbitrary",)),
    )(x, w, dh, dx_out)


def _final_loss(x, w, target, *, tile=256):
    s, d = x.shape
    tile = min(tile, s)

    def body(x_ref, w_ref, t_ref, loss_ref, dx_ref, dxb_ref, dw_ref):
        def f(xv, wv):
            err = jnp.square(_rms(xv, wv) - t_ref[...])
            per_token = jnp.sum(err, axis=1, keepdims=True) * np.float32(1.0 / d)
            return 0.5 * jnp.sum(per_token, axis=0, keepdims=True)

        loss, vjp = jax.vjp(f, x_ref[...], w_ref[...])
        dxv, dwv = vjp(jnp.ones((1, 1), F32))
        dx_ref[...] = dxv
        dxb_ref[...] = dxv.astype(BF16)

        @pl.when(pl.program_id(0) == 0)
        def _():
            dw_ref[...] = jnp.zeros_like(dw_ref)
            loss_ref[...] = jnp.zeros_like(loss_ref)

        dw_ref[...] += dwv
        loss_ref[...] += jnp.broadcast_to(loss, (1, LANE))

    row = pl.BlockSpec((tile, d), lambda i: (i, 0))
    vec = pl.BlockSpec((1, d), lambda i: (0, 0))
    return pl.pallas_call(
        body, name="final_loss", grid=(s // tile,),
        in_specs=[row, vec, row], out_specs=[pl.BlockSpec((1, LANE), lambda i: (0, 0)), row, row, vec],
        out_shape=[jax.ShapeDtypeStruct((1, LANE), F32), jax.ShapeDtypeStruct((s, d), F32),
                   jax.ShapeDtypeStruct((s, d), BF16), jax.ShapeDtypeStruct((1, d), F32)],
        compiler_params=_params(("arbitrary",)),
    )(x, w, target)


def _sgu_chunk(u, v, z, ln_g, ln_b, ws, bcols):
    vg = _gelu(v)
    xc = vg - jnp.mean(vg, axis=-1, keepdims=True)
    vl = xc * lax.rsqrt(jnp.mean(xc * xc, axis=-1, keepdims=True) + EPS) * ln_g + ln_b
    sp = jnp.concatenate([_mm_nn(ws[g], vg_) + bcols[g] for g, vg_ in enumerate(_split_lanes(vl))], axis=1)
    return _gelu(u) * sp * _silu(z)


def _sgu_specs(tile):
    col = lambda j: pl.BlockSpec((tile, D_SGU), lambda i, j=j: (i, j))
    vec = pl.BlockSpec((1, D_SGU), lambda i: (0, 0))
    wspec = pl.BlockSpec((N_GROUPS, SGU_CHUNK, SGU_CHUNK), lambda i: (0, 0, 0))
    bspec = pl.BlockSpec((SGU_CHUNK, N_GROUPS), lambda i: (0, 0))
    return col, vec, wspec, bspec


def _sgu_fwd(proj, ln_g, ln_b, w_s, b_t, *, tile=256):
    s = proj.shape[0]
    tile = min(tile, s)
    col, vec, wspec, bspec = _sgu_specs(tile)

    def body(u_ref, v_ref, z_ref, g_ref, b_ref, w_ref, bt_ref, y_ref, yt_ref):
        ws = tuple(w_ref[g] for g in range(N_GROUPS))
        bcols = tuple(bt_ref[:, g:g + 1] for g in range(N_GROUPS))
        for c in range(tile // SGU_CHUNK):
            rows = pl.ds(c * SGU_CHUNK, SGU_CHUNK)
            y = _sgu_chunk(u_ref[rows, :], v_ref[rows, :], z_ref[rows, :], g_ref[...], b_ref[...], ws, bcols)
            y_ref[rows, :] = y.astype(BF16)
            yt_ref[:, rows] = y.T.astype(BF16)

    return pl.pallas_call(
        body, name="sgu_fwd", grid=(s // tile,),
        in_specs=[col(0), col(1), col(2), vec, vec, wspec, bspec],
        out_specs=[pl.BlockSpec((tile, D_SGU), lambda i: (i, 0)), pl.BlockSpec((D_SGU, tile), lambda i: (0, i))],
        out_shape=[jax.ShapeDtypeStruct((s, D_SGU + D_DN), BF16), jax.ShapeDtypeStruct((D_SGU + D_DN, s), BF16)],
        compiler_params=_params(("arbitrary",)),
    )(proj, proj, proj, ln_g, ln_b, w_s, b_t)


def _sgu_bwd(proj, ln_g, ln_b, w_s, b_t, dy, dp, *, tile=128):
    s = proj.shape[0]
    tile = min(tile, s)
    col, vec, wspec, bspec = _sgu_specs(tile)

    def body(u_ref, v_ref, z_ref, g_ref, b_ref, w_ref, bt_ref, dy_ref, _, dp_ref, dg_ref, db_ref, dw_ref, dbt_ref):
        @pl.when(pl.program_id(0) == 0)
        def _():
            dg_ref[...] = jnp.zeros_like(dg_ref)
            db_ref[...] = jnp.zeros_like(db_ref)
            dw_ref[...] = jnp.zeros_like(dw_ref)
            dbt_ref[...] = jnp.zeros_like(dbt_ref)

        ws = tuple(w_ref[g] for g in range(N_GROUPS))
        bcols = tuple(bt_ref[:, g:g + 1] for g in range(N_GROUPS))
        for c in range(tile // SGU_CHUNK):
            rows = pl.ds(c * SGU_CHUNK, SGU_CHUNK)
            _, vjp = jax.vjp(_sgu_chunk, u_ref[rows, :], v_ref[rows, :], z_ref[rows, :], g_ref[...], b_ref[...], ws, bcols)
            du, dv, dz, dg, db, dws, dbc = vjp(dy_ref[rows, :])
            dp_ref[rows, pl.ds(0, D_SGU)] = du.astype(BF16)
            dp_ref[rows, pl.ds(D_SGU, D_SGU)] = dv.astype(BF16)
            dp_ref[rows, pl.ds(2 * D_SGU, D_SGU)] = dz.astype(BF16)
            dg_ref[...] += dg
            db_ref[...] += db
            for g in range(N_GROUPS):
                dw_ref[g] += dws[g]
                dbt_ref[:, g:g + 1] += dbc[g]

    return pl.pallas_call(
        body, name="sgu_bwd", grid=(s // tile,),
        in_specs=[col(0), col(1), col(2), vec, vec, wspec, bspec, pl.BlockSpec((tile, D_SGU), lambda i: (i, 0)),
                  pl.BlockSpec(memory_space=pl.ANY)],
        out_specs=[pl.BlockSpec((tile, 3 * D_SGU), lambda i: (i, 0)), vec, vec, wspec, bspec],
        out_shape=[jax.ShapeDtypeStruct(dp.shape, BF16), jax.ShapeDtypeStruct((1, D_SGU), F32),
                   jax.ShapeDtypeStruct((1, D_SGU), F32), jax.ShapeDtypeStruct((N_GROUPS, SGU_CHUNK, SGU_CHUNK), F32),
                   jax.ShapeDtypeStruct((SGU_CHUNK, N_GROUPS), F32)],
        input_output_aliases={8: 0},
        compiler_params=_params(("arbitrary",)),
    )(proj, proj, proj, ln_g, ln_b, w_s, b_t, dy, dp)


QKV_BLOCK = 1


def _qkv_head(cq, ck, cv):
    q, k, v = _silu(cq), _silu(ck), _silu(cv)
    q = q * lax.rsqrt(jnp.sum(q * q, axis=-1, keepdims=True) + EPS) * np.float32(HEAD ** -0.5)
    k = k * lax.rsqrt(jnp.sum(k * k, axis=-1, keepdims=True) + EPS)
    return q, k, v


def _gate_fn(p, a_log, dt_bias):
    lane = lax.broadcasted_iota(jnp.int32, p.shape, 1)
    g = -jnp.exp(a_log) * jax.nn.softplus(p + dt_bias)
    return jnp.where(lane < 16, g, jnp.where(lane < N_GATE, jax.nn.sigmoid(p), 0.0))


def _shifted_rows(full, shift, start, rows):
    n = full.shape[0]
    assert start % HALO == 0 and 0 <= start + shift and start + shift + rows <= n
    rolled = full if shift == 0 else pltpu.roll(full, (-shift) % n, 0)
    return rolled[start:start + rows, :]


def _halo_maps(tile, s):
    per = tile // HALO
    prev = lambda i: jnp.maximum(i * per - 1, 0)
    nxt = lambda i: jnp.minimum((i + 1) * per, s // HALO - 1)
    return prev, nxt


def _prep_fwd(proj, conv_w, pg, a_log, dt_bias, *, tile=256):
    s = proj.shape[0]
    width = 3 * D_DN
    tile = min(tile, s)
    n_blk = s // tile
    prev, nxt = _halo_maps(tile, s)

    def body(x_ref, xp_ref, xn_ref, w_ref, pg_ref, al_ref, dt_ref, q_ref, k_ref, v_ref, gate_ref, ext):
        i = pl.program_id(0)
        ext[pl.ds(0, HALO), :] = jnp.where(i > 0, xp_ref[...], 0.0)
        ext[pl.ds(HALO, tile), :] = x_ref[...]
        ext[pl.ds(HALO + tile, HALO), :] = jnp.where(i < n_blk - 1, xn_ref[...], 0.0)
        full, c = ext[...], None
        for j in range(CONV_W):
            term = _shifted_rows(full, j - CONV_W // 2, HALO, tile) * w_ref[j:j + 1, :]
            c = term if c is None else c + term
        for h in range(N_GROUPS):
            q, k, v = _qkv_head(c[:, h * HEAD:(h + 1) * HEAD], c[:, D_DN + h * HEAD:D_DN + (h + 1) * HEAD],
                                c[:, 2 * D_DN + h * HEAD:2 * D_DN + (h + 1) * HEAD])
            q_ref[h] = q
            k_ref[h] = k
            v_ref[h] = v
        gate_ref[...] = _gate_fn(pg_ref[...], al_ref[...], dt_ref[...])

    hm = pl.BlockSpec((N_GROUPS, tile, HEAD), lambda i: (0, i, 0))
    lane_vec = pl.BlockSpec((1, LANE), lambda i: (0, 0))
    return pl.pallas_call(
        body, name="prep_fwd", grid=(n_blk,),
        in_specs=[pl.BlockSpec((tile, width), lambda i: (i, QKV_BLOCK)),
                  pl.BlockSpec((HALO, width), lambda i: (prev(i), QKV_BLOCK)),
                  pl.BlockSpec((HALO, width), lambda i: (nxt(i), QKV_BLOCK)),
                  pl.BlockSpec((HALO, width), lambda i: (0, 0)),
                  pl.BlockSpec((tile, LANE), lambda i: (i, 0)), lane_vec, lane_vec],
        out_specs=[hm, hm, hm, pl.BlockSpec((tile, LANE), lambda i: (i, 0))],
        out_shape=[jax.ShapeDtypeStruct((N_GROUPS, s, HEAD), F32)] * 3 + [jax.ShapeDtypeStruct((s, LANE), F32)],
        scratch_shapes=[pltpu.VMEM((tile + 2 * HALO, width), F32)],
        compiler_params=_params(("arbitrary",)),
    )(proj, proj, proj, conv_w, pg, a_log, dt_bias)


def _prep_bwd(proj, conv_w, pg, a_log, dt_bias, dqkv_f, dqkv_b, dgates, dp, *, tile=128):
    s = proj.shape[0]
    width = 3 * D_DN
    tile = min(tile, s)
    n_blk = s // tile
    prev, nxt = _halo_maps(tile, s)
    ext_rows = tile + 2 * HALO
    pad = CONV_W // 2

    def body(*refs):
        x_ref, xp_ref, xn_ref, w_ref, pg_ref, al_ref, dt_ref = refs[:7]
        ct_refs = refs[7:25]
        dg0_ref, dg1_ref, dg2_ref = refs[25:28]
        dp_ref, dpg_ref, dw_ref, dal_ref, ddt_ref, xext, dcext = refs[29:]
        i = pl.program_id(0)
        has_prev, has_next = i > 0, i < n_blk - 1

        @pl.when(i == 0)
        def _():
            dw_ref[...] = jnp.zeros_like(dw_ref)
            dal_ref[...] = jnp.zeros_like(dal_ref)
            ddt_ref[...] = jnp.zeros_like(ddt_ref)

        zeros = jnp.zeros((HALO, width), F32)
        xext[pl.ds(0, HALO), :] = zeros
        xext[pl.ds(HALO, HALO), :] = jnp.where(has_prev, xp_ref[...], 0.0)
        xext[pl.ds(2 * HALO, tile), :] = x_ref[...]
        xext[pl.ds(2 * HALO + tile, HALO), :] = jnp.where(has_next, xn_ref[...], 0.0)
        xext[pl.ds(3 * HALO + tile, HALO), :] = zeros
        xfull, c = xext[...], None
        for j in range(CONV_W):
            term = _shifted_rows(xfull, j - pad, HALO, ext_rows) * w_ref[j:j + 1, :]
            c = term if c is None else c + term
        for h in range(N_GROUPS):
            lanes = [pl.ds(part * D_DN + h * HEAD, HEAD) for part in range(3)]
            cts = []
            for part in range(3):
                (cur, before, after), (cur2, before2, after2) = ct_refs[3 * part:3 * part + 3], ct_refs[9 + 3 * part:12 + 3 * part]
                cts.append(jnp.concatenate([jnp.where(has_prev, before[h] + before2[h], 0.0), cur[h] + cur2[h],
                                            jnp.where(has_next, after[h] + after2[h], 0.0)], axis=0))
            _, vjp = jax.vjp(_qkv_head, c[:, h * HEAD:(h + 1) * HEAD], c[:, D_DN + h * HEAD:D_DN + (h + 1) * HEAD],
                             c[:, 2 * D_DN + h * HEAD:2 * D_DN + (h + 1) * HEAD])
            for lane_sl, dc in zip(lanes, vjp(tuple(cts))):
                dcext[:, lane_sl] = dc
        dcfull = dcext[...]
        dx, dc_blk = None, dcfull[HALO:HALO + tile, :]
        for j in range(CONV_W):
            term = _shifted_rows(dcfull, pad - j, HALO, tile) * w_ref[j:j + 1, :]
            dx = term if dx is None else dx + term
            dw_ref[j:j + 1, :] += jnp.sum(dc_blk * _shifted_rows(xfull, j - pad, 2 * HALO, tile), axis=0, keepdims=True)
        dp_ref[...] = dx.astype(BF16)
        _, gvjp = jax.vjp(_gate_fn, pg_ref[...], al_ref[...], dt_ref[...])
        dpg, dal, ddt = gvjp(dg0_ref[...] + dg1_ref[...] + dg2_ref[...])
        dpg_ref[...] = dpg.astype(BF16)
        dal_ref[...] += dal
        ddt_ref[...] += ddt

    xs = [pl.BlockSpec((tile, width), lambda i: (i, QKV_BLOCK)), pl.BlockSpec((HALO, width), lambda i: (prev(i), QKV_BLOCK)),
          pl.BlockSpec((HALO, width), lambda i: (nxt(i), QKV_BLOCK))]
    hm = [pl.BlockSpec((N_GROUPS, tile, HEAD), lambda i: (0, i, 0)),
          pl.BlockSpec((N_GROUPS, HALO, HEAD), lambda i: (0, prev(i), 0)),
          pl.BlockSpec((N_GROUPS, HALO, HEAD), lambda i: (0, nxt(i), 0))]
    gate = pl.BlockSpec((tile, LANE), lambda i: (i, 0))
    lane_vec = pl.BlockSpec((1, LANE), lambda i: (0, 0))
    wspec = pl.BlockSpec((HALO, width), lambda i: (0, 0))
    n_in = 3 + 4 + 18 + 3
    ct_arrays = [a for dqkv in (dqkv_f, dqkv_b) for a in dqkv for _ in range(3)]
    return pl.pallas_call(
        body, name="prep_bwd", grid=(n_blk,),
        in_specs=xs + [wspec, gate, lane_vec, lane_vec] + hm * 6 + [gate] * 3 + [pl.BlockSpec(memory_space=pl.ANY)],
        out_specs=[pl.BlockSpec((tile, width), lambda i: (i, QKV_BLOCK)), gate, wspec, lane_vec, lane_vec],
        out_shape=[jax.ShapeDtypeStruct(dp.shape, BF16), jax.ShapeDtypeStruct((s, LANE), BF16),
                   jax.ShapeDtypeStruct((HALO, width), F32), jax.ShapeDtypeStruct((1, LANE), F32),
                   jax.ShapeDtypeStruct((1, LANE), F32)],
        input_output_aliases={n_in: 0},
        scratch_shapes=[pltpu.VMEM((tile + 4 * HALO, width), F32), pltpu.VMEM((ext_rows, width), F32)],
        compiler_params=_params(("arbitrary",)),
    )(proj, proj, proj, conv_w, pg, a_log, dt_bias, *ct_arrays, *dgates, dp)


def _each(f, *cols):
    return tuple(f(*a) for a in zip(*cols))


def _delta_chunk(q, k, v, g_col, g_row, beta, state, t_known, reverse):
    n = DN_CHUNK
    ri = lax.broadcasted_iota(jnp.int32, (n, n), 0)
    ci = lax.broadcasted_iota(jnp.int32, (n, n), 1)
    masks = {False: (ri >= ci, ri <= ci, ri > ci), True: (ri <= ci, ri >= ci, ri < ci)}
    flags = tuple(reverse) if isinstance(reverse, (tuple, list)) else (reverse,) * len(q)
    incl, incl_t, strict = (tuple(masks[f][i] for f in flags) for i in range(3))
    gc_col = _each(lambda gr, mk: jnp.sum(jnp.where(mk, gr, 0.0), axis=1, keepdims=True), g_row, incl)
    gc_row = _each(lambda gc, mk: jnp.sum(jnp.where(mk, gc, 0.0), axis=0, keepdims=True), g_col, incl_t)
    g_tot = _each(lambda gr: jnp.sum(gr, axis=1, keepdims=True), g_row)
    decay = _each(lambda a, b, mk: jnp.where(mk, jnp.exp(jnp.where(mk, a - b, 0.0)), 0.0), gc_col, gc_row, incl)
    kb = _each(lambda a, b: a * b, k, beta)
    kk = _each(_mm_nt, kb, k)
    m = _each(lambda a, d, mk: jnp.where(mk, a * d, 0.0), kk, decay, strict)
    if t_known is None:
        eye = (ri == ci).astype(F32)
        p = _each(lambda a: -a, m)
        t = _each(lambda a: eye + a, p)
        for _ in range(5):
            p = _each(lambda a: _bdot(a, a, ((1,), (0,))), p)
            t = _each(lambda a, b: a + _bdot(a, b, ((1,), (0,))), t, p)
    else:
        t = _each(_inverse_given, m, t_known)
    e_gc = _each(jnp.exp, gc_col)
    u = _each(lambda t_, v_, b_: _mm_nn(t_, v_ * b_), t, v, beta)
    w = _each(lambda t_, kb_, e_: _mm_nn(t_, kb_ * e_), t, kb, e_gc)
    attn = _each(lambda q_, k_, d_: _mm_nt(q_, k_) * d_, q, k, decay)
    ws = _each(_mm_nn, w, state)
    v_new = _each(lambda a, b: a - b, u, ws)
    qs = _each(lambda q_, e_, s_: _mm_nn(q_ * e_, s_), q, e_gc, state)
    av = _each(_mm_nn, attn, v_new)
    o = _each(lambda a, b: a + b, qs, av)
    kv = _each(lambda k_, gt, gc, vn: _mm_tn(k_ * jnp.exp(gt - gc), vn), k, g_tot, gc_col, v_new)
    new_state = _each(lambda s_, gt, kv_: s_ * jnp.exp(gt) + kv_, state, g_tot, kv)
    return o, new_state, t


def _delta_lanes(reverse):
    return (N_GROUPS, 3 * N_GROUPS) if reverse else (0, 2 * N_GROUPS)


def _delta_fwd(q, k, v, gates, rows_f, rows_b, *, comm=None):
    s = q.shape[1]
    nc = s // DN_CHUNK
    heads = range(N_GROUPS)

    def body(qf_ref, kf_ref, vf_ref, gf_ref, rf_ref, qb_ref, kb_ref, vb_ref, gb_ref, rb_ref,
             of_ref, sf_ref, tf_ref, ob_ref, sb_ref, tb_ref, state):
        @pl.when(pl.program_id(0) == 0)
        def _():
            state[...] = jnp.zeros_like(state)

        qs, ks, vs, g_cols, g_rows, betas, flags = [], [], [], [], [], [], []
        for reverse, q_ref, k_ref, v_ref, gate, row_ref in ((False, qf_ref, kf_ref, vf_ref, gf_ref[...], rf_ref),
                                                            (True, qb_ref, kb_ref, vb_ref, gb_ref[...], rb_ref)):
            g_lane, b_lane = _delta_lanes(reverse)
            for j in heads:
                qs.append(q_ref[j])
                ks.append(k_ref[j])
                vs.append(v_ref[j])
                g_cols.append(gate[:, g_lane + j:g_lane + j + 1])
                g_rows.append(row_ref[0, j:j + 1, :])
                betas.append(gate[:, b_lane + j:b_lane + j + 1])
                flags.append(reverse)
        prev = tuple(state[n] for n in range(2 * N_GROUPS))
        o, new, t = _delta_chunk(tuple(qs), tuple(ks), tuple(vs), tuple(g_cols), tuple(g_rows), tuple(betas), prev, None, tuple(flags))
        for n in range(2 * N_GROUPS):
            o_ref, st_ref, t_ref = (of_ref, sf_ref, tf_ref) if n < N_GROUPS else (ob_ref, sb_ref, tb_ref)
            j = n % N_GROUPS
            st_ref[0, j] = prev[n]
            o_ref[j] = o[n]
            t_ref[0, j] = t[n]
            state[n] = new[n]

    def specs(chunk_of):
        hm = pl.BlockSpec((N_GROUPS, DN_CHUNK, HEAD), lambda c: (0, chunk_of(c), 0))
        ins = [hm, hm, hm, pl.BlockSpec((DN_CHUNK, LANE), lambda c: (chunk_of(c), 0)),
               pl.BlockSpec((1, N_GROUPS, DN_CHUNK), lambda c: (chunk_of(c), 0, 0))]
        outs = [hm, pl.BlockSpec((1, N_GROUPS, HEAD, HEAD), lambda c: (chunk_of(c), 0, 0, 0)),
                pl.BlockSpec((1, N_GROUPS, DN_CHUNK, DN_CHUNK), lambda c: (chunk_of(c), 0, 0, 0))]
        return ins, outs

    (ins_f, outs_f), (ins_b, outs_b) = specs(lambda c: c), specs(lambda c: nc - 1 - c)
    shapes = [jax.ShapeDtypeStruct((N_GROUPS, s, HEAD), F32), jax.ShapeDtypeStruct((nc, N_GROUPS, HEAD, HEAD), F32),
              jax.ShapeDtypeStruct((nc, N_GROUPS, DN_CHUNK, DN_CHUNK), F32)]
    results, sent = _call_with_comm(
        body, comm, name="delta_fwd" + ("" if comm is None else "_comm"), grid=(nc,),
        in_specs=ins_f + ins_b, out_specs=outs_f + outs_b, out_shape=shapes + shapes,
        scratch_shapes=[pltpu.VMEM((2 * N_GROUPS, HEAD, HEAD), F32)],
        args=(q, k, v, gates, rows_f, q, k, v, gates, rows_b), compiler_params=_params(("arbitrary",)))
    return results[:3], results[3:], sent


def _delta_bwd(q, k, v, gates, rows_f, rows_b, states_f, states_b, t_f, t_b, do, *, comm=None):
    s = q.shape[1]
    nc = s // DN_CHUNK
    heads = range(N_GROUPS)
    n_in, n_out = 8, 5

    def body(*refs):
        ins = (refs[:n_in], refs[n_in:2 * n_in])
        outs = (refs[2 * n_in:2 * n_in + n_out], refs[2 * n_in + n_out:2 * n_in + 2 * n_out])
        dstate = refs[-1]

        @pl.when(pl.program_id(0) == 0)
        def _():
            dstate[...] = jnp.zeros_like(dstate)

        qs, ks, vs, g_cols, g_rows, betas, sts, ts, dos, flags = ([] for _ in range(10))
        for reverse, (q_ref, k_ref, v_ref, gate_ref, row_ref, st_ref, t_ref, do_ref) in zip((False, True), ins):
            gate = gate_ref[...]
            g_lane, b_lane = _delta_lanes(reverse)
            for j in heads:
                qs.append(q_ref[j])
                ks.append(k_ref[j])
                vs.append(v_ref[j])
                g_cols.append(gate[:, g_lane + j:g_lane + j + 1])
                g_rows.append(row_ref[0, j:j + 1, :])
                betas.append(gate[:, b_lane + j:b_lane + j + 1])
                sts.append(st_ref[0, j])
                ts.append(t_ref[0, j])
                dos.append(do_ref[j])
                flags.append(reverse)
        t_known, flags = tuple(ts), tuple(flags)
        f = lambda q_, k_, v_, gc_, gr_, b_, s_: _delta_chunk(q_, k_, v_, gc_, gr_, b_, s_, t_known, flags)[:2]
        _, vjp = jax.vjp(f, tuple(qs), tuple(ks), tuple(vs), tuple(g_cols), tuple(g_rows), tuple(betas), tuple(sts))
        dq, dk, dv, dg_col, dg_row, dbeta, dprev = vjp((tuple(dos), tuple(dstate[n] for n in range(2 * N_GROUPS))))
        ids = lax.broadcasted_iota(jnp.int32, (DN_CHUNK, LANE), 1)
        for reverse, (dq_ref, dk_ref, dv_ref, dgate_ref, drow_ref) in zip((False, True), outs):
            g_lane, b_lane = _delta_lanes(reverse)
            dgate = jnp.zeros((DN_CHUNK, LANE), F32)
            for j in heads:
                n = j + (N_GROUPS if reverse else 0)
                dq_ref[j] = dq[n]
                dk_ref[j] = dk[n]
                dv_ref[j] = dv[n]
                dstate[n] = dprev[n]
                dgate = dgate + jnp.where(ids == g_lane + j, dg_col[n], 0.0) + jnp.where(ids == b_lane + j, dbeta[n], 0.0)
                drow_ref[0, j:j + 1, :] = dg_row[n]
            dgate_ref[...] = dgate

    def specs(chunk_of):
        hm = pl.BlockSpec((N_GROUPS, DN_CHUNK, HEAD), lambda c: (0, chunk_of(c), 0))
        gate = pl.BlockSpec((DN_CHUNK, LANE), lambda c: (chunk_of(c), 0))
        rows = pl.BlockSpec((1, N_GROUPS, DN_CHUNK), lambda c: (chunk_of(c), 0, 0))
        st = pl.BlockSpec((1, N_GROUPS, HEAD, HEAD), lambda c: (chunk_of(c), 0, 0, 0))
        ts = pl.BlockSpec((1, N_GROUPS, DN_CHUNK, DN_CHUNK), lambda c: (chunk_of(c), 0, 0, 0))
        return [hm, hm, hm, gate, rows, st, ts, hm], [hm, hm, hm, gate, rows]

    (ins_f, outs_f), (ins_b, outs_b) = specs(lambda c: nc - 1 - c), specs(lambda c: c)
    shapes = [jax.ShapeDtypeStruct((N_GROUPS, s, HEAD), F32)] * 3 + [jax.ShapeDtypeStruct((s, LANE), F32),
                                                                       jax.ShapeDtypeStruct((nc, N_GROUPS, DN_CHUNK), F32)]
    results, sent = _call_with_comm(
        body, comm, name="delta_bwd" + ("" if comm is None else "_comm"), grid=(nc,),
        in_specs=ins_f + ins_b, out_specs=outs_f + outs_b, out_shape=shapes + shapes,
        scratch_shapes=[pltpu.VMEM((2 * N_GROUPS, HEAD, HEAD), F32)],
        args=(q, k, v, gates, rows_f, states_f, t_f, do, q, k, v, gates, rows_b, states_b, t_b, do),
        compiler_params=_params(("arbitrary",)))
    return results[:n_out], results[n_out:], sent


ZB_BLOCK = 6


def _post_head(o_f, o_b, z, w):
    return _rms(o_f + o_b, w) * _silu(z)


def _post_fwd(o_f, o_b, proj, w, y, y_t, *, tile=256):
    s = proj.shape[0]
    tile = min(tile, s)

    def body(of_ref, ob_ref, z_ref, w_ref, _, __, y_ref, yt_ref):
        for h in range(N_GROUPS):
            lanes = pl.ds(h * HEAD, HEAD)
            y_h = _post_head(of_ref[h], ob_ref[h], z_ref[:, lanes], w_ref[...])
            y_ref[:, lanes] = y_h.astype(BF16)
            yt_ref[lanes, :] = y_h.T.astype(BF16)

    hm = pl.BlockSpec((N_GROUPS, tile, HEAD), lambda i: (0, i, 0))
    any_spec = pl.BlockSpec(memory_space=pl.ANY)
    return pl.pallas_call(
        body, name="post_fwd", grid=(s // tile,),
        in_specs=[hm, hm, pl.BlockSpec((tile, D_DN), lambda i: (i, ZB_BLOCK)), pl.BlockSpec((1, HEAD), lambda i: (0, 0)),
                  any_spec, any_spec],
        out_specs=[pl.BlockSpec((tile, D_DN), lambda i: (i, 1)), pl.BlockSpec((D_DN, tile), lambda i: (1, i))],
        out_shape=[jax.ShapeDtypeStruct(y.shape, BF16), jax.ShapeDtypeStruct(y_t.shape, BF16)],
        input_output_aliases={4: 0, 5: 1},
        compiler_params=_params(("arbitrary",)),
    )(o_f, o_b, proj, w, y, y_t)


def _post_bwd(o_f, o_b, proj, w, dy, *, tile=256):
    s = proj.shape[0]
    tile = min(tile, s)

    def body(of_ref, ob_ref, z_ref, w_ref, dy_ref, do_ref, dp_ref, dw_ref):
        @pl.when(pl.program_id(0) == 0)
        def _():
            dw_ref[...] = jnp.zeros_like(dw_ref)

        for h in range(N_GROUPS):
            lanes = pl.ds(h * HEAD, HEAD)
            _, vjp = jax.vjp(_post_head, of_ref[h], ob_ref[h], z_ref[:, lanes], w_ref[...])
            do, _, dz, dw = vjp(dy_ref[:, lanes])
            do_ref[h] = do
            dp_ref[:, lanes] = dz.astype(BF16)
            dw_ref[...] += dw

    hm = pl.BlockSpec((N_GROUPS, tile, HEAD), lambda i: (0, i, 0))
    zb = pl.BlockSpec((tile, D_DN), lambda i: (i, ZB_BLOCK))
    vec = pl.BlockSpec((1, HEAD), lambda i: (0, 0))
    return pl.pallas_call(
        body, name="post_bwd", grid=(s // tile,),
        in_specs=[hm, hm, zb, vec, pl.BlockSpec((tile, D_DN), lambda i: (i, 1))], out_specs=[hm, zb, vec],
        out_shape=[jax.ShapeDtypeStruct((N_GROUPS, s, HEAD), F32), jax.ShapeDtypeStruct((s, D_MAIN), BF16),
                   jax.ShapeDtypeStruct((1, HEAD), F32)],
        compiler_params=_params(("arbitrary",)),
    )(o_f, o_b, proj, w, dy)


def _my_index():
    return 4 * lax.axis_index("x") + 2 * lax.axis_index("y") + lax.axis_index("c")


def _peer(k):
    x, y, c = lax.axis_index("x"), lax.axis_index("y"), lax.axis_index("c")
    px, py, pc = x ^ ((k >> 2) & 1), y ^ ((k >> 1) & 1), c ^ (k & 1)
    return (px, py, pc), 4 * px + 2 * py + pc


class _Comm:
    def __init__(self, inputs, outputs, copies):
        self.inputs, self.outputs, self.copies = list(inputs), list(outputs), list(copies)
        self.kept = [o for o in self.outputs if not isinstance(o, jax.ShapeDtypeStruct)]

    def out_shapes(self):
        return [jax.ShapeDtypeStruct(o.shape, o.dtype) for o in self.outputs]

    def aliases(self, first_kept_input, first_output):
        kept_at = [i for i, o in enumerate(self.outputs) if not isinstance(o, jax.ShapeDtypeStruct)]
        return {first_kept_input + n: first_output + i for n, i in enumerate(kept_at)}

    def scratch(self):
        n = len(self.copies)
        return [pltpu.SemaphoreType.DMA((n, N_DEV - 1)), pltpu.SemaphoreType.DMA((n, N_DEV - 1)), pltpu.SemaphoreType.DMA((n,))]

    def _descriptors(self, in_refs, out_refs, sems, arrivals):
        send_sems, recv_sems, local_sems = sems
        me = _my_index()
        local, remote = [], []
        for k in range(N_DEV):
            peer, peer_idx = _peer(k)
            for a, (i_in, i_out, src_of, dst_of) in enumerate(self.copies):
                if k == 0:
                    if not arrivals:
                        local.append(pltpu.make_async_copy(src_of(in_refs[i_in], me), dst_of(out_refs[i_out], me), local_sems.at[a]))
                    continue
                receiver, sender = (me, peer_idx) if arrivals else (peer_idx, me)
                remote.append(pltpu.make_async_remote_copy(
                    src_ref=src_of(in_refs[i_in], receiver), dst_ref=dst_of(out_refs[i_out], sender),
                    send_sem=send_sems.at[a, k - 1], recv_sem=recv_sems.at[a, k - 1], device_id=peer,
                    device_id_type=pl.DeviceIdType.MESH))
        return local, remote

    def start(self, in_refs, out_refs, sems):
        local, sends = self._descriptors(in_refs, out_refs, sems, arrivals=False)
        for cp in local + sends:
            cp.start()

    def wait(self, in_refs, out_refs, sems):
        _, arriving = self._descriptors(in_refs, out_refs, sems, arrivals=True)
        for cp in arriving:
            cp.wait_recv()
        local, sends = self._descriptors(in_refs, out_refs, sems, arrivals=False)
        for cp in sends:
            cp.wait_send()
        for cp in local:
            cp.wait()


def _call_with_comm(body, comm, *, name, grid, in_specs, out_specs, out_shape, scratch_shapes, args, compiler_params):
    if comm is None:
        return pl.pallas_call(body, name=name, grid=grid, in_specs=in_specs, out_specs=out_specs, out_shape=out_shape,
                              scratch_shapes=scratch_shapes, compiler_params=compiler_params)(*args), []
    n_in, n_out, n_scratch = len(in_specs), len(out_specs), len(scratch_shapes)
    c_in, c_kept, c_out = len(comm.inputs), len(comm.kept), len(comm.outputs)

    def at_step(position):
        here = [pl.program_id(d) == p for d, p in enumerate(position)]
        return here[0] if len(here) == 1 else functools.reduce(jnp.logical_and, here)

    def wrapped(*refs):
        main_in, comm_in = refs[:n_in], refs[n_in:n_in + c_in]
        o = n_in + c_in + c_kept
        main_out, comm_out = refs[o:o + n_out], refs[o + n_out:o + n_out + c_out]
        s = o + n_out + c_out
        main_scratch, sems = refs[s:s + n_scratch], refs[s + n_scratch:]

        @pl.when(at_step([0] * len(grid)))
        def _():
            comm.start(comm_in, comm_out, sems)

        body(*main_in, *main_out, *main_scratch)

        @pl.when(at_step([g - 1 for g in grid]))
        def _():
            comm.wait(comm_in, comm_out, sems)

    any_spec = pl.BlockSpec(memory_space=pl.ANY)
    results = pl.pallas_call(
        wrapped, name=name, grid=grid, in_specs=list(in_specs) + [any_spec] * (c_in + c_kept),
        out_specs=list(out_specs) + [any_spec] * c_out, out_shape=list(out_shape) + comm.out_shapes(),
        scratch_shapes=list(scratch_shapes) + comm.scratch(), input_output_aliases=comm.aliases(n_in + c_in, n_out),
        compiler_params=compiler_params,
    )(*args, *comm.inputs, *comm.kept)
    return results[:n_out], results[n_out:]


def _exchange(name, comm):
    c_in, c_kept, c_out = len(comm.inputs), len(comm.kept), len(comm.outputs)

    def body(*refs):
        in_refs, out_refs, sems = refs[:c_in], refs[c_in + c_kept:c_in + c_kept + c_out], refs[c_in + c_kept + c_out:]
        comm.start(in_refs, out_refs, sems)
        comm.wait(in_refs, out_refs, sems)

    any_spec = pl.BlockSpec(memory_space=pl.ANY)
    return pl.pallas_call(
        body, name=name, in_specs=[any_spec] * (c_in + c_kept), out_specs=[any_spec] * c_out, out_shape=comm.out_shapes(),
        scratch_shapes=comm.scratch(), input_output_aliases=comm.aliases(c_in, 0),
    )(*comm.inputs, *comm.kept)


def _whole(ref, _):
    return ref


def _slot(ref, dev):
    return ref.at[dev]


def _row_block(size, axis):
    def pick(ref, dev):
        start = pl.multiple_of(dev * size, size) if size % SUBLANES == 0 else dev * size
        return ref.at[tuple([slice(None)] * axis + [pl.ds(start, size)])]
    return pick


def _blocked(shape, block):
    nd = len(block)
    grid = tuple(shape[d] // block[d] for d in range(nd) if block[d] != shape[d])
    return grid, (lambda *i: tuple(i) + (0,) * (nd - len(grid)))


def _sum_parts(name, parts, block, comm=None):
    n_parts = parts.shape[0]
    grid, idx = _blocked(parts.shape[1:], block)

    def body(p_ref, g_ref):
        g = p_ref[0].astype(F32)
        for j in range(1, n_parts):
            g = g + p_ref[j].astype(F32)
        g_ref[...] = g

    (total,), sent = _call_with_comm(
        body, comm, name=name, grid=grid, in_specs=[pl.BlockSpec((n_parts,) + tuple(block), lambda *i: (0,) + idx(*i))],
        out_specs=[pl.BlockSpec(block, idx)], out_shape=[jax.ShapeDtypeStruct(parts.shape[1:], F32)], scratch_shapes=[],
        args=(parts,), compiler_params=_params(("arbitrary",) * len(grid)))
    return total, sent


def _adamw(name, parts, w, m, v, block):
    n_parts = parts.shape[0]
    nd = len(block)
    grid, _ = _blocked(w.shape, block)
    lead = len(grid)

    def body(p_ref, w_ref, m_ref, v_ref, g_ref, d_ref, mo_ref, vo_ref):
        g = p_ref[0].astype(F32)
        for j in range(1, n_parts):
            g = g + p_ref[j].astype(F32)
        m_new = ADAM_B1 * m_ref[...] + (1.0 - ADAM_B1) * g
        v_new = ADAM_B2 * v_ref[...] + (1.0 - ADAM_B2) * jnp.square(g)
        m_hat = m_new / np.float32(1.0 - ADAM_B1 ** ADAM_STEP)
        v_hat = v_new / np.float32(1.0 - ADAM_B2 ** ADAM_STEP)
        g_ref[...] = g
        d_ref[...] = -ADAM_LR * (m_hat / (jnp.sqrt(v_hat) + ADAM_EPS) + ADAM_WD * w_ref[...])
        mo_ref[...] = m_new
        vo_ref[...] = v_new

    idx = lambda *i: tuple(i) + (0,) * (nd - lead)
    spec = pl.BlockSpec(block, idx)
    pspec = pl.BlockSpec((n_parts,) + tuple(block), lambda *i: (0,) + idx(*i))
    n_blk = int(np.prod(block[:-1])) * (-(-block[-1] // LANE) * LANE)
    return pl.pallas_call(
        body, name=name, grid=grid, in_specs=[pspec, spec, spec, spec], out_specs=[spec] * 4,
        out_shape=[jax.ShapeDtypeStruct(w.shape, F32)] * 4,
        compiler_params=_params(("arbitrary",) * lead, 2 * (n_parts + 7) * n_blk * 4 + VMEM_MARGIN),
    )(parts, w, m, v)


def _to_flat(pieces, row_multiple):
    flat = jnp.concatenate([p.reshape(-1) for p in pieces])
    rows = -(-flat.shape[0] // FLAT_W)
    rows = -(-rows // row_multiple) * row_multiple
    return jnp.pad(flat, (0, rows * FLAT_W - flat.shape[0])).reshape(rows, FLAT_W)


def _lane_row(*pieces):
    flat = jnp.concatenate([p.reshape(-1) for p in pieces])
    return jnp.pad(flat, (0, LANE - flat.shape[0])).reshape(1, LANE)


def _rows_view(a):
    return a.reshape(a.shape[:-1] + (SUBLANES, LANE))


def _in_shard_t(a):
    return _rows_view(jnp.swapaxes(a, 1, 2))


def kernel(x, norm_w, w_in, sgu_ln_g, sgu_ln_b, sgu_w, sgu_b, conv_w, a_log_f, a_log_b, dt_bias_f, dt_bias_b, dn_norm_w, w_out, final_norm_w, loss_target, m_norm_w, m_w_in, m_sgu_ln_g, m_sgu_ln_b, m_sgu_w, m_sgu_b, m_conv_w, m_a_log_f, m_a_log_b, m_dt_bias_f, m_dt_bias_b, m_dn_norm_w, m_w_out, m_final_norm_w, v_norm_w, v_w_in, v_sgu_ln_g, v_sgu_ln_b, v_sgu_w, v_sgu_b, v_conv_w, v_a_log_f, v_a_log_b, v_dt_bias_f, v_dt_bias_b, v_dn_norm_w, v_w_out, v_final_norm_w):
    weights = dict(norm_w=norm_w, w_in=w_in, sgu_ln_g=sgu_ln_g, sgu_ln_b=sgu_ln_b, sgu_w=sgu_w, sgu_b=sgu_b, conv_w=conv_w,
                   a_log_f=a_log_f, a_log_b=a_log_b, dt_bias_f=dt_bias_f, dt_bias_b=dt_bias_b, dn_norm_w=dn_norm_w,
                   w_out=w_out, final_norm_w=final_norm_w)
    m_in = dict(norm_w=m_norm_w, w_in=m_w_in, sgu_ln_g=m_sgu_ln_g, sgu_ln_b=m_sgu_ln_b, sgu_w=m_sgu_w, sgu_b=m_sgu_b,
                conv_w=m_conv_w, a_log_f=m_a_log_f, a_log_b=m_a_log_b, dt_bias_f=m_dt_bias_f, dt_bias_b=m_dt_bias_b,
                dn_norm_w=m_dn_norm_w, w_out=m_w_out, final_norm_w=m_final_norm_w)
    v_in = dict(norm_w=v_norm_w, w_in=v_w_in, sgu_ln_g=v_sgu_ln_g, sgu_ln_b=v_sgu_ln_b, sgu_w=v_sgu_w, sgu_b=v_sgu_b,
                conv_w=v_conv_w, a_log_f=v_a_log_f, a_log_b=v_a_log_b, dt_bias_f=v_dt_bias_f, dt_bias_b=v_dt_bias_b,
                dn_norm_w=v_dn_norm_w, w_out=v_w_out, final_norm_w=v_final_norm_w)
    s = x.shape[1]
    nc = s // DN_CHUNK
    depth = w_in.shape[0]
    in_shard, out_shard, conv_shard = w_in.shape[2], w_out.shape[1], conv_w.shape[2]
    x0 = x.reshape(s, D_MODEL)
    target = loss_target.reshape(s, D_MODEL)

    w_in_shard = _in_shard_t(w_in.astype(BF16))
    w_out_shard = w_out.astype(BF16)

    gathered_in_shape = jax.ShapeDtypeStruct((N_DEV * in_shard, SUBLANES, LANE), BF16)
    gathered_out_shape = jax.ShapeDtypeStruct((D_MODEL, D_MODEL), BF16)
    gathered_in, gathered_out, conv_all = _exchange("gather_first", _Comm(
        [w_in_shard[0], w_out_shard[0], conv_w],
        [gathered_in_shape, gathered_out_shape, jax.ShapeDtypeStruct((N_DEV,) + conv_w.shape, F32)],
        [(0, 0, _whole, _row_block(in_shard, 0)), (1, 1, _whole, _row_block(out_shard, 0)), (2, 2, _whole, _slot)]))
    gather_plan = lambda l: _Comm([w_in_shard[l], w_out_shard[l]], [gathered_in_shape, gathered_out_shape],
                                  [(0, 0, _whole, _row_block(in_shard, 0)), (1, 1, _whole, _row_block(out_shard, 0))])
    conv_full = conv_all.transpose(1, 2, 0, 3).reshape(depth, CONV_W, 3 * D_DN)
    conv_full = jnp.pad(conv_full, ((0, 0), (0, HALO - CONV_W), (0, 0)))
    sgu_b_t = jnp.swapaxes(sgu_b, 1, 2)

    def rows_of(gates, first_lane):
        return gates[:, first_lane:first_lane + N_GROUPS].reshape(nc, DN_CHUNK, N_GROUPS).transpose(0, 2, 1)

    saved = []
    xl = x0
    for l in range(depth):
        w_in_t = gathered_in.reshape(D_IN, D_MODEL)
        w_gate_t = jnp.pad(w_in_t[D_MAIN:], ((0, LANE - N_GATE), (0, 0)))
        w_out_l = gathered_out
        nw = norm_w[l].reshape(1, D_MODEL)
        h, h_t = _rms_fwd(xl, nw)
        proj = _matmul([(_lhs(h, tm=MM_WIDE), _rhs(w_in_t, MM_WIDE, transposed=True))], s, D_MAIN, name="in_proj", trans_b=True, tn=MM_WIDE)
        p_gate = _matmul([(_lhs(h), _rhs(w_gate_t, LANE, transposed=True))], s, LANE, name="in_proj_gate", trans_b=True, tn=LANE)
        ln_g, ln_b = sgu_ln_g[l].reshape(1, D_SGU), sgu_ln_b[l].reshape(1, D_SGU)
        y, y_t = _sgu_fwd(proj, ln_g, ln_b, sgu_w[l], sgu_b_t[l])
        a_log = _lane_row(a_log_f[l], a_log_b[l])
        dt_bias = _lane_row(dt_bias_f[l], dt_bias_b[l])
        q, k, v, gates = _prep_fwd(proj, conv_full[l], p_gate, a_log, dt_bias)
        rows_f, rows_b = rows_of(gates, 0), rows_of(gates, N_GROUPS)
        more = l + 1 < depth
        (o_f, st_f, t_f), (o_b, st_b, t_b), next_weights = _delta_fwd(q, k, v, gates, rows_f, rows_b,
                                                                     comm=gather_plan(l + 1) if more else None)
        dn_w = dn_norm_w[l].reshape(1, HEAD)
        y, y_t = _post_fwd(o_f, o_b, proj, dn_w, y, y_t)
        x_next = _matmul([(_lhs(y, tm=MM_WIDE), _rhs(w_out_l, MM_WIDE))], s, D_MODEL, name="out_proj", add=xl, tn=MM_WIDE)
        saved.append(dict(x=xl, nw=nw, h_t=h_t, proj=proj, p_gate=p_gate, ln_g=ln_g, ln_b=ln_b, a_log=a_log, dt_bias=dt_bias,
                          q=q, k=k, v=v, gates=gates, rows_f=rows_f, rows_b=rows_b, o_f=o_f, o_b=o_b, st_f=st_f, st_b=st_b,
                          t_f=t_f, t_b=t_b, dn_w=dn_w, y_t=y_t, w_in_t=w_in_t, w_gate_t=w_gate_t, w_out=w_out_l))
        xl = x_next
        if more:
            gathered_in, gathered_out = next_weights

    loss_part, dx, dx_bf, d_final = _final_loss(xl, final_norm_w.reshape(1, D_MODEL), target)
    loss = lax.psum(loss_part[0, 0], MESH_AXES)

    small_names = [n for n in SMALL if n not in ("conv_w", "final_norm_w")]
    grads = {n: [None] * depth for n in small_names + ["conv_w"]}
    t_rows = -(-D_IN // LANE) * LANE
    layer_slot = lambda l: (lambda ref, dev: ref.at[dev, l])
    parts = [jax.ShapeDtypeStruct((N_DEV, depth, in_shard, SUBLANES, LANE), BF16),
             jax.ShapeDtypeStruct((N_DEV, depth, out_shard, D_MODEL), BF16)]

    def exchange_plan(layer, g_in_rows, g_out, more_inputs=(), more_outputs=(), more_copies=()):
        return _Comm([g_in_rows, g_out, *more_inputs], [*parts, *more_outputs],
                     [(0, 0, _row_block(in_shard, 0), layer_slot(layer)), (1, 1, _row_block(out_shard, 0), layer_slot(layer)),
                      *more_copies])

    pending = None
    for l in reversed(range(depth)):
        sv = saved[l]
        dy = _matmul([(_lhs(dx_bf, tm=MM_WIDE), _rhs(sv["w_out"], MM_WIDE, transposed=True))], s, D_MODEL, name="out_proj_dy", trans_b=True,
                     tn=MM_WIDE)
        g_out = _matmul([(_lhs(sv["y_t"]), _rhs(dx_bf, MM_TILE))], D_MODEL, D_MODEL, name="out_proj_dw", out_dtype=BF16)
        do, dp, d_dn = _post_bwd(sv["o_f"], sv["o_b"], sv["proj"], sv["dn_w"], dy)
        (*dqkv_f, dgate_f, drows_f), (*dqkv_b, dgate_b, drows_b), exchanged = _delta_bwd(
            sv["q"], sv["k"], sv["v"], sv["gates"], sv["rows_f"], sv["rows_b"], sv["st_f"], sv["st_b"], sv["t_f"], sv["t_b"], do,
            comm=None if pending is None else exchange_plan(*pending))
        if exchanged:
            parts = list(exchanged)
        drows = jnp.concatenate([drows_f.transpose(0, 2, 1).reshape(s, N_GROUPS), drows_b.transpose(0, 2, 1).reshape(s, N_GROUPS)], axis=1)
        drows = jnp.pad(drows, ((0, 0), (0, LANE - 2 * N_GROUPS)))
        dp, d_gate, d_conv, d_alog, d_dt = _prep_bwd(sv["proj"], conv_full[l], sv["p_gate"], sv["a_log"], sv["dt_bias"],
                                                     dqkv_f, dqkv_b, (dgate_f, dgate_b, drows), dp)
        dp, d_lg, d_lb, d_sw, d_sbt = _sgu_bwd(sv["proj"], sv["ln_g"], sv["ln_b"], sgu_w[l], sgu_b_t[l], dy, dp)
        h_t = sv["h_t"]
        g_in_t = _matmul([(_lhs(h_t), _rhs(dp, MM_TILE))], D_MODEL, D_MAIN, name="in_proj_dw", out_dtype=BF16, out_t_rows=t_rows)
        g_in_t = _matmul([(_lhs(h_t), _rhs(d_gate, LANE))], D_MODEL, LANE, name="in_proj_dw_gate", out_dtype=BF16, tn=LANE,
                         transpose_into=g_in_t, out_row_block=D_MAIN // LANE)
        pending = (l, _rows_view(g_in_t), g_out)
        dh_pairs = [(_lhs(dp), _rhs(sv["w_in_t"], MM_TILE, k=D_MAIN)), (_lhs(d_gate), _rhs(sv["w_gate_t"], MM_TILE))]
        if l > 0:
            dh = _matmul(dh_pairs, s, D_MODEL, name="in_proj_dh")
        else:
            dh, parts = _matmul(dh_pairs, s, D_MODEL, name="in_proj_dh_comm", comm=exchange_plan(*pending))
        dx, dx_bf, d_nw = _rms_bwd(sv["x"], sv["nw"], dh, dx)
        grads["norm_w"][l] = d_nw.reshape(D_MODEL)
        grads["sgu_ln_g"][l], grads["sgu_ln_b"][l] = d_lg.reshape(D_SGU), d_lb.reshape(D_SGU)
        grads["sgu_w"][l], grads["sgu_b"][l] = d_sw, d_sbt.T
        grads["conv_w"][l] = d_conv[:CONV_W]
        grads["a_log_f"][l], grads["a_log_b"][l] = d_alog[0, :N_GROUPS], d_alog[0, N_GROUPS:2 * N_GROUPS]
        grads["dt_bias_f"][l], grads["dt_bias_b"][l] = d_dt[0, :N_GROUPS], d_dt[0, N_GROUPS:2 * N_GROUPS]
        grads["dn_norm_w"][l] = d_dn.reshape(HEAD)
    grad_x = dx.reshape(1, s, D_MODEL)

    g_small = {n: jnp.stack(grads[n]) for n in small_names}
    g_small["final_norm_w"] = d_final.reshape(D_MODEL)
    conv_by_dev = jnp.stack(grads["conv_w"]).reshape(depth, CONV_W, N_DEV, conv_shard).transpose(2, 0, 1, 3).reshape(N_DEV, -1)
    replicated = jnp.concatenate([g_small[n].reshape(-1) for n in SMALL if n != "conv_w"])
    n_small = conv_by_dev.shape[1] + replicated.shape[0]
    small_rows = -(-n_small // (FLAT_W * HALO)) * HALO
    send_small = jnp.concatenate([conv_by_dev, jnp.broadcast_to(replicated, (N_DEV, replicated.shape[0])),
                                  jnp.zeros((N_DEV, small_rows * FLAT_W - n_small), F32)], axis=1).reshape(N_DEV, small_rows, FLAT_W)
    parts_in, parts_out = parts

    in_rows = 100 if in_shard % 100 == 0 else in_shard
    g_in_t, (parts_small,) = _sum_parts("sum_in_comm", parts_in, (1, in_rows, SUBLANES, LANE), comm=_Comm(
        [send_small], [jax.ShapeDtypeStruct((N_DEV, small_rows, FLAT_W), F32)], [(0, 0, _slot, _slot)]))
    g_in = jnp.swapaxes(g_in_t.reshape(depth, in_shard, D_MODEL), 1, 2)
    res_in = _adamw("adamw_in", g_in[None], w_in, m_w_in, v_w_in, (1, min(256, D_MODEL), in_shard))
    res_out = _adamw("adamw_out", parts_out, w_out, m_w_out, v_w_out, (1, min(128, out_shard), D_MODEL))
    res_small = _adamw("adamw_small", parts_small, _to_flat([weights[n] for n in SMALL], HALO), _to_flat([m_in[n] for n in SMALL], HALO),
                       _to_flat([v_in[n] for n in SMALL], HALO), (small_rows // 4 if small_rows % 32 == 0 else small_rows, FLAT_W))
    outs = [dict(w_in=a, w_out=b) for a, b in zip(res_in, res_out)]
    off = 0
    for n in SMALL:
        size = int(np.prod(weights[n].shape))
        for o, d in zip(res_small, outs):
            d[n] = o.reshape(-1)[off:off + size].reshape(weights[n].shape)
        off += size
    g_out, delta_out, m_out, v_out = outs
    return (loss, grad_x, *[g_out[n] for n in WEIGHTS], *[delta_out[n] for n in WEIGHTS],
            *[m_out[n] for n in WEIGHTS], *[v_out[n] for n in WEIGHTS])
```

```python
import functools

import numpy as np
import jax
import jax.numpy as jnp
from jax import lax
from jax.experimental import pallas as pl
from jax.experimental.pallas import tpu as pltpu

F32 = jnp.float32
BF16 = jnp.bfloat16

N_DEV = 8
D_MODEL = 2048
D_SGU = 1024
D_DN = 1024
N_GROUPS = 8
HEAD = 128
SGU_CHUNK = 128
DN_CHUNK = 64
CONV_W = 5
N_GATE = 32
D_MAIN = 3 * D_SGU + 4 * D_DN
D_IN = D_MAIN + N_GATE
LANE = 128
SUBLANES = 16
HALO = 8
EPS = 1e-6
ADAM_LR, ADAM_B1, ADAM_B2, ADAM_EPS, ADAM_WD, ADAM_STEP = 0.001, 0.9, 0.999, 1e-08, 0.01, 10
FLAT_W = 1024
MM_TILE = 512
MM_WIDE = 1024
VMEM_MARGIN = 8 << 20

MESH_AXES = ("x", "y", "c")
WEIGHTS = ("norm_w", "w_in", "sgu_ln_g", "sgu_ln_b", "sgu_w", "sgu_b", "conv_w", "a_log_f", "a_log_b",
           "dt_bias_f", "dt_bias_b", "dn_norm_w", "w_out", "final_norm_w")
SMALL = ("conv_w",) + tuple(n for n in WEIGHTS if n not in ("w_in", "w_out", "conv_w"))


def _bdot(a, b, dims):
    return lax.dot_general(a.astype(BF16), b.astype(BF16), (dims, ((), ())), preferred_element_type=F32)


@jax.custom_vjp
def _mm_nn(a, b):
    return _bdot(a, b, ((1,), (0,)))


def _mm_nn_fwd(a, b):
    return _mm_nn(a, b), (a, b)


def _mm_nn_bwd(res, ct):
    a, b = res
    return _bdot(ct, b, ((1,), (1,))), _bdot(a, ct, ((0,), (0,)))


_mm_nn.defvjp(_mm_nn_fwd, _mm_nn_bwd)


@jax.custom_vjp
def _mm_nt(a, b):
    return _bdot(a, b, ((1,), (1,)))


def _mm_nt_fwd(a, b):
    return _mm_nt(a, b), (a, b)


def _mm_nt_bwd(res, ct):
    a, b = res
    return _bdot(ct, b, ((1,), (0,))), _bdot(ct, a, ((0,), (0,)))


_mm_nt.defvjp(_mm_nt_fwd, _mm_nt_bwd)


@jax.custom_vjp
def _mm_tn(a, b):
    return _bdot(a, b, ((0,), (0,)))


def _mm_tn_fwd(a, b):
    return _mm_tn(a, b), (a, b)


def _mm_tn_bwd(res, ct):
    a, b = res
    return _bdot(b, ct, ((1,), (1,))), _bdot(a, ct, ((1,), (0,)))


_mm_tn.defvjp(_mm_tn_fwd, _mm_tn_bwd)


@jax.custom_vjp
def _inverse_given(m, t):
    return t


def _inverse_given_fwd(m, t):
    return t, t


def _inverse_given_bwd(t, ct):
    dm = -_bdot(t, _bdot(ct, t, ((1,), (1,))), ((0,), (0,)))
    return dm, jnp.zeros_like(t)


_inverse_given.defvjp(_inverse_given_fwd, _inverse_given_bwd)


@jax.custom_vjp
def _split_lanes(x):
    return tuple(x[:, i * LANE:(i + 1) * LANE] for i in range(x.shape[1] // LANE))


def _split_lanes_fwd(x):
    return _split_lanes(x), None


def _split_lanes_bwd(_, cts):
    return (jnp.concatenate(cts, axis=1),)


_split_lanes.defvjp(_split_lanes_fwd, _split_lanes_bwd)


def _silu(t):
    return t * jax.nn.sigmoid(t)


def _gelu(t):
    return 0.5 * t * (1.0 + lax.erf(t * np.float32(0.7071067811865476)))


def _rms(x, w):
    return x * lax.rsqrt(jnp.mean(x * x, axis=-1, keepdims=True) + EPS) * w


def _params(sem, vmem_bytes=None):
    kw = dict(dimension_semantics=sem)
    if vmem_bytes is not None:
        kw["vmem_limit_bytes"] = int(vmem_bytes)
    return pltpu.CompilerParams(**kw)


def _lhs(a, k=None, k_block=0, tm=MM_TILE):
    k = a.shape[1] if k is None else k
    tm = min(tm, a.shape[0])
    return a, pl.BlockSpec((tm, k), lambda i, j: (i, k_block)), tm * k * a.dtype.itemsize


def _rhs(b, tn, *, layer=None, transposed=False, k=None, k_block=0, n_offset=0):
    lead = () if layer is None else (layer,)
    none = () if layer is None else (None,)
    if transposed:
        k = b.shape[-1]
        spec = pl.BlockSpec(none + (tn, k), lambda i, j: lead + (j + n_offset, 0))
    else:
        k = b.shape[-2] if k is None else k
        spec = pl.BlockSpec(none + (k, tn), lambda i, j: lead + (k_block, j))
    return b, spec, tn * k * b.dtype.itemsize


def _matmul(pairs, m, n, *, name, trans_b=False, add=None, out_dtype=F32, tn=MM_TILE, out_t_rows=None, transpose_into=None,
            out_row_block=0, comm=None):
    tm = pairs[0][0][1].block_shape[0]
    assert m % tm == 0 and n % tn == 0 and all(a_spec.block_shape[0] == tm for (_, a_spec, _), _ in pairs)
    n_pairs = len(pairs)
    contract = ((1,), (1,)) if trans_b else ((1,), (0,))
    transposed_out = out_t_rows is not None or transpose_into is not None

    def body(*refs):
        o_ref = refs[-1]
        acc = None
        for i in range(n_pairs):
            d = lax.dot_general(refs[2 * i][...].astype(BF16), refs[2 * i + 1][...].astype(BF16),
                                (contract, ((), ())), preferred_element_type=F32)
            acc = d if acc is None else acc + d
        if add is not None:
            acc = acc + refs[2 * n_pairs][...]
        o_ref[...] = (acc.T if transposed_out else acc).astype(out_dtype)

    in_specs, args, vmem, aliases = [], [], 0, {}
    for (a, a_spec, a_bytes), (b, b_spec, b_bytes) in pairs:
        in_specs += [a_spec, b_spec]
        args += [a, b]
        vmem += 2 * (a_bytes + b_bytes)
    if add is not None:
        in_specs.append(pl.BlockSpec((tm, tn), lambda i, j: (i, j)))
        args.append(add)
        vmem += 2 * tm * tn * 4
    vmem += 2 * tm * tn * jnp.dtype(out_dtype).itemsize + 3 * tm * tn * 4
    if transposed_out:
        out_spec = pl.BlockSpec((tn, tm), lambda i, j: (j + out_row_block, i))
        if transpose_into is not None:
            in_specs.append(pl.BlockSpec(memory_space=pl.ANY))
            args.append(transpose_into)
            aliases = {len(args) - 1: 0}
            out_shape = jax.ShapeDtypeStruct(transpose_into.shape, out_dtype)
        else:
            out_shape = jax.ShapeDtypeStruct((out_t_rows, m), out_dtype)
    else:
        out_spec = pl.BlockSpec((tm, tn), lambda i, j: (i, j))
        out_shape = jax.ShapeDtypeStruct((m, n), out_dtype)
    if comm is not None:
        assert not aliases
        (out,), sent = _call_with_comm(
            body, comm, name=name, grid=(m // tm, n // tn), in_specs=in_specs, out_specs=[out_spec], out_shape=[out_shape],
            scratch_shapes=[], args=args, compiler_params=_params(("arbitrary", "arbitrary"), vmem + VMEM_MARGIN))
        return out, sent
    return pl.pallas_call(
        body, name=name, grid=(m // tm, n // tn), in_specs=in_specs, out_specs=out_spec, out_shape=out_shape,
        input_output_aliases=aliases, compiler_params=_params(("parallel", "arbitrary"), vmem + VMEM_MARGIN),
    )(*args)


def _rms_fwd(x, w, *, tile=512):
    s, d = x.shape
    tile = min(tile, s)

    def body(x_ref, w_ref, h_ref, ht_ref):
        h = _rms(x_ref[...], w_ref[...])
        h_ref[...] = h.astype(BF16)
        ht_ref[...] = h.T.astype(BF16)

    return pl.pallas_call(
        body, name="rms_fwd", grid=(s // tile,),
        in_specs=[pl.BlockSpec((tile, d), lambda i: (i, 0)), pl.BlockSpec((1, d), lambda i: (0, 0))],
        out_specs=[pl.BlockSpec((tile, d), lambda i: (i, 0)), pl.BlockSpec((d, tile), lambda i: (0, i))],
        out_shape=[jax.ShapeDtypeStruct((s, d), BF16), jax.ShapeDtypeStruct((d, s), BF16)],
        compiler_params=_params(("arbitrary",)),
    )(x, w)


def _rms_bwd(x, w, dh, dx_out, *, tile=256):
    s, d = x.shape
    tile = min(tile, s)

    def body(x_ref, w_ref, dh_ref, dxo_ref, dx_ref, dxb_ref, dw_ref):
        _, vjp = jax.vjp(_rms, x_ref[...], w_ref[...])
        dxv, dwv = vjp(dh_ref[...])
        dx = dxo_ref[...] + dxv
        dx_ref[...] = dx
        dxb_ref[...] = dx.astype(BF16)

        @pl.when(pl.program_id(0) == 0)
        def _():
            dw_ref[...] = jnp.zeros_like(dw_ref)

        dw_ref[...] += dwv

    row = pl.BlockSpec((tile, d), lambda i: (i, 0))
    vec = pl.BlockSpec((1, d), lambda i: (0, 0))
    return pl.pallas_call(
        body, name="rms_bwd", grid=(s // tile,),
        in_specs=[row, vec, row, row], out_specs=[row, row, vec],
        out_shape=[jax.ShapeDtypeStruct((s, d), F32), jax.ShapeDtypeStruct((s, d), BF16), jax.ShapeDtypeStruct((1, d), F32)],
        compiler_params=_params(("arbitrary",)),
    )(x, w, dh, dx_out)


def _final_loss(x, w, target, *, tile=256):
    s, d = x.shape
    tile = min(tile, s)

    def body(x_ref, w_ref, t_ref, loss_ref, dx_ref, dxb_ref, dw_ref):
        def f(xv, wv):
            err = jnp.square(_rms(xv, wv) - t_ref[...])
            per_token = jnp.sum(err, axis=1, keepdims=True) * np.float32(1.0 / d)
            return 0.5 * jnp.sum(per_token, axis=0, keepdims=True)

        loss, vjp = jax.vjp(f, x_ref[...], w_ref[...])
        dxv, dwv = vjp(jnp.ones((1, 1), F32))
        dx_ref[...] = dxv
        dxb_ref[...] = dxv.astype(BF16)

        @pl.when(pl.program_id(0) == 0)
        def _():
            dw_ref[...] = jnp.zeros_like(dw_ref)
            loss_ref[...] = jnp.zeros_like(loss_ref)

        dw_ref[...] += dwv
        loss_ref[...] += jnp.broadcast_to(loss, (1, LANE))

    row = pl.BlockSpec((tile, d), lambda i: (i, 0))
    vec = pl.BlockSpec((1, d), lambda i: (0, 0))
    return pl.pallas_call(
        body, name="final_loss", grid=(s // tile,),
        in_specs=[row, vec, row], out_specs=[pl.BlockSpec((1, LANE), lambda i: (0, 0)), row, row, vec],
        out_shape=[jax.ShapeDtypeStruct((1, LANE), F32), jax.ShapeDtypeStruct((s, d), F32),
                   jax.ShapeDtypeStruct((s, d), BF16), jax.ShapeDtypeStruct((1, d), F32)],
        compiler_params=_params(("arbitrary",)),
    )(x, w, target)


def _sgu_chunk(u, v, z, ln_g, ln_b, ws, bcols):
    vg = _gelu(v)
    xc = vg - jnp.mean(vg, axis=-1, keepdims=True)
    vl = xc * lax.rsqrt(jnp.mean(xc * xc, axis=-1, keepdims=True) + EPS) * ln_g + ln_b
    sp = jnp.concatenate([_mm_nn(ws[g], vg_) + bcols[g] for g, vg_ in enumerate(_split_lanes(vl))], axis=1)
    return _gelu(u) * sp * _silu(z)


def _sgu_specs(tile):
    col = lambda j: pl.BlockSpec((tile, D_SGU), lambda i, j=j: (i, j))
    vec = pl.BlockSpec((1, D_SGU), lambda i: (0, 0))
    wspec = pl.BlockSpec((N_GROUPS, SGU_CHUNK, SGU_CHUNK), lambda i: (0, 0, 0))
    bspec = pl.BlockSpec((SGU_CHUNK, N_GROUPS), lambda i: (0, 0))
    return col, vec, wspec, bspec


def _sgu_fwd(proj, ln_g, ln_b, w_s, b_t, *, tile=256):
    s = proj.shape[0]
    tile = min(tile, s)
    col, vec, wspec, bspec = _sgu_specs(tile)

    def body(u_ref, v_ref, z_ref, g_ref, b_ref, w_ref, bt_ref, y_ref, yt_ref):
        ws = tuple(w_ref[g] for g in range(N_GROUPS))
        bcols = tuple(bt_ref[:, g:g + 1] for g in range(N_GROUPS))
        for c in range(tile // SGU_CHUNK):
            rows = pl.ds(c * SGU_CHUNK, SGU_CHUNK)
            y = _sgu_chunk(u_ref[rows, :], v_ref[rows, :], z_ref[rows, :], g_ref[...], b_ref[...], ws, bcols)
            y_ref[rows, :] = y.astype(BF16)
            yt_ref[:, rows] = y.T.astype(BF16)

    return pl.pallas_call(
        body, name="sgu_fwd", grid=(s // tile,),
        in_specs=[col(0), col(1), col(2), vec, vec, wspec, bspec],
        out_specs=[pl.BlockSpec((tile, D_SGU), lambda i: (i, 0)), pl.BlockSpec((D_SGU, tile), lambda i: (0, i))],
        out_shape=[jax.ShapeDtypeStruct((s, D_SGU + D_DN), BF16), jax.ShapeDtypeStruct((D_SGU + D_DN, s), BF16)],
        compiler_params=_params(("arbitrary",)),
    )(proj, proj, proj, ln_g, ln_b, w_s, b_t)


def _sgu_bwd(proj, ln_g, ln_b, w_s, b_t, dy, dp, *, tile=128):
    s = proj.shape[0]
    tile = min(tile, s)
    col, vec, wspec, bspec = _sgu_specs(tile)

    def body(u_ref, v_ref, z_ref, g_ref, b_ref, w_ref, bt_ref, dy_ref, _, dp_ref, dg_ref, db_ref, dw_ref, dbt_ref):
        @pl.when(pl.program_id(0) == 0)
        def _():
            dg_ref[...] = jnp.zeros_like(dg_ref)
            db_ref[...] = jnp.zeros_like(db_ref)
            dw_ref[...] = jnp.zeros_like(dw_ref)
            dbt_ref[...] = jnp.zeros_like(dbt_ref)

        ws = tuple(w_ref[g] for g in range(N_GROUPS))
        bcols = tuple(bt_ref[:, g:g + 1] for g in range(N_GROUPS))
        for c in range(tile // SGU_CHUNK):
            rows = pl.ds(c * SGU_CHUNK, SGU_CHUNK)
            _, vjp = jax.vjp(_sgu_chunk, u_ref[rows, :], v_ref[rows, :], z_ref[rows, :], g_ref[...], b_ref[...], ws, bcols)
            du, dv, dz, dg, db, dws, dbc = vjp(dy_ref[rows, :])
            dp_ref[rows, pl.ds(0, D_SGU)] = du.astype(BF16)
            dp_ref[rows, pl.ds(D_SGU, D_SGU)] = dv.astype(BF16)
            dp_ref[rows, pl.ds(2 * D_SGU, D_SGU)] = dz.astype(BF16)
            dg_ref[...] += dg
            db_ref[...] += db
            for g in range(N_GROUPS):
                dw_ref[g] += dws[g]
                dbt_ref[:, g:g + 1] += dbc[g]

    return pl.pallas_call(
        body, name="sgu_bwd", grid=(s // tile,),
        in_specs=[col(0), col(1), col(2), vec, vec, wspec, bspec, pl.BlockSpec((tile, D_SGU), lambda i: (i, 0)),
                  pl.BlockSpec(memory_space=pl.ANY)],
        out_specs=[pl.BlockSpec((tile, 3 * D_SGU), lambda i: (i, 0)), vec, vec, wspec, bspec],
        out_shape=[jax.ShapeDtypeStruct(dp.shape, BF16), jax.ShapeDtypeStruct((1, D_SGU), F32),
                   jax.ShapeDtypeStruct((1, D_SGU), F32), jax.ShapeDtypeStruct((N_GROUPS, SGU_CHUNK, SGU_CHUNK), F32),
                   jax.ShapeDtypeStruct((SGU_CHUNK, N_GROUPS), F32)],
        input_output_aliases={8: 0},
        compiler_params=_params(("arbitrary",)),
    )(proj, proj, proj, ln_g, ln_b, w_s, b_t, dy, dp)


QKV_BLOCK = 1


def _qkv_head(cq, ck, cv):
    q, k, v = _silu(cq), _silu(ck), _silu(cv)
    q = q * lax.rsqrt(jnp.sum(q * q, axis=-1, keepdims=True) + EPS) * np.float32(HEAD ** -0.5)
    k = k * lax.rsqrt(jnp.sum(k * k, axis=-1, keepdims=True) + EPS)
    return q, k, v


def _gate_fn(p, a_log, dt_bias):
    lane = lax.broadcasted_iota(jnp.int32, p.shape, 1)
    g = -jnp.exp(a_log) * jax.nn.softplus(p + dt_bias)
    return jnp.where(lane < 16, g, jnp.where(lane < N_GATE, jax.nn.sigmoid(p), 0.0))


def _shifted_rows(full, shift, start, rows):
    n = full.shape[0]
    assert start % HALO == 0 and 0 <= start + shift and start + shift + rows <= n
    rolled = full if shift == 0 else pltpu.roll(full, (-shift) % n, 0)
    return rolled[start:start + rows, :]


def _halo_maps(tile, s):
    per = tile // HALO
    prev = lambda i: jnp.maximum(i * per - 1, 0)
    nxt = lambda i: jnp.minimum((i + 1) * per, s // HALO - 1)
    return prev, nxt


def _prep_fwd(proj, conv_w, pg, a_log, dt_bias, *, tile=256):
    s = proj.shape[0]
    width = 3 * D_DN
    tile = min(tile, s)
    n_blk = s // tile
    prev, nxt = _halo_maps(tile, s)

    def body(x_ref, xp_ref, xn_ref, w_ref, pg_ref, al_ref, dt_ref, q_ref, k_ref, v_ref, gate_ref, ext):
        i = pl.program_id(0)
        ext[pl.ds(0, HALO), :] = jnp.where(i > 0, xp_ref[...], 0.0)
        ext[pl.ds(HALO, tile), :] = x_ref[...]
        ext[pl.ds(HALO + tile, HALO), :] = jnp.where(i < n_blk - 1, xn_ref[...], 0.0)
        full, c = ext[...], None
        for j in range(CONV_W):
            term = _shifted_rows(full, j - CONV_W // 2, HALO, tile) * w_ref[j:j + 1, :]
            c = term if c is None else c + term
        for h in range(N_GROUPS):
            q, k, v = _qkv_head(c[:, h * HEAD:(h + 1) * HEAD], c[:, D_DN + h * HEAD:D_DN + (h + 1) * HEAD],
                                c[:, 2 * D_DN + h * HEAD:2 * D_DN + (h + 1) * HEAD])
            q_ref[h] = q
            k_ref[h] = k
            v_ref[h] = v
        gate_ref[...] = _gate_fn(pg_ref[...], al_ref[...], dt_ref[...])

    hm = pl.BlockSpec((N_GROUPS, tile, HEAD), lambda i: (0, i, 0))
    lane_vec = pl.BlockSpec((1, LANE), lambda i: (0, 0))
    return pl.pallas_call(
        body, name="prep_fwd", grid=(n_blk,),
        in_specs=[pl.BlockSpec((tile, width), lambda i: (i, QKV_BLOCK)),
                  pl.BlockSpec((HALO, width), lambda i: (prev(i), QKV_BLOCK)),
                  pl.BlockSpec((HALO, width), lambda i: (nxt(i), QKV_BLOCK)),
                  pl.BlockSpec((HALO, width), lambda i: (0, 0)),
                  pl.BlockSpec((tile, LANE), lambda i: (i, 0)), lane_vec, lane_vec],
        out_specs=[hm, hm, hm, pl.BlockSpec((tile, LANE), lambda i: (i, 0))],
        out_shape=[jax.ShapeDtypeStruct((N_GROUPS, s, HEAD), F32)] * 3 + [jax.ShapeDtypeStruct((s, LANE), F32)],
        scratch_shapes=[pltpu.VMEM((tile + 2 * HALO, width), F32)],
        compiler_params=_params(("arbitrary",)),
    )(proj, proj, proj, conv_w, pg, a_log, dt_bias)


def _prep_bwd(proj, conv_w, pg, a_log, dt_bias, dqkv_f, dqkv_b, dgates, dp, *, tile=128):
    s = proj.shape[0]
    width = 3 * D_DN
    tile = min(tile, s)
    n_blk = s // tile
    prev, nxt = _halo_maps(tile, s)
    ext_rows = tile + 2 * HALO
    pad = CONV_W // 2

    def body(*refs):
        x_ref, xp_ref, xn_ref, w_ref, pg_ref, al_ref, dt_ref = refs[:7]
        ct_refs = refs[7:25]
        dg0_ref, dg1_ref, dg2_ref = refs[25:28]
        dp_ref, dpg_ref, dw_ref, dal_ref, ddt_ref, xext, dcext = refs[29:]
        i = pl.program_id(0)
        has_prev, has_next = i > 0, i < n_blk - 1

        @pl.when(i == 0)
        def _():
            dw_ref[...] = jnp.zeros_like(dw_ref)
            dal_ref[...] = jnp.zeros_like(dal_ref)
            ddt_ref[...] = jnp.zeros_like(ddt_ref)

        zeros = jnp.zeros((HALO, width), F32)
        xext[pl.ds(0, HALO), :] = zeros
        xext[pl.ds(HALO, HALO), :] = jnp.where(has_prev, xp_ref[...], 0.0)
        xext[pl.ds(2 * HALO, tile), :] = x_ref[...]
        xext[pl.ds(2 * HALO + tile, HALO), :] = jnp.where(has_next, xn_ref[...], 0.0)
        xext[pl.ds(3 * HALO + tile, HALO), :] = zeros
        xfull, c = xext[...], None
        for j in range(CONV_W):
            term = _shifted_rows(xfull, j - pad, HALO, ext_rows) * w_ref[j:j + 1, :]
            c = term if c is None else c + term
        for h in range(N_GROUPS):
            lanes = [pl.ds(part * D_DN + h * HEAD, HEAD) for part in range(3)]
            cts = []
            for part in range(3):
                (cur, before, after), (cur2, before2, after2) = ct_refs[3 * part:3 * part + 3], ct_refs[9 + 3 * part:12 + 3 * part]
                cts.append(jnp.concatenate([jnp.where(has_prev, before[h] + before2[h], 0.0), cur[h] + cur2[h],
                                            jnp.where(has_next, after[h] + after2[h], 0.0)], axis=0))
            _, vjp = jax.vjp(_qkv_head, c[:, h * HEAD:(h + 1) * HEAD], c[:, D_DN + h * HEAD:D_DN + (h + 1) * HEAD],
                             c[:, 2 * D_DN + h * HEAD:2 * D_DN + (h + 1) * HEAD])
            for lane_sl, dc in zip(lanes, vjp(tuple(cts))):
                dcext[:, lane_sl] = dc
        dcfull = dcext[...]
        dx, dc_blk = None, dcfull[HALO:HALO + tile, :]
        for j in range(CONV_W):
            term = _shifted_rows(dcfull, pad - j, HALO, tile) * w_ref[j:j + 1, :]
            dx = term if dx is None else dx + term
            dw_ref[j:j + 1, :] += jnp.sum(dc_blk * _shifted_rows(xfull, j - pad, 2 * HALO, tile), axis=0, keepdims=True)
        dp_ref[...] = dx.astype(BF16)
        _, gvjp = jax.vjp(_gate_fn, pg_ref[...], al_ref[...], dt_ref[...])
        dpg, dal, ddt = gvjp(dg0_ref[...] + dg1_ref[...] + dg2_ref[...])
        dpg_ref[...] = dpg.astype(BF16)
        dal_ref[...] += dal
        ddt_ref[...] += ddt

    xs = [pl.BlockSpec((tile, width), lambda i: (i, QKV_BLOCK)), pl.BlockSpec((HALO, width), lambda i: (prev(i), QKV_BLOCK)),
          pl.BlockSpec((HALO, width), lambda i: (nxt(i), QKV_BLOCK))]
    hm = [pl.BlockSpec((N_GROUPS, tile, HEAD), lambda i: (0, i, 0)),
          pl.BlockSpec((N_GROUPS, HALO, HEAD), lambda i: (0, prev(i), 0)),
          pl.BlockSpec((N_GROUPS, HALO, HEAD), lambda i: (0, nxt(i), 0))]
    gate = pl.BlockSpec((tile, LANE), lambda i: (i, 0))
    lane_vec = pl.BlockSpec((1, LANE), lambda i: (0, 0))
    wspec = pl.BlockSpec((HALO, width), lambda i: (0, 0))
    n_in = 3 + 4 + 18 + 3
    ct_arrays = [a for dqkv in (dqkv_f, dqkv_b) for a in dqkv for _ in range(3)]
    return pl.pallas_call(
        body, name="prep_bwd", grid=(n_blk,),
        in_specs=xs + [wspec, gate, lane_vec, lane_vec] + hm * 6 + [gate] * 3 + [pl.BlockSpec(memory_space=pl.ANY)],
        out_specs=[pl.BlockSpec((tile, width), lambda i: (i, QKV_BLOCK)), gate, wspec, lane_vec, lane_vec],
        out_shape=[jax.ShapeDtypeStruct(dp.shape, BF16), jax.ShapeDtypeStruct((s, LANE), BF16),
                   jax.ShapeDtypeStruct((HALO, width), F32), jax.ShapeDtypeStruct((1, LANE), F32),
                   jax.ShapeDtypeStruct((1, LANE), F32)],
        input_output_aliases={n_in: 0},
        scratch_shapes=[pltpu.VMEM((tile + 4 * HALO, width), F32), pltpu.VMEM((ext_rows, width), F32)],
        compiler_params=_params(("arbitrary",)),
    )(proj, proj, proj, conv_w, pg, a_log, dt_bias, *ct_arrays, *dgates, dp)


def _each(f, *cols):
    return tuple(f(*a) for a in zip(*cols))


def _delta_chunk(q, k, v, g_col, g_row, beta, state, t_known, reverse):
    n = DN_CHUNK
    ri = lax.broadcasted_iota(jnp.int32, (n, n), 0)
    ci = lax.broadcasted_iota(jnp.int32, (n, n), 1)
    masks = {False: (ri >= ci, ri <= ci, ri > ci), True: (ri <= ci, ri >= ci, ri < ci)}
    flags = tuple(reverse) if isinstance(reverse, (tuple, list)) else (reverse,) * len(q)
    incl, incl_t, strict = (tuple(masks[f][i] for f in flags) for i in range(3))
    gc_col = _each(lambda gr, mk: jnp.sum(jnp.where(mk, gr, 0.0), axis=1, keepdims=True), g_row, incl)
    gc_row = _each(lambda gc, mk: jnp.sum(jnp.where(mk, gc, 0.0), axis=0, keepdims=True), g_col, incl_t)
    g_tot = _each(lambda gr: jnp.sum(gr, axis=1, keepdims=True), g_row)
    decay = _each(lambda a, b, mk: jnp.where(mk, jnp.exp(jnp.where(mk, a - b, 0.0)), 0.0), gc_col, gc_row, incl)
    kb = _each(lambda a, b: a * b, k, beta)
    kk = _each(_mm_nt, kb, k)
    m = _each(lambda a, d, mk: jnp.where(mk, a * d, 0.0), kk, decay, strict)
    if t_known is None:
        eye = (ri == ci).astype(F32)
        p = _each(lambda a: -a, m)
        t = _each(lambda a: eye + a, p)
        for _ in range(5):
            p = _each(lambda a: _bdot(a, a, ((1,), (0,))), p)
            t = _each(lambda a, b: a + _bdot(a, b, ((1,), (0,))), t, p)
    else:
        t = _each(_inverse_given, m, t_known)
    e_gc = _each(jnp.exp, gc_col)
    u = _each(lambda t_, v_, b_: _mm_nn(t_, v_ * b_), t, v, beta)
    w = _each(lambda t_, kb_, e_: _mm_nn(t_, kb_ * e_), t, kb, e_gc)
    attn = _each(lambda q_, k_, d_: _mm_nt(q_, k_) * d_, q, k, decay)
    ws = _each(_mm_nn, w, state)
    v_new = _each(lambda a, b: a - b, u, ws)
    qs = _each(lambda q_, e_, s_: _mm_nn(q_ * e_, s_), q, e_gc, state)
    av = _each(_mm_nn, attn, v_new)
    o = _each(lambda a, b: a + b, qs, av)
    kv = _each(lambda k_, gt, gc, vn: _mm_tn(k_ * jnp.exp(gt - gc), vn), k, g_tot, gc_col, v_new)
    new_state = _each(lambda s_, gt, kv_: s_ * jnp.exp(gt) + kv_, state, g_tot, kv)
    return o, new_state, t


def _delta_lanes(reverse):
    return (N_GROUPS, 3 * N_GROUPS) if reverse else (0, 2 * N_GROUPS)


def _delta_fwd(q, k, v, gates, rows_f, rows_b, *, comm=None):
    s = q.shape[1]
    nc = s // DN_CHUNK
    heads = range(N_GROUPS)

    def body(qf_ref, kf_ref, vf_ref, gf_ref, rf_ref, qb_ref, kb_ref, vb_ref, gb_ref, rb_ref,
             of_ref, sf_ref, tf_ref, ob_ref, sb_ref, tb_ref, state):
        @pl.when(pl.program_id(0) == 0)
        def _():
            state[...] = jnp.zeros_like(state)

        qs, ks, vs, g_cols, g_rows, betas, flags = [], [], [], [], [], [], []
        for reverse, q_ref, k_ref, v_ref, gate, row_ref in ((False, qf_ref, kf_ref, vf_ref, gf_ref[...], rf_ref),
                                                            (True, qb_ref, kb_ref, vb_ref, gb_ref[...], rb_ref)):
            g_lane, b_lane = _delta_lanes(reverse)
            for j in heads:
                qs.append(q_ref[j])
                ks.append(k_ref[j])
                vs.append(v_ref[j])
                g_cols.append(gate[:, g_lane + j:g_lane + j + 1])
                g_rows.append(row_ref[0, j:j + 1, :])
                betas.append(gate[:, b_lane + j:b_lane + j + 1])
                flags.append(reverse)
        prev = tuple(state[n] for n in range(2 * N_GROUPS))
        o, new, t = _delta_chunk(tuple(qs), tuple(ks), tuple(vs), tuple(g_cols), tuple(g_rows), tuple(betas), prev, None, tuple(flags))
        for n in range(2 * N_GROUPS):
            o_ref, st_ref, t_ref = (of_ref, sf_ref, tf_ref) if n < N_GROUPS else (ob_ref, sb_ref, tb_ref)
            j = n % N_GROUPS
            st_ref[0, j] = prev[n]
            o_ref[j] = o[n]
            t_ref[0, j] = t[n]
            state[n] = new[n]

    def specs(chunk_of):
        hm = pl.BlockSpec((N_GROUPS, DN_CHUNK, HEAD), lambda c: (0, chunk_of(c), 0))
        ins = [hm, hm, hm, pl.BlockSpec((DN_CHUNK, LANE), lambda c: (chunk_of(c), 0)),
               pl.BlockSpec((1, N_GROUPS, DN_CHUNK), lambda c: (chunk_of(c), 0, 0))]
        outs = [hm, pl.BlockSpec((1, N_GROUPS, HEAD, HEAD), lambda c: (chunk_of(c), 0, 0, 0)),
                pl.BlockSpec((1, N_GROUPS, DN_CHUNK, DN_CHUNK), lambda c: (chunk_of(c), 0, 0, 0))]
        return ins, outs

    (ins_f, outs_f), (ins_b, outs_b) = specs(lambda c: c), specs(lambda c: nc - 1 - c)
    shapes = [jax.ShapeDtypeStruct((N_GROUPS, s, HEAD), F32), jax.ShapeDtypeStruct((nc, N_GROUPS, HEAD, HEAD), F32),
              jax.ShapeDtypeStruct((nc, N_GROUPS, DN_CHUNK, DN_CHUNK), F32)]
    results, sent = _call_with_comm(
        body, comm, name="delta_fwd" + ("" if comm is None else "_comm"), grid=(nc,),
        in_specs=ins_f + ins_b, out_specs=outs_f + outs_b, out_shape=shapes + shapes,
        scratch_shapes=[pltpu.VMEM((2 * N_GROUPS, HEAD, HEAD), F32)],
        args=(q, k, v, gates, rows_f, q, k, v, gates, rows_b), compiler_params=_params(("arbitrary",)))
    return results[:3], results[3:], sent


def _delta_bwd(q, k, v, gates, rows_f, rows_b, states_f, states_b, t_f, t_b, do, *, comm=None):
    s = q.shape[1]
    nc = s // DN_CHUNK
    heads = range(N_GROUPS)
    n_in, n_out = 8, 5

    def body(*refs):
        ins = (refs[:n_in], refs[n_in:2 * n_in])
        outs = (refs[2 * n_in:2 * n_in + n_out], refs[2 * n_in + n_out:2 * n_in + 2 * n_out])
        dstate = refs[-1]

        @pl.when(pl.program_id(0) == 0)
        def _():
            dstate[...] = jnp.zeros_like(dstate)

        qs, ks, vs, g_cols, g_rows, betas, sts, ts, dos, flags = ([] for _ in range(10))
        for reverse, (q_ref, k_ref, v_ref, gate_ref, row_ref, st_ref, t_ref, do_ref) in zip((False, True), ins):
            gate = gate_ref[...]
            g_lane, b_lane = _delta_lanes(reverse)
            for j in heads:
                qs.append(q_ref[j])
                ks.append(k_ref[j])
                vs.append(v_ref[j])
                g_cols.append(gate[:, g_lane + j:g_lane + j + 1])
                g_rows.append(row_ref[0, j:j + 1, :])
                betas.append(gate[:, b_lane + j:b_lane + j + 1])
                sts.append(st_ref[0, j])
                ts.append(t_ref[0, j])
                dos.append(do_ref[j])
                flags.append(reverse)
        t_known, flags = tuple(ts), tuple(flags)
        f = lambda q_, k_, v_, gc_, gr_, b_, s_: _delta_chunk(q_, k_, v_, gc_, gr_, b_, s_, t_known, flags)[:2]
        _, vjp = jax.vjp(f, tuple(qs), tuple(ks), tuple(vs), tuple(g_cols), tuple(g_rows), tuple(betas), tuple(sts))
        dq, dk, dv, dg_col, dg_row, dbeta, dprev = vjp((tuple(dos), tuple(dstate[n] for n in range(2 * N_GROUPS))))
        ids = lax.broadcasted_iota(jnp.int32, (DN_CHUNK, LANE), 1)
        for reverse, (dq_ref, dk_ref, dv_ref, dgate_ref, drow_ref) in zip((False, True), outs):
            g_lane, b_lane = _delta_lanes(reverse)
            dgate = jnp.zeros((DN_CHUNK, LANE), F32)
            for j in heads:
                n = j + (N_GROUPS if reverse else 0)
                dq_ref[j] = dq[n]
                dk_ref[j] = dk[n]
                dv_ref[j] = dv[n]
                dstate[n] = dprev[n]
                dgate = dgate + jnp.where(ids == g_lane + j, dg_col[n], 0.0) + jnp.where(ids == b_lane + j, dbeta[n], 0.0)
                drow_ref[0, j:j + 1, :] = dg_row[n]
            dgate_ref[...] = dgate

    def specs(chunk_of):
        hm = pl.BlockSpec((N_GROUPS, DN_CHUNK, HEAD), lambda c: (0, chunk_of(c), 0))
        gate = pl.BlockSpec((DN_CHUNK, LANE), lambda c: (chunk_of(c), 0))
        rows = pl.BlockSpec((1, N_GROUPS, DN_CHUNK), lambda c: (chunk_of(c), 0, 0))
        st = pl.BlockSpec((1, N_GROUPS, HEAD, HEAD), lambda c: (chunk_of(c), 0, 0, 0))
        ts = pl.BlockSpec((1, N_GROUPS, DN_CHUNK, DN_CHUNK), lambda c: (chunk_of(c), 0, 0, 0))
        return [hm, hm, hm, gate, rows, st, ts, hm], [hm, hm, hm, gate, rows]

    (ins_f, outs_f), (ins_b, outs_b) = specs(lambda c: nc - 1 - c), specs(lambda c: c)
    shapes = [jax.ShapeDtypeStruct((N_GROUPS, s, HEAD), F32)] * 3 + [jax.ShapeDtypeStruct((s, LANE), F32),
                                                                       jax.ShapeDtypeStruct((nc, N_GROUPS, DN_CHUNK), F32)]
    results, sent = _call_with_comm(
        body, comm, name="delta_bwd" + ("" if comm is None else "_comm"), grid=(nc,),
        in_specs=ins_f + ins_b, out_specs=outs_f + outs_b, out_shape=shapes + shapes,
        scratch_shapes=[pltpu.VMEM((2 * N_GROUPS, HEAD, HEAD), F32)],
        args=(q, k, v, gates, rows_f, states_f, t_f, do, q, k, v, gates, rows_b, states_b, t_b, do),
        compiler_params=_params(("arbitrary",)))
    return results[:n_out], results[n_out:], sent


ZB_BLOCK = 6


def _post_head(o_f, o_b, z, w):
    return _rms(o_f + o_b, w) * _silu(z)


def _post_fwd(o_f, o_b, proj, w, y, y_t, *, tile=256):
    s = proj.shape[0]
    tile = min(tile, s)

    def body(of_ref, ob_ref, z_ref, w_ref, _, __, y_ref, yt_ref):
        for h in range(N_GROUPS):
            lanes = pl.ds(h * HEAD, HEAD)
            y_h = _post_head(of_ref[h], ob_ref[h], z_ref[:, lanes], w_ref[...])
            y_ref[:, lanes] = y_h.astype(BF16)
            yt_ref[lanes, :] = y_h.T.astype(BF16)

    hm = pl.BlockSpec((N_GROUPS, tile, HEAD), lambda i: (0, i, 0))
    any_spec = pl.BlockSpec(memory_space=pl.ANY)
    return pl.pallas_call(
        body, name="post_fwd", grid=(s // tile,),
        in_specs=[hm, hm, pl.BlockSpec((tile, D_DN), lambda i: (i, ZB_BLOCK)), pl.BlockSpec((1, HEAD), lambda i: (0, 0)),
                  any_spec, any_spec],
        out_specs=[pl.BlockSpec((tile, D_DN), lambda i: (i, 1)), pl.BlockSpec((D_DN, tile), lambda i: (1, i))],
        out_shape=[jax.ShapeDtypeStruct(y.shape, BF16), jax.ShapeDtypeStruct(y_t.shape, BF16)],
        input_output_aliases={4: 0, 5: 1},
        compiler_params=_params(("arbitrary",)),
    )(o_f, o_b, proj, w, y, y_t)


def _post_bwd(o_f, o_b, proj, w, dy, *, tile=256):
    s = proj.shape[0]
    tile = min(tile, s)

    def body(of_ref, ob_ref, z_ref, w_ref, dy_ref, do_ref, dp_ref, dw_ref):
        @pl.when(pl.program_id(0) == 0)
        def _():
            dw_ref[...] = jnp.zeros_like(dw_ref)

        for h in range(N_GROUPS):
            lanes = pl.ds(h * HEAD, HEAD)
            _, vjp = jax.vjp(_post_head, of_ref[h], ob_ref[h], z_ref[:, lanes], w_ref[...])
            do, _, dz, dw = vjp(dy_ref[:, lanes])
            do_ref[h] = do
            dp_ref[:, lanes] = dz.astype(BF16)
            dw_ref[...] += dw

    hm = pl.BlockSpec((N_GROUPS, tile, HEAD), lambda i: (0, i, 0))
    zb = pl.BlockSpec((tile, D_DN), lambda i: (i, ZB_BLOCK))
    vec = pl.BlockSpec((1, HEAD), lambda i: (0, 0))
    return pl.pallas_call(
        body, name="post_bwd", grid=(s // tile,),
        in_specs=[hm, hm, zb, vec, pl.BlockSpec((tile, D_DN), lambda i: (i, 1))], out_specs=[hm, zb, vec],
        out_shape=[jax.ShapeDtypeStruct((N_GROUPS, s, HEAD), F32), jax.ShapeDtypeStruct((s, D_MAIN), BF16),
                   jax.ShapeDtypeStruct((1, HEAD), F32)],
        compiler_params=_params(("arbitrary",)),
    )(o_f, o_b, proj, w, dy)


def _my_index():
    return 4 * lax.axis_index("x") + 2 * lax.axis_index("y") + lax.axis_index("c")


def _peer(k):
    x, y, c = lax.axis_index("x"), lax.axis_index("y"), lax.axis_index("c")
    px, py, pc = x ^ ((k >> 2) & 1), y ^ ((k >> 1) & 1), c ^ (k & 1)
    return (px, py, pc), 4 * px + 2 * py + pc


class _Comm:
    def __init__(self, inputs, outputs, copies):
        self.inputs, self.outputs, self.copies = list(inputs), list(outputs), list(copies)
        self.kept = [o for o in self.outputs if not isinstance(o, jax.ShapeDtypeStruct)]

    def out_shapes(self):
        return [jax.ShapeDtypeStruct(o.shape, o.dtype) for o in self.outputs]

    def aliases(self, first_kept_input, first_output):
        kept_at = [i for i, o in enumerate(self.outputs) if not isinstance(o, jax.ShapeDtypeStruct)]
        return {first_kept_input + n: first_output + i for n, i in enumerate(kept_at)}

    def scratch(self):
        n = len(self.copies)
        return [pltpu.SemaphoreType.DMA((n, N_DEV - 1)), pltpu.SemaphoreType.DMA((n, N_DEV - 1)), pltpu.SemaphoreType.DMA((n,))]

    def _descriptors(self, in_refs, out_refs, sems, arrivals):
        send_sems, recv_sems, local_sems = sems
        me = _my_index()
        local, remote = [], []
        for k in range(N_DEV):
            peer, peer_idx = _peer(k)
            for a, (i_in, i_out, src_of, dst_of) in enumerate(self.copies):
                if k == 0:
                    if not arrivals:
                        local.append(pltpu.make_async_copy(src_of(in_refs[i_in], me), dst_of(out_refs[i_out], me), local_sems.at[a]))
                    continue
                receiver, sender = (me, peer_idx) if arrivals else (peer_idx, me)
                remote.append(pltpu.make_async_remote_copy(
                    src_ref=src_of(in_refs[i_in], receiver), dst_ref=dst_of(out_refs[i_out], sender),
                    send_sem=send_sems.at[a, k - 1], recv_sem=recv_sems.at[a, k - 1], device_id=peer,
                    device_id_type=pl.DeviceIdType.MESH))
        return local, remote

    def run_two_level(self, in_refs, out_refs, sems):
        send_sems, recv_sems, local_sems = sems
        me = _my_index()
        sibling, _ = _peer(1)

        def remote(a, k, src, block_owner, to):
            _, i_out, _, dst_of = self.copies[a]
            return pltpu.make_async_remote_copy(
                src_ref=src, dst_ref=dst_of(out_refs[i_out], block_owner), send_sem=send_sems.at[a, k - 1],
                recv_sem=recv_sems.at[a, k - 1], device_id=to, device_id_type=pl.DeviceIdType.MESH)

        sources = [src_of(in_refs[i_in], me) for i_in, _, src_of, _ in self.copies]
        n = range(len(self.copies))
        local = [pltpu.make_async_copy(sources[a], self.copies[a][3](out_refs[self.copies[a][1]], me), local_sems.at[a]) for a in n]
        first = [remote(a, k, sources[a], me, _peer(k)[0]) for k in (1, 2, 4, 6) for a in n]
        for cp in local + first:
            cp.start()
        passed = []
        for k in (2, 4, 6):
            peer, peer_idx = _peer(k)
            for a in n:
                remote(a, k, sources[a], peer_idx, peer).wait_recv()
                landed = self.copies[a][3](out_refs[self.copies[a][1]], peer_idx)
                passed.append(remote(a, k + 1, landed, peer_idx, sibling))
                passed[-1].start()
        for k in (1, 3, 5, 7):
            peer, peer_idx = _peer(k)
            for a in n:
                remote(a, k, sources[a], peer_idx, peer).wait_recv()
        for cp in first + passed:
            cp.wait_send()
        for cp in local:
            cp.wait()

    def start(self, in_refs, out_refs, sems):
        local, sends = self._descriptors(in_refs, out_refs, sems, arrivals=False)
        for cp in local + sends:
            cp.start()

    def wait(self, in_refs, out_refs, sems):
        _, arriving = self._descriptors(in_refs, out_refs, sems, arrivals=True)
        for cp in arriving:
            cp.wait_recv()
        local, sends = self._descriptors(in_refs, out_refs, sems, arrivals=False)
        for cp in sends:
            cp.wait_send()
        for cp in local:
            cp.wait()


def _call_with_comm(body, comm, *, name, grid, in_specs, out_specs, out_shape, scratch_shapes, args, compiler_params):
    if comm is None:
        return pl.pallas_call(body, name=name, grid=grid, in_specs=in_specs, out_specs=out_specs, out_shape=out_shape,
                              scratch_shapes=scratch_shapes, compiler_params=compiler_params)(*args), []
    n_in, n_out, n_scratch = len(in_specs), len(out_specs), len(scratch_shapes)
    c_in, c_kept, c_out = len(comm.inputs), len(comm.kept), len(comm.outputs)

    def at_step(position):
        here = [pl.program_id(d) == p for d, p in enumerate(position)]
        return here[0] if len(here) == 1 else functools.reduce(jnp.logical_and, here)

    def wrapped(*refs):
        main_in, comm_in = refs[:n_in], refs[n_in:n_in + c_in]
        o = n_in + c_in + c_kept
        main_out, comm_out = refs[o:o + n_out], refs[o + n_out:o + n_out + c_out]
        s = o + n_out + c_out
        main_scratch, sems = refs[s:s + n_scratch], refs[s + n_scratch:]

        @pl.when(at_step([0] * len(grid)))
        def _():
            comm.start(comm_in, comm_out, sems)

        body(*main_in, *main_out, *main_scratch)

        @pl.when(at_step([g - 1 for g in grid]))
        def _():
            comm.wait(comm_in, comm_out, sems)

    any_spec = pl.BlockSpec(memory_space=pl.ANY)
    results = pl.pallas_call(
        wrapped, name=name, grid=grid, in_specs=list(in_specs) + [any_spec] * (c_in + c_kept),
        out_specs=list(out_specs) + [any_spec] * c_out, out_shape=list(out_shape) + comm.out_shapes(),
        scratch_shapes=list(scratch_shapes) + comm.scratch(), input_output_aliases=comm.aliases(n_in + c_in, n_out),
        compiler_params=compiler_params,
    )(*args, *comm.inputs, *comm.kept)
    return results[:n_out], results[n_out:]


def _exchange(name, comm, two_level=False):
    c_in, c_kept, c_out = len(comm.inputs), len(comm.kept), len(comm.outputs)

    def body(*refs):
        in_refs, out_refs, sems = refs[:c_in], refs[c_in + c_kept:c_in + c_kept + c_out], refs[c_in + c_kept + c_out:]
        if two_level:
            comm.run_two_level(in_refs, out_refs, sems)
        else:
            comm.start(in_refs, out_refs, sems)
            comm.wait(in_refs, out_refs, sems)

    any_spec = pl.BlockSpec(memory_space=pl.ANY)
    return pl.pallas_call(
        body, name=name, in_specs=[any_spec] * (c_in + c_kept), out_specs=[any_spec] * c_out, out_shape=comm.out_shapes(),
        scratch_shapes=comm.scratch(), input_output_aliases=comm.aliases(c_in, 0),
    )(*comm.inputs, *comm.kept)


def _whole(ref, _):
    return ref


def _slot(ref, dev):
    return ref.at[dev]


def _row_block(size, axis):
    def pick(ref, dev):
        start = pl.multiple_of(dev * size, size) if size % SUBLANES == 0 else dev * size
        return ref.at[tuple([slice(None)] * axis + [pl.ds(start, size)])]
    return pick


def _blocked(shape, block):
    nd = len(block)
    grid = tuple(shape[d] // block[d] for d in range(nd) if block[d] != shape[d])
    return grid, (lambda *i: tuple(i) + (0,) * (nd - len(grid)))


def _sum_parts(name, parts, block, comm=None):
    n_parts = parts.shape[0]
    grid, idx = _blocked(parts.shape[1:], block)

    def body(p_ref, g_ref):
        g = p_ref[0].astype(F32)
        for j in range(1, n_parts):
            g = g + p_ref[j].astype(F32)
        g_ref[...] = g

    (total,), sent = _call_with_comm(
        body, comm, name=name, grid=grid, in_specs=[pl.BlockSpec((n_parts,) + tuple(block), lambda *i: (0,) + idx(*i))],
        out_specs=[pl.BlockSpec(block, idx)], out_shape=[jax.ShapeDtypeStruct(parts.shape[1:], F32)], scratch_shapes=[],
        args=(parts,), compiler_params=_params(("arbitrary",) * len(grid)))
    return total, sent


def _adamw(name, parts, w, m, v, block):
    n_parts = parts.shape[0]
    nd = len(block)
    grid, _ = _blocked(w.shape, block)
    lead = len(grid)

    def body(p_ref, w_ref, m_ref, v_ref, g_ref, d_ref, mo_ref, vo_ref):
        g = p_ref[0].astype(F32)
        for j in range(1, n_parts):
            g = g + p_ref[j].astype(F32)
        m_new = ADAM_B1 * m_ref[...] + (1.0 - ADAM_B1) * g
        v_new = ADAM_B2 * v_ref[...] + (1.0 - ADAM_B2) * jnp.square(g)
        m_hat = m_new / np.float32(1.0 - ADAM_B1 ** ADAM_STEP)
        v_hat = v_new / np.float32(1.0 - ADAM_B2 ** ADAM_STEP)
        g_ref[...] = g
        d_ref[...] = -ADAM_LR * (m_hat / (jnp.sqrt(v_hat) + ADAM_EPS) + ADAM_WD * w_ref[...])
        mo_ref[...] = m_new
        vo_ref[...] = v_new

    idx = lambda *i: tuple(i) + (0,) * (nd - lead)
    spec = pl.BlockSpec(block, idx)
    pspec = pl.BlockSpec((n_parts,) + tuple(block), lambda *i: (0,) + idx(*i))
    n_blk = int(np.prod(block[:-1])) * (-(-block[-1] // LANE) * LANE)
    return pl.pallas_call(
        body, name=name, grid=grid, in_specs=[pspec, spec, spec, spec], out_specs=[spec] * 4,
        out_shape=[jax.ShapeDtypeStruct(w.shape, F32)] * 4,
        compiler_params=_params(("arbitrary",) * lead, 2 * (n_parts + 7) * n_blk * 4 + VMEM_MARGIN),
    )(parts, w, m, v)


def _to_flat(pieces, row_multiple):
    flat = jnp.concatenate([p.reshape(-1) for p in pieces])
    rows = -(-flat.shape[0] // FLAT_W)
    rows = -(-rows // row_multiple) * row_multiple
    return jnp.pad(flat, (0, rows * FLAT_W - flat.shape[0])).reshape(rows, FLAT_W)


def _lane_row(*pieces):
    flat = jnp.concatenate([p.reshape(-1) for p in pieces])
    return jnp.pad(flat, (0, LANE - flat.shape[0])).reshape(1, LANE)


def _rows_view(a):
    return a.reshape(a.shape[:-1] + (SUBLANES, LANE))


def _in_shard_t(a):
    return _rows_view(jnp.swapaxes(a, 1, 2))


def kernel(x, norm_w, w_in, sgu_ln_g, sgu_ln_b, sgu_w, sgu_b, conv_w, a_log_f, a_log_b, dt_bias_f, dt_bias_b, dn_norm_w, w_out, final_norm_w, loss_target, m_norm_w, m_w_in, m_sgu_ln_g, m_sgu_ln_b, m_sgu_w, m_sgu_b, m_conv_w, m_a_log_f, m_a_log_b, m_dt_bias_f, m_dt_bias_b, m_dn_norm_w, m_w_out, m_final_norm_w, v_norm_w, v_w_in, v_sgu_ln_g, v_sgu_ln_b, v_sgu_w, v_sgu_b, v_conv_w, v_a_log_f, v_a_log_b, v_dt_bias_f, v_dt_bias_b, v_dn_norm_w, v_w_out, v_final_norm_w):
    weights = dict(norm_w=norm_w, w_in=w_in, sgu_ln_g=sgu_ln_g, sgu_ln_b=sgu_ln_b, sgu_w=sgu_w, sgu_b=sgu_b, conv_w=conv_w,
                   a_log_f=a_log_f, a_log_b=a_log_b, dt_bias_f=dt_bias_f, dt_bias_b=dt_bias_b, dn_norm_w=dn_norm_w,
                   w_out=w_out, final_norm_w=final_norm_w)
    m_in = dict(norm_w=m_norm_w, w_in=m_w_in, sgu_ln_g=m_sgu_ln_g, sgu_ln_b=m_sgu_ln_b, sgu_w=m_sgu_w, sgu_b=m_sgu_b,
                conv_w=m_conv_w, a_log_f=m_a_log_f, a_log_b=m_a_log_b, dt_bias_f=m_dt_bias_f, dt_bias_b=m_dt_bias_b,
                dn_norm_w=m_dn_norm_w, w_out=m_w_out, final_norm_w=m_final_norm_w)
    v_in = dict(norm_w=v_norm_w, w_in=v_w_in, sgu_ln_g=v_sgu_ln_g, sgu_ln_b=v_sgu_ln_b, sgu_w=v_sgu_w, sgu_b=v_sgu_b,
                conv_w=v_conv_w, a_log_f=v_a_log_f, a_log_b=v_a_log_b, dt_bias_f=v_dt_bias_f, dt_bias_b=v_dt_bias_b,
                dn_norm_w=v_dn_norm_w, w_out=v_w_out, final_norm_w=v_final_norm_w)
    s = x.shape[1]
    nc = s // DN_CHUNK
    depth = w_in.shape[0]
    in_shard, out_shard, conv_shard = w_in.shape[2], w_out.shape[1], conv_w.shape[2]
    x0 = x.reshape(s, D_MODEL)
    target = loss_target.reshape(s, D_MODEL)

    w_in_shard = _in_shard_t(w_in.astype(BF16))
    w_out_shard = w_out.astype(BF16)

    gathered_in_shape = jax.ShapeDtypeStruct((N_DEV * in_shard, SUBLANES, LANE), BF16)
    gathered_out_shape = jax.ShapeDtypeStruct((D_MODEL, D_MODEL), BF16)
    gathered_in, gathered_out, conv_all = _exchange("gather_first", _Comm(
        [w_in_shard[0], w_out_shard[0], conv_w],
        [gathered_in_shape, gathered_out_shape, jax.ShapeDtypeStruct((N_DEV,) + conv_w.shape, F32)],
        [(0, 0, _whole, _row_block(in_shard, 0)), (1, 1, _whole, _row_block(out_shard, 0)), (2, 2, _whole, _slot)]), two_level=True)
    gather_plan = lambda l: _Comm([w_in_shard[l], w_out_shard[l]], [gathered_in_shape, gathered_out_shape],
                                  [(0, 0, _whole, _row_block(in_shard, 0)), (1, 1, _whole, _row_block(out_shard, 0))])
    conv_full = conv_all.transpose(1, 2, 0, 3).reshape(depth, CONV_W, 3 * D_DN)
    conv_full = jnp.pad(conv_full, ((0, 0), (0, HALO - CONV_W), (0, 0)))
    sgu_b_t = jnp.swapaxes(sgu_b, 1, 2)

    def rows_of(gates, first_lane):
        return gates[:, first_lane:first_lane + N_GROUPS].reshape(nc, DN_CHUNK, N_GROUPS).transpose(0, 2, 1)

    saved = []
    xl = x0
    for l in range(depth):
        w_in_t = gathered_in.reshape(D_IN, D_MODEL)
        w_gate_t = jnp.pad(w_in_t[D_MAIN:], ((0, LANE - N_GATE), (0, 0)))
        w_out_l = gathered_out
        nw = norm_w[l].reshape(1, D_MODEL)
        h, h_t = _rms_fwd(xl, nw)
        proj = _matmul([(_lhs(h, tm=MM_WIDE), _rhs(w_in_t, MM_WIDE, transposed=True))], s, D_MAIN, name="in_proj", trans_b=True, tn=MM_WIDE)
        p_gate = _matmul([(_lhs(h), _rhs(w_gate_t, LANE, transposed=True))], s, LANE, name="in_proj_gate", trans_b=True, tn=LANE)
        ln_g, ln_b = sgu_ln_g[l].reshape(1, D_SGU), sgu_ln_b[l].reshape(1, D_SGU)
        y, y_t = _sgu_fwd(proj, ln_g, ln_b, sgu_w[l], sgu_b_t[l])
        a_log = _lane_row(a_log_f[l], a_log_b[l])
        dt_bias = _lane_row(dt_bias_f[l], dt_bias_b[l])
        q, k, v, gates = _prep_fwd(proj, conv_full[l], p_gate, a_log, dt_bias)
        rows_f, rows_b = rows_of(gates, 0), rows_of(gates, N_GROUPS)
        more = l + 1 < depth
        (o_f, st_f, t_f), (o_b, st_b, t_b), next_weights = _delta_fwd(q, k, v, gates, rows_f, rows_b,
                                                                     comm=gather_plan(l + 1) if more else None)
        dn_w = dn_norm_w[l].reshape(1, HEAD)
        y, y_t = _post_fwd(o_f, o_b, proj, dn_w, y, y_t)
        x_next = _matmul([(_lhs(y, tm=MM_WIDE), _rhs(w_out_l, MM_WIDE))], s, D_MODEL, name="out_proj", add=xl, tn=MM_WIDE)
        saved.append(dict(x=xl, nw=nw, h_t=h_t, proj=proj, p_gate=p_gate, ln_g=ln_g, ln_b=ln_b, a_log=a_log, dt_bias=dt_bias,
                          q=q, k=k, v=v, gates=gates, rows_f=rows_f, rows_b=rows_b, o_f=o_f, o_b=o_b, st_f=st_f, st_b=st_b,
                          t_f=t_f, t_b=t_b, dn_w=dn_w, y_t=y_t, w_in_t=w_in_t, w_gate_t=w_gate_t, w_out=w_out_l))
        xl = x_next
        if more:
            gathered_in, gathered_out = next_weights

    loss_part, dx, dx_bf, d_final = _final_loss(xl, final_norm_w.reshape(1, D_MODEL), target)
    loss = lax.psum(loss_part[0, 0], MESH_AXES)

    t_rows = -(-D_IN // LANE) * LANE
    layer_small = [n for n in SMALL if n != "final_norm_w"]
    n_layer_small = sum(int(np.prod(weights[n].shape[1:])) for n in layer_small) + D_MODEL
    small_rows = -(-n_layer_small // (FLAT_W * HALO)) * HALO

    def small_segments(values):
        top = values["final_norm_w"]
        return jnp.concatenate([_to_flat([values[n][l] for n in layer_small] + [top if l == depth - 1 else jnp.zeros_like(top)], HALO)
                                for l in range(depth)], axis=0)

    def small_to_send(layer_grads, top_grad):
        conv = layer_grads["conv_w"].reshape(CONV_W, N_DEV, conv_shard).transpose(1, 0, 2).reshape(N_DEV, -1)
        rest = jnp.concatenate([layer_grads[n].reshape(-1) for n in layer_small[1:]] + [top_grad])
        flat = jnp.concatenate([conv, jnp.broadcast_to(rest, (N_DEV, rest.shape[0]))], axis=1)
        return jnp.pad(flat, ((0, 0), (0, small_rows * FLAT_W - flat.shape[1]))).reshape(N_DEV, small_rows, FLAT_W)

    layer_slot = lambda l: (lambda ref, dev: ref.at[dev, l])
    small_slot = lambda l: (lambda ref, dev: ref.at[dev, pl.ds(l * small_rows, small_rows)])
    parts = [jax.ShapeDtypeStruct((N_DEV, depth, in_shard, SUBLANES, LANE), BF16),
             jax.ShapeDtypeStruct((N_DEV, depth, out_shard, D_MODEL), BF16),
             jax.ShapeDtypeStruct((N_DEV, depth * small_rows, FLAT_W), F32)]

    def exchange_plan(layer, g_in_rows, g_out, small=None):
        inputs, outputs = [g_in_rows, g_out], parts[:2]
        copies = [(0, 0, _row_block(in_shard, 0), layer_slot(layer)), (1, 1, _row_block(out_shard, 0), layer_slot(layer))]
        if small is not None:
            inputs, outputs = inputs + [small], parts
            copies.append((2, 2, _slot, small_slot(layer)))
        return _Comm(inputs, outputs, copies)

    pending = None
    for l in reversed(range(depth)):
        sv = saved[l]
        dy = _matmul([(_lhs(dx_bf, tm=MM_WIDE), _rhs(sv["w_out"], MM_WIDE, transposed=True))], s, D_MODEL, name="out_proj_dy", trans_b=True,
                     tn=MM_WIDE)
        g_out = _matmul([(_lhs(sv["y_t"]), _rhs(dx_bf, MM_TILE))], D_MODEL, D_MODEL, name="out_proj_dw", out_dtype=BF16)
        do, dp, d_dn = _post_bwd(sv["o_f"], sv["o_b"], sv["proj"], sv["dn_w"], dy)
        (*dqkv_f, dgate_f, drows_f), (*dqkv_b, dgate_b, drows_b), exchanged = _delta_bwd(
            sv["q"], sv["k"], sv["v"], sv["gates"], sv["rows_f"], sv["rows_b"], sv["st_f"], sv["st_b"], sv["t_f"], sv["t_b"], do,
            comm=None if pending is None else exchange_plan(*pending))
        if exchanged:
            parts = list(exchanged)
        drows = jnp.concatenate([drows_f.transpose(0, 2, 1).reshape(s, N_GROUPS), drows_b.transpose(0, 2, 1).reshape(s, N_GROUPS)], axis=1)
        drows = jnp.pad(drows, ((0, 0), (0, LANE - 2 * N_GROUPS)))
        dp, d_gate, d_conv, d_alog, d_dt = _prep_bwd(sv["proj"], conv_full[l], sv["p_gate"], sv["a_log"], sv["dt_bias"],
                                                     dqkv_f, dqkv_b, (dgate_f, dgate_b, drows), dp)
        dp, d_lg, d_lb, d_sw, d_sbt = _sgu_bwd(sv["proj"], sv["ln_g"], sv["ln_b"], sgu_w[l], sgu_b_t[l], dy, dp)
        h_t = sv["h_t"]
        g_in_t = _matmul([(_lhs(h_t), _rhs(dp, MM_TILE))], D_MODEL, D_MAIN, name="in_proj_dw", out_dtype=BF16, out_t_rows=t_rows)
        g_in_t = _matmul([(_lhs(h_t), _rhs(d_gate, LANE))], D_MODEL, LANE, name="in_proj_dw_gate", out_dtype=BF16, tn=LANE,
                         transpose_into=g_in_t, out_row_block=D_MAIN // LANE)
        big = (l, _rows_view(g_in_t), g_out)
        dh_pairs = [(_lhs(dp), _rhs(sv["w_in_t"], MM_TILE, k=D_MAIN)), (_lhs(d_gate), _rhs(sv["w_gate_t"], MM_TILE))]
        if l > 0:
            dh = _matmul(dh_pairs, s, D_MODEL, name="in_proj_dh")
        else:
            dh, exchanged = _matmul(dh_pairs, s, D_MODEL, name="in_proj_dh_comm", comm=exchange_plan(*big))
            parts = [*exchanged, parts[2]]
        dx, dx_bf, d_nw = _rms_bwd(sv["x"], sv["nw"], dh, dx)
        layer_grads = dict(
            conv_w=d_conv[:CONV_W], norm_w=d_nw, sgu_ln_g=d_lg, sgu_ln_b=d_lb, sgu_w=d_sw, sgu_b=d_sbt.T,
            a_log_f=d_alog[0, :N_GROUPS], a_log_b=d_alog[0, N_GROUPS:2 * N_GROUPS],
            dt_bias_f=d_dt[0, :N_GROUPS], dt_bias_b=d_dt[0, N_GROUPS:2 * N_GROUPS], dn_norm_w=d_dn)
        small = small_to_send(layer_grads, d_final.reshape(D_MODEL) if l == depth - 1 else jnp.zeros((D_MODEL,), F32))
        pending = (*big, small)
    grad_x = dx.reshape(1, s, D_MODEL)

    parts_in, parts_out, parts_small = parts
    in_rows = 100 if in_shard % 100 == 0 else in_shard
    g_in_t, (parts_small,) = _sum_parts("sum_in_comm", parts_in, (1, in_rows, SUBLANES, LANE), comm=_Comm(
        [pending[3]], [parts_small], [(0, 0, _slot, small_slot(pending[0]))]))
    g_in = jnp.swapaxes(g_in_t.reshape(depth, in_shard, D_MODEL), 1, 2)
    res_in = _adamw("adamw_in", g_in[None], w_in, m_w_in, v_w_in, (1, min(256, D_MODEL), in_shard))
    res_out = _adamw("adamw_out", parts_out, w_out, m_w_out, v_w_out, (1, min(128, out_shard), D_MODEL))
    res_small = _adamw("adamw_small", parts_small, small_segments(weights), small_segments(m_in), small_segments(v_in),
                       (small_rows, FLAT_W))
    outs = [dict(w_in=a, w_out=b) for a, b in zip(res_in, res_out)]
    for o, d in zip(res_small, outs):
        by_layer, off = o.reshape(depth, small_rows * FLAT_W), 0
        for n in layer_small:
            size = int(np.prod(weights[n].shape[1:]))
            d[n] = by_layer[:, off:off + size].reshape(weights[n].shape)
            off += size
        d["final_norm_w"] = by_layer[depth - 1, off:off + D_MODEL]
    g_out, delta_out, m_out, v_out = outs
    return (loss, grad_x, *[g_out[n] for n in WEIGHTS], *[delta_out[n] for n in WEIGHTS],
            *[m_out[n] for n in WEIGHTS], *[v_out[n] for n in WEIGHTS])
```

```python
import functools

import numpy as np
import jax
import jax.numpy as jnp
from jax import lax
from jax.experimental import pallas as pl
from jax.experimental.pallas import tpu as pltpu

F32 = jnp.float32
BF16 = jnp.bfloat16

N_DEV = 8
D_MODEL = 2048
D_SGU = 1024
D_DN = 1024
N_GROUPS = 8
HEAD = 128
SGU_CHUNK = 128
DN_CHUNK = 64
CONV_W = 5
N_GATE = 32
D_MAIN = 3 * D_SGU + 4 * D_DN
D_IN = D_MAIN + N_GATE
LANE = 128
SUBLANES = 16
HALO = 8
EPS = 1e-6
ADAM_LR, ADAM_B1, ADAM_B2, ADAM_EPS, ADAM_WD, ADAM_STEP = 0.001, 0.9, 0.999, 1e-08, 0.01, 10
FLAT_W = 1024
MM_TILE = 512
MM_WIDE = 1024
VMEM_MARGIN = 8 << 20

MESH_AXES = ("x", "y", "c")
WEIGHTS = ("norm_w", "w_in", "sgu_ln_g", "sgu_ln_b", "sgu_w", "sgu_b", "conv_w", "a_log_f", "a_log_b",
           "dt_bias_f", "dt_bias_b", "dn_norm_w", "w_out", "final_norm_w")
SMALL = ("conv_w",) + tuple(n for n in WEIGHTS if n not in ("w_in", "w_out", "conv_w"))


def _bdot(a, b, dims):
    return lax.dot_general(a.astype(BF16), b.astype(BF16), (dims, ((), ())), preferred_element_type=F32)


@jax.custom_vjp
def _mm_nn(a, b):
    return _bdot(a, b, ((1,), (0,)))


def _mm_nn_fwd(a, b):
    return _mm_nn(a, b), (a, b)


def _mm_nn_bwd(res, ct):
    a, b = res
    return _bdot(ct, b, ((1,), (1,))), _bdot(a, ct, ((0,), (0,)))


_mm_nn.defvjp(_mm_nn_fwd, _mm_nn_bwd)


@jax.custom_vjp
def _mm_nt(a, b):
    return _bdot(a, b, ((1,), (1,)))


def _mm_nt_fwd(a, b):
    return _mm_nt(a, b), (a, b)


def _mm_nt_bwd(res, ct):
    a, b = res
    return _bdot(ct, b, ((1,), (0,))), _bdot(ct, a, ((0,), (0,)))


_mm_nt.defvjp(_mm_nt_fwd, _mm_nt_bwd)


@jax.custom_vjp
def _mm_tn(a, b):
    return _bdot(a, b, ((0,), (0,)))


def _mm_tn_fwd(a, b):
    return _mm_tn(a, b), (a, b)


def _mm_tn_bwd(res, ct):
    a, b = res
    return _bdot(b, ct, ((1,), (1,))), _bdot(a, ct, ((1,), (0,)))


_mm_tn.defvjp(_mm_tn_fwd, _mm_tn_bwd)


@jax.custom_vjp
def _inverse_given(m, t):
    return t


def _inverse_given_fwd(m, t):
    return t, t


def _inverse_given_bwd(t, ct):
    dm = -_bdot(t, _bdot(ct, t, ((1,), (1,))), ((0,), (0,)))
    return dm, jnp.zeros_like(t)


_inverse_given.defvjp(_inverse_given_fwd, _inverse_given_bwd)


@jax.custom_vjp
def _split_lanes(x):
    return tuple(x[:, i * LANE:(i + 1) * LANE] for i in range(x.shape[1] // LANE))


def _split_lanes_fwd(x):
    return _split_lanes(x), None


def _split_lanes_bwd(_, cts):
    return (jnp.concatenate(cts, axis=1),)


_split_lanes.defvjp(_split_lanes_fwd, _split_lanes_bwd)


def _silu(t):
    return t * jax.nn.sigmoid(t)


def _gelu(t):
    return 0.5 * t * (1.0 + lax.erf(t * np.float32(0.7071067811865476)))


def _rms(x, w):
    return x * lax.rsqrt(jnp.mean(x * x, axis=-1, keepdims=True) + EPS) * w


def _params(sem, vmem_bytes=None):
    kw = dict(dimension_semantics=sem)
    if vmem_bytes is not None:
        kw["vmem_limit_bytes"] = int(vmem_bytes)
    return pltpu.CompilerParams(**kw)


def _lhs(a, k=None, k_block=0, tm=MM_TILE):
    k = a.shape[1] if k is None else k
    tm = min(tm, a.shape[0])
    return a, pl.BlockSpec((tm, k), lambda i, j: (i, k_block)), tm * k * a.dtype.itemsize


def _rhs(b, tn, *, layer=None, transposed=False, k=None, k_block=0, n_offset=0):
    lead = () if layer is None else (layer,)
    none = () if layer is None else (None,)
    if transposed:
        k = b.shape[-1]
        spec = pl.BlockSpec(none + (tn, k), lambda i, j: lead + (j + n_offset, 0))
    else:
        k = b.shape[-2] if k is None else k
        spec = pl.BlockSpec(none + (k, tn), lambda i, j: lead + (k_block, j))
    return b, spec, tn * k * b.dtype.itemsize


def _matmul(pairs, m, n, *, name, trans_b=False, add=None, out_dtype=F32, tn=MM_TILE, out_t_rows=None, transpose_into=None,
            out_row_block=0, comm=None):
    tm = pairs[0][0][1].block_shape[0]
    assert m % tm == 0 and n % tn == 0 and all(a_spec.block_shape[0] == tm for (_, a_spec, _), _ in pairs)
    n_pairs = len(pairs)
    contract = ((1,), (1,)) if trans_b else ((1,), (0,))
    transposed_out = out_t_rows is not None or transpose_into is not None

    def body(*refs):
        o_ref = refs[-1]
        acc = None
        for i in range(n_pairs):
            d = lax.dot_general(refs[2 * i][...].astype(BF16), refs[2 * i + 1][...].astype(BF16),
                                (contract, ((), ())), preferred_element_type=F32)
            acc = d if acc is None else acc + d
        if add is not None:
            acc = acc + refs[2 * n_pairs][...]
        o_ref[...] = (acc.T if transposed_out else acc).astype(out_dtype)

    in_specs, args, vmem, aliases = [], [], 0, {}
    for (a, a_spec, a_bytes), (b, b_spec, b_bytes) in pairs:
        in_specs += [a_spec, b_spec]
        args += [a, b]
        vmem += 2 * (a_bytes + b_bytes)
    if add is not None:
        in_specs.append(pl.BlockSpec((tm, tn), lambda i, j: (i, j)))
        args.append(add)
        vmem += 2 * tm * tn * 4
    vmem += 2 * tm * tn * jnp.dtype(out_dtype).itemsize + 3 * tm * tn * 4
    if transposed_out:
        out_spec = pl.BlockSpec((tn, tm), lambda i, j: (j + out_row_block, i))
        if transpose_into is not None:
            in_specs.append(pl.BlockSpec(memory_space=pl.ANY))
            args.append(transpose_into)
            aliases = {len(args) - 1: 0}
            out_shape = jax.ShapeDtypeStruct(transpose_into.shape, out_dtype)
        else:
            out_shape = jax.ShapeDtypeStruct((out_t_rows, m), out_dtype)
    else:
        out_spec = pl.BlockSpec((tm, tn), lambda i, j: (i, j))
        out_shape = jax.ShapeDtypeStruct((m, n), out_dtype)
    if comm is not None:
        assert not aliases
        (out,), sent = _call_with_comm(
            body, comm, name=name, grid=(m // tm, n // tn), in_specs=in_specs, out_specs=[out_spec], out_shape=[out_shape],
            scratch_shapes=[], args=args, compiler_params=_params(("arbitrary", "arbitrary"), vmem + VMEM_MARGIN))
        return out, sent
    return pl.pallas_call(
        body, name=name, grid=(m // tm, n // tn), in_specs=in_specs, out_specs=out_spec, out_shape=out_shape,
        input_output_aliases=aliases, compiler_params=_params(("parallel", "arbitrary"), vmem + VMEM_MARGIN),
    )(*args)


def _rms_fwd(x, w, *, tile=512):
    s, d = x.shape
    tile = min(tile, s)

    def body(x_ref, w_ref, h_ref, ht_ref):
        h = _rms(x_ref[...], w_ref[...])
        h_ref[...] = h.astype(BF16)
        ht_ref[...] = h.T.astype(BF16)

    return pl.pallas_call(
        body, name="rms_fwd", grid=(s // tile,),
        in_specs=[pl.BlockSpec((tile, d), lambda i: (i, 0)), pl.BlockSpec((1, d), lambda i: (0, 0))],
        out_specs=[pl.BlockSpec((tile, d), lambda i: (i, 0)), pl.BlockSpec((d, tile), lambda i: (0, i))],
        out_shape=[jax.ShapeDtypeStruct((s, d), BF16), jax.ShapeDtypeStruct((d, s), BF16)],
        compiler_params=_params(("arbitrary",)),
    )(x, w)


def _rms_bwd(x, w, dh, dx_out, *, tile=256):
    s, d = x.shape
    tile = min(tile, s)

    def body(x_ref, w_ref, dh_ref, dxo_ref, dx_ref, dxb_ref, dw_ref):
        _, vjp = jax.vjp(_rms, x_ref[...], w_ref[...])
        dxv, dwv = vjp(dh_ref[...])
        dx = dxo_ref[...] + dxv
        dx_ref[...] = dx
        dxb_ref[...] = dx.astype(BF16)

        @pl.when(pl.program_id(0) == 0)
        def _():
            dw_ref[...] = jnp.zeros_like(dw_ref)

        dw_ref[...] += dwv

    row = pl.BlockSpec((tile, d), lambda i: (i, 0))
    vec = pl.BlockSpec((1, d), lambda i: (0, 0))
    return pl.pallas_call(
        body, name="rms_bwd", grid=(s // tile,),
        in_specs=[row, vec, row, row], out_specs=[row, row, vec],
        out_shape=[jax.ShapeDtypeStruct((s, d), F32), jax.ShapeDtypeStruct((s, d), BF16), jax.ShapeDtypeStruct((1, d), F32)],
        compiler_params=_params(("arbitrary",)),
    )(x, w, dh, dx_out)


def _final_loss(x, w, target, *, tile=256):
    s, d = x.shape
    tile = min(tile, s)

    def body(x_ref, w_ref, t_ref, loss_ref, dx_ref, dxb_ref, dw_ref):
        def f(xv, wv):
            err = jnp.square(_rms(xv, wv) - t_ref[...])
            per_token = jnp.sum(err, axis=1, keepdims=True) * np.float32(1.0 / d)
            return 0.5 * jnp.sum(per_token, axis=0, keepdims=True)

        loss, vjp = jax.vjp(f, x_ref[...], w_ref[...])
        dxv, dwv = vjp(jnp.ones((1, 1), F32))
        dx_ref[...] = dxv
        dxb_ref[...] = dxv.astype(BF16)

        @pl.when(pl.program_id(0) == 0)
        def _():
            dw_ref[...] = jnp.zeros_like(dw_ref)
            loss_ref[...] = jnp.zeros_like(loss_ref)

        dw_ref[...] += dwv
        loss_ref[...] += jnp.broadcast_to(loss, (1, LANE))

    row = pl.BlockSpec((tile, d), lambda i: (i, 0))
    vec = pl.BlockSpec((1, d), lambda i: (0, 0))
    return pl.pallas_call(
        body, name="final_loss", grid=(s // tile,),
        in_specs=[row, vec, row], out_specs=[pl.BlockSpec((1, LANE), lambda i: (0, 0)), row, row, vec],
        out_shape=[jax.ShapeDtypeStruct((1, LANE), F32), jax.ShapeDtypeStruct((s, d), F32),
                   jax.ShapeDtypeStruct((s, d), BF16), jax.ShapeDtypeStruct((1, d), F32)],
        compiler_params=_params(("arbitrary",)),
    )(x, w, target)


def _sgu_chunk(u, v, z, ln_g, ln_b, ws, bcols):
    vg = _gelu(v)
    xc = vg - jnp.mean(vg, axis=-1, keepdims=True)
    vl = xc * lax.rsqrt(jnp.mean(xc * xc, axis=-1, keepdims=True) + EPS) * ln_g + ln_b
    sp = jnp.concatenate([_mm_nn(ws[g], vg_) + bcols[g] for g, vg_ in enumerate(_split_lanes(vl))], axis=1)
    return _gelu(u) * sp * _silu(z)


def _sgu_specs(tile):
    col = lambda j: pl.BlockSpec((tile, D_SGU), lambda i, j=j: (i, j))
    vec = pl.BlockSpec((1, D_SGU), lambda i: (0, 0))
    wspec = pl.BlockSpec((N_GROUPS, SGU_CHUNK, SGU_CHUNK), lambda i: (0, 0, 0))
    bspec = pl.BlockSpec((SGU_CHUNK, N_GROUPS), lambda i: (0, 0))
    return col, vec, wspec, bspec


def _sgu_fwd(proj, ln_g, ln_b, w_s, b_t, *, tile=256):
    s = proj.shape[0]
    tile = min(tile, s)
    col, vec, wspec, bspec = _sgu_specs(tile)

    def body(u_ref, v_ref, z_ref, g_ref, b_ref, w_ref, bt_ref, y_ref, yt_ref):
        ws = tuple(w_ref[g] for g in range(N_GROUPS))
        bcols = tuple(bt_ref[:, g:g + 1] for g in range(N_GROUPS))
        for c in range(tile // SGU_CHUNK):
            rows = pl.ds(c * SGU_CHUNK, SGU_CHUNK)
            y = _sgu_chunk(u_ref[rows, :], v_ref[rows, :], z_ref[rows, :], g_ref[...], b_ref[...], ws, bcols)
            y_ref[rows, :] = y.astype(BF16)
            yt_ref[:, rows] = y.T.astype(BF16)

    return pl.pallas_call(
        body, name="sgu_fwd", grid=(s // tile,),
        in_specs=[col(0), col(1), col(2), vec, vec, wspec, bspec],
        out_specs=[pl.BlockSpec((tile, D_SGU), lambda i: (i, 0)), pl.BlockSpec((D_SGU, tile), lambda i: (0, i))],
        out_shape=[jax.ShapeDtypeStruct((s, D_SGU + D_DN), BF16), jax.ShapeDtypeStruct((D_SGU + D_DN, s), BF16)],
        compiler_params=_params(("arbitrary",)),
    )(proj, proj, proj, ln_g, ln_b, w_s, b_t)


def _sgu_bwd(proj, ln_g, ln_b, w_s, b_t, dy, dp, *, tile=128):
    s = proj.shape[0]
    tile = min(tile, s)
    col, vec, wspec, bspec = _sgu_specs(tile)

    def body(u_ref, v_ref, z_ref, g_ref, b_ref, w_ref, bt_ref, dy_ref, _, dp_ref, dg_ref, db_ref, dw_ref, dbt_ref):
        @pl.when(pl.program_id(0) == 0)
        def _():
            dg_ref[...] = jnp.zeros_like(dg_ref)
            db_ref[...] = jnp.zeros_like(db_ref)
            dw_ref[...] = jnp.zeros_like(dw_ref)
            dbt_ref[...] = jnp.zeros_like(dbt_ref)

        ws = tuple(w_ref[g] for g in range(N_GROUPS))
        bcols = tuple(bt_ref[:, g:g + 1] for g in range(N_GROUPS))
        for c in range(tile // SGU_CHUNK):
            rows = pl.ds(c * SGU_CHUNK, SGU_CHUNK)
            _, vjp = jax.vjp(_sgu_chunk, u_ref[rows, :], v_ref[rows, :], z_ref[rows, :], g_ref[...], b_ref[...], ws, bcols)
            du, dv, dz, dg, db, dws, dbc = vjp(dy_ref[rows, :])
            dp_ref[rows, pl.ds(0, D_SGU)] = du.astype(BF16)
            dp_ref[rows, pl.ds(D_SGU, D_SGU)] = dv.astype(BF16)
            dp_ref[rows, pl.ds(2 * D_SGU, D_SGU)] = dz.astype(BF16)
            dg_ref[...] += dg
            db_ref[...] += db
            for g in range(N_GROUPS):
                dw_ref[g] += dws[g]
                dbt_ref[:, g:g + 1] += dbc[g]

    return pl.pallas_call(
        body, name="sgu_bwd", grid=(s // tile,),
        in_specs=[col(0), col(1), col(2), vec, vec, wspec, bspec, pl.BlockSpec((tile, D_SGU), lambda i: (i, 0)),
                  pl.BlockSpec(memory_space=pl.ANY)],
        out_specs=[pl.BlockSpec((tile, 3 * D_SGU), lambda i: (i, 0)), vec, vec, wspec, bspec],
        out_shape=[jax.ShapeDtypeStruct(dp.shape, BF16), jax.ShapeDtypeStruct((1, D_SGU), F32),
                   jax.ShapeDtypeStruct((1, D_SGU), F32), jax.ShapeDtypeStruct((N_GROUPS, SGU_CHUNK, SGU_CHUNK), F32),
                   jax.ShapeDtypeStruct((SGU_CHUNK, N_GROUPS), F32)],
        input_output_aliases={8: 0},
        compiler_params=_params(("arbitrary",)),
    )(proj, proj, proj, ln_g, ln_b, w_s, b_t, dy, dp)


QKV_BLOCK = 1


def _qkv_head(cq, ck, cv):
    q, k, v = _silu(cq), _silu(ck), _silu(cv)
    q = q * lax.rsqrt(jnp.sum(q * q, axis=-1, keepdims=True) + EPS) * np.float32(HEAD ** -0.5)
    k = k * lax.rsqrt(jnp.sum(k * k, axis=-1, keepdims=True) + EPS)
    return q, k, v


def _gate_fn(p, a_log, dt_bias):
    lane = lax.broadcasted_iota(jnp.int32, p.shape, 1)
    g = -jnp.exp(a_log) * jax.nn.softplus(p + dt_bias)
    return jnp.where(lane < 16, g, jnp.where(lane < N_GATE, jax.nn.sigmoid(p), 0.0))


def _shifted_rows(full, shift, start, rows):
    n = full.shape[0]
    assert start % HALO == 0 and 0 <= start + shift and start + shift + rows <= n
    rolled = full if shift == 0 else pltpu.roll(full, (-shift) % n, 0)
    return rolled[start:start + rows, :]


def _halo_maps(tile, s):
    per = tile // HALO
    prev = lambda i: jnp.maximum(i * per - 1, 0)
    nxt = lambda i: jnp.minimum((i + 1) * per, s // HALO - 1)
    return prev, nxt


def _prep_fwd(proj, conv_w, pg, a_log, dt_bias, *, tile=256):
    s = proj.shape[0]
    width = 3 * D_DN
    tile = min(tile, s)
    n_blk = s // tile
    prev, nxt = _halo_maps(tile, s)

    def body(x_ref, xp_ref, xn_ref, w_ref, pg_ref, al_ref, dt_ref, q_ref, k_ref, v_ref, gate_ref, ext):
        i = pl.program_id(0)
        ext[pl.ds(0, HALO), :] = jnp.where(i > 0, xp_ref[...], 0.0)
        ext[pl.ds(HALO, tile), :] = x_ref[...]
        ext[pl.ds(HALO + tile, HALO), :] = jnp.where(i < n_blk - 1, xn_ref[...], 0.0)
        full, c = ext[...], None
        for j in range(CONV_W):
            term = _shifted_rows(full, j - CONV_W // 2, HALO, tile) * w_ref[j:j + 1, :]
            c = term if c is None else c + term
        for h in range(N_GROUPS):
            q, k, v = _qkv_head(c[:, h * HEAD:(h + 1) * HEAD], c[:, D_DN + h * HEAD:D_DN + (h + 1) * HEAD],
                                c[:, 2 * D_DN + h * HEAD:2 * D_DN + (h + 1) * HEAD])
            q_ref[h] = q
            k_ref[h] = k
            v_ref[h] = v
        gate_ref[...] = _gate_fn(pg_ref[...], al_ref[...], dt_ref[...])

    hm = pl.BlockSpec((N_GROUPS, tile, HEAD), lambda i: (0, i, 0))
    lane_vec = pl.BlockSpec((1, LANE), lambda i: (0, 0))
    return pl.pallas_call(
        body, name="prep_fwd", grid=(n_blk,),
        in_specs=[pl.BlockSpec((tile, width), lambda i: (i, QKV_BLOCK)),
                  pl.BlockSpec((HALO, width), lambda i: (prev(i), QKV_BLOCK)),
                  pl.BlockSpec((HALO, width), lambda i: (nxt(i), QKV_BLOCK)),
                  pl.BlockSpec((HALO, width), lambda i: (0, 0)),
                  pl.BlockSpec((tile, LANE), lambda i: (i, 0)), lane_vec, lane_vec],
        out_specs=[hm, hm, hm, pl.BlockSpec((tile, LANE), lambda i: (i, 0))],
        out_shape=[jax.ShapeDtypeStruct((N_GROUPS, s, HEAD), F32)] * 3 + [jax.ShapeDtypeStruct((s, LANE), F32)],
        scratch_shapes=[pltpu.VMEM((tile + 2 * HALO, width), F32)],
        compiler_params=_params(("arbitrary",)),
    )(proj, proj, proj, conv_w, pg, a_log, dt_bias)


def _prep_bwd(proj, conv_w, pg, a_log, dt_bias, dqkv_f, dqkv_b, dgates, dp, *, tile=128):
    s = proj.shape[0]
    width = 3 * D_DN
    tile = min(tile, s)
    n_blk = s // tile
    prev, nxt = _halo_maps(tile, s)
    ext_rows = tile + 2 * HALO
    pad = CONV_W // 2

    def body(*refs):
        x_ref, xp_ref, xn_ref, w_ref, pg_ref, al_ref, dt_ref = refs[:7]
        ct_refs = refs[7:25]
        dg0_ref, dg1_ref, dg2_ref = refs[25:28]
        dp_ref, dpg_ref, dw_ref, dal_ref, ddt_ref, xext, dcext = refs[29:]
        i = pl.program_id(0)
        has_prev, has_next = i > 0, i < n_blk - 1

        @pl.when(i == 0)
        def _():
            dw_ref[...] = jnp.zeros_like(dw_ref)
            dal_ref[...] = jnp.zeros_like(dal_ref)
            ddt_ref[...] = jnp.zeros_like(ddt_ref)

        zeros = jnp.zeros((HALO, width), F32)
        xext[pl.ds(0, HALO), :] = zeros
        xext[pl.ds(HALO, HALO), :] = jnp.where(has_prev, xp_ref[...], 0.0)
        xext[pl.ds(2 * HALO, tile), :] = x_ref[...]
        xext[pl.ds(2 * HALO + tile, HALO), :] = jnp.where(has_next, xn_ref[...], 0.0)
        xext[pl.ds(3 * HALO + tile, HALO), :] = zeros
        xfull, c = xext[...], None
        for j in range(CONV_W):
            term = _shifted_rows(xfull, j - pad, HALO, ext_rows) * w_ref[j:j + 1, :]
            c = term if c is None else c + term
        for h in range(N_GROUPS):
            lanes = [pl.ds(part * D_DN + h * HEAD, HEAD) for part in range(3)]
            cts = []
            for part in range(3):
                (cur, before, after), (cur2, before2, after2) = ct_refs[3 * part:3 * part + 3], ct_refs[9 + 3 * part:12 + 3 * part]
                cts.append(jnp.concatenate([jnp.where(has_prev, before[h] + before2[h], 0.0), cur[h] + cur2[h],
                                            jnp.where(has_next, after[h] + after2[h], 0.0)], axis=0))
            _, vjp = jax.vjp(_qkv_head, c[:, h * HEAD:(h + 1) * HEAD], c[:, D_DN + h * HEAD:D_DN + (h + 1) * HEAD],
                             c[:, 2 * D_DN + h * HEAD:2 * D_DN + (h + 1) * HEAD])
            for lane_sl, dc in zip(lanes, vjp(tuple(cts))):
                dcext[:, lane_sl] = dc
        dcfull = dcext[...]
        dx, dc_blk = None, dcfull[HALO:HALO + tile, :]
        for j in range(CONV_W):
            term = _shifted_rows(dcfull, pad - j, HALO, tile) * w_ref[j:j + 1, :]
            dx = term if dx is None else dx + term
            dw_ref[j:j + 1, :] += jnp.sum(dc_blk * _shifted_rows(xfull, j - pad, 2 * HALO, tile), axis=0, keepdims=True)
        dp_ref[...] = dx.astype(BF16)
        _, gvjp = jax.vjp(_gate_fn, pg_ref[...], al_ref[...], dt_ref[...])
        dpg, dal, ddt = gvjp(dg0_ref[...] + dg1_ref[...] + dg2_ref[...])
        dpg_ref[...] = dpg.astype(BF16)
        dal_ref[...] += dal
        ddt_ref[...] += ddt

    xs = [pl.BlockSpec((tile, width), lambda i: (i, QKV_BLOCK)), pl.BlockSpec((HALO, width), lambda i: (prev(i), QKV_BLOCK)),
          pl.BlockSpec((HALO, width), lambda i: (nxt(i), QKV_BLOCK))]
    hm = [pl.BlockSpec((N_GROUPS, tile, HEAD), lambda i: (0, i, 0)),
          pl.BlockSpec((N_GROUPS, HALO, HEAD), lambda i: (0, prev(i), 0)),
          pl.BlockSpec((N_GROUPS, HALO, HEAD), lambda i: (0, nxt(i), 0))]
    gate = pl.BlockSpec((tile, LANE), lambda i: (i, 0))
    lane_vec = pl.BlockSpec((1, LANE), lambda i: (0, 0))
    wspec = pl.BlockSpec((HALO, width), lambda i: (0, 0))
    n_in = 3 + 4 + 18 + 3
    ct_arrays = [a for dqkv in (dqkv_f, dqkv_b) for a in dqkv for _ in range(3)]
    return pl.pallas_call(
        body, name="prep_bwd", grid=(n_blk,),
        in_specs=xs + [wspec, gate, lane_vec, lane_vec] + hm * 6 + [gate] * 3 + [pl.BlockSpec(memory_space=pl.ANY)],
        out_specs=[pl.BlockSpec((tile, width), lambda i: (i, QKV_BLOCK)), gate, wspec, lane_vec, lane_vec],
        out_shape=[jax.ShapeDtypeStruct(dp.shape, BF16), jax.ShapeDtypeStruct((s, LANE), BF16),
                   jax.ShapeDtypeStruct((HALO, width), F32), jax.ShapeDtypeStruct((1, LANE), F32),
                   jax.ShapeDtypeStruct((1, LANE), F32)],
        input_output_aliases={n_in: 0},
        scratch_shapes=[pltpu.VMEM((tile + 4 * HALO, width), F32), pltpu.VMEM((ext_rows, width), F32)],
        compiler_params=_params(("arbitrary",)),
    )(proj, proj, proj, conv_w, pg, a_log, dt_bias, *ct_arrays, *dgates, dp)


def _each(f, *cols):
    return tuple(f(*a) for a in zip(*cols))


def _delta_chunk(q, k, v, g_col, g_row, beta, state, t_known, reverse):
    n = DN_CHUNK
    ri = lax.broadcasted_iota(jnp.int32, (n, n), 0)
    ci = lax.broadcasted_iota(jnp.int32, (n, n), 1)
    masks = {False: (ri >= ci, ri <= ci, ri > ci), True: (ri <= ci, ri >= ci, ri < ci)}
    flags = tuple(reverse) if isinstance(reverse, (tuple, list)) else (reverse,) * len(q)
    incl, incl_t, strict = (tuple(masks[f][i] for f in flags) for i in range(3))
    gc_col = _each(lambda gr, mk: jnp.sum(jnp.where(mk, gr, 0.0), axis=1, keepdims=True), g_row, incl)
    gc_row = _each(lambda gc, mk: jnp.sum(jnp.where(mk, gc, 0.0), axis=0, keepdims=True), g_col, incl_t)
    g_tot = _each(lambda gr: jnp.sum(gr, axis=1, keepdims=True), g_row)
    decay = _each(lambda a, b, mk: jnp.where(mk, jnp.exp(jnp.where(mk, a - b, 0.0)), 0.0), gc_col, gc_row, incl)
    kb = _each(lambda a, b: a * b, k, beta)
    kk = _each(_mm_nt, kb, k)
    m = _each(lambda a, d, mk: jnp.where(mk, a * d, 0.0), kk, decay, strict)
    if t_known is None:
        eye = (ri == ci).astype(F32)
        p = _each(lambda a: -a, m)
        t = _each(lambda a: eye + a, p)
        for _ in range(5):
            p = _each(lambda a: _bdot(a, a, ((1,), (0,))), p)
            t = _each(lambda a, b: a + _bdot(a, b, ((1,), (0,))), t, p)
    else:
        t = _each(_inverse_given, m, t_known)
    e_gc = _each(jnp.exp, gc_col)
    u = _each(lambda t_, v_, b_: _mm_nn(t_, v_ * b_), t, v, beta)
    w = _each(lambda t_, kb_, e_: _mm_nn(t_, kb_ * e_), t, kb, e_gc)
    attn = _each(lambda q_, k_, d_: _mm_nt(q_, k_) * d_, q, k, decay)
    ws = _each(_mm_nn, w, state)
    v_new = _each(lambda a, b: a - b, u, ws)
    qs = _each(lambda q_, e_, s_: _mm_nn(q_ * e_, s_), q, e_gc, state)
    av = _each(_mm_nn, attn, v_new)
    o = _each(lambda a, b: a + b, qs, av)
    kv = _each(lambda k_, gt, gc, vn: _mm_tn(k_ * jnp.exp(gt - gc), vn), k, g_tot, gc_col, v_new)
    new_state = _each(lambda s_, gt, kv_: s_ * jnp.exp(gt) + kv_, state, g_tot, kv)
    return o, new_state, t


def _delta_lanes(reverse):
    return (N_GROUPS, 3 * N_GROUPS) if reverse else (0, 2 * N_GROUPS)


def _delta_fwd(q, k, v, gates, rows_f, rows_b, *, comm=None):
    s = q.shape[1]
    nc = s // DN_CHUNK
    heads = range(N_GROUPS)

    def body(qf_ref, kf_ref, vf_ref, gf_ref, rf_ref, qb_ref, kb_ref, vb_ref, gb_ref, rb_ref,
             of_ref, sf_ref, tf_ref, ob_ref, sb_ref, tb_ref, state):
        @pl.when(pl.program_id(0) == 0)
        def _():
            state[...] = jnp.zeros_like(state)

        qs, ks, vs, g_cols, g_rows, betas, flags = [], [], [], [], [], [], []
        for reverse, q_ref, k_ref, v_ref, gate, row_ref in ((False, qf_ref, kf_ref, vf_ref, gf_ref[...], rf_ref),
                                                            (True, qb_ref, kb_ref, vb_ref, gb_ref[...], rb_ref)):
            g_lane, b_lane = _delta_lanes(reverse)
            for j in heads:
                qs.append(q_ref[j])
                ks.append(k_ref[j])
                vs.append(v_ref[j])
                g_cols.append(gate[:, g_lane + j:g_lane + j + 1])
                g_rows.append(row_ref[0, j:j + 1, :])
                betas.append(gate[:, b_lane + j:b_lane + j + 1])
                flags.append(reverse)
        prev = tuple(state[n] for n in range(2 * N_GROUPS))
        o, new, t = _delta_chunk(tuple(qs), tuple(ks), tuple(vs), tuple(g_cols), tuple(g_rows), tuple(betas), prev, None, tuple(flags))
        for n in range(2 * N_GROUPS):
            o_ref, st_ref, t_ref = (of_ref, sf_ref, tf_ref) if n < N_GROUPS else (ob_ref, sb_ref, tb_ref)
            j = n % N_GROUPS
            st_ref[0, j] = prev[n]
            o_ref[j] = o[n]
            t_ref[0, j] = t[n]
            state[n] = new[n]

    def specs(chunk_of):
        hm = pl.BlockSpec((N_GROUPS, DN_CHUNK, HEAD), lambda c: (0, chunk_of(c), 0))
        ins = [hm, hm, hm, pl.BlockSpec((DN_CHUNK, LANE), lambda c: (chunk_of(c), 0)),
               pl.BlockSpec((1, N_GROUPS, DN_CHUNK), lambda c: (chunk_of(c), 0, 0))]
        outs = [hm, pl.BlockSpec((1, N_GROUPS, HEAD, HEAD), lambda c: (chunk_of(c), 0, 0, 0)),
                pl.BlockSpec((1, N_GROUPS, DN_CHUNK, DN_CHUNK), lambda c: (chunk_of(c), 0, 0, 0))]
        return ins, outs

    (ins_f, outs_f), (ins_b, outs_b) = specs(lambda c: c), specs(lambda c: nc - 1 - c)
    shapes = [jax.ShapeDtypeStruct((N_GROUPS, s, HEAD), F32), jax.ShapeDtypeStruct((nc, N_GROUPS, HEAD, HEAD), F32),
              jax.ShapeDtypeStruct((nc, N_GROUPS, DN_CHUNK, DN_CHUNK), F32)]
    results, sent = _call_with_comm(
        body, comm, name="delta_fwd" + ("" if comm is None else "_comm"), grid=(nc,),
        in_specs=ins_f + ins_b, out_specs=outs_f + outs_b, out_shape=shapes + shapes,
        scratch_shapes=[pltpu.VMEM((2 * N_GROUPS, HEAD, HEAD), F32)],
        args=(q, k, v, gates, rows_f, q, k, v, gates, rows_b), compiler_params=_params(("arbitrary",)))
    return results[:3], results[3:], sent


def _delta_bwd(q, k, v, gates, rows_f, rows_b, states_f, states_b, t_f, t_b, do, *, comm=None):
    s = q.shape[1]
    nc = s // DN_CHUNK
    heads = range(N_GROUPS)
    n_in, n_out = 8, 5

    def body(*refs):
        ins = (refs[:n_in], refs[n_in:2 * n_in])
        outs = (refs[2 * n_in:2 * n_in + n_out], refs[2 * n_in + n_out:2 * n_in + 2 * n_out])
        dstate = refs[-1]

        @pl.when(pl.program_id(0) == 0)
        def _():
            dstate[...] = jnp.zeros_like(dstate)

        qs, ks, vs, g_cols, g_rows, betas, sts, ts, dos, flags = ([] for _ in range(10))
        for reverse, (q_ref, k_ref, v_ref, gate_ref, row_ref, st_ref, t_ref, do_ref) in zip((False, True), ins):
            gate = gate_ref[...]
            g_lane, b_lane = _delta_lanes(reverse)
            for j in heads:
                qs.append(q_ref[j])
                ks.append(k_ref[j])
                vs.append(v_ref[j])
                g_cols.append(gate[:, g_lane + j:g_lane + j + 1])
                g_rows.append(row_ref[0, j:j + 1, :])
                betas.append(gate[:, b_lane + j:b_lane + j + 1])
                sts.append(st_ref[0, j])
                ts.append(t_ref[0, j])
                dos.append(do_ref[j])
                flags.append(reverse)
        t_known, flags = tuple(ts), tuple(flags)
        f = lambda q_, k_, v_, gc_, gr_, b_, s_: _delta_chunk(q_, k_, v_, gc_, gr_, b_, s_, t_known, flags)[:2]
        _, vjp = jax.vjp(f, tuple(qs), tuple(ks), tuple(vs), tuple(g_cols), tuple(g_rows), tuple(betas), tuple(sts))
        dq, dk, dv, dg_col, dg_row, dbeta, dprev = vjp((tuple(dos), tuple(dstate[n] for n in range(2 * N_GROUPS))))
        ids = lax.broadcasted_iota(jnp.int32, (DN_CHUNK, LANE), 1)
        for reverse, (dq_ref, dk_ref, dv_ref, dgate_ref, drow_ref) in zip((False, True), outs):
            g_lane, b_lane = _delta_lanes(reverse)
            dgate = jnp.zeros((DN_CHUNK, LANE), F32)
            for j in heads:
                n = j + (N_GROUPS if reverse else 0)
                dq_ref[j] = dq[n]
                dk_ref[j] = dk[n]
                dv_ref[j] = dv[n]
                dstate[n] = dprev[n]
                dgate = dgate + jnp.where(ids == g_lane + j, dg_col[n], 0.0) + jnp.where(ids == b_lane + j, dbeta[n], 0.0)
                drow_ref[0, j:j + 1, :] = dg_row[n]
            dgate_ref[...] = dgate

    def specs(chunk_of):
        hm = pl.BlockSpec((N_GROUPS, DN_CHUNK, HEAD), lambda c: (0, chunk_of(c), 0))
        gate = pl.BlockSpec((DN_CHUNK, LANE), lambda c: (chunk_of(c), 0))
        rows = pl.BlockSpec((1, N_GROUPS, DN_CHUNK), lambda c: (chunk_of(c), 0, 0))
        st = pl.BlockSpec((1, N_GROUPS, HEAD, HEAD), lambda c: (chunk_of(c), 0, 0, 0))
        ts = pl.BlockSpec((1, N_GROUPS, DN_CHUNK, DN_CHUNK), lambda c: (chunk_of(c), 0, 0, 0))
        return [hm, hm, hm, gate, rows, st, ts, hm], [hm, hm, hm, gate, rows]

    (ins_f, outs_f), (ins_b, outs_b) = specs(lambda c: nc - 1 - c), specs(lambda c: c)
    shapes = [jax.ShapeDtypeStruct((N_GROUPS, s, HEAD), F32)] * 3 + [jax.ShapeDtypeStruct((s, LANE), F32),
                                                                       jax.ShapeDtypeStruct((nc, N_GROUPS, DN_CHUNK), F32)]
    results, sent = _call_with_comm(
        body, comm, name="delta_bwd" + ("" if comm is None else "_comm"), grid=(nc,),
        in_specs=ins_f + ins_b, out_specs=outs_f + outs_b, out_shape=shapes + shapes,
        scratch_shapes=[pltpu.VMEM((2 * N_GROUPS, HEAD, HEAD), F32)],
        args=(q, k, v, gates, rows_f, states_f, t_f, do, q, k, v, gates, rows_b, states_b, t_b, do),
        compiler_params=_params(("arbitrary",)))
    return results[:n_out], results[n_out:], sent


ZB_BLOCK = 6


def _post_head(o_f, o_b, z, w):
    return _rms(o_f + o_b, w) * _silu(z)


def _post_fwd(o_f, o_b, proj, w, y, y_t, *, tile=256):
    s = proj.shape[0]
    tile = min(tile, s)

    def body(of_ref, ob_ref, z_ref, w_ref, _, __, y_ref, yt_ref):
        for h in range(N_GROUPS):
            lanes = pl.ds(h * HEAD, HEAD)
            y_h = _post_head(of_ref[h], ob_ref[h], z_ref[:, lanes], w_ref[...])
            y_ref[:, lanes] = y_h.astype(BF16)
            yt_ref[lanes, :] = y_h.T.astype(BF16)

    hm = pl.BlockSpec((N_GROUPS, tile, HEAD), lambda i: (0, i, 0))
    any_spec = pl.BlockSpec(memory_space=pl.ANY)
    return pl.pallas_call(
        body, name="post_fwd", grid=(s // tile,),
        in_specs=[hm, hm, pl.BlockSpec((tile, D_DN), lambda i: (i, ZB_BLOCK)), pl.BlockSpec((1, HEAD), lambda i: (0, 0)),
                  any_spec, any_spec],
        out_specs=[pl.BlockSpec((tile, D_DN), lambda i: (i, 1)), pl.BlockSpec((D_DN, tile), lambda i: (1, i))],
        out_shape=[jax.ShapeDtypeStruct(y.shape, BF16), jax.ShapeDtypeStruct(y_t.shape, BF16)],
        input_output_aliases={4: 0, 5: 1},
        compiler_params=_params(("arbitrary",)),
    )(o_f, o_b, proj, w, y, y_t)


def _post_bwd(o_f, o_b, proj, w, dy, *, tile=256):
    s = proj.shape[0]
    tile = min(tile, s)

    def body(of_ref, ob_ref, z_ref, w_ref, dy_ref, do_ref, dp_ref, dw_ref):
        @pl.when(pl.program_id(0) == 0)
        def _():
            dw_ref[...] = jnp.zeros_like(dw_ref)

        for h in range(N_GROUPS):
            lanes = pl.ds(h * HEAD, HEAD)
            _, vjp = jax.vjp(_post_head, of_ref[h], ob_ref[h], z_ref[:, lanes], w_ref[...])
            do, _, dz, dw = vjp(dy_ref[:, lanes])
            do_ref[h] = do
            dp_ref[:, lanes] = dz.astype(BF16)
            dw_ref[...] += dw

    hm = pl.BlockSpec((N_GROUPS, tile, HEAD), lambda i: (0, i, 0))
    zb = pl.BlockSpec((tile, D_DN), lambda i: (i, ZB_BLOCK))
    vec = pl.BlockSpec((1, HEAD), lambda i: (0, 0))
    return pl.pallas_call(
        body, name="post_bwd", grid=(s // tile,),
        in_specs=[hm, hm, zb, vec, pl.BlockSpec((tile, D_DN), lambda i: (i, 1))], out_specs=[hm, zb, vec],
        out_shape=[jax.ShapeDtypeStruct((N_GROUPS, s, HEAD), F32), jax.ShapeDtypeStruct((s, D_MAIN), BF16),
                   jax.ShapeDtypeStruct((1, HEAD), F32)],
        compiler_params=_params(("arbitrary",)),
    )(o_f, o_b, proj, w, dy)


def _my_index():
    return 4 * lax.axis_index("x") + 2 * lax.axis_index("y") + lax.axis_index("c")


def _peer(k):
    x, y, c = lax.axis_index("x"), lax.axis_index("y"), lax.axis_index("c")
    px, py, pc = x ^ ((k >> 2) & 1), y ^ ((k >> 1) & 1), c ^ (k & 1)
    return (px, py, pc), 4 * px + 2 * py + pc


class _Comm:
    def __init__(self, inputs, outputs, copies):
        self.inputs, self.outputs, self.copies = list(inputs), list(outputs), list(copies)
        self.kept = [o for o in self.outputs if not isinstance(o, jax.ShapeDtypeStruct)]

    def out_shapes(self):
        return [jax.ShapeDtypeStruct(o.shape, o.dtype) for o in self.outputs]

    def aliases(self, first_kept_input, first_output):
        kept_at = [i for i, o in enumerate(self.outputs) if not isinstance(o, jax.ShapeDtypeStruct)]
        return {first_kept_input + n: first_output + i for n, i in enumerate(kept_at)}

    def scratch(self):
        n = len(self.copies)
        return [pltpu.SemaphoreType.DMA((n, N_DEV - 1)), pltpu.SemaphoreType.DMA((n, N_DEV - 1)), pltpu.SemaphoreType.DMA((n,))]

    def _descriptors(self, in_refs, out_refs, sems, arrivals):
        send_sems, recv_sems, local_sems = sems
        me = _my_index()
        local, remote = [], []
        for k in range(N_DEV):
            peer, peer_idx = _peer(k)
            for a, (i_in, i_out, src_of, dst_of) in enumerate(self.copies):
                if k == 0:
                    if not arrivals:
                        local.append(pltpu.make_async_copy(src_of(in_refs[i_in], me), dst_of(out_refs[i_out], me), local_sems.at[a]))
                    continue
                receiver, sender = (me, peer_idx) if arrivals else (peer_idx, me)
                remote.append(pltpu.make_async_remote_copy(
                    src_ref=src_of(in_refs[i_in], receiver), dst_ref=dst_of(out_refs[i_out], sender),
                    send_sem=send_sems.at[a, k - 1], recv_sem=recv_sems.at[a, k - 1], device_id=peer,
                    device_id_type=pl.DeviceIdType.MESH))
        return local, remote

    def run_two_level(self, in_refs, out_refs, sems):
        send_sems, recv_sems, local_sems = sems
        me = _my_index()
        sibling, _ = _peer(1)

        def remote(a, k, src, block_owner, to):
            _, i_out, _, dst_of = self.copies[a]
            return pltpu.make_async_remote_copy(
                src_ref=src, dst_ref=dst_of(out_refs[i_out], block_owner), send_sem=send_sems.at[a, k - 1],
                recv_sem=recv_sems.at[a, k - 1], device_id=to, device_id_type=pl.DeviceIdType.MESH)

        sources = [src_of(in_refs[i_in], me) for i_in, _, src_of, _ in self.copies]
        n = range(len(self.copies))
        local = [pltpu.make_async_copy(sources[a], self.copies[a][3](out_refs[self.copies[a][1]], me), local_sems.at[a]) for a in n]
        first = [remote(a, k, sources[a], me, _peer(k)[0]) for k in (1, 2, 4, 6) for a in n]
        for cp in local + first:
            cp.start()
        passed = []
        for k in (2, 4, 6):
            peer, peer_idx = _peer(k)
            for a in n:
                remote(a, k, sources[a], peer_idx, peer).wait_recv()
                landed = self.copies[a][3](out_refs[self.copies[a][1]], peer_idx)
                passed.append(remote(a, k + 1, landed, peer_idx, sibling))
                passed[-1].start()
        for k in (1, 3, 5, 7):
            peer, peer_idx = _peer(k)
            for a in n:
                remote(a, k, sources[a], peer_idx, peer).wait_recv()
        for cp in first + passed:
            cp.wait_send()
        for cp in local:
            cp.wait()

    def start(self, in_refs, out_refs, sems):
        local, sends = self._descriptors(in_refs, out_refs, sems, arrivals=False)
        for cp in local + sends:
            cp.start()

    def wait(self, in_refs, out_refs, sems):
        _, arriving = self._descriptors(in_refs, out_refs, sems, arrivals=True)
        for cp in arriving:
            cp.wait_recv()
        local, sends = self._descriptors(in_refs, out_refs, sems, arrivals=False)
        for cp in sends:
            cp.wait_send()
        for cp in local:
            cp.wait()


def _call_with_comm(body, comm, *, name, grid, in_specs, out_specs, out_shape, scratch_shapes, args, compiler_params):
    if comm is None:
        return pl.pallas_call(body, name=name, grid=grid, in_specs=in_specs, out_specs=out_specs, out_shape=out_shape,
                              scratch_shapes=scratch_shapes, compiler_params=compiler_params)(*args), []
    n_in, n_out, n_scratch = len(in_specs), len(out_specs), len(scratch_shapes)
    c_in, c_kept, c_out = len(comm.inputs), len(comm.kept), len(comm.outputs)

    def at_step(position):
        here = [pl.program_id(d) == p for d, p in enumerate(position)]
        return here[0] if len(here) == 1 else functools.reduce(jnp.logical_and, here)

    def wrapped(*refs):
        main_in, comm_in = refs[:n_in], refs[n_in:n_in + c_in]
        o = n_in + c_in + c_kept
        main_out, comm_out = refs[o:o + n_out], refs[o + n_out:o + n_out + c_out]
        s = o + n_out + c_out
        main_scratch, sems = refs[s:s + n_scratch], refs[s + n_scratch:]

        @pl.when(at_step([0] * len(grid)))
        def _():
            comm.start(comm_in, comm_out, sems)

        body(*main_in, *main_out, *main_scratch)

        @pl.when(at_step([g - 1 for g in grid]))
        def _():
            comm.wait(comm_in, comm_out, sems)

    any_spec = pl.BlockSpec(memory_space=pl.ANY)
    results = pl.pallas_call(
        wrapped, name=name, grid=grid, in_specs=list(in_specs) + [any_spec] * (c_in + c_kept),
        out_specs=list(out_specs) + [any_spec] * c_out, out_shape=list(out_shape) + comm.out_shapes(),
        scratch_shapes=list(scratch_shapes) + comm.scratch(), input_output_aliases=comm.aliases(n_in + c_in, n_out),
        compiler_params=compiler_params,
    )(*args, *comm.inputs, *comm.kept)
    return results[:n_out], results[n_out:]


def _exchange(name, comm, two_level=False):
    c_in, c_kept, c_out = len(comm.inputs), len(comm.kept), len(comm.outputs)

    def body(*refs):
        in_refs, out_refs, sems = refs[:c_in], refs[c_in + c_kept:c_in + c_kept + c_out], refs[c_in + c_kept + c_out:]
        if two_level:
            comm.run_two_level(in_refs, out_refs, sems)
        else:
            comm.start(in_refs, out_refs, sems)
            comm.wait(in_refs, out_refs, sems)

    any_spec = pl.BlockSpec(memory_space=pl.ANY)
    return pl.pallas_call(
        body, name=name, in_specs=[any_spec] * (c_in + c_kept), out_specs=[any_spec] * c_out, out_shape=comm.out_shapes(),
        scratch_shapes=comm.scratch(), input_output_aliases=comm.aliases(c_in, 0),
    )(*comm.inputs, *comm.kept)


def _whole(ref, _):
    return ref


def _slot(ref, dev):
    return ref.at[dev]


def _row_block(size, axis):
    def pick(ref, dev):
        start = pl.multiple_of(dev * size, size) if size % SUBLANES == 0 else dev * size
        return ref.at[tuple([slice(None)] * axis + [pl.ds(start, size)])]
    return pick


def _blocked(shape, block):
    nd = len(block)
    grid = tuple(shape[d] // block[d] for d in range(nd) if block[d] != shape[d])
    return grid, (lambda *i: tuple(i) + (0,) * (nd - len(grid)))


def _sum_parts(name, parts, block, comm=None):
    n_parts = parts.shape[0]
    grid, idx = _blocked(parts.shape[1:], block)

    def body(p_ref, g_ref):
        g = p_ref[0].astype(F32)
        for j in range(1, n_parts):
            g = g + p_ref[j].astype(F32)
        g_ref[...] = g

    (total,), sent = _call_with_comm(
        body, comm, name=name, grid=grid, in_specs=[pl.BlockSpec((n_parts,) + tuple(block), lambda *i: (0,) + idx(*i))],
        out_specs=[pl.BlockSpec(block, idx)], out_shape=[jax.ShapeDtypeStruct(parts.shape[1:], F32)], scratch_shapes=[],
        args=(parts,), compiler_params=_params(("arbitrary",) * len(grid)))
    return total, sent


def _adamw(name, parts, w, m, v, block):
    n_parts = parts.shape[0]
    nd = len(block)
    grid, _ = _blocked(w.shape, block)
    lead = len(grid)

    def body(p_ref, w_ref, m_ref, v_ref, g_ref, d_ref, mo_ref, vo_ref):
        g = p_ref[0].astype(F32)
        for j in range(1, n_parts):
            g = g + p_ref[j].astype(F32)
        m_new = ADAM_B1 * m_ref[...] + (1.0 - ADAM_B1) * g
        v_new = ADAM_B2 * v_ref[...] + (1.0 - ADAM_B2) * jnp.square(g)
        m_hat = m_new / np.float32(1.0 - ADAM_B1 ** ADAM_STEP)
        v_hat = v_new / np.float32(1.0 - ADAM_B2 ** ADAM_STEP)
        g_ref[...] = g
        d_ref[...] = -ADAM_LR * (m_hat / (jnp.sqrt(v_hat) + ADAM_EPS) + ADAM_WD * w_ref[...])
        mo_ref[...] = m_new
        vo_ref[...] = v_new

    idx = lambda *i: tuple(i) + (0,) * (nd - lead)
    spec = pl.BlockSpec(block, idx)
    pspec = pl.BlockSpec((n_parts,) + tuple(block), lambda *i: (0,) + idx(*i))
    n_blk = int(np.prod(block[:-1])) * (-(-block[-1] // LANE) * LANE)
    return pl.pallas_call(
        body, name=name, grid=grid, in_specs=[pspec, spec, spec, spec], out_specs=[spec] * 4,
        out_shape=[jax.ShapeDtypeStruct(w.shape, F32)] * 4,
        compiler_params=_params(("arbitrary",) * lead, 2 * (n_parts + 7) * n_blk * 4 + VMEM_MARGIN),
    )(parts, w, m, v)


def _to_flat(pieces, row_multiple):
    flat = jnp.concatenate([p.reshape(-1) for p in pieces])
    rows = -(-flat.shape[0] // FLAT_W)
    rows = -(-rows // row_multiple) * row_multiple
    return jnp.pad(flat, (0, rows * FLAT_W - flat.shape[0])).reshape(rows, FLAT_W)


def _lane_row(*pieces):
    flat = jnp.concatenate([p.reshape(-1) for p in pieces])
    return jnp.pad(flat, (0, LANE - flat.shape[0])).reshape(1, LANE)


def _rows_view(a):
    return a.reshape(a.shape[:-1] + (SUBLANES, LANE))


def _in_shard_t(a):
    return _rows_view(jnp.swapaxes(a, 1, 2))


def kernel(x, norm_w, w_in, sgu_ln_g, sgu_ln_b, sgu_w, sgu_b, conv_w, a_log_f, a_log_b, dt_bias_f, dt_bias_b, dn_norm_w, w_out, final_norm_w, loss_target, m_norm_w, m_w_in, m_sgu_ln_g, m_sgu_ln_b, m_sgu_w, m_sgu_b, m_conv_w, m_a_log_f, m_a_log_b, m_dt_bias_f, m_dt_bias_b, m_dn_norm_w, m_w_out, m_final_norm_w, v_norm_w, v_w_in, v_sgu_ln_g, v_sgu_ln_b, v_sgu_w, v_sgu_b, v_conv_w, v_a_log_f, v_a_log_b, v_dt_bias_f, v_dt_bias_b, v_dn_norm_w, v_w_out, v_final_norm_w):
    weights = dict(norm_w=norm_w, w_in=w_in, sgu_ln_g=sgu_ln_g, sgu_ln_b=sgu_ln_b, sgu_w=sgu_w, sgu_b=sgu_b, conv_w=conv_w,
                   a_log_f=a_log_f, a_log_b=a_log_b, dt_bias_f=dt_bias_f, dt_bias_b=dt_bias_b, dn_norm_w=dn_norm_w,
                   w_out=w_out, final_norm_w=final_norm_w)
    m_in = dict(norm_w=m_norm_w, w_in=m_w_in, sgu_ln_g=m_sgu_ln_g, sgu_ln_b=m_sgu_ln_b, sgu_w=m_sgu_w, sgu_b=m_sgu_b,
                conv_w=m_conv_w, a_log_f=m_a_log_f, a_log_b=m_a_log_b, dt_bias_f=m_dt_bias_f, dt_bias_b=m_dt_bias_b,
                dn_norm_w=m_dn_norm_w, w_out=m_w_out, final_norm_w=m_final_norm_w)
    v_in = dict(norm_w=v_norm_w, w_in=v_w_in, sgu_ln_g=v_sgu_ln_g, sgu_ln_b=v_sgu_ln_b, sgu_w=v_sgu_w, sgu_b=v_sgu_b,
                conv_w=v_conv_w, a_log_f=v_a_log_f, a_log_b=v_a_log_b, dt_bias_f=v_dt_bias_f, dt_bias_b=v_dt_bias_b,
                dn_norm_w=v_dn_norm_w, w_out=v_w_out, final_norm_w=v_final_norm_w)
    s = x.shape[1]
    nc = s // DN_CHUNK
    depth = w_in.shape[0]
    in_shard, out_shard, conv_shard = w_in.shape[2], w_out.shape[1], conv_w.shape[2]
    x0 = x.reshape(s, D_MODEL)
    target = loss_target.reshape(s, D_MODEL)

    w_in_shard = _in_shard_t(w_in.astype(BF16))
    w_out_shard = w_out.astype(BF16)

    gathered_in_shape = jax.ShapeDtypeStruct((N_DEV * in_shard, SUBLANES, LANE), BF16)
    gathered_out_shape = jax.ShapeDtypeStruct((D_MODEL, D_MODEL), BF16)
    gathered_in, gathered_out, conv_all = _exchange("gather_first", _Comm(
        [w_in_shard[0], w_out_shard[0], conv_w],
        [gathered_in_shape, gathered_out_shape, jax.ShapeDtypeStruct((N_DEV,) + conv_w.shape, F32)],
        [(0, 0, _whole, _row_block(in_shard, 0)), (1, 1, _whole, _row_block(out_shard, 0)), (2, 2, _whole, _slot)]), two_level=True)
    gather_plan = lambda l: _Comm([w_in_shard[l], w_out_shard[l]], [gathered_in_shape, gathered_out_shape],
                                  [(0, 0, _whole, _row_block(in_shard, 0)), (1, 1, _whole, _row_block(out_shard, 0))])
    conv_full = conv_all.transpose(1, 2, 0, 3).reshape(depth, CONV_W, 3 * D_DN)
    conv_full = jnp.pad(conv_full, ((0, 0), (0, HALO - CONV_W), (0, 0)))
    sgu_b_t = jnp.swapaxes(sgu_b, 1, 2)

    def rows_of(gates, first_lane):
        return gates[:, first_lane:first_lane + N_GROUPS].reshape(nc, DN_CHUNK, N_GROUPS).transpose(0, 2, 1)

    saved = []
    xl = x0
    for l in range(depth):
        w_in_t = gathered_in.reshape(D_IN, D_MODEL)
        w_gate_t = jnp.pad(w_in_t[D_MAIN:], ((0, LANE - N_GATE), (0, 0)))
        w_out_l = gathered_out
        nw = norm_w[l].reshape(1, D_MODEL)
        h, h_t = _rms_fwd(xl, nw)
        proj = _matmul([(_lhs(h, tm=MM_WIDE), _rhs(w_in_t, MM_WIDE, transposed=True))], s, D_MAIN, name="in_proj", trans_b=True, tn=MM_WIDE)
        p_gate = _matmul([(_lhs(h), _rhs(w_gate_t, LANE, transposed=True))], s, LANE, name="in_proj_gate", trans_b=True, tn=LANE)
        ln_g, ln_b = sgu_ln_g[l].reshape(1, D_SGU), sgu_ln_b[l].reshape(1, D_SGU)
        y, y_t = _sgu_fwd(proj, ln_g, ln_b, sgu_w[l], sgu_b_t[l])
        a_log = _lane_row(a_log_f[l], a_log_b[l])
        dt_bias = _lane_row(dt_bias_f[l], dt_bias_b[l])
        q, k, v, gates = _prep_fwd(proj, conv_full[l], p_gate, a_log, dt_bias)
        rows_f, rows_b = rows_of(gates, 0), rows_of(gates, N_GROUPS)
        more = l + 1 < depth
        (o_f, st_f, t_f), (o_b, st_b, t_b), next_weights = _delta_fwd(q, k, v, gates, rows_f, rows_b,
                                                                     comm=gather_plan(l + 1) if more else None)
        dn_w = dn_norm_w[l].reshape(1, HEAD)
        y, y_t = _post_fwd(o_f, o_b, proj, dn_w, y, y_t)
        x_next = _matmul([(_lhs(y, tm=MM_WIDE), _rhs(w_out_l, MM_WIDE))], s, D_MODEL, name="out_proj", add=xl, tn=MM_WIDE)
        saved.append(dict(x=xl, nw=nw, h_t=h_t, proj=proj, p_gate=p_gate, ln_g=ln_g, ln_b=ln_b, a_log=a_log, dt_bias=dt_bias,
                          q=q, k=k, v=v, gates=gates, rows_f=rows_f, rows_b=rows_b, o_f=o_f, o_b=o_b, st_f=st_f, st_b=st_b,
                          t_f=t_f, t_b=t_b, dn_w=dn_w, y_t=y_t, w_in_t=w_in_t, w_gate_t=w_gate_t, w_out=w_out_l))
        xl = x_next
        if more:
            gathered_in, gathered_out = next_weights

    loss_part, dx, dx_bf, d_final = _final_loss(xl, final_norm_w.reshape(1, D_MODEL), target)
    loss = lax.psum(loss_part[0, 0], MESH_AXES)

    t_rows = -(-D_IN // LANE) * LANE
    layer_small = [n for n in SMALL if n != "final_norm_w"]
    n_layer_small = sum(int(np.prod(weights[n].shape[1:])) for n in layer_small) + D_MODEL
    small_rows = -(-n_layer_small // (FLAT_W * HALO)) * HALO

    def small_segments(values):
        top = values["final_norm_w"]
        return jnp.concatenate([_to_flat([values[n][l] for n in layer_small] + [top if l == depth - 1 else jnp.zeros_like(top)], HALO)
                                for l in range(depth)], axis=0)

    def small_to_send(layer_grads, top_grad):
        conv = layer_grads["conv_w"].reshape(CONV_W, N_DEV, conv_shard).transpose(1, 0, 2).reshape(N_DEV, -1)
        rest = jnp.concatenate([layer_grads[n].reshape(-1) for n in layer_small[1:]] + [top_grad])
        flat = jnp.concatenate([conv, jnp.broadcast_to(rest, (N_DEV, rest.shape[0]))], axis=1)
        return jnp.pad(flat, ((0, 0), (0, small_rows * FLAT_W - flat.shape[1]))).reshape(N_DEV, small_rows, FLAT_W)

    layer_slot = lambda l: (lambda ref, dev: ref.at[dev, l])
    small_slot = lambda l: (lambda ref, dev: ref.at[dev, pl.ds(l * small_rows, small_rows)])
    parts = [jax.ShapeDtypeStruct((N_DEV, depth, in_shard, SUBLANES, LANE), BF16),
             jax.ShapeDtypeStruct((N_DEV, depth, out_shard, D_MODEL), BF16),
             jax.ShapeDtypeStruct((N_DEV, depth * small_rows, FLAT_W), F32)]

    def exchange_plan(items):
        source = (_row_block(in_shard, 0), _row_block(out_shard, 0), _slot)
        target = (layer_slot, layer_slot, small_slot)
        return _Comm([a for _, _, a in items], parts,
                     [(i, which, source[which], target[which](layer)) for i, (which, layer, _) in enumerate(items)])

    pending = []
    for l in reversed(range(depth)):
        sv = saved[l]
        dy = _matmul([(_lhs(dx_bf, tm=MM_WIDE), _rhs(sv["w_out"], MM_WIDE, transposed=True))], s, D_MODEL, name="out_proj_dy", trans_b=True,
                     tn=MM_WIDE)
        g_out = _matmul([(_lhs(sv["y_t"]), _rhs(dx_bf, MM_TILE))], D_MODEL, D_MODEL, name="out_proj_dw", out_dtype=BF16)
        do, dp, d_dn = _post_bwd(sv["o_f"], sv["o_b"], sv["proj"], sv["dn_w"], dy)
        (*dqkv_f, dgate_f, drows_f), (*dqkv_b, dgate_b, drows_b), exchanged = _delta_bwd(
            sv["q"], sv["k"], sv["v"], sv["gates"], sv["rows_f"], sv["rows_b"], sv["st_f"], sv["st_b"], sv["t_f"], sv["t_b"], do,
            comm=exchange_plan([(1, l, g_out)] + pending))
        parts = list(exchanged)
        drows = jnp.concatenate([drows_f.transpose(0, 2, 1).reshape(s, N_GROUPS), drows_b.transpose(0, 2, 1).reshape(s, N_GROUPS)], axis=1)
        drows = jnp.pad(drows, ((0, 0), (0, LANE - 2 * N_GROUPS)))
        dp, d_gate, d_conv, d_alog, d_dt = _prep_bwd(sv["proj"], conv_full[l], sv["p_gate"], sv["a_log"], sv["dt_bias"],
                                                     dqkv_f, dqkv_b, (dgate_f, dgate_b, drows), dp)
        dp, d_lg, d_lb, d_sw, d_sbt = _sgu_bwd(sv["proj"], sv["ln_g"], sv["ln_b"], sgu_w[l], sgu_b_t[l], dy, dp)
        h_t = sv["h_t"]
        g_in_t = _matmul([(_lhs(h_t), _rhs(dp, MM_TILE))], D_MODEL, D_MAIN, name="in_proj_dw", out_dtype=BF16, out_t_rows=t_rows)
        g_in_t = _matmul([(_lhs(h_t), _rhs(d_gate, LANE))], D_MODEL, LANE, name="in_proj_dw_gate", out_dtype=BF16, tn=LANE,
                         transpose_into=g_in_t, out_row_block=D_MAIN // LANE)
        g_in_rows = _rows_view(g_in_t)
        dh_pairs = [(_lhs(dp), _rhs(sv["w_in_t"], MM_TILE, k=D_MAIN)), (_lhs(d_gate), _rhs(sv["w_gate_t"], MM_TILE))]
        if l > 0:
            dh = _matmul(dh_pairs, s, D_MODEL, name="in_proj_dh")
        else:
            dh, exchanged = _matmul(dh_pairs, s, D_MODEL, name="in_proj_dh_comm", comm=exchange_plan([(0, l, g_in_rows)]))
            parts = list(exchanged)
        dx, dx_bf, d_nw = _rms_bwd(sv["x"], sv["nw"], dh, dx)
        layer_grads = dict(
            conv_w=d_conv[:CONV_W], norm_w=d_nw, sgu_ln_g=d_lg, sgu_ln_b=d_lb, sgu_w=d_sw, sgu_b=d_sbt.T,
            a_log_f=d_alog[0, :N_GROUPS], a_log_b=d_alog[0, N_GROUPS:2 * N_GROUPS],
            dt_bias_f=d_dt[0, :N_GROUPS], dt_bias_b=d_dt[0, N_GROUPS:2 * N_GROUPS], dn_norm_w=d_dn)
        small = small_to_send(layer_grads, d_final.reshape(D_MODEL) if l == depth - 1 else jnp.zeros((D_MODEL,), F32))
        pending = [(2, l, small)] + ([(0, l, g_in_rows)] if l > 0 else [])
    grad_x = dx.reshape(1, s, D_MODEL)

    parts_in, parts_out, parts_small = parts
    in_rows = 100 if in_shard % 100 == 0 else in_shard
    (_, bottom, bottom_small), = pending
    g_in_t, (parts_small,) = _sum_parts("sum_in_comm", parts_in, (1, in_rows, SUBLANES, LANE), comm=_Comm(
        [bottom_small], [parts_small], [(0, 0, _slot, small_slot(bottom))]))
    g_in = jnp.swapaxes(g_in_t.reshape(depth, in_shard, D_MODEL), 1, 2)
    res_in = _adamw("adamw_in", g_in[None], w_in, m_w_in, v_w_in, (1, min(256, D_MODEL), in_shard))
    res_out = _adamw("adamw_out", parts_out, w_out, m_w_out, v_w_out, (1, min(128, out_shard), D_MODEL))
    res_small = _adamw("adamw_small", parts_small, small_segments(weights), small_segments(m_in), small_segments(v_in),
                       (small_rows, FLAT_W))
    outs = [dict(w_in=a, w_out=b) for a, b in zip(res_in, res_out)]
    for o, d in zip(res_small, outs):
        by_layer, off = o.reshape(depth, small_rows * FLAT_W), 0
        for n in layer_small:
            size = int(np.prod(weights[n].shape[1:]))
            d[n] = by_layer[:, off:off + size].reshape(weights[n].shape)
            off += size
        d["final_norm_w"] = by_layer[depth - 1, off:off + D_MODEL]
    g_out, delta_out, m_out, v_out = outs
    return (loss, grad_x, *[g_out[n] for n in WEIGHTS], *[delta_out[n] for n in WEIGHTS],
            *[m_out[n] for n in WEIGHTS], *[v_out[n] for n in WEIGHTS])
```

```python
import functools

import numpy as np
import jax
import jax.numpy as jnp
from jax import lax
from jax.experimental import pallas as pl
from jax.experimental.pallas import tpu as pltpu

F32 = jnp.float32
BF16 = jnp.bfloat16

N_DEV = 8
D_MODEL = 2048
D_SGU = 1024
D_DN = 1024
N_GROUPS = 8
HEAD = 128
SGU_CHUNK = 128
DN_CHUNK = 64
CONV_W = 5
N_GATE = 32
D_MAIN = 3 * D_SGU + 4 * D_DN
D_IN = D_MAIN + N_GATE
LANE = 128
SUBLANES = 16
HALO = 8
EPS = 1e-6
ADAM_LR, ADAM_B1, ADAM_B2, ADAM_EPS, ADAM_WD, ADAM_STEP = 0.001, 0.9, 0.999, 1e-08, 0.01, 10
FLAT_W = 1024
MM_TILE = 512
MM_WIDE = 1024
VMEM_MARGIN = 8 << 20

MESH_AXES = ("x", "y", "c")
WEIGHTS = ("norm_w", "w_in", "sgu_ln_g", "sgu_ln_b", "sgu_w", "sgu_b", "conv_w", "a_log_f", "a_log_b",
           "dt_bias_f", "dt_bias_b", "dn_norm_w", "w_out", "final_norm_w")
SMALL = ("conv_w",) + tuple(n for n in WEIGHTS if n not in ("w_in", "w_out", "conv_w"))


def _bdot(a, b, dims):
    return lax.dot_general(a.astype(BF16), b.astype(BF16), (dims, ((), ())), preferred_element_type=F32)


@jax.custom_vjp
def _mm_nn(a, b):
    return _bdot(a, b, ((1,), (0,)))


def _mm_nn_fwd(a, b):
    return _mm_nn(a, b), (a, b)


def _mm_nn_bwd(res, ct):
    a, b = res
    return _bdot(ct, b, ((1,), (1,))), _bdot(a, ct, ((0,), (0,)))


_mm_nn.defvjp(_mm_nn_fwd, _mm_nn_bwd)


@jax.custom_vjp
def _mm_nt(a, b):
    return _bdot(a, b, ((1,), (1,)))


def _mm_nt_fwd(a, b):
    return _mm_nt(a, b), (a, b)


def _mm_nt_bwd(res, ct):
    a, b = res
    return _bdot(ct, b, ((1,), (0,))), _bdot(ct, a, ((0,), (0,)))


_mm_nt.defvjp(_mm_nt_fwd, _mm_nt_bwd)


@jax.custom_vjp
def _inverse_given(m, t):
    return t


def _inverse_given_fwd(m, t):
    return t, t


def _inverse_given_bwd(t, ct):
    dm = -_bdot(t, _bdot(ct, t, ((1,), (1,))), ((0,), (0,)))
    return dm, jnp.zeros_like(t)


_inverse_given.defvjp(_inverse_given_fwd, _inverse_given_bwd)


@jax.custom_vjp
def _split_lanes(x):
    return tuple(x[:, i * LANE:(i + 1) * LANE] for i in range(x.shape[1] // LANE))


def _split_lanes_fwd(x):
    return _split_lanes(x), None


def _split_lanes_bwd(_, cts):
    return (jnp.concatenate(cts, axis=1),)


_split_lanes.defvjp(_split_lanes_fwd, _split_lanes_bwd)


@functools.partial(jax.custom_vjp, nondiff_argnums=(1,))
def _split_rows(x, n):
    return x[:n], x[n:]


def _split_rows_fwd(x, n):
    return _split_rows(x, n), None


def _split_rows_bwd(n, _, cts):
    return (jnp.concatenate(cts, axis=0),)


_split_rows.defvjp(_split_rows_fwd, _split_rows_bwd)


def _silu(t):
    return t * jax.nn.sigmoid(t)


def _gelu(t):
    return 0.5 * t * (1.0 + lax.erf(t * np.float32(0.7071067811865476)))


def _rms(x, w):
    return x * lax.rsqrt(jnp.mean(x * x, axis=-1, keepdims=True) + EPS) * w


def _params(sem, vmem_bytes=None):
    kw = dict(dimension_semantics=sem)
    if vmem_bytes is not None:
        kw["vmem_limit_bytes"] = int(vmem_bytes)
    return pltpu.CompilerParams(**kw)


def _lhs(a, k=None, k_block=0, tm=MM_TILE):
    k = a.shape[1] if k is None else k
    tm = min(tm, a.shape[0])
    return a, pl.BlockSpec((tm, k), lambda i, j: (i, k_block)), tm * k * a.dtype.itemsize


def _rhs(b, tn, *, layer=None, transposed=False, k=None, k_block=0, n_offset=0):
    lead = () if layer is None else (layer,)
    none = () if layer is None else (None,)
    if transposed:
        k = b.shape[-1]
        spec = pl.BlockSpec(none + (tn, k), lambda i, j: lead + (j + n_offset, 0))
    else:
        k = b.shape[-2] if k is None else k
        spec = pl.BlockSpec(none + (k, tn), lambda i, j: lead + (k_block, j))
    return b, spec, tn * k * b.dtype.itemsize


def _matmul(pairs, m, n, *, name, trans_b=False, add=None, out_dtype=F32, tn=MM_TILE, out_t_rows=None, transpose_into=None,
            out_row_block=0, comm=None):
    tm = pairs[0][0][1].block_shape[0]
    assert m % tm == 0 and n % tn == 0 and all(a_spec.block_shape[0] == tm for (_, a_spec, _), _ in pairs)
    n_pairs = len(pairs)
    contract = ((1,), (1,)) if trans_b else ((1,), (0,))
    transposed_out = out_t_rows is not None or transpose_into is not None

    def body(*refs):
        o_ref = refs[-1]
        acc = None
        for i in range(n_pairs):
            d = lax.dot_general(refs[2 * i][...].astype(BF16), refs[2 * i + 1][...].astype(BF16),
                                (contract, ((), ())), preferred_element_type=F32)
            acc = d if acc is None else acc + d
        if add is not None:
            acc = acc + refs[2 * n_pairs][...]
        o_ref[...] = (acc.T if transposed_out else acc).astype(out_dtype)

    in_specs, args, vmem, aliases = [], [], 0, {}
    for (a, a_spec, a_bytes), (b, b_spec, b_bytes) in pairs:
        in_specs += [a_spec, b_spec]
        args += [a, b]
        vmem += 2 * (a_bytes + b_bytes)
    if add is not None:
        in_specs.append(pl.BlockSpec((tm, tn), lambda i, j: (i, j)))
        args.append(add)
        vmem += 2 * tm * tn * 4
    vmem += 2 * tm * tn * jnp.dtype(out_dtype).itemsize + 3 * tm * tn * 4
    if transposed_out:
        out_spec = pl.BlockSpec((tn, tm), lambda i, j: (j + out_row_block, i))
        if transpose_into is not None:
            in_specs.append(pl.BlockSpec(memory_space=pl.ANY))
            args.append(transpose_into)
            aliases = {len(args) - 1: 0}
            out_shape = jax.ShapeDtypeStruct(transpose_into.shape, out_dtype)
        else:
            out_shape = jax.ShapeDtypeStruct((out_t_rows, m), out_dtype)
    else:
        out_spec = pl.BlockSpec((tm, tn), lambda i, j: (i, j))
        out_shape = jax.ShapeDtypeStruct((m, n), out_dtype)
    if comm is not None:
        assert not aliases
        (out,), sent = _call_with_comm(
            body, comm, name=name, grid=(m // tm, n // tn), in_specs=in_specs, out_specs=[out_spec], out_shape=[out_shape],
            scratch_shapes=[], args=args, compiler_params=_params(("arbitrary", "arbitrary"), vmem + VMEM_MARGIN))
        return out, sent
    return pl.pallas_call(
        body, name=name, grid=(m // tm, n // tn), in_specs=in_specs, out_specs=out_spec, out_shape=out_shape,
        input_output_aliases=aliases, compiler_params=_params(("parallel", "arbitrary"), vmem + VMEM_MARGIN),
    )(*args)


def _rms_fwd(x, w, *, tile=512):
    s, d = x.shape
    tile = min(tile, s)

    def body(x_ref, w_ref, h_ref, ht_ref):
        h = _rms(x_ref[...], w_ref[...])
        h_ref[...] = h.astype(BF16)
        ht_ref[...] = h.T.astype(BF16)

    return pl.pallas_call(
        body, name="rms_fwd", grid=(s // tile,),
        in_specs=[pl.BlockSpec((tile, d), lambda i: (i, 0)), pl.BlockSpec((1, d), lambda i: (0, 0))],
        out_specs=[pl.BlockSpec((tile, d), lambda i: (i, 0)), pl.BlockSpec((d, tile), lambda i: (0, i))],
        out_shape=[jax.ShapeDtypeStruct((s, d), BF16), jax.ShapeDtypeStruct((d, s), BF16)],
        compiler_params=_params(("arbitrary",)),
    )(x, w)


def _rms_bwd(x, w, dh, dx_out, *, tile=256):
    s, d = x.shape
    tile = min(tile, s)

    def body(x_ref, w_ref, dh_ref, dxo_ref, dx_ref, dxb_ref, dw_ref):
        _, vjp = jax.vjp(_rms, x_ref[...], w_ref[...])
        dxv, dwv = vjp(dh_ref[...])
        dx = dxo_ref[...] + dxv
        dx_ref[...] = dx
        dxb_ref[...] = dx.astype(BF16)

        @pl.when(pl.program_id(0) == 0)
        def _():
            dw_ref[...] = jnp.zeros_like(dw_ref)

        dw_ref[...] += dwv

    row = pl.BlockSpec((tile, d), lambda i: (i, 0))
    vec = pl.BlockSpec((1, d), lambda i: (0, 0))
    return pl.pallas_call(
        body, name="rms_bwd", grid=(s // tile,),
        in_specs=[row, vec, row, row], out_specs=[row, row, vec],
        out_shape=[jax.ShapeDtypeStruct((s, d), F32), jax.ShapeDtypeStruct((s, d), BF16), jax.ShapeDtypeStruct((1, d), F32)],
        compiler_params=_params(("arbitrary",)),
    )(x, w, dh, dx_out)


def _final_loss(x, w, target, *, tile=256):
    s, d = x.shape
    tile = min(tile, s)

    def body(x_ref, w_ref, t_ref, loss_ref, dx_ref, dxb_ref, dw_ref):
        def f(xv, wv):
            err = jnp.square(_rms(xv, wv) - t_ref[...])
            per_token = jnp.sum(err, axis=1, keepdims=True) * np.float32(1.0 / d)
            return 0.5 * jnp.sum(per_token, axis=0, keepdims=True)

        loss, vjp = jax.vjp(f, x_ref[...], w_ref[...])
        dxv, dwv = vjp(jnp.ones((1, 1), F32))
        dx_ref[...] = dxv
        dxb_ref[...] = dxv.astype(BF16)

        @pl.when(pl.program_id(0) == 0)
        def _():
            dw_ref[...] = jnp.zeros_like(dw_ref)
            loss_ref[...] = jnp.zeros_like(loss_ref)

        dw_ref[...] += dwv
        loss_ref[...] += jnp.broadcast_to(loss, (1, LANE))

    row = pl.BlockSpec((tile, d), lambda i: (i, 0))
    vec = pl.BlockSpec((1, d), lambda i: (0, 0))
    return pl.pallas_call(
        body, name="final_loss", grid=(s // tile,),
        in_specs=[row, vec, row], out_specs=[pl.BlockSpec((1, LANE), lambda i: (0, 0)), row, row, vec],
        out_shape=[jax.ShapeDtypeStruct((1, LANE), F32), jax.ShapeDtypeStruct((s, d), F32),
                   jax.ShapeDtypeStruct((s, d), BF16), jax.ShapeDtypeStruct((1, d), F32)],
        compiler_params=_params(("arbitrary",)),
    )(x, w, target)


def _sgu_chunk(u, v, z, ln_g, ln_b, ws, bcols):
    vg = _gelu(v)
    xc = vg - jnp.mean(vg, axis=-1, keepdims=True)
    vl = xc * lax.rsqrt(jnp.mean(xc * xc, axis=-1, keepdims=True) + EPS) * ln_g + ln_b
    sp = jnp.concatenate([_mm_nn(ws[g], vg_) + bcols[g] for g, vg_ in enumerate(_split_lanes(vl))], axis=1)
    return _gelu(u) * sp * _silu(z)


def _sgu_specs(tile):
    col = lambda j: pl.BlockSpec((tile, D_SGU), lambda i, j=j: (i, j))
    vec = pl.BlockSpec((1, D_SGU), lambda i: (0, 0))
    wspec = pl.BlockSpec((N_GROUPS, SGU_CHUNK, SGU_CHUNK), lambda i: (0, 0, 0))
    bspec = pl.BlockSpec((SGU_CHUNK, N_GROUPS), lambda i: (0, 0))
    return col, vec, wspec, bspec


def _sgu_fwd(proj, ln_g, ln_b, w_s, b_t, *, tile=256):
    s = proj.shape[0]
    tile = min(tile, s)
    col, vec, wspec, bspec = _sgu_specs(tile)

    def body(u_ref, v_ref, z_ref, g_ref, b_ref, w_ref, bt_ref, y_ref, yt_ref):
        ws = tuple(w_ref[g] for g in range(N_GROUPS))
        bcols = tuple(bt_ref[:, g:g + 1] for g in range(N_GROUPS))
        for c in range(tile // SGU_CHUNK):
            rows = pl.ds(c * SGU_CHUNK, SGU_CHUNK)
            y = _sgu_chunk(u_ref[rows, :], v_ref[rows, :], z_ref[rows, :], g_ref[...], b_ref[...], ws, bcols)
            y_ref[rows, :] = y.astype(BF16)
            yt_ref[:, rows] = y.T.astype(BF16)

    return pl.pallas_call(
        body, name="sgu_fwd", grid=(s // tile,),
        in_specs=[col(0), col(1), col(2), vec, vec, wspec, bspec],
        out_specs=[pl.BlockSpec((tile, D_SGU), lambda i: (i, 0)), pl.BlockSpec((D_SGU, tile), lambda i: (0, i))],
        out_shape=[jax.ShapeDtypeStruct((s, D_SGU + D_DN), BF16), jax.ShapeDtypeStruct((D_SGU + D_DN, s), BF16)],
        compiler_params=_params(("arbitrary",)),
    )(proj, proj, proj, ln_g, ln_b, w_s, b_t)


def _sgu_bwd(proj, ln_g, ln_b, w_s, b_t, dy, dp, *, tile=128):
    s = proj.shape[0]
    tile = min(tile, s)
    col, vec, wspec, bspec = _sgu_specs(tile)

    def body(u_ref, v_ref, z_ref, g_ref, b_ref, w_ref, bt_ref, dy_ref, _, dp_ref, dg_ref, db_ref, dw_ref, dbt_ref):
        @pl.when(pl.program_id(0) == 0)
        def _():
            dg_ref[...] = jnp.zeros_like(dg_ref)
            db_ref[...] = jnp.zeros_like(db_ref)
            dw_ref[...] = jnp.zeros_like(dw_ref)
            dbt_ref[...] = jnp.zeros_like(dbt_ref)

        ws = tuple(w_ref[g] for g in range(N_GROUPS))
        bcols = tuple(bt_ref[:, g:g + 1] for g in range(N_GROUPS))
        for c in range(tile // SGU_CHUNK):
            rows = pl.ds(c * SGU_CHUNK, SGU_CHUNK)
            _, vjp = jax.vjp(_sgu_chunk, u_ref[rows, :], v_ref[rows, :], z_ref[rows, :], g_ref[...], b_ref[...], ws, bcols)
            du, dv, dz, dg, db, dws, dbc = vjp(dy_ref[rows, :])
            dp_ref[rows, pl.ds(0, D_SGU)] = du.astype(BF16)
            dp_ref[rows, pl.ds(D_SGU, D_SGU)] = dv.astype(BF16)
            dp_ref[rows, pl.ds(2 * D_SGU, D_SGU)] = dz.astype(BF16)
            dg_ref[...] += dg
            db_ref[...] += db
            for g in range(N_GROUPS):
                dw_ref[g] += dws[g]
                dbt_ref[:, g:g + 1] += dbc[g]

    return pl.pallas_call(
        body, name="sgu_bwd", grid=(s // tile,),
        in_specs=[col(0), col(1), col(2), vec, vec, wspec, bspec, pl.BlockSpec((tile, D_SGU), lambda i: (i, 0)),
                  pl.BlockSpec(memory_space=pl.ANY)],
        out_specs=[pl.BlockSpec((tile, 3 * D_SGU), lambda i: (i, 0)), vec, vec, wspec, bspec],
        out_shape=[jax.ShapeDtypeStruct(dp.shape, BF16), jax.ShapeDtypeStruct((1, D_SGU), F32),
                   jax.ShapeDtypeStruct((1, D_SGU), F32), jax.ShapeDtypeStruct((N_GROUPS, SGU_CHUNK, SGU_CHUNK), F32),
                   jax.ShapeDtypeStruct((SGU_CHUNK, N_GROUPS), F32)],
        input_output_aliases={8: 0},
        compiler_params=_params(("arbitrary",)),
    )(proj, proj, proj, ln_g, ln_b, w_s, b_t, dy, dp)


QKV_BLOCK = 1


def _qkv_head(cq, ck, cv):
    q, k, v = _silu(cq), _silu(ck), _silu(cv)
    q = q * lax.rsqrt(jnp.sum(q * q, axis=-1, keepdims=True) + EPS) * np.float32(HEAD ** -0.5)
    k = k * lax.rsqrt(jnp.sum(k * k, axis=-1, keepdims=True) + EPS)
    return q, k, v


def _gate_fn(p, a_log, dt_bias):
    lane = lax.broadcasted_iota(jnp.int32, p.shape, 1)
    g = -jnp.exp(a_log) * jax.nn.softplus(p + dt_bias)
    return jnp.where(lane < 16, g, jnp.where(lane < N_GATE, jax.nn.sigmoid(p), 0.0))


def _shifted_rows(full, shift, start, rows):
    n = full.shape[0]
    assert start % HALO == 0 and 0 <= start + shift and start + shift + rows <= n
    rolled = full if shift == 0 else pltpu.roll(full, (-shift) % n, 0)
    return rolled[start:start + rows, :]


def _halo_maps(tile, s):
    per = tile // HALO
    prev = lambda i: jnp.maximum(i * per - 1, 0)
    nxt = lambda i: jnp.minimum((i + 1) * per, s // HALO - 1)
    return prev, nxt


def _prep_fwd(proj, conv_w, pg, a_log, dt_bias, *, tile=256):
    s = proj.shape[0]
    width = 3 * D_DN
    tile = min(tile, s)
    n_blk = s // tile
    prev, nxt = _halo_maps(tile, s)

    def body(x_ref, xp_ref, xn_ref, w_ref, pg_ref, al_ref, dt_ref, q_ref, k_ref, v_ref, gate_ref, ext):
        i = pl.program_id(0)
        ext[pl.ds(0, HALO), :] = jnp.where(i > 0, xp_ref[...], 0.0)
        ext[pl.ds(HALO, tile), :] = x_ref[...]
        ext[pl.ds(HALO + tile, HALO), :] = jnp.where(i < n_blk - 1, xn_ref[...], 0.0)
        full, c = ext[...], None
        for j in range(CONV_W):
            term = _shifted_rows(full, j - CONV_W // 2, HALO, tile) * w_ref[j:j + 1, :]
            c = term if c is None else c + term
        for h in range(N_GROUPS):
            q, k, v = _qkv_head(c[:, h * HEAD:(h + 1) * HEAD], c[:, D_DN + h * HEAD:D_DN + (h + 1) * HEAD],
                                c[:, 2 * D_DN + h * HEAD:2 * D_DN + (h + 1) * HEAD])
            q_ref[h] = q
            k_ref[h] = k
            v_ref[h] = v
        gate_ref[...] = _gate_fn(pg_ref[...], al_ref[...], dt_ref[...])

    hm = pl.BlockSpec((N_GROUPS, tile, HEAD), lambda i: (0, i, 0))
    lane_vec = pl.BlockSpec((1, LANE), lambda i: (0, 0))
    return pl.pallas_call(
        body, name="prep_fwd", grid=(n_blk,),
        in_specs=[pl.BlockSpec((tile, width), lambda i: (i, QKV_BLOCK)),
                  pl.BlockSpec((HALO, width), lambda i: (prev(i), QKV_BLOCK)),
                  pl.BlockSpec((HALO, width), lambda i: (nxt(i), QKV_BLOCK)),
                  pl.BlockSpec((HALO, width), lambda i: (0, 0)),
                  pl.BlockSpec((tile, LANE), lambda i: (i, 0)), lane_vec, lane_vec],
        out_specs=[hm, hm, hm, pl.BlockSpec((tile, LANE), lambda i: (i, 0))],
        out_shape=[jax.ShapeDtypeStruct((N_GROUPS, s, HEAD), F32)] * 3 + [jax.ShapeDtypeStruct((s, LANE), F32)],
        scratch_shapes=[pltpu.VMEM((tile + 2 * HALO, width), F32)],
        compiler_params=_params(("arbitrary",)),
    )(proj, proj, proj, conv_w, pg, a_log, dt_bias)


def _prep_bwd(proj, conv_w, pg, a_log, dt_bias, dqkv_f, dqkv_b, dgates, dp, *, tile=128):
    s = proj.shape[0]
    width = 3 * D_DN
    tile = min(tile, s)
    n_blk = s // tile
    prev, nxt = _halo_maps(tile, s)
    ext_rows = tile + 2 * HALO
    pad = CONV_W // 2

    def body(*refs):
        x_ref, xp_ref, xn_ref, w_ref, pg_ref, al_ref, dt_ref = refs[:7]
        ct_refs = refs[7:25]
        dg0_ref, dg1_ref, dg2_ref = refs[25:28]
        dp_ref, dpg_ref, dw_ref, dal_ref, ddt_ref, xext, dcext = refs[29:]
        i = pl.program_id(0)
        has_prev, has_next = i > 0, i < n_blk - 1

        @pl.when(i == 0)
        def _():
            dw_ref[...] = jnp.zeros_like(dw_ref)
            dal_ref[...] = jnp.zeros_like(dal_ref)
            ddt_ref[...] = jnp.zeros_like(ddt_ref)

        zeros = jnp.zeros((HALO, width), F32)
        xext[pl.ds(0, HALO), :] = zeros
        xext[pl.ds(HALO, HALO), :] = jnp.where(has_prev, xp_ref[...], 0.0)
        xext[pl.ds(2 * HALO, tile), :] = x_ref[...]
        xext[pl.ds(2 * HALO + tile, HALO), :] = jnp.where(has_next, xn_ref[...], 0.0)
        xext[pl.ds(3 * HALO + tile, HALO), :] = zeros
        xfull, c = xext[...], None
        for j in range(CONV_W):
            term = _shifted_rows(xfull, j - pad, HALO, ext_rows) * w_ref[j:j + 1, :]
            c = term if c is None else c + term
        for h in range(N_GROUPS):
            lanes = [pl.ds(part * D_DN + h * HEAD, HEAD) for part in range(3)]
            cts = []
            for part in range(3):
                (cur, before, after), (cur2, before2, after2) = ct_refs[3 * part:3 * part + 3], ct_refs[9 + 3 * part:12 + 3 * part]
                cts.append(jnp.concatenate([jnp.where(has_prev, before[h] + before2[h], 0.0), cur[h] + cur2[h],
                                            jnp.where(has_next, after[h] + after2[h], 0.0)], axis=0))
            _, vjp = jax.vjp(_qkv_head, c[:, h * HEAD:(h + 1) * HEAD], c[:, D_DN + h * HEAD:D_DN + (h + 1) * HEAD],
                             c[:, 2 * D_DN + h * HEAD:2 * D_DN + (h + 1) * HEAD])
            for lane_sl, dc in zip(lanes, vjp(tuple(cts))):
                dcext[:, lane_sl] = dc
        dcfull = dcext[...]
        dx, dc_blk = None, dcfull[HALO:HALO + tile, :]
        for j in range(CONV_W):
            term = _shifted_rows(dcfull, pad - j, HALO, tile) * w_ref[j:j + 1, :]
            dx = term if dx is None else dx + term
            dw_ref[j:j + 1, :] += jnp.sum(dc_blk * _shifted_rows(xfull, j - pad, 2 * HALO, tile), axis=0, keepdims=True)
        dp_ref[...] = dx.astype(BF16)
        _, gvjp = jax.vjp(_gate_fn, pg_ref[...], al_ref[...], dt_ref[...])
        dpg, dal, ddt = gvjp(dg0_ref[...] + dg1_ref[...] + dg2_ref[...])
        dpg_ref[...] = dpg.astype(BF16)
        dal_ref[...] += dal
        ddt_ref[...] += ddt

    xs = [pl.BlockSpec((tile, width), lambda i: (i, QKV_BLOCK)), pl.BlockSpec((HALO, width), lambda i: (prev(i), QKV_BLOCK)),
          pl.BlockSpec((HALO, width), lambda i: (nxt(i), QKV_BLOCK))]
    hm = [pl.BlockSpec((N_GROUPS, tile, HEAD), lambda i: (0, i, 0)),
          pl.BlockSpec((N_GROUPS, HALO, HEAD), lambda i: (0, prev(i), 0)),
          pl.BlockSpec((N_GROUPS, HALO, HEAD), lambda i: (0, nxt(i), 0))]
    gate = pl.BlockSpec((tile, LANE), lambda i: (i, 0))
    lane_vec = pl.BlockSpec((1, LANE), lambda i: (0, 0))
    wspec = pl.BlockSpec((HALO, width), lambda i: (0, 0))
    n_in = 3 + 4 + 18 + 3
    ct_arrays = [a for dqkv in (dqkv_f, dqkv_b) for a in dqkv for _ in range(3)]
    return pl.pallas_call(
        body, name="prep_bwd", grid=(n_blk,),
        in_specs=xs + [wspec, gate, lane_vec, lane_vec] + hm * 6 + [gate] * 3 + [pl.BlockSpec(memory_space=pl.ANY)],
        out_specs=[pl.BlockSpec((tile, width), lambda i: (i, QKV_BLOCK)), gate, wspec, lane_vec, lane_vec],
        out_shape=[jax.ShapeDtypeStruct(dp.shape, BF16), jax.ShapeDtypeStruct((s, LANE), BF16),
                   jax.ShapeDtypeStruct((HALO, width), F32), jax.ShapeDtypeStruct((1, LANE), F32),
                   jax.ShapeDtypeStruct((1, LANE), F32)],
        input_output_aliases={n_in: 0},
        scratch_shapes=[pltpu.VMEM((tile + 4 * HALO, width), F32), pltpu.VMEM((ext_rows, width), F32)],
        compiler_params=_params(("arbitrary",)),
    )(proj, proj, proj, conv_w, pg, a_log, dt_bias, *ct_arrays, *dgates, dp)


def _each(f, *cols):
    return tuple(f(*a) for a in zip(*cols))


def _delta_chunk(q, k, v, g_col, g_row, beta, state, t_known, reverse):
    n = DN_CHUNK
    ri = lax.broadcasted_iota(jnp.int32, (n, n), 0)
    ci = lax.broadcasted_iota(jnp.int32, (n, n), 1)
    masks = {False: (ri >= ci, ri <= ci, ri > ci), True: (ri <= ci, ri >= ci, ri < ci)}
    flags = tuple(reverse) if isinstance(reverse, (tuple, list)) else (reverse,) * len(q)
    incl, incl_t, strict = (tuple(masks[f][i] for f in flags) for i in range(3))
    gc_col = _each(lambda gr, mk: jnp.sum(jnp.where(mk, gr, 0.0), axis=1, keepdims=True), g_row, incl)
    gc_row = _each(lambda gc, mk: jnp.sum(jnp.where(mk, gc, 0.0), axis=0, keepdims=True), g_col, incl_t)
    g_tot = _each(lambda gr: jnp.sum(gr, axis=1, keepdims=True), g_row)
    decay = _each(lambda a, b, mk: jnp.where(mk, jnp.exp(jnp.where(mk, a - b, 0.0)), 0.0), gc_col, gc_row, incl)
    rows = lambda *xs: jnp.concatenate(xs, axis=0)
    kb = _each(lambda a, b: a * b, k, beta)
    kk_qk = _each(lambda kb_, q_, k_: _split_rows(_mm_nt(rows(kb_, q_), k_), n), kb, q, k)
    m = _each(lambda two, d, mk: jnp.where(mk, two[0] * d, 0.0), kk_qk, decay, strict)
    if t_known is None:
        eye = (ri == ci).astype(F32)
        p = _each(lambda a: -a, m)
        t = tuple(eye for _ in p)
        for _ in range(5):
            both = _each(lambda p_, t_: _bdot(rows(p_, t_), p_, ((1,), (0,))), p, t)
            p = _each(lambda b: b[:n], both)
            t = _each(lambda t_, b: t_ + b[n:], t, both)
        t = _each(lambda t_, p_: t_ + _bdot(t_, p_, ((1,), (0,))), t, p)
    else:
        t = _each(_inverse_given, m, t_known)
    e_gc = _each(jnp.exp, gc_col)
    u_w = _each(lambda t_, v_, b_, kb_, e_: _split_lanes(_mm_nn(t_, jnp.concatenate([v_ * b_, kb_ * e_], axis=1))), t, v, beta, kb, e_gc)
    attn = _each(lambda two, d_: two[1] * d_, kk_qk, decay)
    ws_qs = _each(lambda uw, q_, e_, s_: _split_rows(_mm_nn(rows(uw[1], q_ * e_), s_), n), u_w, q, e_gc, state)
    v_new = _each(lambda uw, wq: uw[0] - wq[0], u_w, ws_qs)
    k_dec_t = _each(lambda k_, gt, gc: jnp.transpose(k_ * jnp.exp(gt - gc)), k, g_tot, gc_col)
    av_kv = _each(lambda a_, kt, vn: _split_rows(_mm_nn(rows(a_, kt), vn), n), attn, k_dec_t, v_new)
    o = _each(lambda wq, ak: wq[1] + ak[0], ws_qs, av_kv)
    new_state = _each(lambda s_, gt, ak: s_ * jnp.exp(gt) + ak[1], state, g_tot, av_kv)
    return o, new_state, t


def _delta_lanes(reverse):
    return (N_GROUPS, 3 * N_GROUPS) if reverse else (0, 2 * N_GROUPS)


def _delta_fwd(q, k, v, gates, rows_f, rows_b, *, comm=None):
    s = q.shape[1]
    nc = s // DN_CHUNK
    heads = range(N_GROUPS)

    def body(qf_ref, kf_ref, vf_ref, gf_ref, rf_ref, qb_ref, kb_ref, vb_ref, gb_ref, rb_ref,
             of_ref, sf_ref, tf_ref, ob_ref, sb_ref, tb_ref, state):
        @pl.when(pl.program_id(0) == 0)
        def _():
            state[...] = jnp.zeros_like(state)

        qs, ks, vs, g_cols, g_rows, betas, flags = [], [], [], [], [], [], []
        for reverse, q_ref, k_ref, v_ref, gate, row_ref in ((False, qf_ref, kf_ref, vf_ref, gf_ref[...], rf_ref),
                                                            (True, qb_ref, kb_ref, vb_ref, gb_ref[...], rb_ref)):
            g_lane, b_lane = _delta_lanes(reverse)
            for j in heads:
                qs.append(q_ref[j])
                ks.append(k_ref[j])
                vs.append(v_ref[j])
                g_cols.append(gate[:, g_lane + j:g_lane + j + 1])
                g_rows.append(row_ref[0, j:j + 1, :])
                betas.append(gate[:, b_lane + j:b_lane + j + 1])
                flags.append(reverse)
        prev = tuple(state[n] for n in range(2 * N_GROUPS))
        o, new, t = _delta_chunk(tuple(qs), tuple(ks), tuple(vs), tuple(g_cols), tuple(g_rows), tuple(betas), prev, None, tuple(flags))
        for n in range(2 * N_GROUPS):
            o_ref, st_ref, t_ref = (of_ref, sf_ref, tf_ref) if n < N_GROUPS else (ob_ref, sb_ref, tb_ref)
            j = n % N_GROUPS
            st_ref[0, j] = prev[n]
            o_ref[j] = o[n]
            t_ref[0, j] = t[n]
            state[n] = new[n]

    def specs(chunk_of):
        hm = pl.BlockSpec((N_GROUPS, DN_CHUNK, HEAD), lambda c: (0, chunk_of(c), 0))
        ins = [hm, hm, hm, pl.BlockSpec((DN_CHUNK, LANE), lambda c: (chunk_of(c), 0)),
               pl.BlockSpec((1, N_GROUPS, DN_CHUNK), lambda c: (chunk_of(c), 0, 0))]
        outs = [hm, pl.BlockSpec((1, N_GROUPS, HEAD, HEAD), lambda c: (chunk_of(c), 0, 0, 0)),
                pl.BlockSpec((1, N_GROUPS, DN_CHUNK, DN_CHUNK), lambda c: (chunk_of(c), 0, 0, 0))]
        return ins, outs

    (ins_f, outs_f), (ins_b, outs_b) = specs(lambda c: c), specs(lambda c: nc - 1 - c)
    shapes = [jax.ShapeDtypeStruct((N_GROUPS, s, HEAD), F32), jax.ShapeDtypeStruct((nc, N_GROUPS, HEAD, HEAD), F32),
              jax.ShapeDtypeStruct((nc, N_GROUPS, DN_CHUNK, DN_CHUNK), F32)]
    results, sent = _call_with_comm(
        body, comm, name="delta_fwd" + ("" if comm is None else "_comm"), grid=(nc,),
        in_specs=ins_f + ins_b, out_specs=outs_f + outs_b, out_shape=shapes + shapes,
        scratch_shapes=[pltpu.VMEM((2 * N_GROUPS, HEAD, HEAD), F32)],
        args=(q, k, v, gates, rows_f, q, k, v, gates, rows_b), compiler_params=_params(("arbitrary",)))
    return results[:3], results[3:], sent


def _delta_bwd(q, k, v, gates, rows_f, rows_b, states_f, states_b, t_f, t_b, do, *, comm=None):
    s = q.shape[1]
    nc = s // DN_CHUNK
    heads = range(N_GROUPS)
    n_in, n_out = 8, 5

    def body(*refs):
        ins = (refs[:n_in], refs[n_in:2 * n_in])
        outs = (refs[2 * n_in:2 * n_in + n_out], refs[2 * n_in + n_out:2 * n_in + 2 * n_out])
        dstate = refs[-1]

        @pl.when(pl.program_id(0) == 0)
        def _():
            dstate[...] = jnp.zeros_like(dstate)

        qs, ks, vs, g_cols, g_rows, betas, sts, ts, dos, flags = ([] for _ in range(10))
        for reverse, (q_ref, k_ref, v_ref, gate_ref, row_ref, st_ref, t_ref, do_ref) in zip((False, True), ins):
            gate = gate_ref[...]
            g_lane, b_lane = _delta_lanes(reverse)
            for j in heads:
                qs.append(q_ref[j])
                ks.append(k_ref[j])
                vs.append(v_ref[j])
                g_cols.append(gate[:, g_lane + j:g_lane + j + 1])
                g_rows.append(row_ref[0, j:j + 1, :])
                betas.append(gate[:, b_lane + j:b_lane + j + 1])
                sts.append(st_ref[0, j])
                ts.append(t_ref[0, j])
                dos.append(do_ref[j])
                flags.append(reverse)
        t_known, flags = tuple(ts), tuple(flags)
        f = lambda q_, k_, v_, gc_, gr_, b_, s_: _delta_chunk(q_, k_, v_, gc_, gr_, b_, s_, t_known, flags)[:2]
        _, vjp = jax.vjp(f, tuple(qs), tuple(ks), tuple(vs), tuple(g_cols), tuple(g_rows), tuple(betas), tuple(sts))
        dq, dk, dv, dg_col, dg_row, dbeta, dprev = vjp((tuple(dos), tuple(dstate[n] for n in range(2 * N_GROUPS))))
        ids = lax.broadcasted_iota(jnp.int32, (DN_CHUNK, LANE), 1)
        for reverse, (dq_ref, dk_ref, dv_ref, dgate_ref, drow_ref) in zip((False, True), outs):
            g_lane, b_lane = _delta_lanes(reverse)
            dgate = jnp.zeros((DN_CHUNK, LANE), F32)
            for j in heads:
                n = j + (N_GROUPS if reverse else 0)
                dq_ref[j] = dq[n]
                dk_ref[j] = dk[n]
                dv_ref[j] = dv[n]
                dstate[n] = dprev[n]
                dgate = dgate + jnp.where(ids == g_lane + j, dg_col[n], 0.0) + jnp.where(ids == b_lane + j, dbeta[n], 0.0)
                drow_ref[0, j:j + 1, :] = dg_row[n]
            dgate_ref[...] = dgate

    def specs(chunk_of):
        hm = pl.BlockSpec((N_GROUPS, DN_CHUNK, HEAD), lambda c: (0, chunk_of(c), 0))
        gate = pl.BlockSpec((DN_CHUNK, LANE), lambda c: (chunk_of(c), 0))
        rows = pl.BlockSpec((1, N_GROUPS, DN_CHUNK), lambda c: (chunk_of(c), 0, 0))
        st = pl.BlockSpec((1, N_GROUPS, HEAD, HEAD), lambda c: (chunk_of(c), 0, 0, 0))
        ts = pl.BlockSpec((1, N_GROUPS, DN_CHUNK, DN_CHUNK), lambda c: (chunk_of(c), 0, 0, 0))
        return [hm, hm, hm, gate, rows, st, ts, hm], [hm, hm, hm, gate, rows]

    (ins_f, outs_f), (ins_b, outs_b) = specs(lambda c: nc - 1 - c), specs(lambda c: c)
    shapes = [jax.ShapeDtypeStruct((N_GROUPS, s, HEAD), F32)] * 3 + [jax.ShapeDtypeStruct((s, LANE), F32),
                                                                       jax.ShapeDtypeStruct((nc, N_GROUPS, DN_CHUNK), F32)]
    results, sent = _call_with_comm(
        body, comm, name="delta_bwd" + ("" if comm is None else "_comm"), grid=(nc,),
        in_specs=ins_f + ins_b, out_specs=outs_f + outs_b, out_shape=shapes + shapes,
        scratch_shapes=[pltpu.VMEM((2 * N_GROUPS, HEAD, HEAD), F32)],
        args=(q, k, v, gates, rows_f, states_f, t_f, do, q, k, v, gates, rows_b, states_b, t_b, do),
        compiler_params=_params(("arbitrary",)))
    return results[:n_out], results[n_out:], sent


ZB_BLOCK = 6


def _post_head(o_f, o_b, z, w):
    return _rms(o_f + o_b, w) * _silu(z)


def _post_fwd(o_f, o_b, proj, w, y, y_t, *, tile=256):
    s = proj.shape[0]
    tile = min(tile, s)

    def body(of_ref, ob_ref, z_ref, w_ref, _, __, y_ref, yt_ref):
        for h in range(N_GROUPS):
            lanes = pl.ds(h * HEAD, HEAD)
            y_h = _post_head(of_ref[h], ob_ref[h], z_ref[:, lanes], w_ref[...])
            y_ref[:, lanes] = y_h.astype(BF16)
            yt_ref[lanes, :] = y_h.T.astype(BF16)

    hm = pl.BlockSpec((N_GROUPS, tile, HEAD), lambda i: (0, i, 0))
    any_spec = pl.BlockSpec(memory_space=pl.ANY)
    return pl.pallas_call(
        body, name="post_fwd", grid=(s // tile,),
        in_specs=[hm, hm, pl.BlockSpec((tile, D_DN), lambda i: (i, ZB_BLOCK)), pl.BlockSpec((1, HEAD), lambda i: (0, 0)),
                  any_spec, any_spec],
        out_specs=[pl.BlockSpec((tile, D_DN), lambda i: (i, 1)), pl.BlockSpec((D_DN, tile), lambda i: (1, i))],
        out_shape=[jax.ShapeDtypeStruct(y.shape, BF16), jax.ShapeDtypeStruct(y_t.shape, BF16)],
        input_output_aliases={4: 0, 5: 1},
        compiler_params=_params(("arbitrary",)),
    )(o_f, o_b, proj, w, y, y_t)


def _post_bwd(o_f, o_b, proj, w, dy, *, tile=256):
    s = proj.shape[0]
    tile = min(tile, s)

    def body(of_ref, ob_ref, z_ref, w_ref, dy_ref, do_ref, dp_ref, dw_ref):
        @pl.when(pl.program_id(0) == 0)
        def _():
            dw_ref[...] = jnp.zeros_like(dw_ref)

        for h in range(N_GROUPS):
            lanes = pl.ds(h * HEAD, HEAD)
            _, vjp = jax.vjp(_post_head, of_ref[h], ob_ref[h], z_ref[:, lanes], w_ref[...])
            do, _, dz, dw = vjp(dy_ref[:, lanes])
            do_ref[h] = do
            dp_ref[:, lanes] = dz.astype(BF16)
            dw_ref[...] += dw

    hm = pl.BlockSpec((N_GROUPS, tile, HEAD), lambda i: (0, i, 0))
    zb = pl.BlockSpec((tile, D_DN), lambda i: (i, ZB_BLOCK))
    vec = pl.BlockSpec((1, HEAD), lambda i: (0, 0))
    return pl.pallas_call(
        body, name="post_bwd", grid=(s // tile,),
        in_specs=[hm, hm, zb, vec, pl.BlockSpec((tile, D_DN), lambda i: (i, 1))], out_specs=[hm, zb, vec],
        out_shape=[jax.ShapeDtypeStruct((N_GROUPS, s, HEAD), F32), jax.ShapeDtypeStruct((s, D_MAIN), BF16),
                   jax.ShapeDtypeStruct((1, HEAD), F32)],
        compiler_params=_params(("arbitrary",)),
    )(o_f, o_b, proj, w, dy)


def _my_index():
    return 4 * lax.axis_index("x") + 2 * lax.axis_index("y") + lax.axis_index("c")


def _peer(k):
    x, y, c = lax.axis_index("x"), lax.axis_index("y"), lax.axis_index("c")
    px, py, pc = x ^ ((k >> 2) & 1), y ^ ((k >> 1) & 1), c ^ (k & 1)
    return (px, py, pc), 4 * px + 2 * py + pc


class _Comm:
    def __init__(self, inputs, outputs, copies):
        self.inputs, self.outputs, self.copies = list(inputs), list(outputs), list(copies)
        self.kept = [o for o in self.outputs if not isinstance(o, jax.ShapeDtypeStruct)]

    def out_shapes(self):
        return [jax.ShapeDtypeStruct(o.shape, o.dtype) for o in self.outputs]

    def aliases(self, first_kept_input, first_output):
        kept_at = [i for i, o in enumerate(self.outputs) if not isinstance(o, jax.ShapeDtypeStruct)]
        return {first_kept_input + n: first_output + i for n, i in enumerate(kept_at)}

    def scratch(self):
        n = len(self.copies)
        return [pltpu.SemaphoreType.DMA((n, N_DEV - 1)), pltpu.SemaphoreType.DMA((n, N_DEV - 1)), pltpu.SemaphoreType.DMA((n,))]

    def _descriptors(self, in_refs, out_refs, sems, arrivals):
        send_sems, recv_sems, local_sems = sems
        me = _my_index()
        local, remote = [], []
        for k in range(N_DEV):
            peer, peer_idx = _peer(k)
            for a, (i_in, i_out, src_of, dst_of) in enumerate(self.copies):
                if k == 0:
                    if not arrivals:
                        local.append(pltpu.make_async_copy(src_of(in_refs[i_in], me), dst_of(out_refs[i_out], me), local_sems.at[a]))
                    continue
                receiver, sender = (me, peer_idx) if arrivals else (peer_idx, me)
                remote.append(pltpu.make_async_remote_copy(
                    src_ref=src_of(in_refs[i_in], receiver), dst_ref=dst_of(out_refs[i_out], sender),
                    send_sem=send_sems.at[a, k - 1], recv_sem=recv_sems.at[a, k - 1], device_id=peer,
                    device_id_type=pl.DeviceIdType.MESH))
        return local, remote

    def run_two_level(self, in_refs, out_refs, sems):
        send_sems, recv_sems, local_sems = sems
        me = _my_index()
        sibling, _ = _peer(1)

        def remote(a, k, src, block_owner, to):
            _, i_out, _, dst_of = self.copies[a]
            return pltpu.make_async_remote_copy(
                src_ref=src, dst_ref=dst_of(out_refs[i_out], block_owner), send_sem=send_sems.at[a, k - 1],
                recv_sem=recv_sems.at[a, k - 1], device_id=to, device_id_type=pl.DeviceIdType.MESH)

        sources = [src_of(in_refs[i_in], me) for i_in, _, src_of, _ in self.copies]
        n = range(len(self.copies))
        local = [pltpu.make_async_copy(sources[a], self.copies[a][3](out_refs[self.copies[a][1]], me), local_sems.at[a]) for a in n]
        first = [remote(a, k, sources[a], me, _peer(k)[0]) for k in (1, 2, 4, 6) for a in n]
        for cp in local + first:
            cp.start()
        passed = []
        for k in (2, 4, 6):
            peer, peer_idx = _peer(k)
            for a in n:
                remote(a, k, sources[a], peer_idx, peer).wait_recv()
                landed = self.copies[a][3](out_refs[self.copies[a][1]], peer_idx)
                passed.append(remote(a, k + 1, landed, peer_idx, sibling))
                passed[-1].start()
        for k in (1, 3, 5, 7):
            peer, peer_idx = _peer(k)
            for a in n:
                remote(a, k, sources[a], peer_idx, peer).wait_recv()
        for cp in first + passed:
            cp.wait_send()
        for cp in local:
            cp.wait()

    def start(self, in_refs, out_refs, sems):
        local, sends = self._descriptors(in_refs, out_refs, sems, arrivals=False)
        for cp in local + sends:
            cp.start()

    def wait(self, in_refs, out_refs, sems):
        _, arriving = self._descriptors(in_refs, out_refs, sems, arrivals=True)
        for cp in arriving:
            cp.wait_recv()
        local, sends = self._descriptors(in_refs, out_refs, sems, arrivals=False)
        for cp in sends:
            cp.wait_send()
        for cp in local:
            cp.wait()


def _call_with_comm(body, comm, *, name, grid, in_specs, out_specs, out_shape, scratch_shapes, args, compiler_params):
    if comm is None:
        return pl.pallas_call(body, name=name, grid=grid, in_specs=in_specs, out_specs=out_specs, out_shape=out_shape,
                              scratch_shapes=scratch_shapes, compiler_params=compiler_params)(*args), []
    n_in, n_out, n_scratch = len(in_specs), len(out_specs), len(scratch_shapes)
    c_in, c_kept, c_out = len(comm.inputs), len(comm.kept), len(comm.outputs)

    def at_step(position):
        here = [pl.program_id(d) == p for d, p in enumerate(position)]
        return here[0] if len(here) == 1 else functools.reduce(jnp.logical_and, here)

    def wrapped(*refs):
        main_in, comm_in = refs[:n_in], refs[n_in:n_in + c_in]
        o = n_in + c_in + c_kept
        main_out, comm_out = refs[o:o + n_out], refs[o + n_out:o + n_out + c_out]
        s = o + n_out + c_out
        main_scratch, sems = refs[s:s + n_scratch], refs[s + n_scratch:]

        @pl.when(at_step([0] * len(grid)))
        def _():
            comm.start(comm_in, comm_out, sems)

        body(*main_in, *main_out, *main_scratch)

        @pl.when(at_step([g - 1 for g in grid]))
        def _():
            comm.wait(comm_in, comm_out, sems)

    any_spec = pl.BlockSpec(memory_space=pl.ANY)
    results = pl.pallas_call(
        wrapped, name=name, grid=grid, in_specs=list(in_specs) + [any_spec] * (c_in + c_kept),
        out_specs=list(out_specs) + [any_spec] * c_out, out_shape=list(out_shape) + comm.out_shapes(),
        scratch_shapes=list(scratch_shapes) + comm.scratch(), input_output_aliases=comm.aliases(n_in + c_in, n_out),
        compiler_params=compiler_params,
    )(*args, *comm.inputs, *comm.kept)
    return results[:n_out], results[n_out:]


def _exchange(name, comm, two_level=False):
    c_in, c_kept, c_out = len(comm.inputs), len(comm.kept), len(comm.outputs)

    def body(*refs):
        in_refs, out_refs, sems = refs[:c_in], refs[c_in + c_kept:c_in + c_kept + c_out], refs[c_in + c_kept + c_out:]
        if two_level:
            comm.run_two_level(in_refs, out_refs, sems)
        else:
            comm.start(in_refs, out_refs, sems)
            comm.wait(in_refs, out_refs, sems)

    any_spec = pl.BlockSpec(memory_space=pl.ANY)
    return pl.pallas_call(
        body, name=name, in_specs=[any_spec] * (c_in + c_kept), out_specs=[any_spec] * c_out, out_shape=comm.out_shapes(),
        scratch_shapes=comm.scratch(), input_output_aliases=comm.aliases(c_in, 0),
    )(*comm.inputs, *comm.kept)


def _whole(ref, _):
    return ref


def _slot(ref, dev):
    return ref.at[dev]


def _row_block(size, axis):
    def pick(ref, dev):
        start = pl.multiple_of(dev * size, size) if size % SUBLANES == 0 else dev * size
        return ref.at[tuple([slice(None)] * axis + [pl.ds(start, size)])]
    return pick


def _blocked(shape, block):
    nd = len(block)
    grid = tuple(shape[d] // block[d] for d in range(nd) if block[d] != shape[d])
    return grid, (lambda *i: tuple(i) + (0,) * (nd - len(grid)))


def _sum_parts(name, parts, block, comm=None):
    n_parts = parts.shape[0]
    grid, idx = _blocked(parts.shape[1:], block)

    def body(p_ref, g_ref):
        g = p_ref[0].astype(F32)
        for j in range(1, n_parts):
            g = g + p_ref[j].astype(F32)
        g_ref[...] = g

    (total,), sent = _call_with_comm(
        body, comm, name=name, grid=grid, in_specs=[pl.BlockSpec((n_parts,) + tuple(block), lambda *i: (0,) + idx(*i))],
        out_specs=[pl.BlockSpec(block, idx)], out_shape=[jax.ShapeDtypeStruct(parts.shape[1:], F32)], scratch_shapes=[],
        args=(parts,), compiler_params=_params(("arbitrary",) * len(grid)))
    return total, sent


def _adamw(name, parts, w, m, v, block):
    n_parts = parts.shape[0]
    nd = len(block)
    grid, _ = _blocked(w.shape, block)
    lead = len(grid)

    def body(p_ref, w_ref, m_ref, v_ref, g_ref, d_ref, mo_ref, vo_ref):
        g = p_ref[0].astype(F32)
        for j in range(1, n_parts):
            g = g + p_ref[j].astype(F32)
        m_new = ADAM_B1 * m_ref[...] + (1.0 - ADAM_B1) * g
        v_new = ADAM_B2 * v_ref[...] + (1.0 - ADAM_B2) * jnp.square(g)
        m_hat = m_new / np.float32(1.0 - ADAM_B1 ** ADAM_STEP)
        v_hat = v_new / np.float32(1.0 - ADAM_B2 ** ADAM_STEP)
        g_ref[...] = g
        d_ref[...] = -ADAM_LR * (m_hat / (jnp.sqrt(v_hat) + ADAM_EPS) + ADAM_WD * w_ref[...])
        mo_ref[...] = m_new
        vo_ref[...] = v_new

    idx = lambda *i: tuple(i) + (0,) * (nd - lead)
    spec = pl.BlockSpec(block, idx)
    pspec = pl.BlockSpec((n_parts,) + tuple(block), lambda *i: (0,) + idx(*i))
    n_blk = int(np.prod(block[:-1])) * (-(-block[-1] // LANE) * LANE)
    return pl.pallas_call(
        body, name=name, grid=grid, in_specs=[pspec, spec, spec, spec], out_specs=[spec] * 4,
        out_shape=[jax.ShapeDtypeStruct(w.shape, F32)] * 4,
        compiler_params=_params(("arbitrary",) * lead, 2 * (n_parts + 7) * n_blk * 4 + VMEM_MARGIN),
    )(parts, w, m, v)


def _to_flat(pieces, row_multiple):
    flat = jnp.concatenate([p.reshape(-1) for p in pieces])
    rows = -(-flat.shape[0] // FLAT_W)
    rows = -(-rows // row_multiple) * row_multiple
    return jnp.pad(flat, (0, rows * FLAT_W - flat.shape[0])).reshape(rows, FLAT_W)


def _lane_row(*pieces):
    flat = jnp.concatenate([p.reshape(-1) for p in pieces])
    return jnp.pad(flat, (0, LANE - flat.shape[0])).reshape(1, LANE)


def _rows_view(a):
    return a.reshape(a.shape[:-1] + (SUBLANES, LANE))


def _in_shard_t(a):
    return _rows_view(jnp.swapaxes(a, 1, 2))


def kernel(x, norm_w, w_in, sgu_ln_g, sgu_ln_b, sgu_w, sgu_b, conv_w, a_log_f, a_log_b, dt_bias_f, dt_bias_b, dn_norm_w, w_out, final_norm_w, loss_target, m_norm_w, m_w_in, m_sgu_ln_g, m_sgu_ln_b, m_sgu_w, m_sgu_b, m_conv_w, m_a_log_f, m_a_log_b, m_dt_bias_f, m_dt_bias_b, m_dn_norm_w, m_w_out, m_final_norm_w, v_norm_w, v_w_in, v_sgu_ln_g, v_sgu_ln_b, v_sgu_w, v_sgu_b, v_conv_w, v_a_log_f, v_a_log_b, v_dt_bias_f, v_dt_bias_b, v_dn_norm_w, v_w_out, v_final_norm_w):
    weights = dict(norm_w=norm_w, w_in=w_in, sgu_ln_g=sgu_ln_g, sgu_ln_b=sgu_ln_b, sgu_w=sgu_w, sgu_b=sgu_b, conv_w=conv_w,
                   a_log_f=a_log_f, a_log_b=a_log_b, dt_bias_f=dt_bias_f, dt_bias_b=dt_bias_b, dn_norm_w=dn_norm_w,
                   w_out=w_out, final_norm_w=final_norm_w)
    m_in = dict(norm_w=m_norm_w, w_in=m_w_in, sgu_ln_g=m_sgu_ln_g, sgu_ln_b=m_sgu_ln_b, sgu_w=m_sgu_w, sgu_b=m_sgu_b,
                conv_w=m_conv_w, a_log_f=m_a_log_f, a_log_b=m_a_log_b, dt_bias_f=m_dt_bias_f, dt_bias_b=m_dt_bias_b,
                dn_norm_w=m_dn_norm_w, w_out=m_w_out, final_norm_w=m_final_norm_w)
    v_in = dict(norm_w=v_norm_w, w_in=v_w_in, sgu_ln_g=v_sgu_ln_g, sgu_ln_b=v_sgu_ln_b, sgu_w=v_sgu_w, sgu_b=v_sgu_b,
                conv_w=v_conv_w, a_log_f=v_a_log_f, a_log_b=v_a_log_b, dt_bias_f=v_dt_bias_f, dt_bias_b=v_dt_bias_b,
                dn_norm_w=v_dn_norm_w, w_out=v_w_out, final_norm_w=v_final_norm_w)
    s = x.shape[1]
    nc = s // DN_CHUNK
    depth = w_in.shape[0]
    in_shard, out_shard, conv_shard = w_in.shape[2], w_out.shape[1], conv_w.shape[2]
    x0 = x.reshape(s, D_MODEL)
    target = loss_target.reshape(s, D_MODEL)

    w_in_shard = _in_shard_t(w_in.astype(BF16))
    w_out_shard = w_out.astype(BF16)

    gathered_in_shape = jax.ShapeDtypeStruct((N_DEV * in_shard, SUBLANES, LANE), BF16)
    gathered_out_shape = jax.ShapeDtypeStruct((D_MODEL, D_MODEL), BF16)
    gathered_in, gathered_out, conv_all = _exchange("gather_first", _Comm(
        [w_in_shard[0], w_out_shard[0], conv_w],
        [gathered_in_shape, gathered_out_shape, jax.ShapeDtypeStruct((N_DEV,) + conv_w.shape, F32)],
        [(0, 0, _whole, _row_block(in_shard, 0)), (1, 1, _whole, _row_block(out_shard, 0)), (2, 2, _whole, _slot)]), two_level=True)
    gather_plan = lambda l: _Comm([w_in_shard[l], w_out_shard[l]], [gathered_in_shape, gathered_out_shape],
                                  [(0, 0, _whole, _row_block(in_shard, 0)), (1, 1, _whole, _row_block(out_shard, 0))])
    conv_full = conv_all.transpose(1, 2, 0, 3).reshape(depth, CONV_W, 3 * D_DN)
    conv_full = jnp.pad(conv_full, ((0, 0), (0, HALO - CONV_W), (0, 0)))
    sgu_b_t = jnp.swapaxes(sgu_b, 1, 2)

    def rows_of(gates, first_lane):
        return gates[:, first_lane:first_lane + N_GROUPS].reshape(nc, DN_CHUNK, N_GROUPS).transpose(0, 2, 1)

    saved = []
    xl = x0
    for l in range(depth):
        w_in_t = gathered_in.reshape(D_IN, D_MODEL)
        w_gate_t = jnp.pad(w_in_t[D_MAIN:], ((0, LANE - N_GATE), (0, 0)))
        w_out_l = gathered_out
        nw = norm_w[l].reshape(1, D_MODEL)
        h, h_t = _rms_fwd(xl, nw)
        proj = _matmul([(_lhs(h, tm=MM_WIDE), _rhs(w_in_t, MM_WIDE, transposed=True))], s, D_MAIN, name="in_proj", trans_b=True, tn=MM_WIDE)
        p_gate = _matmul([(_lhs(h), _rhs(w_gate_t, LANE, transposed=True))], s, LANE, name="in_proj_gate", trans_b=True, tn=LANE)
        ln_g, ln_b = sgu_ln_g[l].reshape(1, D_SGU), sgu_ln_b[l].reshape(1, D_SGU)
        y, y_t = _sgu_fwd(proj, ln_g, ln_b, sgu_w[l], sgu_b_t[l])
        a_log = _lane_row(a_log_f[l], a_log_b[l])
        dt_bias = _lane_row(dt_bias_f[l], dt_bias_b[l])
        q, k, v, gates = _prep_fwd(proj, conv_full[l], p_gate, a_log, dt_bias)
        rows_f, rows_b = rows_of(gates, 0), rows_of(gates, N_GROUPS)
        more = l + 1 < depth
        (o_f, st_f, t_f), (o_b, st_b, t_b), next_weights = _delta_fwd(q, k, v, gates, rows_f, rows_b,
                                                                     comm=gather_plan(l + 1) if more else None)
        dn_w = dn_norm_w[l].reshape(1, HEAD)
        y, y_t = _post_fwd(o_f, o_b, proj, dn_w, y, y_t)
        x_next = _matmul([(_lhs(y, tm=MM_WIDE), _rhs(w_out_l, MM_WIDE))], s, D_MODEL, name="out_proj", add=xl, tn=MM_WIDE)
        saved.append(dict(x=xl, nw=nw, h_t=h_t, proj=proj, p_gate=p_gate, ln_g=ln_g, ln_b=ln_b, a_log=a_log, dt_bias=dt_bias,
                          q=q, k=k, v=v, gates=gates, rows_f=rows_f, rows_b=rows_b, o_f=o_f, o_b=o_b, st_f=st_f, st_b=st_b,
                          t_f=t_f, t_b=t_b, dn_w=dn_w, y_t=y_t, w_in_t=w_in_t, w_gate_t=w_gate_t, w_out=w_out_l))
        xl = x_next
        if more:
            gathered_in, gathered_out = next_weights

    loss_part, dx, dx_bf, d_final = _final_loss(xl, final_norm_w.reshape(1, D_MODEL), target)
    loss = lax.psum(loss_part[0, 0], MESH_AXES)

    t_rows = -(-D_IN // LANE) * LANE
    layer_small = [n for n in SMALL if n != "final_norm_w"]
    n_layer_small = sum(int(np.prod(weights[n].shape[1:])) for n in layer_small) + D_MODEL
    small_rows = -(-n_layer_small // (FLAT_W * HALO)) * HALO

    def small_segments(values):
        top = values["final_norm_w"]
        return jnp.concatenate([_to_flat([values[n][l] for n in layer_small] + [top if l == depth - 1 else jnp.zeros_like(top)], HALO)
                                for l in range(depth)], axis=0)

    def small_to_send(layer_grads, top_grad):
        conv = layer_grads["conv_w"].reshape(CONV_W, N_DEV, conv_shard).transpose(1, 0, 2).reshape(N_DEV, -1)
        rest = jnp.concatenate([layer_grads[n].reshape(-1) for n in layer_small[1:]] + [top_grad])
        flat = jnp.concatenate([conv, jnp.broadcast_to(rest, (N_DEV, rest.shape[0]))], axis=1)
        return jnp.pad(flat, ((0, 0), (0, small_rows * FLAT_W - flat.shape[1]))).reshape(N_DEV, small_rows, FLAT_W)

    layer_slot = lambda l: (lambda ref, dev: ref.at[dev, l])
    small_slot = lambda l: (lambda ref, dev: ref.at[dev, pl.ds(l * small_rows, small_rows)])
    parts = [jax.ShapeDtypeStruct((N_DEV, depth, in_shard, SUBLANES, LANE), BF16),
             jax.ShapeDtypeStruct((N_DEV, depth, out_shard, D_MODEL), BF16),
             jax.ShapeDtypeStruct((N_DEV, depth * small_rows, FLAT_W), F32)]

    def exchange_plan(items):
        source = (_row_block(in_shard, 0), _row_block(out_shard, 0), _slot)
        target = (layer_slot, layer_slot, small_slot)
        return _Comm([a for _, _, a in items], parts,
                     [(i, which, source[which], target[which](layer)) for i, (which, layer, _) in enumerate(items)])

    pending = []
    for l in reversed(range(depth)):
        sv = saved[l]
        dy = _matmul([(_lhs(dx_bf, tm=MM_WIDE), _rhs(sv["w_out"], MM_WIDE, transposed=True))], s, D_MODEL, name="out_proj_dy", trans_b=True,
                     tn=MM_WIDE)
        g_out = _matmul([(_lhs(sv["y_t"]), _rhs(dx_bf, MM_TILE))], D_MODEL, D_MODEL, name="out_proj_dw", out_dtype=BF16)
        do, dp, d_dn = _post_bwd(sv["o_f"], sv["o_b"], sv["proj"], sv["dn_w"], dy)
        (*dqkv_f, dgate_f, drows_f), (*dqkv_b, dgate_b, drows_b), exchanged = _delta_bwd(
            sv["q"], sv["k"], sv["v"], sv["gates"], sv["rows_f"], sv["rows_b"], sv["st_f"], sv["st_b"], sv["t_f"], sv["t_b"], do,
            comm=exchange_plan([(1, l, g_out)] + pending))
        parts = list(exchanged)
        drows = jnp.concatenate([drows_f.transpose(0, 2, 1).reshape(s, N_GROUPS), drows_b.transpose(0, 2, 1).reshape(s, N_GROUPS)], axis=1)
        drows = jnp.pad(drows, ((0, 0), (0, LANE - 2 * N_GROUPS)))
        dp, d_gate, d_conv, d_alog, d_dt = _prep_bwd(sv["proj"], conv_full[l], sv["p_gate"], sv["a_log"], sv["dt_bias"],
                                                     dqkv_f, dqkv_b, (dgate_f, dgate_b, drows), dp)
        dp, d_lg, d_lb, d_sw, d_sbt = _sgu_bwd(sv["proj"], sv["ln_g"], sv["ln_b"], sgu_w[l], sgu_b_t[l], dy, dp)
        h_t = sv["h_t"]
        g_in_t = _matmul([(_lhs(h_t), _rhs(dp, MM_TILE))], D_MODEL, D_MAIN, name="in_proj_dw", out_dtype=BF16, out_t_rows=t_rows)
        g_in_t = _matmul([(_lhs(h_t), _rhs(d_gate, LANE))], D_MODEL, LANE, name="in_proj_dw_gate", out_dtype=BF16, tn=LANE,
                         transpose_into=g_in_t, out_row_block=D_MAIN // LANE)
        g_in_rows = _rows_view(g_in_t)
        dh_pairs = [(_lhs(dp), _rhs(sv["w_in_t"], MM_TILE, k=D_MAIN)), (_lhs(d_gate), _rhs(sv["w_gate_t"], MM_TILE))]
        if l > 0:
            dh = _matmul(dh_pairs, s, D_MODEL, name="in_proj_dh")
        else:
            dh, exchanged = _matmul(dh_pairs, s, D_MODEL, name="in_proj_dh_comm", comm=exchange_plan([(0, l, g_in_rows)]))
            parts = list(exchanged)
        dx, dx_bf, d_nw = _rms_bwd(sv["x"], sv["nw"], dh, dx)
        layer_grads = dict(
            conv_w=d_conv[:CONV_W], norm_w=d_nw, sgu_ln_g=d_lg, sgu_ln_b=d_lb, sgu_w=d_sw, sgu_b=d_sbt.T,
            a_log_f=d_alog[0, :N_GROUPS], a_log_b=d_alog[0, N_GROUPS:2 * N_GROUPS],
            dt_bias_f=d_dt[0, :N_GROUPS], dt_bias_b=d_dt[0, N_GROUPS:2 * N_GROUPS], dn_norm_w=d_dn)
        small = small_to_send(layer_grads, d_final.reshape(D_MODEL) if l == depth - 1 else jnp.zeros((D_MODEL,), F32))
        pending = [(2, l, small)] + ([(0, l, g_in_rows)] if l > 0 else [])
    grad_x = dx.reshape(1, s, D_MODEL)

    parts_in, parts_out, parts_small = parts
    in_rows = 100 if in_shard % 100 == 0 else in_shard
    (_, bottom, bottom_small), = pending
    g_in_t, (parts_small,) = _sum_parts("sum_in_comm", parts_in, (1, in_rows, SUBLANES, LANE), comm=_Comm(
        [bottom_small], [parts_small], [(0, 0, _slot, small_slot(bottom))]))
    g_in = jnp.swapaxes(g_in_t.reshape(depth, in_shard, D_MODEL), 1, 2)
    res_in = _adamw("adamw_in", g_in[None], w_in, m_w_in, v_w_in, (1, min(256, D_MODEL), in_shard))
    res_out = _adamw("adamw_out", parts_out, w_out, m_w_out, v_w_out, (1, min(128, out_shard), D_MODEL))
    res_small = _adamw("adamw_small", parts_small, small_segments(weights), small_segments(m_in), small_segments(v_in),
                       (small_rows, FLAT_W))
    outs = [dict(w_in=a, w_out=b) for a, b in zip(res_in, res_out)]
    for o, d in zip(res_small, outs):
        by_layer, off = o.reshape(depth, small_rows * FLAT_W), 0
        for n in layer_small:
            size = int(np.prod(weights[n].shape[1:]))
            d[n] = by_layer[:, off:off + size].reshape(weights[n].shape)
            off += size
        d["final_norm_w"] = by_layer[depth - 1, off:off + D_MODEL]
    g_out, delta_out, m_out, v_out = outs
    return (loss, grad_x, *[g_out[n] for n in WEIGHTS], *[delta_out[n] for n in WEIGHTS],
            *[m_out[n] for n in WEIGHTS], *[v_out[n] for n in WEIGHTS])
```

```python
import functools

import numpy as np
import jax
import jax.numpy as jnp
from jax import lax
from jax.experimental import pallas as pl
from jax.experimental.pallas import tpu as pltpu

F32 = jnp.float32
BF16 = jnp.bfloat16

N_DEV = 8
D_MODEL = 2048
D_SGU = 1024
D_DN = 1024
N_GROUPS = 8
HEAD = 128
SGU_CHUNK = 128
DN_CHUNK = 64
CONV_W = 5
N_GATE = 32
D_MAIN = 3 * D_SGU + 4 * D_DN
D_IN = D_MAIN + N_GATE
LANE = 128
SUBLANES = 16
HALO = 8
EPS = 1e-6
ADAM_LR, ADAM_B1, ADAM_B2, ADAM_EPS, ADAM_WD, ADAM_STEP = 0.001, 0.9, 0.999, 1e-08, 0.01, 10
FLAT_W = 1024
MM_TILE = 512
MM_WIDE = 1024
VMEM_MARGIN = 8 << 20

MESH_AXES = ("x", "y", "c")
WEIGHTS = ("norm_w", "w_in", "sgu_ln_g", "sgu_ln_b", "sgu_w", "sgu_b", "conv_w", "a_log_f", "a_log_b",
           "dt_bias_f", "dt_bias_b", "dn_norm_w", "w_out", "final_norm_w")
SMALL = ("conv_w",) + tuple(n for n in WEIGHTS if n not in ("w_in", "w_out", "conv_w"))


def _bdot(a, b, dims):
    return lax.dot_general(a.astype(BF16), b.astype(BF16), (dims, ((), ())), preferred_element_type=F32)


@jax.custom_vjp
def _mm_nn(a, b):
    return _bdot(a, b, ((1,), (0,)))


def _mm_nn_fwd(a, b):
    return _mm_nn(a, b), (a, b)


def _mm_nn_bwd(res, ct):
    a, b = res
    return _bdot(ct, b, ((1,), (1,))), _bdot(a, ct, ((0,), (0,)))


_mm_nn.defvjp(_mm_nn_fwd, _mm_nn_bwd)


@jax.custom_vjp
def _mm_nt(a, b):
    return _bdot(a, b, ((1,), (1,)))


def _mm_nt_fwd(a, b):
    return _mm_nt(a, b), (a, b)


def _mm_nt_bwd(res, ct):
    a, b = res
    return _bdot(ct, b, ((1,), (0,))), _bdot(ct, a, ((0,), (0,)))


_mm_nt.defvjp(_mm_nt_fwd, _mm_nt_bwd)


@jax.custom_vjp
def _inverse_given(m, t):
    return t


def _inverse_given_fwd(m, t):
    return t, t


def _inverse_given_bwd(t, ct):
    dm = -_bdot(t, _bdot(ct, t, ((1,), (1,))), ((0,), (0,)))
    return dm, jnp.zeros_like(t)


_inverse_given.defvjp(_inverse_given_fwd, _inverse_given_bwd)


@jax.custom_vjp
def _split_lanes(x):
    return tuple(x[:, i * LANE:(i + 1) * LANE] for i in range(x.shape[1] // LANE))


def _split_lanes_fwd(x):
    return _split_lanes(x), None


def _split_lanes_bwd(_, cts):
    return (jnp.concatenate(cts, axis=1),)


_split_lanes.defvjp(_split_lanes_fwd, _split_lanes_bwd)


@functools.partial(jax.custom_vjp, nondiff_argnums=(1,))
def _split_rows(x, n):
    return x[:n], x[n:]


def _split_rows_fwd(x, n):
    return _split_rows(x, n), None


def _split_rows_bwd(n, _, cts):
    return (jnp.concatenate(cts, axis=0),)


_split_rows.defvjp(_split_rows_fwd, _split_rows_bwd)


def _silu(t):
    return t * jax.nn.sigmoid(t)


def _gelu(t):
    return 0.5 * t * (1.0 + lax.erf(t * np.float32(0.7071067811865476)))


def _rms(x, w):
    return x * lax.rsqrt(jnp.mean(x * x, axis=-1, keepdims=True) + EPS) * w


def _params(sem, vmem_bytes=None):
    kw = dict(dimension_semantics=sem)
    if vmem_bytes is not None:
        kw["vmem_limit_bytes"] = int(vmem_bytes)
    return pltpu.CompilerParams(**kw)


def _lhs(a, k=None, k_block=0, tm=MM_TILE):
    k = a.shape[1] if k is None else k
    tm = min(tm, a.shape[0])
    return a, pl.BlockSpec((tm, k), lambda i, j: (i, k_block)), tm * k * a.dtype.itemsize


def _rhs(b, tn, *, layer=None, transposed=False, k=None, k_block=0, n_offset=0):
    lead = () if layer is None else (layer,)
    none = () if layer is None else (None,)
    if transposed:
        k = b.shape[-1]
        spec = pl.BlockSpec(none + (tn, k), lambda i, j: lead + (j + n_offset, 0))
    else:
        k = b.shape[-2] if k is None else k
        spec = pl.BlockSpec(none + (k, tn), lambda i, j: lead + (k_block, j))
    return b, spec, tn * k * b.dtype.itemsize


def _matmul(pairs, m, n, *, name, trans_b=False, add=None, out_dtype=F32, tn=MM_TILE, out_t_rows=None, transpose_into=None,
            out_row_block=0, comm=None):
    tm = pairs[0][0][1].block_shape[0]
    assert m % tm == 0 and n % tn == 0 and all(a_spec.block_shape[0] == tm for (_, a_spec, _), _ in pairs)
    n_pairs = len(pairs)
    contract = ((1,), (1,)) if trans_b else ((1,), (0,))
    transposed_out = out_t_rows is not None or transpose_into is not None

    def body(*refs):
        o_ref = refs[-1]
        acc = None
        for i in range(n_pairs):
            d = lax.dot_general(refs[2 * i][...].astype(BF16), refs[2 * i + 1][...].astype(BF16),
                                (contract, ((), ())), preferred_element_type=F32)
            acc = d if acc is None else acc + d
        if add is not None:
            acc = acc + refs[2 * n_pairs][...]
        o_ref[...] = (acc.T if transposed_out else acc).astype(out_dtype)

    in_specs, args, vmem, aliases = [], [], 0, {}
    for (a, a_spec, a_bytes), (b, b_spec, b_bytes) in pairs:
        in_specs += [a_spec, b_spec]
        args += [a, b]
        vmem += 2 * (a_bytes + b_bytes)
    if add is not None:
        in_specs.append(pl.BlockSpec((tm, tn), lambda i, j: (i, j)))
        args.append(add)
        vmem += 2 * tm * tn * 4
    vmem += 2 * tm * tn * jnp.dtype(out_dtype).itemsize + 3 * tm * tn * 4
    if transposed_out:
        out_spec = pl.BlockSpec((tn, tm), lambda i, j: (j + out_row_block, i))
        if transpose_into is not None:
            in_specs.append(pl.BlockSpec(memory_space=pl.ANY))
            args.append(transpose_into)
            aliases = {len(args) - 1: 0}
            out_shape = jax.ShapeDtypeStruct(transpose_into.shape, out_dtype)
        else:
            out_shape = jax.ShapeDtypeStruct((out_t_rows, m), out_dtype)
    else:
        out_spec = pl.BlockSpec((tm, tn), lambda i, j: (i, j))
        out_shape = jax.ShapeDtypeStruct((m, n), out_dtype)
    if comm is not None:
        assert not aliases
        (out,), sent = _call_with_comm(
            body, comm, name=name, grid=(m // tm, n // tn), in_specs=in_specs, out_specs=[out_spec], out_shape=[out_shape],
            scratch_shapes=[], args=args, compiler_params=_params(("arbitrary", "arbitrary"), vmem + VMEM_MARGIN))
        return out, sent
    return pl.pallas_call(
        body, name=name, grid=(m // tm, n // tn), in_specs=in_specs, out_specs=out_spec, out_shape=out_shape,
        input_output_aliases=aliases, compiler_params=_params(("parallel", "arbitrary"), vmem + VMEM_MARGIN),
    )(*args)


def _rms_fwd(x, w, *, tile=512):
    s, d = x.shape
    tile = min(tile, s)

    def body(x_ref, w_ref, h_ref, ht_ref):
        h = _rms(x_ref[...], w_ref[...])
        h_ref[...] = h.astype(BF16)
        ht_ref[...] = h.T.astype(BF16)

    return pl.pallas_call(
        body, name="rms_fwd", grid=(s // tile,),
        in_specs=[pl.BlockSpec((tile, d), lambda i: (i, 0)), pl.BlockSpec((1, d), lambda i: (0, 0))],
        out_specs=[pl.BlockSpec((tile, d), lambda i: (i, 0)), pl.BlockSpec((d, tile), lambda i: (0, i))],
        out_shape=[jax.ShapeDtypeStruct((s, d), BF16), jax.ShapeDtypeStruct((d, s), BF16)],
        compiler_params=_params(("arbitrary",)),
    )(x, w)


def _rms_bwd(x, w, dh, dx_out, *, tile=256):
    s, d = x.shape
    tile = min(tile, s)

    def body(x_ref, w_ref, dh_ref, dxo_ref, dx_ref, dxb_ref, dw_ref):
        _, vjp = jax.vjp(_rms, x_ref[...], w_ref[...])
        dxv, dwv = vjp(dh_ref[...])
        dx = dxo_ref[...] + dxv
        dx_ref[...] = dx
        dxb_ref[...] = dx.astype(BF16)

        @pl.when(pl.program_id(0) == 0)
        def _():
            dw_ref[...] = jnp.zeros_like(dw_ref)

        dw_ref[...] += dwv

    row = pl.BlockSpec((tile, d), lambda i: (i, 0))
    vec = pl.BlockSpec((1, d), lambda i: (0, 0))
    return pl.pallas_call(
        body, name="rms_bwd", grid=(s // tile,),
        in_specs=[row, vec, row, row], out_specs=[row, row, vec],
        out_shape=[jax.ShapeDtypeStruct((s, d), F32), jax.ShapeDtypeStruct((s, d), BF16), jax.ShapeDtypeStruct((1, d), F32)],
        compiler_params=_params(("arbitrary",)),
    )(x, w, dh, dx_out)


def _final_loss(x, w, target, *, tile=256):
    s, d = x.shape
    tile = min(tile, s)

    def body(x_ref, w_ref, t_ref, loss_ref, dx_ref, dxb_ref, dw_ref):
        def f(xv, wv):
            err = jnp.square(_rms(xv, wv) - t_ref[...])
            per_token = jnp.sum(err, axis=1, keepdims=True) * np.float32(1.0 / d)
            return 0.5 * jnp.sum(per_token, axis=0, keepdims=True)

        loss, vjp = jax.vjp(f, x_ref[...], w_ref[...])
        dxv, dwv = vjp(jnp.ones((1, 1), F32))
        dx_ref[...] = dxv
        dxb_ref[...] = dxv.astype(BF16)

        @pl.when(pl.program_id(0) == 0)
        def _():
            dw_ref[...] = jnp.zeros_like(dw_ref)
            loss_ref[...] = jnp.zeros_like(loss_ref)

        dw_ref[...] += dwv
        loss_ref[...] += jnp.broadcast_to(loss, (1, LANE))

    row = pl.BlockSpec((tile, d), lambda i: (i, 0))
    vec = pl.BlockSpec((1, d), lambda i: (0, 0))
    return pl.pallas_call(
        body, name="final_loss", grid=(s // tile,),
        in_specs=[row, vec, row], out_specs=[pl.BlockSpec((1, LANE), lambda i: (0, 0)), row, row, vec],
        out_shape=[jax.ShapeDtypeStruct((1, LANE), F32), jax.ShapeDtypeStruct((s, d), F32),
                   jax.ShapeDtypeStruct((s, d), BF16), jax.ShapeDtypeStruct((1, d), F32)],
        compiler_params=_params(("arbitrary",)),
    )(x, w, target)


def _sgu_chunk(u, v, z, ln_g, ln_b, ws, bcols):
    vg = _gelu(v)
    xc = vg - jnp.mean(vg, axis=-1, keepdims=True)
    vl = xc * lax.rsqrt(jnp.mean(xc * xc, axis=-1, keepdims=True) + EPS) * ln_g + ln_b
    sp = jnp.concatenate([_mm_nn(ws[g], vg_) + bcols[g] for g, vg_ in enumerate(_split_lanes(vl))], axis=1)
    return _gelu(u) * sp * _silu(z)


def _sgu_specs(tile):
    col = lambda j: pl.BlockSpec((tile, D_SGU), lambda i, j=j: (i, j))
    vec = pl.BlockSpec((1, D_SGU), lambda i: (0, 0))
    wspec = pl.BlockSpec((N_GROUPS, SGU_CHUNK, SGU_CHUNK), lambda i: (0, 0, 0))
    bspec = pl.BlockSpec((SGU_CHUNK, N_GROUPS), lambda i: (0, 0))
    return col, vec, wspec, bspec


def _sgu_fwd(proj, ln_g, ln_b, w_s, b_t, *, tile=256):
    s = proj.shape[0]
    tile = min(tile, s)
    col, vec, wspec, bspec = _sgu_specs(tile)

    def body(u_ref, v_ref, z_ref, g_ref, b_ref, w_ref, bt_ref, y_ref, yt_ref):
        ws = tuple(w_ref[g] for g in range(N_GROUPS))
        bcols = tuple(bt_ref[:, g:g + 1] for g in range(N_GROUPS))
        for c in range(tile // SGU_CHUNK):
            rows = pl.ds(c * SGU_CHUNK, SGU_CHUNK)
            y = _sgu_chunk(u_ref[rows, :], v_ref[rows, :], z_ref[rows, :], g_ref[...], b_ref[...], ws, bcols)
            y_ref[rows, :] = y.astype(BF16)
            yt_ref[:, rows] = y.T.astype(BF16)

    return pl.pallas_call(
        body, name="sgu_fwd", grid=(s // tile,),
        in_specs=[col(0), col(1), col(2), vec, vec, wspec, bspec],
        out_specs=[pl.BlockSpec((tile, D_SGU), lambda i: (i, 0)), pl.BlockSpec((D_SGU, tile), lambda i: (0, i))],
        out_shape=[jax.ShapeDtypeStruct((s, D_SGU + D_DN), BF16), jax.ShapeDtypeStruct((D_SGU + D_DN, s), BF16)],
        compiler_params=_params(("arbitrary",)),
    )(proj, proj, proj, ln_g, ln_b, w_s, b_t)


def _sgu_bwd(proj, ln_g, ln_b, w_s, b_t, dy, dp, *, tile=128):
    s = proj.shape[0]
    tile = min(tile, s)
    col, vec, wspec, bspec = _sgu_specs(tile)

    def body(u_ref, v_ref, z_ref, g_ref, b_ref, w_ref, bt_ref, dy_ref, _, dp_ref, dg_ref, db_ref, dw_ref, dbt_ref):
        @pl.when(pl.program_id(0) == 0)
        def _():
            dg_ref[...] = jnp.zeros_like(dg_ref)
            db_ref[...] = jnp.zeros_like(db_ref)
            dw_ref[...] = jnp.zeros_like(dw_ref)
            dbt_ref[...] = jnp.zeros_like(dbt_ref)

        ws = tuple(w_ref[g] for g in range(N_GROUPS))
        bcols = tuple(bt_ref[:, g:g + 1] for g in range(N_GROUPS))
        for c in range(tile // SGU_CHUNK):
            rows = pl.ds(c * SGU_CHUNK, SGU_CHUNK)
            _, vjp = jax.vjp(_sgu_chunk, u_ref[rows, :], v_ref[rows, :], z_ref[rows, :], g_ref[...], b_ref[...], ws, bcols)
            du, dv, dz, dg, db, dws, dbc = vjp(dy_ref[rows, :])
            dp_ref[rows, pl.ds(0, D_SGU)] = du.astype(BF16)
            dp_ref[rows, pl.ds(D_SGU, D_SGU)] = dv.astype(BF16)
            dp_ref[rows, pl.ds(2 * D_SGU, D_SGU)] = dz.astype(BF16)
            dg_ref[...] += dg
            db_ref[...] += db
            for g in range(N_GROUPS):
                dw_ref[g] += dws[g]
                dbt_ref[:, g:g + 1] += dbc[g]

    return pl.pallas_call(
        body, name="sgu_bwd", grid=(s // tile,),
        in_specs=[col(0), col(1), col(2), vec, vec, wspec, bspec, pl.BlockSpec((tile, D_SGU), lambda i: (i, 0)),
                  pl.BlockSpec(memory_space=pl.ANY)],
        out_specs=[pl.BlockSpec((tile, 3 * D_SGU), lambda i: (i, 0)), vec, vec, wspec, bspec],
        out_shape=[jax.ShapeDtypeStruct(dp.shape, BF16), jax.ShapeDtypeStruct((1, D_SGU), F32),
                   jax.ShapeDtypeStruct((1, D_SGU), F32), jax.ShapeDtypeStruct((N_GROUPS, SGU_CHUNK, SGU_CHUNK), F32),
                   jax.ShapeDtypeStruct((SGU_CHUNK, N_GROUPS), F32)],
        input_output_aliases={8: 0},
        compiler_params=_params(("arbitrary",)),
    )(proj, proj, proj, ln_g, ln_b, w_s, b_t, dy, dp)


QKV_BLOCK = 1


def _qkv_head(cq, ck, cv):
    q, k, v = _silu(cq), _silu(ck), _silu(cv)
    q = q * lax.rsqrt(jnp.sum(q * q, axis=-1, keepdims=True) + EPS) * np.float32(HEAD ** -0.5)
    k = k * lax.rsqrt(jnp.sum(k * k, axis=-1, keepdims=True) + EPS)
    return q, k, v


def _gate_fn(p, a_log, dt_bias):
    lane = lax.broadcasted_iota(jnp.int32, p.shape, 1)
    g = -jnp.exp(a_log) * jax.nn.softplus(p + dt_bias)
    return jnp.where(lane < 16, g, jnp.where(lane < N_GATE, jax.nn.sigmoid(p), 0.0))


def _shifted_rows(full, shift, start, rows):
    n = full.shape[0]
    assert start % HALO == 0 and 0 <= start + shift and start + shift + rows <= n
    rolled = full if shift == 0 else pltpu.roll(full, (-shift) % n, 0)
    return rolled[start:start + rows, :]


def _halo_maps(tile, s):
    per = tile // HALO
    prev = lambda i: jnp.maximum(i * per - 1, 0)
    nxt = lambda i: jnp.minimum((i + 1) * per, s // HALO - 1)
    return prev, nxt


def _prep_fwd(proj, conv_w, pg, a_log, dt_bias, *, tile=256):
    s = proj.shape[0]
    width = 3 * D_DN
    tile = min(tile, s)
    n_blk = s // tile
    prev, nxt = _halo_maps(tile, s)

    def body(x_ref, xp_ref, xn_ref, w_ref, pg_ref, al_ref, dt_ref, q_ref, k_ref, v_ref, gate_ref, ext):
        i = pl.program_id(0)
        ext[pl.ds(0, HALO), :] = jnp.where(i > 0, xp_ref[...], 0.0)
        ext[pl.ds(HALO, tile), :] = x_ref[...]
        ext[pl.ds(HALO + tile, HALO), :] = jnp.where(i < n_blk - 1, xn_ref[...], 0.0)
        full, c = ext[...], None
        for j in range(CONV_W):
            term = _shifted_rows(full, j - CONV_W // 2, HALO, tile) * w_ref[j:j + 1, :]
            c = term if c is None else c + term
        for h in range(N_GROUPS):
            q, k, v = _qkv_head(c[:, h * HEAD:(h + 1) * HEAD], c[:, D_DN + h * HEAD:D_DN + (h + 1) * HEAD],
                                c[:, 2 * D_DN + h * HEAD:2 * D_DN + (h + 1) * HEAD])
            q_ref[h] = q
            k_ref[h] = k
            v_ref[h] = v
        gate_ref[...] = _gate_fn(pg_ref[...], al_ref[...], dt_ref[...])

    hm = pl.BlockSpec((N_GROUPS, tile, HEAD), lambda i: (0, i, 0))
    lane_vec = pl.BlockSpec((1, LANE), lambda i: (0, 0))
    return pl.pallas_call(
        body, name="prep_fwd", grid=(n_blk,),
        in_specs=[pl.BlockSpec((tile, width), lambda i: (i, QKV_BLOCK)),
                  pl.BlockSpec((HALO, width), lambda i: (prev(i), QKV_BLOCK)),
                  pl.BlockSpec((HALO, width), lambda i: (nxt(i), QKV_BLOCK)),
                  pl.BlockSpec((HALO, width), lambda i: (0, 0)),
                  pl.BlockSpec((tile, LANE), lambda i: (i, 0)), lane_vec, lane_vec],
        out_specs=[hm, hm, hm, pl.BlockSpec((tile, LANE), lambda i: (i, 0))],
        out_shape=[jax.ShapeDtypeStruct((N_GROUPS, s, HEAD), F32)] * 3 + [jax.ShapeDtypeStruct((s, LANE), F32)],
        scratch_shapes=[pltpu.VMEM((tile + 2 * HALO, width), F32)],
        compiler_params=_params(("arbitrary",)),
    )(proj, proj, proj, conv_w, pg, a_log, dt_bias)


def _prep_bwd(proj, conv_w, pg, a_log, dt_bias, dqkv_f, dqkv_b, dgates, dp, *, tile=128):
    s = proj.shape[0]
    width = 3 * D_DN
    tile = min(tile, s)
    n_blk = s // tile
    prev, nxt = _halo_maps(tile, s)
    ext_rows = tile + 2 * HALO
    pad = CONV_W // 2

    def body(*refs):
        x_ref, xp_ref, xn_ref, w_ref, pg_ref, al_ref, dt_ref = refs[:7]
        ct_refs = refs[7:25]
        dg0_ref, dg1_ref, dg2_ref = refs[25:28]
        dp_ref, dpg_ref, dw_ref, dal_ref, ddt_ref, xext, dcext = refs[29:]
        i = pl.program_id(0)
        has_prev, has_next = i > 0, i < n_blk - 1

        @pl.when(i == 0)
        def _():
            dw_ref[...] = jnp.zeros_like(dw_ref)
            dal_ref[...] = jnp.zeros_like(dal_ref)
            ddt_ref[...] = jnp.zeros_like(ddt_ref)

        zeros = jnp.zeros((HALO, width), F32)
        xext[pl.ds(0, HALO), :] = zeros
        xext[pl.ds(HALO, HALO), :] = jnp.where(has_prev, xp_ref[...], 0.0)
        xext[pl.ds(2 * HALO, tile), :] = x_ref[...]
        xext[pl.ds(2 * HALO + tile, HALO), :] = jnp.where(has_next, xn_ref[...], 0.0)
        xext[pl.ds(3 * HALO + tile, HALO), :] = zeros
        xfull, c = xext[...], None
        for j in range(CONV_W):
            term = _shifted_rows(xfull, j - pad, HALO, ext_rows) * w_ref[j:j + 1, :]
            c = term if c is None else c + term
        for h in range(N_GROUPS):
            lanes = [pl.ds(part * D_DN + h * HEAD, HEAD) for part in range(3)]
            cts = []
            for part in range(3):
                (cur, before, after), (cur2, before2, after2) = ct_refs[3 * part:3 * part + 3], ct_refs[9 + 3 * part:12 + 3 * part]
                cts.append(jnp.concatenate([jnp.where(has_prev, before[h] + before2[h], 0.0), cur[h] + cur2[h],
                                            jnp.where(has_next, after[h] + after2[h], 0.0)], axis=0))
            _, vjp = jax.vjp(_qkv_head, c[:, h * HEAD:(h + 1) * HEAD], c[:, D_DN + h * HEAD:D_DN + (h + 1) * HEAD],
                             c[:, 2 * D_DN + h * HEAD:2 * D_DN + (h + 1) * HEAD])
            for lane_sl, dc in zip(lanes, vjp(tuple(cts))):
                dcext[:, lane_sl] = dc
        dcfull = dcext[...]
        dx, dc_blk = None, dcfull[HALO:HALO + tile, :]
        for j in range(CONV_W):
            term = _shifted_rows(dcfull, pad - j, HALO, tile) * w_ref[j:j + 1, :]
            dx = term if dx is None else dx + term
            dw_ref[j:j + 1, :] += jnp.sum(dc_blk * _shifted_rows(xfull, j - pad, 2 * HALO, tile), axis=0, keepdims=True)
        dp_ref[...] = dx.astype(BF16)
        _, gvjp = jax.vjp(_gate_fn, pg_ref[...], al_ref[...], dt_ref[...])
        dpg, dal, ddt = gvjp(dg0_ref[...] + dg1_ref[...] + dg2_ref[...])
        dpg_ref[...] = dpg.astype(BF16)
        dal_ref[...] += dal
        ddt_ref[...] += ddt

    xs = [pl.BlockSpec((tile, width), lambda i: (i, QKV_BLOCK)), pl.BlockSpec((HALO, width), lambda i: (prev(i), QKV_BLOCK)),
          pl.BlockSpec((HALO, width), lambda i: (nxt(i), QKV_BLOCK))]
    hm = [pl.BlockSpec((N_GROUPS, tile, HEAD), lambda i: (0, i, 0)),
          pl.BlockSpec((N_GROUPS, HALO, HEAD), lambda i: (0, prev(i), 0)),
          pl.BlockSpec((N_GROUPS, HALO, HEAD), lambda i: (0, nxt(i), 0))]
    gate = pl.BlockSpec((tile, LANE), lambda i: (i, 0))
    lane_vec = pl.BlockSpec((1, LANE), lambda i: (0, 0))
    wspec = pl.BlockSpec((HALO, width), lambda i: (0, 0))
    n_in = 3 + 4 + 18 + 3
    ct_arrays = [a for dqkv in (dqkv_f, dqkv_b) for a in dqkv for _ in range(3)]
    return pl.pallas_call(
        body, name="prep_bwd", grid=(n_blk,),
        in_specs=xs + [wspec, gate, lane_vec, lane_vec] + hm * 6 + [gate] * 3 + [pl.BlockSpec(memory_space=pl.ANY)],
        out_specs=[pl.BlockSpec((tile, width), lambda i: (i, QKV_BLOCK)), gate, wspec, lane_vec, lane_vec],
        out_shape=[jax.ShapeDtypeStruct(dp.shape, BF16), jax.ShapeDtypeStruct((s, LANE), BF16),
                   jax.ShapeDtypeStruct((HALO, width), F32), jax.ShapeDtypeStruct((1, LANE), F32),
                   jax.ShapeDtypeStruct((1, LANE), F32)],
        input_output_aliases={n_in: 0},
        scratch_shapes=[pltpu.VMEM((tile + 4 * HALO, width), F32), pltpu.VMEM((ext_rows, width), F32)],
        compiler_params=_params(("arbitrary",)),
    )(proj, proj, proj, conv_w, pg, a_log, dt_bias, *ct_arrays, *dgates, dp)


def _each(f, *cols):
    return tuple(f(*a) for a in zip(*cols))


def _delta_chunk(q, k, v, g_col, g_row, beta, state, t_known, reverse):
    n = DN_CHUNK
    ri = lax.broadcasted_iota(jnp.int32, (n, n), 0)
    ci = lax.broadcasted_iota(jnp.int32, (n, n), 1)
    masks = {False: (ri >= ci, ri <= ci, ri > ci), True: (ri <= ci, ri >= ci, ri < ci)}
    flags = tuple(reverse) if isinstance(reverse, (tuple, list)) else (reverse,) * len(q)
    incl, incl_t, strict = (tuple(masks[f][i] for f in flags) for i in range(3))
    gc_col = _each(lambda gr, mk: jnp.sum(jnp.where(mk, gr, 0.0), axis=1, keepdims=True), g_row, incl)
    gc_row = _each(lambda gc, mk: jnp.sum(jnp.where(mk, gc, 0.0), axis=0, keepdims=True), g_col, incl_t)
    g_tot = _each(lambda gr: jnp.sum(gr, axis=1, keepdims=True), g_row)
    decay = _each(lambda a, b, mk: jnp.where(mk, jnp.exp(jnp.where(mk, a - b, 0.0)), 0.0), gc_col, gc_row, incl)
    rows = lambda *xs: jnp.concatenate(xs, axis=0)
    kb = _each(lambda a, b: a * b, k, beta)
    kk_qk = _each(lambda kb_, q_, k_: _split_rows(_mm_nt(rows(kb_, q_), k_), n), kb, q, k)
    m = _each(lambda two, d, mk: jnp.where(mk, two[0] * d, 0.0), kk_qk, decay, strict)
    if t_known is None:
        eye = (ri == ci).astype(F32)
        p = _each(lambda a: -a, m)
        t = tuple(eye for _ in p)
        for _ in range(5):
            both = _each(lambda p_, t_: _bdot(rows(p_, t_), p_, ((1,), (0,))), p, t)
            p = _each(lambda b: b[:n], both)
            t = _each(lambda t_, b: t_ + b[n:], t, both)
        t = _each(lambda t_, p_: t_ + _bdot(t_, p_, ((1,), (0,))), t, p)
    else:
        t = _each(_inverse_given, m, t_known)
    e_gc = _each(jnp.exp, gc_col)
    u_w = _each(lambda t_, v_, b_, kb_, e_: _split_lanes(_mm_nn(t_, jnp.concatenate([v_ * b_, kb_ * e_], axis=1))), t, v, beta, kb, e_gc)
    attn = _each(lambda two, d_: two[1] * d_, kk_qk, decay)
    ws_qs = _each(lambda uw, q_, e_, s_: _split_rows(_mm_nn(rows(uw[1], q_ * e_), s_), n), u_w, q, e_gc, state)
    v_new = _each(lambda uw, wq: uw[0] - wq[0], u_w, ws_qs)
    k_dec_t = _each(lambda k_, gt, gc: jnp.transpose(k_ * jnp.exp(gt - gc)), k, g_tot, gc_col)
    av_kv = _each(lambda a_, kt, vn: _split_rows(_mm_nn(rows(a_, kt), vn), n), attn, k_dec_t, v_new)
    o = _each(lambda wq, ak: wq[1] + ak[0], ws_qs, av_kv)
    new_state = _each(lambda s_, gt, ak: s_ * jnp.exp(gt) + ak[1], state, g_tot, av_kv)
    return o, new_state, t


def _delta_lanes(reverse):
    return (N_GROUPS, 3 * N_GROUPS) if reverse else (0, 2 * N_GROUPS)


def _delta_fwd(q, k, v, gates, rows_f, rows_b, *, comm=None):
    s = q.shape[1]
    nc = s // DN_CHUNK
    heads = range(N_GROUPS)

    def body(qf_ref, kf_ref, vf_ref, gf_ref, rf_ref, qb_ref, kb_ref, vb_ref, gb_ref, rb_ref,
             of_ref, sf_ref, tf_ref, ob_ref, sb_ref, tb_ref, state):
        @pl.when(pl.program_id(0) == 0)
        def _():
            state[...] = jnp.zeros_like(state)

        qs, ks, vs, g_cols, g_rows, betas, flags = [], [], [], [], [], [], []
        for reverse, q_ref, k_ref, v_ref, gate, row_ref in ((False, qf_ref, kf_ref, vf_ref, gf_ref[...], rf_ref),
                                                            (True, qb_ref, kb_ref, vb_ref, gb_ref[...], rb_ref)):
            g_lane, b_lane = _delta_lanes(reverse)
            for j in heads:
                qs.append(q_ref[j])
                ks.append(k_ref[j])
                vs.append(v_ref[j])
                g_cols.append(gate[:, g_lane + j:g_lane + j + 1])
                g_rows.append(row_ref[0, j:j + 1, :])
                betas.append(gate[:, b_lane + j:b_lane + j + 1])
                flags.append(reverse)
        prev = tuple(state[n] for n in range(2 * N_GROUPS))
        o, new, t = _delta_chunk(tuple(qs), tuple(ks), tuple(vs), tuple(g_cols), tuple(g_rows), tuple(betas), prev, None, tuple(flags))
        for n in range(2 * N_GROUPS):
            o_ref, st_ref, t_ref = (of_ref, sf_ref, tf_ref) if n < N_GROUPS else (ob_ref, sb_ref, tb_ref)
            j = n % N_GROUPS
            st_ref[0, j] = prev[n]
            o_ref[j] = o[n]
            t_ref[0, j] = t[n]
            state[n] = new[n]

    def specs(chunk_of):
        hm = pl.BlockSpec((N_GROUPS, DN_CHUNK, HEAD), lambda c: (0, chunk_of(c), 0))
        ins = [hm, hm, hm, pl.BlockSpec((DN_CHUNK, LANE), lambda c: (chunk_of(c), 0)),
               pl.BlockSpec((1, N_GROUPS, DN_CHUNK), lambda c: (chunk_of(c), 0, 0))]
        outs = [hm, pl.BlockSpec((1, N_GROUPS, HEAD, HEAD), lambda c: (chunk_of(c), 0, 0, 0)),
                pl.BlockSpec((1, N_GROUPS, DN_CHUNK, DN_CHUNK), lambda c: (chunk_of(c), 0, 0, 0))]
        return ins, outs

    (ins_f, outs_f), (ins_b, outs_b) = specs(lambda c: c), specs(lambda c: nc - 1 - c)
    shapes = [jax.ShapeDtypeStruct((N_GROUPS, s, HEAD), F32), jax.ShapeDtypeStruct((nc, N_GROUPS, HEAD, HEAD), F32),
              jax.ShapeDtypeStruct((nc, N_GROUPS, DN_CHUNK, DN_CHUNK), F32)]
    results, sent = _call_with_comm(
        body, comm, name="delta_fwd" + ("" if comm is None else "_comm"), grid=(nc,),
        in_specs=ins_f + ins_b, out_specs=outs_f + outs_b, out_shape=shapes + shapes,
        scratch_shapes=[pltpu.VMEM((2 * N_GROUPS, HEAD, HEAD), F32)],
        args=(q, k, v, gates, rows_f, q, k, v, gates, rows_b), compiler_params=_params(("arbitrary",)))
    return results[:3], results[3:], sent


def _delta_bwd(q, k, v, gates, rows_f, rows_b, states_f, states_b, t_f, t_b, do, *, comm=None):
    s = q.shape[1]
    nc = s // DN_CHUNK
    heads = range(N_GROUPS)
    n_in, n_out = 8, 5

    def body(*refs):
        ins = (refs[:n_in], refs[n_in:2 * n_in])
        outs = (refs[2 * n_in:2 * n_in + n_out], refs[2 * n_in + n_out:2 * n_in + 2 * n_out])
        dstate = refs[-1]

        @pl.when(pl.program_id(0) == 0)
        def _():
            dstate[...] = jnp.zeros_like(dstate)

        qs, ks, vs, g_cols, g_rows, betas, sts, ts, dos, flags = ([] for _ in range(10))
        for reverse, (q_ref, k_ref, v_ref, gate_ref, row_ref, st_ref, t_ref, do_ref) in zip((False, True), ins):
            gate = gate_ref[...]
            g_lane, b_lane = _delta_lanes(reverse)
            for j in heads:
                qs.append(q_ref[j])
                ks.append(k_ref[j])
                vs.append(v_ref[j])
                g_cols.append(gate[:, g_lane + j:g_lane + j + 1])
                g_rows.append(row_ref[0, j:j + 1, :])
                betas.append(gate[:, b_lane + j:b_lane + j + 1])
                sts.append(st_ref[0, j])
                ts.append(t_ref[0, j])
                dos.append(do_ref[j])
                flags.append(reverse)
        t_known, flags = tuple(ts), tuple(flags)
        f = lambda q_, k_, v_, gc_, gr_, b_, s_: _delta_chunk(q_, k_, v_, gc_, gr_, b_, s_, t_known, flags)[:2]
        _, vjp = jax.vjp(f, tuple(qs), tuple(ks), tuple(vs), tuple(g_cols), tuple(g_rows), tuple(betas), tuple(sts))
        dq, dk, dv, dg_col, dg_row, dbeta, dprev = vjp((tuple(dos), tuple(dstate[n] for n in range(2 * N_GROUPS))))
        ids = lax.broadcasted_iota(jnp.int32, (DN_CHUNK, LANE), 1)
        for reverse, (dq_ref, dk_ref, dv_ref, dgate_ref, drow_ref) in zip((False, True), outs):
            g_lane, b_lane = _delta_lanes(reverse)
            dgate = jnp.zeros((DN_CHUNK, LANE), F32)
            for j in heads:
                n = j + (N_GROUPS if reverse else 0)
                dq_ref[j] = dq[n]
                dk_ref[j] = dk[n]
                dv_ref[j] = dv[n]
                dstate[n] = dprev[n]
                dgate = dgate + jnp.where(ids == g_lane + j, dg_col[n], 0.0) + jnp.where(ids == b_lane + j, dbeta[n], 0.0)
                drow_ref[0, j:j + 1, :] = dg_row[n]
            dgate_ref[...] = dgate

    def specs(chunk_of):
        hm = pl.BlockSpec((N_GROUPS, DN_CHUNK, HEAD), lambda c: (0, chunk_of(c), 0))
        gate = pl.BlockSpec((DN_CHUNK, LANE), lambda c: (chunk_of(c), 0))
        rows = pl.BlockSpec((1, N_GROUPS, DN_CHUNK), lambda c: (chunk_of(c), 0, 0))
        st = pl.BlockSpec((1, N_GROUPS, HEAD, HEAD), lambda c: (chunk_of(c), 0, 0, 0))
        ts = pl.BlockSpec((1, N_GROUPS, DN_CHUNK, DN_CHUNK), lambda c: (chunk_of(c), 0, 0, 0))
        return [hm, hm, hm, gate, rows, st, ts, hm], [hm, hm, hm, gate, rows]

    (ins_f, outs_f), (ins_b, outs_b) = specs(lambda c: nc - 1 - c), specs(lambda c: c)
    shapes = [jax.ShapeDtypeStruct((N_GROUPS, s, HEAD), F32)] * 3 + [jax.ShapeDtypeStruct((s, LANE), F32),
                                                                       jax.ShapeDtypeStruct((nc, N_GROUPS, DN_CHUNK), F32)]
    results, sent = _call_with_comm(
        body, comm, name="delta_bwd" + ("" if comm is None else "_comm"), grid=(nc,),
        in_specs=ins_f + ins_b, out_specs=outs_f + outs_b, out_shape=shapes + shapes,
        scratch_shapes=[pltpu.VMEM((2 * N_GROUPS, HEAD, HEAD), F32)],
        args=(q, k, v, gates, rows_f, states_f, t_f, do, q, k, v, gates, rows_b, states_b, t_b, do),
        compiler_params=_params(("arbitrary",)))
    return results[:n_out], results[n_out:], sent


ZB_BLOCK = 6


def _post_head(o_f, o_b, z, w):
    return _rms(o_f + o_b, w) * _silu(z)


def _post_fwd(o_f, o_b, proj, w, y, y_t, *, tile=256):
    s = proj.shape[0]
    tile = min(tile, s)

    def body(of_ref, ob_ref, z_ref, w_ref, _, __, y_ref, yt_ref):
        for h in range(N_GROUPS):
            lanes = pl.ds(h * HEAD, HEAD)
            y_h = _post_head(of_ref[h], ob_ref[h], z_ref[:, lanes], w_ref[...])
            y_ref[:, lanes] = y_h.astype(BF16)
            yt_ref[lanes, :] = y_h.T.astype(BF16)

    hm = pl.BlockSpec((N_GROUPS, tile, HEAD), lambda i: (0, i, 0))
    any_spec = pl.BlockSpec(memory_space=pl.ANY)
    return pl.pallas_call(
        body, name="post_fwd", grid=(s // tile,),
        in_specs=[hm, hm, pl.BlockSpec((tile, D_DN), lambda i: (i, ZB_BLOCK)), pl.BlockSpec((1, HEAD), lambda i: (0, 0)),
                  any_spec, any_spec],
        out_specs=[pl.BlockSpec((tile, D_DN), lambda i: (i, 1)), pl.BlockSpec((D_DN, tile), lambda i: (1, i))],
        out_shape=[jax.ShapeDtypeStruct(y.shape, BF16), jax.ShapeDtypeStruct(y_t.shape, BF16)],
        input_output_aliases={4: 0, 5: 1},
        compiler_params=_params(("arbitrary",)),
    )(o_f, o_b, proj, w, y, y_t)


def _post_bwd(o_f, o_b, proj, w, dy, *, tile=256):
    s = proj.shape[0]
    tile = min(tile, s)

    def body(of_ref, ob_ref, z_ref, w_ref, dy_ref, do_ref, dp_ref, dw_ref):
        @pl.when(pl.program_id(0) == 0)
        def _():
            dw_ref[...] = jnp.zeros_like(dw_ref)

        for h in range(N_GROUPS):
            lanes = pl.ds(h * HEAD, HEAD)
            _, vjp = jax.vjp(_post_head, of_ref[h], ob_ref[h], z_ref[:, lanes], w_ref[...])
            do, _, dz, dw = vjp(dy_ref[:, lanes])
            do_ref[h] = do
            dp_ref[:, lanes] = dz.astype(BF16)
            dw_ref[...] += dw

    hm = pl.BlockSpec((N_GROUPS, tile, HEAD), lambda i: (0, i, 0))
    zb = pl.BlockSpec((tile, D_DN), lambda i: (i, ZB_BLOCK))
    vec = pl.BlockSpec((1, HEAD), lambda i: (0, 0))
    return pl.pallas_call(
        body, name="post_bwd", grid=(s // tile,),
        in_specs=[hm, hm, zb, vec, pl.BlockSpec((tile, D_DN), lambda i: (i, 1))], out_specs=[hm, zb, vec],
        out_shape=[jax.ShapeDtypeStruct((N_GROUPS, s, HEAD), F32), jax.ShapeDtypeStruct((s, D_MAIN), BF16),
                   jax.ShapeDtypeStruct((1, HEAD), F32)],
        compiler_params=_params(("arbitrary",)),
    )(o_f, o_b, proj, w, dy)


def _my_index():
    return 4 * lax.axis_index("x") + 2 * lax.axis_index("y") + lax.axis_index("c")


def _peer(k):
    x, y, c = lax.axis_index("x"), lax.axis_index("y"), lax.axis_index("c")
    px, py, pc = x ^ ((k >> 2) & 1), y ^ ((k >> 1) & 1), c ^ (k & 1)
    return (px, py, pc), 4 * px + 2 * py + pc


class _Comm:
    def __init__(self, inputs, outputs, copies, by_chip=False):
        self.inputs, self.outputs, self.copies, self.by_chip = list(inputs), list(outputs), list(copies), by_chip
        self.kept = [o for o in self.outputs if not isinstance(o, jax.ShapeDtypeStruct)]

    def out_shapes(self):
        return [jax.ShapeDtypeStruct(o.shape, o.dtype) for o in self.outputs]

    def aliases(self, first_kept_input, first_output):
        kept_at = [i for i, o in enumerate(self.outputs) if not isinstance(o, jax.ShapeDtypeStruct)]
        return {first_kept_input + n: first_output + i for n, i in enumerate(kept_at)}

    def scratch(self):
        n = len(self.copies)
        return [pltpu.SemaphoreType.DMA((n, N_DEV - 1)), pltpu.SemaphoreType.DMA((n, N_DEV - 1)), pltpu.SemaphoreType.DMA((n,))]

    def _descriptors(self, in_refs, out_refs, sems, arrivals):
        send_sems, recv_sems, local_sems = sems
        me = _my_index()
        local, remote = [], []
        for k in range(N_DEV):
            peer, peer_idx = _peer(k)
            for a, (i_in, i_out, src_of, dst_of) in enumerate(self.copies):
                if k == 0:
                    if not arrivals:
                        local.append(pltpu.make_async_copy(src_of(in_refs[i_in], me), dst_of(out_refs[i_out], me), local_sems.at[a]))
                    continue
                receiver, sender = (me, peer_idx) if arrivals else (peer_idx, me)
                remote.append(pltpu.make_async_remote_copy(
                    src_ref=src_of(in_refs[i_in], receiver), dst_ref=dst_of(out_refs[i_out], sender),
                    send_sem=send_sems.at[a, k - 1], recv_sem=recv_sems.at[a, k - 1], device_id=peer,
                    device_id_type=pl.DeviceIdType.MESH))
        return local, remote

    def two_level(self, in_refs, out_refs, sems):
        send_sems, recv_sems, local_sems = sems
        me = _my_index()
        sibling, _ = _peer(1)
        n = range(len(self.copies))
        sources = [src_of(in_refs[i_in], me) for i_in, _, src_of, _ in self.copies]
        landing = lambda a, owner: self.copies[a][3](out_refs[self.copies[a][1]], owner)

        def remote(a, k, src, owner, to):
            return pltpu.make_async_remote_copy(
                src_ref=src, dst_ref=landing(a, owner), send_sem=send_sems.at[a, k - 1], recv_sem=recv_sems.at[a, k - 1],
                device_id=to, device_id_type=pl.DeviceIdType.MESH)

        local = lambda: [pltpu.make_async_copy(sources[a], landing(a, me), local_sems.at[a]) for a in n]
        first = lambda: [remote(a, k, sources[a], me, _peer(k)[0]) for k in (1, 2, 4, 6) for a in n]
        passed = lambda: [remote(a, k + 1, landing(a, _peer(k)[1]), _peer(k)[1], sibling) for k in (2, 4, 6) for a in n]
        arrivals = lambda ks: [remote(a, k, sources[a], _peer(k)[1], _peer(k)[0]) for k in ks for a in n]

        def start():
            for cp in local() + first():
                cp.start()

        def pass_on():
            for cp in arrivals((2, 4, 6)):
                cp.wait_recv()
            for cp in passed():
                cp.start()

        def finish():
            for cp in arrivals((1, 3, 5, 7)):
                cp.wait_recv()
            for cp in first() + passed():
                cp.wait_send()
            for cp in local():
                cp.wait()

        return start, pass_on, finish

    def phases(self, in_refs, out_refs, sems):
        if self.by_chip:
            return self.two_level(in_refs, out_refs, sems)
        return (lambda: self.start(in_refs, out_refs, sems)), None, (lambda: self.wait(in_refs, out_refs, sems))

    def start(self, in_refs, out_refs, sems):
        local, sends = self._descriptors(in_refs, out_refs, sems, arrivals=False)
        for cp in local + sends:
            cp.start()

    def wait(self, in_refs, out_refs, sems):
        _, arriving = self._descriptors(in_refs, out_refs, sems, arrivals=True)
        for cp in arriving:
            cp.wait_recv()
        local, sends = self._descriptors(in_refs, out_refs, sems, arrivals=False)
        for cp in sends:
            cp.wait_send()
        for cp in local:
            cp.wait()


def _call_with_comm(body, comm, *, name, grid, in_specs, out_specs, out_shape, scratch_shapes, args, compiler_params):
    if comm is None:
        return pl.pallas_call(body, name=name, grid=grid, in_specs=in_specs, out_specs=out_specs, out_shape=out_shape,
                              scratch_shapes=scratch_shapes, compiler_params=compiler_params)(*args), []
    n_in, n_out, n_scratch = len(in_specs), len(out_specs), len(scratch_shapes)
    c_in, c_kept, c_out = len(comm.inputs), len(comm.kept), len(comm.outputs)

    def at_step(position):
        here = [pl.program_id(d) == p for d, p in enumerate(position)]
        return here[0] if len(here) == 1 else functools.reduce(jnp.logical_and, here)

    def wrapped(*refs):
        main_in, comm_in = refs[:n_in], refs[n_in:n_in + c_in]
        o = n_in + c_in + c_kept
        main_out, comm_out = refs[o:o + n_out], refs[o + n_out:o + n_out + c_out]
        s = o + n_out + c_out
        main_scratch, sems = refs[s:s + n_scratch], refs[s + n_scratch:]

        start, pass_on, finish = comm.phases(comm_in, comm_out, sems)
        pl.when(at_step([0] * len(grid)))(start)
        if pass_on is not None:
            assert len(grid) == 1 and grid[0] >= 3
            pl.when(at_step([2 * grid[0] // 3]))(pass_on)
        body(*main_in, *main_out, *main_scratch)
        pl.when(at_step([g - 1 for g in grid]))(finish)

    any_spec = pl.BlockSpec(memory_space=pl.ANY)
    results = pl.pallas_call(
        wrapped, name=name, grid=grid, in_specs=list(in_specs) + [any_spec] * (c_in + c_kept),
        out_specs=list(out_specs) + [any_spec] * c_out, out_shape=list(out_shape) + comm.out_shapes(),
        scratch_shapes=list(scratch_shapes) + comm.scratch(), input_output_aliases=comm.aliases(n_in + c_in, n_out),
        compiler_params=compiler_params,
    )(*args, *comm.inputs, *comm.kept)
    return results[:n_out], results[n_out:]


def _exchange(name, comm):
    c_in, c_kept, c_out = len(comm.inputs), len(comm.kept), len(comm.outputs)

    def body(*refs):
        in_refs, out_refs, sems = refs[:c_in], refs[c_in + c_kept:c_in + c_kept + c_out], refs[c_in + c_kept + c_out:]
        for phase in comm.phases(in_refs, out_refs, sems):
            if phase is not None:
                phase()

    any_spec = pl.BlockSpec(memory_space=pl.ANY)
    return pl.pallas_call(
        body, name=name, in_specs=[any_spec] * (c_in + c_kept), out_specs=[any_spec] * c_out, out_shape=comm.out_shapes(),
        scratch_shapes=comm.scratch(), input_output_aliases=comm.aliases(c_in, 0),
    )(*comm.inputs, *comm.kept)


def _whole(ref, _):
    return ref


def _slot(ref, dev):
    return ref.at[dev]


def _row_block(size, axis):
    def pick(ref, dev):
        start = pl.multiple_of(dev * size, size) if size % SUBLANES == 0 else dev * size
        return ref.at[tuple([slice(None)] * axis + [pl.ds(start, size)])]
    return pick


def _blocked(shape, block):
    nd = len(block)
    grid = tuple(shape[d] // block[d] for d in range(nd) if block[d] != shape[d])
    return grid, (lambda *i: tuple(i) + (0,) * (nd - len(grid)))


def _sum_parts(name, parts, block, comm=None):
    n_parts = parts.shape[0]
    grid, idx = _blocked(parts.shape[1:], block)

    def body(p_ref, g_ref):
        g = p_ref[0].astype(F32)
        for j in range(1, n_parts):
            g = g + p_ref[j].astype(F32)
        g_ref[...] = g

    (total,), sent = _call_with_comm(
        body, comm, name=name, grid=grid, in_specs=[pl.BlockSpec((n_parts,) + tuple(block), lambda *i: (0,) + idx(*i))],
        out_specs=[pl.BlockSpec(block, idx)], out_shape=[jax.ShapeDtypeStruct(parts.shape[1:], F32)], scratch_shapes=[],
        args=(parts,), compiler_params=_params(("arbitrary",) * len(grid)))
    return total, sent


def _adamw(name, parts, w, m, v, block):
    n_parts = parts.shape[0]
    nd = len(block)
    grid, _ = _blocked(w.shape, block)
    lead = len(grid)

    def body(p_ref, w_ref, m_ref, v_ref, g_ref, d_ref, mo_ref, vo_ref):
        g = p_ref[0].astype(F32)
        for j in range(1, n_parts):
            g = g + p_ref[j].astype(F32)
        m_new = ADAM_B1 * m_ref[...] + (1.0 - ADAM_B1) * g
        v_new = ADAM_B2 * v_ref[...] + (1.0 - ADAM_B2) * jnp.square(g)
        m_hat = m_new / np.float32(1.0 - ADAM_B1 ** ADAM_STEP)
        v_hat = v_new / np.float32(1.0 - ADAM_B2 ** ADAM_STEP)
        g_ref[...] = g
        d_ref[...] = -ADAM_LR * (m_hat / (jnp.sqrt(v_hat) + ADAM_EPS) + ADAM_WD * w_ref[...])
        mo_ref[...] = m_new
        vo_ref[...] = v_new

    idx = lambda *i: tuple(i) + (0,) * (nd - lead)
    spec = pl.BlockSpec(block, idx)
    pspec = pl.BlockSpec((n_parts,) + tuple(block), lambda *i: (0,) + idx(*i))
    n_blk = int(np.prod(block[:-1])) * (-(-block[-1] // LANE) * LANE)
    return pl.pallas_call(
        body, name=name, grid=grid, in_specs=[pspec, spec, spec, spec], out_specs=[spec] * 4,
        out_shape=[jax.ShapeDtypeStruct(w.shape, F32)] * 4,
        compiler_params=_params(("arbitrary",) * lead, 2 * (n_parts + 7) * n_blk * 4 + VMEM_MARGIN),
    )(parts, w, m, v)


def _to_flat(pieces, row_multiple):
    flat = jnp.concatenate([p.reshape(-1) for p in pieces])
    rows = -(-flat.shape[0] // FLAT_W)
    rows = -(-rows // row_multiple) * row_multiple
    return jnp.pad(flat, (0, rows * FLAT_W - flat.shape[0])).reshape(rows, FLAT_W)


def _lane_row(*pieces):
    flat = jnp.concatenate([p.reshape(-1) for p in pieces])
    return jnp.pad(flat, (0, LANE - flat.shape[0])).reshape(1, LANE)


def _rows_view(a):
    return a.reshape(a.shape[:-1] + (SUBLANES, LANE))


def _in_shard_t(a):
    return _rows_view(jnp.swapaxes(a, 1, 2))


def kernel(x, norm_w, w_in, sgu_ln_g, sgu_ln_b, sgu_w, sgu_b, conv_w, a_log_f, a_log_b, dt_bias_f, dt_bias_b, dn_norm_w, w_out, final_norm_w, loss_target, m_norm_w, m_w_in, m_sgu_ln_g, m_sgu_ln_b, m_sgu_w, m_sgu_b, m_conv_w, m_a_log_f, m_a_log_b, m_dt_bias_f, m_dt_bias_b, m_dn_norm_w, m_w_out, m_final_norm_w, v_norm_w, v_w_in, v_sgu_ln_g, v_sgu_ln_b, v_sgu_w, v_sgu_b, v_conv_w, v_a_log_f, v_a_log_b, v_dt_bias_f, v_dt_bias_b, v_dn_norm_w, v_w_out, v_final_norm_w):
    weights = dict(norm_w=norm_w, w_in=w_in, sgu_ln_g=sgu_ln_g, sgu_ln_b=sgu_ln_b, sgu_w=sgu_w, sgu_b=sgu_b, conv_w=conv_w,
                   a_log_f=a_log_f, a_log_b=a_log_b, dt_bias_f=dt_bias_f, dt_bias_b=dt_bias_b, dn_norm_w=dn_norm_w,
                   w_out=w_out, final_norm_w=final_norm_w)
    m_in = dict(norm_w=m_norm_w, w_in=m_w_in, sgu_ln_g=m_sgu_ln_g, sgu_ln_b=m_sgu_ln_b, sgu_w=m_sgu_w, sgu_b=m_sgu_b,
                conv_w=m_conv_w, a_log_f=m_a_log_f, a_log_b=m_a_log_b, dt_bias_f=m_dt_bias_f, dt_bias_b=m_dt_bias_b,
                dn_norm_w=m_dn_norm_w, w_out=m_w_out, final_norm_w=m_final_norm_w)
    v_in = dict(norm_w=v_norm_w, w_in=v_w_in, sgu_ln_g=v_sgu_ln_g, sgu_ln_b=v_sgu_ln_b, sgu_w=v_sgu_w, sgu_b=v_sgu_b,
                conv_w=v_conv_w, a_log_f=v_a_log_f, a_log_b=v_a_log_b, dt_bias_f=v_dt_bias_f, dt_bias_b=v_dt_bias_b,
                dn_norm_w=v_dn_norm_w, w_out=v_w_out, final_norm_w=v_final_norm_w)
    s = x.shape[1]
    nc = s // DN_CHUNK
    depth = w_in.shape[0]
    in_shard, out_shard, conv_shard = w_in.shape[2], w_out.shape[1], conv_w.shape[2]
    x0 = x.reshape(s, D_MODEL)
    target = loss_target.reshape(s, D_MODEL)

    w_in_shard = _in_shard_t(w_in.astype(BF16))
    w_out_shard = w_out.astype(BF16)

    gathered_in_shape = jax.ShapeDtypeStruct((N_DEV * in_shard, SUBLANES, LANE), BF16)
    gathered_out_shape = jax.ShapeDtypeStruct((D_MODEL, D_MODEL), BF16)
    gathered_in, gathered_out, conv_all = _exchange("gather_first", _Comm(
        [w_in_shard[0], w_out_shard[0], conv_w],
        [gathered_in_shape, gathered_out_shape, jax.ShapeDtypeStruct((N_DEV,) + conv_w.shape, F32)],
        [(0, 0, _whole, _row_block(in_shard, 0)), (1, 1, _whole, _row_block(out_shard, 0)), (2, 2, _whole, _slot)], by_chip=True))
    gather_plan = lambda l: _Comm([w_in_shard[l], w_out_shard[l]], [gathered_in_shape, gathered_out_shape],
                                  [(0, 0, _whole, _row_block(in_shard, 0)), (1, 1, _whole, _row_block(out_shard, 0))], by_chip=True)
    conv_full = conv_all.transpose(1, 2, 0, 3).reshape(depth, CONV_W, 3 * D_DN)
    conv_full = jnp.pad(conv_full, ((0, 0), (0, HALO - CONV_W), (0, 0)))
    sgu_b_t = jnp.swapaxes(sgu_b, 1, 2)

    def rows_of(gates, first_lane):
        return gates[:, first_lane:first_lane + N_GROUPS].reshape(nc, DN_CHUNK, N_GROUPS).transpose(0, 2, 1)

    saved = []
    xl = x0
    for l in range(depth):
        w_in_t = gathered_in.reshape(D_IN, D_MODEL)
        w_gate_t = jnp.pad(w_in_t[D_MAIN:], ((0, LANE - N_GATE), (0, 0)))
        w_out_l = gathered_out
        nw = norm_w[l].reshape(1, D_MODEL)
        h, h_t = _rms_fwd(xl, nw)
        proj = _matmul([(_lhs(h, tm=MM_WIDE), _rhs(w_in_t, MM_WIDE, transposed=True))], s, D_MAIN, name="in_proj", trans_b=True, tn=MM_WIDE)
        p_gate = _matmul([(_lhs(h), _rhs(w_gate_t, LANE, transposed=True))], s, LANE, name="in_proj_gate", trans_b=True, tn=LANE)
        ln_g, ln_b = sgu_ln_g[l].reshape(1, D_SGU), sgu_ln_b[l].reshape(1, D_SGU)
        y, y_t = _sgu_fwd(proj, ln_g, ln_b, sgu_w[l], sgu_b_t[l])
        a_log = _lane_row(a_log_f[l], a_log_b[l])
        dt_bias = _lane_row(dt_bias_f[l], dt_bias_b[l])
        q, k, v, gates = _prep_fwd(proj, conv_full[l], p_gate, a_log, dt_bias)
        rows_f, rows_b = rows_of(gates, 0), rows_of(gates, N_GROUPS)
        more = l + 1 < depth
        (o_f, st_f, t_f), (o_b, st_b, t_b), next_weights = _delta_fwd(q, k, v, gates, rows_f, rows_b,
                                                                     comm=gather_plan(l + 1) if more else None)
        dn_w = dn_norm_w[l].reshape(1, HEAD)
        y, y_t = _post_fwd(o_f, o_b, proj, dn_w, y, y_t)
        x_next = _matmul([(_lhs(y, tm=MM_WIDE), _rhs(w_out_l, MM_WIDE))], s, D_MODEL, name="out_proj", add=xl, tn=MM_WIDE)
        saved.append(dict(x=xl, nw=nw, h_t=h_t, proj=proj, p_gate=p_gate, ln_g=ln_g, ln_b=ln_b, a_log=a_log, dt_bias=dt_bias,
                          q=q, k=k, v=v, gates=gates, rows_f=rows_f, rows_b=rows_b, o_f=o_f, o_b=o_b, st_f=st_f, st_b=st_b,
                          t_f=t_f, t_b=t_b, dn_w=dn_w, y_t=y_t, w_in_t=w_in_t, w_gate_t=w_gate_t, w_out=w_out_l))
        xl = x_next
        if more:
            gathered_in, gathered_out = next_weights

    loss_part, dx, dx_bf, d_final = _final_loss(xl, final_norm_w.reshape(1, D_MODEL), target)
    loss = lax.psum(loss_part[0, 0], MESH_AXES)

    t_rows = -(-D_IN // LANE) * LANE
    layer_small = [n for n in SMALL if n != "final_norm_w"]
    n_layer_small = sum(int(np.prod(weights[n].shape[1:])) for n in layer_small) + D_MODEL
    small_rows = -(-n_layer_small // (FLAT_W * HALO)) * HALO

    def small_segments(values):
        top = values["final_norm_w"]
        return jnp.concatenate([_to_flat([values[n][l] for n in layer_small] + [top if l == depth - 1 else jnp.zeros_like(top)], HALO)
                                for l in range(depth)], axis=0)

    def small_to_send(layer_grads, top_grad):
        conv = layer_grads["conv_w"].reshape(CONV_W, N_DEV, conv_shard).transpose(1, 0, 2).reshape(N_DEV, -1)
        rest = jnp.concatenate([layer_grads[n].reshape(-1) for n in layer_small[1:]] + [top_grad])
        flat = jnp.concatenate([conv, jnp.broadcast_to(rest, (N_DEV, rest.shape[0]))], axis=1)
        return jnp.pad(flat, ((0, 0), (0, small_rows * FLAT_W - flat.shape[1]))).reshape(N_DEV, small_rows, FLAT_W)

    layer_slot = lambda l: (lambda ref, dev: ref.at[dev, l])
    small_slot = lambda l: (lambda ref, dev: ref.at[dev, pl.ds(l * small_rows, small_rows)])
    parts = [jax.ShapeDtypeStruct((N_DEV, depth, in_shard, SUBLANES, LANE), BF16),
             jax.ShapeDtypeStruct((N_DEV, depth, out_shard, D_MODEL), BF16),
             jax.ShapeDtypeStruct((N_DEV, depth * small_rows, FLAT_W), F32)]

    def exchange_plan(items):
        source = (_row_block(in_shard, 0), _row_block(out_shard, 0), _slot)
        target = (layer_slot, layer_slot, small_slot)
        return _Comm([a for _, _, a in items], parts,
                     [(i, which, source[which], target[which](layer)) for i, (which, layer, _) in enumerate(items)])

    pending = []
    for l in reversed(range(depth)):
        sv = saved[l]
        dy = _matmul([(_lhs(dx_bf, tm=MM_WIDE), _rhs(sv["w_out"], MM_WIDE, transposed=True))], s, D_MODEL, name="out_proj_dy", trans_b=True,
                     tn=MM_WIDE)
        g_out = _matmul([(_lhs(sv["y_t"]), _rhs(dx_bf, MM_TILE))], D_MODEL, D_MODEL, name="out_proj_dw", out_dtype=BF16)
        do, dp, d_dn = _post_bwd(sv["o_f"], sv["o_b"], sv["proj"], sv["dn_w"], dy)
        (*dqkv_f, dgate_f, drows_f), (*dqkv_b, dgate_b, drows_b), exchanged = _delta_bwd(
            sv["q"], sv["k"], sv["v"], sv["gates"], sv["rows_f"], sv["rows_b"], sv["st_f"], sv["st_b"], sv["t_f"], sv["t_b"], do,
            comm=exchange_plan([(1, l, g_out)] + pending))
        parts = list(exchanged)
        drows = jnp.concatenate([drows_f.transpose(0, 2, 1).reshape(s, N_GROUPS), drows_b.transpose(0, 2, 1).reshape(s, N_GROUPS)], axis=1)
        drows = jnp.pad(drows, ((0, 0), (0, LANE - 2 * N_GROUPS)))
        dp, d_gate, d_conv, d_alog, d_dt = _prep_bwd(sv["proj"], conv_full[l], sv["p_gate"], sv["a_log"], sv["dt_bias"],
                                                     dqkv_f, dqkv_b, (dgate_f, dgate_b, drows), dp)
        dp, d_lg, d_lb, d_sw, d_sbt = _sgu_bwd(sv["proj"], sv["ln_g"], sv["ln_b"], sgu_w[l], sgu_b_t[l], dy, dp)
        h_t = sv["h_t"]
        g_in_t = _matmul([(_lhs(h_t), _rhs(dp, MM_TILE))], D_MODEL, D_MAIN, name="in_proj_dw", out_dtype=BF16, out_t_rows=t_rows)
        g_in_t = _matmul([(_lhs(h_t), _rhs(d_gate, LANE))], D_MODEL, LANE, name="in_proj_dw_gate", out_dtype=BF16, tn=LANE,
                         transpose_into=g_in_t, out_row_block=D_MAIN // LANE)
        g_in_rows = _rows_view(g_in_t)
        dh_pairs = [(_lhs(dp), _rhs(sv["w_in_t"], MM_TILE, k=D_MAIN)), (_lhs(d_gate), _rhs(sv["w_gate_t"], MM_TILE))]
        if l > 0:
            dh = _matmul(dh_pairs, s, D_MODEL, name="in_proj_dh")
        else:
            dh, exchanged = _matmul(dh_pairs, s, D_MODEL, name="in_proj_dh_comm", comm=exchange_plan([(0, l, g_in_rows)]))
            parts = list(exchanged)
        dx, dx_bf, d_nw = _rms_bwd(sv["x"], sv["nw"], dh, dx)
        layer_grads = dict(
            conv_w=d_conv[:CONV_W], norm_w=d_nw, sgu_ln_g=d_lg, sgu_ln_b=d_lb, sgu_w=d_sw, sgu_b=d_sbt.T,
            a_log_f=d_alog[0, :N_GROUPS], a_log_b=d_alog[0, N_GROUPS:2 * N_GROUPS],
            dt_bias_f=d_dt[0, :N_GROUPS], dt_bias_b=d_dt[0, N_GROUPS:2 * N_GROUPS], dn_norm_w=d_dn)
        small = small_to_send(layer_grads, d_final.reshape(D_MODEL) if l == depth - 1 else jnp.zeros((D_MODEL,), F32))
        pending = [(2, l, small)] + ([(0, l, g_in_rows)] if l > 0 else [])
    grad_x = dx.reshape(1, s, D_MODEL)

    parts_in, parts_out, parts_small = parts
    in_rows = 100 if in_shard % 100 == 0 else in_shard
    (_, bottom, bottom_small), = pending
    g_in_t, (parts_small,) = _sum_parts("sum_in_comm", parts_in, (1, in_rows, SUBLANES, LANE), comm=_Comm(
        [bottom_small], [parts_small], [(0, 0, _slot, small_slot(bottom))]))
    g_in = jnp.swapaxes(g_in_t.reshape(depth, in_shard, D_MODEL), 1, 2)
    res_in = _adamw("adamw_in", g_in[None], w_in, m_w_in, v_w_in, (1, min(256, D_MODEL), in_shard))
    res_out = _adamw("adamw_out", parts_out, w_out, m_w_out, v_w_out, (1, min(128, out_shard), D_MODEL))
    res_small = _adamw("adamw_small", parts_small, small_segments(weights), small_segments(m_in), small_segments(v_in),
                       (small_rows, FLAT_W))
    outs = [dict(w_in=a, w_out=b) for a, b in zip(res_in, res_out)]
    for o, d in zip(res_small, outs):
        by_layer, off = o.reshape(depth, small_rows * FLAT_W), 0
        for n in layer_small:
            size = int(np.prod(weights[n].shape[1:]))
            d[n] = by_layer[:, off:off + size].reshape(weights[n].shape)
            off += size
        d["final_norm_w"] = by_layer[depth - 1, off:off + D_MODEL]
    g_out, delta_out, m_out, v_out = outs
    return (loss, grad_x, *[g_out[n] for n in WEIGHTS], *[delta_out[n] for n in WEIGHTS],
            *[m_out[n] for n in WEIGHTS], *[v_out[n] for n in WEIGHTS])
```

```python
import functools

import numpy as np
import jax
import jax.numpy as jnp
from jax import lax
from jax.experimental import pallas as pl
from jax.experimental.pallas import tpu as pltpu

F32 = jnp.float32
BF16 = jnp.bfloat16

N_DEV = 8
D_MODEL = 2048
D_SGU = 1024
D_DN = 1024
N_GROUPS = 8
HEAD = 128
SGU_CHUNK = 128
DN_CHUNK = 64
CONV_W = 5
N_GATE = 32
D_MAIN = 3 * D_SGU + 4 * D_DN
D_IN = D_MAIN + N_GATE
LANE = 128
SUBLANES = 16
HALO = 8
EPS = 1e-6
ADAM_LR, ADAM_B1, ADAM_B2, ADAM_EPS, ADAM_WD, ADAM_STEP = 0.001, 0.9, 0.999, 1e-08, 0.01, 10
FLAT_W = 1024
MM_TILE = 512
MM_WIDE = 1024
VMEM_MARGIN = 8 << 20

MESH_AXES = ("x", "y", "c")
WEIGHTS = ("norm_w", "w_in", "sgu_ln_g", "sgu_ln_b", "sgu_w", "sgu_b", "conv_w", "a_log_f", "a_log_b",
           "dt_bias_f", "dt_bias_b", "dn_norm_w", "w_out", "final_norm_w")
SMALL = ("conv_w",) + tuple(n for n in WEIGHTS if n not in ("w_in", "w_out", "conv_w"))


def _bdot(a, b, dims):
    return lax.dot_general(a.astype(BF16), b.astype(BF16), (dims, ((), ())), preferred_element_type=F32)


@jax.custom_vjp
def _mm_nn(a, b):
    return _bdot(a, b, ((1,), (0,)))


def _mm_nn_fwd(a, b):
    return _mm_nn(a, b), (a, b)


def _mm_nn_bwd(res, ct):
    a, b = res
    return _bdot(ct, b, ((1,), (1,))), _bdot(a, ct, ((0,), (0,)))


_mm_nn.defvjp(_mm_nn_fwd, _mm_nn_bwd)


@jax.custom_vjp
def _mm_nt(a, b):
    return _bdot(a, b, ((1,), (1,)))


def _mm_nt_fwd(a, b):
    return _mm_nt(a, b), (a, b)


def _mm_nt_bwd(res, ct):
    a, b = res
    return _bdot(ct, b, ((1,), (0,))), _bdot(ct, a, ((0,), (0,)))


_mm_nt.defvjp(_mm_nt_fwd, _mm_nt_bwd)


@jax.custom_vjp
def _inverse_given(m, t):
    return t


def _inverse_given_fwd(m, t):
    return t, t


def _inverse_given_bwd(t, ct):
    dm = -_bdot(t, _bdot(ct, t, ((1,), (1,))), ((0,), (0,)))
    return dm, jnp.zeros_like(t)


_inverse_given.defvjp(_inverse_given_fwd, _inverse_given_bwd)


@jax.custom_vjp
def _split_lanes(x):
    return tuple(x[:, i * LANE:(i + 1) * LANE] for i in range(x.shape[1] // LANE))


def _split_lanes_fwd(x):
    return _split_lanes(x), None


def _split_lanes_bwd(_, cts):
    return (jnp.concatenate(cts, axis=1),)


_split_lanes.defvjp(_split_lanes_fwd, _split_lanes_bwd)


@functools.partial(jax.custom_vjp, nondiff_argnums=(1,))
def _split_rows(x, n):
    return x[:n], x[n:]


def _split_rows_fwd(x, n):
    return _split_rows(x, n), None


def _split_rows_bwd(n, _, cts):
    return (jnp.concatenate(cts, axis=0),)


_split_rows.defvjp(_split_rows_fwd, _split_rows_bwd)


def _silu(t):
    return t * jax.nn.sigmoid(t)


def _gelu(t):
    return 0.5 * t * (1.0 + lax.erf(t * np.float32(0.7071067811865476)))


def _rms(x, w):
    return x * lax.rsqrt(jnp.mean(x * x, axis=-1, keepdims=True) + EPS) * w


def _params(sem, vmem_bytes=None):
    kw = dict(dimension_semantics=sem)
    if vmem_bytes is not None:
        kw["vmem_limit_bytes"] = int(vmem_bytes)
    return pltpu.CompilerParams(**kw)


def _lhs(a, k=None, k_block=0, tm=MM_TILE):
    k = a.shape[1] if k is None else k
    tm = min(tm, a.shape[0])
    return a, pl.BlockSpec((tm, k), lambda i, j: (i, k_block)), tm * k * a.dtype.itemsize


def _rhs(b, tn, *, layer=None, transposed=False, k=None, k_block=0, n_offset=0):
    lead = () if layer is None else (layer,)
    none = () if layer is None else (None,)
    if transposed:
        k = b.shape[-1]
        spec = pl.BlockSpec(none + (tn, k), lambda i, j: lead + (j + n_offset, 0))
    else:
        k = b.shape[-2] if k is None else k
        spec = pl.BlockSpec(none + (k, tn), lambda i, j: lead + (k_block, j))
    return b, spec, tn * k * b.dtype.itemsize


def _matmul(pairs, m, n, *, name, trans_b=False, add=None, out_dtype=F32, tn=MM_TILE, out_t_rows=None, transpose_into=None,
            out_row_block=0, comm=None):
    tm = pairs[0][0][1].block_shape[0]
    assert m % tm == 0 and n % tn == 0 and all(a_spec.block_shape[0] == tm for (_, a_spec, _), _ in pairs)
    n_pairs = len(pairs)
    contract = ((1,), (1,)) if trans_b else ((1,), (0,))
    transposed_out = out_t_rows is not None or transpose_into is not None

    def body(*refs):
        o_ref = refs[-1]
        acc = None
        for i in range(n_pairs):
            d = lax.dot_general(refs[2 * i][...].astype(BF16), refs[2 * i + 1][...].astype(BF16),
                                (contract, ((), ())), preferred_element_type=F32)
            acc = d if acc is None else acc + d
        if add is not None:
            acc = acc + refs[2 * n_pairs][...]
        o_ref[...] = (acc.T if transposed_out else acc).astype(out_dtype)

    in_specs, args, vmem, aliases = [], [], 0, {}
    for (a, a_spec, a_bytes), (b, b_spec, b_bytes) in pairs:
        in_specs += [a_spec, b_spec]
        args += [a, b]
        vmem += 2 * (a_bytes + b_bytes)
    if add is not None:
        in_specs.append(pl.BlockSpec((tm, tn), lambda i, j: (i, j)))
        args.append(add)
        vmem += 2 * tm * tn * 4
    vmem += 2 * tm * tn * jnp.dtype(out_dtype).itemsize + 3 * tm * tn * 4
    if transposed_out:
        out_spec = pl.BlockSpec((tn, tm), lambda i, j: (j + out_row_block, i))
        if transpose_into is not None:
            in_specs.append(pl.BlockSpec(memory_space=pl.ANY))
            args.append(transpose_into)
            aliases = {len(args) - 1: 0}
            out_shape = jax.ShapeDtypeStruct(transpose_into.shape, out_dtype)
        else:
            out_shape = jax.ShapeDtypeStruct((out_t_rows, m), out_dtype)
    else:
        out_spec = pl.BlockSpec((tm, tn), lambda i, j: (i, j))
        out_shape = jax.ShapeDtypeStruct((m, n), out_dtype)
    if comm is not None:
        assert not aliases
        (out,), sent = _call_with_comm(
            body, comm, name=name, grid=(m // tm, n // tn), in_specs=in_specs, out_specs=[out_spec], out_shape=[out_shape],
            scratch_shapes=[], args=args, compiler_params=_params(("arbitrary", "arbitrary"), vmem + VMEM_MARGIN))
        return out, sent
    return pl.pallas_call(
        body, name=name, grid=(m // tm, n // tn), in_specs=in_specs, out_specs=out_spec, out_shape=out_shape,
        input_output_aliases=aliases, compiler_params=_params(("parallel", "arbitrary"), vmem + VMEM_MARGIN),
    )(*args)


def _rms_fwd(x, w, *, tile=512):
    s, d = x.shape
    tile = min(tile, s)

    def body(x_ref, w_ref, h_ref, ht_ref):
        h = _rms(x_ref[...], w_ref[...])
        h_ref[...] = h.astype(BF16)
        ht_ref[...] = h.T.astype(BF16)

    return pl.pallas_call(
        body, name="rms_fwd", grid=(s // tile,),
        in_specs=[pl.BlockSpec((tile, d), lambda i: (i, 0)), pl.BlockSpec((1, d), lambda i: (0, 0))],
        out_specs=[pl.BlockSpec((tile, d), lambda i: (i, 0)), pl.BlockSpec((d, tile), lambda i: (0, i))],
        out_shape=[jax.ShapeDtypeStruct((s, d), BF16), jax.ShapeDtypeStruct((d, s), BF16)],
        compiler_params=_params(("arbitrary",)),
    )(x, w)


def _rms_bwd(x, w, dh, dx_out, *, tile=256):
    s, d = x.shape
    tile = min(tile, s)

    def body(x_ref, w_ref, dh_ref, dxo_ref, dx_ref, dxb_ref, dw_ref):
        _, vjp = jax.vjp(_rms, x_ref[...], w_ref[...])
        dxv, dwv = vjp(dh_ref[...])
        dx = dxo_ref[...] + dxv
        dx_ref[...] = dx
        dxb_ref[...] = dx.astype(BF16)

        @pl.when(pl.program_id(0) == 0)
        def _():
            dw_ref[...] = jnp.zeros_like(dw_ref)

        dw_ref[...] += dwv

    row = pl.BlockSpec((tile, d), lambda i: (i, 0))
    vec = pl.BlockSpec((1, d), lambda i: (0, 0))
    return pl.pallas_call(
        body, name="rms_bwd", grid=(s // tile,),
        in_specs=[row, vec, row, row], out_specs=[row, row, vec],
        out_shape=[jax.ShapeDtypeStruct((s, d), F32), jax.ShapeDtypeStruct((s, d), BF16), jax.ShapeDtypeStruct((1, d), F32)],
        compiler_params=_params(("arbitrary",)),
    )(x, w, dh, dx_out)


def _final_loss(x, w, target, *, tile=256):
    s, d = x.shape
    tile = min(tile, s)

    def body(x_ref, w_ref, t_ref, loss_ref, dx_ref, dxb_ref, dw_ref):
        def f(xv, wv):
            err = jnp.square(_rms(xv, wv) - t_ref[...])
            per_token = jnp.sum(err, axis=1, keepdims=True) * np.float32(1.0 / d)
            return 0.5 * jnp.sum(per_token, axis=0, keepdims=True)

        loss, vjp = jax.vjp(f, x_ref[...], w_ref[...])
        dxv, dwv = vjp(jnp.ones((1, 1), F32))
        dx_ref[...] = dxv
        dxb_ref[...] = dxv.astype(BF16)

        @pl.when(pl.program_id(0) == 0)
        def _():
            dw_ref[...] = jnp.zeros_like(dw_ref)
            loss_ref[...] = jnp.zeros_like(loss_ref)

        dw_ref[...] += dwv
        loss_ref[...] += jnp.broadcast_to(loss, (1, LANE))

    row = pl.BlockSpec((tile, d), lambda i: (i, 0))
    vec = pl.BlockSpec((1, d), lambda i: (0, 0))
    return pl.pallas_call(
        body, name="final_loss", grid=(s // tile,),
        in_specs=[row, vec, row], out_specs=[pl.BlockSpec((1, LANE), lambda i: (0, 0)), row, row, vec],
        out_shape=[jax.ShapeDtypeStruct((1, LANE), F32), jax.ShapeDtypeStruct((s, d), F32),
                   jax.ShapeDtypeStruct((s, d), BF16), jax.ShapeDtypeStruct((1, d), F32)],
        compiler_params=_params(("arbitrary",)),
    )(x, w, target)


def _sgu_chunk(u, v, z, ln_g, ln_b, ws, bcols):
    vg = _gelu(v)
    xc = vg - jnp.mean(vg, axis=-1, keepdims=True)
    vl = xc * lax.rsqrt(jnp.mean(xc * xc, axis=-1, keepdims=True) + EPS) * ln_g + ln_b
    sp = jnp.concatenate([_mm_nn(ws[g], vg_) + bcols[g] for g, vg_ in enumerate(_split_lanes(vl))], axis=1)
    return _gelu(u) * sp * _silu(z)


def _sgu_specs(tile):
    col = lambda j: pl.BlockSpec((tile, D_SGU), lambda i, j=j: (i, j))
    vec = pl.BlockSpec((1, D_SGU), lambda i: (0, 0))
    wspec = pl.BlockSpec((N_GROUPS, SGU_CHUNK, SGU_CHUNK), lambda i: (0, 0, 0))
    bspec = pl.BlockSpec((SGU_CHUNK, N_GROUPS), lambda i: (0, 0))
    return col, vec, wspec, bspec


def _sgu_fwd(proj, ln_g, ln_b, w_s, b_t, *, tile=256):
    s = proj.shape[0]
    tile = min(tile, s)
    col, vec, wspec, bspec = _sgu_specs(tile)

    def body(u_ref, v_ref, z_ref, g_ref, b_ref, w_ref, bt_ref, y_ref, yt_ref):
        ws = tuple(w_ref[g] for g in range(N_GROUPS))
        bcols = tuple(bt_ref[:, g:g + 1] for g in range(N_GROUPS))
        for c in range(tile // SGU_CHUNK):
            rows = pl.ds(c * SGU_CHUNK, SGU_CHUNK)
            y = _sgu_chunk(u_ref[rows, :], v_ref[rows, :], z_ref[rows, :], g_ref[...], b_ref[...], ws, bcols)
            y_ref[rows, :] = y.astype(BF16)
            yt_ref[:, rows] = y.T.astype(BF16)

    return pl.pallas_call(
        body, name="sgu_fwd", grid=(s // tile,),
        in_specs=[col(0), col(1), col(2), vec, vec, wspec, bspec],
        out_specs=[pl.BlockSpec((tile, D_SGU), lambda i: (i, 0)), pl.BlockSpec((D_SGU, tile), lambda i: (0, i))],
        out_shape=[jax.ShapeDtypeStruct((s, D_SGU + D_DN), BF16), jax.ShapeDtypeStruct((D_SGU + D_DN, s), BF16)],
        compiler_params=_params(("arbitrary",)),
    )(proj, proj, proj, ln_g, ln_b, w_s, b_t)


def _sgu_bwd(proj, ln_g, ln_b, w_s, b_t, dy, dp, *, tile=256):
    s = proj.shape[0]
    tile = min(tile, s)
    col, vec, wspec, bspec = _sgu_specs(tile)

    def body(u_ref, v_ref, z_ref, g_ref, b_ref, w_ref, bt_ref, dy_ref, _, dp_ref, dg_ref, db_ref, dw_ref, dbt_ref):
        @pl.when(pl.program_id(0) == 0)
        def _():
            dg_ref[...] = jnp.zeros_like(dg_ref)
            db_ref[...] = jnp.zeros_like(db_ref)
            dw_ref[...] = jnp.zeros_like(dw_ref)
            dbt_ref[...] = jnp.zeros_like(dbt_ref)

        ws = tuple(w_ref[g] for g in range(N_GROUPS))
        bcols = tuple(bt_ref[:, g:g + 1] for g in range(N_GROUPS))
        for c in range(tile // SGU_CHUNK):
            rows = pl.ds(c * SGU_CHUNK, SGU_CHUNK)
            _, vjp = jax.vjp(_sgu_chunk, u_ref[rows, :], v_ref[rows, :], z_ref[rows, :], g_ref[...], b_ref[...], ws, bcols)
            du, dv, dz, dg, db, dws, dbc = vjp(dy_ref[rows, :])
            dp_ref[rows, pl.ds(0, D_SGU)] = du.astype(BF16)
            dp_ref[rows, pl.ds(D_SGU, D_SGU)] = dv.astype(BF16)
            dp_ref[rows, pl.ds(2 * D_SGU, D_SGU)] = dz.astype(BF16)
            dg_ref[...] += dg
            db_ref[...] += db
            for g in range(N_GROUPS):
                dw_ref[g] += dws[g]
                dbt_ref[:, g:g + 1] += dbc[g]

    return pl.pallas_call(
        body, name="sgu_bwd", grid=(s // tile,),
        in_specs=[col(0), col(1), col(2), vec, vec, wspec, bspec, pl.BlockSpec((tile, D_SGU), lambda i: (i, 0)),
                  pl.BlockSpec(memory_space=pl.ANY)],
        out_specs=[pl.BlockSpec((tile, 3 * D_SGU), lambda i: (i, 0)), vec, vec, wspec, bspec],
        out_shape=[jax.ShapeDtypeStruct(dp.shape, BF16), jax.ShapeDtypeStruct((1, D_SGU), F32),
                   jax.ShapeDtypeStruct((1, D_SGU), F32), jax.ShapeDtypeStruct((N_GROUPS, SGU_CHUNK, SGU_CHUNK), F32),
                   jax.ShapeDtypeStruct((SGU_CHUNK, N_GROUPS), F32)],
        input_output_aliases={8: 0},
        compiler_params=_params(("arbitrary",)),
    )(proj, proj, proj, ln_g, ln_b, w_s, b_t, dy, dp)


QKV_BLOCK = 1


def _qkv_head(cq, ck, cv):
    q, k, v = _silu(cq), _silu(ck), _silu(cv)
    q = q * lax.rsqrt(jnp.sum(q * q, axis=-1, keepdims=True) + EPS) * np.float32(HEAD ** -0.5)
    k = k * lax.rsqrt(jnp.sum(k * k, axis=-1, keepdims=True) + EPS)
    return q, k, v


def _gate_fn(p, a_log, dt_bias):
    lane = lax.broadcasted_iota(jnp.int32, p.shape, 1)
    g = -jnp.exp(a_log) * jax.nn.softplus(p + dt_bias)
    return jnp.where(lane < 16, g, jnp.where(lane < N_GATE, jax.nn.sigmoid(p), 0.0))


def _shifted_rows(full, shift, start, rows):
    n = full.shape[0]
    assert start % HALO == 0 and 0 <= start + shift and start + shift + rows <= n
    rolled = full if shift == 0 else pltpu.roll(full, (-shift) % n, 0)
    return rolled[start:start + rows, :]


def _halo_maps(tile, s):
    per = tile // HALO
    prev = lambda i: jnp.maximum(i * per - 1, 0)
    nxt = lambda i: jnp.minimum((i + 1) * per, s // HALO - 1)
    return prev, nxt


def _prep_fwd(proj, conv_w, pg, a_log, dt_bias, *, tile=256):
    s = proj.shape[0]
    width = 3 * D_DN
    tile = min(tile, s)
    n_blk = s // tile
    prev, nxt = _halo_maps(tile, s)

    def body(x_ref, xp_ref, xn_ref, w_ref, pg_ref, al_ref, dt_ref, q_ref, k_ref, v_ref, gate_ref, ext):
        i = pl.program_id(0)
        ext[pl.ds(0, HALO), :] = jnp.where(i > 0, xp_ref[...], 0.0)
        ext[pl.ds(HALO, tile), :] = x_ref[...]
        ext[pl.ds(HALO + tile, HALO), :] = jnp.where(i < n_blk - 1, xn_ref[...], 0.0)
        full, c = ext[...], None
        for j in range(CONV_W):
            term = _shifted_rows(full, j - CONV_W // 2, HALO, tile) * w_ref[j:j + 1, :]
            c = term if c is None else c + term
        for h in range(N_GROUPS):
            q, k, v = _qkv_head(c[:, h * HEAD:(h + 1) * HEAD], c[:, D_DN + h * HEAD:D_DN + (h + 1) * HEAD],
                                c[:, 2 * D_DN + h * HEAD:2 * D_DN + (h + 1) * HEAD])
            q_ref[h] = q
            k_ref[h] = k
            v_ref[h] = v
        gate_ref[...] = _gate_fn(pg_ref[...], al_ref[...], dt_ref[...])

    hm = pl.BlockSpec((N_GROUPS, tile, HEAD), lambda i: (0, i, 0))
    lane_vec = pl.BlockSpec((1, LANE), lambda i: (0, 0))
    return pl.pallas_call(
        body, name="prep_fwd", grid=(n_blk,),
        in_specs=[pl.BlockSpec((tile, width), lambda i: (i, QKV_BLOCK)),
                  pl.BlockSpec((HALO, width), lambda i: (prev(i), QKV_BLOCK)),
                  pl.BlockSpec((HALO, width), lambda i: (nxt(i), QKV_BLOCK)),
                  pl.BlockSpec((HALO, width), lambda i: (0, 0)),
                  pl.BlockSpec((tile, LANE), lambda i: (i, 0)), lane_vec, lane_vec],
        out_specs=[hm, hm, hm, pl.BlockSpec((tile, LANE), lambda i: (i, 0))],
        out_shape=[jax.ShapeDtypeStruct((N_GROUPS, s, HEAD), F32)] * 3 + [jax.ShapeDtypeStruct((s, LANE), F32)],
        scratch_shapes=[pltpu.VMEM((tile + 2 * HALO, width), F32)],
        compiler_params=_params(("arbitrary",)),
    )(proj, proj, proj, conv_w, pg, a_log, dt_bias)


def _prep_bwd(proj, conv_w, pg, a_log, dt_bias, dqkv_f, dqkv_b, dgates, dp, *, tile=128):
    s = proj.shape[0]
    width = 3 * D_DN
    tile = min(tile, s)
    n_blk = s // tile
    prev, nxt = _halo_maps(tile, s)
    ext_rows = tile + 2 * HALO
    pad = CONV_W // 2

    def body(*refs):
        x_ref, xp_ref, xn_ref, w_ref, pg_ref, al_ref, dt_ref = refs[:7]
        ct_refs = refs[7:25]
        dg0_ref, dg1_ref, dg2_ref = refs[25:28]
        dp_ref, dpg_ref, dw_ref, dal_ref, ddt_ref, xext, dcext = refs[29:]
        i = pl.program_id(0)
        has_prev, has_next = i > 0, i < n_blk - 1

        @pl.when(i == 0)
        def _():
            dw_ref[...] = jnp.zeros_like(dw_ref)
            dal_ref[...] = jnp.zeros_like(dal_ref)
            ddt_ref[...] = jnp.zeros_like(ddt_ref)

        zeros = jnp.zeros((HALO, width), F32)
        xext[pl.ds(0, HALO), :] = zeros
        xext[pl.ds(HALO, HALO), :] = jnp.where(has_prev, xp_ref[...], 0.0)
        xext[pl.ds(2 * HALO, tile), :] = x_ref[...]
        xext[pl.ds(2 * HALO + tile, HALO), :] = jnp.where(has_next, xn_ref[...], 0.0)
        xext[pl.ds(3 * HALO + tile, HALO), :] = zeros
        xfull, c = xext[...], None
        for j in range(CONV_W):
            term = _shifted_rows(xfull, j - pad, HALO, ext_rows) * w_ref[j:j + 1, :]
            c = term if c is None else c + term
        for h in range(N_GROUPS):
            lanes = [pl.ds(part * D_DN + h * HEAD, HEAD) for part in range(3)]
            cts = []
            for part in range(3):
                (cur, before, after), (cur2, before2, after2) = ct_refs[3 * part:3 * part + 3], ct_refs[9 + 3 * part:12 + 3 * part]
                cts.append(jnp.concatenate([jnp.where(has_prev, before[h] + before2[h], 0.0), cur[h] + cur2[h],
                                            jnp.where(has_next, after[h] + after2[h], 0.0)], axis=0))
            _, vjp = jax.vjp(_qkv_head, c[:, h * HEAD:(h + 1) * HEAD], c[:, D_DN + h * HEAD:D_DN + (h + 1) * HEAD],
                             c[:, 2 * D_DN + h * HEAD:2 * D_DN + (h + 1) * HEAD])
            for lane_sl, dc in zip(lanes, vjp(tuple(cts))):
                dcext[:, lane_sl] = dc
        dcfull = dcext[...]
        dx, dc_blk = None, dcfull[HALO:HALO + tile, :]
        for j in range(CONV_W):
            term = _shifted_rows(dcfull, pad - j, HALO, tile) * w_ref[j:j + 1, :]
            dx = term if dx is None else dx + term
            dw_ref[j:j + 1, :] += jnp.sum(dc_blk * _shifted_rows(xfull, j - pad, 2 * HALO, tile), axis=0, keepdims=True)
        dp_ref[...] = dx.astype(BF16)
        _, gvjp = jax.vjp(_gate_fn, pg_ref[...], al_ref[...], dt_ref[...])
        dpg, dal, ddt = gvjp(dg0_ref[...] + dg1_ref[...] + dg2_ref[...])
        dpg_ref[...] = dpg.astype(BF16)
        dal_ref[...] += dal
        ddt_ref[...] += ddt

    xs = [pl.BlockSpec((tile, width), lambda i: (i, QKV_BLOCK)), pl.BlockSpec((HALO, width), lambda i: (prev(i), QKV_BLOCK)),
          pl.BlockSpec((HALO, width), lambda i: (nxt(i), QKV_BLOCK))]
    hm = [pl.BlockSpec((N_GROUPS, tile, HEAD), lambda i: (0, i, 0)),
          pl.BlockSpec((N_GROUPS, HALO, HEAD), lambda i: (0, prev(i), 0)),
          pl.BlockSpec((N_GROUPS, HALO, HEAD), lambda i: (0, nxt(i), 0))]
    gate = pl.BlockSpec((tile, LANE), lambda i: (i, 0))
    lane_vec = pl.BlockSpec((1, LANE), lambda i: (0, 0))
    wspec = pl.BlockSpec((HALO, width), lambda i: (0, 0))
    n_in = 3 + 4 + 18 + 3
    ct_arrays = [a for dqkv in (dqkv_f, dqkv_b) for a in dqkv for _ in range(3)]
    return pl.pallas_call(
        body, name="prep_bwd", grid=(n_blk,),
        in_specs=xs + [wspec, gate, lane_vec, lane_vec] + hm * 6 + [gate] * 3 + [pl.BlockSpec(memory_space=pl.ANY)],
        out_specs=[pl.BlockSpec((tile, width), lambda i: (i, QKV_BLOCK)), gate, wspec, lane_vec, lane_vec],
        out_shape=[jax.ShapeDtypeStruct(dp.shape, BF16), jax.ShapeDtypeStruct((s, LANE), BF16),
                   jax.ShapeDtypeStruct((HALO, width), F32), jax.ShapeDtypeStruct((1, LANE), F32),
                   jax.ShapeDtypeStruct((1, LANE), F32)],
        input_output_aliases={n_in: 0},
        scratch_shapes=[pltpu.VMEM((tile + 4 * HALO, width), F32), pltpu.VMEM((ext_rows, width), F32)],
        compiler_params=_params(("arbitrary",)),
    )(proj, proj, proj, conv_w, pg, a_log, dt_bias, *ct_arrays, *dgates, dp)


def _each(f, *cols):
    return tuple(f(*a) for a in zip(*cols))


def _delta_chunk(q, k, v, g_col, g_row, beta, state, t_known, reverse):
    n = DN_CHUNK
    ri = lax.broadcasted_iota(jnp.int32, (n, n), 0)
    ci = lax.broadcasted_iota(jnp.int32, (n, n), 1)
    masks = {False: (ri >= ci, ri <= ci, ri > ci), True: (ri <= ci, ri >= ci, ri < ci)}
    flags = tuple(reverse) if isinstance(reverse, (tuple, list)) else (reverse,) * len(q)
    incl, incl_t, strict = (tuple(masks[f][i] for f in flags) for i in range(3))
    gc_col = _each(lambda gr, mk: jnp.sum(jnp.where(mk, gr, 0.0), axis=1, keepdims=True), g_row, incl)
    gc_row = _each(lambda gc, mk: jnp.sum(jnp.where(mk, gc, 0.0), axis=0, keepdims=True), g_col, incl_t)
    g_tot = _each(lambda gr: jnp.sum(gr, axis=1, keepdims=True), g_row)
    decay = _each(lambda a, b, mk: jnp.where(mk, jnp.exp(jnp.where(mk, a - b, 0.0)), 0.0), gc_col, gc_row, incl)
    rows = lambda *xs: jnp.concatenate(xs, axis=0)
    kb = _each(lambda a, b: a * b, k, beta)
    kk_qk = _each(lambda kb_, q_, k_: _split_rows(_mm_nt(rows(kb_, q_), k_), n), kb, q, k)
    m = _each(lambda two, d, mk: jnp.where(mk, two[0] * d, 0.0), kk_qk, decay, strict)
    if t_known is None:
        eye = (ri == ci).astype(F32)
        p = _each(lambda a: -a, m)
        t = tuple(eye for _ in p)
        for _ in range(5):
            both = _each(lambda p_, t_: _bdot(rows(p_, t_), p_, ((1,), (0,))), p, t)
            p = _each(lambda b: b[:n], both)
            t = _each(lambda t_, b: t_ + b[n:], t, both)
        t = _each(lambda t_, p_: t_ + _bdot(t_, p_, ((1,), (0,))), t, p)
    else:
        t = _each(_inverse_given, m, t_known)
    e_gc = _each(jnp.exp, gc_col)
    u_w = _each(lambda t_, v_, b_, kb_, e_: _split_lanes(_mm_nn(t_, jnp.concatenate([v_ * b_, kb_ * e_], axis=1))), t, v, beta, kb, e_gc)
    attn = _each(lambda two, d_: two[1] * d_, kk_qk, decay)
    ws_qs = _each(lambda uw, q_, e_, s_: _split_rows(_mm_nn(rows(uw[1], q_ * e_), s_), n), u_w, q, e_gc, state)
    v_new = _each(lambda uw, wq: uw[0] - wq[0], u_w, ws_qs)
    k_dec_t = _each(lambda k_, gt, gc: jnp.transpose(k_ * jnp.exp(gt - gc)), k, g_tot, gc_col)
    av_kv = _each(lambda a_, kt, vn: _split_rows(_mm_nn(rows(a_, kt), vn), n), attn, k_dec_t, v_new)
    o = _each(lambda wq, ak: wq[1] + ak[0], ws_qs, av_kv)
    new_state = _each(lambda s_, gt, ak: s_ * jnp.exp(gt) + ak[1], state, g_tot, av_kv)
    return o, new_state, t


def _delta_lanes(reverse):
    return (N_GROUPS, 3 * N_GROUPS) if reverse else (0, 2 * N_GROUPS)


def _delta_fwd(q, k, v, gates, rows_f, rows_b, *, comm=None):
    s = q.shape[1]
    nc = s // DN_CHUNK
    heads = range(N_GROUPS)

    def body(qf_ref, kf_ref, vf_ref, gf_ref, rf_ref, qb_ref, kb_ref, vb_ref, gb_ref, rb_ref,
             of_ref, sf_ref, tf_ref, ob_ref, sb_ref, tb_ref, state):
        @pl.when(pl.program_id(0) == 0)
        def _():
            state[...] = jnp.zeros_like(state)

        qs, ks, vs, g_cols, g_rows, betas, flags = [], [], [], [], [], [], []
        for reverse, q_ref, k_ref, v_ref, gate, row_ref in ((False, qf_ref, kf_ref, vf_ref, gf_ref[...], rf_ref),
                                                            (True, qb_ref, kb_ref, vb_ref, gb_ref[...], rb_ref)):
            g_lane, b_lane = _delta_lanes(reverse)
            for j in heads:
                qs.append(q_ref[j])
                ks.append(k_ref[j])
                vs.append(v_ref[j])
                g_cols.append(gate[:, g_lane + j:g_lane + j + 1])
                g_rows.append(row_ref[0, j:j + 1, :])
                betas.append(gate[:, b_lane + j:b_lane + j + 1])
                flags.append(reverse)
        prev = tuple(state[n] for n in range(2 * N_GROUPS))
        o, new, t = _delta_chunk(tuple(qs), tuple(ks), tuple(vs), tuple(g_cols), tuple(g_rows), tuple(betas), prev, None, tuple(flags))
        for n in range(2 * N_GROUPS):
            o_ref, st_ref, t_ref = (of_ref, sf_ref, tf_ref) if n < N_GROUPS else (ob_ref, sb_ref, tb_ref)
            j = n % N_GROUPS
            st_ref[0, j] = prev[n]
            o_ref[j] = o[n]
            t_ref[0, j] = t[n]
            state[n] = new[n]

    def specs(chunk_of):
        hm = pl.BlockSpec((N_GROUPS, DN_CHUNK, HEAD), lambda c: (0, chunk_of(c), 0))
        ins = [hm, hm, hm, pl.BlockSpec((DN_CHUNK, LANE), lambda c: (chunk_of(c), 0)),
               pl.BlockSpec((1, N_GROUPS, DN_CHUNK), lambda c: (chunk_of(c), 0, 0))]
        outs = [hm, pl.BlockSpec((1, N_GROUPS, HEAD, HEAD), lambda c: (chunk_of(c), 0, 0, 0)),
                pl.BlockSpec((1, N_GROUPS, DN_CHUNK, DN_CHUNK), lambda c: (chunk_of(c), 0, 0, 0))]
        return ins, outs

    (ins_f, outs_f), (ins_b, outs_b) = specs(lambda c: c), specs(lambda c: nc - 1 - c)
    shapes = [jax.ShapeDtypeStruct((N_GROUPS, s, HEAD), F32), jax.ShapeDtypeStruct((nc, N_GROUPS, HEAD, HEAD), F32),
              jax.ShapeDtypeStruct((nc, N_GROUPS, DN_CHUNK, DN_CHUNK), F32)]
    results, sent = _call_with_comm(
        body, comm, name="delta_fwd" + ("" if comm is None else "_comm"), grid=(nc,),
        in_specs=ins_f + ins_b, out_specs=outs_f + outs_b, out_shape=shapes + shapes,
        scratch_shapes=[pltpu.VMEM((2 * N_GROUPS, HEAD, HEAD), F32)],
        args=(q, k, v, gates, rows_f, q, k, v, gates, rows_b), compiler_params=_params(("arbitrary",)))
    return results[:3], results[3:], sent


def _delta_bwd(q, k, v, gates, rows_f, rows_b, states_f, states_b, t_f, t_b, do, *, comm=None):
    s = q.shape[1]
    nc = s // DN_CHUNK
    heads = range(N_GROUPS)
    n_in, n_out = 8, 5

    def body(*refs):
        ins = (refs[:n_in], refs[n_in:2 * n_in])
        outs = (refs[2 * n_in:2 * n_in + n_out], refs[2 * n_in + n_out:2 * n_in + 2 * n_out])
        dstate = refs[-1]

        @pl.when(pl.program_id(0) == 0)
        def _():
            dstate[...] = jnp.zeros_like(dstate)

        qs, ks, vs, g_cols, g_rows, betas, sts, ts, dos, flags = ([] for _ in range(10))
        for reverse, (q_ref, k_ref, v_ref, gate_ref, row_ref, st_ref, t_ref, do_ref) in zip((False, True), ins):
            gate = gate_ref[...]
            g_lane, b_lane = _delta_lanes(reverse)
            for j in heads:
                qs.append(q_ref[j])
                ks.append(k_ref[j])
                vs.append(v_ref[j])
                g_cols.append(gate[:, g_lane + j:g_lane + j + 1])
                g_rows.append(row_ref[0, j:j + 1, :])
                betas.append(gate[:, b_lane + j:b_lane + j + 1])
                sts.append(st_ref[0, j])
                ts.append(t_ref[0, j])
                dos.append(do_ref[j])
                flags.append(reverse)
        t_known, flags = tuple(ts), tuple(flags)
        f = lambda q_, k_, v_, gc_, gr_, b_, s_: _delta_chunk(q_, k_, v_, gc_, gr_, b_, s_, t_known, flags)[:2]
        _, vjp = jax.vjp(f, tuple(qs), tuple(ks), tuple(vs), tuple(g_cols), tuple(g_rows), tuple(betas), tuple(sts))
        dq, dk, dv, dg_col, dg_row, dbeta, dprev = vjp((tuple(dos), tuple(dstate[n] for n in range(2 * N_GROUPS))))
        ids = lax.broadcasted_iota(jnp.int32, (DN_CHUNK, LANE), 1)
        for reverse, (dq_ref, dk_ref, dv_ref, dgate_ref, drow_ref) in zip((False, True), outs):
            g_lane, b_lane = _delta_lanes(reverse)
            dgate = jnp.zeros((DN_CHUNK, LANE), F32)
            for j in heads:
                n = j + (N_GROUPS if reverse else 0)
                dq_ref[j] = dq[n]
                dk_ref[j] = dk[n]
                dv_ref[j] = dv[n]
                dstate[n] = dprev[n]
                dgate = dgate + jnp.where(ids == g_lane + j, dg_col[n], 0.0) + jnp.where(ids == b_lane + j, dbeta[n], 0.0)
                drow_ref[0, j:j + 1, :] = dg_row[n]
            dgate_ref[...] = dgate

    def specs(chunk_of):
        hm = pl.BlockSpec((N_GROUPS, DN_CHUNK, HEAD), lambda c: (0, chunk_of(c), 0))
        gate = pl.BlockSpec((DN_CHUNK, LANE), lambda c: (chunk_of(c), 0))
        rows = pl.BlockSpec((1, N_GROUPS, DN_CHUNK), lambda c: (chunk_of(c), 0, 0))
        st = pl.BlockSpec((1, N_GROUPS, HEAD, HEAD), lambda c: (chunk_of(c), 0, 0, 0))
        ts = pl.BlockSpec((1, N_GROUPS, DN_CHUNK, DN_CHUNK), lambda c: (chunk_of(c), 0, 0, 0))
        return [hm, hm, hm, gate, rows, st, ts, hm], [hm, hm, hm, gate, rows]

    (ins_f, outs_f), (ins_b, outs_b) = specs(lambda c: nc - 1 - c), specs(lambda c: c)
    shapes = [jax.ShapeDtypeStruct((N_GROUPS, s, HEAD), F32)] * 3 + [jax.ShapeDtypeStruct((s, LANE), F32),
                                                                       jax.ShapeDtypeStruct((nc, N_GROUPS, DN_CHUNK), F32)]
    results, sent = _call_with_comm(
        body, comm, name="delta_bwd" + ("" if comm is None else "_comm"), grid=(nc,),
        in_specs=ins_f + ins_b, out_specs=outs_f + outs_b, out_shape=shapes + shapes,
        scratch_shapes=[pltpu.VMEM((2 * N_GROUPS, HEAD, HEAD), F32)],
        args=(q, k, v, gates, rows_f, states_f, t_f, do, q, k, v, gates, rows_b, states_b, t_b, do),
        compiler_params=_params(("arbitrary",)))
    return results[:n_out], results[n_out:], sent


ZB_BLOCK = 6


def _post_head(o_f, o_b, z, w):
    return _rms(o_f + o_b, w) * _silu(z)


def _post_fwd(o_f, o_b, proj, w, y, y_t, *, tile=256):
    s = proj.shape[0]
    tile = min(tile, s)

    def body(of_ref, ob_ref, z_ref, w_ref, _, __, y_ref, yt_ref):
        for h in range(N_GROUPS):
            lanes = pl.ds(h * HEAD, HEAD)
            y_h = _post_head(of_ref[h], ob_ref[h], z_ref[:, lanes], w_ref[...])
            y_ref[:, lanes] = y_h.astype(BF16)
            yt_ref[lanes, :] = y_h.T.astype(BF16)

    hm = pl.BlockSpec((N_GROUPS, tile, HEAD), lambda i: (0, i, 0))
    any_spec = pl.BlockSpec(memory_space=pl.ANY)
    return pl.pallas_call(
        body, name="post_fwd", grid=(s // tile,),
        in_specs=[hm, hm, pl.BlockSpec((tile, D_DN), lambda i: (i, ZB_BLOCK)), pl.BlockSpec((1, HEAD), lambda i: (0, 0)),
                  any_spec, any_spec],
        out_specs=[pl.BlockSpec((tile, D_DN), lambda i: (i, 1)), pl.BlockSpec((D_DN, tile), lambda i: (1, i))],
        out_shape=[jax.ShapeDtypeStruct(y.shape, BF16), jax.ShapeDtypeStruct(y_t.shape, BF16)],
        input_output_aliases={4: 0, 5: 1},
        compiler_params=_params(("arbitrary",)),
    )(o_f, o_b, proj, w, y, y_t)


def _post_bwd(o_f, o_b, proj, w, dy, *, tile=256):
    s = proj.shape[0]
    tile = min(tile, s)

    def body(of_ref, ob_ref, z_ref, w_ref, dy_ref, do_ref, dp_ref, dw_ref):
        @pl.when(pl.program_id(0) == 0)
        def _():
            dw_ref[...] = jnp.zeros_like(dw_ref)

        for h in range(N_GROUPS):
            lanes = pl.ds(h * HEAD, HEAD)
            _, vjp = jax.vjp(_post_head, of_ref[h], ob_ref[h], z_ref[:, lanes], w_ref[...])
            do, _, dz, dw = vjp(dy_ref[:, lanes])
            do_ref[h] = do
            dp_ref[:, lanes] = dz.astype(BF16)
            dw_ref[...] += dw

    hm = pl.BlockSpec((N_GROUPS, tile, HEAD), lambda i: (0, i, 0))
    zb = pl.BlockSpec((tile, D_DN), lambda i: (i, ZB_BLOCK))
    vec = pl.BlockSpec((1, HEAD), lambda i: (0, 0))
    return pl.pallas_call(
        body, name="post_bwd", grid=(s // tile,),
        in_specs=[hm, hm, zb, vec, pl.BlockSpec((tile, D_DN), lambda i: (i, 1))], out_specs=[hm, zb, vec],
        out_shape=[jax.ShapeDtypeStruct((N_GROUPS, s, HEAD), F32), jax.ShapeDtypeStruct((s, D_MAIN), BF16),
                   jax.ShapeDtypeStruct((1, HEAD), F32)],
        compiler_params=_params(("arbitrary",)),
    )(o_f, o_b, proj, w, dy)


def _my_index():
    return 4 * lax.axis_index("x") + 2 * lax.axis_index("y") + lax.axis_index("c")


def _peer(k):
    x, y, c = lax.axis_index("x"), lax.axis_index("y"), lax.axis_index("c")
    px, py, pc = x ^ ((k >> 2) & 1), y ^ ((k >> 1) & 1), c ^ (k & 1)
    return (px, py, pc), 4 * px + 2 * py + pc


class _Comm:
    def __init__(self, inputs, outputs, copies, by_chip=False):
        self.inputs, self.outputs, self.copies, self.by_chip = list(inputs), list(outputs), list(copies), by_chip
        self.kept = [o for o in self.outputs if not isinstance(o, jax.ShapeDtypeStruct)]

    def out_shapes(self):
        return [jax.ShapeDtypeStruct(o.shape, o.dtype) for o in self.outputs]

    def aliases(self, first_kept_input, first_output):
        kept_at = [i for i, o in enumerate(self.outputs) if not isinstance(o, jax.ShapeDtypeStruct)]
        return {first_kept_input + n: first_output + i for n, i in enumerate(kept_at)}

    def scratch(self):
        n = len(self.copies)
        return [pltpu.SemaphoreType.DMA((n, N_DEV - 1)), pltpu.SemaphoreType.DMA((n, N_DEV - 1)), pltpu.SemaphoreType.DMA((n,))]

    def _descriptors(self, in_refs, out_refs, sems, arrivals):
        send_sems, recv_sems, local_sems = sems
        me = _my_index()
        local, remote = [], []
        for k in range(N_DEV):
            peer, peer_idx = _peer(k)
            for a, (i_in, i_out, src_of, dst_of) in enumerate(self.copies):
                if k == 0:
                    if not arrivals:
                        local.append(pltpu.make_async_copy(src_of(in_refs[i_in], me), dst_of(out_refs[i_out], me), local_sems.at[a]))
                    continue
                receiver, sender = (me, peer_idx) if arrivals else (peer_idx, me)
                remote.append(pltpu.make_async_remote_copy(
                    src_ref=src_of(in_refs[i_in], receiver), dst_ref=dst_of(out_refs[i_out], sender),
                    send_sem=send_sems.at[a, k - 1], recv_sem=recv_sems.at[a, k - 1], device_id=peer,
                    device_id_type=pl.DeviceIdType.MESH))
        return local, remote

    def two_level(self, in_refs, out_refs, sems):
        send_sems, recv_sems, local_sems = sems
        me = _my_index()
        sibling, _ = _peer(1)
        n = range(len(self.copies))
        sources = [src_of(in_refs[i_in], me) for i_in, _, src_of, _ in self.copies]
        landing = lambda a, owner: self.copies[a][3](out_refs[self.copies[a][1]], owner)

        def remote(a, k, src, owner, to):
            return pltpu.make_async_remote_copy(
                src_ref=src, dst_ref=landing(a, owner), send_sem=send_sems.at[a, k - 1], recv_sem=recv_sems.at[a, k - 1],
                device_id=to, device_id_type=pl.DeviceIdType.MESH)

        local = lambda: [pltpu.make_async_copy(sources[a], landing(a, me), local_sems.at[a]) for a in n]
        first = lambda: [remote(a, k, sources[a], me, _peer(k)[0]) for k in (1, 2, 4, 6) for a in n]
        passed = lambda: [remote(a, k + 1, landing(a, _peer(k)[1]), _peer(k)[1], sibling) for k in (2, 4, 6) for a in n]
        arrivals = lambda ks: [remote(a, k, sources[a], _peer(k)[1], _peer(k)[0]) for k in ks for a in n]

        def start():
            for cp in local() + first():
                cp.start()

        def pass_on():
            for cp in arrivals((2, 4, 6)):
                cp.wait_recv()
            for cp in passed():
                cp.start()

        def finish():
            for cp in arrivals((1, 3, 5, 7)):
                cp.wait_recv()
            for cp in first() + passed():
                cp.wait_send()
            for cp in local():
                cp.wait()

        return start, pass_on, finish

    def phases(self, in_refs, out_refs, sems):
        if self.by_chip:
            return self.two_level(in_refs, out_refs, sems)
        return (lambda: self.start(in_refs, out_refs, sems)), None, (lambda: self.wait(in_refs, out_refs, sems))

    def start(self, in_refs, out_refs, sems):
        local, sends = self._descriptors(in_refs, out_refs, sems, arrivals=False)
        for cp in local + sends:
            cp.start()

    def wait(self, in_refs, out_refs, sems):
        _, arriving = self._descriptors(in_refs, out_refs, sems, arrivals=True)
        for cp in arriving:
            cp.wait_recv()
        local, sends = self._descriptors(in_refs, out_refs, sems, arrivals=False)
        for cp in sends:
            cp.wait_send()
        for cp in local:
            cp.wait()


def _call_with_comm(body, comm, *, name, grid, in_specs, out_specs, out_shape, scratch_shapes, args, compiler_params):
    if comm is None:
        return pl.pallas_call(body, name=name, grid=grid, in_specs=in_specs, out_specs=out_specs, out_shape=out_shape,
                              scratch_shapes=scratch_shapes, compiler_params=compiler_params)(*args), []
    n_in, n_out, n_scratch = len(in_specs), len(out_specs), len(scratch_shapes)
    c_in, c_kept, c_out = len(comm.inputs), len(comm.kept), len(comm.outputs)

    def at_step(position):
        here = [pl.program_id(d) == p for d, p in enumerate(position)]
        return here[0] if len(here) == 1 else functools.reduce(jnp.logical_and, here)

    def wrapped(*refs):
        main_in, comm_in = refs[:n_in], refs[n_in:n_in + c_in]
        o = n_in + c_in + c_kept
        main_out, comm_out = refs[o:o + n_out], refs[o + n_out:o + n_out + c_out]
        s = o + n_out + c_out
        main_scratch, sems = refs[s:s + n_scratch], refs[s + n_scratch:]

        start, pass_on, finish = comm.phases(comm_in, comm_out, sems)
        pl.when(at_step([0] * len(grid)))(start)
        if pass_on is not None:
            assert len(grid) == 1 and grid[0] >= 3
            pl.when(at_step([2 * grid[0] // 3]))(pass_on)
        body(*main_in, *main_out, *main_scratch)
        pl.when(at_step([g - 1 for g in grid]))(finish)

    any_spec = pl.BlockSpec(memory_space=pl.ANY)
    results = pl.pallas_call(
        wrapped, name=name, grid=grid, in_specs=list(in_specs) + [any_spec] * (c_in + c_kept),
        out_specs=list(out_specs) + [any_spec] * c_out, out_shape=list(out_shape) + comm.out_shapes(),
        scratch_shapes=list(scratch_shapes) + comm.scratch(), input_output_aliases=comm.aliases(n_in + c_in, n_out),
        compiler_params=compiler_params,
    )(*args, *comm.inputs, *comm.kept)
    return results[:n_out], results[n_out:]


def _exchange(name, comm):
    c_in, c_kept, c_out = len(comm.inputs), len(comm.kept), len(comm.outputs)

    def body(*refs):
        in_refs, out_refs, sems = refs[:c_in], refs[c_in + c_kept:c_in + c_kept + c_out], refs[c_in + c_kept + c_out:]
        for phase in comm.phases(in_refs, out_refs, sems):
            if phase is not None:
                phase()

    any_spec = pl.BlockSpec(memory_space=pl.ANY)
    return pl.pallas_call(
        body, name=name, in_specs=[any_spec] * (c_in + c_kept), out_specs=[any_spec] * c_out, out_shape=comm.out_shapes(),
        scratch_shapes=comm.scratch(), input_output_aliases=comm.aliases(c_in, 0),
    )(*comm.inputs, *comm.kept)


def _whole(ref, _):
    return ref


def _slot(ref, dev):
    return ref.at[dev]


def _row_block(size, axis):
    def pick(ref, dev):
        start = pl.multiple_of(dev * size, size) if size % SUBLANES == 0 else dev * size
        return ref.at[tuple([slice(None)] * axis + [pl.ds(start, size)])]
    return pick


def _blocked(shape, block):
    nd = len(block)
    grid = tuple(shape[d] // block[d] for d in range(nd) if block[d] != shape[d])
    return grid, (lambda *i: tuple(i) + (0,) * (nd - len(grid)))


def _sum_parts(name, parts, block, comm=None):
    n_parts = parts.shape[0]
    grid, idx = _blocked(parts.shape[1:], block)

    def body(p_ref, g_ref):
        g = p_ref[0].astype(F32)
        for j in range(1, n_parts):
            g = g + p_ref[j].astype(F32)
        g_ref[...] = g

    (total,), sent = _call_with_comm(
        body, comm, name=name, grid=grid, in_specs=[pl.BlockSpec((n_parts,) + tuple(block), lambda *i: (0,) + idx(*i))],
        out_specs=[pl.BlockSpec(block, idx)], out_shape=[jax.ShapeDtypeStruct(parts.shape[1:], F32)], scratch_shapes=[],
        args=(parts,), compiler_params=_params(("arbitrary",) * len(grid)))
    return total, sent


def _adamw(name, parts, w, m, v, block):
    n_parts = parts.shape[0]
    nd = len(block)
    grid, _ = _blocked(w.shape, block)
    lead = len(grid)

    def body(p_ref, w_ref, m_ref, v_ref, g_ref, d_ref, mo_ref, vo_ref):
        g = p_ref[0].astype(F32)
        for j in range(1, n_parts):
            g = g + p_ref[j].astype(F32)
        m_new = ADAM_B1 * m_ref[...] + (1.0 - ADAM_B1) * g
        v_new = ADAM_B2 * v_ref[...] + (1.0 - ADAM_B2) * jnp.square(g)
        m_hat = m_new / np.float32(1.0 - ADAM_B1 ** ADAM_STEP)
        v_hat = v_new / np.float32(1.0 - ADAM_B2 ** ADAM_STEP)
        g_ref[...] = g
        d_ref[...] = -ADAM_LR * (m_hat / (jnp.sqrt(v_hat) + ADAM_EPS) + ADAM_WD * w_ref[...])
        mo_ref[...] = m_new
        vo_ref[...] = v_new

    idx = lambda *i: tuple(i) + (0,) * (nd - lead)
    spec = pl.BlockSpec(block, idx)
    pspec = pl.BlockSpec((n_parts,) + tuple(block), lambda *i: (0,) + idx(*i))
    n_blk = int(np.prod(block[:-1])) * (-(-block[-1] // LANE) * LANE)
    return pl.pallas_call(
        body, name=name, grid=grid, in_specs=[pspec, spec, spec, spec], out_specs=[spec] * 4,
        out_shape=[jax.ShapeDtypeStruct(w.shape, F32)] * 4,
        compiler_params=_params(("arbitrary",) * lead, 2 * (n_parts + 7) * n_blk * 4 + VMEM_MARGIN),
    )(parts, w, m, v)


def _to_flat(pieces, row_multiple):
    flat = jnp.concatenate([p.reshape(-1) for p in pieces])
    rows = -(-flat.shape[0] // FLAT_W)
    rows = -(-rows // row_multiple) * row_multiple
    return jnp.pad(flat, (0, rows * FLAT_W - flat.shape[0])).reshape(rows, FLAT_W)


def _lane_row(*pieces):
    flat = jnp.concatenate([p.reshape(-1) for p in pieces])
    return jnp.pad(flat, (0, LANE - flat.shape[0])).reshape(1, LANE)


def _rows_view(a):
    return a.reshape(a.shape[:-1] + (SUBLANES, LANE))


def _in_shard_t(a):
    return _rows_view(jnp.swapaxes(a, 1, 2))


def kernel(x, norm_w, w_in, sgu_ln_g, sgu_ln_b, sgu_w, sgu_b, conv_w, a_log_f, a_log_b, dt_bias_f, dt_bias_b, dn_norm_w, w_out, final_norm_w, loss_target, m_norm_w, m_w_in, m_sgu_ln_g, m_sgu_ln_b, m_sgu_w, m_sgu_b, m_conv_w, m_a_log_f, m_a_log_b, m_dt_bias_f, m_dt_bias_b, m_dn_norm_w, m_w_out, m_final_norm_w, v_norm_w, v_w_in, v_sgu_ln_g, v_sgu_ln_b, v_sgu_w, v_sgu_b, v_conv_w, v_a_log_f, v_a_log_b, v_dt_bias_f, v_dt_bias_b, v_dn_norm_w, v_w_out, v_final_norm_w):
    weights = dict(norm_w=norm_w, w_in=w_in, sgu_ln_g=sgu_ln_g, sgu_ln_b=sgu_ln_b, sgu_w=sgu_w, sgu_b=sgu_b, conv_w=conv_w,
                   a_log_f=a_log_f, a_log_b=a_log_b, dt_bias_f=dt_bias_f, dt_bias_b=dt_bias_b, dn_norm_w=dn_norm_w,
                   w_out=w_out, final_norm_w=final_norm_w)
    m_in = dict(norm_w=m_norm_w, w_in=m_w_in, sgu_ln_g=m_sgu_ln_g, sgu_ln_b=m_sgu_ln_b, sgu_w=m_sgu_w, sgu_b=m_sgu_b,
                conv_w=m_conv_w, a_log_f=m_a_log_f, a_log_b=m_a_log_b, dt_bias_f=m_dt_bias_f, dt_bias_b=m_dt_bias_b,
                dn_norm_w=m_dn_norm_w, w_out=m_w_out, final_norm_w=m_final_norm_w)
    v_in = dict(norm_w=v_norm_w, w_in=v_w_in, sgu_ln_g=v_sgu_ln_g, sgu_ln_b=v_sgu_ln_b, sgu_w=v_sgu_w, sgu_b=v_sgu_b,
                conv_w=v_conv_w, a_log_f=v_a_log_f, a_log_b=v_a_log_b, dt_bias_f=v_dt_bias_f, dt_bias_b=v_dt_bias_b,
                dn_norm_w=v_dn_norm_w, w_out=v_w_out, final_norm_w=v_final_norm_w)
    s = x.shape[1]
    nc = s // DN_CHUNK
    depth = w_in.shape[0]
    in_shard, out_shard, conv_shard = w_in.shape[2], w_out.shape[1], conv_w.shape[2]
    x0 = x.reshape(s, D_MODEL)
    target = loss_target.reshape(s, D_MODEL)

    w_in_shard = _in_shard_t(w_in.astype(BF16))
    w_out_shard = w_out.astype(BF16)

    gathered_in_shape = jax.ShapeDtypeStruct((N_DEV * in_shard, SUBLANES, LANE), BF16)
    gathered_out_shape = jax.ShapeDtypeStruct((D_MODEL, D_MODEL), BF16)
    gathered_in, gathered_out, conv_all = _exchange("gather_first", _Comm(
        [w_in_shard[0], w_out_shard[0], conv_w],
        [gathered_in_shape, gathered_out_shape, jax.ShapeDtypeStruct((N_DEV,) + conv_w.shape, F32)],
        [(0, 0, _whole, _row_block(in_shard, 0)), (1, 1, _whole, _row_block(out_shard, 0)), (2, 2, _whole, _slot)], by_chip=True))
    gather_plan = lambda l: _Comm([w_in_shard[l], w_out_shard[l]], [gathered_in_shape, gathered_out_shape],
                                  [(0, 0, _whole, _row_block(in_shard, 0)), (1, 1, _whole, _row_block(out_shard, 0))], by_chip=True)
    conv_full = conv_all.transpose(1, 2, 0, 3).reshape(depth, CONV_W, 3 * D_DN)
    conv_full = jnp.pad(conv_full, ((0, 0), (0, HALO - CONV_W), (0, 0)))
    sgu_b_t = jnp.swapaxes(sgu_b, 1, 2)

    def rows_of(gates, first_lane):
        return gates[:, first_lane:first_lane + N_GROUPS].reshape(nc, DN_CHUNK, N_GROUPS).transpose(0, 2, 1)

    saved = []
    xl = x0
    for l in range(depth):
        w_in_t = gathered_in.reshape(D_IN, D_MODEL)
        w_gate_t = jnp.pad(w_in_t[D_MAIN:], ((0, LANE - N_GATE), (0, 0)))
        w_out_l = gathered_out
        nw = norm_w[l].reshape(1, D_MODEL)
        h, h_t = _rms_fwd(xl, nw)
        proj = _matmul([(_lhs(h, tm=MM_WIDE), _rhs(w_in_t, MM_WIDE, transposed=True))], s, D_MAIN, name="in_proj", trans_b=True, tn=MM_WIDE)
        p_gate = _matmul([(_lhs(h), _rhs(w_gate_t, LANE, transposed=True))], s, LANE, name="in_proj_gate", trans_b=True, tn=LANE)
        ln_g, ln_b = sgu_ln_g[l].reshape(1, D_SGU), sgu_ln_b[l].reshape(1, D_SGU)
        y, y_t = _sgu_fwd(proj, ln_g, ln_b, sgu_w[l], sgu_b_t[l])
        a_log = _lane_row(a_log_f[l], a_log_b[l])
        dt_bias = _lane_row(dt_bias_f[l], dt_bias_b[l])
        q, k, v, gates = _prep_fwd(proj, conv_full[l], p_gate, a_log, dt_bias)
        rows_f, rows_b = rows_of(gates, 0), rows_of(gates, N_GROUPS)
        more = l + 1 < depth
        (o_f, st_f, t_f), (o_b, st_b, t_b), next_weights = _delta_fwd(q, k, v, gates, rows_f, rows_b,
                                                                     comm=gather_plan(l + 1) if more else None)
        dn_w = dn_norm_w[l].reshape(1, HEAD)
        y, y_t = _post_fwd(o_f, o_b, proj, dn_w, y, y_t)
        x_next = _matmul([(_lhs(y, tm=MM_WIDE), _rhs(w_out_l, MM_WIDE))], s, D_MODEL, name="out_proj", add=xl, tn=MM_WIDE)
        saved.append(dict(x=xl, nw=nw, h_t=h_t, proj=proj, p_gate=p_gate, ln_g=ln_g, ln_b=ln_b, a_log=a_log, dt_bias=dt_bias,
                          q=q, k=k, v=v, gates=gates, rows_f=rows_f, rows_b=rows_b, o_f=o_f, o_b=o_b, st_f=st_f, st_b=st_b,
                          t_f=t_f, t_b=t_b, dn_w=dn_w, y_t=y_t, w_in_t=w_in_t, w_gate_t=w_gate_t, w_out=w_out_l))
        xl = x_next
        if more:
            gathered_in, gathered_out = next_weights

    loss_part, dx, dx_bf, d_final = _final_loss(xl, final_norm_w.reshape(1, D_MODEL), target)
    loss = lax.psum(loss_part[0, 0], MESH_AXES)

    t_rows = -(-D_IN // LANE) * LANE
    layer_small = [n for n in SMALL if n != "final_norm_w"]
    n_layer_small = sum(int(np.prod(weights[n].shape[1:])) for n in layer_small) + D_MODEL
    small_rows = -(-n_layer_small // (FLAT_W * HALO)) * HALO

    def small_segments(values):
        top = values["final_norm_w"]
        return jnp.concatenate([_to_flat([values[n][l] for n in layer_small] + [top if l == depth - 1 else jnp.zeros_like(top)], HALO)
                                for l in range(depth)], axis=0)

    def small_to_send(layer_grads, top_grad):
        conv = layer_grads["conv_w"].reshape(CONV_W, N_DEV, conv_shard).transpose(1, 0, 2).reshape(N_DEV, -1)
        rest = jnp.concatenate([layer_grads[n].reshape(-1) for n in layer_small[1:]] + [top_grad])
        flat = jnp.concatenate([conv, jnp.broadcast_to(rest, (N_DEV, rest.shape[0]))], axis=1)
        return jnp.pad(flat, ((0, 0), (0, small_rows * FLAT_W - flat.shape[1]))).reshape(N_DEV, small_rows, FLAT_W)

    layer_slot = lambda l: (lambda ref, dev: ref.at[dev, l])
    small_slot = lambda l: (lambda ref, dev: ref.at[dev, pl.ds(l * small_rows, small_rows)])
    parts = [jax.ShapeDtypeStruct((N_DEV, depth, in_shard, SUBLANES, LANE), BF16),
             jax.ShapeDtypeStruct((N_DEV, depth, out_shard, D_MODEL), BF16),
             jax.ShapeDtypeStruct((N_DEV, depth * small_rows, FLAT_W), F32)]

    def exchange_plan(items):
        source = (_row_block(in_shard, 0), _row_block(out_shard, 0), _slot)
        target = (layer_slot, layer_slot, small_slot)
        return _Comm([a for _, _, a in items], parts,
                     [(i, which, source[which], target[which](layer)) for i, (which, layer, _) in enumerate(items)])

    pending = []
    for l in reversed(range(depth)):
        sv = saved[l]
        dy = _matmul([(_lhs(dx_bf, tm=MM_WIDE), _rhs(sv["w_out"], MM_WIDE, transposed=True))], s, D_MODEL, name="out_proj_dy", trans_b=True,
                     tn=MM_WIDE)
        g_out = _matmul([(_lhs(sv["y_t"]), _rhs(dx_bf, MM_TILE))], D_MODEL, D_MODEL, name="out_proj_dw", out_dtype=BF16)
        do, dp, d_dn = _post_bwd(sv["o_f"], sv["o_b"], sv["proj"], sv["dn_w"], dy)
        (*dqkv_f, dgate_f, drows_f), (*dqkv_b, dgate_b, drows_b), exchanged = _delta_bwd(
            sv["q"], sv["k"], sv["v"], sv["gates"], sv["rows_f"], sv["rows_b"], sv["st_f"], sv["st_b"], sv["t_f"], sv["t_b"], do,
            comm=exchange_plan([(1, l, g_out)] + pending))
        parts = list(exchanged)
        drows = jnp.concatenate([drows_f.transpose(0, 2, 1).reshape(s, N_GROUPS), drows_b.transpose(0, 2, 1).reshape(s, N_GROUPS)], axis=1)
        drows = jnp.pad(drows, ((0, 0), (0, LANE - 2 * N_GROUPS)))
        dp, d_gate, d_conv, d_alog, d_dt = _prep_bwd(sv["proj"], conv_full[l], sv["p_gate"], sv["a_log"], sv["dt_bias"],
                                                     dqkv_f, dqkv_b, (dgate_f, dgate_b, drows), dp)
        dp, d_lg, d_lb, d_sw, d_sbt = _sgu_bwd(sv["proj"], sv["ln_g"], sv["ln_b"], sgu_w[l], sgu_b_t[l], dy, dp)
        h_t = sv["h_t"]
        g_in_t = _matmul([(_lhs(h_t), _rhs(dp, MM_TILE))], D_MODEL, D_MAIN, name="in_proj_dw", out_dtype=BF16, out_t_rows=t_rows)
        g_in_t = _matmul([(_lhs(h_t), _rhs(d_gate, LANE))], D_MODEL, LANE, name="in_proj_dw_gate", out_dtype=BF16, tn=LANE,
                         transpose_into=g_in_t, out_row_block=D_MAIN // LANE)
        g_in_rows = _rows_view(g_in_t)
        dh_pairs = [(_lhs(dp), _rhs(sv["w_in_t"], MM_TILE, k=D_MAIN)), (_lhs(d_gate), _rhs(sv["w_gate_t"], MM_TILE))]
        if l > 0:
            dh = _matmul(dh_pairs, s, D_MODEL, name="in_proj_dh")
        else:
            dh, exchanged = _matmul(dh_pairs, s, D_MODEL, name="in_proj_dh_comm", comm=exchange_plan([(0, l, g_in_rows)]))
            parts = list(exchanged)
        dx, dx_bf, d_nw = _rms_bwd(sv["x"], sv["nw"], dh, dx)
        layer_grads = dict(
            conv_w=d_conv[:CONV_W], norm_w=d_nw, sgu_ln_g=d_lg, sgu_ln_b=d_lb, sgu_w=d_sw, sgu_b=d_sbt.T,
            a_log_f=d_alog[0, :N_GROUPS], a_log_b=d_alog[0, N_GROUPS:2 * N_GROUPS],
            dt_bias_f=d_dt[0, :N_GROUPS], dt_bias_b=d_dt[0, N_GROUPS:2 * N_GROUPS], dn_norm_w=d_dn)
        small = small_to_send(layer_grads, d_final.reshape(D_MODEL) if l == depth - 1 else jnp.zeros((D_MODEL,), F32))
        pending = [(2, l, small)] + ([(0, l, g_in_rows)] if l > 0 else [])
    grad_x = dx.reshape(1, s, D_MODEL)

    parts_in, parts_out, parts_small = parts
    in_rows = 100 if in_shard % 100 == 0 else in_shard
    (_, bottom, bottom_small), = pending
    g_in_t, (parts_small,) = _sum_parts("sum_in_comm", parts_in, (1, in_rows, SUBLANES, LANE), comm=_Comm(
        [bottom_small], [parts_small], [(0, 0, _slot, small_slot(bottom))]))
    g_in = jnp.swapaxes(g_in_t.reshape(depth, in_shard, D_MODEL), 1, 2)
    res_in = _adamw("adamw_in", g_in[None], w_in, m_w_in, v_w_in, (1, min(256, D_MODEL), in_shard))
    res_out = _adamw("adamw_out", parts_out, w_out, m_w_out, v_w_out, (1, min(128, out_shard), D_MODEL))
    res_small = _adamw("adamw_small", parts_small, small_segments(weights), small_segments(m_in), small_segments(v_in),
                       (small_rows, FLAT_W))
    outs = [dict(w_in=a, w_out=b) for a, b in zip(res_in, res_out)]
    for o, d in zip(res_small, outs):
        by_layer, off = o.reshape(depth, small_rows * FLAT_W), 0
        for n in layer_small:
            size = int(np.prod(weights[n].shape[1:]))
            d[n] = by_layer[:, off:off + size].reshape(weights[n].shape)
            off += size
        d["final_norm_w"] = by_layer[depth - 1, off:off + D_MODEL]
    g_out, delta_out, m_out, v_out = outs
    return (loss, grad_x, *[g_out[n] for n in WEIGHTS], *[delta_out[n] for n in WEIGHTS],
            *[m_out[n] for n in WEIGHTS], *[v_out[n] for n in WEIGHTS])
```

```python
import functools

import numpy as np
import jax
import jax.numpy as jnp
from jax import lax
from jax.experimental import pallas as pl
from jax.experimental.pallas import tpu as pltpu

F32 = jnp.float32
BF16 = jnp.bfloat16

N_DEV = 8
D_MODEL = 2048
D_SGU = 1024
D_DN = 1024
N_GROUPS = 8
HEAD = 128
SGU_CHUNK = 128
DN_CHUNK = 64
CONV_W = 5
N_GATE = 32
D_MAIN = 3 * D_SGU + 4 * D_DN
D_IN = D_MAIN + N_GATE
LANE = 128
SUBLANES = 16
HALO = 8
EPS = 1e-6
ADAM_LR, ADAM_B1, ADAM_B2, ADAM_EPS, ADAM_WD, ADAM_STEP = 0.001, 0.9, 0.999, 1e-08, 0.01, 10
FLAT_W = 1024
MM_TILE = 512
MM_WIDE = 1024
VMEM_MARGIN = 8 << 20

MESH_AXES = ("x", "y", "c")
WEIGHTS = ("norm_w", "w_in", "sgu_ln_g", "sgu_ln_b", "sgu_w", "sgu_b", "conv_w", "a_log_f", "a_log_b",
           "dt_bias_f", "dt_bias_b", "dn_norm_w", "w_out", "final_norm_w")
SMALL = ("conv_w",) + tuple(n for n in WEIGHTS if n not in ("w_in", "w_out", "conv_w"))


def _bdot(a, b, dims):
    return lax.dot_general(a.astype(BF16), b.astype(BF16), (dims, ((), ())), preferred_element_type=F32)


@jax.custom_vjp
def _mm_nn(a, b):
    return _bdot(a, b, ((1,), (0,)))


def _mm_nn_fwd(a, b):
    return _mm_nn(a, b), (a, b)


def _mm_nn_bwd(res, ct):
    a, b = res
    return _bdot(ct, b, ((1,), (1,))), _bdot(a, ct, ((0,), (0,)))


_mm_nn.defvjp(_mm_nn_fwd, _mm_nn_bwd)


@jax.custom_vjp
def _mm_nt(a, b):
    return _bdot(a, b, ((1,), (1,)))


def _mm_nt_fwd(a, b):
    return _mm_nt(a, b), (a, b)


def _mm_nt_bwd(res, ct):
    a, b = res
    return _bdot(ct, b, ((1,), (0,))), _bdot(ct, a, ((0,), (0,)))


_mm_nt.defvjp(_mm_nt_fwd, _mm_nt_bwd)


@jax.custom_vjp
def _inverse_given(m, t):
    return t


def _inverse_given_fwd(m, t):
    return t, t


def _inverse_given_bwd(t, ct):
    dm = -_bdot(t, _bdot(ct, t, ((1,), (1,))), ((0,), (0,)))
    return dm, jnp.zeros_like(t)


_inverse_given.defvjp(_inverse_given_fwd, _inverse_given_bwd)


@jax.custom_vjp
def _split_lanes(x):
    return tuple(x[:, i * LANE:(i + 1) * LANE] for i in range(x.shape[1] // LANE))


def _split_lanes_fwd(x):
    return _split_lanes(x), None


def _split_lanes_bwd(_, cts):
    return (jnp.concatenate(cts, axis=1),)


_split_lanes.defvjp(_split_lanes_fwd, _split_lanes_bwd)


@functools.partial(jax.custom_vjp, nondiff_argnums=(1,))
def _split_rows(x, n):
    return x[:n], x[n:]


def _split_rows_fwd(x, n):
    return _split_rows(x, n), None


def _split_rows_bwd(n, _, cts):
    return (jnp.concatenate(cts, axis=0),)


_split_rows.defvjp(_split_rows_fwd, _split_rows_bwd)


def _silu(t):
    return t * jax.nn.sigmoid(t)


def _gelu(t):
    return 0.5 * t * (1.0 + lax.erf(t * np.float32(0.7071067811865476)))


def _rms(x, w):
    return x * lax.rsqrt(jnp.mean(x * x, axis=-1, keepdims=True) + EPS) * w


def _params(sem, vmem_bytes=None):
    kw = dict(dimension_semantics=sem)
    if vmem_bytes is not None:
        kw["vmem_limit_bytes"] = int(vmem_bytes)
    return pltpu.CompilerParams(**kw)


def _lhs(a, k=None, k_block=0, tm=MM_TILE):
    k = a.shape[1] if k is None else k
    tm = min(tm, a.shape[0])
    return a, pl.BlockSpec((tm, k), lambda i, j: (i, k_block)), tm * k * a.dtype.itemsize


def _rhs(b, tn, *, layer=None, transposed=False, k=None, k_block=0, n_offset=0):
    lead = () if layer is None else (layer,)
    none = () if layer is None else (None,)
    if transposed:
        k = b.shape[-1]
        spec = pl.BlockSpec(none + (tn, k), lambda i, j: lead + (j + n_offset, 0))
    else:
        k = b.shape[-2] if k is None else k
        spec = pl.BlockSpec(none + (k, tn), lambda i, j: lead + (k_block, j))
    return b, spec, tn * k * b.dtype.itemsize


def _matmul(pairs, m, n, *, name, trans_b=False, add=None, out_dtype=F32, tn=MM_TILE, out_t_rows=None, transpose_into=None,
            out_row_block=0, comm=None):
    tm = pairs[0][0][1].block_shape[0]
    assert m % tm == 0 and n % tn == 0 and all(a_spec.block_shape[0] == tm for (_, a_spec, _), _ in pairs)
    n_pairs = len(pairs)
    contract = ((1,), (1,)) if trans_b else ((1,), (0,))
    transposed_out = out_t_rows is not None or transpose_into is not None

    def body(*refs):
        o_ref = refs[-1]
        acc = None
        for i in range(n_pairs):
            d = lax.dot_general(refs[2 * i][...].astype(BF16), refs[2 * i + 1][...].astype(BF16),
                                (contract, ((), ())), preferred_element_type=F32)
            acc = d if acc is None else acc + d
        if add is not None:
            acc = acc + refs[2 * n_pairs][...]
        o_ref[...] = (acc.T if transposed_out else acc).astype(out_dtype)

    in_specs, args, vmem, aliases = [], [], 0, {}
    for (a, a_spec, a_bytes), (b, b_spec, b_bytes) in pairs:
        in_specs += [a_spec, b_spec]
        args += [a, b]
        vmem += 2 * (a_bytes + b_bytes)
    if add is not None:
        in_specs.append(pl.BlockSpec((tm, tn), lambda i, j: (i, j)))
        args.append(add)
        vmem += 2 * tm * tn * 4
    vmem += 2 * tm * tn * jnp.dtype(out_dtype).itemsize + 3 * tm * tn * 4
    if transposed_out:
        out_spec = pl.BlockSpec((tn, tm), lambda i, j: (j + out_row_block, i))
        if transpose_into is not None:
            in_specs.append(pl.BlockSpec(memory_space=pl.ANY))
            args.append(transpose_into)
            aliases = {len(args) - 1: 0}
            out_shape = jax.ShapeDtypeStruct(transpose_into.shape, out_dtype)
        else:
            out_shape = jax.ShapeDtypeStruct((out_t_rows, m), out_dtype)
    else:
        out_spec = pl.BlockSpec((tm, tn), lambda i, j: (i, j))
        out_shape = jax.ShapeDtypeStruct((m, n), out_dtype)
    if comm is not None:
        assert not aliases
        (out,), sent = _call_with_comm(
            body, comm, name=name, grid=(m // tm, n // tn), in_specs=in_specs, out_specs=[out_spec], out_shape=[out_shape],
            scratch_shapes=[], args=args, compiler_params=_params(("arbitrary", "arbitrary"), vmem + VMEM_MARGIN))
        return out, sent
    return pl.pallas_call(
        body, name=name, grid=(m // tm, n // tn), in_specs=in_specs, out_specs=out_spec, out_shape=out_shape,
        input_output_aliases=aliases, compiler_params=_params(("parallel", "arbitrary"), vmem + VMEM_MARGIN),
    )(*args)


def _rms_fwd(x, w, *, tile=512):
    s, d = x.shape
    tile = min(tile, s)

    def body(x_ref, w_ref, h_ref, ht_ref):
        h = _rms(x_ref[...], w_ref[...])
        h_ref[...] = h.astype(BF16)
        ht_ref[...] = h.T.astype(BF16)

    return pl.pallas_call(
        body, name="rms_fwd", grid=(s // tile,),
        in_specs=[pl.BlockSpec((tile, d), lambda i: (i, 0)), pl.BlockSpec((1, d), lambda i: (0, 0))],
        out_specs=[pl.BlockSpec((tile, d), lambda i: (i, 0)), pl.BlockSpec((d, tile), lambda i: (0, i))],
        out_shape=[jax.ShapeDtypeStruct((s, d), BF16), jax.ShapeDtypeStruct((d, s), BF16)],
        compiler_params=_params(("arbitrary",)),
    )(x, w)


def _rms_bwd(x, w, dh, dx_out, *, tile=512):
    s, d = x.shape
    tile = min(tile, s)

    def body(x_ref, w_ref, dh_ref, dxo_ref, dx_ref, dxb_ref, dw_ref):
        _, vjp = jax.vjp(_rms, x_ref[...], w_ref[...])
        dxv, dwv = vjp(dh_ref[...])
        dx = dxo_ref[...] + dxv
        dx_ref[...] = dx
        dxb_ref[...] = dx.astype(BF16)

        @pl.when(pl.program_id(0) == 0)
        def _():
            dw_ref[...] = jnp.zeros_like(dw_ref)

        dw_ref[...] += dwv

    row = pl.BlockSpec((tile, d), lambda i: (i, 0))
    vec = pl.BlockSpec((1, d), lambda i: (0, 0))
    return pl.pallas_call(
        body, name="rms_bwd", grid=(s // tile,),
        in_specs=[row, vec, row, row], out_specs=[row, row, vec],
        out_shape=[jax.ShapeDtypeStruct((s, d), F32), jax.ShapeDtypeStruct((s, d), BF16), jax.ShapeDtypeStruct((1, d), F32)],
        compiler_params=_params(("arbitrary",)),
    )(x, w, dh, dx_out)


def _final_loss(x, w, target, *, tile=256):
    s, d = x.shape
    tile = min(tile, s)

    def body(x_ref, w_ref, t_ref, loss_ref, dx_ref, dxb_ref, dw_ref):
        def f(xv, wv):
            err = jnp.square(_rms(xv, wv) - t_ref[...])
            per_token = jnp.sum(err, axis=1, keepdims=True) * np.float32(1.0 / d)
            return 0.5 * jnp.sum(per_token, axis=0, keepdims=True)

        loss, vjp = jax.vjp(f, x_ref[...], w_ref[...])
        dxv, dwv = vjp(jnp.ones((1, 1), F32))
        dx_ref[...] = dxv
        dxb_ref[...] = dxv.astype(BF16)

        @pl.when(pl.program_id(0) == 0)
        def _():
            dw_ref[...] = jnp.zeros_like(dw_ref)
            loss_ref[...] = jnp.zeros_like(loss_ref)

        dw_ref[...] += dwv
        loss_ref[...] += jnp.broadcast_to(loss, (1, LANE))

    row = pl.BlockSpec((tile, d), lambda i: (i, 0))
    vec = pl.BlockSpec((1, d), lambda i: (0, 0))
    return pl.pallas_call(
        body, name="final_loss", grid=(s // tile,),
        in_specs=[row, vec, row], out_specs=[pl.BlockSpec((1, LANE), lambda i: (0, 0)), row, row, vec],
        out_shape=[jax.ShapeDtypeStruct((1, LANE), F32), jax.ShapeDtypeStruct((s, d), F32),
                   jax.ShapeDtypeStruct((s, d), BF16), jax.ShapeDtypeStruct((1, d), F32)],
        compiler_params=_params(("arbitrary",)),
    )(x, w, target)


def _sgu_chunk(u, v, z, ln_g, ln_b, ws, bcols):
    vg = _gelu(v)
    xc = vg - jnp.mean(vg, axis=-1, keepdims=True)
    vl = xc * lax.rsqrt(jnp.mean(xc * xc, axis=-1, keepdims=True) + EPS) * ln_g + ln_b
    sp = jnp.concatenate([_mm_nn(ws[g], vg_) + bcols[g] for g, vg_ in enumerate(_split_lanes(vl))], axis=1)
    return _gelu(u) * sp * _silu(z)


def _sgu_specs(tile):
    col = lambda j: pl.BlockSpec((tile, D_SGU), lambda i, j=j: (i, j))
    vec = pl.BlockSpec((1, D_SGU), lambda i: (0, 0))
    wspec = pl.BlockSpec((N_GROUPS, SGU_CHUNK, SGU_CHUNK), lambda i: (0, 0, 0))
    bspec = pl.BlockSpec((SGU_CHUNK, N_GROUPS), lambda i: (0, 0))
    return col, vec, wspec, bspec


def _sgu_fwd(proj, ln_g, ln_b, w_s, b_t, *, tile=256):
    s = proj.shape[0]
    tile = min(tile, s)
    col, vec, wspec, bspec = _sgu_specs(tile)

    def body(u_ref, v_ref, z_ref, g_ref, b_ref, w_ref, bt_ref, y_ref, yt_ref):
        ws = tuple(w_ref[g] for g in range(N_GROUPS))
        bcols = tuple(bt_ref[:, g:g + 1] for g in range(N_GROUPS))
        for c in range(tile // SGU_CHUNK):
            rows = pl.ds(c * SGU_CHUNK, SGU_CHUNK)
            y = _sgu_chunk(u_ref[rows, :], v_ref[rows, :], z_ref[rows, :], g_ref[...], b_ref[...], ws, bcols)
            y_ref[rows, :] = y.astype(BF16)
            yt_ref[:, rows] = y.T.astype(BF16)

    return pl.pallas_call(
        body, name="sgu_fwd", grid=(s // tile,),
        in_specs=[col(0), col(1), col(2), vec, vec, wspec, bspec],
        out_specs=[pl.BlockSpec((tile, D_SGU), lambda i: (i, 0)), pl.BlockSpec((D_SGU, tile), lambda i: (0, i))],
        out_shape=[jax.ShapeDtypeStruct((s, D_SGU + D_DN), BF16), jax.ShapeDtypeStruct((D_SGU + D_DN, s), BF16)],
        compiler_params=_params(("arbitrary",)),
    )(proj, proj, proj, ln_g, ln_b, w_s, b_t)


def _sgu_bwd(proj, ln_g, ln_b, w_s, b_t, dy, dp, *, tile=256):
    s = proj.shape[0]
    tile = min(tile, s)
    col, vec, wspec, bspec = _sgu_specs(tile)

    def body(u_ref, v_ref, z_ref, g_ref, b_ref, w_ref, bt_ref, dy_ref, _, dp_ref, dg_ref, db_ref, dw_ref, dbt_ref):
        @pl.when(pl.program_id(0) == 0)
        def _():
            dg_ref[...] = jnp.zeros_like(dg_ref)
            db_ref[...] = jnp.zeros_like(db_ref)
            dw_ref[...] = jnp.zeros_like(dw_ref)
            dbt_ref[...] = jnp.zeros_like(dbt_ref)

        ws = tuple(w_ref[g] for g in range(N_GROUPS))
        bcols = tuple(bt_ref[:, g:g + 1] for g in range(N_GROUPS))
        for c in range(tile // SGU_CHUNK):
            rows = pl.ds(c * SGU_CHUNK, SGU_CHUNK)
            _, vjp = jax.vjp(_sgu_chunk, u_ref[rows, :], v_ref[rows, :], z_ref[rows, :], g_ref[...], b_ref[...], ws, bcols)
            du, dv, dz, dg, db, dws, dbc = vjp(dy_ref[rows, :])
            dp_ref[rows, pl.ds(0, D_SGU)] = du.astype(BF16)
            dp_ref[rows, pl.ds(D_SGU, D_SGU)] = dv.astype(BF16)
            dp_ref[rows, pl.ds(2 * D_SGU, D_SGU)] = dz.astype(BF16)
            dg_ref[...] += dg
            db_ref[...] += db
            for g in range(N_GROUPS):
                dw_ref[g] += dws[g]
                dbt_ref[:, g:g + 1] += dbc[g]

    return pl.pallas_call(
        body, name="sgu_bwd", grid=(s // tile,),
        in_specs=[col(0), col(1), col(2), vec, vec, wspec, bspec, pl.BlockSpec((tile, D_SGU), lambda i: (i, 0)),
                  pl.BlockSpec(memory_space=pl.ANY)],
        out_specs=[pl.BlockSpec((tile, 3 * D_SGU), lambda i: (i, 0)), vec, vec, wspec, bspec],
        out_shape=[jax.ShapeDtypeStruct(dp.shape, BF16), jax.ShapeDtypeStruct((1, D_SGU), F32),
                   jax.ShapeDtypeStruct((1, D_SGU), F32), jax.ShapeDtypeStruct((N_GROUPS, SGU_CHUNK, SGU_CHUNK), F32),
                   jax.ShapeDtypeStruct((SGU_CHUNK, N_GROUPS), F32)],
        input_output_aliases={8: 0},
        compiler_params=_params(("arbitrary",)),
    )(proj, proj, proj, ln_g, ln_b, w_s, b_t, dy, dp)


QKV_BLOCK = 1


def _qkv_head(cq, ck, cv):
    q, k, v = _silu(cq), _silu(ck), _silu(cv)
    q = q * lax.rsqrt(jnp.sum(q * q, axis=-1, keepdims=True) + EPS) * np.float32(HEAD ** -0.5)
    k = k * lax.rsqrt(jnp.sum(k * k, axis=-1, keepdims=True) + EPS)
    return q, k, v


def _gate_fn(p, a_log, dt_bias):
    lane = lax.broadcasted_iota(jnp.int32, p.shape, 1)
    g = -jnp.exp(a_log) * jax.nn.softplus(p + dt_bias)
    return jnp.where(lane < 16, g, jnp.where(lane < N_GATE, jax.nn.sigmoid(p), 0.0))


def _shifted_rows(full, shift, start, rows):
    n = full.shape[0]
    assert start % HALO == 0 and 0 <= start + shift and start + shift + rows <= n
    rolled = full if shift == 0 else pltpu.roll(full, (-shift) % n, 0)
    return rolled[start:start + rows, :]


def _halo_maps(tile, s):
    per = tile // HALO
    prev = lambda i: jnp.maximum(i * per - 1, 0)
    nxt = lambda i: jnp.minimum((i + 1) * per, s // HALO - 1)
    return prev, nxt


def _prep_fwd(proj, conv_w, pg, a_log, dt_bias, *, tile=256):
    s = proj.shape[0]
    width = 3 * D_DN
    tile = min(tile, s)
    n_blk = s // tile
    prev, nxt = _halo_maps(tile, s)

    def body(x_ref, xp_ref, xn_ref, w_ref, pg_ref, al_ref, dt_ref, q_ref, k_ref, v_ref, gate_ref, ext):
        i = pl.program_id(0)
        ext[pl.ds(0, HALO), :] = jnp.where(i > 0, xp_ref[...], 0.0)
        ext[pl.ds(HALO, tile), :] = x_ref[...]
        ext[pl.ds(HALO + tile, HALO), :] = jnp.where(i < n_blk - 1, xn_ref[...], 0.0)
        full, c = ext[...], None
        for j in range(CONV_W):
            term = _shifted_rows(full, j - CONV_W // 2, HALO, tile) * w_ref[j:j + 1, :]
            c = term if c is None else c + term
        for h in range(N_GROUPS):
            q, k, v = _qkv_head(c[:, h * HEAD:(h + 1) * HEAD], c[:, D_DN + h * HEAD:D_DN + (h + 1) * HEAD],
                                c[:, 2 * D_DN + h * HEAD:2 * D_DN + (h + 1) * HEAD])
            q_ref[h] = q
            k_ref[h] = k
            v_ref[h] = v
        gate_ref[...] = _gate_fn(pg_ref[...], al_ref[...], dt_ref[...])

    hm = pl.BlockSpec((N_GROUPS, tile, HEAD), lambda i: (0, i, 0))
    lane_vec = pl.BlockSpec((1, LANE), lambda i: (0, 0))
    return pl.pallas_call(
        body, name="prep_fwd", grid=(n_blk,),
        in_specs=[pl.BlockSpec((tile, width), lambda i: (i, QKV_BLOCK)),
                  pl.BlockSpec((HALO, width), lambda i: (prev(i), QKV_BLOCK)),
                  pl.BlockSpec((HALO, width), lambda i: (nxt(i), QKV_BLOCK)),
                  pl.BlockSpec((HALO, width), lambda i: (0, 0)),
                  pl.BlockSpec((tile, LANE), lambda i: (i, 0)), lane_vec, lane_vec],
        out_specs=[hm, hm, hm, pl.BlockSpec((tile, LANE), lambda i: (i, 0))],
        out_shape=[jax.ShapeDtypeStruct((N_GROUPS, s, HEAD), F32)] * 3 + [jax.ShapeDtypeStruct((s, LANE), F32)],
        scratch_shapes=[pltpu.VMEM((tile + 2 * HALO, width), F32)],
        compiler_params=_params(("arbitrary",)),
    )(proj, proj, proj, conv_w, pg, a_log, dt_bias)


def _prep_bwd(proj, conv_w, pg, a_log, dt_bias, dqkv_f, dqkv_b, dgates, dp, *, tile=128):
    s = proj.shape[0]
    width = 3 * D_DN
    tile = min(tile, s)
    n_blk = s // tile
    prev, nxt = _halo_maps(tile, s)
    ext_rows = tile + 2 * HALO
    pad = CONV_W // 2

    def body(*refs):
        x_ref, xp_ref, xn_ref, w_ref, pg_ref, al_ref, dt_ref = refs[:7]
        ct_refs = refs[7:25]
        dg0_ref, dg1_ref, dg2_ref = refs[25:28]
        dp_ref, dpg_ref, dw_ref, dal_ref, ddt_ref, xext, dcext = refs[29:]
        i = pl.program_id(0)
        has_prev, has_next = i > 0, i < n_blk - 1

        @pl.when(i == 0)
        def _():
            dw_ref[...] = jnp.zeros_like(dw_ref)
            dal_ref[...] = jnp.zeros_like(dal_ref)
            ddt_ref[...] = jnp.zeros_like(ddt_ref)

        zeros = jnp.zeros((HALO, width), F32)
        xext[pl.ds(0, HALO), :] = zeros
        xext[pl.ds(HALO, HALO), :] = jnp.where(has_prev, xp_ref[...], 0.0)
        xext[pl.ds(2 * HALO, tile), :] = x_ref[...]
        xext[pl.ds(2 * HALO + tile, HALO), :] = jnp.where(has_next, xn_ref[...], 0.0)
        xext[pl.ds(3 * HALO + tile, HALO), :] = zeros
        xfull, c = xext[...], None
        for j in range(CONV_W):
            term = _shifted_rows(xfull, j - pad, HALO, ext_rows) * w_ref[j:j + 1, :]
            c = term if c is None else c + term
        for h in range(N_GROUPS):
            lanes = [pl.ds(part * D_DN + h * HEAD, HEAD) for part in range(3)]
            cts = []
            for part in range(3):
                (cur, before, after), (cur2, before2, after2) = ct_refs[3 * part:3 * part + 3], ct_refs[9 + 3 * part:12 + 3 * part]
                cts.append(jnp.concatenate([jnp.where(has_prev, before[h] + before2[h], 0.0), cur[h] + cur2[h],
                                            jnp.where(has_next, after[h] + after2[h], 0.0)], axis=0))
            _, vjp = jax.vjp(_qkv_head, c[:, h * HEAD:(h + 1) * HEAD], c[:, D_DN + h * HEAD:D_DN + (h + 1) * HEAD],
                             c[:, 2 * D_DN + h * HEAD:2 * D_DN + (h + 1) * HEAD])
            for lane_sl, dc in zip(lanes, vjp(tuple(cts))):
                dcext[:, lane_sl] = dc
        dcfull = dcext[...]
        dx, dc_blk = None, dcfull[HALO:HALO + tile, :]
        for j in range(CONV_W):
            term = _shifted_rows(dcfull, pad - j, HALO, tile) * w_ref[j:j + 1, :]
            dx = term if dx is None else dx + term
            dw_ref[j:j + 1, :] += jnp.sum(dc_blk * _shifted_rows(xfull, j - pad, 2 * HALO, tile), axis=0, keepdims=True)
        dp_ref[...] = dx.astype(BF16)
        _, gvjp = jax.vjp(_gate_fn, pg_ref[...], al_ref[...], dt_ref[...])
        dpg, dal, ddt = gvjp(dg0_ref[...] + dg1_ref[...] + dg2_ref[...])
        dpg_ref[...] = dpg.astype(BF16)
        dal_ref[...] += dal
        ddt_ref[...] += ddt

    xs = [pl.BlockSpec((tile, width), lambda i: (i, QKV_BLOCK)), pl.BlockSpec((HALO, width), lambda i: (prev(i), QKV_BLOCK)),
          pl.BlockSpec((HALO, width), lambda i: (nxt(i), QKV_BLOCK))]
    hm = [pl.BlockSpec((N_GROUPS, tile, HEAD), lambda i: (0, i, 0)),
          pl.BlockSpec((N_GROUPS, HALO, HEAD), lambda i: (0, prev(i), 0)),
          pl.BlockSpec((N_GROUPS, HALO, HEAD), lambda i: (0, nxt(i), 0))]
    gate = pl.BlockSpec((tile, LANE), lambda i: (i, 0))
    lane_vec = pl.BlockSpec((1, LANE), lambda i: (0, 0))
    wspec = pl.BlockSpec((HALO, width), lambda i: (0, 0))
    n_in = 3 + 4 + 18 + 3
    ct_arrays = [a for dqkv in (dqkv_f, dqkv_b) for a in dqkv for _ in range(3)]
    return pl.pallas_call(
        body, name="prep_bwd", grid=(n_blk,),
        in_specs=xs + [wspec, gate, lane_vec, lane_vec] + hm * 6 + [gate] * 3 + [pl.BlockSpec(memory_space=pl.ANY)],
        out_specs=[pl.BlockSpec((tile, width), lambda i: (i, QKV_BLOCK)), gate, wspec, lane_vec, lane_vec],
        out_shape=[jax.ShapeDtypeStruct(dp.shape, BF16), jax.ShapeDtypeStruct((s, LANE), BF16),
                   jax.ShapeDtypeStruct((HALO, width), F32), jax.ShapeDtypeStruct((1, LANE), F32),
                   jax.ShapeDtypeStruct((1, LANE), F32)],
        input_output_aliases={n_in: 0},
        scratch_shapes=[pltpu.VMEM((tile + 4 * HALO, width), F32), pltpu.VMEM((ext_rows, width), F32)],
        compiler_params=_params(("arbitrary",)),
    )(proj, proj, proj, conv_w, pg, a_log, dt_bias, *ct_arrays, *dgates, dp)


def _each(f, *cols):
    return tuple(f(*a) for a in zip(*cols))


def _delta_chunk(q, k, v, g_col, g_row, beta, state, t_known, reverse):
    n = DN_CHUNK
    ri = lax.broadcasted_iota(jnp.int32, (n, n), 0)
    ci = lax.broadcasted_iota(jnp.int32, (n, n), 1)
    masks = {False: (ri >= ci, ri <= ci, ri > ci), True: (ri <= ci, ri >= ci, ri < ci)}
    flags = tuple(reverse) if isinstance(reverse, (tuple, list)) else (reverse,) * len(q)
    incl, incl_t, strict = (tuple(masks[f][i] for f in flags) for i in range(3))
    gc_col = _each(lambda gr, mk: jnp.sum(jnp.where(mk, gr, 0.0), axis=1, keepdims=True), g_row, incl)
    gc_row = _each(lambda gc, mk: jnp.sum(jnp.where(mk, gc, 0.0), axis=0, keepdims=True), g_col, incl_t)
    g_tot = _each(lambda gr: jnp.sum(gr, axis=1, keepdims=True), g_row)
    decay = _each(lambda a, b, mk: jnp.where(mk, jnp.exp(jnp.where(mk, a - b, 0.0)), 0.0), gc_col, gc_row, incl)
    rows = lambda *xs: jnp.concatenate(xs, axis=0)
    kb = _each(lambda a, b: a * b, k, beta)
    kk_qk = _each(lambda kb_, q_, k_: _split_rows(_mm_nt(rows(kb_, q_), k_), n), kb, q, k)
    m = _each(lambda two, d, mk: jnp.where(mk, two[0] * d, 0.0), kk_qk, decay, strict)
    if t_known is None:
        eye = (ri == ci).astype(F32)
        p = _each(lambda a: -a, m)
        t = tuple(eye for _ in p)
        for _ in range(5):
            both = _each(lambda p_, t_: _bdot(rows(p_, t_), p_, ((1,), (0,))), p, t)
            p = _each(lambda b: b[:n], both)
            t = _each(lambda t_, b: t_ + b[n:], t, both)
        t = _each(lambda t_, p_: t_ + _bdot(t_, p_, ((1,), (0,))), t, p)
    else:
        t = _each(_inverse_given, m, t_known)
    e_gc = _each(jnp.exp, gc_col)
    u_w = _each(lambda t_, v_, b_, kb_, e_: _split_lanes(_mm_nn(t_, jnp.concatenate([v_ * b_, kb_ * e_], axis=1))), t, v, beta, kb, e_gc)
    attn = _each(lambda two, d_: two[1] * d_, kk_qk, decay)
    ws_qs = _each(lambda uw, q_, e_, s_: _split_rows(_mm_nn(rows(uw[1], q_ * e_), s_), n), u_w, q, e_gc, state)
    v_new = _each(lambda uw, wq: uw[0] - wq[0], u_w, ws_qs)
    k_dec_t = _each(lambda k_, gt, gc: jnp.transpose(k_ * jnp.exp(gt - gc)), k, g_tot, gc_col)
    av_kv = _each(lambda a_, kt, vn: _split_rows(_mm_nn(rows(a_, kt), vn), n), attn, k_dec_t, v_new)
    o = _each(lambda wq, ak: wq[1] + ak[0], ws_qs, av_kv)
    new_state = _each(lambda s_, gt, ak: s_ * jnp.exp(gt) + ak[1], state, g_tot, av_kv)
    return o, new_state, t


def _delta_lanes(reverse):
    return (N_GROUPS, 3 * N_GROUPS) if reverse else (0, 2 * N_GROUPS)


def _delta_fwd(q, k, v, gates, rows_f, rows_b, *, comm=None):
    s = q.shape[1]
    nc = s // DN_CHUNK
    heads = range(N_GROUPS)

    def body(qf_ref, kf_ref, vf_ref, gf_ref, rf_ref, qb_ref, kb_ref, vb_ref, gb_ref, rb_ref,
             of_ref, sf_ref, tf_ref, ob_ref, sb_ref, tb_ref, state):
        @pl.when(pl.program_id(0) == 0)
        def _():
            state[...] = jnp.zeros_like(state)

        qs, ks, vs, g_cols, g_rows, betas, flags = [], [], [], [], [], [], []
        for reverse, q_ref, k_ref, v_ref, gate, row_ref in ((False, qf_ref, kf_ref, vf_ref, gf_ref[...], rf_ref),
                                                            (True, qb_ref, kb_ref, vb_ref, gb_ref[...], rb_ref)):
            g_lane, b_lane = _delta_lanes(reverse)
            for j in heads:
                qs.append(q_ref[j])
                ks.append(k_ref[j])
                vs.append(v_ref[j])
                g_cols.append(gate[:, g_lane + j:g_lane + j + 1])
                g_rows.append(row_ref[0, j:j + 1, :])
                betas.append(gate[:, b_lane + j:b_lane + j + 1])
                flags.append(reverse)
        prev = tuple(state[n] for n in range(2 * N_GROUPS))
        o, new, t = _delta_chunk(tuple(qs), tuple(ks), tuple(vs), tuple(g_cols), tuple(g_rows), tuple(betas), prev, None, tuple(flags))
        for n in range(2 * N_GROUPS):
            o_ref, st_ref, t_ref = (of_ref, sf_ref, tf_ref) if n < N_GROUPS else (ob_ref, sb_ref, tb_ref)
            j = n % N_GROUPS
            st_ref[0, j] = prev[n]
            o_ref[j] = o[n]
            t_ref[0, j] = t[n]
            state[n] = new[n]

    def specs(chunk_of):
        hm = pl.BlockSpec((N_GROUPS, DN_CHUNK, HEAD), lambda c: (0, chunk_of(c), 0))
        ins = [hm, hm, hm, pl.BlockSpec((DN_CHUNK, LANE), lambda c: (chunk_of(c), 0)),
               pl.BlockSpec((1, N_GROUPS, DN_CHUNK), lambda c: (chunk_of(c), 0, 0))]
        outs = [hm, pl.BlockSpec((1, N_GROUPS, HEAD, HEAD), lambda c: (chunk_of(c), 0, 0, 0)),
                pl.BlockSpec((1, N_GROUPS, DN_CHUNK, DN_CHUNK), lambda c: (chunk_of(c), 0, 0, 0))]
        return ins, outs

    (ins_f, outs_f), (ins_b, outs_b) = specs(lambda c: c), specs(lambda c: nc - 1 - c)
    shapes = [jax.ShapeDtypeStruct((N_GROUPS, s, HEAD), F32), jax.ShapeDtypeStruct((nc, N_GROUPS, HEAD, HEAD), F32),
              jax.ShapeDtypeStruct((nc, N_GROUPS, DN_CHUNK, DN_CHUNK), F32)]
    results, sent = _call_with_comm(
        body, comm, name="delta_fwd" + ("" if comm is None else "_comm"), grid=(nc,),
        in_specs=ins_f + ins_b, out_specs=outs_f + outs_b, out_shape=shapes + shapes,
        scratch_shapes=[pltpu.VMEM((2 * N_GROUPS, HEAD, HEAD), F32)],
        args=(q, k, v, gates, rows_f, q, k, v, gates, rows_b), compiler_params=_params(("arbitrary",)))
    return results[:3], results[3:], sent


def _delta_bwd(q, k, v, gates, rows_f, rows_b, states_f, states_b, t_f, t_b, do, *, comm=None):
    s = q.shape[1]
    nc = s // DN_CHUNK
    heads = range(N_GROUPS)
    n_in, n_out = 8, 5

    def body(*refs):
        ins = (refs[:n_in], refs[n_in:2 * n_in])
        outs = (refs[2 * n_in:2 * n_in + n_out], refs[2 * n_in + n_out:2 * n_in + 2 * n_out])
        dstate = refs[-1]

        @pl.when(pl.program_id(0) == 0)
        def _():
            dstate[...] = jnp.zeros_like(dstate)

        qs, ks, vs, g_cols, g_rows, betas, sts, ts, dos, flags = ([] for _ in range(10))
        for reverse, (q_ref, k_ref, v_ref, gate_ref, row_ref, st_ref, t_ref, do_ref) in zip((False, True), ins):
            gate = gate_ref[...]
            g_lane, b_lane = _delta_lanes(reverse)
            for j in heads:
                qs.append(q_ref[j])
                ks.append(k_ref[j])
                vs.append(v_ref[j])
                g_cols.append(gate[:, g_lane + j:g_lane + j + 1])
                g_rows.append(row_ref[0, j:j + 1, :])
                betas.append(gate[:, b_lane + j:b_lane + j + 1])
                sts.append(st_ref[0, j])
                ts.append(t_ref[0, j])
                dos.append(do_ref[j])
                flags.append(reverse)
        t_known, flags = tuple(ts), tuple(flags)
        f = lambda q_, k_, v_, gc_, gr_, b_, s_: _delta_chunk(q_, k_, v_, gc_, gr_, b_, s_, t_known, flags)[:2]
        _, vjp = jax.vjp(f, tuple(qs), tuple(ks), tuple(vs), tuple(g_cols), tuple(g_rows), tuple(betas), tuple(sts))
        dq, dk, dv, dg_col, dg_row, dbeta, dprev = vjp((tuple(dos), tuple(dstate[n] for n in range(2 * N_GROUPS))))
        ids = lax.broadcasted_iota(jnp.int32, (DN_CHUNK, LANE), 1)
        for reverse, (dq_ref, dk_ref, dv_ref, dgate_ref, drow_ref) in zip((False, True), outs):
            g_lane, b_lane = _delta_lanes(reverse)
            dgate = jnp.zeros((DN_CHUNK, LANE), F32)
            for j in heads:
                n = j + (N_GROUPS if reverse else 0)
                dq_ref[j] = dq[n]
                dk_ref[j] = dk[n]
                dv_ref[j] = dv[n]
                dstate[n] = dprev[n]
                dgate = dgate + jnp.where(ids == g_lane + j, dg_col[n], 0.0) + jnp.where(ids == b_lane + j, dbeta[n], 0.0)
                drow_ref[0, j:j + 1, :] = dg_row[n]
            dgate_ref[...] = dgate

    def specs(chunk_of):
        hm = pl.BlockSpec((N_GROUPS, DN_CHUNK, HEAD), lambda c: (0, chunk_of(c), 0))
        gate = pl.BlockSpec((DN_CHUNK, LANE), lambda c: (chunk_of(c), 0))
        rows = pl.BlockSpec((1, N_GROUPS, DN_CHUNK), lambda c: (chunk_of(c), 0, 0))
        st = pl.BlockSpec((1, N_GROUPS, HEAD, HEAD), lambda c: (chunk_of(c), 0, 0, 0))
        ts = pl.BlockSpec((1, N_GROUPS, DN_CHUNK, DN_CHUNK), lambda c: (chunk_of(c), 0, 0, 0))
        return [hm, hm, hm, gate, rows, st, ts, hm], [hm, hm, hm, gate, rows]

    (ins_f, outs_f), (ins_b, outs_b) = specs(lambda c: nc - 1 - c), specs(lambda c: c)
    shapes = [jax.ShapeDtypeStruct((N_GROUPS, s, HEAD), F32)] * 3 + [jax.ShapeDtypeStruct((s, LANE), F32),
                                                                       jax.ShapeDtypeStruct((nc, N_GROUPS, DN_CHUNK), F32)]
    results, sent = _call_with_comm(
        body, comm, name="delta_bwd" + ("" if comm is None else "_comm"), grid=(nc,),
        in_specs=ins_f + ins_b, out_specs=outs_f + outs_b, out_shape=shapes + shapes,
        scratch_shapes=[pltpu.VMEM((2 * N_GROUPS, HEAD, HEAD), F32)],
        args=(q, k, v, gates, rows_f, states_f, t_f, do, q, k, v, gates, rows_b, states_b, t_b, do),
        compiler_params=_params(("arbitrary",)))
    return results[:n_out], results[n_out:], sent


ZB_BLOCK = 6


def _post_head(o_f, o_b, z, w):
    return _rms(o_f + o_b, w) * _silu(z)


def _post_fwd(o_f, o_b, proj, w, y, y_t, *, tile=512):
    s = proj.shape[0]
    tile = min(tile, s)

    def body(of_ref, ob_ref, z_ref, w_ref, _, __, y_ref, yt_ref):
        for h in range(N_GROUPS):
            lanes = pl.ds(h * HEAD, HEAD)
            y_h = _post_head(of_ref[h], ob_ref[h], z_ref[:, lanes], w_ref[...])
            y_ref[:, lanes] = y_h.astype(BF16)
            yt_ref[lanes, :] = y_h.T.astype(BF16)

    hm = pl.BlockSpec((N_GROUPS, tile, HEAD), lambda i: (0, i, 0))
    any_spec = pl.BlockSpec(memory_space=pl.ANY)
    return pl.pallas_call(
        body, name="post_fwd", grid=(s // tile,),
        in_specs=[hm, hm, pl.BlockSpec((tile, D_DN), lambda i: (i, ZB_BLOCK)), pl.BlockSpec((1, HEAD), lambda i: (0, 0)),
                  any_spec, any_spec],
        out_specs=[pl.BlockSpec((tile, D_DN), lambda i: (i, 1)), pl.BlockSpec((D_DN, tile), lambda i: (1, i))],
        out_shape=[jax.ShapeDtypeStruct(y.shape, BF16), jax.ShapeDtypeStruct(y_t.shape, BF16)],
        input_output_aliases={4: 0, 5: 1},
        compiler_params=_params(("arbitrary",)),
    )(o_f, o_b, proj, w, y, y_t)


def _post_bwd(o_f, o_b, proj, w, dy, *, tile=512):
    s = proj.shape[0]
    tile = min(tile, s)

    def body(of_ref, ob_ref, z_ref, w_ref, dy_ref, do_ref, dp_ref, dw_ref):
        @pl.when(pl.program_id(0) == 0)
        def _():
            dw_ref[...] = jnp.zeros_like(dw_ref)

        for h in range(N_GROUPS):
            lanes = pl.ds(h * HEAD, HEAD)
            _, vjp = jax.vjp(_post_head, of_ref[h], ob_ref[h], z_ref[:, lanes], w_ref[...])
            do, _, dz, dw = vjp(dy_ref[:, lanes])
            do_ref[h] = do
            dp_ref[:, lanes] = dz.astype(BF16)
            dw_ref[...] += dw

    hm = pl.BlockSpec((N_GROUPS, tile, HEAD), lambda i: (0, i, 0))
    zb = pl.BlockSpec((tile, D_DN), lambda i: (i, ZB_BLOCK))
    vec = pl.BlockSpec((1, HEAD), lambda i: (0, 0))
    return pl.pallas_call(
        body, name="post_bwd", grid=(s // tile,),
        in_specs=[hm, hm, zb, vec, pl.BlockSpec((tile, D_DN), lambda i: (i, 1))], out_specs=[hm, zb, vec],
        out_shape=[jax.ShapeDtypeStruct((N_GROUPS, s, HEAD), F32), jax.ShapeDtypeStruct((s, D_MAIN), BF16),
                   jax.ShapeDtypeStruct((1, HEAD), F32)],
        compiler_params=_params(("arbitrary",)),
    )(o_f, o_b, proj, w, dy)


def _my_index():
    return 4 * lax.axis_index("x") + 2 * lax.axis_index("y") + lax.axis_index("c")


def _peer(k):
    x, y, c = lax.axis_index("x"), lax.axis_index("y"), lax.axis_index("c")
    px, py, pc = x ^ ((k >> 2) & 1), y ^ ((k >> 1) & 1), c ^ (k & 1)
    return (px, py, pc), 4 * px + 2 * py + pc


class _Comm:
    def __init__(self, inputs, outputs, copies, by_chip=False):
        self.inputs, self.outputs, self.copies, self.by_chip = list(inputs), list(outputs), list(copies), by_chip
        self.kept = [o for o in self.outputs if not isinstance(o, jax.ShapeDtypeStruct)]

    def out_shapes(self):
        return [jax.ShapeDtypeStruct(o.shape, o.dtype) for o in self.outputs]

    def aliases(self, first_kept_input, first_output):
        kept_at = [i for i, o in enumerate(self.outputs) if not isinstance(o, jax.ShapeDtypeStruct)]
        return {first_kept_input + n: first_output + i for n, i in enumerate(kept_at)}

    def scratch(self):
        n = len(self.copies)
        return [pltpu.SemaphoreType.DMA((n, N_DEV - 1)), pltpu.SemaphoreType.DMA((n, N_DEV - 1)), pltpu.SemaphoreType.DMA((n,))]

    def _descriptors(self, in_refs, out_refs, sems, arrivals):
        send_sems, recv_sems, local_sems = sems
        me = _my_index()
        local, remote = [], []
        for k in range(N_DEV):
            peer, peer_idx = _peer(k)
            for a, (i_in, i_out, src_of, dst_of) in enumerate(self.copies):
                if k == 0:
                    if not arrivals:
                        local.append(pltpu.make_async_copy(src_of(in_refs[i_in], me), dst_of(out_refs[i_out], me), local_sems.at[a]))
                    continue
                receiver, sender = (me, peer_idx) if arrivals else (peer_idx, me)
                remote.append(pltpu.make_async_remote_copy(
                    src_ref=src_of(in_refs[i_in], receiver), dst_ref=dst_of(out_refs[i_out], sender),
                    send_sem=send_sems.at[a, k - 1], recv_sem=recv_sems.at[a, k - 1], device_id=peer,
                    device_id_type=pl.DeviceIdType.MESH))
        return local, remote

    def two_level(self, in_refs, out_refs, sems):
        send_sems, recv_sems, local_sems = sems
        me = _my_index()
        sibling, _ = _peer(1)
        n = range(len(self.copies))
        sources = [src_of(in_refs[i_in], me) for i_in, _, src_of, _ in self.copies]
        landing = lambda a, owner: self.copies[a][3](out_refs[self.copies[a][1]], owner)

        def remote(a, k, src, owner, to):
            return pltpu.make_async_remote_copy(
                src_ref=src, dst_ref=landing(a, owner), send_sem=send_sems.at[a, k - 1], recv_sem=recv_sems.at[a, k - 1],
                device_id=to, device_id_type=pl.DeviceIdType.MESH)

        local = lambda: [pltpu.make_async_copy(sources[a], landing(a, me), local_sems.at[a]) for a in n]
        first = lambda: [remote(a, k, sources[a], me, _peer(k)[0]) for k in (1, 2, 4, 6) for a in n]
        passed = lambda: [remote(a, k + 1, landing(a, _peer(k)[1]), _peer(k)[1], sibling) for k in (2, 4, 6) for a in n]
        arrivals = lambda ks: [remote(a, k, sources[a], _peer(k)[1], _peer(k)[0]) for k in ks for a in n]

        def start():
            for cp in local() + first():
                cp.start()

        def pass_on():
            for cp in arrivals((2, 4, 6)):
                cp.wait_recv()
            for cp in passed():
                cp.start()

        def finish():
            for cp in arrivals((1, 3, 5, 7)):
                cp.wait_recv()
            for cp in first() + passed():
                cp.wait_send()
            for cp in local():
                cp.wait()

        return start, pass_on, finish

    def phases(self, in_refs, out_refs, sems):
        if self.by_chip:
            return self.two_level(in_refs, out_refs, sems)
        return (lambda: self.start(in_refs, out_refs, sems)), None, (lambda: self.wait(in_refs, out_refs, sems))

    def start(self, in_refs, out_refs, sems):
        local, sends = self._descriptors(in_refs, out_refs, sems, arrivals=False)
        for cp in local + sends:
            cp.start()

    def wait(self, in_refs, out_refs, sems):
        _, arriving = self._descriptors(in_refs, out_refs, sems, arrivals=True)
        for cp in arriving:
            cp.wait_recv()
        local, sends = self._descriptors(in_refs, out_refs, sems, arrivals=False)
        for cp in sends:
            cp.wait_send()
        for cp in local:
            cp.wait()


def _call_with_comm(body, comm, *, name, grid, in_specs, out_specs, out_shape, scratch_shapes, args, compiler_params):
    if comm is None:
        return pl.pallas_call(body, name=name, grid=grid, in_specs=in_specs, out_specs=out_specs, out_shape=out_shape,
                              scratch_shapes=scratch_shapes, compiler_params=compiler_params)(*args), []
    n_in, n_out, n_scratch = len(in_specs), len(out_specs), len(scratch_shapes)
    c_in, c_kept, c_out = len(comm.inputs), len(comm.kept), len(comm.outputs)

    def at_step(position):
        here = [pl.program_id(d) == p for d, p in enumerate(position)]
        return here[0] if len(here) == 1 else functools.reduce(jnp.logical_and, here)

    def wrapped(*refs):
        main_in, comm_in = refs[:n_in], refs[n_in:n_in + c_in]
        o = n_in + c_in + c_kept
        main_out, comm_out = refs[o:o + n_out], refs[o + n_out:o + n_out + c_out]
        s = o + n_out + c_out
        main_scratch, sems = refs[s:s + n_scratch], refs[s + n_scratch:]

        start, pass_on, finish = comm.phases(comm_in, comm_out, sems)
        pl.when(at_step([0] * len(grid)))(start)
        if pass_on is not None:
            assert len(grid) == 1 and grid[0] >= 3
            pl.when(at_step([2 * grid[0] // 3]))(pass_on)
        body(*main_in, *main_out, *main_scratch)
        pl.when(at_step([g - 1 for g in grid]))(finish)

    any_spec = pl.BlockSpec(memory_space=pl.ANY)
    results = pl.pallas_call(
        wrapped, name=name, grid=grid, in_specs=list(in_specs) + [any_spec] * (c_in + c_kept),
        out_specs=list(out_specs) + [any_spec] * c_out, out_shape=list(out_shape) + comm.out_shapes(),
        scratch_shapes=list(scratch_shapes) + comm.scratch(), input_output_aliases=comm.aliases(n_in + c_in, n_out),
        compiler_params=compiler_params,
    )(*args, *comm.inputs, *comm.kept)
    return results[:n_out], results[n_out:]


def _exchange(name, comm):
    c_in, c_kept, c_out = len(comm.inputs), len(comm.kept), len(comm.outputs)

    def body(*refs):
        in_refs, out_refs, sems = refs[:c_in], refs[c_in + c_kept:c_in + c_kept + c_out], refs[c_in + c_kept + c_out:]
        for phase in comm.phases(in_refs, out_refs, sems):
            if phase is not None:
                phase()

    any_spec = pl.BlockSpec(memory_space=pl.ANY)
    return pl.pallas_call(
        body, name=name, in_specs=[any_spec] * (c_in + c_kept), out_specs=[any_spec] * c_out, out_shape=comm.out_shapes(),
        scratch_shapes=comm.scratch(), input_output_aliases=comm.aliases(c_in, 0),
    )(*comm.inputs, *comm.kept)


def _whole(ref, _):
    return ref


def _slot(ref, dev):
    return ref.at[dev]


def _row_block(size, axis):
    def pick(ref, dev):
        start = pl.multiple_of(dev * size, size) if size % SUBLANES == 0 else dev * size
        return ref.at[tuple([slice(None)] * axis + [pl.ds(start, size)])]
    return pick


def _blocked(shape, block):
    nd = len(block)
    grid = tuple(shape[d] // block[d] for d in range(nd) if block[d] != shape[d])
    return grid, (lambda *i: tuple(i) + (0,) * (nd - len(grid)))


def _sum_parts(name, parts, block, comm=None):
    n_parts = parts.shape[0]
    grid, idx = _blocked(parts.shape[1:], block)

    def body(p_ref, g_ref):
        g = p_ref[0].astype(F32)
        for j in range(1, n_parts):
            g = g + p_ref[j].astype(F32)
        g_ref[...] = g

    (total,), sent = _call_with_comm(
        body, comm, name=name, grid=grid, in_specs=[pl.BlockSpec((n_parts,) + tuple(block), lambda *i: (0,) + idx(*i))],
        out_specs=[pl.BlockSpec(block, idx)], out_shape=[jax.ShapeDtypeStruct(parts.shape[1:], F32)], scratch_shapes=[],
        args=(parts,), compiler_params=_params(("arbitrary",) * len(grid)))
    return total, sent


def _adamw(name, parts, w, m, v, block):
    n_parts = parts.shape[0]
    nd = len(block)
    grid, _ = _blocked(w.shape, block)
    lead = len(grid)

    def body(p_ref, w_ref, m_ref, v_ref, g_ref, d_ref, mo_ref, vo_ref):
        g = p_ref[0].astype(F32)
        for j in range(1, n_parts):
            g = g + p_ref[j].astype(F32)
        m_new = ADAM_B1 * m_ref[...] + (1.0 - ADAM_B1) * g
        v_new = ADAM_B2 * v_ref[...] + (1.0 - ADAM_B2) * jnp.square(g)
        m_hat = m_new / np.float32(1.0 - ADAM_B1 ** ADAM_STEP)
        v_hat = v_new / np.float32(1.0 - ADAM_B2 ** ADAM_STEP)
        g_ref[...] = g
        d_ref[...] = -ADAM_LR * (m_hat / (jnp.sqrt(v_hat) + ADAM_EPS) + ADAM_WD * w_ref[...])
        mo_ref[...] = m_new
        vo_ref[...] = v_new

    idx = lambda *i: tuple(i) + (0,) * (nd - lead)
    spec = pl.BlockSpec(block, idx)
    pspec = pl.BlockSpec((n_parts,) + tuple(block), lambda *i: (0,) + idx(*i))
    n_blk = int(np.prod(block[:-1])) * (-(-block[-1] // LANE) * LANE)
    return pl.pallas_call(
        body, name=name, grid=grid, in_specs=[pspec, spec, spec, spec], out_specs=[spec] * 4,
        out_shape=[jax.ShapeDtypeStruct(w.shape, F32)] * 4,
        compiler_params=_params(("arbitrary",) * lead, 2 * (n_parts + 7) * n_blk * 4 + VMEM_MARGIN),
    )(parts, w, m, v)


def _to_flat(pieces, row_multiple):
    flat = jnp.concatenate([p.reshape(-1) for p in pieces])
    rows = -(-flat.shape[0] // FLAT_W)
    rows = -(-rows // row_multiple) * row_multiple
    return jnp.pad(flat, (0, rows * FLAT_W - flat.shape[0])).reshape(rows, FLAT_W)


def _lane_row(*pieces):
    flat = jnp.concatenate([p.reshape(-1) for p in pieces])
    return jnp.pad(flat, (0, LANE - flat.shape[0])).reshape(1, LANE)


def _rows_view(a):
    return a.reshape(a.shape[:-1] + (SUBLANES, LANE))


def _in_shard_t(a):
    return _rows_view(jnp.swapaxes(a, 1, 2))


def kernel(x, norm_w, w_in, sgu_ln_g, sgu_ln_b, sgu_w, sgu_b, conv_w, a_log_f, a_log_b, dt_bias_f, dt_bias_b, dn_norm_w, w_out, final_norm_w, loss_target, m_norm_w, m_w_in, m_sgu_ln_g, m_sgu_ln_b, m_sgu_w, m_sgu_b, m_conv_w, m_a_log_f, m_a_log_b, m_dt_bias_f, m_dt_bias_b, m_dn_norm_w, m_w_out, m_final_norm_w, v_norm_w, v_w_in, v_sgu_ln_g, v_sgu_ln_b, v_sgu_w, v_sgu_b, v_conv_w, v_a_log_f, v_a_log_b, v_dt_bias_f, v_dt_bias_b, v_dn_norm_w, v_w_out, v_final_norm_w):
    weights = dict(norm_w=norm_w, w_in=w_in, sgu_ln_g=sgu_ln_g, sgu_ln_b=sgu_ln_b, sgu_w=sgu_w, sgu_b=sgu_b, conv_w=conv_w,
                   a_log_f=a_log_f, a_log_b=a_log_b, dt_bias_f=dt_bias_f, dt_bias_b=dt_bias_b, dn_norm_w=dn_norm_w,
                   w_out=w_out, final_norm_w=final_norm_w)
    m_in = dict(norm_w=m_norm_w, w_in=m_w_in, sgu_ln_g=m_sgu_ln_g, sgu_ln_b=m_sgu_ln_b, sgu_w=m_sgu_w, sgu_b=m_sgu_b,
                conv_w=m_conv_w, a_log_f=m_a_log_f, a_log_b=m_a_log_b, dt_bias_f=m_dt_bias_f, dt_bias_b=m_dt_bias_b,
                dn_norm_w=m_dn_norm_w, w_out=m_w_out, final_norm_w=m_final_norm_w)
    v_in = dict(norm_w=v_norm_w, w_in=v_w_in, sgu_ln_g=v_sgu_ln_g, sgu_ln_b=v_sgu_ln_b, sgu_w=v_sgu_w, sgu_b=v_sgu_b,
                conv_w=v_conv_w, a_log_f=v_a_log_f, a_log_b=v_a_log_b, dt_bias_f=v_dt_bias_f, dt_bias_b=v_dt_bias_b,
                dn_norm_w=v_dn_norm_w, w_out=v_w_out, final_norm_w=v_final_norm_w)
    s = x.shape[1]
    nc = s // DN_CHUNK
    depth = w_in.shape[0]
    in_shard, out_shard, conv_shard = w_in.shape[2], w_out.shape[1], conv_w.shape[2]
    x0 = x.reshape(s, D_MODEL)
    target = loss_target.reshape(s, D_MODEL)

    w_in_shard = _in_shard_t(w_in.astype(BF16))
    w_out_shard = w_out.astype(BF16)

    gathered_in_shape = jax.ShapeDtypeStruct((N_DEV * in_shard, SUBLANES, LANE), BF16)
    gathered_out_shape = jax.ShapeDtypeStruct((D_MODEL, D_MODEL), BF16)
    gathered_in, gathered_out, conv_all = _exchange("gather_first", _Comm(
        [w_in_shard[0], w_out_shard[0], conv_w],
        [gathered_in_shape, gathered_out_shape, jax.ShapeDtypeStruct((N_DEV,) + conv_w.shape, F32)],
        [(0, 0, _whole, _row_block(in_shard, 0)), (1, 1, _whole, _row_block(out_shard, 0)), (2, 2, _whole, _slot)], by_chip=True))
    gather_plan = lambda l: _Comm([w_in_shard[l], w_out_shard[l]], [gathered_in_shape, gathered_out_shape],
                                  [(0, 0, _whole, _row_block(in_shard, 0)), (1, 1, _whole, _row_block(out_shard, 0))], by_chip=True)
    conv_full = conv_all.transpose(1, 2, 0, 3).reshape(depth, CONV_W, 3 * D_DN)
    conv_full = jnp.pad(conv_full, ((0, 0), (0, HALO - CONV_W), (0, 0)))
    sgu_b_t = jnp.swapaxes(sgu_b, 1, 2)

    def rows_of(gates, first_lane):
        return gates[:, first_lane:first_lane + N_GROUPS].reshape(nc, DN_CHUNK, N_GROUPS).transpose(0, 2, 1)

    saved = []
    xl = x0
    for l in range(depth):
        w_in_t = gathered_in.reshape(D_IN, D_MODEL)
        w_gate_t = jnp.pad(w_in_t[D_MAIN:], ((0, LANE - N_GATE), (0, 0)))
        w_out_l = gathered_out
        nw = norm_w[l].reshape(1, D_MODEL)
        h, h_t = _rms_fwd(xl, nw)
        proj = _matmul([(_lhs(h, tm=MM_WIDE), _rhs(w_in_t, MM_WIDE, transposed=True))], s, D_MAIN, name="in_proj", trans_b=True, tn=MM_WIDE)
        p_gate = _matmul([(_lhs(h), _rhs(w_gate_t, LANE, transposed=True))], s, LANE, name="in_proj_gate", trans_b=True, tn=LANE)
        ln_g, ln_b = sgu_ln_g[l].reshape(1, D_SGU), sgu_ln_b[l].reshape(1, D_SGU)
        y, y_t = _sgu_fwd(proj, ln_g, ln_b, sgu_w[l], sgu_b_t[l])
        a_log = _lane_row(a_log_f[l], a_log_b[l])
        dt_bias = _lane_row(dt_bias_f[l], dt_bias_b[l])
        q, k, v, gates = _prep_fwd(proj, conv_full[l], p_gate, a_log, dt_bias)
        rows_f, rows_b = rows_of(gates, 0), rows_of(gates, N_GROUPS)
        more = l + 1 < depth
        (o_f, st_f, t_f), (o_b, st_b, t_b), next_weights = _delta_fwd(q, k, v, gates, rows_f, rows_b,
                                                                     comm=gather_plan(l + 1) if more else None)
        dn_w = dn_norm_w[l].reshape(1, HEAD)
        y, y_t = _post_fwd(o_f, o_b, proj, dn_w, y, y_t)
        x_next = _matmul([(_lhs(y, tm=MM_WIDE), _rhs(w_out_l, MM_WIDE))], s, D_MODEL, name="out_proj", add=xl, tn=MM_WIDE)
        saved.append(dict(x=xl, nw=nw, h_t=h_t, proj=proj, p_gate=p_gate, ln_g=ln_g, ln_b=ln_b, a_log=a_log, dt_bias=dt_bias,
                          q=q, k=k, v=v, gates=gates, rows_f=rows_f, rows_b=rows_b, o_f=o_f, o_b=o_b, st_f=st_f, st_b=st_b,
                          t_f=t_f, t_b=t_b, dn_w=dn_w, y_t=y_t, w_in_t=w_in_t, w_gate_t=w_gate_t, w_out=w_out_l))
        xl = x_next
        if more:
            gathered_in, gathered_out = next_weights

    loss_part, dx, dx_bf, d_final = _final_loss(xl, final_norm_w.reshape(1, D_MODEL), target)
    loss = lax.psum(loss_part[0, 0], MESH_AXES)

    t_rows = -(-D_IN // LANE) * LANE
    layer_small = [n for n in SMALL if n != "final_norm_w"]
    n_layer_small = sum(int(np.prod(weights[n].shape[1:])) for n in layer_small) + D_MODEL
    small_rows = -(-n_layer_small // (FLAT_W * HALO)) * HALO

    def small_segments(values):
        top = values["final_norm_w"]
        return jnp.concatenate([_to_flat([values[n][l] for n in layer_small] + [top if l == depth - 1 else jnp.zeros_like(top)], HALO)
                                for l in range(depth)], axis=0)

    def small_to_send(layer_grads, top_grad):
        conv = layer_grads["conv_w"].reshape(CONV_W, N_DEV, conv_shard).transpose(1, 0, 2).reshape(N_DEV, -1)
        rest = jnp.concatenate([layer_grads[n].reshape(-1) for n in layer_small[1:]] + [top_grad])
        flat = jnp.concatenate([conv, jnp.broadcast_to(rest, (N_DEV, rest.shape[0]))], axis=1)
        return jnp.pad(flat, ((0, 0), (0, small_rows * FLAT_W - flat.shape[1]))).reshape(N_DEV, small_rows, FLAT_W)

    layer_slot = lambda l: (lambda ref, dev: ref.at[dev, l])
    small_slot = lambda l: (lambda ref, dev: ref.at[dev, pl.ds(l * small_rows, small_rows)])
    parts = [jax.ShapeDtypeStruct((N_DEV, depth, in_shard, SUBLANES, LANE), BF16),
             jax.ShapeDtypeStruct((N_DEV, depth, out_shard, D_MODEL), BF16),
             jax.ShapeDtypeStruct((N_DEV, depth * small_rows, FLAT_W), F32)]

    def exchange_plan(items):
        source = (_row_block(in_shard, 0), _row_block(out_shard, 0), _slot)
        target = (layer_slot, layer_slot, small_slot)
        return _Comm([a for _, _, a in items], parts,
                     [(i, which, source[which], target[which](layer)) for i, (which, layer, _) in enumerate(items)])

    pending = []
    for l in reversed(range(depth)):
        sv = saved[l]
        dy = _matmul([(_lhs(dx_bf, tm=MM_WIDE), _rhs(sv["w_out"], MM_WIDE, transposed=True))], s, D_MODEL, name="out_proj_dy", trans_b=True,
                     tn=MM_WIDE)
        g_out = _matmul([(_lhs(sv["y_t"]), _rhs(dx_bf, MM_TILE))], D_MODEL, D_MODEL, name="out_proj_dw", out_dtype=BF16)
        do, dp, d_dn = _post_bwd(sv["o_f"], sv["o_b"], sv["proj"], sv["dn_w"], dy)
        (*dqkv_f, dgate_f, drows_f), (*dqkv_b, dgate_b, drows_b), exchanged = _delta_bwd(
            sv["q"], sv["k"], sv["v"], sv["gates"], sv["rows_f"], sv["rows_b"], sv["st_f"], sv["st_b"], sv["t_f"], sv["t_b"], do,
            comm=exchange_plan([(1, l, g_out)] + pending))
        parts = list(exchanged)
        drows = jnp.concatenate([drows_f.transpose(0, 2, 1).reshape(s, N_GROUPS), drows_b.transpose(0, 2, 1).reshape(s, N_GROUPS)], axis=1)
        drows = jnp.pad(drows, ((0, 0), (0, LANE - 2 * N_GROUPS)))
        dp, d_gate, d_conv, d_alog, d_dt = _prep_bwd(sv["proj"], conv_full[l], sv["p_gate"], sv["a_log"], sv["dt_bias"],
                                                     dqkv_f, dqkv_b, (dgate_f, dgate_b, drows), dp)
        dp, d_lg, d_lb, d_sw, d_sbt = _sgu_bwd(sv["proj"], sv["ln_g"], sv["ln_b"], sgu_w[l], sgu_b_t[l], dy, dp)
        h_t = sv["h_t"]
        g_in_t = _matmul([(_lhs(h_t), _rhs(dp, MM_TILE))], D_MODEL, D_MAIN, name="in_proj_dw", out_dtype=BF16, out_t_rows=t_rows)
        g_in_t = _matmul([(_lhs(h_t), _rhs(d_gate, LANE))], D_MODEL, LANE, name="in_proj_dw_gate", out_dtype=BF16, tn=LANE,
                         transpose_into=g_in_t, out_row_block=D_MAIN // LANE)
        g_in_rows = _rows_view(g_in_t)
        dh_pairs = [(_lhs(dp), _rhs(sv["w_in_t"], MM_TILE, k=D_MAIN)), (_lhs(d_gate), _rhs(sv["w_gate_t"], MM_TILE))]
        if l > 0:
            dh = _matmul(dh_pairs, s, D_MODEL, name="in_proj_dh")
        else:
            dh, exchanged = _matmul(dh_pairs, s, D_MODEL, name="in_proj_dh_comm", comm=exchange_plan([(0, l, g_in_rows)]))
            parts = list(exchanged)
        dx, dx_bf, d_nw = _rms_bwd(sv["x"], sv["nw"], dh, dx)
        layer_grads = dict(
            conv_w=d_conv[:CONV_W], norm_w=d_nw, sgu_ln_g=d_lg, sgu_ln_b=d_lb, sgu_w=d_sw, sgu_b=d_sbt.T,
            a_log_f=d_alog[0, :N_GROUPS], a_log_b=d_alog[0, N_GROUPS:2 * N_GROUPS],
            dt_bias_f=d_dt[0, :N_GROUPS], dt_bias_b=d_dt[0, N_GROUPS:2 * N_GROUPS], dn_norm_w=d_dn)
        small = small_to_send(layer_grads, d_final.reshape(D_MODEL) if l == depth - 1 else jnp.zeros((D_MODEL,), F32))
        pending = [(2, l, small)] + ([(0, l, g_in_rows)] if l > 0 else [])
    grad_x = dx.reshape(1, s, D_MODEL)

    parts_in, parts_out, parts_small = parts
    in_rows = 100 if in_shard % 100 == 0 else in_shard
    (_, bottom, bottom_small), = pending
    g_in_t, (parts_small,) = _sum_parts("sum_in_comm", parts_in, (1, in_rows, SUBLANES, LANE), comm=_Comm(
        [bottom_small], [parts_small], [(0, 0, _slot, small_slot(bottom))]))
    g_in = jnp.swapaxes(g_in_t.reshape(depth, in_shard, D_MODEL), 1, 2)
    res_in = _adamw("adamw_in", g_in[None], w_in, m_w_in, v_w_in, (1, min(256, D_MODEL), in_shard))
    res_out = _adamw("adamw_out", parts_out, w_out, m_w_out, v_w_out, (1, min(128, out_shard), D_MODEL))
    res_small = _adamw("adamw_small", parts_small, small_segments(weights), small_segments(m_in), small_segments(v_in),
                       (small_rows, FLAT_W))
    outs = [dict(w_in=a, w_out=b) for a, b in zip(res_in, res_out)]
    for o, d in zip(res_small, outs):
        by_layer, off = o.reshape(depth, small_rows * FLAT_W), 0
        for n in layer_small:
            size = int(np.prod(weights[n].shape[1:]))
            d[n] = by_layer[:, off:off + size].reshape(weights[n].shape)
            off += size
        d["final_norm_w"] = by_layer[depth - 1, off:off + D_MODEL]
    g_out, delta_out, m_out, v_out = outs
    return (loss, grad_x, *[g_out[n] for n in WEIGHTS], *[delta_out[n] for n in WEIGHTS],
            *[m_out[n] for n in WEIGHTS], *[v_out[n] for n in WEIGHTS])
```

```python
import functools

import numpy as np
import jax
import jax.numpy as jnp
from jax import lax
from jax.experimental import pallas as pl
from jax.experimental.pallas import tpu as pltpu

F32 = jnp.float32
BF16 = jnp.bfloat16

N_DEV = 8
D_MODEL = 2048
D_SGU = 1024
D_DN = 1024
N_GROUPS = 8
HEAD = 128
SGU_CHUNK = 128
DN_CHUNK = 64
CONV_W = 5
N_GATE = 32
D_MAIN = 3 * D_SGU + 4 * D_DN
D_IN = D_MAIN + N_GATE
LANE = 128
SUBLANES = 16
HALO = 8
EPS = 1e-6
ADAM_LR, ADAM_B1, ADAM_B2, ADAM_EPS, ADAM_WD, ADAM_STEP = 0.001, 0.9, 0.999, 1e-08, 0.01, 10
FLAT_W = 1024
MM_TILE = 512
MM_WIDE = 1024
VMEM_MARGIN = 8 << 20

MESH_AXES = ("x", "y", "c")
WEIGHTS = ("norm_w", "w_in", "sgu_ln_g", "sgu_ln_b", "sgu_w", "sgu_b", "conv_w", "a_log_f", "a_log_b",
           "dt_bias_f", "dt_bias_b", "dn_norm_w", "w_out", "final_norm_w")
SMALL = ("conv_w",) + tuple(n for n in WEIGHTS if n not in ("w_in", "w_out", "conv_w"))


def _bdot(a, b, dims):
    return lax.dot_general(a.astype(BF16), b.astype(BF16), (dims, ((), ())), preferred_element_type=F32)


@jax.custom_vjp
def _mm_nn(a, b):
    return _bdot(a, b, ((1,), (0,)))


def _mm_nn_fwd(a, b):
    return _mm_nn(a, b), (a, b)


def _mm_nn_bwd(res, ct):
    a, b = res
    return _bdot(ct, b, ((1,), (1,))), _bdot(a, ct, ((0,), (0,)))


_mm_nn.defvjp(_mm_nn_fwd, _mm_nn_bwd)


@jax.custom_vjp
def _mm_nt(a, b):
    return _bdot(a, b, ((1,), (1,)))


def _mm_nt_fwd(a, b):
    return _mm_nt(a, b), (a, b)


def _mm_nt_bwd(res, ct):
    a, b = res
    return _bdot(ct, b, ((1,), (0,))), _bdot(ct, a, ((0,), (0,)))


_mm_nt.defvjp(_mm_nt_fwd, _mm_nt_bwd)


@jax.custom_vjp
def _inverse_given(m, t):
    return t


def _inverse_given_fwd(m, t):
    return t, t


def _inverse_given_bwd(t, ct):
    dm = -_bdot(t, _bdot(ct, t, ((1,), (1,))), ((0,), (0,)))
    return dm, jnp.zeros_like(t)


_inverse_given.defvjp(_inverse_given_fwd, _inverse_given_bwd)


@jax.custom_vjp
def _split_lanes(x):
    return tuple(x[:, i * LANE:(i + 1) * LANE] for i in range(x.shape[1] // LANE))


def _split_lanes_fwd(x):
    return _split_lanes(x), None


def _split_lanes_bwd(_, cts):
    return (jnp.concatenate(cts, axis=1),)


_split_lanes.defvjp(_split_lanes_fwd, _split_lanes_bwd)


@functools.partial(jax.custom_vjp, nondiff_argnums=(1,))
def _split_rows(x, n):
    return x[:n], x[n:]


def _split_rows_fwd(x, n):
    return _split_rows(x, n), None


def _split_rows_bwd(n, _, cts):
    return (jnp.concatenate(cts, axis=0),)


_split_rows.defvjp(_split_rows_fwd, _split_rows_bwd)


def _silu(t):
    return t * jax.nn.sigmoid(t)


def _gelu(t):
    return 0.5 * t * (1.0 + lax.erf(t * np.float32(0.7071067811865476)))


def _rms(x, w):
    return x * lax.rsqrt(jnp.mean(x * x, axis=-1, keepdims=True) + EPS) * w


def _params(sem, vmem_bytes=None):
    kw = dict(dimension_semantics=sem)
    if vmem_bytes is not None:
        kw["vmem_limit_bytes"] = int(vmem_bytes)
    return pltpu.CompilerParams(**kw)


def _lhs(a, k=None, k_block=0, tm=MM_TILE):
    k = a.shape[1] if k is None else k
    tm = min(tm, a.shape[0])
    return a, pl.BlockSpec((tm, k), lambda i, j: (i, k_block)), tm * k * a.dtype.itemsize


def _rhs(b, tn, *, layer=None, transposed=False, k=None, k_block=0, n_offset=0):
    lead = () if layer is None else (layer,)
    none = () if layer is None else (None,)
    if transposed:
        k = b.shape[-1]
        spec = pl.BlockSpec(none + (tn, k), lambda i, j: lead + (j + n_offset, 0))
    else:
        k = b.shape[-2] if k is None else k
        spec = pl.BlockSpec(none + (k, tn), lambda i, j: lead + (k_block, j))
    return b, spec, tn * k * b.dtype.itemsize


def _matmul(pairs, m, n, *, name, trans_b=False, add=None, out_dtype=F32, tn=MM_TILE, out_t_rows=None, transpose_into=None,
            out_row_block=0, comm=None):
    tm = pairs[0][0][1].block_shape[0]
    assert m % tm == 0 and n % tn == 0 and all(a_spec.block_shape[0] == tm for (_, a_spec, _), _ in pairs)
    n_pairs = len(pairs)
    contract = ((1,), (1,)) if trans_b else ((1,), (0,))
    transposed_out = out_t_rows is not None or transpose_into is not None

    def body(*refs):
        o_ref = refs[-1]
        acc = None
        for i in range(n_pairs):
            d = lax.dot_general(refs[2 * i][...].astype(BF16), refs[2 * i + 1][...].astype(BF16),
                                (contract, ((), ())), preferred_element_type=F32)
            acc = d if acc is None else acc + d
        if add is not None:
            acc = acc + refs[2 * n_pairs][...]
        o_ref[...] = (acc.T if transposed_out else acc).astype(out_dtype)

    in_specs, args, vmem, aliases = [], [], 0, {}
    for (a, a_spec, a_bytes), (b, b_spec, b_bytes) in pairs:
        in_specs += [a_spec, b_spec]
        args += [a, b]
        vmem += 2 * (a_bytes + b_bytes)
    if add is not None:
        in_specs.append(pl.BlockSpec((tm, tn), lambda i, j: (i, j)))
        args.append(add)
        vmem += 2 * tm * tn * 4
    vmem += 2 * tm * tn * jnp.dtype(out_dtype).itemsize + 3 * tm * tn * 4
    if transposed_out:
        out_spec = pl.BlockSpec((tn, tm), lambda i, j: (j + out_row_block, i))
        if transpose_into is not None:
            in_specs.append(pl.BlockSpec(memory_space=pl.ANY))
            args.append(transpose_into)
            aliases = {len(args) - 1: 0}
            out_shape = jax.ShapeDtypeStruct(transpose_into.shape, out_dtype)
        else:
            out_shape = jax.ShapeDtypeStruct((out_t_rows, m), out_dtype)
    else:
        out_spec = pl.BlockSpec((tm, tn), lambda i, j: (i, j))
        out_shape = jax.ShapeDtypeStruct((m, n), out_dtype)
    if comm is not None:
        assert not aliases
        (out,), sent = _call_with_comm(
            body, comm, name=name, grid=(m // tm, n // tn), in_specs=in_specs, out_specs=[out_spec], out_shape=[out_shape],
            scratch_shapes=[], args=args, compiler_params=_params(("arbitrary", "arbitrary"), vmem + VMEM_MARGIN))
        return out, sent
    return pl.pallas_call(
        body, name=name, grid=(m // tm, n // tn), in_specs=in_specs, out_specs=out_spec, out_shape=out_shape,
        input_output_aliases=aliases, compiler_params=_params(("parallel", "arbitrary"), vmem + VMEM_MARGIN),
    )(*args)


def _rms_fwd(x, w, *, tile=512):
    s, d = x.shape
    tile = min(tile, s)

    def body(x_ref, w_ref, h_ref, ht_ref):
        h = _rms(x_ref[...], w_ref[...])
        h_ref[...] = h.astype(BF16)
        ht_ref[...] = h.T.astype(BF16)

    return pl.pallas_call(
        body, name="rms_fwd", grid=(s // tile,),
        in_specs=[pl.BlockSpec((tile, d), lambda i: (i, 0)), pl.BlockSpec((1, d), lambda i: (0, 0))],
        out_specs=[pl.BlockSpec((tile, d), lambda i: (i, 0)), pl.BlockSpec((d, tile), lambda i: (0, i))],
        out_shape=[jax.ShapeDtypeStruct((s, d), BF16), jax.ShapeDtypeStruct((d, s), BF16)],
        compiler_params=_params(("arbitrary",)),
    )(x, w)


def _rms_bwd(x, w, dh, dx_out, *, tile=512):
    s, d = x.shape
    tile = min(tile, s)

    def body(x_ref, w_ref, dh_ref, dxo_ref, dx_ref, dxb_ref, dw_ref):
        _, vjp = jax.vjp(_rms, x_ref[...], w_ref[...])
        dxv, dwv = vjp(dh_ref[...])
        dx = dxo_ref[...] + dxv
        dx_ref[...] = dx
        dxb_ref[...] = dx.astype(BF16)

        @pl.when(pl.program_id(0) == 0)
        def _():
            dw_ref[...] = jnp.zeros_like(dw_ref)

        dw_ref[...] += dwv

    row = pl.BlockSpec((tile, d), lambda i: (i, 0))
    vec = pl.BlockSpec((1, d), lambda i: (0, 0))
    return pl.pallas_call(
        body, name="rms_bwd", grid=(s // tile,),
        in_specs=[row, vec, row, row], out_specs=[row, row, vec],
        out_shape=[jax.ShapeDtypeStruct((s, d), F32), jax.ShapeDtypeStruct((s, d), BF16), jax.ShapeDtypeStruct((1, d), F32)],
        compiler_params=_params(("arbitrary",)),
    )(x, w, dh, dx_out)


def _final_loss(x, w, target, *, tile=512):
    s, d = x.shape
    tile = min(tile, s)

    def body(x_ref, w_ref, t_ref, loss_ref, dx_ref, dxb_ref, dw_ref):
        def f(xv, wv):
            err = jnp.square(_rms(xv, wv) - t_ref[...])
            per_token = jnp.sum(err, axis=1, keepdims=True) * np.float32(1.0 / d)
            return 0.5 * jnp.sum(per_token, axis=0, keepdims=True)

        loss, vjp = jax.vjp(f, x_ref[...], w_ref[...])
        dxv, dwv = vjp(jnp.ones((1, 1), F32))
        dx_ref[...] = dxv
        dxb_ref[...] = dxv.astype(BF16)

        @pl.when(pl.program_id(0) == 0)
        def _():
            dw_ref[...] = jnp.zeros_like(dw_ref)
            loss_ref[...] = jnp.zeros_like(loss_ref)

        dw_ref[...] += dwv
        loss_ref[...] += jnp.broadcast_to(loss, (1, LANE))

    row = pl.BlockSpec((tile, d), lambda i: (i, 0))
    vec = pl.BlockSpec((1, d), lambda i: (0, 0))
    return pl.pallas_call(
        body, name="final_loss", grid=(s // tile,),
        in_specs=[row, vec, row], out_specs=[pl.BlockSpec((1, LANE), lambda i: (0, 0)), row, row, vec],
        out_shape=[jax.ShapeDtypeStruct((1, LANE), F32), jax.ShapeDtypeStruct((s, d), F32),
                   jax.ShapeDtypeStruct((s, d), BF16), jax.ShapeDtypeStruct((1, d), F32)],
        compiler_params=_params(("arbitrary",)),
    )(x, w, target)


def _sgu_chunk(u, v, z, ln_g, ln_b, ws, bcols):
    vg = _gelu(v)
    xc = vg - jnp.mean(vg, axis=-1, keepdims=True)
    vl = xc * lax.rsqrt(jnp.mean(xc * xc, axis=-1, keepdims=True) + EPS) * ln_g + ln_b
    sp = jnp.concatenate([_mm_nn(ws[g], vg_) + bcols[g] for g, vg_ in enumerate(_split_lanes(vl))], axis=1)
    return _gelu(u) * sp * _silu(z)


def _sgu_specs(tile):
    col = lambda j: pl.BlockSpec((tile, D_SGU), lambda i, j=j: (i, j))
    vec = pl.BlockSpec((1, D_SGU), lambda i: (0, 0))
    wspec = pl.BlockSpec((N_GROUPS, SGU_CHUNK, SGU_CHUNK), lambda i: (0, 0, 0))
    bspec = pl.BlockSpec((SGU_CHUNK, N_GROUPS), lambda i: (0, 0))
    return col, vec, wspec, bspec


def _sgu_fwd(proj, ln_g, ln_b, w_s, b_t, *, tile=512):
    s = proj.shape[0]
    tile = min(tile, s)
    col, vec, wspec, bspec = _sgu_specs(tile)

    def body(u_ref, v_ref, z_ref, g_ref, b_ref, w_ref, bt_ref, y_ref, yt_ref):
        ws = tuple(w_ref[g] for g in range(N_GROUPS))
        bcols = tuple(bt_ref[:, g:g + 1] for g in range(N_GROUPS))
        for c in range(tile // SGU_CHUNK):
            rows = pl.ds(c * SGU_CHUNK, SGU_CHUNK)
            y = _sgu_chunk(u_ref[rows, :], v_ref[rows, :], z_ref[rows, :], g_ref[...], b_ref[...], ws, bcols)
            y_ref[rows, :] = y.astype(BF16)
            yt_ref[:, rows] = y.T.astype(BF16)

    return pl.pallas_call(
        body, name="sgu_fwd", grid=(s // tile,),
        in_specs=[col(0), col(1), col(2), vec, vec, wspec, bspec],
        out_specs=[pl.BlockSpec((tile, D_SGU), lambda i: (i, 0)), pl.BlockSpec((D_SGU, tile), lambda i: (0, i))],
        out_shape=[jax.ShapeDtypeStruct((s, D_SGU + D_DN), BF16), jax.ShapeDtypeStruct((D_SGU + D_DN, s), BF16)],
        compiler_params=_params(("arbitrary",)),
    )(proj, proj, proj, ln_g, ln_b, w_s, b_t)


def _sgu_bwd(proj, ln_g, ln_b, w_s, b_t, dy, dp, *, tile=256):
    s = proj.shape[0]
    tile = min(tile, s)
    col, vec, wspec, bspec = _sgu_specs(tile)

    def body(u_ref, v_ref, z_ref, g_ref, b_ref, w_ref, bt_ref, dy_ref, _, dp_ref, dg_ref, db_ref, dw_ref, dbt_ref):
        @pl.when(pl.program_id(0) == 0)
        def _():
            dg_ref[...] = jnp.zeros_like(dg_ref)
            db_ref[...] = jnp.zeros_like(db_ref)
            dw_ref[...] = jnp.zeros_like(dw_ref)
            dbt_ref[...] = jnp.zeros_like(dbt_ref)

        ws = tuple(w_ref[g] for g in range(N_GROUPS))
        bcols = tuple(bt_ref[:, g:g + 1] for g in range(N_GROUPS))
        for c in range(tile // SGU_CHUNK):
            rows = pl.ds(c * SGU_CHUNK, SGU_CHUNK)
            _, vjp = jax.vjp(_sgu_chunk, u_ref[rows, :], v_ref[rows, :], z_ref[rows, :], g_ref[...], b_ref[...], ws, bcols)
            du, dv, dz, dg, db, dws, dbc = vjp(dy_ref[rows, :])
            dp_ref[rows, pl.ds(0, D_SGU)] = du.astype(BF16)
            dp_ref[rows, pl.ds(D_SGU, D_SGU)] = dv.astype(BF16)
            dp_ref[rows, pl.ds(2 * D_SGU, D_SGU)] = dz.astype(BF16)
            dg_ref[...] += dg
            db_ref[...] += db
            for g in range(N_GROUPS):
                dw_ref[g] += dws[g]
                dbt_ref[:, g:g + 1] += dbc[g]

    return pl.pallas_call(
        body, name="sgu_bwd", grid=(s // tile,),
        in_specs=[col(0), col(1), col(2), vec, vec, wspec, bspec, pl.BlockSpec((tile, D_SGU), lambda i: (i, 0)),
                  pl.BlockSpec(memory_space=pl.ANY)],
        out_specs=[pl.BlockSpec((tile, 3 * D_SGU), lambda i: (i, 0)), vec, vec, wspec, bspec],
        out_shape=[jax.ShapeDtypeStruct(dp.shape, BF16), jax.ShapeDtypeStruct((1, D_SGU), F32),
                   jax.ShapeDtypeStruct((1, D_SGU), F32), jax.ShapeDtypeStruct((N_GROUPS, SGU_CHUNK, SGU_CHUNK), F32),
                   jax.ShapeDtypeStruct((SGU_CHUNK, N_GROUPS), F32)],
        input_output_aliases={8: 0},
        compiler_params=_params(("arbitrary",)),
    )(proj, proj, proj, ln_g, ln_b, w_s, b_t, dy, dp)


QKV_BLOCK = 1


def _qkv_head(cq, ck, cv):
    q, k, v = _silu(cq), _silu(ck), _silu(cv)
    q = q * lax.rsqrt(jnp.sum(q * q, axis=-1, keepdims=True) + EPS) * np.float32(HEAD ** -0.5)
    k = k * lax.rsqrt(jnp.sum(k * k, axis=-1, keepdims=True) + EPS)
    return q, k, v


def _gate_fn(p, a_log, dt_bias):
    lane = lax.broadcasted_iota(jnp.int32, p.shape, 1)
    g = -jnp.exp(a_log) * jax.nn.softplus(p + dt_bias)
    return jnp.where(lane < 16, g, jnp.where(lane < N_GATE, jax.nn.sigmoid(p), 0.0))


def _shifted_rows(full, shift, start, rows):
    n = full.shape[0]
    assert start % HALO == 0 and 0 <= start + shift and start + shift + rows <= n
    rolled = full if shift == 0 else pltpu.roll(full, (-shift) % n, 0)
    return rolled[start:start + rows, :]


def _halo_maps(tile, s):
    per = tile // HALO
    prev = lambda i: jnp.maximum(i * per - 1, 0)
    nxt = lambda i: jnp.minimum((i + 1) * per, s // HALO - 1)
    return prev, nxt


def _prep_fwd(proj, conv_w, pg, a_log, dt_bias, *, tile=256):
    s = proj.shape[0]
    width = 3 * D_DN
    tile = min(tile, s)
    n_blk = s // tile
    prev, nxt = _halo_maps(tile, s)

    def body(x_ref, xp_ref, xn_ref, w_ref, pg_ref, al_ref, dt_ref, q_ref, k_ref, v_ref, gate_ref, ext):
        i = pl.program_id(0)
        ext[pl.ds(0, HALO), :] = jnp.where(i > 0, xp_ref[...], 0.0)
        ext[pl.ds(HALO, tile), :] = x_ref[...]
        ext[pl.ds(HALO + tile, HALO), :] = jnp.where(i < n_blk - 1, xn_ref[...], 0.0)
        full, c = ext[...], None
        for j in range(CONV_W):
            term = _shifted_rows(full, j - CONV_W // 2, HALO, tile) * w_ref[j:j + 1, :]
            c = term if c is None else c + term
        for h in range(N_GROUPS):
            q, k, v = _qkv_head(c[:, h * HEAD:(h + 1) * HEAD], c[:, D_DN + h * HEAD:D_DN + (h + 1) * HEAD],
                                c[:, 2 * D_DN + h * HEAD:2 * D_DN + (h + 1) * HEAD])
            q_ref[h] = q
            k_ref[h] = k
            v_ref[h] = v
        gate_ref[...] = _gate_fn(pg_ref[...], al_ref[...], dt_ref[...])

    hm = pl.BlockSpec((N_GROUPS, tile, HEAD), lambda i: (0, i, 0))
    lane_vec = pl.BlockSpec((1, LANE), lambda i: (0, 0))
    return pl.pallas_call(
        body, name="prep_fwd", grid=(n_blk,),
        in_specs=[pl.BlockSpec((tile, width), lambda i: (i, QKV_BLOCK)),
                  pl.BlockSpec((HALO, width), lambda i: (prev(i), QKV_BLOCK)),
                  pl.BlockSpec((HALO, width), lambda i: (nxt(i), QKV_BLOCK)),
                  pl.BlockSpec((HALO, width), lambda i: (0, 0)),
                  pl.BlockSpec((tile, LANE), lambda i: (i, 0)), lane_vec, lane_vec],
        out_specs=[hm, hm, hm, pl.BlockSpec((tile, LANE), lambda i: (i, 0))],
        out_shape=[jax.ShapeDtypeStruct((N_GROUPS, s, HEAD), F32)] * 3 + [jax.ShapeDtypeStruct((s, LANE), F32)],
        scratch_shapes=[pltpu.VMEM((tile + 2 * HALO, width), F32)],
        compiler_params=_params(("arbitrary",)),
    )(proj, proj, proj, conv_w, pg, a_log, dt_bias)


def _prep_bwd(proj, conv_w, pg, a_log, dt_bias, dqkv_f, dqkv_b, dgates, dp, *, tile=128):
    s = proj.shape[0]
    width = 3 * D_DN
    tile = min(tile, s)
    n_blk = s // tile
    prev, nxt = _halo_maps(tile, s)
    ext_rows = tile + 2 * HALO
    pad = CONV_W // 2

    def body(*refs):
        x_ref, xp_ref, xn_ref, w_ref, pg_ref, al_ref, dt_ref = refs[:7]
        ct_refs = refs[7:25]
        dg0_ref, dg1_ref, dg2_ref = refs[25:28]
        dp_ref, dpg_ref, dw_ref, dal_ref, ddt_ref, xext, dcext = refs[29:]
        i = pl.program_id(0)
        has_prev, has_next = i > 0, i < n_blk - 1

        @pl.when(i == 0)
        def _():
            dw_ref[...] = jnp.zeros_like(dw_ref)
            dal_ref[...] = jnp.zeros_like(dal_ref)
            ddt_ref[...] = jnp.zeros_like(ddt_ref)

        zeros = jnp.zeros((HALO, width), F32)
        xext[pl.ds(0, HALO), :] = zeros
        xext[pl.ds(HALO, HALO), :] = jnp.where(has_prev, xp_ref[...], 0.0)
        xext[pl.ds(2 * HALO, tile), :] = x_ref[...]
        xext[pl.ds(2 * HALO + tile, HALO), :] = jnp.where(has_next, xn_ref[...], 0.0)
        xext[pl.ds(3 * HALO + tile, HALO), :] = zeros
        xfull, c = xext[...], None
        for j in range(CONV_W):
            term = _shifted_rows(xfull, j - pad, HALO, ext_rows) * w_ref[j:j + 1, :]
            c = term if c is None else c + term
        for h in range(N_GROUPS):
            lanes = [pl.ds(part * D_DN + h * HEAD, HEAD) for part in range(3)]
            cts = []
            for part in range(3):
                (cur, before, after), (cur2, before2, after2) = ct_refs[3 * part:3 * part + 3], ct_refs[9 + 3 * part:12 + 3 * part]
                cts.append(jnp.concatenate([jnp.where(has_prev, before[h] + before2[h], 0.0), cur[h] + cur2[h],
                                            jnp.where(has_next, after[h] + after2[h], 0.0)], axis=0))
            _, vjp = jax.vjp(_qkv_head, c[:, h * HEAD:(h + 1) * HEAD], c[:, D_DN + h * HEAD:D_DN + (h + 1) * HEAD],
                             c[:, 2 * D_DN + h * HEAD:2 * D_DN + (h + 1) * HEAD])
            for lane_sl, dc in zip(lanes, vjp(tuple(cts))):
                dcext[:, lane_sl] = dc
        dcfull = dcext[...]
        dx, dc_blk = None, dcfull[HALO:HALO + tile, :]
        for j in range(CONV_W):
            term = _shifted_rows(dcfull, pad - j, HALO, tile) * w_ref[j:j + 1, :]
            dx = term if dx is None else dx + term
            dw_ref[j:j + 1, :] += jnp.sum(dc_blk * _shifted_rows(xfull, j - pad, 2 * HALO, tile), axis=0, keepdims=True)
        dp_ref[...] = dx.astype(BF16)
        _, gvjp = jax.vjp(_gate_fn, pg_ref[...], al_ref[...], dt_ref[...])
        dpg, dal, ddt = gvjp(dg0_ref[...] + dg1_ref[...] + dg2_ref[...])
        dpg_ref[...] = dpg.astype(BF16)
        dal_ref[...] += dal
        ddt_ref[...] += ddt

    xs = [pl.BlockSpec((tile, width), lambda i: (i, QKV_BLOCK)), pl.BlockSpec((HALO, width), lambda i: (prev(i), QKV_BLOCK)),
          pl.BlockSpec((HALO, width), lambda i: (nxt(i), QKV_BLOCK))]
    hm = [pl.BlockSpec((N_GROUPS, tile, HEAD), lambda i: (0, i, 0)),
          pl.BlockSpec((N_GROUPS, HALO, HEAD), lambda i: (0, prev(i), 0)),
          pl.BlockSpec((N_GROUPS, HALO, HEAD), lambda i: (0, nxt(i), 0))]
    gate = pl.BlockSpec((tile, LANE), lambda i: (i, 0))
    lane_vec = pl.BlockSpec((1, LANE), lambda i: (0, 0))
    wspec = pl.BlockSpec((HALO, width), lambda i: (0, 0))
    n_in = 3 + 4 + 18 + 3
    ct_arrays = [a for dqkv in (dqkv_f, dqkv_b) for a in dqkv for _ in range(3)]
    return pl.pallas_call(
        body, name="prep_bwd", grid=(n_blk,),
        in_specs=xs + [wspec, gate, lane_vec, lane_vec] + hm * 6 + [gate] * 3 + [pl.BlockSpec(memory_space=pl.ANY)],
        out_specs=[pl.BlockSpec((tile, width), lambda i: (i, QKV_BLOCK)), gate, wspec, lane_vec, lane_vec],
        out_shape=[jax.ShapeDtypeStruct(dp.shape, BF16), jax.ShapeDtypeStruct((s, LANE), BF16),
                   jax.ShapeDtypeStruct((HALO, width), F32), jax.ShapeDtypeStruct((1, LANE), F32),
                   jax.ShapeDtypeStruct((1, LANE), F32)],
        input_output_aliases={n_in: 0},
        scratch_shapes=[pltpu.VMEM((tile + 4 * HALO, width), F32), pltpu.VMEM((ext_rows, width), F32)],
        compiler_params=_params(("arbitrary",)),
    )(proj, proj, proj, conv_w, pg, a_log, dt_bias, *ct_arrays, *dgates, dp)


def _each(f, *cols):
    return tuple(f(*a) for a in zip(*cols))


def _delta_chunk(q, k, v, g_col, g_row, beta, state, t_known, reverse):
    n = DN_CHUNK
    ri = lax.broadcasted_iota(jnp.int32, (n, n), 0)
    ci = lax.broadcasted_iota(jnp.int32, (n, n), 1)
    masks = {False: (ri >= ci, ri <= ci, ri > ci), True: (ri <= ci, ri >= ci, ri < ci)}
    flags = tuple(reverse) if isinstance(reverse, (tuple, list)) else (reverse,) * len(q)
    incl, incl_t, strict = (tuple(masks[f][i] for f in flags) for i in range(3))
    gc_col = _each(lambda gr, mk: jnp.sum(jnp.where(mk, gr, 0.0), axis=1, keepdims=True), g_row, incl)
    gc_row = _each(lambda gc, mk: jnp.sum(jnp.where(mk, gc, 0.0), axis=0, keepdims=True), g_col, incl_t)
    g_tot = _each(lambda gr: jnp.sum(gr, axis=1, keepdims=True), g_row)
    decay = _each(lambda a, b, mk: jnp.where(mk, jnp.exp(jnp.where(mk, a - b, 0.0)), 0.0), gc_col, gc_row, incl)
    rows = lambda *xs: jnp.concatenate(xs, axis=0)
    kb = _each(lambda a, b: a * b, k, beta)
    kk_qk = _each(lambda kb_, q_, k_: _split_rows(_mm_nt(rows(kb_, q_), k_), n), kb, q, k)
    m = _each(lambda two, d, mk: jnp.where(mk, two[0] * d, 0.0), kk_qk, decay, strict)
    if t_known is None:
        eye = (ri == ci).astype(F32)
        p = _each(lambda a: -a, m)
        t = tuple(eye for _ in p)
        for _ in range(5):
            both = _each(lambda p_, t_: _bdot(rows(p_, t_), p_, ((1,), (0,))), p, t)
            p = _each(lambda b: b[:n], both)
            t = _each(lambda t_, b: t_ + b[n:], t, both)
        t = _each(lambda t_, p_: t_ + _bdot(t_, p_, ((1,), (0,))), t, p)
    else:
        t = _each(_inverse_given, m, t_known)
    e_gc = _each(jnp.exp, gc_col)
    u_w = _each(lambda t_, v_, b_, kb_, e_: _split_lanes(_mm_nn(t_, jnp.concatenate([v_ * b_, kb_ * e_], axis=1))), t, v, beta, kb, e_gc)
    attn = _each(lambda two, d_: two[1] * d_, kk_qk, decay)
    ws_qs = _each(lambda uw, q_, e_, s_: _split_rows(_mm_nn(rows(uw[1], q_ * e_), s_), n), u_w, q, e_gc, state)
    v_new = _each(lambda uw, wq: uw[0] - wq[0], u_w, ws_qs)
    k_dec_t = _each(lambda k_, gt, gc: jnp.transpose(k_ * jnp.exp(gt - gc)), k, g_tot, gc_col)
    av_kv = _each(lambda a_, kt, vn: _split_rows(_mm_nn(rows(a_, kt), vn), n), attn, k_dec_t, v_new)
    o = _each(lambda wq, ak: wq[1] + ak[0], ws_qs, av_kv)
    new_state = _each(lambda s_, gt, ak: s_ * jnp.exp(gt) + ak[1], state, g_tot, av_kv)
    return o, new_state, t


def _delta_lanes(reverse):
    return (N_GROUPS, 3 * N_GROUPS) if reverse else (0, 2 * N_GROUPS)


def _delta_fwd(q, k, v, gates, rows_f, rows_b, *, comm=None):
    s = q.shape[1]
    nc = s // DN_CHUNK
    heads = range(N_GROUPS)

    def body(qf_ref, kf_ref, vf_ref, gf_ref, rf_ref, qb_ref, kb_ref, vb_ref, gb_ref, rb_ref,
             of_ref, sf_ref, tf_ref, ob_ref, sb_ref, tb_ref, state):
        @pl.when(pl.program_id(0) == 0)
        def _():
            state[...] = jnp.zeros_like(state)

        qs, ks, vs, g_cols, g_rows, betas, flags = [], [], [], [], [], [], []
        for reverse, q_ref, k_ref, v_ref, gate, row_ref in ((False, qf_ref, kf_ref, vf_ref, gf_ref[...], rf_ref),
                                                            (True, qb_ref, kb_ref, vb_ref, gb_ref[...], rb_ref)):
            g_lane, b_lane = _delta_lanes(reverse)
            for j in heads:
                qs.append(q_ref[j])
                ks.append(k_ref[j])
                vs.append(v_ref[j])
                g_cols.append(gate[:, g_lane + j:g_lane + j + 1])
                g_rows.append(row_ref[0, j:j + 1, :])
                betas.append(gate[:, b_lane + j:b_lane + j + 1])
                flags.append(reverse)
        prev = tuple(state[n] for n in range(2 * N_GROUPS))
        o, new, t = _delta_chunk(tuple(qs), tuple(ks), tuple(vs), tuple(g_cols), tuple(g_rows), tuple(betas), prev, None, tuple(flags))
        for n in range(2 * N_GROUPS):
            o_ref, st_ref, t_ref = (of_ref, sf_ref, tf_ref) if n < N_GROUPS else (ob_ref, sb_ref, tb_ref)
            j = n % N_GROUPS
            st_ref[0, j] = prev[n]
            o_ref[j] = o[n]
            t_ref[0, j] = t[n]
            state[n] = new[n]

    def specs(chunk_of):
        hm = pl.BlockSpec((N_GROUPS, DN_CHUNK, HEAD), lambda c: (0, chunk_of(c), 0))
        ins = [hm, hm, hm, pl.BlockSpec((DN_CHUNK, LANE), lambda c: (chunk_of(c), 0)),
               pl.BlockSpec((1, N_GROUPS, DN_CHUNK), lambda c: (chunk_of(c), 0, 0))]
        outs = [hm, pl.BlockSpec((1, N_GROUPS, HEAD, HEAD), lambda c: (chunk_of(c), 0, 0, 0)),
                pl.BlockSpec((1, N_GROUPS, DN_CHUNK, DN_CHUNK), lambda c: (chunk_of(c), 0, 0, 0))]
        return ins, outs

    (ins_f, outs_f), (ins_b, outs_b) = specs(lambda c: c), specs(lambda c: nc - 1 - c)
    shapes = [jax.ShapeDtypeStruct((N_GROUPS, s, HEAD), F32), jax.ShapeDtypeStruct((nc, N_GROUPS, HEAD, HEAD), F32),
              jax.ShapeDtypeStruct((nc, N_GROUPS, DN_CHUNK, DN_CHUNK), F32)]
    results, sent = _call_with_comm(
        body, comm, name="delta_fwd" + ("" if comm is None else "_comm"), grid=(nc,),
        in_specs=ins_f + ins_b, out_specs=outs_f + outs_b, out_shape=shapes + shapes,
        scratch_shapes=[pltpu.VMEM((2 * N_GROUPS, HEAD, HEAD), F32)],
        args=(q, k, v, gates, rows_f, q, k, v, gates, rows_b), compiler_params=_params(("arbitrary",)))
    return results[:3], results[3:], sent


def _delta_bwd(q, k, v, gates, rows_f, rows_b, states_f, states_b, t_f, t_b, do, *, comm=None):
    s = q.shape[1]
    nc = s // DN_CHUNK
    heads = range(N_GROUPS)
    n_in, n_out = 8, 5

    def body(*refs):
        ins = (refs[:n_in], refs[n_in:2 * n_in])
        outs = (refs[2 * n_in:2 * n_in + n_out], refs[2 * n_in + n_out:2 * n_in + 2 * n_out])
        dstate = refs[-1]

        @pl.when(pl.program_id(0) == 0)
        def _():
            dstate[...] = jnp.zeros_like(dstate)

        qs, ks, vs, g_cols, g_rows, betas, sts, ts, dos, flags = ([] for _ in range(10))
        for reverse, (q_ref, k_ref, v_ref, gate_ref, row_ref, st_ref, t_ref, do_ref) in zip((False, True), ins):
            gate = gate_ref[...]
            g_lane, b_lane = _delta_lanes(reverse)
            for j in heads:
                qs.append(q_ref[j])
                ks.append(k_ref[j])
                vs.append(v_ref[j])
                g_cols.append(gate[:, g_lane + j:g_lane + j + 1])
                g_rows.append(row_ref[0, j:j + 1, :])
                betas.append(gate[:, b_lane + j:b_lane + j + 1])
                sts.append(st_ref[0, j])
                ts.append(t_ref[0, j])
                dos.append(do_ref[j])
                flags.append(reverse)
        t_known, flags = tuple(ts), tuple(flags)
        f = lambda q_, k_, v_, gc_, gr_, b_, s_: _delta_chunk(q_, k_, v_, gc_, gr_, b_, s_, t_known, flags)[:2]
        _, vjp = jax.vjp(f, tuple(qs), tuple(ks), tuple(vs), tuple(g_cols), tuple(g_rows), tuple(betas), tuple(sts))
        dq, dk, dv, dg_col, dg_row, dbeta, dprev = vjp((tuple(dos), tuple(dstate[n] for n in range(2 * N_GROUPS))))
        ids = lax.broadcasted_iota(jnp.int32, (DN_CHUNK, LANE), 1)
        for reverse, (dq_ref, dk_ref, dv_ref, dgate_ref, drow_ref) in zip((False, True), outs):
            g_lane, b_lane = _delta_lanes(reverse)
            dgate = jnp.zeros((DN_CHUNK, LANE), F32)
            for j in heads:
                n = j + (N_GROUPS if reverse else 0)
                dq_ref[j] = dq[n]
                dk_ref[j] = dk[n]
                dv_ref[j] = dv[n]
                dstate[n] = dprev[n]
                dgate = dgate + jnp.where(ids == g_lane + j, dg_col[n], 0.0) + jnp.where(ids == b_lane + j, dbeta[n], 0.0)
                drow_ref[0, j:j + 1, :] = dg_row[n]
            dgate_ref[...] = dgate

    def specs(chunk_of):
        hm = pl.BlockSpec((N_GROUPS, DN_CHUNK, HEAD), lambda c: (0, chunk_of(c), 0))
        gate = pl.BlockSpec((DN_CHUNK, LANE), lambda c: (chunk_of(c), 0))
        rows = pl.BlockSpec((1, N_GROUPS, DN_CHUNK), lambda c: (chunk_of(c), 0, 0))
        st = pl.BlockSpec((1, N_GROUPS, HEAD, HEAD), lambda c: (chunk_of(c), 0, 0, 0))
        ts = pl.BlockSpec((1, N_GROUPS, DN_CHUNK, DN_CHUNK), lambda c: (chunk_of(c), 0, 0, 0))
        return [hm, hm, hm, gate, rows, st, ts, hm], [hm, hm, hm, gate, rows]

    (ins_f, outs_f), (ins_b, outs_b) = specs(lambda c: nc - 1 - c), specs(lambda c: c)
    shapes = [jax.ShapeDtypeStruct((N_GROUPS, s, HEAD), F32)] * 3 + [jax.ShapeDtypeStruct((s, LANE), F32),
                                                                       jax.ShapeDtypeStruct((nc, N_GROUPS, DN_CHUNK), F32)]
    results, sent = _call_with_comm(
        body, comm, name="delta_bwd" + ("" if comm is None else "_comm"), grid=(nc,),
        in_specs=ins_f + ins_b, out_specs=outs_f + outs_b, out_shape=shapes + shapes,
        scratch_shapes=[pltpu.VMEM((2 * N_GROUPS, HEAD, HEAD), F32)],
        args=(q, k, v, gates, rows_f, states_f, t_f, do, q, k, v, gates, rows_b, states_b, t_b, do),
        compiler_params=_params(("arbitrary",)))
    return results[:n_out], results[n_out:], sent


ZB_BLOCK = 6


def _post_head(o_f, o_b, z, w):
    return _rms(o_f + o_b, w) * _silu(z)


def _post_fwd(o_f, o_b, proj, w, y, y_t, *, tile=512):
    s = proj.shape[0]
    tile = min(tile, s)

    def body(of_ref, ob_ref, z_ref, w_ref, _, __, y_ref, yt_ref):
        for h in range(N_GROUPS):
            lanes = pl.ds(h * HEAD, HEAD)
            y_h = _post_head(of_ref[h], ob_ref[h], z_ref[:, lanes], w_ref[...])
            y_ref[:, lanes] = y_h.astype(BF16)
            yt_ref[lanes, :] = y_h.T.astype(BF16)

    hm = pl.BlockSpec((N_GROUPS, tile, HEAD), lambda i: (0, i, 0))
    any_spec = pl.BlockSpec(memory_space=pl.ANY)
    return pl.pallas_call(
        body, name="post_fwd", grid=(s // tile,),
        in_specs=[hm, hm, pl.BlockSpec((tile, D_DN), lambda i: (i, ZB_BLOCK)), pl.BlockSpec((1, HEAD), lambda i: (0, 0)),
                  any_spec, any_spec],
        out_specs=[pl.BlockSpec((tile, D_DN), lambda i: (i, 1)), pl.BlockSpec((D_DN, tile), lambda i: (1, i))],
        out_shape=[jax.ShapeDtypeStruct(y.shape, BF16), jax.ShapeDtypeStruct(y_t.shape, BF16)],
        input_output_aliases={4: 0, 5: 1},
        compiler_params=_params(("arbitrary",)),
    )(o_f, o_b, proj, w, y, y_t)


def _post_bwd(o_f, o_b, proj, w, dy, *, tile=512):
    s = proj.shape[0]
    tile = min(tile, s)

    def body(of_ref, ob_ref, z_ref, w_ref, dy_ref, do_ref, dp_ref, dw_ref):
        @pl.when(pl.program_id(0) == 0)
        def _():
            dw_ref[...] = jnp.zeros_like(dw_ref)

        for h in range(N_GROUPS):
            lanes = pl.ds(h * HEAD, HEAD)
            _, vjp = jax.vjp(_post_head, of_ref[h], ob_ref[h], z_ref[:, lanes], w_ref[...])
            do, _, dz, dw = vjp(dy_ref[:, lanes])
            do_ref[h] = do
            dp_ref[:, lanes] = dz.astype(BF16)
            dw_ref[...] += dw

    hm = pl.BlockSpec((N_GROUPS, tile, HEAD), lambda i: (0, i, 0))
    zb = pl.BlockSpec((tile, D_DN), lambda i: (i, ZB_BLOCK))
    vec = pl.BlockSpec((1, HEAD), lambda i: (0, 0))
    return pl.pallas_call(
        body, name="post_bwd", grid=(s // tile,),
        in_specs=[hm, hm, zb, vec, pl.BlockSpec((tile, D_DN), lambda i: (i, 1))], out_specs=[hm, zb, vec],
        out_shape=[jax.ShapeDtypeStruct((N_GROUPS, s, HEAD), F32), jax.ShapeDtypeStruct((s, D_MAIN), BF16),
                   jax.ShapeDtypeStruct((1, HEAD), F32)],
        compiler_params=_params(("arbitrary",)),
    )(o_f, o_b, proj, w, dy)


def _my_index():
    return 4 * lax.axis_index("x") + 2 * lax.axis_index("y") + lax.axis_index("c")


def _peer(k):
    x, y, c = lax.axis_index("x"), lax.axis_index("y"), lax.axis_index("c")
    px, py, pc = x ^ ((k >> 2) & 1), y ^ ((k >> 1) & 1), c ^ (k & 1)
    return (px, py, pc), 4 * px + 2 * py + pc


class _Comm:
    def __init__(self, inputs, outputs, copies, by_chip=False):
        self.inputs, self.outputs, self.copies, self.by_chip = list(inputs), list(outputs), list(copies), by_chip
        self.kept = [o for o in self.outputs if not isinstance(o, jax.ShapeDtypeStruct)]

    def out_shapes(self):
        return [jax.ShapeDtypeStruct(o.shape, o.dtype) for o in self.outputs]

    def aliases(self, first_kept_input, first_output):
        kept_at = [i for i, o in enumerate(self.outputs) if not isinstance(o, jax.ShapeDtypeStruct)]
        return {first_kept_input + n: first_output + i for n, i in enumerate(kept_at)}

    def scratch(self):
        n = len(self.copies)
        return [pltpu.SemaphoreType.DMA((n, N_DEV - 1)), pltpu.SemaphoreType.DMA((n, N_DEV - 1)), pltpu.SemaphoreType.DMA((n,))]

    def _descriptors(self, in_refs, out_refs, sems, arrivals):
        send_sems, recv_sems, local_sems = sems
        me = _my_index()
        local, remote = [], []
        for k in range(N_DEV):
            peer, peer_idx = _peer(k)
            for a, (i_in, i_out, src_of, dst_of) in enumerate(self.copies):
                if k == 0:
                    if not arrivals:
                        local.append(pltpu.make_async_copy(src_of(in_refs[i_in], me), dst_of(out_refs[i_out], me), local_sems.at[a]))
                    continue
                receiver, sender = (me, peer_idx) if arrivals else (peer_idx, me)
                remote.append(pltpu.make_async_remote_copy(
                    src_ref=src_of(in_refs[i_in], receiver), dst_ref=dst_of(out_refs[i_out], sender),
                    send_sem=send_sems.at[a, k - 1], recv_sem=recv_sems.at[a, k - 1], device_id=peer,
                    device_id_type=pl.DeviceIdType.MESH))
        return local, remote

    def two_level(self, in_refs, out_refs, sems):
        send_sems, recv_sems, local_sems = sems
        me = _my_index()
        sibling, _ = _peer(1)
        n = range(len(self.copies))
        sources = [src_of(in_refs[i_in], me) for i_in, _, src_of, _ in self.copies]
        landing = lambda a, owner: self.copies[a][3](out_refs[self.copies[a][1]], owner)

        def remote(a, k, src, owner, to):
            return pltpu.make_async_remote_copy(
                src_ref=src, dst_ref=landing(a, owner), send_sem=send_sems.at[a, k - 1], recv_sem=recv_sems.at[a, k - 1],
                device_id=to, device_id_type=pl.DeviceIdType.MESH)

        local = lambda: [pltpu.make_async_copy(sources[a], landing(a, me), local_sems.at[a]) for a in n]
        first = lambda: [remote(a, k, sources[a], me, _peer(k)[0]) for k in (1, 2, 4, 6) for a in n]
        passed = lambda: [remote(a, k + 1, landing(a, _peer(k)[1]), _peer(k)[1], sibling) for k in (2, 4, 6) for a in n]
        arrivals = lambda ks: [remote(a, k, sources[a], _peer(k)[1], _peer(k)[0]) for k in ks for a in n]

        def start():
            for cp in local() + first():
                cp.start()

        def pass_on():
            for cp in arrivals((2, 4, 6)):
                cp.wait_recv()
            for cp in passed():
                cp.start()

        def finish():
            for cp in arrivals((1, 3, 5, 7)):
                cp.wait_recv()
            for cp in first() + passed():
                cp.wait_send()
            for cp in local():
                cp.wait()

        return start, pass_on, finish

    def phases(self, in_refs, out_refs, sems):
        if self.by_chip:
            return self.two_level(in_refs, out_refs, sems)
        return (lambda: self.start(in_refs, out_refs, sems)), None, (lambda: self.wait(in_refs, out_refs, sems))

    def start(self, in_refs, out_refs, sems):
        local, sends = self._descriptors(in_refs, out_refs, sems, arrivals=False)
        for cp in local + sends:
            cp.start()

    def wait(self, in_refs, out_refs, sems):
        _, arriving = self._descriptors(in_refs, out_refs, sems, arrivals=True)
        for cp in arriving:
            cp.wait_recv()
        local, sends = self._descriptors(in_refs, out_refs, sems, arrivals=False)
        for cp in sends:
            cp.wait_send()
        for cp in local:
            cp.wait()


def _call_with_comm(body, comm, *, name, grid, in_specs, out_specs, out_shape, scratch_shapes, args, compiler_params):
    if comm is None:
        return pl.pallas_call(body, name=name, grid=grid, in_specs=in_specs, out_specs=out_specs, out_shape=out_shape,
                              scratch_shapes=scratch_shapes, compiler_params=compiler_params)(*args), []
    n_in, n_out, n_scratch = len(in_specs), len(out_specs), len(scratch_shapes)
    c_in, c_kept, c_out = len(comm.inputs), len(comm.kept), len(comm.outputs)

    def at_step(position):
        here = [pl.program_id(d) == p for d, p in enumerate(position)]
        return here[0] if len(here) == 1 else functools.reduce(jnp.logical_and, here)

    def wrapped(*refs):
        main_in, comm_in = refs[:n_in], refs[n_in:n_in + c_in]
        o = n_in + c_in + c_kept
        main_out, comm_out = refs[o:o + n_out], refs[o + n_out:o + n_out + c_out]
        s = o + n_out + c_out
        main_scratch, sems = refs[s:s + n_scratch], refs[s + n_scratch:]

        start, pass_on, finish = comm.phases(comm_in, comm_out, sems)
        pl.when(at_step([0] * len(grid)))(start)
        if pass_on is not None:
            assert len(grid) == 1 and grid[0] >= 3
            pl.when(at_step([2 * grid[0] // 3]))(pass_on)
        body(*main_in, *main_out, *main_scratch)
        pl.when(at_step([g - 1 for g in grid]))(finish)

    any_spec = pl.BlockSpec(memory_space=pl.ANY)
    results = pl.pallas_call(
        wrapped, name=name, grid=grid, in_specs=list(in_specs) + [any_spec] * (c_in + c_kept),
        out_specs=list(out_specs) + [any_spec] * c_out, out_shape=list(out_shape) + comm.out_shapes(),
        scratch_shapes=list(scratch_shapes) + comm.scratch(), input_output_aliases=comm.aliases(n_in + c_in, n_out),
        compiler_params=compiler_params,
    )(*args, *comm.inputs, *comm.kept)
    return results[:n_out], results[n_out:]


def _exchange(name, comm):
    c_in, c_kept, c_out = len(comm.inputs), len(comm.kept), len(comm.outputs)

    def body(*refs):
        in_refs, out_refs, sems = refs[:c_in], refs[c_in + c_kept:c_in + c_kept + c_out], refs[c_in + c_kept + c_out:]
        for phase in comm.phases(in_refs, out_refs, sems):
            if phase is not None:
                phase()

    any_spec = pl.BlockSpec(memory_space=pl.ANY)
    return pl.pallas_call(
        body, name=name, in_specs=[any_spec] * (c_in + c_kept), out_specs=[any_spec] * c_out, out_shape=comm.out_shapes(),
        scratch_shapes=comm.scratch(), input_output_aliases=comm.aliases(c_in, 0),
    )(*comm.inputs, *comm.kept)


def _whole(ref, _):
    return ref


def _slot(ref, dev):
    return ref.at[dev]


def _row_block(size, axis):
    def pick(ref, dev):
        start = pl.multiple_of(dev * size, size) if size % SUBLANES == 0 else dev * size
        return ref.at[tuple([slice(None)] * axis + [pl.ds(start, size)])]
    return pick


def _blocked(shape, block):
    nd = len(block)
    grid = tuple(shape[d] // block[d] for d in range(nd) if block[d] != shape[d])
    return grid, (lambda *i: tuple(i) + (0,) * (nd - len(grid)))


def _sum_parts(name, parts, block, comm=None):
    n_parts = parts.shape[0]
    grid, idx = _blocked(parts.shape[1:], block)

    def body(p_ref, g_ref):
        g = p_ref[0].astype(F32)
        for j in range(1, n_parts):
            g = g + p_ref[j].astype(F32)
        g_ref[...] = g

    (total,), sent = _call_with_comm(
        body, comm, name=name, grid=grid, in_specs=[pl.BlockSpec((n_parts,) + tuple(block), lambda *i: (0,) + idx(*i))],
        out_specs=[pl.BlockSpec(block, idx)], out_shape=[jax.ShapeDtypeStruct(parts.shape[1:], F32)], scratch_shapes=[],
        args=(parts,), compiler_params=_params(("arbitrary",) * len(grid)))
    return total, sent


def _adamw(name, parts, w, m, v, block):
    n_parts = parts.shape[0]
    nd = len(block)
    grid, _ = _blocked(w.shape, block)
    lead = len(grid)

    def body(p_ref, w_ref, m_ref, v_ref, g_ref, d_ref, mo_ref, vo_ref):
        g = p_ref[0].astype(F32)
        for j in range(1, n_parts):
            g = g + p_ref[j].astype(F32)
        m_new = ADAM_B1 * m_ref[...] + (1.0 - ADAM_B1) * g
        v_new = ADAM_B2 * v_ref[...] + (1.0 - ADAM_B2) * jnp.square(g)
        m_hat = m_new / np.float32(1.0 - ADAM_B1 ** ADAM_STEP)
        v_hat = v_new / np.float32(1.0 - ADAM_B2 ** ADAM_STEP)
        g_ref[...] = g
        d_ref[...] = -ADAM_LR * (m_hat / (jnp.sqrt(v_hat) + ADAM_EPS) + ADAM_WD * w_ref[...])
        mo_ref[...] = m_new
        vo_ref[...] = v_new

    idx = lambda *i: tuple(i) + (0,) * (nd - lead)
    spec = pl.BlockSpec(block, idx)
    pspec = pl.BlockSpec((n_parts,) + tuple(block), lambda *i: (0,) + idx(*i))
    n_blk = int(np.prod(block[:-1])) * (-(-block[-1] // LANE) * LANE)
    return pl.pallas_call(
        body, name=name, grid=grid, in_specs=[pspec, spec, spec, spec], out_specs=[spec] * 4,
        out_shape=[jax.ShapeDtypeStruct(w.shape, F32)] * 4,
        compiler_params=_params(("arbitrary",) * lead, 2 * (n_parts + 7) * n_blk * 4 + VMEM_MARGIN),
    )(parts, w, m, v)


def _to_flat(pieces, row_multiple):
    flat = jnp.concatenate([p.reshape(-1) for p in pieces])
    rows = -(-flat.shape[0] // FLAT_W)
    rows = -(-rows // row_multiple) * row_multiple
    return jnp.pad(flat, (0, rows * FLAT_W - flat.shape[0])).reshape(rows, FLAT_W)


def _lane_row(*pieces):
    flat = jnp.concatenate([p.reshape(-1) for p in pieces])
    return jnp.pad(flat, (0, LANE - flat.shape[0])).reshape(1, LANE)


def _rows_view(a):
    return a.reshape(a.shape[:-1] + (SUBLANES, LANE))


def _in_shard_t(a):
    return _rows_view(jnp.swapaxes(a, 1, 2))


def kernel(x, norm_w, w_in, sgu_ln_g, sgu_ln_b, sgu_w, sgu_b, conv_w, a_log_f, a_log_b, dt_bias_f, dt_bias_b, dn_norm_w, w_out, final_norm_w, loss_target, m_norm_w, m_w_in, m_sgu_ln_g, m_sgu_ln_b, m_sgu_w, m_sgu_b, m_conv_w, m_a_log_f, m_a_log_b, m_dt_bias_f, m_dt_bias_b, m_dn_norm_w, m_w_out, m_final_norm_w, v_norm_w, v_w_in, v_sgu_ln_g, v_sgu_ln_b, v_sgu_w, v_sgu_b, v_conv_w, v_a_log_f, v_a_log_b, v_dt_bias_f, v_dt_bias_b, v_dn_norm_w, v_w_out, v_final_norm_w):
    weights = dict(norm_w=norm_w, w_in=w_in, sgu_ln_g=sgu_ln_g, sgu_ln_b=sgu_ln_b, sgu_w=sgu_w, sgu_b=sgu_b, conv_w=conv_w,
                   a_log_f=a_log_f, a_log_b=a_log_b, dt_bias_f=dt_bias_f, dt_bias_b=dt_bias_b, dn_norm_w=dn_norm_w,
                   w_out=w_out, final_norm_w=final_norm_w)
    m_in = dict(norm_w=m_norm_w, w_in=m_w_in, sgu_ln_g=m_sgu_ln_g, sgu_ln_b=m_sgu_ln_b, sgu_w=m_sgu_w, sgu_b=m_sgu_b,
                conv_w=m_conv_w, a_log_f=m_a_log_f, a_log_b=m_a_log_b, dt_bias_f=m_dt_bias_f, dt_bias_b=m_dt_bias_b,
                dn_norm_w=m_dn_norm_w, w_out=m_w_out, final_norm_w=m_final_norm_w)
    v_in = dict(norm_w=v_norm_w, w_in=v_w_in, sgu_ln_g=v_sgu_ln_g, sgu_ln_b=v_sgu_ln_b, sgu_w=v_sgu_w, sgu_b=v_sgu_b,
                conv_w=v_conv_w, a_log_f=v_a_log_f, a_log_b=v_a_log_b, dt_bias_f=v_dt_bias_f, dt_bias_b=v_dt_bias_b,
                dn_norm_w=v_dn_norm_w, w_out=v_w_out, final_norm_w=v_final_norm_w)
    s = x.shape[1]
    nc = s // DN_CHUNK
    depth = w_in.shape[0]
    in_shard, out_shard, conv_shard = w_in.shape[2], w_out.shape[1], conv_w.shape[2]
    x0 = x.reshape(s, D_MODEL)
    target = loss_target.reshape(s, D_MODEL)

    w_in_shard = _in_shard_t(w_in.astype(BF16))
    w_out_shard = w_out.astype(BF16)

    gathered_in_shape = jax.ShapeDtypeStruct((N_DEV * in_shard, SUBLANES, LANE), BF16)
    gathered_out_shape = jax.ShapeDtypeStruct((D_MODEL, D_MODEL), BF16)
    gathered_in, gathered_out, conv_all = _exchange("gather_first", _Comm(
        [w_in_shard[0], w_out_shard[0], conv_w],
        [gathered_in_shape, gathered_out_shape, jax.ShapeDtypeStruct((N_DEV,) + conv_w.shape, F32)],
        [(0, 0, _whole, _row_block(in_shard, 0)), (1, 1, _whole, _row_block(out_shard, 0)), (2, 2, _whole, _slot)], by_chip=True))
    gather_plan = lambda l: _Comm([w_in_shard[l], w_out_shard[l]], [gathered_in_shape, gathered_out_shape],
                                  [(0, 0, _whole, _row_block(in_shard, 0)), (1, 1, _whole, _row_block(out_shard, 0))], by_chip=True)
    conv_full = conv_all.transpose(1, 2, 0, 3).reshape(depth, CONV_W, 3 * D_DN)
    conv_full = jnp.pad(conv_full, ((0, 0), (0, HALO - CONV_W), (0, 0)))
    sgu_b_t = jnp.swapaxes(sgu_b, 1, 2)

    def rows_of(gates, first_lane):
        return gates[:, first_lane:first_lane + N_GROUPS].reshape(nc, DN_CHUNK, N_GROUPS).transpose(0, 2, 1)

    saved = []
    xl = x0
    for l in range(depth):
        w_in_t = gathered_in.reshape(D_IN, D_MODEL)
        w_gate_t = jnp.pad(w_in_t[D_MAIN:], ((0, LANE - N_GATE), (0, 0)))
        w_out_l = gathered_out
        nw = norm_w[l].reshape(1, D_MODEL)
        h, h_t = _rms_fwd(xl, nw)
        proj = _matmul([(_lhs(h, tm=MM_WIDE), _rhs(w_in_t, MM_WIDE, transposed=True))], s, D_MAIN, name="in_proj", trans_b=True, tn=MM_WIDE)
        p_gate = _matmul([(_lhs(h), _rhs(w_gate_t, LANE, transposed=True))], s, LANE, name="in_proj_gate", trans_b=True, tn=LANE)
        ln_g, ln_b = sgu_ln_g[l].reshape(1, D_SGU), sgu_ln_b[l].reshape(1, D_SGU)
        y, y_t = _sgu_fwd(proj, ln_g, ln_b, sgu_w[l], sgu_b_t[l])
        a_log = _lane_row(a_log_f[l], a_log_b[l])
        dt_bias = _lane_row(dt_bias_f[l], dt_bias_b[l])
        q, k, v, gates = _prep_fwd(proj, conv_full[l], p_gate, a_log, dt_bias)
        rows_f, rows_b = rows_of(gates, 0), rows_of(gates, N_GROUPS)
        more = l + 1 < depth
        (o_f, st_f, t_f), (o_b, st_b, t_b), next_weights = _delta_fwd(q, k, v, gates, rows_f, rows_b,
                                                                     comm=gather_plan(l + 1) if more else None)
        dn_w = dn_norm_w[l].reshape(1, HEAD)
        y, y_t = _post_fwd(o_f, o_b, proj, dn_w, y, y_t)
        x_next = _matmul([(_lhs(y, tm=MM_WIDE), _rhs(w_out_l, MM_WIDE))], s, D_MODEL, name="out_proj", add=xl, tn=MM_WIDE)
        saved.append(dict(x=xl, nw=nw, h_t=h_t, proj=proj, p_gate=p_gate, ln_g=ln_g, ln_b=ln_b, a_log=a_log, dt_bias=dt_bias,
                          q=q, k=k, v=v, gates=gates, rows_f=rows_f, rows_b=rows_b, o_f=o_f, o_b=o_b, st_f=st_f, st_b=st_b,
                          t_f=t_f, t_b=t_b, dn_w=dn_w, y_t=y_t, w_in_t=w_in_t, w_gate_t=w_gate_t, w_out=w_out_l))
        xl = x_next
        if more:
            gathered_in, gathered_out = next_weights

    loss_part, dx, dx_bf, d_final = _final_loss(xl, final_norm_w.reshape(1, D_MODEL), target)
    loss = lax.psum(loss_part[0, 0], MESH_AXES)

    t_rows = -(-D_IN // LANE) * LANE
    layer_small = [n for n in SMALL if n != "final_norm_w"]
    n_layer_small = sum(int(np.prod(weights[n].shape[1:])) for n in layer_small) + D_MODEL
    small_rows = -(-n_layer_small // (FLAT_W * HALO)) * HALO

    def small_segments(values):
        top = values["final_norm_w"]
        return jnp.concatenate([_to_flat([values[n][l] for n in layer_small] + [top if l == depth - 1 else jnp.zeros_like(top)], HALO)
                                for l in range(depth)], axis=0)

    def small_to_send(layer_grads, top_grad):
        conv = layer_grads["conv_w"].reshape(CONV_W, N_DEV, conv_shard).transpose(1, 0, 2).reshape(N_DEV, -1)
        rest = jnp.concatenate([layer_grads[n].reshape(-1) for n in layer_small[1:]] + [top_grad])
        flat = jnp.concatenate([conv, jnp.broadcast_to(rest, (N_DEV, rest.shape[0]))], axis=1)
        return jnp.pad(flat, ((0, 0), (0, small_rows * FLAT_W - flat.shape[1]))).reshape(N_DEV, small_rows, FLAT_W)

    layer_slot = lambda l: (lambda ref, dev: ref.at[dev, l])
    small_slot = lambda l: (lambda ref, dev: ref.at[dev, pl.ds(l * small_rows, small_rows)])
    parts = [jax.ShapeDtypeStruct((N_DEV, depth, in_shard, SUBLANES, LANE), BF16),
             jax.ShapeDtypeStruct((N_DEV, depth, out_shard, D_MODEL), BF16),
             jax.ShapeDtypeStruct((N_DEV, depth * small_rows, FLAT_W), F32)]

    def exchange_plan(items):
        source = (_row_block(in_shard, 0), _row_block(out_shard, 0), _slot)
        target = (layer_slot, layer_slot, small_slot)
        return _Comm([a for _, _, a in items], parts,
                     [(i, which, source[which], target[which](layer)) for i, (which, layer, _) in enumerate(items)])

    pending = []
    for l in reversed(range(depth)):
        sv = saved[l]
        dy = _matmul([(_lhs(dx_bf, tm=MM_WIDE), _rhs(sv["w_out"], MM_WIDE, transposed=True))], s, D_MODEL, name="out_proj_dy", trans_b=True,
                     tn=MM_WIDE)
        g_out = _matmul([(_lhs(sv["y_t"]), _rhs(dx_bf, MM_TILE))], D_MODEL, D_MODEL, name="out_proj_dw", out_dtype=BF16)
        do, dp, d_dn = _post_bwd(sv["o_f"], sv["o_b"], sv["proj"], sv["dn_w"], dy)
        (*dqkv_f, dgate_f, drows_f), (*dqkv_b, dgate_b, drows_b), exchanged = _delta_bwd(
            sv["q"], sv["k"], sv["v"], sv["gates"], sv["rows_f"], sv["rows_b"], sv["st_f"], sv["st_b"], sv["t_f"], sv["t_b"], do,
            comm=exchange_plan([(1, l, g_out)] + pending))
        parts = list(exchanged)
        drows = jnp.concatenate([drows_f.transpose(0, 2, 1).reshape(s, N_GROUPS), drows_b.transpose(0, 2, 1).reshape(s, N_GROUPS)], axis=1)
        drows = jnp.pad(drows, ((0, 0), (0, LANE - 2 * N_GROUPS)))
        dp, d_gate, d_conv, d_alog, d_dt = _prep_bwd(sv["proj"], conv_full[l], sv["p_gate"], sv["a_log"], sv["dt_bias"],
                                                     dqkv_f, dqkv_b, (dgate_f, dgate_b, drows), dp)
        dp, d_lg, d_lb, d_sw, d_sbt = _sgu_bwd(sv["proj"], sv["ln_g"], sv["ln_b"], sgu_w[l], sgu_b_t[l], dy, dp)
        h_t = sv["h_t"]
        g_in_t = _matmul([(_lhs(h_t), _rhs(dp, MM_TILE))], D_MODEL, D_MAIN, name="in_proj_dw", out_dtype=BF16, out_t_rows=t_rows)
        g_in_t = _matmul([(_lhs(h_t), _rhs(d_gate, LANE))], D_MODEL, LANE, name="in_proj_dw_gate", out_dtype=BF16, tn=LANE,
                         transpose_into=g_in_t, out_row_block=D_MAIN // LANE)
        g_in_rows = _rows_view(g_in_t)
        dh_pairs = [(_lhs(dp), _rhs(sv["w_in_t"], MM_TILE, k=D_MAIN)), (_lhs(d_gate), _rhs(sv["w_gate_t"], MM_TILE))]
        if l > 0:
            dh = _matmul(dh_pairs, s, D_MODEL, name="in_proj_dh")
        else:
            dh, exchanged = _matmul(dh_pairs, s, D_MODEL, name="in_proj_dh_comm", comm=exchange_plan([(0, l, g_in_rows)]))
            parts = list(exchanged)
        dx, dx_bf, d_nw = _rms_bwd(sv["x"], sv["nw"], dh, dx)
        layer_grads = dict(
            conv_w=d_conv[:CONV_W], norm_w=d_nw, sgu_ln_g=d_lg, sgu_ln_b=d_lb, sgu_w=d_sw, sgu_b=d_sbt.T,
            a_log_f=d_alog[0, :N_GROUPS], a_log_b=d_alog[0, N_GROUPS:2 * N_GROUPS],
            dt_bias_f=d_dt[0, :N_GROUPS], dt_bias_b=d_dt[0, N_GROUPS:2 * N_GROUPS], dn_norm_w=d_dn)
        small = small_to_send(layer_grads, d_final.reshape(D_MODEL) if l == depth - 1 else jnp.zeros((D_MODEL,), F32))
        pending = [(2, l, small)] + ([(0, l, g_in_rows)] if l > 0 else [])
    grad_x = dx.reshape(1, s, D_MODEL)

    parts_in, parts_out, parts_small = parts
    in_rows = 100 if in_shard % 100 == 0 else in_shard
    (_, bottom, bottom_small), = pending
    g_in_t, (parts_small,) = _sum_parts("sum_in_comm", parts_in, (1, in_rows, SUBLANES, LANE), comm=_Comm(
        [bottom_small], [parts_small], [(0, 0, _slot, small_slot(bottom))]))
    g_in = jnp.swapaxes(g_in_t.reshape(depth, in_shard, D_MODEL), 1, 2)
    res_in = _adamw("adamw_in", g_in[None], w_in, m_w_in, v_w_in, (1, min(256, D_MODEL), in_shard))
    res_out = _adamw("adamw_out", parts_out, w_out, m_w_out, v_w_out, (1, min(128, out_shard), D_MODEL))
    res_small = _adamw("adamw_small", parts_small, small_segments(weights), small_segments(m_in), small_segments(v_in),
                       (small_rows, FLAT_W))
    outs = [dict(w_in=a, w_out=b) for a, b in zip(res_in, res_out)]
    for o, d in zip(res_small, outs):
        by_layer, off = o.reshape(depth, small_rows * FLAT_W), 0
        for n in layer_small:
            size = int(np.prod(weights[n].shape[1:]))
            d[n] = by_layer[:, off:off + size].reshape(weights[n].shape)
            off += size
        d["final_norm_w"] = by_layer[depth - 1, off:off + D_MODEL]
    g_out, delta_out, m_out, v_out = outs
    return (loss, grad_x, *[g_out[n] for n in WEIGHTS], *[delta_out[n] for n in WEIGHTS],
            *[m_out[n] for n in WEIGHTS], *[v_out[n] for n in WEIGHTS])
```
